```python
import jax, jax.numpy as jnp
from jax import lax
import numpy as np

D_MODEL = 1024
BATCH = 8
SEQ = 8192
DEPTH = 2

PLE_DIM = 256
N_MIXERS = 2
N_FOX_LAYERS = (DEPTH + 1) // 2
N_DIL_LAYERS = DEPTH // 2

FOX_HEADS = 16
FOX_HEAD_DIM = D_MODEL // FOX_HEADS
FOX_WIDTH = FOX_HEADS * FOX_HEAD_DIM
FOX_IN = 4 * FOX_WIDTH + FOX_HEADS
QUERY_BLOCK = 128
FORGET_BIAS_CENTER = 2.0

DIL_PATTERN = ((128, 1), (512, 4), (2048, 16))
DIL_GROUPS = len(DIL_PATTERN)
DIL_HEADS_PER_GROUP = 8
DIL_HEAD_DIM = D_MODEL // DIL_HEADS_PER_GROUP
DIL_HEADS = DIL_GROUPS * DIL_HEADS_PER_GROUP
DIL_QKV = DIL_HEADS * DIL_HEAD_DIM
DIL_WIDTH = DIL_HEADS_PER_GROUP * DIL_HEAD_DIM
DIL_IN = 3 * DIL_QKV + DIL_WIDTH
ALIBI_MAX_EXP = 8.0

RMS_EPS = 1e-6

kernel_name = "fox_dilated_hybrid_trunk"


def rms_norm(x, g):
    xf = x.astype(jnp.float32)
    y = xf * lax.rsqrt(jnp.mean(xf * xf, axis=-1, keepdims=True) + RMS_EPS)
    return (y * g.astype(jnp.float32)).astype(x.dtype)


def alibi_slopes(n):
    return 2.0 ** (-ALIBI_MAX_EXP * jnp.arange(1, n + 1, dtype=jnp.float32) / n)


def fox_mixer(h, w_in, b_f, w_out):
    B, S, _ = h.shape
    proj = h @ w_in
    q = proj[..., :FOX_WIDTH].reshape(B, S, FOX_HEADS, FOX_HEAD_DIM)
    k = proj[..., FOX_WIDTH:2 * FOX_WIDTH].reshape(B, S, FOX_HEADS, FOX_HEAD_DIM)
    v = proj[..., 2 * FOX_WIDTH:3 * FOX_WIDTH].reshape(B, S, FOX_HEADS, FOX_HEAD_DIM)
    z = proj[..., 3 * FOX_WIDTH:4 * FOX_WIDTH]
    f_logit = proj[..., 4 * FOX_WIDTH:]
    log_f = jax.nn.log_sigmoid((f_logit + b_f).astype(jnp.float32))
    c = jnp.cumsum(log_f, axis=1)
    nb = S // QUERY_BLOCK
    qb = q.reshape(B, nb, QUERY_BLOCK, FOX_HEADS, FOX_HEAD_DIM).transpose(1, 0, 2, 3, 4)
    cqb = c.reshape(B, nb, QUERY_BLOCK, FOX_HEADS).transpose(1, 0, 3, 2)
    qpos = jnp.arange(S).reshape(nb, QUERY_BLOCK)
    kpos = jnp.arange(S)
    ck = c.transpose(0, 2, 1)
    scale = FOX_HEAD_DIM ** -0.5

    def block(args):
        qi, ci, pi = args
        s = jnp.einsum('bqhd,bkhd->bhqk', qi, k).astype(jnp.float32) * scale
        s = s + ci[..., None] - ck[:, :, None, :]
        s = jnp.where((kpos[None, :] <= pi[:, None])[None, None], s, -jnp.inf)
        pr = jax.nn.softmax(s, axis=-1)
        return jnp.einsum('bhqk,bkhd->bqhd', pr.astype(v.dtype), v)

    o = lax.map(block, (qb, cqb, qpos))
    o = o.transpose(1, 0, 2, 3, 4).reshape(B, S, FOX_WIDTH)
    return (o * jax.nn.silu(z)) @ w_out


def dilated_window_attention(q, k, v, slopes, window, dilation):
    B, S, Hg, hd = q.shape
    L = S // dilation
    nW = window // dilation
    nb = -(-L // nW)
    Lp = nb * nW
    Bd = B * dilation

    def to_blocks(t):
        t = t.reshape(B, L, dilation, Hg, hd).transpose(0, 2, 1, 3, 4).reshape(Bd, L, Hg, hd)
        t = jnp.pad(t, ((0, 0), (0, Lp - L), (0, 0), (0, 0)))
        return t.reshape(Bd, nb, nW, Hg, hd)

    def with_prev(t):
        prev = jnp.pad(t, ((0, 0), (1, 0), (0, 0), (0, 0), (0, 0)))[:, :-1]
        return jnp.concatenate([prev, t], axis=2)

    qb = to_blocks(q)
    kk = with_prev(to_blocks(k))
    vv = with_prev(to_blocks(v))
    i = jnp.arange(nW)[:, None]
    j = jnp.arange(2 * nW)[None, :]
    dist = nW + i - j
    key_pos = (jnp.arange(nb)[:, None, None] - 1) * nW + j[None]
    valid = (dist >= 0)[None] & (dist <= nW)[None] & (key_pos >= 0)
    bias = -slopes.astype(jnp.float32)[:, None, None] * (dist * dilation).astype(jnp.float32)
    scale = hd ** -0.5
    s = jnp.einsum('znqhd,znkhd->znhqk', qb, kk).astype(jnp.float32) * scale + bias[None, None]
    s = jnp.where(valid[None, :, None], s, -jnp.inf)
    lse = jax.nn.logsumexp(s, axis=-1, keepdims=True)
    pr = jnp.exp(s - lse)
    o = jnp.einsum('znhqk,znkhd->znqhd', pr.astype(v.dtype), vv)
    o = o.reshape(Bd, Lp, Hg, hd)[:, :L]
    o = o.reshape(B, dilation, L, Hg, hd).transpose(0, 2, 1, 3, 4).reshape(B, S, Hg, hd)
    lse = lse[..., 0].transpose(0, 1, 3, 2).reshape(Bd, Lp, Hg)[:, :L]
    lse = lse.reshape(B, dilation, L, Hg).transpose(0, 2, 1, 3).reshape(B, S, Hg)
    return o, lse


def dilated_mixer(h, w_in, w_out):
    B, S, _ = h.shape
    proj = h @ w_in
    shp = (B, S, DIL_GROUPS, DIL_HEADS_PER_GROUP, DIL_HEAD_DIM)
    q = proj[..., :DIL_QKV].reshape(shp)
    k = proj[..., DIL_QKV:2 * DIL_QKV].reshape(shp)
    v = proj[..., 2 * DIL_QKV:3 * DIL_QKV].reshape(shp)
    z = proj[..., 3 * DIL_QKV:]
    slopes = alibi_slopes(DIL_HEADS).reshape(DIL_GROUPS, DIL_HEADS_PER_GROUP)
    outs, lses = [], []
    for g, (window, dilation) in enumerate(DIL_PATTERN):
        o, l = dilated_window_attention(q[:, :, g], k[:, :, g], v[:, :, g], slopes[g], window, dilation)
        outs.append(o)
        lses.append(l)
    wts = jax.nn.softmax(jnp.stack(lses), axis=0)
    o = jnp.sum(wts[..., None] * jnp.stack(outs).astype(jnp.float32), axis=0)
    o = o.astype(h.dtype).reshape(B, S, DIL_WIDTH)
    return (o * jax.nn.silu(z)) @ w_out


def _fwd_setup_inputs(seed: int = 0) -> dict:
    key = jax.random.key(seed)
    ks = jax.random.split(key, 14)
    f32 = jnp.float32
    nrm = lambda k, shape, fan_in: jax.random.normal(k, shape, f32) * fan_in ** -0.5
    return {
        "x": jax.random.normal(ks[0], (BATCH, SEQ, D_MODEL), f32),
        "p": jax.random.normal(ks[1], (DEPTH, BATCH, SEQ, PLE_DIM), f32),
        "fox_norm": 1.0 + 0.02 * jax.random.normal(ks[2], (N_FOX_LAYERS, D_MODEL), f32),
        "fox_w_in": nrm(ks[3], (N_FOX_LAYERS, D_MODEL, FOX_IN), D_MODEL),
        "fox_b_f": FORGET_BIAS_CENTER + 0.5 * jax.random.normal(ks[4], (N_FOX_LAYERS, FOX_HEADS), f32),
        "fox_w_out": nrm(ks[5], (N_FOX_LAYERS, FOX_WIDTH, D_MODEL), FOX_WIDTH),
        "dil_norm": 1.0 + 0.02 * jax.random.normal(ks[6], (N_DIL_LAYERS, D_MODEL), f32),
        "dil_w_in": nrm(ks[7], (N_DIL_LAYERS, D_MODEL, DIL_IN), D_MODEL),
        "dil_w_out": nrm(ks[8], (N_DIL_LAYERS, DIL_WIDTH, D_MODEL), DIL_WIDTH),
        "ple_w_up": nrm(ks[9], (DEPTH, PLE_DIM, D_MODEL), PLE_DIM),
        "ple_w_gate": nrm(ks[10], (DEPTH, D_MODEL, D_MODEL), D_MODEL),
        "final_norm": 1.0 + 0.02 * jax.random.normal(ks[11], (D_MODEL,), f32),
    }


def _fwd_reference(x, p, fox_norm, fox_w_in, fox_b_f, fox_w_out, dil_norm, dil_w_in, dil_w_out,
              ple_w_up, ple_w_gate, final_norm):
    h = x
    for i in range(DEPTH):
        j = i // N_MIXERS
        if i % N_MIXERS == 0:
            h = h + fox_mixer(rms_norm(h, fox_norm[j]), fox_w_in[j], fox_b_f[j], fox_w_out[j])
        else:
            h = h + dilated_mixer(rms_norm(h, dil_norm[j]), dil_w_in[j], dil_w_out[j])
        h = h + (p[i] @ ple_w_up[i]) * jax.nn.sigmoid(h @ ple_w_gate[i])
    return rms_norm(h, final_norm)


import jax as _jax
import jax.numpy as _jnp

TWIN_FORMAT = 'train_step'
FWD_PARAMS = ['x', 'p', 'fox_norm', 'fox_w_in', 'fox_b_f', 'fox_w_out', 'dil_norm', 'dil_w_in', 'dil_w_out', 'ple_w_up', 'ple_w_gate', 'final_norm']
TWIN_WEIGHTS = ['fox_norm', 'fox_w_in', 'fox_b_f', 'fox_w_out', 'dil_norm', 'dil_w_in', 'dil_w_out', 'ple_w_up', 'ple_w_gate', 'final_norm']
TWIN_DIFF_INPUT = 'x'
TWIN_INPUTS = ['x', 'p', 'fox_norm', 'fox_w_in', 'fox_b_f', 'fox_w_out', 'dil_norm', 'dil_w_in', 'dil_w_out', 'ple_w_up', 'ple_w_gate', 'final_norm', 'loss_target', 'm_fox_norm', 'm_fox_w_in', 'm_fox_b_f', 'm_fox_w_out', 'm_dil_norm', 'm_dil_w_in', 'm_dil_w_out', 'm_ple_w_up', 'm_ple_w_gate', 'm_final_norm', 'v_fox_norm', 'v_fox_w_in', 'v_fox_b_f', 'v_fox_w_out', 'v_dil_norm', 'v_dil_w_in', 'v_dil_w_out', 'v_ple_w_up', 'v_ple_w_gate', 'v_final_norm']
TWIN_OUTPUTS = ['loss', 'grad_x', 'grad_fox_norm', 'grad_fox_w_in', 'grad_fox_b_f', 'grad_fox_w_out', 'grad_dil_norm', 'grad_dil_w_in', 'grad_dil_w_out', 'grad_ple_w_up', 'grad_ple_w_gate', 'grad_final_norm', 'delta_fox_norm', 'delta_fox_w_in', 'delta_fox_b_f', 'delta_fox_w_out', 'delta_dil_norm', 'delta_dil_w_in', 'delta_dil_w_out', 'delta_ple_w_up', 'delta_ple_w_gate', 'delta_final_norm', 'new_m_fox_norm', 'new_m_fox_w_in', 'new_m_fox_b_f', 'new_m_fox_w_out', 'new_m_dil_norm', 'new_m_dil_w_in', 'new_m_dil_w_out', 'new_m_ple_w_up', 'new_m_ple_w_gate', 'new_m_final_norm', 'new_v_fox_norm', 'new_v_fox_w_in', 'new_v_fox_b_f', 'new_v_fox_w_out', 'new_v_dil_norm', 'new_v_dil_w_in', 'new_v_dil_w_out', 'new_v_ple_w_up', 'new_v_ple_w_gate', 'new_v_final_norm']
TWIN_LEAF_KINDS = {'loss': 'loss', 'grad_x': 'grad_x', 'grad_fox_norm': 'grad_w', 'grad_fox_w_in': 'grad_w', 'grad_fox_b_f': 'grad_w', 'grad_fox_w_out': 'grad_w', 'grad_dil_norm': 'grad_w', 'grad_dil_w_in': 'grad_w', 'grad_dil_w_out': 'grad_w', 'grad_ple_w_up': 'grad_w', 'grad_ple_w_gate': 'grad_w', 'grad_final_norm': 'grad_w', 'delta_fox_norm': 'delta_w', 'delta_fox_w_in': 'delta_w', 'delta_fox_b_f': 'delta_w', 'delta_fox_w_out': 'delta_w', 'delta_dil_norm': 'delta_w', 'delta_dil_w_in': 'delta_w', 'delta_dil_w_out': 'delta_w', 'delta_ple_w_up': 'delta_w', 'delta_ple_w_gate': 'delta_w', 'delta_final_norm': 'delta_w', 'new_m_fox_norm': 'new_m', 'new_m_fox_w_in': 'new_m', 'new_m_fox_b_f': 'new_m', 'new_m_fox_w_out': 'new_m', 'new_m_dil_norm': 'new_m', 'new_m_dil_w_in': 'new_m', 'new_m_dil_w_out': 'new_m', 'new_m_ple_w_up': 'new_m', 'new_m_ple_w_gate': 'new_m', 'new_m_final_norm': 'new_m', 'new_v_fox_norm': 'new_v', 'new_v_fox_w_in': 'new_v', 'new_v_fox_b_f': 'new_v', 'new_v_fox_w_out': 'new_v', 'new_v_dil_norm': 'new_v', 'new_v_dil_w_in': 'new_v', 'new_v_dil_w_out': 'new_v', 'new_v_ple_w_up': 'new_v', 'new_v_ple_w_gate': 'new_v', 'new_v_final_norm': 'new_v'}


def _forward(args):
    return _fwd_reference(*[args[k] for k in FWD_PARAMS])


def _output_shape():
    def fwd():
        inp = _fwd_setup_inputs(0)
        return _fwd_reference(*[inp[k] for k in FWD_PARAMS])
    out = _jax.eval_shape(fwd)
    return out.shape, out.dtype

N_MICROBATCH = 1
ADAM_LR = 0.001
ADAM_B1 = 0.9
ADAM_B2 = 0.999
ADAM_EPS = 1e-08
ADAM_WD = 0.01
ADAM_STEP = 10
PER_EXAMPLE_BATCH_AXIS = {'x': 0, 'p': 1, 'loss_target': 0}
SHARED_INPUTS = []
_WEIGHT_DTYPES = {'fox_norm': _jnp.float32, 'fox_w_in': _jnp.float32, 'fox_b_f': _jnp.float32, 'fox_w_out': _jnp.float32, 'dil_norm': _jnp.float32, 'dil_w_in': _jnp.float32, 'dil_w_out': _jnp.float32, 'ple_w_up': _jnp.float32, 'ple_w_gate': _jnp.float32, 'final_norm': _jnp.float32}
MOMENT_SCALE = {'fox_norm': 9.697802e-02, 'fox_w_in': 4.861164e-02, 'fox_b_f': 3.025040e-01, 'fox_w_out': 5.446748e-02, 'dil_norm': 7.420387e-02, 'dil_w_in': 2.292519e-02, 'dil_w_out': 3.939331e-02, 'ple_w_up': 1.088862e-01, 'ple_w_gate': 4.514757e-02, 'final_norm': 6.404767e+01}


def _to_microbatches(a, axis):
    t = _jnp.moveaxis(a, axis, 0)
    t = t.reshape((N_MICROBATCH, t.shape[0] // N_MICROBATCH) + t.shape[1:])
    return _jnp.moveaxis(t, 1, axis + 1)


def setup_inputs(seed: int = 0) -> dict:
    inp = _fwd_setup_inputs(seed)
    key = _jax.random.fold_in(_jax.random.key(seed), 7919)
    shape, _ = _output_shape()
    out = dict(inp)
    out["loss_target"] = _jax.random.normal(_jax.random.fold_in(key, 0), shape, _jnp.float32)
    for i, name in enumerate(TWIN_WEIGHTS):
        w = inp[name].astype(_jnp.float32)
        if MOMENT_SCALE is None:
            s = _jnp.sqrt(_jnp.mean(_jnp.square(w)) + 1e-30)
        else:
            s = MOMENT_SCALE[name]
        km, kv = _jax.random.split(_jax.random.fold_in(key, i + 1))
        out[name] = w
        out["m_" + name] = s * _jax.random.normal(km, w.shape, _jnp.float32)
        out["v_" + name] = (s * s) * _jax.random.uniform(kv, w.shape, _jnp.float32, 0.5, 1.5)
    if N_MICROBATCH > 1:
        for name, axis in PER_EXAMPLE_BATCH_AXIS.items():
            out[name] = _to_microbatches(out[name], axis)
    return {'x': out['x'], 'p': out['p'], 'fox_norm': out['fox_norm'], 'fox_w_in': out['fox_w_in'], 'fox_b_f': out['fox_b_f'], 'fox_w_out': out['fox_w_out'], 'dil_norm': out['dil_norm'], 'dil_w_in': out['dil_w_in'], 'dil_w_out': out['dil_w_out'], 'ple_w_up': out['ple_w_up'], 'ple_w_gate': out['ple_w_gate'], 'final_norm': out['final_norm'], 'loss_target': out['loss_target'], 'm_fox_norm': out['m_fox_norm'], 'm_fox_w_in': out['m_fox_w_in'], 'm_fox_b_f': out['m_fox_b_f'], 'm_fox_w_out': out['m_fox_w_out'], 'm_dil_norm': out['m_dil_norm'], 'm_dil_w_in': out['m_dil_w_in'], 'm_dil_w_out': out['m_dil_w_out'], 'm_ple_w_up': out['m_ple_w_up'], 'm_ple_w_gate': out['m_ple_w_gate'], 'm_final_norm': out['m_final_norm'], 'v_fox_norm': out['v_fox_norm'], 'v_fox_w_in': out['v_fox_w_in'], 'v_fox_b_f': out['v_fox_b_f'], 'v_fox_w_out': out['v_fox_w_out'], 'v_dil_norm': out['v_dil_norm'], 'v_dil_w_in': out['v_dil_w_in'], 'v_dil_w_out': out['v_dil_w_out'], 'v_ple_w_up': out['v_ple_w_up'], 'v_ple_w_gate': out['v_ple_w_gate'], 'v_final_norm': out['v_final_norm']}


def _loss(weights, diff, rest, loss_target):
    with _jax.named_scope("forward"):
        args = {**rest, TWIN_DIFF_INPUT: diff, **{k: w.astype(_WEIGHT_DTYPES[k]) for k, w in weights.items()}}
        y = _forward(args)
    with _jax.named_scope("loss_head"):
        err = _jnp.square(y.astype(_jnp.float32) - loss_target)
        return 0.5 * _jnp.sum(_jnp.mean(err, axis=-1)) if err.ndim else 0.5 * err


def _adamw(w, g, m, v):
    m = ADAM_B1 * m + (1.0 - ADAM_B1) * g
    v = ADAM_B2 * v + (1.0 - ADAM_B2) * _jnp.square(g)
    m_hat = m / (1.0 - ADAM_B1 ** ADAM_STEP)
    v_hat = v / (1.0 - ADAM_B2 ** ADAM_STEP)
    delta = -ADAM_LR * (m_hat / (_jnp.sqrt(v_hat) + ADAM_EPS) + ADAM_WD * w)
    return delta, m, v


def reference(x, p, fox_norm, fox_w_in, fox_b_f, fox_w_out, dil_norm, dil_w_in, dil_w_out, ple_w_up, ple_w_gate, final_norm, loss_target, m_fox_norm, m_fox_w_in, m_fox_b_f, m_fox_w_out, m_dil_norm, m_dil_w_in, m_dil_w_out, m_ple_w_up, m_ple_w_gate, m_final_norm, v_fox_norm, v_fox_w_in, v_fox_b_f, v_fox_w_out, v_dil_norm, v_dil_w_in, v_dil_w_out, v_ple_w_up, v_ple_w_gate, v_final_norm):
    given = dict(x=x, p=p, fox_norm=fox_norm, fox_w_in=fox_w_in, fox_b_f=fox_b_f, fox_w_out=fox_w_out, dil_norm=dil_norm, dil_w_in=dil_w_in, dil_w_out=dil_w_out, ple_w_up=ple_w_up, ple_w_gate=ple_w_gate, final_norm=final_norm, loss_target=loss_target, m_fox_norm=m_fox_norm, m_fox_w_in=m_fox_w_in, m_fox_b_f=m_fox_b_f, m_fox_w_out=m_fox_w_out, m_dil_norm=m_dil_norm, m_dil_w_in=m_dil_w_in, m_dil_w_out=m_dil_w_out, m_ple_w_up=m_ple_w_up, m_ple_w_gate=m_ple_w_gate, m_final_norm=m_final_norm, v_fox_norm=v_fox_norm, v_fox_w_in=v_fox_w_in, v_fox_b_f=v_fox_b_f, v_fox_w_out=v_fox_w_out, v_dil_norm=v_dil_norm, v_dil_w_in=v_dil_w_in, v_dil_w_out=v_dil_w_out, v_ple_w_up=v_ple_w_up, v_ple_w_gate=v_ple_w_gate, v_final_norm=v_final_norm)
    weights = {n: given[n] for n in TWIN_WEIGHTS}
    shared = {n: given[n] for n in SHARED_INPUTS}
    per_example = {n: given[n] for n in ['x', 'p']}
    grad_fn = _jax.value_and_grad(_loss, argnums=(0, 1))

    def one_microbatch(ex, loss_target):
        ex = dict(ex)
        diff = ex.pop(TWIN_DIFF_INPUT)
        return grad_fn(weights, diff, {**shared, **ex}, loss_target)

    if N_MICROBATCH == 1:
        loss, (grad_w, grad_x) = one_microbatch(per_example, given["loss_target"])
    else:
        def body(carry, xs):
            loss_sum, grad_sum = carry
            l_k, (gw_k, gx_k) = one_microbatch(xs[0], xs[1])
            with _jax.named_scope("update"):
                return (loss_sum + l_k, _jax.tree.map(_jnp.add, grad_sum, gw_k)), gx_k

        init = (_jnp.zeros((), _jnp.float32), _jax.tree.map(_jnp.zeros_like, weights))
        (loss, grad_w), grad_x = _jax.lax.scan(body, init, (per_example, given["loss_target"]))
    with _jax.named_scope("update"):
        delta_w, new_m, new_v = {}, {}, {}
        for n in TWIN_WEIGHTS:
            delta_w[n], new_m[n], new_v[n] = _adamw(weights[n], grad_w[n], given["m_" + n], given["v_" + n])
    return (loss, grad_x, *[grad_w[n] for n in TWIN_WEIGHTS], *[delta_w[n] for n in TWIN_WEIGHTS],
            *[new_m[n] for n in TWIN_WEIGHTS], *[new_v[n] for n in TWIN_WEIGHTS])
```

```python
import functools

import numpy as np
import jax
import jax.numpy as jnp
from jax import lax
from jax.experimental import pallas as pl
from jax.experimental.pallas import tpu as pltpu

F32 = jnp.float32
BF16 = jnp.bfloat16

D_MODEL = 1024
PLE_DIM = 256
FOX_HEADS = 16
FOX_HEAD_DIM = 64
DIL_PATTERN = ((128, 1), (512, 4), (2048, 16))
DIL_GROUPS = 3
DIL_HEADS_PER_GROUP = 8
DIL_HEAD_DIM = 128
DIL_WINDOW_STEPS = 128
DIL_QKV = 3072
ALIBI_MAX_EXP = 8.0
RMS_EPS = 1e-6
ADAM_LR, ADAM_B1, ADAM_B2, ADAM_EPS, ADAM_WD, ADAM_STEP = 0.001, 0.9, 0.999, 1e-08, 0.01, 10

LANES = 128
VMEM_LIMIT = 56 * 1024 * 1024
MESH = pl.DeviceIdType.MESH
NEG = -1e30

FLAT_ROWS = (("fox_w_in", 1028, 1040), ("fox_w_out", 256, 256), ("dil_w_in", 2560, 2560), ("dil_w_out", 256, 256),
             ("ple_w_up", 128, 128), ("ple_w_gate", 512, 512))
FLAT_USED = sum(r for _, _, r in FLAT_ROWS)
FLAT_TOTAL = 4864
FLAT_HALF = FLAT_TOTAL // 2
FLAT_TILE = 608


def _cparams(*sem):
    return pltpu.CompilerParams(dimension_semantics=sem, vmem_limit_bytes=VMEM_LIMIT)


def _sigmoid(x):
    return 1.0 / (1.0 + jnp.exp(-x))


def _mm(a, b, *, name, ta=False, tb=False, out_dtype=F32, add=None):
    if ta:
        K, M = a.shape
    else:
        M, K = a.shape
    if tb:
        N, Kb = b.shape
    else:
        Kb, N = b.shape
    assert K == Kb, (a.shape, b.shape)
    tm, tn, tk = min(M, 1024), min(N, 1024), min(K, 1024)
    assert M % tm == 0 and N % tn == 0 and K % tk == 0, (M, N, K)
    nk = K // tk
    dn = (((0 if ta else 1,), (1 if tb else 0,)), ((), ()))

    def body(*refs):
        if add is None:
            a_ref, b_ref, o_ref, acc = refs
        else:
            a_ref, b_ref, add_ref, o_ref, acc = refs
        k = pl.program_id(2)

        @pl.when(k == 0)
        def _():
            acc[...] = jnp.zeros_like(acc)

        acc[...] += lax.dot_general(a_ref[...].astype(BF16), b_ref[...].astype(BF16), dn,
                                    preferred_element_type=F32)

        @pl.when(k == nk - 1)
        def _():
            r = acc[...]
            if add is not None:
                r = r + add_ref[...]
            o_ref[...] = r.astype(out_dtype)

    a_spec = (pl.BlockSpec((tk, tm), lambda i, j, k: (k, i)) if ta
              else pl.BlockSpec((tm, tk), lambda i, j, k: (i, k)))
    b_spec = (pl.BlockSpec((tn, tk), lambda i, j, k: (j, k)) if tb
              else pl.BlockSpec((tk, tn), lambda i, j, k: (k, j)))
    in_specs = [a_spec, b_spec]
    args = [a, b]
    if add is not None:
        in_specs.append(pl.BlockSpec((tm, tn), lambda i, j, k: (i, j)))
        args.append(add)
    return pl.pallas_call(
        body, name=name, grid=(M // tm, N // tn, nk),
        in_specs=in_specs, out_specs=pl.BlockSpec((tm, tn), lambda i, j, k: (i, j)),
        out_shape=jax.ShapeDtypeStruct((M, N), out_dtype),
        scratch_shapes=[pltpu.VMEM((tm, tn), F32)],
        compiler_params=_cparams("parallel", "parallel", "arbitrary"),
    )(*args)


def _rows(fn, ins, outs, *, name, tm, bcast=(), acc=()):
    R = ins[0].shape[0]
    assert R % tm == 0, (R, tm)
    n_in, n_b, n_out, n_acc = len(ins), len(bcast), len(outs), len(acc)

    def body(*refs):
        in_refs = refs[:n_in + n_b]
        out_refs = refs[n_in + n_b:n_in + n_b + n_out]
        acc_refs = refs[n_in + n_b + n_out:]
        res = fn(*[r[...] for r in in_refs])
        if not isinstance(res, (tuple, list)):
            res = (res,)
        for r, v in zip(out_refs, res[:n_out]):
            r[...] = v.astype(r.dtype)
        first = pl.program_id(0) == 0
        for r, v in zip(acc_refs, res[n_out:]):
            @pl.when(first)
            def _(r=r, v=v):
                r[...] = v.astype(r.dtype)

            @pl.when(jnp.logical_not(first))
            def _(r=r, v=v):
                r[...] += v.astype(r.dtype)

    in_specs = [pl.BlockSpec((tm, a.shape[1]), lambda i: (i, 0)) for a in ins]
    in_specs += [pl.BlockSpec(b.shape, lambda i, nd=b.ndim: (0,) * nd) for b in bcast]
    out_specs = [pl.BlockSpec((tm, c), lambda i: (i, 0)) for c, _ in outs]
    out_specs += [pl.BlockSpec(s, lambda i, nd=len(s): (0,) * nd) for s, _ in acc]
    out_shape = [jax.ShapeDtypeStruct((R, c), dt) for c, dt in outs]
    out_shape += [jax.ShapeDtypeStruct(s, dt) for s, dt in acc]
    res = pl.pallas_call(
        body, name=name, grid=(R // tm,), in_specs=in_specs, out_specs=out_specs, out_shape=out_shape,
        compiler_params=_cparams("arbitrary" if acc else "parallel"),
    )(*ins, *bcast)
    return res


def _cumsum_rows(ins, *, name, reverse=False, pre=None, bcast=()):
    S, C = ins[0].shape
    tb = 256
    nb = S // tb
    assert S % tb == 0
    n_in = len(ins) + len(bcast)

    def body(*refs):
        in_refs, o_ref, carry = refs[:n_in], refs[n_in], refs[n_in + 1]

        @pl.when(pl.program_id(0) == 0)
        def _():
            carry[...] = jnp.zeros_like(carry)

        xv = in_refs[0][...] if pre is None else pre(*[r[...] for r in in_refs])
        r_ = lax.broadcasted_iota(jnp.int32, (tb, tb), 0)
        c_ = lax.broadcasted_iota(jnp.int32, (tb, tb), 1)
        tri = jnp.where((c_ >= r_) if reverse else (c_ <= r_), 1.0, 0.0).astype(BF16)
        hi = xv.astype(BF16)
        r1 = xv - hi.astype(F32)
        mid = r1.astype(BF16)
        lo = (r1 - mid.astype(F32)).astype(BF16)
        cs = (jnp.dot(tri, hi, preferred_element_type=F32) + jnp.dot(tri, mid, preferred_element_type=F32)
              + jnp.dot(tri, lo, preferred_element_type=F32)) + carry[...]
        o_ref[...] = cs
        carry[...] = cs[0:1, :] if reverse else cs[tb - 1:tb, :]

    blk = (lambda i: (nb - 1 - i, 0)) if reverse else (lambda i: (i, 0))
    in_specs = [pl.BlockSpec((tb, C), blk) for _ in ins]
    in_specs += [pl.BlockSpec(b.shape, lambda i, nd=b.ndim: (0,) * nd) for b in bcast]
    return pl.pallas_call(
        body, name=name, grid=(nb,), in_specs=in_specs, out_specs=pl.BlockSpec((tb, C), blk),
        out_shape=jax.ShapeDtypeStruct((S, C), F32), scratch_shapes=[pltpu.VMEM((1, C), F32)],
        compiler_params=_cparams("arbitrary"),
    )(*ins, *bcast)


def _rms_fwd(x, g):
    r = lax.rsqrt(jnp.mean(x * x, axis=1, keepdims=True) + RMS_EPS)
    return (x * r) * g


def _rms_bwd(dn, x, dres, g):
    r = lax.rsqrt(jnp.mean(x * x, axis=1, keepdims=True) + RMS_EPS)
    xh = x * r
    w = dn * g
    dx = r * (w - xh * jnp.mean(w * xh, axis=1, keepdims=True))
    return dres + dx, jnp.sum(dn * xh, axis=0, keepdims=True)


def _final_stage(h, tgt, g):
    r = lax.rsqrt(jnp.mean(h * h, axis=1, keepdims=True) + RMS_EPS)
    xh = h * r
    diff = xh * g - tgt
    loss = 0.5 * jnp.sum(jnp.mean(diff * diff, axis=1, keepdims=True), axis=0, keepdims=True)
    dy = diff * (1.0 / D_MODEL)
    w = dy * g
    dh = r * (w - xh * jnp.mean(w * xh, axis=1, keepdims=True))
    return dh, jnp.sum(dy * xh, axis=0, keepdims=True), jnp.broadcast_to(loss, (1, LANES))


def _ple_fwd(h, u, t):
    return h + u * _sigmoid(t)


def _ple_bwd(dh, u, t):
    s = _sigmoid(t)
    return dh * s, dh * u * s * (1.0 - s)


def _gate_fwd(o, z):
    return o * (z * _sigmoid(z))


def _head_sums(prod, head_dim):
    tm, width = prod.shape
    lane = lax.broadcasted_iota(jnp.int32, (tm, LANES), 1)
    cols = []
    for b in range(width // LANES):
        blk = prod[:, b * LANES:(b + 1) * LANES]
        if head_dim == LANES:
            cols.append(jnp.broadcast_to(jnp.sum(blk, axis=1, keepdims=True), (tm, LANES)))
        else:
            low = lane < head_dim
            sa = jnp.sum(jnp.where(low, blk, 0.0), axis=1, keepdims=True)
            sb = jnp.sum(jnp.where(low, 0.0, blk), axis=1, keepdims=True)
            cols.append(jnp.where(low, sa, sb))
    return jnp.concatenate(cols, axis=1)


def _gate_bwd(dg, o, z, *, head_dim):
    s = _sigmoid(z)
    do = dg * (z * s)
    dz = dg * o * (s * (1.0 + z * (1.0 - s)))
    return do, dz, _head_sums(do * o, head_dim)


def _gate_bwd_fox(dg, o, z, a):
    do, dz, dl = _gate_bwd(dg, o, z, head_dim=FOX_HEAD_DIM)
    lane = lax.broadcasted_iota(jnp.int32, do.shape, 1)
    return do, dz, jnp.where(lane % FOX_HEAD_DIM < FOX_HEAD_DIM // 2, a, dl)


def _forget_dc(dkey, drow):
    lane = lax.broadcasted_iota(jnp.int32, drow.shape, 1)
    return jnp.where(lane % FOX_HEAD_DIM == 0, drow, 0.0) - dkey


def _dil_combine(o1, o2, o3, l1, l2, l3, z):
    m = jnp.maximum(jnp.maximum(l1, l2), l3)
    e1, e2, e3 = jnp.exp(l1 - m), jnp.exp(l2 - m), jnp.exp(l3 - m)
    tot = e1 + e2 + e3
    o = (e1 * o1 + e2 * o2 + e3 * o3) / tot
    return o, o * (z * _sigmoid(z)), m + jnp.log(tot)


def _log_forget(fl, b):
    u = fl + b
    return jnp.minimum(u, 0.0) - jnp.log(1.0 + jnp.exp(-jnp.abs(u)))


def _forget_bwd(dlogf, fl, b):
    du = dlogf * (1.0 / (1.0 + jnp.exp(fl + b)))
    return du, jnp.sum(du, axis=0, keepdims=True)


def _adamw(w, g, m, v):
    m = ADAM_B1 * m + (1.0 - ADAM_B1) * g
    v = ADAM_B2 * v + (1.0 - ADAM_B2) * (g * g)
    m_hat = m / (1.0 - ADAM_B1 ** ADAM_STEP)
    v_hat = v / (1.0 - ADAM_B2 ** ADAM_STEP)
    delta = -ADAM_LR * (m_hat / (jnp.sqrt(v_hat) + ADAM_EPS) + ADAM_WD * w)
    return delta, m, v


FOX_TQ = 512


def _fox_fwd(qkv, c_full, c_row):
    S = qkv.shape[0]
    tq = min(FOX_TQ, S)
    nq = S // tq

    def body(q_ref, k_ref, v_ref, c_ref, crow_ref, o_ref, a_ref):
        i = pl.program_id(1)
        lane = lax.broadcasted_iota(jnp.int32, (tq, LANES), 1)
        low = lane < FOX_HEAD_DIM
        row = lax.broadcasted_iota(jnp.int32, (tq, tq), 0)
        col = lax.broadcasted_iota(jnp.int32, (tq, tq), 1)
        q2 = q_ref[...]
        c_blk = c_ref[...]
        o_heads, a_heads = [], []
        for hh in range(2):
            qm = jnp.where(low if hh == 0 else jnp.logical_not(low), q2, jnp.zeros_like(q2))
            ct = c_blk[:, hh * FOX_HEAD_DIM:hh * FOX_HEAD_DIM + 1]

            def step(kb, carry, diag, qm=qm, ct=ct, hh=hh):
                m, l, acc = carry
                start = pl.multiple_of(kb * tq, tq)
                kblk = k_ref[pl.ds(start, tq), :]
                vblk = v_ref[pl.ds(start, tq), :]
                cs = crow_ref[0, hh:hh + 1, pl.ds(start, tq)]
                s = lax.dot_general(qm, kblk, (((1,), (1,)), ((), ())), preferred_element_type=F32)
                s = s + (ct - cs)
                if diag:
                    s = jnp.where(col <= row, s, NEG)
                m_new = jnp.maximum(m, jnp.max(s, axis=1, keepdims=True))
                alpha = jnp.exp(m - m_new)
                pr = jnp.exp(s - m_new)
                l = alpha * l + jnp.sum(pr, axis=1, keepdims=True)
                acc = alpha * acc + jnp.dot(pr.astype(BF16), vblk, preferred_element_type=F32)
                return m_new, l, acc

            init = (jnp.full((tq, 1), NEG, F32), jnp.zeros((tq, 1), F32), jnp.zeros((tq, LANES), F32))
            carry = lax.fori_loop(0, i, functools.partial(step, diag=False), init)
            m, l, acc = step(i, carry, True)
            o_heads.append(acc / l)
            a_heads.append(jnp.broadcast_to(ct - (m + jnp.log(l)), (tq, LANES)))
        o_ref[...] = jnp.where(low, o_heads[0], o_heads[1])
        a_ref[...] = jnp.where(low, a_heads[0], a_heads[1])

    return pl.pallas_call(
        body, name="fox_attn_fwd", grid=(FOX_HEADS // 2, nq),
        in_specs=[pl.BlockSpec((tq, LANES), lambda h, i: (i, h)),
                  pl.BlockSpec((S, LANES), lambda h, i: (0, 8 + h)),
                  pl.BlockSpec((S, LANES), lambda h, i: (0, 16 + h)),
                  pl.BlockSpec((tq, LANES), lambda h, i: (i, h)),
                  pl.BlockSpec((1, 8, S), lambda h, i: (h, 0, 0))],
        out_specs=[pl.BlockSpec((tq, LANES), lambda h, i: (i, h)),
                   pl.BlockSpec((tq, LANES), lambda h, i: (i, h))],
        out_shape=[jax.ShapeDtypeStruct((S, D_MODEL), F32), jax.ShapeDtypeStruct((S, D_MODEL), F32)],
        compiler_params=_cparams("parallel", "arbitrary"),
    )(qkv, qkv, qkv, c_full, c_row)


def _fox_bwd(qkv, do, ad_full, c_row):
    S = qkv.shape[0]
    tq = min(FOX_TQ, S)
    nq = S // tq

    def body(q_ref, do_ref, ad_ref, k_ref, v_ref, crow_ref, dq_ref, dr_ref, dk_ref, dv_ref, dc_ref):
        kb = pl.program_id(1)
        lane = lax.broadcasted_iota(jnp.int32, (tq, LANES), 1)
        low = lane < FOX_HEAD_DIM
        row = lax.broadcasted_iota(jnp.int32, (tq, tq), 0)
        col = lax.broadcasted_iota(jnp.int32, (tq, tq), 1)

        @pl.when(kb == 0)
        def _():
            dq_ref[...] = jnp.zeros_like(dq_ref)
            dr_ref[...] = jnp.zeros_like(dr_ref)

        k2 = k_ref[...]
        v2 = v_ref[...]
        kstart = pl.multiple_of(kb * tq, tq)
        dk_heads, dv_heads, dc_rows = [], [], []
        for hh in range(2):
            sel = low if hh == 0 else jnp.logical_not(low)
            km = jnp.where(sel, k2, jnp.zeros_like(k2))
            vm = jnp.where(sel, v2, jnp.zeros_like(v2))
            cs = crow_ref[0, hh:hh + 1, pl.ds(kstart, tq)]

            def step(qb, carry, diag, km=km, vm=vm, cs=cs, hh=hh, sel=sel):
                dk, dv, dc = carry
                start = pl.multiple_of(qb * tq, tq)
                qblk = q_ref[pl.ds(start, tq), :]
                doblk = do_ref[pl.ds(start, tq), :]
                at = ad_ref[pl.ds(start, tq), hh * FOX_HEAD_DIM:hh * FOX_HEAD_DIM + 1]
                dt = ad_ref[pl.ds(start, tq), hh * FOX_HEAD_DIM + 32:hh * FOX_HEAD_DIM + 33]
                s = lax.dot_general(qblk, km, (((1,), (1,)), ((), ())), preferred_element_type=F32)
                pr = jnp.exp(s + (at - cs))
                if diag:
                    pr = jnp.where(col <= row, pr, 0.0)
                dp = lax.dot_general(doblk, vm, (((1,), (1,)), ((), ())), preferred_element_type=F32)
                ds = pr * (dp - dt)
                prb = pr.astype(BF16)
                dsb = ds.astype(BF16)
                dv = dv + lax.dot_general(prb, doblk, (((0,), (0,)), ((), ())), preferred_element_type=F32)
                dk = dk + lax.dot_general(dsb, qblk, (((0,), (0,)), ((), ())), preferred_element_type=F32)
                dq_ref[pl.ds(start, tq), :] += jnp.dot(dsb, km, preferred_element_type=F32)
                dr_ref[pl.ds(start, tq), :] += jnp.where(sel, jnp.sum(ds, axis=1, keepdims=True), 0.0)
                dc = dc + jnp.sum(ds, axis=0, keepdims=True)
                return dk, dv, dc

            init = (jnp.zeros((tq, LANES), F32), jnp.zeros((tq, LANES), F32), jnp.zeros((1, tq), F32))
            carry = step(kb, init, True)
            dk, dv, dc = lax.fori_loop(kb + 1, nq, functools.partial(step, diag=False), carry)
            dk_heads.append(dk)
            dv_heads.append(dv)
            dc_rows.append(dc)
        dk_ref[...] = jnp.where(low, dk_heads[0], dk_heads[1]).astype(BF16)
        dv_ref[...] = jnp.where(low, dv_heads[0], dv_heads[1]).astype(BF16)
        dc_ref[0] = jnp.concatenate(dc_rows + [jnp.zeros((6, tq), F32)], axis=0)

        @pl.when(kb == nq - 1)
        def _():
            dq_ref[...] = dq_ref[...] * (FOX_HEAD_DIM ** -0.5)

    res = lambda cb: pl.BlockSpec((S, LANES), lambda h, k, cb=cb: (0, cb + h))
    blk = lambda cb: pl.BlockSpec((tq, LANES), lambda h, k, cb=cb: (k, cb + h))
    return pl.pallas_call(
        body, name="fox_attn_bwd", grid=(FOX_HEADS // 2, nq),
        in_specs=[res(0), res(0), res(0), blk(8), blk(16),
                  pl.BlockSpec((1, 8, S), lambda h, k: (h, 0, 0))],
        out_specs=[res(0), res(0), blk(0), blk(0), pl.BlockSpec((1, 8, tq), lambda h, k: (h, 0, k))],
        out_shape=[jax.ShapeDtypeStruct((S, D_MODEL), F32), jax.ShapeDtypeStruct((S, D_MODEL), F32),
                   jax.ShapeDtypeStruct((S, D_MODEL), BF16),
                   jax.ShapeDtypeStruct((S, D_MODEL), BF16), jax.ShapeDtypeStruct((FOX_HEADS // 2, 8, S), F32)],
        compiler_params=_cparams("parallel", "arbitrary"),
    )(qkv, do, ad_full, qkv, qkv, c_row)


def _alibi_slopes():
    n = DIL_GROUPS * DIL_HEADS_PER_GROUP
    s = np.float32(2.0) ** (np.float32(-ALIBI_MAX_EXP) * np.arange(1, n + 1, dtype=np.float32) / np.float32(n))
    return s.astype(np.float32).reshape(DIL_GROUPS, DIL_HEADS_PER_GROUP)


W = DIL_WINDOW_STEPS
DIL_SCALE = DIL_HEAD_DIM ** -0.5


def _dil_bias(g, h, dist, dilation):
    return (-float(_alibi_slopes()[g, h])) * (dist * dilation).astype(F32)


def _dil_fwd(proj, g):
    S = proj.shape[0]
    d = DIL_PATTERN[g][1]
    L = S // d
    nb = L // W
    pv = proj.reshape(L, d * 3 * DIL_QKV)
    nblk = 3 * DIL_QKV // D_MODEL

    def body(q_ref, kp_ref, kc_ref, vp_ref, vc_ref, o_ref, l_ref):
        n = pl.program_id(1)
        i_ = lax.broadcasted_iota(jnp.int32, (W, 2 * W), 0)
        j_ = lax.broadcasted_iota(jnp.int32, (W, 2 * W), 1)
        dist = W + i_ - j_
        valid = (dist >= 0) & (dist <= W) & ((j_ >= W) | (n > 0))
        for h in range(DIL_HEADS_PER_GROUP):
            sl = slice(h * LANES, (h + 1) * LANES)
            k2 = jnp.concatenate([kp_ref[:, sl], kc_ref[:, sl]], axis=0)
            v2 = jnp.concatenate([vp_ref[:, sl], vc_ref[:, sl]], axis=0)
            s = lax.dot_general(q_ref[:, sl], k2, (((1,), (1,)), ((), ())), preferred_element_type=F32)
            s = s * DIL_SCALE + _dil_bias(g, h, dist, d)
            s = jnp.where(valid, s, NEG)
            m = jnp.max(s, axis=1, keepdims=True)
            pr = jnp.exp(s - m)
            l = jnp.sum(pr, axis=1, keepdims=True)
            o_ref[:, sl] = jnp.dot(pr.astype(BF16), v2, preferred_element_type=F32) / l
            l_ref[:, sl] = jnp.broadcast_to(m + jnp.log(l), (W, LANES))

    spec = lambda off, prev: pl.BlockSpec(
        (W, D_MODEL), lambda r, n: ((jnp.maximum(n - 1, 0) if prev else n), r * nblk + off + g))
    out_spec = pl.BlockSpec((W, D_MODEL), lambda r, n: (n, r))
    o, lse = pl.pallas_call(
        body, name=f"dil_attn_fwd_g{g}", grid=(d, nb),
        in_specs=[spec(0, False), spec(3, True), spec(3, False), spec(6, True), spec(6, False)],
        out_specs=[out_spec, out_spec],
        out_shape=[jax.ShapeDtypeStruct((L, d * D_MODEL), F32), jax.ShapeDtypeStruct((L, d * D_MODEL), F32)],
        compiler_params=_cparams("parallel", "parallel"),
    )(pv, pv, pv, pv, pv)
    return o.reshape(S, D_MODEL), lse.reshape(S, D_MODEL)


def _dil_bwd_q(proj, do, lse, dlt, g):
    S = proj.shape[0]
    d = DIL_PATTERN[g][1]
    L = S // d
    nb = L // W
    pv = proj.reshape(L, d * 3 * DIL_QKV)
    nblk = 3 * DIL_QKV // D_MODEL

    def body(q_ref, kp_ref, kc_ref, vp_ref, vc_ref, do_ref, l_ref, d_ref, dq_ref):
        n = pl.program_id(1)
        i_ = lax.broadcasted_iota(jnp.int32, (W, 2 * W), 0)
        j_ = lax.broadcasted_iota(jnp.int32, (W, 2 * W), 1)
        dist = W + i_ - j_
        valid = (dist >= 0) & (dist <= W) & ((j_ >= W) | (n > 0))
        for h in range(DIL_HEADS_PER_GROUP):
            sl = slice(h * LANES, (h + 1) * LANES)
            k2 = jnp.concatenate([kp_ref[:, sl], kc_ref[:, sl]], axis=0)
            v2 = jnp.concatenate([vp_ref[:, sl], vc_ref[:, sl]], axis=0)
            s = lax.dot_general(q_ref[:, sl], k2, (((1,), (1,)), ((), ())), preferred_element_type=F32)
            s = s * DIL_SCALE + _dil_bias(g, h, dist, d)
            pr = jnp.where(valid, jnp.exp(s - l_ref[:, h * LANES:h * LANES + 1]), 0.0)
            dp = lax.dot_general(do_ref[:, sl], v2, (((1,), (1,)), ((), ())), preferred_element_type=F32)
            ds = pr * (dp - d_ref[:, h * LANES:h * LANES + 1])
            dq_ref[:, sl] = (jnp.dot(ds.astype(BF16), k2, preferred_element_type=F32) * DIL_SCALE).astype(BF16)

    spec = lambda off, prev: pl.BlockSpec(
        (W, D_MODEL), lambda r, n: ((jnp.maximum(n - 1, 0) if prev else n), r * nblk + off + g))
    row_spec = pl.BlockSpec((W, D_MODEL), lambda r, n: (n, r))
    dv_ = lambda a: a.reshape(L, d * D_MODEL)
    dq = pl.pallas_call(
        body, name=f"dil_attn_bwd_q_g{g}", grid=(d, nb),
        in_specs=[spec(0, False), spec(3, True), spec(3, False), spec(6, True), spec(6, False),
                  row_spec, row_spec, row_spec],
        out_specs=row_spec, out_shape=jax.ShapeDtypeStruct((L, d * D_MODEL), BF16),
        compiler_params=_cparams("parallel", "parallel"),
    )(pv, pv, pv, pv, pv, dv_(do), dv_(lse), dv_(dlt))
    return dq.reshape(S, D_MODEL)


def _dil_bwd_kv(proj, do, lse, dlt, g):
    S = proj.shape[0]
    d = DIL_PATTERN[g][1]
    L = S // d
    nb = L // W
    pv = proj.reshape(L, d * 3 * DIL_QKV)
    nblk = 3 * DIL_QKV // D_MODEL

    def body(k_ref, v_ref, qc_ref, qn_ref, doc_ref, don_ref, lc_ref, ln_ref, dc_ref, dn_ref, dk_ref, dv_ref):
        m_ = pl.program_id(1)
        i_ = lax.broadcasted_iota(jnp.int32, (2 * W, W), 0)
        j_ = lax.broadcasted_iota(jnp.int32, (2 * W, W), 1)
        dist = i_ - j_
        valid = (dist >= 0) & (dist <= W) & ((i_ < W) | (m_ < nb - 1))
        for h in range(DIL_HEADS_PER_GROUP):
            sl = slice(h * LANES, (h + 1) * LANES)
            one = slice(h * LANES, h * LANES + 1)
            q2 = jnp.concatenate([qc_ref[:, sl], qn_ref[:, sl]], axis=0)
            do2 = jnp.concatenate([doc_ref[:, sl], don_ref[:, sl]], axis=0)
            l2 = jnp.concatenate([lc_ref[:, one], ln_ref[:, one]], axis=0)
            d2 = jnp.concatenate([dc_ref[:, one], dn_ref[:, one]], axis=0)
            s = lax.dot_general(q2, k_ref[:, sl], (((1,), (1,)), ((), ())), preferred_element_type=F32)
            s = s * DIL_SCALE + _dil_bias(g, h, dist, d)
            pr = jnp.where(valid, jnp.exp(s - l2), 0.0)
            dp = lax.dot_general(do2, v_ref[:, sl], (((1,), (1,)), ((), ())), preferred_element_type=F32)
            ds = pr * (dp - d2)
            dv_ref[:, sl] = lax.dot_general(pr.astype(BF16), do2, (((0,), (0,)), ((), ())),
                                            preferred_element_type=F32).astype(BF16)
            dk_ref[:, sl] = (lax.dot_general(ds.astype(BF16), q2, (((0,), (0,)), ((), ())),
                                             preferred_element_type=F32) * DIL_SCALE).astype(BF16)

    spec = lambda off, nxt: pl.BlockSpec(
        (W, D_MODEL), lambda r, n: ((jnp.minimum(n + 1, nb - 1) if nxt else n), r * nblk + off + g))
    row = lambda nxt: pl.BlockSpec((W, D_MODEL), lambda r, n: ((jnp.minimum(n + 1, nb - 1) if nxt else n), r))
    dv_ = lambda a: a.reshape(L, d * D_MODEL)
    dk, dv = pl.pallas_call(
        body, name=f"dil_attn_bwd_kv_g{g}", grid=(d, nb),
        in_specs=[spec(3, False), spec(6, False), spec(0, False), spec(0, True),
                  row(False), row(True), row(False), row(True), row(False), row(True)],
        out_specs=[row(False), row(False)],
        out_shape=[jax.ShapeDtypeStruct((L, d * D_MODEL), BF16), jax.ShapeDtypeStruct((L, d * D_MODEL), BF16)],
        compiler_params=_cparams("parallel", "parallel"),
    )(pv, pv, pv, pv, dv_(do), dv_(do), dv_(lse), dv_(lse), dv_(dlt), dv_(dlt))
    return dk.reshape(S, D_MODEL), dv.reshape(S, D_MODEL)


ANY = pl.BlockSpec(memory_space=pl.ANY)


def _place():
    x, y, c = lax.axis_index("x"), lax.axis_index("y"), lax.axis_index("c")
    chips = [(1 - x, y), (x, 1 - y), (1 - x, 1 - y)]
    return x, y, c, chips


def _gather_weights(wb):
    R = wb.shape[0]
    H = R // 2

    def body(w_ref, out_ref, send1, recv1, send2, recv2, lsem):
        x, y, c, chips = _place()
        me = 2 * x + y
        sib = (x, y, 1 - c)
        half = pl.ds(c * H, H)
        mine = pltpu.make_async_copy(w_ref, out_ref.at[me], lsem)
        mine.start()
        first = [pltpu.make_async_remote_copy(
            src_ref=w_ref.at[half], dst_ref=out_ref.at[me, half], send_sem=send1.at[k], recv_sem=recv1.at[k],
            device_id=(*chip, c), device_id_type=MESH) for k, chip in enumerate(chips)]
        for cp in first:
            cp.start()
        passed = []
        for k, (cx, cy) in enumerate(chips):
            slot = out_ref.at[2 * cx + cy, half]
            pltpu.make_async_remote_copy(src_ref=slot, dst_ref=slot, send_sem=send1.at[k], recv_sem=recv1.at[k],
                                         device_id=(cx, cy, c), device_id_type=MESH).wait_recv()
            cp = pltpu.make_async_remote_copy(src_ref=slot, dst_ref=slot, send_sem=send2.at[k], recv_sem=recv2.at[k],
                                              device_id=sib, device_id_type=MESH)
            cp.start()
            passed.append(cp)
        for k, (cx, cy) in enumerate(chips):
            slot = out_ref.at[2 * cx + cy, pl.ds((1 - c) * H, H)]
            pltpu.make_async_remote_copy(src_ref=slot, dst_ref=slot, send_sem=send2.at[k], recv_sem=recv2.at[k],
                                         device_id=sib, device_id_type=MESH).wait_recv()
        for cp in first + passed:
            cp.wait_send()
        mine.wait()

    return pl.pallas_call(
        body, name="gather_weights", in_specs=[ANY], out_specs=ANY,
        out_shape=jax.ShapeDtypeStruct((4, R, D_MODEL), wb.dtype),
        scratch_shapes=[pltpu.SemaphoreType.DMA((3,)), pltpu.SemaphoreType.DMA((3,)),
                        pltpu.SemaphoreType.DMA((3,)), pltpu.SemaphoreType.DMA((3,)), pltpu.SemaphoreType.DMA],
    )(wb)


def _swap_halves(g):
    H = g.shape[1] // 2

    def body(g_ref, out_ref, send, recv):
        x, y, c, _ = _place()
        sib = (x, y, 1 - c)
        cps = [pltpu.make_async_remote_copy(
            src_ref=g_ref.at[s, pl.ds((1 - c) * H, H)], dst_ref=out_ref.at[s], send_sem=send.at[s],
            recv_sem=recv.at[s], device_id=sib, device_id_type=MESH) for s in range(4)]
        for cp in cps:
            cp.start()
        for cp in cps:
            cp.wait()

    return pl.pallas_call(
        body, name="swap_halves", in_specs=[ANY], out_specs=ANY,
        out_shape=jax.ShapeDtypeStruct((4, H, D_MODEL), g.dtype),
        scratch_shapes=[pltpu.SemaphoreType.DMA((4,)), pltpu.SemaphoreType.DMA((4,))],
    )(g)


def _scatter_partials(pb):
    def body(p_ref, out_ref, send, recv, lsem):
        x, y, c, chips = _place()
        me = 2 * x + y
        mine = pltpu.make_async_copy(p_ref.at[me], out_ref.at[me], lsem)
        mine.start()
        cps = [pltpu.make_async_remote_copy(
            src_ref=p_ref.at[2 * cx + cy], dst_ref=out_ref.at[me], send_sem=send.at[k], recv_sem=recv.at[k],
            device_id=(cx, cy, c), device_id_type=MESH) for k, (cx, cy) in enumerate(chips)]
        for cp in cps:
            cp.start()
        for k, (cx, cy) in enumerate(chips):
            slot = out_ref.at[2 * cx + cy]
            pltpu.make_async_remote_copy(src_ref=slot, dst_ref=slot, send_sem=send.at[k], recv_sem=recv.at[k],
                                         device_id=(cx, cy, c), device_id_type=MESH).wait_recv()
        for cp in cps:
            cp.wait_send()
        mine.wait()

    return pl.pallas_call(
        body, name="scatter_partials", in_specs=[ANY], out_specs=ANY,
        out_shape=jax.ShapeDtypeStruct(pb.shape, pb.dtype),
        scratch_shapes=[pltpu.SemaphoreType.DMA((3,)), pltpu.SemaphoreType.DMA((3,)), pltpu.SemaphoreType.DMA],
    )(pb)


def _join_halves(f):
    H = f.shape[0]

    def body(f_ref, out_ref, send, recv, lsem):
        x, y, c, _ = _place()
        mine = pltpu.make_async_copy(f_ref, out_ref.at[pl.ds(c * H, H)], lsem)
        mine.start()
        cp = pltpu.make_async_remote_copy(
            src_ref=f_ref, dst_ref=out_ref.at[pl.ds(c * H, H)], send_sem=send, recv_sem=recv,
            device_id=(x, y, 1 - c), device_id_type=MESH)
        cp.start()
        other = out_ref.at[pl.ds((1 - c) * H, H)]
        pltpu.make_async_remote_copy(src_ref=other, dst_ref=other, send_sem=send, recv_sem=recv,
                                     device_id=(x, y, 1 - c), device_id_type=MESH).wait_recv()
        cp.wait_send()
        mine.wait()

    return pl.pallas_call(
        body, name="join_halves", in_specs=[ANY], out_specs=ANY,
        out_shape=jax.ShapeDtypeStruct((2 * H, D_MODEL), f.dtype),
        scratch_shapes=[pltpu.SemaphoreType.DMA, pltpu.SemaphoreType.DMA, pltpu.SemaphoreType.DMA],
    )(f)


def _gather_tiles(tile, name):
    m_per = tile.shape[0]

    def body(x_ref, out_ref, send_sems, recv_sems, local_sem):
        x, y, c, chips = _place()
        me, sibling = (x, y, c), (x, y, 1 - c)

        def rows(px, py, pc):
            return out_ref.at[pl.ds((4 * px + 2 * py + pc) * m_per, m_per), :]

        def copy(k, block, to, src=None):
            return pltpu.make_async_remote_copy(
                src_ref=rows(*block) if src is None else src, dst_ref=rows(*block),
                send_sem=send_sems.at[k], recv_sem=recv_sems.at[k], device_id=to, device_id_type=MESH)

        mine = pltpu.make_async_copy(x_ref, rows(*me), local_sem)
        mine.start()
        first = [copy(0, me, sibling, src=x_ref)]
        first += [copy(1 + j, me, (*chip, c), src=x_ref) for j, chip in enumerate(chips)]
        for cp in first:
            cp.start()
        passed = [copy(4 + j, (*chip, c), sibling) for j, chip in enumerate(chips)]
        for j, chip in enumerate(chips):
            copy(1 + j, (*chip, c), me).wait_recv()
            passed[j].start()
        copy(0, sibling, me).wait_recv()
        for j, chip in enumerate(chips):
            copy(4 + j, (*chip, 1 - c), me).wait_recv()
        for cp in first + passed:
            cp.wait_send()
        mine.wait()

    return pl.pallas_call(
        body, name=name, out_shape=jax.ShapeDtypeStruct((8 * m_per, D_MODEL), tile.dtype),
        in_specs=[pl.BlockSpec(memory_space=pltpu.VMEM)], out_specs=pl.BlockSpec(memory_space=pltpu.VMEM),
        scratch_shapes=[pltpu.SemaphoreType.DMA((7,)), pltpu.SemaphoreType.DMA((7,)), pltpu.SemaphoreType.DMA],
    )(tile)


def _pad_rows(a, rows):
    return jnp.pad(a, ((0, rows - a.shape[0]), (0, 0)))


def _pad_row(v):
    v = v.reshape(1, -1)
    return jnp.pad(v, ((0, 0), (0, D_MODEL - v.shape[1])))


def _pack_shard(fwi, fwo, dwi, dwo, pwu, pwg):
    parts = [_pad_rows(fwi.reshape(1028, D_MODEL), 1040), fwo.reshape(256, D_MODEL), dwi.reshape(2560, D_MODEL),
             dwo.reshape(256, D_MODEL), pwu.reshape(128, D_MODEL), pwg.reshape(512, D_MODEL),
             jnp.zeros((FLAT_TOTAL - FLAT_USED, D_MODEL), fwi.dtype)]
    return jnp.concatenate(parts, axis=0)


def _unpack_shard(flat):
    out, r0 = {}, 0
    for name, rows, slot in FLAT_ROWS:
        out[name] = flat[r0:r0 + rows]
        r0 += slot
    return (out["fox_w_in"].reshape(1, D_MODEL, 1028), out["fox_w_out"].reshape(1, 256, D_MODEL),
            out["dil_w_in"].reshape(1, D_MODEL, 2560), out["dil_w_out"].reshape(1, 256, D_MODEL),
            out["ple_w_up"].reshape(2, PLE_DIM, 256), out["ple_w_gate"].reshape(2, 256, D_MODEL))


def _unpack_gathered(wall):
    out, r0 = {}, 0
    for name, rows, slot in FLAT_ROWS:
        out[name] = wall[:, r0:r0 + rows]
        r0 += slot
    cols = lambda a, n: a.reshape(4, D_MODEL, n).transpose(1, 0, 2).reshape(D_MODEL, 4 * n)
    fwi = cols(out["fox_w_in"], 1028)
    dwi = cols(out["dil_w_in"], 2560)
    fwo = out["fox_w_out"].reshape(D_MODEL, D_MODEL)
    dwo = out["dil_w_out"].reshape(D_MODEL, D_MODEL)
    pwu4 = out["ple_w_up"].reshape(4, 2, PLE_DIM, 256)
    pwg4 = out["ple_w_gate"].reshape(4, 2, 256, D_MODEL)
    pwu = [pwu4[:, i].transpose(1, 0, 2).reshape(PLE_DIM, D_MODEL) for i in range(2)]
    pwg = [pwg4[:, i].reshape(D_MODEL, D_MODEL) for i in range(2)]
    return fwi, fwo, dwi, dwo, pwu, pwg


def _pack_grads(gfwi, gfwo, gdwi, gdwo, gpwu, gpwg):
    cols = lambda a, n: a.reshape(D_MODEL, 4, n).transpose(1, 0, 2).reshape(4, n, D_MODEL)
    up = jnp.stack([a.reshape(PLE_DIM, 4, 256).transpose(1, 0, 2) for a in gpwu], axis=1)
    gate = jnp.stack([a.reshape(4, 256, D_MODEL) for a in gpwg], axis=1)
    parts = [jnp.pad(cols(gfwi, 1028), ((0, 0), (0, 12), (0, 0))), gfwo.reshape(4, 256, D_MODEL), cols(gdwi, 2560),
             gdwo.reshape(4, 256, D_MODEL), up.reshape(4, 128, D_MODEL), gate.reshape(4, 512, D_MODEL),
             jnp.zeros((4, FLAT_TOTAL - FLAT_USED, D_MODEL), F32)]
    return jnp.concatenate(parts, axis=1)


def _local_step(x, p0, p1, tgt, fox_g, dil_g, fin_g, b_f, fwi, fwo, dwi, dwo, pwu, pwg):
    S = x.shape[0]
    tm = min(256, S)
    nh = FOX_HEADS // 2
    w_qkv0 = jnp.concatenate([fwi[:, :D_MODEL] * 0.125, fwi[:, D_MODEL:3 * D_MODEL]], axis=1)
    w_z0 = fwi[:, 3 * D_MODEL:4 * D_MODEL]
    w_f0 = jnp.repeat(fwi[:, 4 * D_MODEL:], FOX_HEAD_DIM, axis=1)
    b_full = jnp.repeat(b_f.reshape(1, FOX_HEADS), FOX_HEAD_DIM, axis=1)
    w_qkv1 = dwi[:, :3 * DIL_QKV]
    w_z1 = dwi[:, 3 * DIL_QKV:]

    n0, = _rows(_rms_fwd, [x], [(D_MODEL, BF16)], name="norm0", tm=tm, bcast=[fox_g])
    qkv0 = _mm(n0, w_qkv0, out_dtype=BF16, name="proj_qkv0")
    z0 = _mm(n0, w_z0, name="proj_z0")
    fl0 = _mm(n0, w_f0, name="proj_f0")
    c_full = _cumsum_rows([fl0], name="forget_cumsum", pre=_log_forget, bcast=[b_full])
    c_row = jnp.pad(c_full[:, ::FOX_HEAD_DIM].T.reshape(nh, 2, S), ((0, 0), (0, 6), (0, 0)))
    o0, a0 = _fox_fwd(qkv0, c_full, c_row)
    g0, = _rows(_gate_fwd, [o0, z0], [(D_MODEL, BF16)], name="gate0", tm=tm)
    h1 = _mm(g0, fwo, add=x, name="out_proj0")
    u0 = _mm(p0, pwu[0], name="ple_up0")
    t0 = _mm(h1, pwg[0], name="ple_gate0")
    h2, = _rows(_ple_fwd, [h1, u0, t0], [(D_MODEL, F32)], name="ple_mix0", tm=tm)

    n1, = _rows(_rms_fwd, [h2], [(D_MODEL, BF16)], name="norm1", tm=tm, bcast=[dil_g])
    qkv1 = _mm(n1, w_qkv1, out_dtype=BF16, name="proj_qkv1")
    z1 = _mm(n1, w_z1, name="proj_z1")
    og, lg = zip(*[_dil_fwd(qkv1, g) for g in range(DIL_GROUPS)])
    o1, g1, lse1 = _rows(_dil_combine, [*og, *lg, z1], [(D_MODEL, F32), (D_MODEL, BF16), (D_MODEL, F32)],
                         name="dil_combine", tm=tm)
    h3 = _mm(g1, dwo, add=h2, name="out_proj1")
    u1 = _mm(p1, pwu[1], name="ple_up1")
    t1 = _mm(h3, pwg[1], name="ple_gate1")
    h4, = _rows(_ple_fwd, [h3, u1, t1], [(D_MODEL, F32)], name="ple_mix1", tm=tm)

    dh4, g_fin, loss = _rows(_final_stage, [h4, tgt], [(D_MODEL, F32)], name="loss_head", tm=tm, bcast=[fin_g],
                             acc=[((1, D_MODEL), F32), ((1, LANES), F32)])

    du1, dt1 = _rows(_ple_bwd, [dh4, u1, t1], [(D_MODEL, BF16), (D_MODEL, BF16)], name="ple_bwd1", tm=tm)
    g_up1 = _mm(p1, du1, ta=True, name="grad_ple_up1")
    g_gate1 = _mm(h3, dt1, ta=True, name="grad_ple_gate1")
    dh3 = _mm(dt1, pwg[1], tb=True, add=dh4, name="ple_back1")
    dg1 = _mm(dh3, dwo, tb=True, name="out_back1")
    g_dwo = _mm(g1, dh3, ta=True, name="grad_out1")
    do1, dz1, dl1 = _rows(functools.partial(_gate_bwd, head_dim=DIL_HEAD_DIM), [dg1, o1, z1],
                          [(D_MODEL, BF16), (D_MODEL, BF16), (D_MODEL, F32)], name="gate_bwd1", tm=tm)
    dq1 = [_dil_bwd_q(qkv1, do1, lse1, dl1, g) for g in range(DIL_GROUPS)]
    dk1, dv1 = zip(*[_dil_bwd_kv(qkv1, do1, lse1, dl1, g) for g in range(DIL_GROUPS)])
    dproj1 = jnp.concatenate([*dq1, *dk1, *dv1, dz1], axis=1)
    g_dwi = _mm(n1, dproj1, ta=True, name="grad_in1")
    dn1 = _mm(dproj1, dwi, tb=True, name="in_back1")
    dh2, g_dil = _rows(_rms_bwd, [dn1, h2, dh3], [(D_MODEL, F32)], name="norm_bwd1", tm=tm, bcast=[dil_g],
                       acc=[((1, D_MODEL), F32)])

    du0, dt0 = _rows(_ple_bwd, [dh2, u0, t0], [(D_MODEL, BF16), (D_MODEL, BF16)], name="ple_bwd0", tm=tm)
    g_up0 = _mm(p0, du0, ta=True, name="grad_ple_up0")
    g_gate0 = _mm(h1, dt0, ta=True, name="grad_ple_gate0")
    dh1 = _mm(dt0, pwg[0], tb=True, add=dh2, name="ple_back0")
    dg0 = _mm(dh1, fwo, tb=True, name="out_back0")
    g_fwo = _mm(g0, dh1, ta=True, name="grad_out0")
    do0, dz0, ad0 = _rows(_gate_bwd_fox, [dg0, o0, z0, a0],
                          [(D_MODEL, BF16), (D_MODEL, BF16), (D_MODEL, F32)], name="gate_bwd0", tm=tm)
    dq0, drow, dk0, dv0, dcol = _fox_bwd(qkv0, do0, ad0, c_row)
    dkey = lax.pad(dcol[:, :2, :].reshape(FOX_HEADS, S).T, jnp.zeros((), F32),
                   [(0, 0, 0), (0, FOX_HEAD_DIM - 1, FOX_HEAD_DIM - 1)])
    dlogf = _cumsum_rows([dkey, drow], name="forget_cumsum_bwd", reverse=True, pre=_forget_dc)
    df0, g_bf = _rows(_forget_bwd, [dlogf, fl0], [(D_MODEL, BF16)], name="forget_bwd", tm=tm, bcast=[b_full],
                      acc=[((1, D_MODEL), F32)])
    dproj0 = jnp.concatenate([dq0.astype(BF16), dk0, dv0, dz0], axis=1)
    w_main0 = fwi[:, :4 * D_MODEL]
    g_fwi_main = _mm(n0, dproj0, ta=True, name="grad_in0")
    g_fwi_f = _mm(n0, df0, ta=True, name="grad_in0_forget")
    dn0 = _mm(dproj0, w_main0, tb=True, name="in_back0")
    dn0 = _mm(df0, w_f0, tb=True, add=dn0, name="in_back0_forget")
    dx, g_fox = _rows(_rms_bwd, [dn0, x, dh1], [(D_MODEL, F32)], name="norm_bwd0", tm=tm, bcast=[fox_g],
                      acc=[((1, D_MODEL), F32)])
    g_fwi = jnp.concatenate([g_fwi_main, g_fwi_f[:, ::FOX_HEAD_DIM]], axis=1)
    return (loss, dx, (g_fwi, g_fwo, g_dwi, g_dwo, [g_up0, g_up1], [g_gate0, g_gate1]),
            (g_fox, g_bf, g_fin, g_dil))


def kernel(x, p, fox_norm, fox_w_in, fox_b_f, fox_w_out, dil_norm, dil_w_in, dil_w_out, ple_w_up, ple_w_gate, final_norm, loss_target, m_fox_norm, m_fox_w_in, m_fox_b_f, m_fox_w_out, m_dil_norm, m_dil_w_in, m_dil_w_out, m_ple_w_up, m_ple_w_gate, m_final_norm, v_fox_norm, v_fox_w_in, v_fox_b_f, v_fox_w_out, v_dil_norm, v_dil_w_in, v_dil_w_out, v_ple_w_up, v_ple_w_gate, v_final_norm):
    xi, yi, ci = lax.axis_index("x"), lax.axis_index("y"), lax.axis_index("c")
    chip = 2 * xi + yi

    w_flat = _pack_shard(fox_w_in, fox_w_out, dil_w_in, dil_w_out, ple_w_up, ple_w_gate)
    m_flat = _pack_shard(m_fox_w_in, m_fox_w_out, m_dil_w_in, m_dil_w_out, m_ple_w_up, m_ple_w_gate)
    v_flat = _pack_shard(v_fox_w_in, v_fox_w_out, v_dil_w_in, v_dil_w_out, v_ple_w_up, v_ple_w_gate)
    wall = _gather_weights(w_flat.astype(BF16))
    fwi, fwo, dwi, dwo, pwu, pwg = _unpack_gathered(wall)
    gains = _gather_tiles(_pad_rows(_pad_row(dil_norm), 8), "gather_gain").reshape(4, 2, 8, D_MODEL)
    dil_g = gains[:, 0, 0, :256].reshape(1, D_MODEL)

    loss_part, dx, grads, small = _local_step(
        x[0], p[0, 0], p[1, 0], loss_target[0], fox_norm.reshape(1, D_MODEL), dil_g,
        final_norm.reshape(1, D_MODEL), fox_b_f, fwi, fwo, dwi, dwo, pwu, pwg)
    g_fox, g_bf, g_fin, g_dil = small

    g_all = _pack_grads(*grads)
    theirs = _swap_halves(g_all)
    mine = lax.dynamic_slice_in_dim(g_all, ci * FLAT_HALF, FLAT_HALF, axis=1)
    part, = _rows(lambda a, b: a + b, [mine.reshape(4 * FLAT_HALF, D_MODEL), theirs.reshape(4 * FLAT_HALF, D_MODEL)],
                  [(D_MODEL, BF16)], name="pair_sum", tm=FLAT_TILE)
    by_chip = _scatter_partials(part.reshape(4, FLAT_HALF, D_MODEL))
    half_sum, = _rows(lambda a, b, c, d: ((a.astype(F32) + b.astype(F32)) + c.astype(F32)) + d.astype(F32),
                      [by_chip[s] for s in range(4)], [(D_MODEL, F32)], name="chip_sum", tm=FLAT_TILE)
    g_flat = _join_halves(half_sum)

    tile = jnp.concatenate([g_fox, g_bf, g_fin, g_dil, jnp.pad(loss_part, ((0, 0), (0, D_MODEL - LANES))),
                            jnp.zeros((3, D_MODEL), F32)], axis=0)
    tiles = _gather_tiles(tile, "gather_small")
    tot, = _rows(lambda *t: functools.reduce(lambda a, b: a + b, t), [tiles[8 * s:8 * s + 8] for s in range(8)],
                 [(D_MODEL, F32)], name="small_sum", tm=8)
    loss = tot[4, 0]
    g_small = jnp.concatenate([
        tot[0:1], _pad_row(tot[1, ::FOX_HEAD_DIM]), tot[2:3],
        _pad_row(lax.dynamic_slice_in_dim(tot[3], chip * 256, 256)), jnp.zeros((4, D_MODEL), F32)], axis=0)
    small_tile = lambda a, b, c, d: jnp.concatenate(
        [_pad_row(a), _pad_row(b), _pad_row(c), _pad_row(d), jnp.zeros((4, D_MODEL), F32)], axis=0)
    w_small = small_tile(fox_norm, fox_b_f, final_norm, dil_norm)
    m_small = small_tile(m_fox_norm, m_fox_b_f, m_final_norm, m_dil_norm)
    v_small = small_tile(v_fox_norm, v_fox_b_f, v_final_norm, v_dil_norm)

    three = [(D_MODEL, F32)] * 3
    d_flat, nm_flat, nv_flat = _rows(_adamw, [w_flat, g_flat, m_flat, v_flat], three, name="adamw", tm=FLAT_TILE)
    d_small, nm_small, nv_small = _rows(_adamw, [w_small, g_small, m_small, v_small], three, name="adamw_small", tm=8)

    def leaves(flat, small_rows):
        fwi_, fwo_, dwi_, dwo_, pwu_, pwg_ = _unpack_shard(flat)
        return (small_rows[0:1], fwi_, small_rows[1:2, :FOX_HEADS], fwo_, small_rows[3:4, :256], dwi_, dwo_,
                pwu_, pwg_, small_rows[2])

    return (loss, dx[None], *leaves(g_flat, g_small), *leaves(d_flat, d_small), *leaves(nm_flat, nm_small),
            *leaves(nv_flat, nv_small))
```

```python
import functools

import numpy as np
import jax
import jax.numpy as jnp
from jax import lax
from jax.experimental import pallas as pl
from jax.experimental.pallas import tpu as pltpu

F32 = jnp.float32
BF16 = jnp.bfloat16

D_MODEL = 1024
PLE_DIM = 256
FOX_HEADS = 16
FOX_HEAD_DIM = 64
DIL_PATTERN = ((128, 1), (512, 4), (2048, 16))
DIL_GROUPS = 3
DIL_HEADS_PER_GROUP = 8
DIL_HEAD_DIM = 128
DIL_WINDOW_STEPS = 128
DIL_QKV = 3072
ALIBI_MAX_EXP = 8.0
RMS_EPS = 1e-6
ADAM_LR, ADAM_B1, ADAM_B2, ADAM_EPS, ADAM_WD, ADAM_STEP = 0.001, 0.9, 0.999, 1e-08, 0.01, 10

LANES = 128
VMEM_LIMIT = 56 * 1024 * 1024
MESH = pl.DeviceIdType.MESH
NEG = -1e30

FLAT_ROWS = (("fox_w_in", 1028, 1040), ("fox_w_out", 256, 256), ("dil_w_in", 2560, 2560), ("dil_w_out", 256, 256),
             ("ple_w_up", 128, 128), ("ple_w_gate", 512, 512))
FLAT_USED = sum(r for _, _, r in FLAT_ROWS)
FLAT_TOTAL = 4864
FLAT_HALF = FLAT_TOTAL // 2
FLAT_TILE = 608


def _cparams(*sem):
    return pltpu.CompilerParams(dimension_semantics=sem, vmem_limit_bytes=VMEM_LIMIT)


def _sigmoid(x):
    return 1.0 / (1.0 + jnp.exp(-x))


def _mm(a, b, *, name, ta=False, tb=False, out_dtype=F32, add=None):
    if ta:
        K, M = a.shape
    else:
        M, K = a.shape
    if tb:
        N, Kb = b.shape
    else:
        Kb, N = b.shape
    assert K == Kb, (a.shape, b.shape)
    tm, tn, tk = min(M, 1024), min(N, 1024), min(K, 1024)
    assert M % tm == 0 and N % tn == 0 and K % tk == 0, (M, N, K)
    nk = K // tk
    dn = (((0 if ta else 1,), (1 if tb else 0,)), ((), ()))

    def body(*refs):
        if add is None:
            a_ref, b_ref, o_ref, acc = refs
        else:
            a_ref, b_ref, add_ref, o_ref, acc = refs
        k = pl.program_id(2)

        @pl.when(k == 0)
        def _():
            acc[...] = jnp.zeros_like(acc)

        acc[...] += lax.dot_general(a_ref[...].astype(BF16), b_ref[...].astype(BF16), dn,
                                    preferred_element_type=F32)

        @pl.when(k == nk - 1)
        def _():
            r = acc[...]
            if add is not None:
                r = r + add_ref[...]
            o_ref[...] = r.astype(out_dtype)

    a_spec = (pl.BlockSpec((tk, tm), lambda i, j, k: (k, i)) if ta
              else pl.BlockSpec((tm, tk), lambda i, j, k: (i, k)))
    b_spec = (pl.BlockSpec((tn, tk), lambda i, j, k: (j, k)) if tb
              else pl.BlockSpec((tk, tn), lambda i, j, k: (k, j)))
    in_specs = [a_spec, b_spec]
    args = [a, b]
    if add is not None:
        in_specs.append(pl.BlockSpec((tm, tn), lambda i, j, k: (i, j)))
        args.append(add)
    return pl.pallas_call(
        body, name=name, grid=(M // tm, N // tn, nk),
        in_specs=in_specs, out_specs=pl.BlockSpec((tm, tn), lambda i, j, k: (i, j)),
        out_shape=jax.ShapeDtypeStruct((M, N), out_dtype),
        scratch_shapes=[pltpu.VMEM((tm, tn), F32)],
        compiler_params=_cparams("parallel", "parallel", "arbitrary"),
    )(*args)


def _rows(fn, ins, outs, *, name, tm, bcast=(), acc=()):
    R = ins[0].shape[0]
    assert R % tm == 0, (R, tm)
    n_in, n_b, n_out, n_acc = len(ins), len(bcast), len(outs), len(acc)

    def body(*refs):
        in_refs = refs[:n_in + n_b]
        out_refs = refs[n_in + n_b:n_in + n_b + n_out]
        acc_refs = refs[n_in + n_b + n_out:]
        res = fn(*[r[...] for r in in_refs])
        if not isinstance(res, (tuple, list)):
            res = (res,)
        for r, v in zip(out_refs, res[:n_out]):
            r[...] = v.astype(r.dtype)
        first = pl.program_id(0) == 0
        for r, v in zip(acc_refs, res[n_out:]):
            @pl.when(first)
            def _(r=r, v=v):
                r[...] = v.astype(r.dtype)

            @pl.when(jnp.logical_not(first))
            def _(r=r, v=v):
                r[...] += v.astype(r.dtype)

    in_specs = [pl.BlockSpec((tm, a.shape[1]), lambda i: (i, 0)) for a in ins]
    in_specs += [pl.BlockSpec(b.shape, lambda i, nd=b.ndim: (0,) * nd) for b in bcast]
    out_specs = [pl.BlockSpec((tm, c), lambda i: (i, 0)) for c, _ in outs]
    out_specs += [pl.BlockSpec(s, lambda i, nd=len(s): (0,) * nd) for s, _ in acc]
    out_shape = [jax.ShapeDtypeStruct((R, c), dt) for c, dt in outs]
    out_shape += [jax.ShapeDtypeStruct(s, dt) for s, dt in acc]
    res = pl.pallas_call(
        body, name=name, grid=(R // tm,), in_specs=in_specs, out_specs=out_specs, out_shape=out_shape,
        compiler_params=_cparams("arbitrary" if acc else "parallel"),
    )(*ins, *bcast)
    return res


def _cumsum_rows(ins, *, name, reverse=False, pre=None, bcast=()):
    S, C = ins[0].shape
    tb = 256
    nb = S // tb
    assert S % tb == 0
    n_in = len(ins) + len(bcast)

    def body(*refs):
        in_refs, o_ref, carry = refs[:n_in], refs[n_in], refs[n_in + 1]

        @pl.when(pl.program_id(0) == 0)
        def _():
            carry[...] = jnp.zeros_like(carry)

        xv = in_refs[0][...] if pre is None else pre(*[r[...] for r in in_refs])
        r_ = lax.broadcasted_iota(jnp.int32, (tb, tb), 0)
        c_ = lax.broadcasted_iota(jnp.int32, (tb, tb), 1)
        tri = jnp.where((c_ >= r_) if reverse else (c_ <= r_), 1.0, 0.0).astype(BF16)
        hi = xv.astype(BF16)
        r1 = xv - hi.astype(F32)
        mid = r1.astype(BF16)
        lo = (r1 - mid.astype(F32)).astype(BF16)
        cs = (jnp.dot(tri, hi, preferred_element_type=F32) + jnp.dot(tri, mid, preferred_element_type=F32)
              + jnp.dot(tri, lo, preferred_element_type=F32)) + carry[...]
        o_ref[...] = cs
        carry[...] = cs[0:1, :] if reverse else cs[tb - 1:tb, :]

    blk = (lambda i: (nb - 1 - i, 0)) if reverse else (lambda i: (i, 0))
    in_specs = [pl.BlockSpec((tb, C), blk) for _ in ins]
    in_specs += [pl.BlockSpec(b.shape, lambda i, nd=b.ndim: (0,) * nd) for b in bcast]
    return pl.pallas_call(
        body, name=name, grid=(nb,), in_specs=in_specs, out_specs=pl.BlockSpec((tb, C), blk),
        out_shape=jax.ShapeDtypeStruct((S, C), F32), scratch_shapes=[pltpu.VMEM((1, C), F32)],
        compiler_params=_cparams("arbitrary"),
    )(*ins, *bcast)


def _rms_fwd(x, g):
    r = lax.rsqrt(jnp.mean(x * x, axis=1, keepdims=True) + RMS_EPS)
    return (x * r) * g


def _rms_bwd(dn, x, dres, g):
    r = lax.rsqrt(jnp.mean(x * x, axis=1, keepdims=True) + RMS_EPS)
    xh = x * r
    w = dn * g
    dx = r * (w - xh * jnp.mean(w * xh, axis=1, keepdims=True))
    return dres + dx, jnp.sum(dn * xh, axis=0, keepdims=True)


def _final_stage(h, tgt, g):
    r = lax.rsqrt(jnp.mean(h * h, axis=1, keepdims=True) + RMS_EPS)
    xh = h * r
    diff = xh * g - tgt
    loss = 0.5 * jnp.sum(jnp.mean(diff * diff, axis=1, keepdims=True), axis=0, keepdims=True)
    dy = diff * (1.0 / D_MODEL)
    w = dy * g
    dh = r * (w - xh * jnp.mean(w * xh, axis=1, keepdims=True))
    return dh, jnp.sum(dy * xh, axis=0, keepdims=True), jnp.broadcast_to(loss, (1, LANES))


def _ple_fwd(h, u, t):
    return h + u * _sigmoid(t)


def _ple_bwd(dh, u, t):
    s = _sigmoid(t)
    return dh * s, dh * u * s * (1.0 - s)


def _gate_fwd(o, z):
    return o * (z * _sigmoid(z))


def _head_sums(prod, head_dim):
    tm, width = prod.shape
    lane = lax.broadcasted_iota(jnp.int32, (tm, LANES), 1)
    cols = []
    for b in range(width // LANES):
        blk = prod[:, b * LANES:(b + 1) * LANES]
        if head_dim == LANES:
            cols.append(jnp.broadcast_to(jnp.sum(blk, axis=1, keepdims=True), (tm, LANES)))
        else:
            low = lane < head_dim
            sa = jnp.sum(jnp.where(low, blk, 0.0), axis=1, keepdims=True)
            sb = jnp.sum(jnp.where(low, 0.0, blk), axis=1, keepdims=True)
            cols.append(jnp.where(low, sa, sb))
    return jnp.concatenate(cols, axis=1)


def _gate_bwd(dg, o, z, *, head_dim):
    s = _sigmoid(z)
    do = dg * (z * s)
    dz = dg * o * (s * (1.0 + z * (1.0 - s)))
    return do, dz, _head_sums(do * o, head_dim)


def _gate_bwd_fox(dg, o, z, a):
    do, dz, dl = _gate_bwd(dg, o, z, head_dim=FOX_HEAD_DIM)
    lane = lax.broadcasted_iota(jnp.int32, do.shape, 1)
    return do, dz, jnp.where(lane % FOX_HEAD_DIM < FOX_HEAD_DIM // 2, a, dl)


def _forget_dc(dkey, drow):
    lane = lax.broadcasted_iota(jnp.int32, drow.shape, 1)
    return jnp.where(lane % FOX_HEAD_DIM == 0, drow, 0.0) - dkey


def _dil_combine(o1, o2, o3, l1, l2, l3, z):
    m = jnp.maximum(jnp.maximum(l1, l2), l3)
    e1, e2, e3 = jnp.exp(l1 - m), jnp.exp(l2 - m), jnp.exp(l3 - m)
    tot = e1 + e2 + e3
    o = (e1 * o1 + e2 * o2 + e3 * o3) / tot
    return o, o * (z * _sigmoid(z)), m + jnp.log(tot)


def _log_forget(fl, b):
    u = fl + b
    return jnp.minimum(u, 0.0) - jnp.log(1.0 + jnp.exp(-jnp.abs(u)))


def _forget_bwd(dlogf, fl, b):
    du = dlogf * (1.0 / (1.0 + jnp.exp(fl + b)))
    return du, jnp.sum(du, axis=0, keepdims=True)


def _adamw(w, g, m, v):
    m = ADAM_B1 * m + (1.0 - ADAM_B1) * g
    v = ADAM_B2 * v + (1.0 - ADAM_B2) * (g * g)
    m_hat = m / (1.0 - ADAM_B1 ** ADAM_STEP)
    v_hat = v / (1.0 - ADAM_B2 ** ADAM_STEP)
    delta = -ADAM_LR * (m_hat / (jnp.sqrt(v_hat) + ADAM_EPS) + ADAM_WD * w)
    return delta, m, v


FOX_TQ = 512


def _fox_fwd(qkv, c_full, c_row):
    S = qkv.shape[0]
    tq = min(FOX_TQ, S)
    nq = S // tq

    def body(q_ref, k_ref, v_ref, c_ref, crow_ref, o_ref, a_ref):
        i = pl.program_id(1)
        lane = lax.broadcasted_iota(jnp.int32, (tq, LANES), 1)
        low = lane < FOX_HEAD_DIM
        row = lax.broadcasted_iota(jnp.int32, (tq, tq), 0)
        col = lax.broadcasted_iota(jnp.int32, (tq, tq), 1)
        q2 = q_ref[...]
        c_blk = c_ref[...]
        o_heads, a_heads = [], []
        for hh in range(2):
            qm = jnp.where(low if hh == 0 else jnp.logical_not(low), q2, jnp.zeros_like(q2))
            ct = c_blk[:, hh * FOX_HEAD_DIM:hh * FOX_HEAD_DIM + 1]

            def step(kb, carry, diag, qm=qm, ct=ct, hh=hh):
                m, l, acc = carry
                start = pl.multiple_of(kb * tq, tq)
                kblk = k_ref[pl.ds(start, tq), :]
                vblk = v_ref[pl.ds(start, tq), :]
                cs = crow_ref[0, hh:hh + 1, pl.ds(start, tq)]
                s = lax.dot_general(qm, kblk, (((1,), (1,)), ((), ())), preferred_element_type=F32)
                s = s + (ct - cs)
                if diag:
                    s = jnp.where(col <= row, s, NEG)
                m_new = jnp.maximum(m, jnp.max(s, axis=1, keepdims=True))
                alpha = jnp.exp(m - m_new)
                pr = jnp.exp(s - m_new)
                l = alpha * l + jnp.sum(pr, axis=1, keepdims=True)
                acc = alpha * acc + jnp.dot(pr.astype(BF16), vblk, preferred_element_type=F32)
                return m_new, l, acc

            init = (jnp.full((tq, 1), NEG, F32), jnp.zeros((tq, 1), F32), jnp.zeros((tq, LANES), F32))
            carry = lax.fori_loop(0, i, functools.partial(step, diag=False), init)
            m, l, acc = step(i, carry, True)
            o_heads.append(acc / l)
            a_heads.append(jnp.broadcast_to(ct - (m + jnp.log(l)), (tq, LANES)))
        o_ref[...] = jnp.where(low, o_heads[0], o_heads[1])
        a_ref[...] = jnp.where(low, a_heads[0], a_heads[1])

    return pl.pallas_call(
        body, name="fox_attn_fwd", grid=(FOX_HEADS // 2, nq),
        in_specs=[pl.BlockSpec((tq, LANES), lambda h, i: (i, h)),
                  pl.BlockSpec((S, LANES), lambda h, i: (0, 8 + h)),
                  pl.BlockSpec((S, LANES), lambda h, i: (0, 16 + h)),
                  pl.BlockSpec((tq, LANES), lambda h, i: (i, h)),
                  pl.BlockSpec((1, 8, S), lambda h, i: (h, 0, 0))],
        out_specs=[pl.BlockSpec((tq, LANES), lambda h, i: (i, h)),
                   pl.BlockSpec((tq, LANES), lambda h, i: (i, h))],
        out_shape=[jax.ShapeDtypeStruct((S, D_MODEL), F32), jax.ShapeDtypeStruct((S, D_MODEL), F32)],
        compiler_params=_cparams("parallel", "arbitrary"),
    )(qkv, qkv, qkv, c_full, c_row)


def _fox_bwd(qkv, do, ad_full, c_row):
    S = qkv.shape[0]
    tq = min(FOX_TQ, S)
    nq = S // tq

    def body(q_ref, do_ref, ad_ref, k_ref, v_ref, crow_ref, dq_ref, dr_ref, dk_ref, dv_ref, dc_ref):
        kb = pl.program_id(1)
        lane = lax.broadcasted_iota(jnp.int32, (tq, LANES), 1)
        low = lane < FOX_HEAD_DIM
        row = lax.broadcasted_iota(jnp.int32, (tq, tq), 0)
        col = lax.broadcasted_iota(jnp.int32, (tq, tq), 1)

        @pl.when(kb == 0)
        def _():
            dq_ref[...] = jnp.zeros_like(dq_ref)
            dr_ref[...] = jnp.zeros_like(dr_ref)

        k2 = k_ref[...]
        v2 = v_ref[...]
        kstart = pl.multiple_of(kb * tq, tq)
        dk_heads, dv_heads, dc_rows = [], [], []
        for hh in range(2):
            sel = low if hh == 0 else jnp.logical_not(low)
            km = jnp.where(sel, k2, jnp.zeros_like(k2))
            vm = jnp.where(sel, v2, jnp.zeros_like(v2))
            cs = crow_ref[0, hh:hh + 1, pl.ds(kstart, tq)]

            def step(qb, carry, diag, km=km, vm=vm, cs=cs, hh=hh, sel=sel):
                dk, dv, dc = carry
                start = pl.multiple_of(qb * tq, tq)
                qblk = q_ref[pl.ds(start, tq), :]
                doblk = do_ref[pl.ds(start, tq), :]
                at = ad_ref[pl.ds(start, tq), hh * FOX_HEAD_DIM:hh * FOX_HEAD_DIM + 1]
                dt = ad_ref[pl.ds(start, tq), hh * FOX_HEAD_DIM + 32:hh * FOX_HEAD_DIM + 33]
                s = lax.dot_general(qblk, km, (((1,), (1,)), ((), ())), preferred_element_type=F32)
                pr = jnp.exp(s + (at - cs))
                if diag:
                    pr = jnp.where(col <= row, pr, 0.0)
                dp = lax.dot_general(doblk, vm, (((1,), (1,)), ((), ())), preferred_element_type=F32)
                ds = pr * (dp - dt)
                prb = pr.astype(BF16)
                dsb = ds.astype(BF16)
                dv = dv + lax.dot_general(prb, doblk, (((0,), (0,)), ((), ())), preferred_element_type=F32)
                dk = dk + lax.dot_general(dsb, qblk, (((0,), (0,)), ((), ())), preferred_element_type=F32)
                dq_ref[pl.ds(start, tq), :] += jnp.dot(dsb, km, preferred_element_type=F32)
                dr_ref[pl.ds(start, tq), :] += jnp.where(sel, jnp.sum(ds, axis=1, keepdims=True), 0.0)
                dc = dc + jnp.sum(ds, axis=0, keepdims=True)
                return dk, dv, dc

            init = (jnp.zeros((tq, LANES), F32), jnp.zeros((tq, LANES), F32), jnp.zeros((1, tq), F32))
            carry = step(kb, init, True)
            dk, dv, dc = lax.fori_loop(kb + 1, nq, functools.partial(step, diag=False), carry)
            dk_heads.append(dk)
            dv_heads.append(dv)
            dc_rows.append(dc)
        dk_ref[...] = jnp.where(low, dk_heads[0], dk_heads[1]).astype(BF16)
        dv_ref[...] = jnp.where(low, dv_heads[0], dv_heads[1]).astype(BF16)
        dc_ref[0] = jnp.concatenate(dc_rows + [jnp.zeros((6, tq), F32)], axis=0)

        @pl.when(kb == nq - 1)
        def _():
            dq_ref[...] = dq_ref[...] * (FOX_HEAD_DIM ** -0.5)

    res = lambda cb: pl.BlockSpec((S, LANES), lambda h, k, cb=cb: (0, cb + h))
    blk = lambda cb: pl.BlockSpec((tq, LANES), lambda h, k, cb=cb: (k, cb + h))
    return pl.pallas_call(
        body, name="fox_attn_bwd", grid=(FOX_HEADS // 2, nq),
        in_specs=[res(0), res(0), res(0), blk(8), blk(16),
                  pl.BlockSpec((1, 8, S), lambda h, k: (h, 0, 0))],
        out_specs=[res(0), res(0), blk(0), blk(0), pl.BlockSpec((1, 8, tq), lambda h, k: (h, 0, k))],
        out_shape=[jax.ShapeDtypeStruct((S, D_MODEL), F32), jax.ShapeDtypeStruct((S, D_MODEL), F32),
                   jax.ShapeDtypeStruct((S, D_MODEL), BF16),
                   jax.ShapeDtypeStruct((S, D_MODEL), BF16), jax.ShapeDtypeStruct((FOX_HEADS // 2, 8, S), F32)],
        compiler_params=_cparams("parallel", "arbitrary"),
    )(qkv, do, ad_full, qkv, qkv, c_row)


def _split3(x):
    p1 = x.astype(BF16).astype(F32)
    r = x - p1
    p2 = r.astype(BF16).astype(F32)
    return p1, p2, r - p2


def _lane_in_head(shape):
    return lax.broadcasted_iota(jnp.int32, shape, 1) % FOX_HEAD_DIM


def _query_extras(x):
    lm = _lane_in_head(x.shape)
    p1, p2, p3 = _split3(x)
    return jnp.where(lm == 0, p1, jnp.where(lm == 1, p2, jnp.where(lm == 2, p3, jnp.where(lm < 6, 1.0, 0.0))))


def _key_extras(c):
    lm = _lane_in_head(c.shape)
    p1, p2, p3 = _split3(c)
    return jnp.where(lm < 3, 1.0, jnp.where(lm == 3, -p1, jnp.where(lm == 4, -p2, jnp.where(lm == 5, -p3, 0.0))))


def _fox_extras(c):
    return _query_extras(c), _key_extras(c)


def _swapped_head_sums(prod):
    tm, width = prod.shape
    lane = lax.broadcasted_iota(jnp.int32, (tm, LANES), 1)
    low = lane < FOX_HEAD_DIM
    cols = []
    for b in range(width // LANES):
        blk = prod[:, b * LANES:(b + 1) * LANES]
        sa = jnp.sum(jnp.where(low, blk, 0.0), axis=1, keepdims=True)
        sb = jnp.sum(jnp.where(low, 0.0, blk), axis=1, keepdims=True)
        cols.append(jnp.where(low, sb, sa))
    return jnp.concatenate(cols, axis=1)


def _gate_bwd_fox2(dg, o, z, a):
    s = _sigmoid(z)
    do = dg * (z * s)
    dz = dg * o * (s * (1.0 + z * (1.0 - s)))
    lm = _lane_in_head(do.shape)
    d1, d2, d3 = _split3(-_swapped_head_sums(do * o))
    dx = jnp.where(lm == 0, d1, jnp.where(lm == 1, d2, jnp.where(lm == 2, d3, 0.0)))
    return do, dz, _query_extras(a), dx


def _forget_dc2(drx, dkx):
    lm = _lane_in_head(drx.shape)
    return jnp.where(lm == 0, drx, 0.0) - jnp.where(lm == 3, dkx, 0.0)


def _forget_bwd2(dl, fl, b):
    lm = _lane_in_head(dl.shape)
    width = dl.shape[1]
    both = dl + pltpu.roll(dl, 3, 1) + pltpu.roll(dl, width - 3, 1)
    du = jnp.where(lm == 0, both, 0.0) * (1.0 / (1.0 + jnp.exp(fl + b)))
    return du, jnp.sum(du, axis=0, keepdims=True)


def _fox_fwd2(qkv, qx, kx):
    S = qkv.shape[0]
    tq = min(FOX_TQ, S)
    nq = S // tq
    nt = (((1,), (1,)), ((), ()))

    def body(q_ref, qx_ref, k_ref, v_ref, kx_ref, o_ref, a_ref):
        i = pl.program_id(1)
        low = lax.broadcasted_iota(jnp.int32, (tq, LANES), 1) < FOX_HEAD_DIM
        row = lax.broadcasted_iota(jnp.int32, (tq, tq), 0)
        col = lax.broadcasted_iota(jnp.int32, (tq, tq), 1)
        q2, x2 = q_ref[...], qx_ref[...]
        qa = (jnp.where(low, q2, x2), jnp.where(low, x2, q2))

        def step(kb, carry, diag):
            start = pl.multiple_of(kb * tq, tq)
            k2 = k_ref[pl.ds(start, tq), :]
            v2 = v_ref[pl.ds(start, tq), :]
            y2 = kx_ref[pl.ds(start, tq), :]
            one = jnp.ones_like(v2)
            ka = (jnp.where(low, k2, y2), jnp.where(low, y2, k2))
            va = (jnp.where(low, v2, one), jnp.where(low, one, v2))
            out = []
            for hh in range(2):
                m, acc = carry[hh]
                s = lax.dot_general(qa[hh], ka[hh], nt, preferred_element_type=F32)
                if diag:
                    s = jnp.where(col <= row, s, NEG)
                m_new = jnp.maximum(m, jnp.max(s, axis=1, keepdims=True))
                pr = jnp.exp(s - m_new)
                acc = jnp.exp(m - m_new) * acc + jnp.dot(pr.astype(BF16), va[hh], preferred_element_type=F32)
                out.append((m_new, acc))
            return tuple(out)

        init = ((jnp.full((tq, 1), NEG, F32), jnp.zeros((tq, LANES), F32)),) * 2
        carry = lax.fori_loop(0, i, functools.partial(step, diag=False), init)
        (m_a, acc_a), (m_b, acc_b) = step(i, carry, True)
        l_a, l_b = acc_a[:, FOX_HEAD_DIM:FOX_HEAD_DIM + 1], acc_b[:, 0:1]
        xf = x2.astype(F32)
        c_a = xf[:, 64:65] + xf[:, 65:66] + xf[:, 66:67]
        c_b = xf[:, 0:1] + xf[:, 1:2] + xf[:, 2:3]
        o_ref[...] = jnp.where(low, acc_a / l_a, acc_b / l_b)
        a_ref[...] = jnp.where(low, jnp.broadcast_to(c_b - (m_b + jnp.log(l_b)), (tq, LANES)),
                               jnp.broadcast_to(c_a - (m_a + jnp.log(l_a)), (tq, LANES)))

    blk = lambda cb: pl.BlockSpec((tq, LANES), lambda h, i, cb=cb: (i, cb + h))
    res = lambda cb: pl.BlockSpec((S, LANES), lambda h, i, cb=cb: (0, cb + h))
    return pl.pallas_call(
        body, name="fox_attn_fwd", grid=(FOX_HEADS // 2, nq),
        in_specs=[blk(0), blk(0), res(8), res(16), res(0)],
        out_specs=[blk(0), blk(0)],
        out_shape=[jax.ShapeDtypeStruct((S, D_MODEL), F32), jax.ShapeDtypeStruct((S, D_MODEL), F32)],
        compiler_params=_cparams("parallel", "arbitrary"),
    )(qkv, qx, qkv, qkv, kx)


def _fox_bwd2(qkv, do, qxa, dx, kx):
    S = qkv.shape[0]
    tq = min(FOX_TQ, S)
    nq = S // tq
    nt = (((1,), (1,)), ((), ()))
    tn = (((0,), (0,)), ((), ()))

    def body(q_ref, qx_ref, do_ref, dx_ref, k_ref, v_ref, kx_ref, dq_ref, dr_ref, dk_ref, dv_ref, dkx_ref):
        kb = pl.program_id(1)
        low = lax.broadcasted_iota(jnp.int32, (tq, LANES), 1) < FOX_HEAD_DIM
        row = lax.broadcasted_iota(jnp.int32, (tq, tq), 0)
        col = lax.broadcasted_iota(jnp.int32, (tq, tq), 1)

        @pl.when(kb == 0)
        def _():
            dq_ref[...] = jnp.zeros_like(dq_ref)
            dr_ref[...] = jnp.zeros_like(dr_ref)

        k2, v2, y2 = k_ref[...], v_ref[...], kx_ref[...]
        one = jnp.ones_like(v2)
        ka = (jnp.where(low, k2, y2), jnp.where(low, y2, k2))
        va = (jnp.where(low, v2, one), jnp.where(low, one, v2))

        def step(qb, carry, diag):
            start = pl.multiple_of(qb * tq, tq)
            q2 = q_ref[pl.ds(start, tq), :]
            x2 = qx_ref[pl.ds(start, tq), :]
            d2 = do_ref[pl.ds(start, tq), :]
            e2 = dx_ref[pl.ds(start, tq), :]
            qa = (jnp.where(low, q2, x2), jnp.where(low, x2, q2))
            da = (jnp.where(low, d2, e2), jnp.where(low, e2, d2))
            new, res = [], []
            for hh in range(2):
                dk, dv = carry[hh]
                s = lax.dot_general(qa[hh], ka[hh], nt, preferred_element_type=F32)
                if diag:
                    s = jnp.where(col <= row, s, NEG)
                pr = jnp.exp(s)
                ds = pr * lax.dot_general(da[hh], va[hh], nt, preferred_element_type=F32)
                prb, dsb = pr.astype(BF16), ds.astype(BF16)
                dv = dv + lax.dot_general(prb, da[hh], tn, preferred_element_type=F32)
                dk = dk + lax.dot_general(dsb, qa[hh], tn, preferred_element_type=F32)
                res.append(jnp.dot(dsb, ka[hh], preferred_element_type=F32))
                new.append((dk, dv))
            dq_ref[pl.ds(start, tq), :] += jnp.where(low, res[0], res[1])
            dr_ref[pl.ds(start, tq), :] += jnp.where(low, res[1], res[0])
            return tuple(new)

        init = ((jnp.zeros((tq, LANES), F32), jnp.zeros((tq, LANES), F32)),) * 2
        carry = step(kb, init, True)
        (dk_a, dv_a), (dk_b, dv_b) = lax.fori_loop(kb + 1, nq, functools.partial(step, diag=False), carry)
        dk_ref[...] = jnp.where(low, dk_a, dk_b).astype(BF16)
        dv_ref[...] = jnp.where(low, dv_a, dv_b).astype(BF16)
        dkx_ref[...] = jnp.where(low, dk_b, dk_a)

        @pl.when(kb == nq - 1)
        def _():
            dq_ref[...] = dq_ref[...] * (FOX_HEAD_DIM ** -0.5)

    res = lambda cb: pl.BlockSpec((S, LANES), lambda h, k, cb=cb: (0, cb + h))
    blk = lambda cb: pl.BlockSpec((tq, LANES), lambda h, k, cb=cb: (k, cb + h))
    f32, b16 = jax.ShapeDtypeStruct((S, D_MODEL), F32), jax.ShapeDtypeStruct((S, D_MODEL), BF16)
    return pl.pallas_call(
        body, name="fox_attn_bwd", grid=(FOX_HEADS // 2, nq),
        in_specs=[res(0), res(0), res(0), res(0), blk(8), blk(16), blk(0)],
        out_specs=[res(0), res(0), blk(0), blk(0), blk(0)],
        out_shape=[f32, f32, b16, b16, f32],
        compiler_params=_cparams("parallel", "arbitrary"),
    )(qkv, qxa, do, dx, qkv, qkv, kx)


def _alibi_slopes():
    n = DIL_GROUPS * DIL_HEADS_PER_GROUP
    s = np.float32(2.0) ** (np.float32(-ALIBI_MAX_EXP) * np.arange(1, n + 1, dtype=np.float32) / np.float32(n))
    return s.astype(np.float32).reshape(DIL_GROUPS, DIL_HEADS_PER_GROUP)


W = DIL_WINDOW_STEPS
DIL_SCALE = DIL_HEAD_DIM ** -0.5


def _dil_bias(g, h, dist, dilation):
    return (-float(_alibi_slopes()[g, h])) * (dist * dilation).astype(F32)


def _dil_fwd(proj, g):
    S = proj.shape[0]
    d = DIL_PATTERN[g][1]
    L = S // d
    nb = L // W
    pv = proj.reshape(L, d * 3 * DIL_QKV)
    nblk = 3 * DIL_QKV // D_MODEL

    def body(q_ref, kp_ref, kc_ref, vp_ref, vc_ref, o_ref, l_ref):
        n = pl.program_id(1)
        i_ = lax.broadcasted_iota(jnp.int32, (W, 2 * W), 0)
        j_ = lax.broadcasted_iota(jnp.int32, (W, 2 * W), 1)
        dist = W + i_ - j_
        valid = (dist >= 0) & (dist <= W) & ((j_ >= W) | (n > 0))
        for h in range(DIL_HEADS_PER_GROUP):
            sl = slice(h * LANES, (h + 1) * LANES)
            k2 = jnp.concatenate([kp_ref[:, sl], kc_ref[:, sl]], axis=0)
            v2 = jnp.concatenate([vp_ref[:, sl], vc_ref[:, sl]], axis=0)
            s = lax.dot_general(q_ref[:, sl], k2, (((1,), (1,)), ((), ())), preferred_element_type=F32)
            s = s * DIL_SCALE + _dil_bias(g, h, dist, d)
            s = jnp.where(valid, s, NEG)
            m = jnp.max(s, axis=1, keepdims=True)
            pr = jnp.exp(s - m)
            l = jnp.sum(pr, axis=1, keepdims=True)
            o_ref[:, sl] = jnp.dot(pr.astype(BF16), v2, preferred_element_type=F32) / l
            l_ref[:, sl] = jnp.broadcast_to(m + jnp.log(l), (W, LANES))

    spec = lambda off, prev: pl.BlockSpec(
        (W, D_MODEL), lambda r, n: ((jnp.maximum(n - 1, 0) if prev else n), r * nblk + off + g))
    out_spec = pl.BlockSpec((W, D_MODEL), lambda r, n: (n, r))
    o, lse = pl.pallas_call(
        body, name=f"dil_attn_fwd_g{g}", grid=(d, nb),
        in_specs=[spec(0, False), spec(3, True), spec(3, False), spec(6, True), spec(6, False)],
        out_specs=[out_spec, out_spec],
        out_shape=[jax.ShapeDtypeStruct((L, d * D_MODEL), F32), jax.ShapeDtypeStruct((L, d * D_MODEL), F32)],
        compiler_params=_cparams("parallel", "parallel"),
    )(pv, pv, pv, pv, pv)
    return o.reshape(S, D_MODEL), lse.reshape(S, D_MODEL)


def _dil_bwd_q(proj, do, lse, dlt, g):
    S = proj.shape[0]
    d = DIL_PATTERN[g][1]
    L = S // d
    nb = L // W
    pv = proj.reshape(L, d * 3 * DIL_QKV)
    nblk = 3 * DIL_QKV // D_MODEL

    def body(q_ref, kp_ref, kc_ref, vp_ref, vc_ref, do_ref, l_ref, d_ref, dq_ref):
        n = pl.program_id(1)
        i_ = lax.broadcasted_iota(jnp.int32, (W, 2 * W), 0)
        j_ = lax.broadcasted_iota(jnp.int32, (W, 2 * W), 1)
        dist = W + i_ - j_
        valid = (dist >= 0) & (dist <= W) & ((j_ >= W) | (n > 0))
        for h in range(DIL_HEADS_PER_GROUP):
            sl = slice(h * LANES, (h + 1) * LANES)
            k2 = jnp.concatenate([kp_ref[:, sl], kc_ref[:, sl]], axis=0)
            v2 = jnp.concatenate([vp_ref[:, sl], vc_ref[:, sl]], axis=0)
            s = lax.dot_general(q_ref[:, sl], k2, (((1,), (1,)), ((), ())), preferred_element_type=F32)
            s = s * DIL_SCALE + _dil_bias(g, h, dist, d)
            pr = jnp.where(valid, jnp.exp(s - l_ref[:, h * LANES:h * LANES + 1]), 0.0)
            dp = lax.dot_general(do_ref[:, sl], v2, (((1,), (1,)), ((), ())), preferred_element_type=F32)
            ds = pr * (dp - d_ref[:, h * LANES:h * LANES + 1])
            dq_ref[:, sl] = (jnp.dot(ds.astype(BF16), k2, preferred_element_type=F32) * DIL_SCALE).astype(BF16)

    spec = lambda off, prev: pl.BlockSpec(
        (W, D_MODEL), lambda r, n: ((jnp.maximum(n - 1, 0) if prev else n), r * nblk + off + g))
    row_spec = pl.BlockSpec((W, D_MODEL), lambda r, n: (n, r))
    dv_ = lambda a: a.reshape(L, d * D_MODEL)
    dq = pl.pallas_call(
        body, name=f"dil_attn_bwd_q_g{g}", grid=(d, nb),
        in_specs=[spec(0, False), spec(3, True), spec(3, False), spec(6, True), spec(6, False),
                  row_spec, row_spec, row_spec],
        out_specs=row_spec, out_shape=jax.ShapeDtypeStruct((L, d * D_MODEL), BF16),
        compiler_params=_cparams("parallel", "parallel"),
    )(pv, pv, pv, pv, pv, dv_(do), dv_(lse), dv_(dlt))
    return dq.reshape(S, D_MODEL)


def _dil_bwd_kv(proj, do, lse, dlt, g):
    S = proj.shape[0]
    d = DIL_PATTERN[g][1]
    L = S // d
    nb = L // W
    pv = proj.reshape(L, d * 3 * DIL_QKV)
    nblk = 3 * DIL_QKV // D_MODEL

    def body(k_ref, v_ref, qc_ref, qn_ref, doc_ref, don_ref, lc_ref, ln_ref, dc_ref, dn_ref, dk_ref, dv_ref):
        m_ = pl.program_id(1)
        i_ = lax.broadcasted_iota(jnp.int32, (2 * W, W), 0)
        j_ = lax.broadcasted_iota(jnp.int32, (2 * W, W), 1)
        dist = i_ - j_
        valid = (dist >= 0) & (dist <= W) & ((i_ < W) | (m_ < nb - 1))
        for h in range(DIL_HEADS_PER_GROUP):
            sl = slice(h * LANES, (h + 1) * LANES)
            one = slice(h * LANES, h * LANES + 1)
            q2 = jnp.concatenate([qc_ref[:, sl], qn_ref[:, sl]], axis=0)
            do2 = jnp.concatenate([doc_ref[:, sl], don_ref[:, sl]], axis=0)
            l2 = jnp.concatenate([lc_ref[:, one], ln_ref[:, one]], axis=0)
            d2 = jnp.concatenate([dc_ref[:, one], dn_ref[:, one]], axis=0)
            s = lax.dot_general(q2, k_ref[:, sl], (((1,), (1,)), ((), ())), preferred_element_type=F32)
            s = s * DIL_SCALE + _dil_bias(g, h, dist, d)
            pr = jnp.where(valid, jnp.exp(s - l2), 0.0)
            dp = lax.dot_general(do2, v_ref[:, sl], (((1,), (1,)), ((), ())), preferred_element_type=F32)
            ds = pr * (dp - d2)
            dv_ref[:, sl] = lax.dot_general(pr.astype(BF16), do2, (((0,), (0,)), ((), ())),
                                            preferred_element_type=F32).astype(BF16)
            dk_ref[:, sl] = (lax.dot_general(ds.astype(BF16), q2, (((0,), (0,)), ((), ())),
                                             preferred_element_type=F32) * DIL_SCALE).astype(BF16)

    spec = lambda off, nxt: pl.BlockSpec(
        (W, D_MODEL), lambda r, n: ((jnp.minimum(n + 1, nb - 1) if nxt else n), r * nblk + off + g))
    row = lambda nxt: pl.BlockSpec((W, D_MODEL), lambda r, n: ((jnp.minimum(n + 1, nb - 1) if nxt else n), r))
    dv_ = lambda a: a.reshape(L, d * D_MODEL)
    dk, dv = pl.pallas_call(
        body, name=f"dil_attn_bwd_kv_g{g}", grid=(d, nb),
        in_specs=[spec(3, False), spec(6, False), spec(0, False), spec(0, True),
                  row(False), row(True), row(False), row(True), row(False), row(True)],
        out_specs=[row(False), row(False)],
        out_shape=[jax.ShapeDtypeStruct((L, d * D_MODEL), BF16), jax.ShapeDtypeStruct((L, d * D_MODEL), BF16)],
        compiler_params=_cparams("parallel", "parallel"),
    )(pv, pv, pv, pv, dv_(do), dv_(do), dv_(lse), dv_(lse), dv_(dlt), dv_(dlt))
    return dk.reshape(S, D_MODEL), dv.reshape(S, D_MODEL)


ANY = pl.BlockSpec(memory_space=pl.ANY)


def _place():
    x, y, c = lax.axis_index("x"), lax.axis_index("y"), lax.axis_index("c")
    chips = [(1 - x, y), (x, 1 - y), (1 - x, 1 - y)]
    return x, y, c, chips


def _gather_weights(wb):
    R = wb.shape[0]
    H = R // 2

    def body(w_ref, out_ref, send1, recv1, send2, recv2):
        x, y, c, chips = _place()
        me = 2 * x + y
        sib = (x, y, 1 - c)
        half = pl.ds(c * H, H)
        first = [pltpu.make_async_remote_copy(
            src_ref=w_ref.at[half], dst_ref=out_ref.at[me, half], send_sem=send1.at[k], recv_sem=recv1.at[k],
            device_id=(*chip, c), device_id_type=MESH) for k, chip in enumerate(chips)]
        for cp in first:
            cp.start()
        passed = []
        for k, (cx, cy) in enumerate(chips):
            slot = out_ref.at[2 * cx + cy, half]
            pltpu.make_async_remote_copy(src_ref=slot, dst_ref=slot, send_sem=send1.at[k], recv_sem=recv1.at[k],
                                         device_id=(cx, cy, c), device_id_type=MESH).wait_recv()
            cp = pltpu.make_async_remote_copy(src_ref=slot, dst_ref=slot, send_sem=send2.at[k], recv_sem=recv2.at[k],
                                              device_id=sib, device_id_type=MESH)
            cp.start()
            passed.append(cp)
        for k, (cx, cy) in enumerate(chips):
            slot = out_ref.at[2 * cx + cy, pl.ds((1 - c) * H, H)]
            pltpu.make_async_remote_copy(src_ref=slot, dst_ref=slot, send_sem=send2.at[k], recv_sem=recv2.at[k],
                                         device_id=sib, device_id_type=MESH).wait_recv()
        for cp in first + passed:
            cp.wait_send()

    return pl.pallas_call(
        body, name="gather_weights", in_specs=[ANY], out_specs=ANY,
        out_shape=jax.ShapeDtypeStruct((4, R, D_MODEL), wb.dtype),
        scratch_shapes=[pltpu.SemaphoreType.DMA((3,)), pltpu.SemaphoreType.DMA((3,)),
                        pltpu.SemaphoreType.DMA((3,)), pltpu.SemaphoreType.DMA((3,))],
    )(wb)


def _swap_halves(g):
    H = g.shape[1] // 2

    def body(g_ref, out_ref, send, recv):
        x, y, c, _ = _place()
        sib = (x, y, 1 - c)
        cps = [pltpu.make_async_remote_copy(
            src_ref=g_ref.at[s, pl.ds((1 - c) * H, H)], dst_ref=out_ref.at[s], send_sem=send.at[s],
            recv_sem=recv.at[s], device_id=sib, device_id_type=MESH) for s in range(4)]
        for cp in cps:
            cp.start()
        for cp in cps:
            cp.wait()

    return pl.pallas_call(
        body, name="swap_halves", in_specs=[ANY], out_specs=ANY,
        out_shape=jax.ShapeDtypeStruct((4, H, D_MODEL), g.dtype),
        scratch_shapes=[pltpu.SemaphoreType.DMA((4,)), pltpu.SemaphoreType.DMA((4,))],
    )(g)


def _scatter_partials(pb):
    def body(p_ref, out_ref, send, recv):
        x, y, c, chips = _place()
        me = 2 * x + y
        cps = [pltpu.make_async_remote_copy(
            src_ref=p_ref.at[2 * cx + cy], dst_ref=out_ref.at[me], send_sem=send.at[k], recv_sem=recv.at[k],
            device_id=(cx, cy, c), device_id_type=MESH) for k, (cx, cy) in enumerate(chips)]
        for cp in cps:
            cp.start()
        for k, (cx, cy) in enumerate(chips):
            slot = out_ref.at[2 * cx + cy]
            pltpu.make_async_remote_copy(src_ref=slot, dst_ref=slot, send_sem=send.at[k], recv_sem=recv.at[k],
                                         device_id=(cx, cy, c), device_id_type=MESH).wait_recv()
        for cp in cps:
            cp.wait_send()

    return pl.pallas_call(
        body, name="scatter_partials", in_specs=[ANY], out_specs=ANY,
        out_shape=jax.ShapeDtypeStruct(pb.shape, pb.dtype),
        scratch_shapes=[pltpu.SemaphoreType.DMA((3,)), pltpu.SemaphoreType.DMA((3,))],
    )(pb)


def _sibling_half(f):
    def body(f_ref, out_ref, send, recv):
        x, y, c, _ = _place()
        cp = pltpu.make_async_remote_copy(src_ref=f_ref, dst_ref=out_ref, send_sem=send, recv_sem=recv,
                                          device_id=(x, y, 1 - c), device_id_type=MESH)
        cp.start()
        cp.wait()

    return pl.pallas_call(
        body, name="sibling_half", in_specs=[ANY], out_specs=ANY,
        out_shape=jax.ShapeDtypeStruct(f.shape, f.dtype),
        scratch_shapes=[pltpu.SemaphoreType.DMA, pltpu.SemaphoreType.DMA],
    )(f)


def _gather_tiles(tile, name):
    m_per = tile.shape[0]

    def body(x_ref, out_ref, send_sems, recv_sems, local_sem):
        x, y, c, chips = _place()
        me, sibling = (x, y, c), (x, y, 1 - c)

        def rows(px, py, pc):
            return out_ref.at[pl.ds((4 * px + 2 * py + pc) * m_per, m_per), :]

        def copy(k, block, to, src=None):
            return pltpu.make_async_remote_copy(
                src_ref=rows(*block) if src is None else src, dst_ref=rows(*block),
                send_sem=send_sems.at[k], recv_sem=recv_sems.at[k], device_id=to, device_id_type=MESH)

        mine = pltpu.make_async_copy(x_ref, rows(*me), local_sem)
        mine.start()
        first = [copy(0, me, sibling, src=x_ref)]
        first += [copy(1 + j, me, (*chip, c), src=x_ref) for j, chip in enumerate(chips)]
        for cp in first:
            cp.start()
        passed = [copy(4 + j, (*chip, c), sibling) for j, chip in enumerate(chips)]
        for j, chip in enumerate(chips):
            copy(1 + j, (*chip, c), me).wait_recv()
            passed[j].start()
        copy(0, sibling, me).wait_recv()
        for j, chip in enumerate(chips):
            copy(4 + j, (*chip, 1 - c), me).wait_recv()
        for cp in first + passed:
            cp.wait_send()
        mine.wait()

    return pl.pallas_call(
        body, name=name, out_shape=jax.ShapeDtypeStruct((8 * m_per, D_MODEL), tile.dtype),
        in_specs=[pl.BlockSpec(memory_space=pltpu.VMEM)], out_specs=pl.BlockSpec(memory_space=pltpu.VMEM),
        scratch_shapes=[pltpu.SemaphoreType.DMA((7,)), pltpu.SemaphoreType.DMA((7,)), pltpu.SemaphoreType.DMA],
    )(tile)


def _pad_rows(a, rows):
    return jnp.pad(a, ((0, rows - a.shape[0]), (0, 0)))


def _pad_row(v):
    v = v.reshape(1, -1)
    return jnp.pad(v, ((0, 0), (0, D_MODEL - v.shape[1])))


def _pack_shard(fwi, fwo, dwi, dwo, pwu, pwg):
    parts = [_pad_rows(fwi.reshape(1028, D_MODEL), 1040), fwo.reshape(256, D_MODEL), dwi.reshape(2560, D_MODEL),
             dwo.reshape(256, D_MODEL), pwu.reshape(128, D_MODEL), pwg.reshape(512, D_MODEL),
             jnp.zeros((FLAT_TOTAL - FLAT_USED, D_MODEL), fwi.dtype)]
    return jnp.concatenate(parts, axis=0)


def _unpack_shard(flat):
    out, r0 = {}, 0
    for name, rows, slot in FLAT_ROWS:
        out[name] = flat[r0:r0 + rows]
        r0 += slot
    return (out["fox_w_in"].reshape(1, D_MODEL, 1028), out["fox_w_out"].reshape(1, 256, D_MODEL),
            out["dil_w_in"].reshape(1, D_MODEL, 2560), out["dil_w_out"].reshape(1, 256, D_MODEL),
            out["ple_w_up"].reshape(2, PLE_DIM, 256), out["ple_w_gate"].reshape(2, 256, D_MODEL))


def _unpack_gathered(wall):
    out, r0 = {}, 0
    for name, rows, slot in FLAT_ROWS:
        out[name] = wall[:, r0:r0 + rows]
        r0 += slot
    cols = lambda a, n: a.reshape(4, D_MODEL, n).transpose(1, 0, 2).reshape(D_MODEL, 4 * n)
    fwi = cols(out["fox_w_in"], 1028)
    dwi = cols(out["dil_w_in"], 2560)
    fwo = out["fox_w_out"].reshape(D_MODEL, D_MODEL)
    dwo = out["dil_w_out"].reshape(D_MODEL, D_MODEL)
    pwu4 = out["ple_w_up"].reshape(4, 2, PLE_DIM, 256)
    pwg4 = out["ple_w_gate"].reshape(4, 2, 256, D_MODEL)
    pwu = [pwu4[:, i].transpose(1, 0, 2).reshape(PLE_DIM, D_MODEL) for i in range(2)]
    pwg = [pwg4[:, i].reshape(D_MODEL, D_MODEL) for i in range(2)]
    return fwi, fwo, dwi, dwo, pwu, pwg


def _pack_grads(gfwi, gfwo, gdwi, gdwo, gpwu, gpwg):
    cols = lambda a, n: a.reshape(D_MODEL, 4, n).transpose(1, 0, 2).reshape(4, n, D_MODEL)
    up = jnp.stack([a.reshape(PLE_DIM, 4, 256).transpose(1, 0, 2) for a in gpwu], axis=1)
    gate = jnp.stack([a.reshape(4, 256, D_MODEL) for a in gpwg], axis=1)
    parts = [jnp.pad(cols(gfwi, 1028), ((0, 0), (0, 12), (0, 0))), gfwo.reshape(4, 256, D_MODEL), cols(gdwi, 2560),
             gdwo.reshape(4, 256, D_MODEL), up.reshape(4, 128, D_MODEL), gate.reshape(4, 512, D_MODEL),
             jnp.zeros((4, FLAT_TOTAL - FLAT_USED, D_MODEL), F32)]
    return jnp.concatenate(parts, axis=1)


def _local_step(x, p0, p1, tgt, fox_g, dil_g, fin_g, b_f, fwi, fwo, dwi, dwo, pwu, pwg):
    S = x.shape[0]
    tm = min(256, S)
    nh = FOX_HEADS // 2
    w_qkv0 = jnp.concatenate([fwi[:, :D_MODEL] * 0.125, fwi[:, D_MODEL:3 * D_MODEL]], axis=1)
    w_z0 = fwi[:, 3 * D_MODEL:4 * D_MODEL]
    swapped = lambda a: jnp.repeat(a.reshape(-1, nh, 2)[:, :, ::-1].reshape(-1, FOX_HEADS), FOX_HEAD_DIM, axis=1)
    w_f0 = swapped(fwi[:, 4 * D_MODEL:])
    b_full = swapped(b_f.reshape(1, FOX_HEADS))
    w_qkv1 = dwi[:, :3 * DIL_QKV]
    w_z1 = dwi[:, 3 * DIL_QKV:]

    n0, = _rows(_rms_fwd, [x], [(D_MODEL, BF16)], name="norm0", tm=tm, bcast=[fox_g])
    qkv0 = _mm(n0, w_qkv0, out_dtype=BF16, name="proj_qkv0")
    z0 = _mm(n0, w_z0, name="proj_z0")
    fl0 = _mm(n0, w_f0, name="proj_f0")
    c_full = _cumsum_rows([fl0], name="forget_cumsum", pre=_log_forget, bcast=[b_full])
    qx0, kx0 = _rows(_fox_extras, [c_full], [(D_MODEL, BF16), (D_MODEL, BF16)], name="fox_extras", tm=tm)
    o0, a0 = _fox_fwd2(qkv0, qx0, kx0)
    g0, = _rows(_gate_fwd, [o0, z0], [(D_MODEL, BF16)], name="gate0", tm=tm)
    h1 = _mm(g0, fwo, add=x, name="out_proj0")
    u0 = _mm(p0, pwu[0], name="ple_up0")
    t0 = _mm(h1, pwg[0], name="ple_gate0")
    h2, = _rows(_ple_fwd, [h1, u0, t0], [(D_MODEL, F32)], name="ple_mix0", tm=tm)

    n1, = _rows(_rms_fwd, [h2], [(D_MODEL, BF16)], name="norm1", tm=tm, bcast=[dil_g])
    qkv1 = _mm(n1, w_qkv1, out_dtype=BF16, name="proj_qkv1")
    z1 = _mm(n1, w_z1, name="proj_z1")
    og, lg = zip(*[_dil_fwd(qkv1, g) for g in range(DIL_GROUPS)])
    o1, g1, lse1 = _rows(_dil_combine, [*og, *lg, z1], [(D_MODEL, F32), (D_MODEL, BF16), (D_MODEL, F32)],
                         name="dil_combine", tm=tm)
    h3 = _mm(g1, dwo, add=h2, name="out_proj1")
    u1 = _mm(p1, pwu[1], name="ple_up1")
    t1 = _mm(h3, pwg[1], name="ple_gate1")
    h4, = _rows(_ple_fwd, [h3, u1, t1], [(D_MODEL, F32)], name="ple_mix1", tm=tm)

    dh4, g_fin, loss = _rows(_final_stage, [h4, tgt], [(D_MODEL, F32)], name="loss_head", tm=tm, bcast=[fin_g],
                             acc=[((1, D_MODEL), F32), ((1, LANES), F32)])

    du1, dt1 = _rows(_ple_bwd, [dh4, u1, t1], [(D_MODEL, BF16), (D_MODEL, BF16)], name="ple_bwd1", tm=tm)
    g_up1 = _mm(p1, du1, ta=True, name="grad_ple_up1")
    g_gate1 = _mm(h3, dt1, ta=True, name="grad_ple_gate1")
    dh3 = _mm(dt1, pwg[1], tb=True, add=dh4, name="ple_back1")
    dg1 = _mm(dh3, dwo, tb=True, name="out_back1")
    g_dwo = _mm(g1, dh3, ta=True, name="grad_out1")
    do1, dz1, dl1 = _rows(functools.partial(_gate_bwd, head_dim=DIL_HEAD_DIM), [dg1, o1, z1],
                          [(D_MODEL, BF16), (D_MODEL, BF16), (D_MODEL, F32)], name="gate_bwd1", tm=tm)
    dq1 = [_dil_bwd_q(qkv1, do1, lse1, dl1, g) for g in range(DIL_GROUPS)]
    dk1, dv1 = zip(*[_dil_bwd_kv(qkv1, do1, lse1, dl1, g) for g in range(DIL_GROUPS)])
    dproj1 = jnp.concatenate([*dq1, *dk1, *dv1, dz1], axis=1)
    g_dwi = _mm(n1, dproj1, ta=True, name="grad_in1")
    dn1 = _mm(dproj1, dwi, tb=True, name="in_back1")
    dh2, g_dil = _rows(_rms_bwd, [dn1, h2, dh3], [(D_MODEL, F32)], name="norm_bwd1", tm=tm, bcast=[dil_g],
                       acc=[((1, D_MODEL), F32)])

    du0, dt0 = _rows(_ple_bwd, [dh2, u0, t0], [(D_MODEL, BF16), (D_MODEL, BF16)], name="ple_bwd0", tm=tm)
    g_up0 = _mm(p0, du0, ta=True, name="grad_ple_up0")
    g_gate0 = _mm(h1, dt0, ta=True, name="grad_ple_gate0")
    dh1 = _mm(dt0, pwg[0], tb=True, add=dh2, name="ple_back0")
    dg0 = _mm(dh1, fwo, tb=True, name="out_back0")
    g_fwo = _mm(g0, dh1, ta=True, name="grad_out0")
    do0, dz0, qxa0, dx0 = _rows(_gate_bwd_fox2, [dg0, o0, z0, a0], [(D_MODEL, BF16)] * 4, name="gate_bwd0", tm=tm)
    dq0, drx, dk0, dv0, dkx = _fox_bwd2(qkv0, do0, qxa0, dx0, kx0)
    dlogf = _cumsum_rows([drx, dkx], name="forget_cumsum_bwd", reverse=True, pre=_forget_dc2)
    df0, g_bf = _rows(_forget_bwd2, [dlogf, fl0], [(D_MODEL, BF16)], name="forget_bwd", tm=tm, bcast=[b_full],
                      acc=[((1, D_MODEL), F32)])
    dproj0 = jnp.concatenate([dq0.astype(BF16), dk0, dv0, dz0], axis=1)
    w_main0 = fwi[:, :4 * D_MODEL]
    g_fwi_main = _mm(n0, dproj0, ta=True, name="grad_in0")
    g_fwi_f = _mm(n0, df0, ta=True, name="grad_in0_forget")
    dn0 = _mm(dproj0, w_main0, tb=True, name="in_back0")
    dn0 = _mm(df0, w_f0, tb=True, add=dn0, name="in_back0_forget")
    dx, g_fox = _rows(_rms_bwd, [dn0, x, dh1], [(D_MODEL, F32)], name="norm_bwd0", tm=tm, bcast=[fox_g],
                      acc=[((1, D_MODEL), F32)])
    first_lane = lambda a: a.reshape(-1, nh, 2, FOX_HEAD_DIM)[:, :, ::-1, 0].reshape(-1, FOX_HEADS)
    g_fwi = jnp.concatenate([g_fwi_main, first_lane(g_fwi_f)], axis=1)
    return (loss, dx, (g_fwi, g_fwo, g_dwi, g_dwo, [g_up0, g_up1], [g_gate0, g_gate1]),
            (g_fox, _pad_row(first_lane(g_bf)), g_fin, g_dil))


def kernel(x, p, fox_norm, fox_w_in, fox_b_f, fox_w_out, dil_norm, dil_w_in, dil_w_out, ple_w_up, ple_w_gate, final_norm, loss_target, m_fox_norm, m_fox_w_in, m_fox_b_f, m_fox_w_out, m_dil_norm, m_dil_w_in, m_dil_w_out, m_ple_w_up, m_ple_w_gate, m_final_norm, v_fox_norm, v_fox_w_in, v_fox_b_f, v_fox_w_out, v_dil_norm, v_dil_w_in, v_dil_w_out, v_ple_w_up, v_ple_w_gate, v_final_norm):
    xi, yi, ci = lax.axis_index("x"), lax.axis_index("y"), lax.axis_index("c")
    chip = 2 * xi + yi

    w_flat = _pack_shard(fox_w_in, fox_w_out, dil_w_in, dil_w_out, ple_w_up, ple_w_gate)
    m_flat = _pack_shard(m_fox_w_in, m_fox_w_out, m_dil_w_in, m_dil_w_out, m_ple_w_up, m_ple_w_gate)
    v_flat = _pack_shard(v_fox_w_in, v_fox_w_out, v_dil_w_in, v_dil_w_out, v_ple_w_up, v_ple_w_gate)
    w_bf = w_flat.astype(BF16)
    wall = lax.dynamic_update_slice(_gather_weights(w_bf), w_bf[None], (chip, 0, 0))
    fwi, fwo, dwi, dwo, pwu, pwg = _unpack_gathered(wall)
    gains = _gather_tiles(_pad_rows(_pad_row(dil_norm), 8), "gather_gain").reshape(4, 2, 8, D_MODEL)
    dil_g = gains[:, 0, 0, :256].reshape(1, D_MODEL)

    loss_part, dx, grads, small = _local_step(
        x[0], p[0, 0], p[1, 0], loss_target[0], fox_norm.reshape(1, D_MODEL), dil_g,
        final_norm.reshape(1, D_MODEL), fox_b_f, fwi, fwo, dwi, dwo, pwu, pwg)
    g_fox, g_bf, g_fin, g_dil = small

    g_all = _pack_grads(*grads)
    theirs = _swap_halves(g_all)
    mine = lax.dynamic_slice_in_dim(g_all, ci * FLAT_HALF, FLAT_HALF, axis=1)
    part, = _rows(lambda a, b: a + b, [mine.reshape(4 * FLAT_HALF, D_MODEL), theirs.reshape(4 * FLAT_HALF, D_MODEL)],
                  [(D_MODEL, BF16)], name="pair_sum", tm=FLAT_TILE)
    part = part.reshape(4, FLAT_HALF, D_MODEL)
    own = lax.dynamic_slice_in_dim(part, chip, 1, axis=0)
    by_chip = lax.dynamic_update_slice(_scatter_partials(part), own, (chip, 0, 0))
    half_sum, = _rows(lambda a, b, c, d: ((a.astype(F32) + b.astype(F32)) + c.astype(F32)) + d.astype(F32),
                      [by_chip[s] for s in range(4)], [(D_MODEL, F32)], name="chip_sum", tm=FLAT_TILE)
    other_half = _sibling_half(half_sum)
    g_flat = jnp.where(ci == 0, jnp.concatenate([half_sum, other_half], axis=0),
                       jnp.concatenate([other_half, half_sum], axis=0))

    tile = jnp.concatenate([g_fox, g_bf, g_fin, g_dil, jnp.pad(loss_part, ((0, 0), (0, D_MODEL - LANES))),
                            jnp.zeros((3, D_MODEL), F32)], axis=0)
    tiles = _gather_tiles(tile, "gather_small")
    tot, = _rows(lambda *t: functools.reduce(lambda a, b: a + b, t), [tiles[8 * s:8 * s + 8] for s in range(8)],
                 [(D_MODEL, F32)], name="small_sum", tm=8)
    loss = tot[4, 0]
    g_small = jnp.concatenate([
        tot[0:3],
        _pad_row(lax.dynamic_slice_in_dim(tot[3], chip * 256, 256)), jnp.zeros((4, D_MODEL), F32)], axis=0)
    small_tile = lambda a, b, c, d: jnp.concatenate(
        [_pad_row(a), _pad_row(b), _pad_row(c), _pad_row(d), jnp.zeros((4, D_MODEL), F32)], axis=0)
    w_small = small_tile(fox_norm, fox_b_f, final_norm, dil_norm)
    m_small = small_tile(m_fox_norm, m_fox_b_f, m_final_norm, m_dil_norm)
    v_small = small_tile(v_fox_norm, v_fox_b_f, v_final_norm, v_dil_norm)

    three = [(D_MODEL, F32)] * 3
    d_flat, nm_flat, nv_flat = _rows(_adamw, [w_flat, g_flat, m_flat, v_flat], three, name="adamw", tm=FLAT_TILE)
    d_small, nm_small, nv_small = _rows(_adamw, [w_small, g_small, m_small, v_small], three, name="adamw_small", tm=8)

    def leaves(flat, small_rows):
        fwi_, fwo_, dwi_, dwo_, pwu_, pwg_ = _unpack_shard(flat)
        return (small_rows[0:1], fwi_, small_rows[1:2, :FOX_HEADS], fwo_, small_rows[3:4, :256], dwi_, dwo_,
                pwu_, pwg_, small_rows[2])

    return (loss, dx[None], *leaves(g_flat, g_small), *leaves(d_flat, d_small), *leaves(nm_flat, nm_small),
            *leaves(nv_flat, nv_small))
```

```python
import functools

import numpy as np
import jax
import jax.numpy as jnp
from jax import lax
from jax.experimental import pallas as pl
from jax.experimental.pallas import tpu as pltpu

F32 = jnp.float32
BF16 = jnp.bfloat16

D_MODEL = 1024
PLE_DIM = 256
FOX_HEADS = 16
FOX_HEAD_DIM = 64
DIL_PATTERN = ((128, 1), (512, 4), (2048, 16))
DIL_GROUPS = 3
DIL_HEADS_PER_GROUP = 8
DIL_HEAD_DIM = 128
DIL_WINDOW_STEPS = 128
DIL_QKV = 3072
ALIBI_MAX_EXP = 8.0
RMS_EPS = 1e-6
ADAM_LR, ADAM_B1, ADAM_B2, ADAM_EPS, ADAM_WD, ADAM_STEP = 0.001, 0.9, 0.999, 1e-08, 0.01, 10

LANES = 128
VMEM_LIMIT = 56 * 1024 * 1024
MESH = pl.DeviceIdType.MESH
NEG = -1e30

FLAT_ROWS = (("fox_w_in", 1028, 1040), ("fox_w_out", 256, 256), ("dil_w_in", 2560, 2560), ("dil_w_out", 256, 256),
             ("ple_w_up", 128, 128), ("ple_w_gate", 512, 512))
FLAT_USED = sum(r for _, _, r in FLAT_ROWS)
FLAT_TOTAL = 4864
FLAT_HALF = FLAT_TOTAL // 2
FLAT_TILE = 608


def _cparams(*sem):
    return pltpu.CompilerParams(dimension_semantics=sem, vmem_limit_bytes=VMEM_LIMIT)


def _sigmoid(x):
    return 1.0 / (1.0 + jnp.exp(-x))


def _mm(a, b, *, name, ta=False, tb=False, out_dtype=F32, add=None):
    if ta:
        K, M = a.shape
    else:
        M, K = a.shape
    if tb:
        N, Kb = b.shape
    else:
        Kb, N = b.shape
    assert K == Kb, (a.shape, b.shape)
    tm, tn, tk = min(M, 1024), min(N, 1024), min(K, 1024)
    assert M % tm == 0 and N % tn == 0 and K % tk == 0, (M, N, K)
    nk = K // tk
    dn = (((0 if ta else 1,), (1 if tb else 0,)), ((), ()))

    def body(*refs):
        if add is None:
            a_ref, b_ref, o_ref, acc = refs
        else:
            a_ref, b_ref, add_ref, o_ref, acc = refs
        k = pl.program_id(2)

        @pl.when(k == 0)
        def _():
            acc[...] = jnp.zeros_like(acc)

        acc[...] += lax.dot_general(a_ref[...].astype(BF16), b_ref[...].astype(BF16), dn,
                                    preferred_element_type=F32)

        @pl.when(k == nk - 1)
        def _():
            r = acc[...]
            if add is not None:
                r = r + add_ref[...]
            o_ref[...] = r.astype(out_dtype)

    a_spec = (pl.BlockSpec((tk, tm), lambda i, j, k: (k, i)) if ta
              else pl.BlockSpec((tm, tk), lambda i, j, k: (i, k)))
    b_spec = (pl.BlockSpec((tn, tk), lambda i, j, k: (j, k)) if tb
              else pl.BlockSpec((tk, tn), lambda i, j, k: (k, j)))
    in_specs = [a_spec, b_spec]
    args = [a, b]
    if add is not None:
        in_specs.append(pl.BlockSpec((tm, tn), lambda i, j, k: (i, j)))
        args.append(add)
    return pl.pallas_call(
        body, name=name, grid=(M // tm, N // tn, nk),
        in_specs=in_specs, out_specs=pl.BlockSpec((tm, tn), lambda i, j, k: (i, j)),
        out_shape=jax.ShapeDtypeStruct((M, N), out_dtype),
        scratch_shapes=[pltpu.VMEM((tm, tn), F32)],
        compiler_params=_cparams("parallel", "parallel", "arbitrary"),
    )(*args)


def _rows(fn, ins, outs, *, name, tm, bcast=(), acc=()):
    R = ins[0].shape[0]
    assert R % tm == 0, (R, tm)
    n_in, n_b, n_out, n_acc = len(ins), len(bcast), len(outs), len(acc)

    def body(*refs):
        in_refs = refs[:n_in + n_b]
        out_refs = refs[n_in + n_b:n_in + n_b + n_out]
        acc_refs = refs[n_in + n_b + n_out:]
        res = fn(*[r[...] for r in in_refs])
        if not isinstance(res, (tuple, list)):
            res = (res,)
        for r, v in zip(out_refs, res[:n_out]):
            r[...] = v.astype(r.dtype)
        first = pl.program_id(0) == 0
        for r, v in zip(acc_refs, res[n_out:]):
            @pl.when(first)
            def _(r=r, v=v):
                r[...] = v.astype(r.dtype)

            @pl.when(jnp.logical_not(first))
            def _(r=r, v=v):
                r[...] += v.astype(r.dtype)

    in_specs = [pl.BlockSpec((tm, a.shape[1]), lambda i: (i, 0)) for a in ins]
    in_specs += [pl.BlockSpec(b.shape, lambda i, nd=b.ndim: (0,) * nd) for b in bcast]
    out_specs = [pl.BlockSpec((tm, c), lambda i: (i, 0)) for c, _ in outs]
    out_specs += [pl.BlockSpec(s, lambda i, nd=len(s): (0,) * nd) for s, _ in acc]
    out_shape = [jax.ShapeDtypeStruct((R, c), dt) for c, dt in outs]
    out_shape += [jax.ShapeDtypeStruct(s, dt) for s, dt in acc]
    res = pl.pallas_call(
        body, name=name, grid=(R // tm,), in_specs=in_specs, out_specs=out_specs, out_shape=out_shape,
        compiler_params=_cparams("arbitrary" if acc else "parallel"),
    )(*ins, *bcast)
    return res


def _cumsum_rows(ins, *, name, reverse=False, pre=None, bcast=()):
    S, C = ins[0].shape
    tb = 256
    nb = S // tb
    assert S % tb == 0
    n_in = len(ins) + len(bcast)

    def body(*refs):
        in_refs, o_ref, carry = refs[:n_in], refs[n_in], refs[n_in + 1]

        @pl.when(pl.program_id(0) == 0)
        def _():
            carry[...] = jnp.zeros_like(carry)

        xv = in_refs[0][...] if pre is None else pre(*[r[...] for r in in_refs])
        r_ = lax.broadcasted_iota(jnp.int32, (tb, tb), 0)
        c_ = lax.broadcasted_iota(jnp.int32, (tb, tb), 1)
        tri = jnp.where((c_ >= r_) if reverse else (c_ <= r_), 1.0, 0.0).astype(BF16)
        hi = xv.astype(BF16)
        r1 = xv - hi.astype(F32)
        mid = r1.astype(BF16)
        lo = (r1 - mid.astype(F32)).astype(BF16)
        cs = (jnp.dot(tri, hi, preferred_element_type=F32) + jnp.dot(tri, mid, preferred_element_type=F32)
              + jnp.dot(tri, lo, preferred_element_type=F32)) + carry[...]
        o_ref[...] = cs
        carry[...] = cs[0:1, :] if reverse else cs[tb - 1:tb, :]

    blk = (lambda i: (nb - 1 - i, 0)) if reverse else (lambda i: (i, 0))
    in_specs = [pl.BlockSpec((tb, C), blk) for _ in ins]
    in_specs += [pl.BlockSpec(b.shape, lambda i, nd=b.ndim: (0,) * nd) for b in bcast]
    return pl.pallas_call(
        body, name=name, grid=(nb,), in_specs=in_specs, out_specs=pl.BlockSpec((tb, C), blk),
        out_shape=jax.ShapeDtypeStruct((S, C), F32), scratch_shapes=[pltpu.VMEM((1, C), F32)],
        compiler_params=_cparams("arbitrary"),
    )(*ins, *bcast)


def _rms_fwd(x, g):
    r = lax.rsqrt(jnp.mean(x * x, axis=1, keepdims=True) + RMS_EPS)
    return (x * r) * g


def _rms_bwd(dn, x, dres, g):
    r = lax.rsqrt(jnp.mean(x * x, axis=1, keepdims=True) + RMS_EPS)
    xh = x * r
    w = dn * g
    dx = r * (w - xh * jnp.mean(w * xh, axis=1, keepdims=True))
    return dres + dx, jnp.sum(dn * xh, axis=0, keepdims=True)


def _final_stage(h, tgt, g):
    r = lax.rsqrt(jnp.mean(h * h, axis=1, keepdims=True) + RMS_EPS)
    xh = h * r
    diff = xh * g - tgt
    loss = 0.5 * jnp.sum(jnp.mean(diff * diff, axis=1, keepdims=True), axis=0, keepdims=True)
    dy = diff * (1.0 / D_MODEL)
    w = dy * g
    dh = r * (w - xh * jnp.mean(w * xh, axis=1, keepdims=True))
    return dh, jnp.sum(dy * xh, axis=0, keepdims=True), jnp.broadcast_to(loss, (1, LANES))


def _ple_fwd(h, u, t):
    return h + u * _sigmoid(t)


def _ple_bwd(dh, u, t):
    s = _sigmoid(t)
    return dh * s, dh * u * s * (1.0 - s)


def _gate_fwd(o, z):
    return o * (z * _sigmoid(z))


def _head_sums(prod):
    tm, width = prod.shape
    cols = [jnp.broadcast_to(jnp.sum(prod[:, b * LANES:(b + 1) * LANES], axis=1, keepdims=True), (tm, LANES))
            for b in range(width // LANES)]
    return jnp.concatenate(cols, axis=1)


def _gate_bwd(dg, o, z):
    s = _sigmoid(z)
    do = dg * (z * s)
    dz = dg * o * (s * (1.0 + z * (1.0 - s)))
    return do, dz, _head_sums(do * o)


def _log_forget(fl, b):
    u = fl + b
    return jnp.minimum(u, 0.0) - jnp.log(1.0 + jnp.exp(-jnp.abs(u)))


def _adamw(w, g, m, v):
    m = ADAM_B1 * m + (1.0 - ADAM_B1) * g
    v = ADAM_B2 * v + (1.0 - ADAM_B2) * (g * g)
    m_hat = m / (1.0 - ADAM_B1 ** ADAM_STEP)
    v_hat = v / (1.0 - ADAM_B2 ** ADAM_STEP)
    delta = -ADAM_LR * (m_hat / (jnp.sqrt(v_hat) + ADAM_EPS) + ADAM_WD * w)
    return delta, m, v


FOX_TQ = 512


def _split3(x):
    p1 = x.astype(BF16).astype(F32)
    r = x - p1
    p2 = r.astype(BF16).astype(F32)
    return p1, p2, r - p2


def _lane_in_head(shape):
    return lax.broadcasted_iota(jnp.int32, shape, 1) % FOX_HEAD_DIM


def _query_extras(x):
    lm = _lane_in_head(x.shape)
    p1, p2, p3 = _split3(x)
    return jnp.where(lm == 0, p1, jnp.where(lm == 1, p2, jnp.where(lm == 2, p3, jnp.where(lm < 6, 1.0, 0.0))))


def _key_extras(c):
    lm = _lane_in_head(c.shape)
    p1, p2, p3 = _split3(c)
    return jnp.where(lm < 3, 1.0, jnp.where(lm == 3, -p1, jnp.where(lm == 4, -p2, jnp.where(lm == 5, -p3, 0.0))))


def _fox_extras(c):
    return _query_extras(c), _key_extras(c)


def _swapped_head_sums(prod):
    tm, width = prod.shape
    lane = lax.broadcasted_iota(jnp.int32, (tm, LANES), 1)
    low = lane < FOX_HEAD_DIM
    cols = []
    for b in range(width // LANES):
        blk = prod[:, b * LANES:(b + 1) * LANES]
        sa = jnp.sum(jnp.where(low, blk, 0.0), axis=1, keepdims=True)
        sb = jnp.sum(jnp.where(low, 0.0, blk), axis=1, keepdims=True)
        cols.append(jnp.where(low, sb, sa))
    return jnp.concatenate(cols, axis=1)


def _gate_bwd_fox(dg, o, z, a):
    s = _sigmoid(z)
    do = dg * (z * s)
    dz = dg * o * (s * (1.0 + z * (1.0 - s)))
    lm = _lane_in_head(do.shape)
    d1, d2, d3 = _split3(-_swapped_head_sums(do * o))
    dx = jnp.where(lm == 0, d1, jnp.where(lm == 1, d2, jnp.where(lm == 2, d3, 0.0)))
    return do, dz, _query_extras(a), dx


def _forget_dc(drx, dkx):
    lm = _lane_in_head(drx.shape)
    return jnp.where(lm == 0, drx, 0.0) - jnp.where(lm == 3, dkx, 0.0)


def _forget_bwd(dl, fl, b):
    lm = _lane_in_head(dl.shape)
    width = dl.shape[1]
    both = dl + pltpu.roll(dl, 3, 1) + pltpu.roll(dl, width - 3, 1)
    du = jnp.where(lm == 0, both, 0.0) * (1.0 / (1.0 + jnp.exp(fl + b)))
    return du, jnp.sum(du, axis=0, keepdims=True)


def _fox_fwd(qkv, qx, kx):
    S = qkv.shape[0]
    tq = min(FOX_TQ, S)
    nq = S // tq
    nt = (((1,), (1,)), ((), ()))

    def body(q_ref, qx_ref, k_ref, v_ref, kx_ref, o_ref, a_ref):
        i = pl.program_id(1)
        low = lax.broadcasted_iota(jnp.int32, (tq, LANES), 1) < FOX_HEAD_DIM
        row = lax.broadcasted_iota(jnp.int32, (tq, tq), 0)
        col = lax.broadcasted_iota(jnp.int32, (tq, tq), 1)
        q2, x2 = q_ref[...], qx_ref[...]
        qa = (jnp.where(low, q2, x2), jnp.where(low, x2, q2))

        def step(kb, carry, diag):
            start = pl.multiple_of(kb * tq, tq)
            k2 = k_ref[pl.ds(start, tq), :]
            v2 = v_ref[pl.ds(start, tq), :]
            y2 = kx_ref[pl.ds(start, tq), :]
            one = jnp.ones_like(v2)
            ka = (jnp.where(low, k2, y2), jnp.where(low, y2, k2))
            va = (jnp.where(low, v2, one), jnp.where(low, one, v2))
            out = []
            for hh in range(2):
                m, acc = carry[hh]
                s = lax.dot_general(qa[hh], ka[hh], nt, preferred_element_type=F32)
                if diag:
                    s = jnp.where(col <= row, s, NEG)
                m_new = jnp.maximum(m, jnp.max(s, axis=1, keepdims=True))
                pr = jnp.exp(s - m_new)
                acc = jnp.exp(m - m_new) * acc + jnp.dot(pr.astype(BF16), va[hh], preferred_element_type=F32)
                out.append((m_new, acc))
            return tuple(out)

        init = ((jnp.full((tq, 1), NEG, F32), jnp.zeros((tq, LANES), F32)),) * 2
        carry = lax.fori_loop(0, i, functools.partial(step, diag=False), init)
        (m_a, acc_a), (m_b, acc_b) = step(i, carry, True)
        l_a, l_b = acc_a[:, FOX_HEAD_DIM:FOX_HEAD_DIM + 1], acc_b[:, 0:1]
        xf = x2.astype(F32)
        c_a = xf[:, 64:65] + xf[:, 65:66] + xf[:, 66:67]
        c_b = xf[:, 0:1] + xf[:, 1:2] + xf[:, 2:3]
        o_ref[...] = jnp.where(low, acc_a / l_a, acc_b / l_b)
        a_ref[...] = jnp.where(low, jnp.broadcast_to(c_b - (m_b + jnp.log(l_b)), (tq, LANES)),
                               jnp.broadcast_to(c_a - (m_a + jnp.log(l_a)), (tq, LANES)))

    blk = lambda cb: pl.BlockSpec((tq, LANES), lambda h, i, cb=cb: (i, cb + h))
    res = lambda cb: pl.BlockSpec((S, LANES), lambda h, i, cb=cb: (0, cb + h))
    return pl.pallas_call(
        body, name="fox_attn_fwd", grid=(FOX_HEADS // 2, nq),
        in_specs=[blk(0), blk(0), res(8), res(16), res(0)],
        out_specs=[blk(0), blk(0)],
        out_shape=[jax.ShapeDtypeStruct((S, D_MODEL), F32), jax.ShapeDtypeStruct((S, D_MODEL), F32)],
        compiler_params=_cparams("parallel", "arbitrary"),
    )(qkv, qx, qkv, qkv, kx)


def _fox_bwd(qkv, do, qxa, dx, kx):
    S = qkv.shape[0]
    tq = min(FOX_TQ, S)
    nq = S // tq
    nt = (((1,), (1,)), ((), ()))
    tn = (((0,), (0,)), ((), ()))

    def body(q_ref, qx_ref, do_ref, dx_ref, k_ref, v_ref, kx_ref, dq_ref, dr_ref, dk_ref, dv_ref, dkx_ref):
        kb = pl.program_id(1)
        low = lax.broadcasted_iota(jnp.int32, (tq, LANES), 1) < FOX_HEAD_DIM
        row = lax.broadcasted_iota(jnp.int32, (tq, tq), 0)
        col = lax.broadcasted_iota(jnp.int32, (tq, tq), 1)

        @pl.when(kb == 0)
        def _():
            dq_ref[...] = jnp.zeros_like(dq_ref)
            dr_ref[...] = jnp.zeros_like(dr_ref)

        k2, v2, y2 = k_ref[...], v_ref[...], kx_ref[...]
        one = jnp.ones_like(v2)
        ka = (jnp.where(low, k2, y2), jnp.where(low, y2, k2))
        va = (jnp.where(low, v2, one), jnp.where(low, one, v2))

        def step(qb, carry, diag):
            start = pl.multiple_of(qb * tq, tq)
            q2 = q_ref[pl.ds(start, tq), :]
            x2 = qx_ref[pl.ds(start, tq), :]
            d2 = do_ref[pl.ds(start, tq), :]
            e2 = dx_ref[pl.ds(start, tq), :]
            qa = (jnp.where(low, q2, x2), jnp.where(low, x2, q2))
            da = (jnp.where(low, d2, e2), jnp.where(low, e2, d2))
            new, res = [], []
            for hh in range(2):
                dk, dv = carry[hh]
                s = lax.dot_general(qa[hh], ka[hh], nt, preferred_element_type=F32)
                if diag:
                    s = jnp.where(col <= row, s, NEG)
                pr = jnp.exp(s)
                ds = pr * lax.dot_general(da[hh], va[hh], nt, preferred_element_type=F32)
                prb, dsb = pr.astype(BF16), ds.astype(BF16)
                dv = dv + lax.dot_general(prb, da[hh], tn, preferred_element_type=F32)
                dk = dk + lax.dot_general(dsb, qa[hh], tn, preferred_element_type=F32)
                res.append(jnp.dot(dsb, ka[hh], preferred_element_type=F32))
                new.append((dk, dv))
            dq_ref[pl.ds(start, tq), :] += jnp.where(low, res[0], res[1])
            dr_ref[pl.ds(start, tq), :] += jnp.where(low, res[1], res[0])
            return tuple(new)

        init = ((jnp.zeros((tq, LANES), F32), jnp.zeros((tq, LANES), F32)),) * 2
        carry = step(kb, init, True)
        (dk_a, dv_a), (dk_b, dv_b) = lax.fori_loop(kb + 1, nq, functools.partial(step, diag=False), carry)
        dk_ref[...] = jnp.where(low, dk_a, dk_b).astype(BF16)
        dv_ref[...] = jnp.where(low, dv_a, dv_b).astype(BF16)
        dkx_ref[...] = jnp.where(low, dk_b, dk_a)

        @pl.when(kb == nq - 1)
        def _():
            dq_ref[...] = dq_ref[...] * (FOX_HEAD_DIM ** -0.5)

    res = lambda cb: pl.BlockSpec((S, LANES), lambda h, k, cb=cb: (0, cb + h))
    blk = lambda cb: pl.BlockSpec((tq, LANES), lambda h, k, cb=cb: (k, cb + h))
    f32, b16 = jax.ShapeDtypeStruct((S, D_MODEL), F32), jax.ShapeDtypeStruct((S, D_MODEL), BF16)
    return pl.pallas_call(
        body, name="fox_attn_bwd", grid=(FOX_HEADS // 2, nq),
        in_specs=[res(0), res(0), res(0), res(0), blk(8), blk(16), blk(0)],
        out_specs=[res(0), res(0), blk(0), blk(0), blk(0)],
        out_shape=[f32, f32, b16, b16, f32],
        compiler_params=_cparams("parallel", "arbitrary"),
    )(qkv, qxa, do, dx, qkv, qkv, kx)


def _alibi_slopes():
    n = DIL_GROUPS * DIL_HEADS_PER_GROUP
    s = np.float32(2.0) ** (np.float32(-ALIBI_MAX_EXP) * np.arange(1, n + 1, dtype=np.float32) / np.float32(n))
    return s.astype(np.float32).reshape(DIL_GROUPS, DIL_HEADS_PER_GROUP)


W = DIL_WINDOW_STEPS
DIL_SCALE = DIL_HEAD_DIM ** -0.5


DIL_ROWS = 2048
NT = (((1,), (1,)), ((), ()))
TN = (((0,), (0,)), ((), ()))
QKV_BLOCKS = DIL_QKV // LANES


def _slope_table():
    t = np.zeros((DIL_HEADS_PER_GROUP, 8, LANES), np.float32)
    t[:, :DIL_GROUPS, :] = _alibi_slopes().T[:, :, None]
    return jnp.asarray(t)


def _phase_rows(start, d):
    return pl.ds(start, W, stride=d) if d > 1 else pl.ds(start, W)


def _for_phases(d, unit):
    if d == 1:
        unit(0)
    else:
        lax.fori_loop(0, d, lambda r, c: (unit(r), c)[1], 0)


def _window_geometry(shape, q_axis_offset):
    i_ = lax.broadcasted_iota(jnp.int32, shape, 0)
    j_ = lax.broadcasted_iota(jnp.int32, shape, 1)
    dist = q_axis_offset + i_ - j_
    return i_, j_, dist, (dist >= 0) & (dist <= W)


def _dil_fwd(qkv, z):
    S = qkv.shape[0]
    rb = min(DIL_ROWS, S)
    nb = S // rb
    spans = [W * d for _, d in DIL_PATTERN]

    def body(*refs):
        slope_ref = refs[0]
        q, kc, kp, vc, vp = refs[1:4], refs[4:7], refs[7:10], refs[10:13], refs[13:16]
        z_ref, o_ref, g_ref, l_ref = refs[16:20]
        og, lg = refs[20:23], refs[23:26]
        i = pl.program_id(1)
        _, j_, dist, inwin = _window_geometry((W, 2 * W), W)
        distf = dist.astype(F32)
        for g, (_, d) in enumerate(DIL_PATTERN):
            span = spans[g]
            bias = -(slope_ref[0, g:g + 1, 0:1] * float(d)) * distf
            for b in range(rb // span):
                def unit(r, g=g, d=d, span=span, b=b, bias=bias):
                    base = b * span + r
                    rows = _phase_rows(base, d)
                    if b > 0:
                        kprev, vprev = kc[g][_phase_rows(base - span, d), :], vc[g][_phase_rows(base - span, d), :]
                        valid = inwin
                    else:
                        kprev, vprev = kp[g][_phase_rows(r, d), :], vp[g][_phase_rows(r, d), :]
                        valid = inwin & ((j_ >= W) | (i > 0))
                    k2 = jnp.concatenate([kprev, kc[g][rows, :]], axis=0).astype(BF16)
                    v2 = jnp.concatenate([vprev, vc[g][rows, :]], axis=0).astype(BF16)
                    s = lax.dot_general(q[g][rows, :].astype(BF16), k2, NT, preferred_element_type=F32)
                    s = jnp.where(valid, s * DIL_SCALE + bias, NEG)
                    m = jnp.max(s, axis=1, keepdims=True)
                    pr = jnp.exp(s - m)
                    l = jnp.sum(pr, axis=1, keepdims=True)
                    og[g][rows, :] = jnp.dot(pr.astype(BF16), v2, preferred_element_type=F32) / l
                    lg[g][rows, :] = jnp.broadcast_to(m + jnp.log(l), (W, LANES))

                _for_phases(d, unit)

        def mix(cix, c):
            sl = pl.ds(pl.multiple_of(cix * 256, 256), 256)
            l1, l2, l3 = lg[0][sl, :], lg[1][sl, :], lg[2][sl, :]
            m = jnp.maximum(jnp.maximum(l1, l2), l3)
            e1, e2, e3 = jnp.exp(l1 - m), jnp.exp(l2 - m), jnp.exp(l3 - m)
            tot = e1 + e2 + e3
            o = (e1 * og[0][sl, :] + e2 * og[1][sl, :] + e3 * og[2][sl, :]) / tot
            zz = z_ref[sl, :]
            o_ref[sl, :] = o
            g_ref[sl, :] = (o * (zz * _sigmoid(zz))).astype(BF16)
            l_ref[sl, :] = m + jnp.log(tot)
            return c

        lax.fori_loop(0, rb // 256, mix, 0)

    cur = lambda off, g: pl.BlockSpec((rb, LANES), lambda h, i: (i, off * QKV_BLOCKS + g * DIL_HEADS_PER_GROUP + h))
    prev = lambda off, g: pl.BlockSpec(
        (spans[g], LANES),
        lambda h, i: (jnp.maximum(i * (rb // spans[g]) - 1, 0), off * QKV_BLOCKS + g * DIL_HEADS_PER_GROUP + h))
    row = pl.BlockSpec((rb, LANES), lambda h, i: (i, h))
    groups = range(DIL_GROUPS)
    in_specs = [pl.BlockSpec((1, 8, LANES), lambda h, i: (h, 0, 0))]
    in_specs += [cur(0, g) for g in groups] + [cur(1, g) for g in groups] + [prev(1, g) for g in groups]
    in_specs += [cur(2, g) for g in groups] + [prev(2, g) for g in groups] + [row]
    f32 = jax.ShapeDtypeStruct((S, D_MODEL), F32)
    return pl.pallas_call(
        body, name="dil_attn_fwd", grid=(DIL_HEADS_PER_GROUP, nb), in_specs=in_specs,
        out_specs=[row, row, row], out_shape=[f32, jax.ShapeDtypeStruct((S, D_MODEL), BF16), f32],
        scratch_shapes=[pltpu.VMEM((rb, LANES), F32)] * 6,
        compiler_params=_cparams("parallel", "parallel"),
    )(_slope_table(), *([qkv] * 15), z)


def _dil_bwd_q(qkv, do, lse, dlt, buf, g):
    S = qkv.shape[0]
    rb = min(DIL_ROWS, S)
    nb = S // rb
    d = DIL_PATTERN[g][1]
    span = W * d

    def body(slope_ref, q_ref, kc, kp, vc, vp, do_ref, l_ref, d_ref, buf_ref, dq_ref):
        i = pl.program_id(1)
        _, j_, dist, inwin = _window_geometry((W, 2 * W), W)
        bias = -(slope_ref[0, g:g + 1, 0:1] * float(d)) * dist.astype(F32)
        for b in range(rb // span):
            def unit(r, b=b):
                base = b * span + r
                rows = _phase_rows(base, d)
                if b > 0:
                    kprev, vprev = kc[_phase_rows(base - span, d), :], vc[_phase_rows(base - span, d), :]
                    valid = inwin
                else:
                    kprev, vprev = kp[_phase_rows(r, d), :], vp[_phase_rows(r, d), :]
                    valid = inwin & ((j_ >= W) | (i > 0))
                k2 = jnp.concatenate([kprev, kc[rows, :]], axis=0).astype(BF16)
                v2 = jnp.concatenate([vprev, vc[rows, :]], axis=0).astype(BF16)
                s = lax.dot_general(q_ref[rows, :].astype(BF16), k2, NT, preferred_element_type=F32)
                s = jnp.where(valid, s * DIL_SCALE + bias - l_ref[rows, 0:1], NEG)
                pr = jnp.exp(s)
                dp = lax.dot_general(do_ref[rows, :].astype(BF16), v2, NT, preferred_element_type=F32)
                ds = pr * (dp - d_ref[rows, 0:1])
                dq_ref[rows, :] = jnp.dot(ds.astype(BF16), k2, preferred_element_type=F32) * DIL_SCALE

            _for_phases(d, unit)

    col = lambda off: off * QKV_BLOCKS + g * DIL_HEADS_PER_GROUP
    cur = lambda off: pl.BlockSpec((rb, LANES), lambda h, i: (i, col(off) + h))
    prev = lambda off: pl.BlockSpec((span, LANES), lambda h, i: (jnp.maximum(i * (rb // span) - 1, 0), col(off) + h))
    row = pl.BlockSpec((rb, LANES), lambda h, i: (i, h))
    return pl.pallas_call(
        body, name=f"dil_attn_bwd_q_g{g}", grid=(DIL_HEADS_PER_GROUP, nb),
        in_specs=[pl.BlockSpec((1, 8, LANES), lambda h, i: (h, 0, 0)), cur(0), cur(1), prev(1), cur(2), prev(2),
                  row, row, row, pl.BlockSpec(memory_space=pl.ANY)],
        out_specs=pl.BlockSpec((rb, LANES), lambda h, i: (i, g * DIL_HEADS_PER_GROUP + h)),
        out_shape=jax.ShapeDtypeStruct(buf.shape, buf.dtype), input_output_aliases={9: 0},
        compiler_params=_cparams("parallel", "parallel"),
    )(_slope_table(), qkv, qkv, qkv, qkv, qkv, do, lse, dlt, buf)


def _dil_bwd_kv(qkv, do, lse, dlt, bufk, bufv, g):
    S = qkv.shape[0]
    rb = min(DIL_ROWS, S)
    nb = S // rb
    d = DIL_PATTERN[g][1]
    span = W * d
    nub = rb // span

    def body(slope_ref, k_ref, v_ref, qc, qn, doc, don, lc, ln, dc, dn, bufk_ref, bufv_ref, dk_ref, dv_ref):
        i = pl.program_id(1)
        i_, _, dist, inwin = _window_geometry((2 * W, W), 0)
        bias = -(slope_ref[0, g:g + 1, 0:1] * float(d)) * dist.astype(F32)
        for b in range(nub):
            def unit(r, b=b):
                base = b * span + r
                rows = _phase_rows(base, d)
                if b < nub - 1:
                    nxt = _phase_rows(base + span, d)
                    qnext, donext, lnext, dnext = qc[nxt, :], doc[nxt, :], lc[nxt, 0:1], dc[nxt, 0:1]
                    valid = inwin
                else:
                    nxt = _phase_rows(r, d)
                    qnext, donext, lnext, dnext = qn[nxt, :], don[nxt, :], ln[nxt, 0:1], dn[nxt, 0:1]
                    valid = inwin & ((i_ < W) | (i < nb - 1))
                q2 = jnp.concatenate([qc[rows, :], qnext], axis=0).astype(BF16)
                do2 = jnp.concatenate([doc[rows, :], donext], axis=0).astype(BF16)
                l2 = jnp.concatenate([lc[rows, 0:1], lnext], axis=0)
                d2 = jnp.concatenate([dc[rows, 0:1], dnext], axis=0)
                s = lax.dot_general(q2, k_ref[rows, :].astype(BF16), NT, preferred_element_type=F32)
                s = jnp.where(valid, s * DIL_SCALE + bias - l2, NEG)
                pr = jnp.exp(s)
                dp = lax.dot_general(do2, v_ref[rows, :].astype(BF16), NT, preferred_element_type=F32)
                ds = pr * (dp - d2)
                dv_ref[rows, :] = lax.dot_general(pr.astype(BF16), do2, TN, preferred_element_type=F32)
                dk_ref[rows, :] = lax.dot_general(ds.astype(BF16), q2, TN, preferred_element_type=F32) * DIL_SCALE

            _for_phases(d, unit)

    col = lambda off: off * QKV_BLOCKS + g * DIL_HEADS_PER_GROUP
    cur = lambda off: pl.BlockSpec((rb, LANES), lambda h, i: (i, col(off) + h))
    nxt_blk = lambda i: jnp.minimum((i + 1) * nub, S // span - 1)
    nxt = lambda off: pl.BlockSpec((span, LANES), lambda h, i: (nxt_blk(i), col(off) + h))
    row = pl.BlockSpec((rb, LANES), lambda h, i: (i, h))
    row_nxt = pl.BlockSpec((span, LANES), lambda h, i: (nxt_blk(i), h))
    any_ = pl.BlockSpec(memory_space=pl.ANY)
    out = pl.BlockSpec((rb, LANES), lambda h, i: (i, g * DIL_HEADS_PER_GROUP + h))
    return pl.pallas_call(
        body, name=f"dil_attn_bwd_kv_g{g}", grid=(DIL_HEADS_PER_GROUP, nb),
        in_specs=[pl.BlockSpec((1, 8, LANES), lambda h, i: (h, 0, 0)), cur(1), cur(2), cur(0), nxt(0),
                  row, row_nxt, row, row_nxt, row, row_nxt, any_, any_],
        out_specs=[out, out],
        out_shape=[jax.ShapeDtypeStruct(bufk.shape, bufk.dtype), jax.ShapeDtypeStruct(bufv.shape, bufv.dtype)],
        input_output_aliases={11: 0, 12: 1},
        compiler_params=_cparams("parallel", "parallel"),
    )(_slope_table(), qkv, qkv, qkv, qkv, do, do, lse, lse, dlt, dlt, bufk, bufv)


ANY = pl.BlockSpec(memory_space=pl.ANY)


def _place():
    x, y, c = lax.axis_index("x"), lax.axis_index("y"), lax.axis_index("c")
    chips = [(1 - x, y), (x, 1 - y), (1 - x, 1 - y)]
    return x, y, c, chips


def _gather_weights(wb):
    R = wb.shape[0]
    H = R // 2

    def body(w_ref, out_ref, send1, recv1, send2, recv2):
        x, y, c, chips = _place()
        me = 2 * x + y
        sib = (x, y, 1 - c)
        half = pl.ds(c * H, H)
        first = [pltpu.make_async_remote_copy(
            src_ref=w_ref.at[half], dst_ref=out_ref.at[me, half], send_sem=send1.at[k], recv_sem=recv1.at[k],
            device_id=(*chip, c), device_id_type=MESH) for k, chip in enumerate(chips)]
        for cp in first:
            cp.start()
        passed = []
        for k, (cx, cy) in enumerate(chips):
            slot = out_ref.at[2 * cx + cy, half]
            pltpu.make_async_remote_copy(src_ref=slot, dst_ref=slot, send_sem=send1.at[k], recv_sem=recv1.at[k],
                                         device_id=(cx, cy, c), device_id_type=MESH).wait_recv()
            cp = pltpu.make_async_remote_copy(src_ref=slot, dst_ref=slot, send_sem=send2.at[k], recv_sem=recv2.at[k],
                                              device_id=sib, device_id_type=MESH)
            cp.start()
            passed.append(cp)
        for k, (cx, cy) in enumerate(chips):
            slot = out_ref.at[2 * cx + cy, pl.ds((1 - c) * H, H)]
            pltpu.make_async_remote_copy(src_ref=slot, dst_ref=slot, send_sem=send2.at[k], recv_sem=recv2.at[k],
                                         device_id=sib, device_id_type=MESH).wait_recv()
        for cp in first + passed:
            cp.wait_send()

    return pl.pallas_call(
        body, name="gather_weights", in_specs=[ANY], out_specs=ANY,
        out_shape=jax.ShapeDtypeStruct((4, R, D_MODEL), wb.dtype),
        scratch_shapes=[pltpu.SemaphoreType.DMA((3,)), pltpu.SemaphoreType.DMA((3,)),
                        pltpu.SemaphoreType.DMA((3,)), pltpu.SemaphoreType.DMA((3,))],
    )(wb)


def _swap_halves(g):
    H = g.shape[1] // 2

    def body(g_ref, out_ref, send, recv):
        x, y, c, _ = _place()
        sib = (x, y, 1 - c)
        cps = [pltpu.make_async_remote_copy(
            src_ref=g_ref.at[s, pl.ds((1 - c) * H, H)], dst_ref=out_ref.at[s], send_sem=send.at[s],
            recv_sem=recv.at[s], device_id=sib, device_id_type=MESH) for s in range(4)]
        for cp in cps:
            cp.start()
        for cp in cps:
            cp.wait()

    return pl.pallas_call(
        body, name="swap_halves", in_specs=[ANY], out_specs=ANY,
        out_shape=jax.ShapeDtypeStruct((4, H, D_MODEL), g.dtype),
        scratch_shapes=[pltpu.SemaphoreType.DMA((4,)), pltpu.SemaphoreType.DMA((4,))],
    )(g)


def _scatter_partials(pb):
    def body(p_ref, out_ref, send, recv):
        x, y, c, chips = _place()
        me = 2 * x + y
        cps = [pltpu.make_async_remote_copy(
            src_ref=p_ref.at[2 * cx + cy], dst_ref=out_ref.at[me], send_sem=send.at[k], recv_sem=recv.at[k],
            device_id=(cx, cy, c), device_id_type=MESH) for k, (cx, cy) in enumerate(chips)]
        for cp in cps:
            cp.start()
        for k, (cx, cy) in enumerate(chips):
            slot = out_ref.at[2 * cx + cy]
            pltpu.make_async_remote_copy(src_ref=slot, dst_ref=slot, send_sem=send.at[k], recv_sem=recv.at[k],
                                         device_id=(cx, cy, c), device_id_type=MESH).wait_recv()
        for cp in cps:
            cp.wait_send()

    return pl.pallas_call(
        body, name="scatter_partials", in_specs=[ANY], out_specs=ANY,
        out_shape=jax.ShapeDtypeStruct(pb.shape, pb.dtype),
        scratch_shapes=[pltpu.SemaphoreType.DMA((3,)), pltpu.SemaphoreType.DMA((3,))],
    )(pb)


def _sibling_half(f):
    def body(f_ref, out_ref, send, recv):
        x, y, c, _ = _place()
        cp = pltpu.make_async_remote_copy(src_ref=f_ref, dst_ref=out_ref, send_sem=send, recv_sem=recv,
                                          device_id=(x, y, 1 - c), device_id_type=MESH)
        cp.start()
        cp.wait()

    return pl.pallas_call(
        body, name="sibling_half", in_specs=[ANY], out_specs=ANY,
        out_shape=jax.ShapeDtypeStruct(f.shape, f.dtype),
        scratch_shapes=[pltpu.SemaphoreType.DMA, pltpu.SemaphoreType.DMA],
    )(f)


def _gather_tiles(tile, name):
    m_per = tile.shape[0]

    def body(x_ref, out_ref, send_sems, recv_sems, local_sem):
        x, y, c, chips = _place()
        me, sibling = (x, y, c), (x, y, 1 - c)

        def rows(px, py, pc):
            return out_ref.at[pl.ds((4 * px + 2 * py + pc) * m_per, m_per), :]

        def copy(k, block, to, src=None):
            return pltpu.make_async_remote_copy(
                src_ref=rows(*block) if src is None else src, dst_ref=rows(*block),
                send_sem=send_sems.at[k], recv_sem=recv_sems.at[k], device_id=to, device_id_type=MESH)

        mine = pltpu.make_async_copy(x_ref, rows(*me), local_sem)
        mine.start()
        first = [copy(0, me, sibling, src=x_ref)]
        first += [copy(1 + j, me, (*chip, c), src=x_ref) for j, chip in enumerate(chips)]
        for cp in first:
            cp.start()
        passed = [copy(4 + j, (*chip, c), sibling) for j, chip in enumerate(chips)]
        for j, chip in enumerate(chips):
            copy(1 + j, (*chip, c), me).wait_recv()
            passed[j].start()
        copy(0, sibling, me).wait_recv()
        for j, chip in enumerate(chips):
            copy(4 + j, (*chip, 1 - c), me).wait_recv()
        for cp in first + passed:
            cp.wait_send()
        mine.wait()

    return pl.pallas_call(
        body, name=name, out_shape=jax.ShapeDtypeStruct((8 * m_per, D_MODEL), tile.dtype),
        in_specs=[pl.BlockSpec(memory_space=pltpu.VMEM)], out_specs=pl.BlockSpec(memory_space=pltpu.VMEM),
        scratch_shapes=[pltpu.SemaphoreType.DMA((7,)), pltpu.SemaphoreType.DMA((7,)), pltpu.SemaphoreType.DMA],
    )(tile)


def _pad_rows(a, rows):
    return jnp.pad(a, ((0, rows - a.shape[0]), (0, 0)))


def _pad_row(v):
    v = v.reshape(1, -1)
    return jnp.pad(v, ((0, 0), (0, D_MODEL - v.shape[1])))


def _pack_shard(fwi, fwo, dwi, dwo, pwu, pwg):
    parts = [_pad_rows(fwi.reshape(1028, D_MODEL), 1040), fwo.reshape(256, D_MODEL), dwi.reshape(2560, D_MODEL),
             dwo.reshape(256, D_MODEL), pwu.reshape(128, D_MODEL), pwg.reshape(512, D_MODEL),
             jnp.zeros((FLAT_TOTAL - FLAT_USED, D_MODEL), fwi.dtype)]
    return jnp.concatenate(parts, axis=0)


def _unpack_shard(flat):
    out, r0 = {}, 0
    for name, rows, slot in FLAT_ROWS:
        out[name] = flat[r0:r0 + rows]
        r0 += slot
    return (out["fox_w_in"].reshape(1, D_MODEL, 1028), out["fox_w_out"].reshape(1, 256, D_MODEL),
            out["dil_w_in"].reshape(1, D_MODEL, 2560), out["dil_w_out"].reshape(1, 256, D_MODEL),
            out["ple_w_up"].reshape(2, PLE_DIM, 256), out["ple_w_gate"].reshape(2, 256, D_MODEL))


def _unpack_gathered(wall):
    out, r0 = {}, 0
    for name, rows, slot in FLAT_ROWS:
        out[name] = wall[:, r0:r0 + rows]
        r0 += slot
    cols = lambda a, n: a.reshape(4, D_MODEL, n).transpose(1, 0, 2).reshape(D_MODEL, 4 * n)
    fwi = cols(out["fox_w_in"], 1028)
    dwi = cols(out["dil_w_in"], 2560)
    fwo = out["fox_w_out"].reshape(D_MODEL, D_MODEL)
    dwo = out["dil_w_out"].reshape(D_MODEL, D_MODEL)
    pwu4 = out["ple_w_up"].reshape(4, 2, PLE_DIM, 256)
    pwg4 = out["ple_w_gate"].reshape(4, 2, 256, D_MODEL)
    pwu = [pwu4[:, i].transpose(1, 0, 2).reshape(PLE_DIM, D_MODEL) for i in range(2)]
    pwg = [pwg4[:, i].reshape(D_MODEL, D_MODEL) for i in range(2)]
    return fwi, fwo, dwi, dwo, pwu, pwg


def _pack_grads(gfwi, gfwo, gdwi, gdwo, gpwu, gpwg):
    cols = lambda a, n: a.reshape(D_MODEL, 4, n).transpose(1, 0, 2).reshape(4, n, D_MODEL)
    up = jnp.stack([a.reshape(PLE_DIM, 4, 256).transpose(1, 0, 2) for a in gpwu], axis=1)
    gate = jnp.stack([a.reshape(4, 256, D_MODEL) for a in gpwg], axis=1)
    parts = [jnp.pad(cols(gfwi, 1028), ((0, 0), (0, 12), (0, 0))), gfwo.reshape(4, 256, D_MODEL), cols(gdwi, 2560),
             gdwo.reshape(4, 256, D_MODEL), up.reshape(4, 128, D_MODEL), gate.reshape(4, 512, D_MODEL),
             jnp.zeros((4, FLAT_TOTAL - FLAT_USED, D_MODEL), F32)]
    return jnp.concatenate(parts, axis=1)


def _local_step(x, p0, p1, tgt, fox_g, dil_g, fin_g, b_f, fwi, fwo, dwi, dwo, pwu, pwg):
    S = x.shape[0]
    tm = min(256, S)
    nh = FOX_HEADS // 2
    w_qkv0 = jnp.concatenate([fwi[:, :D_MODEL] * 0.125, fwi[:, D_MODEL:3 * D_MODEL]], axis=1)
    w_z0 = fwi[:, 3 * D_MODEL:4 * D_MODEL]
    swapped = lambda a: jnp.repeat(a.reshape(-1, nh, 2)[:, :, ::-1].reshape(-1, FOX_HEADS), FOX_HEAD_DIM, axis=1)
    w_f0 = swapped(fwi[:, 4 * D_MODEL:])
    b_full = swapped(b_f.reshape(1, FOX_HEADS))
    w_qkv1 = dwi[:, :3 * DIL_QKV]
    w_z1 = dwi[:, 3 * DIL_QKV:]

    n0, = _rows(_rms_fwd, [x], [(D_MODEL, BF16)], name="norm0", tm=tm, bcast=[fox_g])
    qkv0 = _mm(n0, w_qkv0, out_dtype=BF16, name="proj_qkv0")
    z0 = _mm(n0, w_z0, name="proj_z0")
    fl0 = _mm(n0, w_f0, name="proj_f0")
    c_full = _cumsum_rows([fl0], name="forget_cumsum", pre=_log_forget, bcast=[b_full])
    qx0, kx0 = _rows(_fox_extras, [c_full], [(D_MODEL, BF16), (D_MODEL, BF16)], name="fox_extras", tm=tm)
    o0, a0 = _fox_fwd(qkv0, qx0, kx0)
    g0, = _rows(_gate_fwd, [o0, z0], [(D_MODEL, BF16)], name="gate0", tm=tm)
    h1 = _mm(g0, fwo, add=x, name="out_proj0")
    u0 = _mm(p0, pwu[0], name="ple_up0")
    t0 = _mm(h1, pwg[0], name="ple_gate0")
    h2, = _rows(_ple_fwd, [h1, u0, t0], [(D_MODEL, F32)], name="ple_mix0", tm=tm)

    n1, = _rows(_rms_fwd, [h2], [(D_MODEL, BF16)], name="norm1", tm=tm, bcast=[dil_g])
    qkv1 = _mm(n1, w_qkv1, name="proj_qkv1")
    z1 = _mm(n1, w_z1, name="proj_z1")
    o1, g1, lse1 = _dil_fwd(qkv1, z1)
    h3 = _mm(g1, dwo, add=h2, name="out_proj1")
    u1 = _mm(p1, pwu[1], name="ple_up1")
    t1 = _mm(h3, pwg[1], name="ple_gate1")
    h4, = _rows(_ple_fwd, [h3, u1, t1], [(D_MODEL, F32)], name="ple_mix1", tm=tm)

    dh4, g_fin, loss = _rows(_final_stage, [h4, tgt], [(D_MODEL, F32)], name="loss_head", tm=tm, bcast=[fin_g],
                             acc=[((1, D_MODEL), F32), ((1, LANES), F32)])

    du1, dt1 = _rows(_ple_bwd, [dh4, u1, t1], [(D_MODEL, BF16), (D_MODEL, BF16)], name="ple_bwd1", tm=tm)
    g_up1 = _mm(p1, du1, ta=True, name="grad_ple_up1")
    g_gate1 = _mm(h3, dt1, ta=True, name="grad_ple_gate1")
    dh3 = _mm(dt1, pwg[1], tb=True, add=dh4, name="ple_back1")
    dg1 = _mm(dh3, dwo, tb=True, name="out_back1")
    g_dwo = _mm(g1, dh3, ta=True, name="grad_out1")
    do1, dz1, dl1 = _rows(_gate_bwd, [dg1, o1, z1], [(D_MODEL, F32), (D_MODEL, BF16), (D_MODEL, F32)],
                          name="gate_bwd1", tm=tm)
    dq1, dk1, dv1 = (lax.empty((S, DIL_QKV), F32) for _ in range(3))
    for g in range(DIL_GROUPS):
        dq1 = _dil_bwd_q(qkv1, do1, lse1, dl1, dq1, g)
        dk1, dv1 = _dil_bwd_kv(qkv1, do1, lse1, dl1, dk1, dv1, g)
    parts1 = [dq1, dk1, dv1, dz1]
    edges1 = [0, DIL_QKV, 2 * DIL_QKV, 3 * DIL_QKV, 3 * DIL_QKV + D_MODEL]
    g_dwi = jnp.concatenate([_mm(n1, part, ta=True, name=f"grad_in1_{k}") for k, part in enumerate(parts1)], axis=1)
    dn1 = None
    for k, part in enumerate(parts1):
        dn1 = _mm(part, dwi[:, edges1[k]:edges1[k + 1]], tb=True, add=dn1, name=f"in_back1_{k}")
    dh2, g_dil = _rows(_rms_bwd, [dn1, h2, dh3], [(D_MODEL, F32)], name="norm_bwd1", tm=tm, bcast=[dil_g],
                       acc=[((1, D_MODEL), F32)])

    du0, dt0 = _rows(_ple_bwd, [dh2, u0, t0], [(D_MODEL, BF16), (D_MODEL, BF16)], name="ple_bwd0", tm=tm)
    g_up0 = _mm(p0, du0, ta=True, name="grad_ple_up0")
    g_gate0 = _mm(h1, dt0, ta=True, name="grad_ple_gate0")
    dh1 = _mm(dt0, pwg[0], tb=True, add=dh2, name="ple_back0")
    dg0 = _mm(dh1, fwo, tb=True, name="out_back0")
    g_fwo = _mm(g0, dh1, ta=True, name="grad_out0")
    do0, dz0, qxa0, dx0 = _rows(_gate_bwd_fox, [dg0, o0, z0, a0], [(D_MODEL, BF16)] * 4, name="gate_bwd0", tm=tm)
    dq0, drx, dk0, dv0, dkx = _fox_bwd(qkv0, do0, qxa0, dx0, kx0)
    dlogf = _cumsum_rows([drx, dkx], name="forget_cumsum_bwd", reverse=True, pre=_forget_dc)
    df0, g_bf = _rows(_forget_bwd, [dlogf, fl0], [(D_MODEL, BF16)], name="forget_bwd", tm=tm, bcast=[b_full],
                      acc=[((1, D_MODEL), F32)])
    dproj0 = jnp.concatenate([dq0.astype(BF16), dk0, dv0, dz0], axis=1)
    w_main0 = fwi[:, :4 * D_MODEL]
    g_fwi_main = _mm(n0, dproj0, ta=True, name="grad_in0")
    g_fwi_f = _mm(n0, df0, ta=True, name="grad_in0_forget")
    dn0 = _mm(dproj0, w_main0, tb=True, name="in_back0")
    dn0 = _mm(df0, w_f0, tb=True, add=dn0, name="in_back0_forget")
    dx, g_fox = _rows(_rms_bwd, [dn0, x, dh1], [(D_MODEL, F32)], name="norm_bwd0", tm=tm, bcast=[fox_g],
                      acc=[((1, D_MODEL), F32)])
    first_lane = lambda a: a.reshape(-1, nh, 2, FOX_HEAD_DIM)[:, :, ::-1, 0].reshape(-1, FOX_HEADS)
    g_fwi = jnp.concatenate([g_fwi_main, first_lane(g_fwi_f)], axis=1)
    return (loss, dx, (g_fwi, g_fwo, g_dwi, g_dwo, [g_up0, g_up1], [g_gate0, g_gate1]),
            (g_fox, _pad_row(first_lane(g_bf)), g_fin, g_dil))


def kernel(x, p, fox_norm, fox_w_in, fox_b_f, fox_w_out, dil_norm, dil_w_in, dil_w_out, ple_w_up, ple_w_gate, final_norm, loss_target, m_fox_norm, m_fox_w_in, m_fox_b_f, m_fox_w_out, m_dil_norm, m_dil_w_in, m_dil_w_out, m_ple_w_up, m_ple_w_gate, m_final_norm, v_fox_norm, v_fox_w_in, v_fox_b_f, v_fox_w_out, v_dil_norm, v_dil_w_in, v_dil_w_out, v_ple_w_up, v_ple_w_gate, v_final_norm):
    xi, yi, ci = lax.axis_index("x"), lax.axis_index("y"), lax.axis_index("c")
    chip = 2 * xi + yi

    w_flat = _pack_shard(fox_w_in, fox_w_out, dil_w_in, dil_w_out, ple_w_up, ple_w_gate)
    m_flat = _pack_shard(m_fox_w_in, m_fox_w_out, m_dil_w_in, m_dil_w_out, m_ple_w_up, m_ple_w_gate)
    v_flat = _pack_shard(v_fox_w_in, v_fox_w_out, v_dil_w_in, v_dil_w_out, v_ple_w_up, v_ple_w_gate)
    w_bf = w_flat.astype(BF16)
    wall = lax.dynamic_update_slice(_gather_weights(w_bf), w_bf[None], (chip, 0, 0))
    fwi, fwo, dwi, dwo, pwu, pwg = _unpack_gathered(wall)
    gains = _gather_tiles(_pad_rows(_pad_row(dil_norm), 8), "gather_gain").reshape(4, 2, 8, D_MODEL)
    dil_g = gains[:, 0, 0, :256].reshape(1, D_MODEL)

    loss_part, dx, grads, small = _local_step(
        x[0], p[0, 0], p[1, 0], loss_target[0], fox_norm.reshape(1, D_MODEL), dil_g,
        final_norm.reshape(1, D_MODEL), fox_b_f, fwi, fwo, dwi, dwo, pwu, pwg)
    g_fox, g_bf, g_fin, g_dil = small

    g_all = _pack_grads(*grads)
    theirs = _swap_halves(g_all)
    mine = lax.dynamic_slice_in_dim(g_all, ci * FLAT_HALF, FLAT_HALF, axis=1)
    part, = _rows(lambda a, b: a + b, [mine.reshape(4 * FLAT_HALF, D_MODEL), theirs.reshape(4 * FLAT_HALF, D_MODEL)],
                  [(D_MODEL, BF16)], name="pair_sum", tm=FLAT_TILE)
    part = part.reshape(4, FLAT_HALF, D_MODEL)
    own = lax.dynamic_slice_in_dim(part, chip, 1, axis=0)
    by_chip = lax.dynamic_update_slice(_scatter_partials(part), own, (chip, 0, 0))
    half_sum, = _rows(lambda a, b, c, d: ((a.astype(F32) + b.astype(F32)) + c.astype(F32)) + d.astype(F32),
                      [by_chip[s] for s in range(4)], [(D_MODEL, F32)], name="chip_sum", tm=FLAT_TILE)
    other_half = _sibling_half(half_sum)
    g_flat = jnp.where(ci == 0, jnp.concatenate([half_sum, other_half], axis=0),
                       jnp.concatenate([other_half, half_sum], axis=0))

    tile = jnp.concatenate([g_fox, g_bf, g_fin, g_dil, jnp.pad(loss_part, ((0, 0), (0, D_MODEL - LANES))),
                            jnp.zeros((3, D_MODEL), F32)], axis=0)
    tiles = _gather_tiles(tile, "gather_small")
    tot, = _rows(lambda *t: functools.reduce(lambda a, b: a + b, t), [tiles[8 * s:8 * s + 8] for s in range(8)],
                 [(D_MODEL, F32)], name="small_sum", tm=8)
    loss = tot[4, 0]
    g_small = jnp.concatenate([
        tot[0:3],
        _pad_row(lax.dynamic_slice_in_dim(tot[3], chip * 256, 256)), jnp.zeros((4, D_MODEL), F32)], axis=0)
    small_tile = lambda a, b, c, d: jnp.concatenate(
        [_pad_row(a), _pad_row(b), _pad_row(c), _pad_row(d), jnp.zeros((4, D_MODEL), F32)], axis=0)
    w_small = small_tile(fox_norm, fox_b_f, final_norm, dil_norm)
    m_small = small_tile(m_fox_norm, m_fox_b_f, m_final_norm, m_dil_norm)
    v_small = small_tile(v_fox_norm, v_fox_b_f, v_final_norm, v_dil_norm)

    three = [(D_MODEL, F32)] * 3
    d_flat, nm_flat, nv_flat = _rows(_adamw, [w_flat, g_flat, m_flat, v_flat], three, name="adamw", tm=FLAT_TILE)
    d_small, nm_small, nv_small = _rows(_adamw, [w_small, g_small, m_small, v_small], three, name="adamw_small", tm=8)

    def leaves(flat, small_rows):
        fwi_, fwo_, dwi_, dwo_, pwu_, pwg_ = _unpack_shard(flat)
        return (small_rows[0:1], fwi_, small_rows[1:2, :FOX_HEADS], fwo_, small_rows[3:4, :256], dwi_, dwo_,
                pwu_, pwg_, small_rows[2])

    return (loss, dx[None], *leaves(g_flat, g_small), *leaves(d_flat, d_small), *leaves(nm_flat, nm_small),
            *leaves(nv_flat, nv_small))
```

```python
import functools

import numpy as np
import jax
import jax.numpy as jnp
from jax import lax
from jax.experimental import pallas as pl
from jax.experimental.pallas import tpu as pltpu

F32 = jnp.float32
BF16 = jnp.bfloat16

D_MODEL = 1024
PLE_DIM = 256
FOX_HEADS = 16
FOX_HEAD_DIM = 64
DIL_PATTERN = ((128, 1), (512, 4), (2048, 16))
DIL_GROUPS = 3
DIL_HEADS_PER_GROUP = 8
DIL_HEAD_DIM = 128
DIL_WINDOW_STEPS = 128
DIL_QKV = 3072
ALIBI_MAX_EXP = 8.0
RMS_EPS = 1e-6
ADAM_LR, ADAM_B1, ADAM_B2, ADAM_EPS, ADAM_WD, ADAM_STEP = 0.001, 0.9, 0.999, 1e-08, 0.01, 10

LANES = 128
VMEM_LIMIT = 56 * 1024 * 1024
MESH = pl.DeviceIdType.MESH
NEG = -1e30

FLAT_ROWS = (("fox_w_in", 1028, 1040), ("fox_w_out", 256, 256), ("dil_w_in", 2560, 2560), ("dil_w_out", 256, 256),
             ("ple_w_up", 128, 128), ("ple_w_gate", 512, 512))
FLAT_USED = sum(r for _, _, r in FLAT_ROWS)
FLAT_TOTAL = 4864
FLAT_HALF = FLAT_TOTAL // 2
FLAT_TILE = 608


def _cparams(*sem):
    return pltpu.CompilerParams(dimension_semantics=sem, vmem_limit_bytes=VMEM_LIMIT)


def _sigmoid(x):
    return 1.0 / (1.0 + jnp.exp(-x))


def _mm(a, b, *, name, ta=False, tb=False, out_dtype=F32, add=None):
    if ta:
        K, M = a.shape
    else:
        M, K = a.shape
    if tb:
        N, Kb = b.shape
    else:
        Kb, N = b.shape
    assert K == Kb, (a.shape, b.shape)
    tm, tn, tk = min(M, 1024), min(N, 1024), min(K, 1024)
    assert M % tm == 0 and N % tn == 0 and K % tk == 0, (M, N, K)
    nk = K // tk
    dn = (((0 if ta else 1,), (1 if tb else 0,)), ((), ()))

    def body(*refs):
        if add is None:
            a_ref, b_ref, o_ref, acc = refs
        else:
            a_ref, b_ref, add_ref, o_ref, acc = refs
        k = pl.program_id(2)

        @pl.when(k == 0)
        def _():
            acc[...] = jnp.zeros_like(acc)

        acc[...] += lax.dot_general(a_ref[...].astype(BF16), b_ref[...].astype(BF16), dn,
                                    preferred_element_type=F32)

        @pl.when(k == nk - 1)
        def _():
            r = acc[...]
            if add is not None:
                r = r + add_ref[...]
            o_ref[...] = r.astype(out_dtype)

    a_spec = (pl.BlockSpec((tk, tm), lambda i, j, k: (k, i)) if ta
              else pl.BlockSpec((tm, tk), lambda i, j, k: (i, k)))
    b_spec = (pl.BlockSpec((tn, tk), lambda i, j, k: (j, k)) if tb
              else pl.BlockSpec((tk, tn), lambda i, j, k: (k, j)))
    in_specs = [a_spec, b_spec]
    args = [a, b]
    if add is not None:
        in_specs.append(pl.BlockSpec((tm, tn), lambda i, j, k: (i, j)))
        args.append(add)
    return pl.pallas_call(
        body, name=name, grid=(M // tm, N // tn, nk),
        in_specs=in_specs, out_specs=pl.BlockSpec((tm, tn), lambda i, j, k: (i, j)),
        out_shape=jax.ShapeDtypeStruct((M, N), out_dtype),
        scratch_shapes=[pltpu.VMEM((tm, tn), F32)],
        compiler_params=_cparams("parallel", "parallel", "arbitrary"),
    )(*args)


def _mm_nt_parts(a_parts, b, *, name):
    M = a_parts[0].shape[0]
    N, K = b.shape
    tm, tn, tk = min(M, 1024), min(N, 1024), 1024
    nks = [a.shape[1] // tk for a in a_parts]
    offs = [sum(nks[:p]) for p in range(len(nks))]
    nk = sum(nks)
    assert nk * tk == K and M % tm == 0 and N % tn == 0, (M, N, K)
    n_parts = len(a_parts)

    def body(*refs):
        a_refs, b_ref, o_ref, acc = refs[:n_parts], refs[n_parts], refs[n_parts + 1], refs[n_parts + 2]
        k = pl.program_id(2)

        @pl.when(k == 0)
        def _():
            acc[...] = jnp.zeros_like(acc)

        for a_ref, off, n in zip(a_refs, offs, nks):
            @pl.when((k >= off) & (k < off + n))
            def _(a_ref=a_ref):
                acc[...] += lax.dot_general(a_ref[...].astype(BF16), b_ref[...].astype(BF16),
                                            (((1,), (1,)), ((), ())), preferred_element_type=F32)

        @pl.when(k == nk - 1)
        def _():
            o_ref[...] = acc[...]

    a_specs = [pl.BlockSpec((tm, tk), lambda i, j, k, off=off, n=n: (i, jnp.clip(k - off, 0, n - 1)))
               for off, n in zip(offs, nks)]
    return pl.pallas_call(
        body, name=name, grid=(M // tm, N // tn, nk),
        in_specs=a_specs + [pl.BlockSpec((tn, tk), lambda i, j, k: (j, k))],
        out_specs=pl.BlockSpec((tm, tn), lambda i, j, k: (i, j)),
        out_shape=jax.ShapeDtypeStruct((M, N), F32), scratch_shapes=[pltpu.VMEM((tm, tn), F32)],
        compiler_params=_cparams("parallel", "parallel", "arbitrary"),
    )(*a_parts, b)


def _mm_tn_parts(a, b_parts, *, name):
    K, M = a.shape
    tm, tn, tk = min(M, 1024), 1024, min(K, 1024)
    njs = [b.shape[1] // tn for b in b_parts]
    offs = [sum(njs[:p]) for p in range(len(njs))]
    nj, nk = sum(njs), K // tk
    assert M % tm == 0 and K % tk == 0 and all(b.shape[1] % tn == 0 for b in b_parts)
    n_parts = len(b_parts)

    def body(*refs):
        a_ref, b_refs, o_ref, acc = refs[0], refs[1:1 + n_parts], refs[1 + n_parts], refs[2 + n_parts]
        j, k = pl.program_id(1), pl.program_id(2)

        @pl.when(k == 0)
        def _():
            acc[...] = jnp.zeros_like(acc)

        for b_ref, off, n in zip(b_refs, offs, njs):
            @pl.when((j >= off) & (j < off + n))
            def _(b_ref=b_ref):
                acc[...] += lax.dot_general(a_ref[...].astype(BF16), b_ref[...].astype(BF16),
                                            (((0,), (0,)), ((), ())), preferred_element_type=F32)

        @pl.when(k == nk - 1)
        def _():
            o_ref[...] = acc[...]

    def b_spec(off, n):
        def index(i, j, k):
            mine = (j >= off) & (j < off + n)
            return jnp.where(mine, k, 0), jnp.clip(j - off, 0, n - 1)
        return pl.BlockSpec((tk, tn), index)

    return pl.pallas_call(
        body, name=name, grid=(M // tm, nj, nk),
        in_specs=[pl.BlockSpec((tk, tm), lambda i, j, k: (k, i))] + [b_spec(off, n) for off, n in zip(offs, njs)],
        out_specs=pl.BlockSpec((tm, tn), lambda i, j, k: (i, j)),
        out_shape=jax.ShapeDtypeStruct((M, nj * tn), F32), scratch_shapes=[pltpu.VMEM((tm, tn), F32)],
        compiler_params=_cparams("parallel", "parallel", "arbitrary"),
    )(a, *b_parts)


def _rows(fn, ins, outs, *, name, tm, bcast=(), acc=()):
    R = ins[0].shape[0]
    assert R % tm == 0, (R, tm)
    n_in, n_b, n_out, n_acc = len(ins), len(bcast), len(outs), len(acc)

    def body(*refs):
        in_refs = refs[:n_in + n_b]
        out_refs = refs[n_in + n_b:n_in + n_b + n_out]
        acc_refs = refs[n_in + n_b + n_out:]
        res = fn(*[r[...] for r in in_refs])
        if not isinstance(res, (tuple, list)):
            res = (res,)
        for r, v in zip(out_refs, res[:n_out]):
            r[...] = v.astype(r.dtype)
        first = pl.program_id(0) == 0
        for r, v in zip(acc_refs, res[n_out:]):
            @pl.when(first)
            def _(r=r, v=v):
                r[...] = v.astype(r.dtype)

            @pl.when(jnp.logical_not(first))
            def _(r=r, v=v):
                r[...] += v.astype(r.dtype)

    in_specs = [pl.BlockSpec((tm, a.shape[1]), lambda i: (i, 0)) for a in ins]
    in_specs += [pl.BlockSpec(b.shape, lambda i, nd=b.ndim: (0,) * nd) for b in bcast]
    out_specs = [pl.BlockSpec((tm, c), lambda i: (i, 0)) for c, _ in outs]
    out_specs += [pl.BlockSpec(s, lambda i, nd=len(s): (0,) * nd) for s, _ in acc]
    out_shape = [jax.ShapeDtypeStruct((R, c), dt) for c, dt in outs]
    out_shape += [jax.ShapeDtypeStruct(s, dt) for s, dt in acc]
    res = pl.pallas_call(
        body, name=name, grid=(R // tm,), in_specs=in_specs, out_specs=out_specs, out_shape=out_shape,
        compiler_params=_cparams("arbitrary" if acc else "parallel"),
    )(*ins, *bcast)
    return res


def _cumsum_rows(ins, *, name, reverse=False, pre=None, bcast=()):
    S, C = ins[0].shape
    tb = 256
    nb = S // tb
    assert S % tb == 0
    n_in = len(ins) + len(bcast)

    def body(*refs):
        in_refs, o_ref, carry = refs[:n_in], refs[n_in], refs[n_in + 1]

        @pl.when(pl.program_id(0) == 0)
        def _():
            carry[...] = jnp.zeros_like(carry)

        xv = in_refs[0][...] if pre is None else pre(*[r[...] for r in in_refs])
        r_ = lax.broadcasted_iota(jnp.int32, (tb, tb), 0)
        c_ = lax.broadcasted_iota(jnp.int32, (tb, tb), 1)
        tri = jnp.where((c_ >= r_) if reverse else (c_ <= r_), 1.0, 0.0).astype(BF16)
        hi = xv.astype(BF16)
        r1 = xv - hi.astype(F32)
        mid = r1.astype(BF16)
        lo = (r1 - mid.astype(F32)).astype(BF16)
        cs = (jnp.dot(tri, hi, preferred_element_type=F32) + jnp.dot(tri, mid, preferred_element_type=F32)
              + jnp.dot(tri, lo, preferred_element_type=F32)) + carry[...]
        o_ref[...] = cs
        carry[...] = cs[0:1, :] if reverse else cs[tb - 1:tb, :]

    blk = (lambda i: (nb - 1 - i, 0)) if reverse else (lambda i: (i, 0))
    in_specs = [pl.BlockSpec((tb, C), blk) for _ in ins]
    in_specs += [pl.BlockSpec(b.shape, lambda i, nd=b.ndim: (0,) * nd) for b in bcast]
    return pl.pallas_call(
        body, name=name, grid=(nb,), in_specs=in_specs, out_specs=pl.BlockSpec((tb, C), blk),
        out_shape=jax.ShapeDtypeStruct((S, C), F32), scratch_shapes=[pltpu.VMEM((1, C), F32)],
        compiler_params=_cparams("arbitrary"),
    )(*ins, *bcast)


def _rms_fwd(x, g):
    r = lax.rsqrt(jnp.mean(x * x, axis=1, keepdims=True) + RMS_EPS)
    return (x * r) * g


def _rms_bwd(dn, x, dres, g):
    r = lax.rsqrt(jnp.mean(x * x, axis=1, keepdims=True) + RMS_EPS)
    xh = x * r
    w = dn * g
    dx = r * (w - xh * jnp.mean(w * xh, axis=1, keepdims=True))
    return dres + dx, jnp.sum(dn * xh, axis=0, keepdims=True)


def _final_stage(h, tgt, g):
    r = lax.rsqrt(jnp.mean(h * h, axis=1, keepdims=True) + RMS_EPS)
    xh = h * r
    diff = xh * g - tgt
    loss = 0.5 * jnp.sum(jnp.mean(diff * diff, axis=1, keepdims=True), axis=0, keepdims=True)
    dy = diff * (1.0 / D_MODEL)
    w = dy * g
    dh = r * (w - xh * jnp.mean(w * xh, axis=1, keepdims=True))
    return dh, jnp.sum(dy * xh, axis=0, keepdims=True), jnp.broadcast_to(loss, (1, LANES))


def _ple_fwd(h, u, t):
    return h + u * _sigmoid(t)


def _ple_bwd(dh, u, t):
    s = _sigmoid(t)
    return dh * s, dh * u * s * (1.0 - s)


def _gate_fwd(o, z):
    return o * (z * _sigmoid(z))


def _head_sums(prod):
    tm, width = prod.shape
    cols = [jnp.broadcast_to(jnp.sum(prod[:, b * LANES:(b + 1) * LANES], axis=1, keepdims=True), (tm, LANES))
            for b in range(width // LANES)]
    return jnp.concatenate(cols, axis=1)


def _gate_bwd(dg, o, z):
    s = _sigmoid(z)
    do = dg * (z * s)
    dz = dg * o * (s * (1.0 + z * (1.0 - s)))
    return do, dz, _head_sums(do * o)


def _log_forget(fl, b):
    u = fl + b
    return jnp.minimum(u, 0.0) - jnp.log(1.0 + jnp.exp(-jnp.abs(u)))


def _adamw(w, g, m, v):
    m = ADAM_B1 * m + (1.0 - ADAM_B1) * g
    v = ADAM_B2 * v + (1.0 - ADAM_B2) * (g * g)
    m_hat = m / (1.0 - ADAM_B1 ** ADAM_STEP)
    v_hat = v / (1.0 - ADAM_B2 ** ADAM_STEP)
    delta = -ADAM_LR * (m_hat / (jnp.sqrt(v_hat) + ADAM_EPS) + ADAM_WD * w)
    return delta, m, v


FOX_TQ = 1024


def _split3(x):
    p1 = x.astype(BF16).astype(F32)
    r = x - p1
    p2 = r.astype(BF16).astype(F32)
    return p1, p2, r - p2


def _lane_in_head(shape):
    return lax.broadcasted_iota(jnp.int32, shape, 1) % FOX_HEAD_DIM


def _query_extras(x):
    lm = _lane_in_head(x.shape)
    p1, p2, p3 = _split3(x)
    return jnp.where(lm == 0, p1, jnp.where(lm == 1, p2, jnp.where(lm == 2, p3, jnp.where(lm < 6, 1.0, 0.0))))


def _key_extras(c):
    lm = _lane_in_head(c.shape)
    p1, p2, p3 = _split3(c)
    return jnp.where(lm < 3, 1.0, jnp.where(lm == 3, -p1, jnp.where(lm == 4, -p2, jnp.where(lm == 5, -p3, 0.0))))


def _fox_extras(c):
    return _query_extras(c), _key_extras(c)


def _swapped_head_sums(prod):
    tm, width = prod.shape
    lane = lax.broadcasted_iota(jnp.int32, (tm, LANES), 1)
    low = lane < FOX_HEAD_DIM
    cols = []
    for b in range(width // LANES):
        blk = prod[:, b * LANES:(b + 1) * LANES]
        sa = jnp.sum(jnp.where(low, blk, 0.0), axis=1, keepdims=True)
        sb = jnp.sum(jnp.where(low, 0.0, blk), axis=1, keepdims=True)
        cols.append(jnp.where(low, sb, sa))
    return jnp.concatenate(cols, axis=1)


def _gate_bwd_fox(dg, o, z, a):
    s = _sigmoid(z)
    do = dg * (z * s)
    dz = dg * o * (s * (1.0 + z * (1.0 - s)))
    lm = _lane_in_head(do.shape)
    d1, d2, d3 = _split3(-_swapped_head_sums(do * o))
    dx = jnp.where(lm == 0, d1, jnp.where(lm == 1, d2, jnp.where(lm == 2, d3, 0.0)))
    return do, dz, _query_extras(a), dx


def _forget_dc(drx, dkx):
    lm = _lane_in_head(drx.shape)
    return jnp.where(lm == 0, drx, 0.0) - jnp.where(lm == 3, dkx, 0.0)


def _forget_bwd(dl, fl, b):
    lm = _lane_in_head(dl.shape)
    width = dl.shape[1]
    both = dl + pltpu.roll(dl, 3, 1) + pltpu.roll(dl, width - 3, 1)
    du = jnp.where(lm == 0, both, 0.0) * (1.0 / (1.0 + jnp.exp(fl + b)))
    return du, jnp.sum(du, axis=0, keepdims=True)


def _fox_fwd(qkv, qx, kx):
    S = qkv.shape[0]
    tq = min(FOX_TQ, S)
    nq = S // tq
    nt = (((1,), (1,)), ((), ()))

    def body(q_ref, qx_ref, k_ref, v_ref, kx_ref, o_ref, a_ref):
        i = pl.program_id(1)
        low = lax.broadcasted_iota(jnp.int32, (tq, LANES), 1) < FOX_HEAD_DIM
        row = lax.broadcasted_iota(jnp.int32, (tq, tq), 0)
        col = lax.broadcasted_iota(jnp.int32, (tq, tq), 1)
        q2, x2 = q_ref[...], qx_ref[...]
        qa = (jnp.where(low, q2, x2), jnp.where(low, x2, q2))

        def step(kb, carry, diag):
            start = pl.multiple_of(kb * tq, tq)
            k2 = k_ref[pl.ds(start, tq), :]
            v2 = v_ref[pl.ds(start, tq), :]
            y2 = kx_ref[pl.ds(start, tq), :]
            one = jnp.ones_like(v2)
            ka = (jnp.where(low, k2, y2), jnp.where(low, y2, k2))
            va = (jnp.where(low, v2, one), jnp.where(low, one, v2))
            out = []
            for hh in range(2):
                m, acc = carry[hh]
                s = lax.dot_general(qa[hh], ka[hh], nt, preferred_element_type=F32)
                if diag:
                    s = jnp.where(col <= row, s, NEG)
                m_new = jnp.maximum(m, jnp.max(s, axis=1, keepdims=True))
                pr = jnp.exp(s - m_new)
                acc = jnp.exp(m - m_new) * acc + jnp.dot(pr.astype(BF16), va[hh], preferred_element_type=F32)
                out.append((m_new, acc))
            return tuple(out)

        init = ((jnp.full((tq, 1), NEG, F32), jnp.zeros((tq, LANES), F32)),) * 2
        carry = lax.fori_loop(0, i, functools.partial(step, diag=False), init)
        (m_a, acc_a), (m_b, acc_b) = step(i, carry, True)
        l_a, l_b = acc_a[:, FOX_HEAD_DIM:FOX_HEAD_DIM + 1], acc_b[:, 0:1]
        xf = x2.astype(F32)
        c_a = xf[:, 64:65] + xf[:, 65:66] + xf[:, 66:67]
        c_b = xf[:, 0:1] + xf[:, 1:2] + xf[:, 2:3]
        o_ref[...] = jnp.where(low, acc_a / l_a, acc_b / l_b)
        a_ref[...] = jnp.where(low, jnp.broadcast_to(c_b - (m_b + jnp.log(l_b)), (tq, LANES)),
                               jnp.broadcast_to(c_a - (m_a + jnp.log(l_a)), (tq, LANES)))

    blk = lambda cb: pl.BlockSpec((tq, LANES), lambda h, i, cb=cb: (i, cb + h))
    res = lambda cb: pl.BlockSpec((S, LANES), lambda h, i, cb=cb: (0, cb + h))
    return pl.pallas_call(
        body, name="fox_attn_fwd", grid=(FOX_HEADS // 2, nq),
        in_specs=[blk(0), blk(0), res(8), res(16), res(0)],
        out_specs=[blk(0), blk(0)],
        out_shape=[jax.ShapeDtypeStruct((S, D_MODEL), F32), jax.ShapeDtypeStruct((S, D_MODEL), F32)],
        compiler_params=_cparams("parallel", "arbitrary"),
    )(qkv, qx, qkv, qkv, kx)


def _fox_bwd(qkv, do, qxa, dx, kx):
    S = qkv.shape[0]
    tq = min(FOX_TQ, S)
    nq = S // tq
    nt = (((1,), (1,)), ((), ()))
    tn = (((0,), (0,)), ((), ()))

    def body(q_ref, qx_ref, do_ref, dx_ref, k_ref, v_ref, kx_ref, dq_ref, dr_ref, dk_ref, dv_ref, dkx_ref):
        kb = pl.program_id(1)
        low = lax.broadcasted_iota(jnp.int32, (tq, LANES), 1) < FOX_HEAD_DIM
        row = lax.broadcasted_iota(jnp.int32, (tq, tq), 0)
        col = lax.broadcasted_iota(jnp.int32, (tq, tq), 1)

        @pl.when(kb == 0)
        def _():
            dq_ref[...] = jnp.zeros_like(dq_ref)
            dr_ref[...] = jnp.zeros_like(dr_ref)

        k2, v2, y2 = k_ref[...], v_ref[...], kx_ref[...]
        one = jnp.ones_like(v2)
        ka = (jnp.where(low, k2, y2), jnp.where(low, y2, k2))
        va = (jnp.where(low, v2, one), jnp.where(low, one, v2))

        def step(qb, carry, diag):
            start = pl.multiple_of(qb * tq, tq)
            q2 = q_ref[pl.ds(start, tq), :]
            x2 = qx_ref[pl.ds(start, tq), :]
            d2 = do_ref[pl.ds(start, tq), :]
            e2 = dx_ref[pl.ds(start, tq), :]
            qa = (jnp.where(low, q2, x2), jnp.where(low, x2, q2))
            da = (jnp.where(low, d2, e2), jnp.where(low, e2, d2))
            new, res = [], []
            for hh in range(2):
                dk, dv = carry[hh]
                s = lax.dot_general(qa[hh], ka[hh], nt, preferred_element_type=F32)
                if diag:
                    s = jnp.where(col <= row, s, NEG)
                pr = jnp.exp(s)
                ds = pr * lax.dot_general(da[hh], va[hh], nt, preferred_element_type=F32)
                prb, dsb = pr.astype(BF16), ds.astype(BF16)
                dv = dv + lax.dot_general(prb, da[hh], tn, preferred_element_type=F32)
                dk = dk + lax.dot_general(dsb, qa[hh], tn, preferred_element_type=F32)
                res.append(jnp.dot(dsb, ka[hh], preferred_element_type=F32))
                new.append((dk, dv))
            dq_ref[pl.ds(start, tq), :] += jnp.where(low, res[0], res[1])
            dr_ref[pl.ds(start, tq), :] += jnp.where(low, res[1], res[0])
            return tuple(new)

        init = ((jnp.zeros((tq, LANES), F32), jnp.zeros((tq, LANES), F32)),) * 2
        carry = step(kb, init, True)
        (dk_a, dv_a), (dk_b, dv_b) = lax.fori_loop(kb + 1, nq, functools.partial(step, diag=False), carry)
        dk_ref[...] = jnp.where(low, dk_a, dk_b).astype(BF16)
        dv_ref[...] = jnp.where(low, dv_a, dv_b).astype(BF16)
        dkx_ref[...] = jnp.where(low, dk_b, dk_a)

        @pl.when(kb == nq - 1)
        def _():
            dq_ref[...] = dq_ref[...] * (FOX_HEAD_DIM ** -0.5)

    res = lambda cb: pl.BlockSpec((S, LANES), lambda h, k, cb=cb: (0, cb + h))
    blk = lambda cb: pl.BlockSpec((tq, LANES), lambda h, k, cb=cb: (k, cb + h))
    f32, b16 = jax.ShapeDtypeStruct((S, D_MODEL), F32), jax.ShapeDtypeStruct((S, D_MODEL), BF16)
    return pl.pallas_call(
        body, name="fox_attn_bwd", grid=(FOX_HEADS // 2, nq),
        in_specs=[res(0), res(0), res(0), res(0), blk(8), blk(16), blk(0)],
        out_specs=[res(0), res(0), blk(0), blk(0), blk(0)],
        out_shape=[f32, f32, b16, b16, f32],
        compiler_params=_cparams("parallel", "arbitrary"),
    )(qkv, qxa, do, dx, qkv, qkv, kx)


def _alibi_slopes():
    n = DIL_GROUPS * DIL_HEADS_PER_GROUP
    s = np.float32(2.0) ** (np.float32(-ALIBI_MAX_EXP) * np.arange(1, n + 1, dtype=np.float32) / np.float32(n))
    return s.astype(np.float32).reshape(DIL_GROUPS, DIL_HEADS_PER_GROUP)


W = DIL_WINDOW_STEPS
DIL_SCALE = DIL_HEAD_DIM ** -0.5


DIL_ROWS = 2048
NT = (((1,), (1,)), ((), ()))
TN = (((0,), (0,)), ((), ()))
QKV_BLOCKS = DIL_QKV // LANES


def _slope_table():
    t = np.zeros((DIL_HEADS_PER_GROUP, 8, LANES), np.float32)
    t[:, :DIL_GROUPS, :] = _alibi_slopes().T[:, :, None]
    return jnp.asarray(t)


def _phase_rows(start, d):
    return pl.ds(start, W, stride=d) if d > 1 else pl.ds(start, W)


def _for_phases(d, unit):
    if d == 1:
        unit(0)
    else:
        lax.fori_loop(0, d, lambda r, c: (unit(r), c)[1], 0)


def _window_geometry(shape, q_axis_offset):
    i_ = lax.broadcasted_iota(jnp.int32, shape, 0)
    j_ = lax.broadcasted_iota(jnp.int32, shape, 1)
    dist = q_axis_offset + i_ - j_
    return i_, j_, dist, (dist >= 0) & (dist <= W)


def _dil_fwd(qkv, z):
    S = qkv.shape[0]
    rb = min(DIL_ROWS, S)
    nb = S // rb
    spans = [W * d for _, d in DIL_PATTERN]

    def body(*refs):
        slope_ref = refs[0]
        q, kc, kp, vc, vp = refs[1:4], refs[4:7], refs[7:10], refs[10:13], refs[13:16]
        z_ref, o_ref, g_ref, l_ref = refs[16:20]
        og, lg = refs[20:23], refs[23:26]
        i = pl.program_id(1)
        _, j_, dist, inwin = _window_geometry((W, 2 * W), W)
        distf = dist.astype(F32)
        for g, (_, d) in enumerate(DIL_PATTERN):
            span = spans[g]
            bias = -(slope_ref[0, g:g + 1, 0:1] * float(d)) * distf
            for b in range(rb // span):
                def unit(r, g=g, d=d, span=span, b=b, bias=bias):
                    base = b * span + r
                    rows = _phase_rows(base, d)
                    if b > 0:
                        kprev, vprev = kc[g][_phase_rows(base - span, d), :], vc[g][_phase_rows(base - span, d), :]
                        valid = inwin
                    else:
                        kprev, vprev = kp[g][_phase_rows(r, d), :], vp[g][_phase_rows(r, d), :]
                        valid = inwin & ((j_ >= W) | (i > 0))
                    k2 = jnp.concatenate([kprev, kc[g][rows, :]], axis=0).astype(BF16)
                    v2 = jnp.concatenate([vprev, vc[g][rows, :]], axis=0).astype(BF16)
                    s = lax.dot_general(q[g][rows, :].astype(BF16), k2, NT, preferred_element_type=F32)
                    s = jnp.where(valid, s * DIL_SCALE + bias, NEG)
                    m = jnp.max(s, axis=1, keepdims=True)
                    pr = jnp.exp(s - m)
                    l = jnp.sum(pr, axis=1, keepdims=True)
                    og[g][rows, :] = jnp.dot(pr.astype(BF16), v2, preferred_element_type=F32) / l
                    lg[g][rows, :] = jnp.broadcast_to(m + jnp.log(l), (W, LANES))

                _for_phases(d, unit)

        def mix(cix, c):
            sl = pl.ds(pl.multiple_of(cix * 256, 256), 256)
            l1, l2, l3 = lg[0][sl, :], lg[1][sl, :], lg[2][sl, :]
            m = jnp.maximum(jnp.maximum(l1, l2), l3)
            e1, e2, e3 = jnp.exp(l1 - m), jnp.exp(l2 - m), jnp.exp(l3 - m)
            tot = e1 + e2 + e3
            o = (e1 * og[0][sl, :] + e2 * og[1][sl, :] + e3 * og[2][sl, :]) / tot
            zz = z_ref[sl, :]
            o_ref[sl, :] = o
            g_ref[sl, :] = (o * (zz * _sigmoid(zz))).astype(BF16)
            l_ref[sl, :] = m + jnp.log(tot)
            return c

        lax.fori_loop(0, rb // 256, mix, 0)

    cur = lambda off, g: pl.BlockSpec((rb, LANES), lambda h, i: (i, off * QKV_BLOCKS + g * DIL_HEADS_PER_GROUP + h))
    prev = lambda off, g: pl.BlockSpec(
        (spans[g], LANES),
        lambda h, i: (jnp.maximum(i * (rb // spans[g]) - 1, 0), off * QKV_BLOCKS + g * DIL_HEADS_PER_GROUP + h))
    row = pl.BlockSpec((rb, LANES), lambda h, i: (i, h))
    groups = range(DIL_GROUPS)
    in_specs = [pl.BlockSpec((1, 8, LANES), lambda h, i: (h, 0, 0))]
    in_specs += [cur(0, g) for g in groups] + [cur(1, g) for g in groups] + [prev(1, g) for g in groups]
    in_specs += [cur(2, g) for g in groups] + [prev(2, g) for g in groups] + [row]
    f32 = jax.ShapeDtypeStruct((S, D_MODEL), F32)
    return pl.pallas_call(
        body, name="dil_attn_fwd", grid=(DIL_HEADS_PER_GROUP, nb), in_specs=in_specs,
        out_specs=[row, row, row], out_shape=[f32, jax.ShapeDtypeStruct((S, D_MODEL), BF16), f32],
        scratch_shapes=[pltpu.VMEM((rb, LANES), F32)] * 6,
        compiler_params=_cparams("parallel", "parallel"),
    )(_slope_table(), *([qkv] * 15), z)


def _dil_bwd_q(qkv, do, lse, dlt, buf, g):
    S = qkv.shape[0]
    rb = min(DIL_ROWS, S)
    nb = S // rb
    d = DIL_PATTERN[g][1]
    span = W * d

    def body(slope_ref, q_ref, kc, kp, vc, vp, do_ref, l_ref, d_ref, buf_ref, dq_ref):
        i = pl.program_id(1)
        _, j_, dist, inwin = _window_geometry((W, 2 * W), W)
        bias = -(slope_ref[0, g:g + 1, 0:1] * float(d)) * dist.astype(F32)
        for b in range(rb // span):
            def unit(r, b=b):
                base = b * span + r
                rows = _phase_rows(base, d)
                if b > 0:
                    kprev, vprev = kc[_phase_rows(base - span, d), :], vc[_phase_rows(base - span, d), :]
                    valid = inwin
                else:
                    kprev, vprev = kp[_phase_rows(r, d), :], vp[_phase_rows(r, d), :]
                    valid = inwin & ((j_ >= W) | (i > 0))
                k2 = jnp.concatenate([kprev, kc[rows, :]], axis=0).astype(BF16)
                v2 = jnp.concatenate([vprev, vc[rows, :]], axis=0).astype(BF16)
                s = lax.dot_general(q_ref[rows, :].astype(BF16), k2, NT, preferred_element_type=F32)
                s = jnp.where(valid, s * DIL_SCALE + bias - l_ref[rows, 0:1], NEG)
                pr = jnp.exp(s)
                dp = lax.dot_general(do_ref[rows, :].astype(BF16), v2, NT, preferred_element_type=F32)
                ds = pr * (dp - d_ref[rows, 0:1])
                dq_ref[rows, :] = jnp.dot(ds.astype(BF16), k2, preferred_element_type=F32) * DIL_SCALE

            _for_phases(d, unit)

    col = lambda off: off * QKV_BLOCKS + g * DIL_HEADS_PER_GROUP
    cur = lambda off: pl.BlockSpec((rb, LANES), lambda h, i: (i, col(off) + h))
    prev = lambda off: pl.BlockSpec((span, LANES), lambda h, i: (jnp.maximum(i * (rb // span) - 1, 0), col(off) + h))
    row = pl.BlockSpec((rb, LANES), lambda h, i: (i, h))
    return pl.pallas_call(
        body, name=f"dil_attn_bwd_q_g{g}", grid=(DIL_HEADS_PER_GROUP, nb),
        in_specs=[pl.BlockSpec((1, 8, LANES), lambda h, i: (h, 0, 0)), cur(0), cur(1), prev(1), cur(2), prev(2),
                  row, row, row, pl.BlockSpec(memory_space=pl.ANY)],
        out_specs=pl.BlockSpec((rb, LANES), lambda h, i: (i, g * DIL_HEADS_PER_GROUP + h)),
        out_shape=jax.ShapeDtypeStruct(buf.shape, buf.dtype), input_output_aliases={9: 0},
        compiler_params=_cparams("parallel", "parallel"),
    )(_slope_table(), qkv, qkv, qkv, qkv, qkv, do, lse, dlt, buf)


def _dil_bwd_kv(qkv, do, lse, dlt, bufk, bufv, g):
    S = qkv.shape[0]
    rb = min(DIL_ROWS, S)
    nb = S // rb
    d = DIL_PATTERN[g][1]
    span = W * d
    nub = rb // span

    def body(slope_ref, k_ref, v_ref, qc, qn, doc, don, lc, ln, dc, dn, bufk_ref, bufv_ref, dk_ref, dv_ref):
        i = pl.program_id(1)
        i_, _, dist, inwin = _window_geometry((2 * W, W), 0)
        bias = -(slope_ref[0, g:g + 1, 0:1] * float(d)) * dist.astype(F32)
        for b in range(nub):
            def unit(r, b=b):
                base = b * span + r
                rows = _phase_rows(base, d)
                if b < nub - 1:
                    nxt = _phase_rows(base + span, d)
                    qnext, donext, lnext, dnext = qc[nxt, :], doc[nxt, :], lc[nxt, 0:1], dc[nxt, 0:1]
                    valid = inwin
                else:
                    nxt = _phase_rows(r, d)
                    qnext, donext, lnext, dnext = qn[nxt, :], don[nxt, :], ln[nxt, 0:1], dn[nxt, 0:1]
                    valid = inwin & ((i_ < W) | (i < nb - 1))
                q2 = jnp.concatenate([qc[rows, :], qnext], axis=0).astype(BF16)
                do2 = jnp.concatenate([doc[rows, :], donext], axis=0).astype(BF16)
                l2 = jnp.concatenate([lc[rows, 0:1], lnext], axis=0)
                d2 = jnp.concatenate([dc[rows, 0:1], dnext], axis=0)
                s = lax.dot_general(q2, k_ref[rows, :].astype(BF16), NT, preferred_element_type=F32)
                s = jnp.where(valid, s * DIL_SCALE + bias - l2, NEG)
                pr = jnp.exp(s)
                dp = lax.dot_general(do2, v_ref[rows, :].astype(BF16), NT, preferred_element_type=F32)
                ds = pr * (dp - d2)
                dv_ref[rows, :] = lax.dot_general(pr.astype(BF16), do2, TN, preferred_element_type=F32)
                dk_ref[rows, :] = lax.dot_general(ds.astype(BF16), q2, TN, preferred_element_type=F32) * DIL_SCALE

            _for_phases(d, unit)

    col = lambda off: off * QKV_BLOCKS + g * DIL_HEADS_PER_GROUP
    cur = lambda off: pl.BlockSpec((rb, LANES), lambda h, i: (i, col(off) + h))
    nxt_blk = lambda i: jnp.minimum((i + 1) * nub, S // span - 1)
    nxt = lambda off: pl.BlockSpec((span, LANES), lambda h, i: (nxt_blk(i), col(off) + h))
    row = pl.BlockSpec((rb, LANES), lambda h, i: (i, h))
    row_nxt = pl.BlockSpec((span, LANES), lambda h, i: (nxt_blk(i), h))
    any_ = pl.BlockSpec(memory_space=pl.ANY)
    out = pl.BlockSpec((rb, LANES), lambda h, i: (i, g * DIL_HEADS_PER_GROUP + h))
    return pl.pallas_call(
        body, name=f"dil_attn_bwd_kv_g{g}", grid=(DIL_HEADS_PER_GROUP, nb),
        in_specs=[pl.BlockSpec((1, 8, LANES), lambda h, i: (h, 0, 0)), cur(1), cur(2), cur(0), nxt(0),
                  row, row_nxt, row, row_nxt, row, row_nxt, any_, any_],
        out_specs=[out, out],
        out_shape=[jax.ShapeDtypeStruct(bufk.shape, bufk.dtype), jax.ShapeDtypeStruct(bufv.shape, bufv.dtype)],
        input_output_aliases={11: 0, 12: 1},
        compiler_params=_cparams("parallel", "parallel"),
    )(_slope_table(), qkv, qkv, qkv, qkv, do, do, lse, lse, dlt, dlt, bufk, bufv)


ANY = pl.BlockSpec(memory_space=pl.ANY)


def _place():
    x, y, c = lax.axis_index("x"), lax.axis_index("y"), lax.axis_index("c")
    chips = [(1 - x, y), (x, 1 - y), (1 - x, 1 - y)]
    return x, y, c, chips


def _gather_weights(wb):
    R = wb.shape[0]
    H = R // 2

    def body(w_ref, out_ref, send1, recv1, send2, recv2):
        x, y, c, chips = _place()
        me = 2 * x + y
        sib = (x, y, 1 - c)
        half = pl.ds(c * H, H)
        first = [pltpu.make_async_remote_copy(
            src_ref=w_ref.at[half], dst_ref=out_ref.at[me, half], send_sem=send1.at[k], recv_sem=recv1.at[k],
            device_id=(*chip, c), device_id_type=MESH) for k, chip in enumerate(chips)]
        for cp in first:
            cp.start()
        passed = []
        for k, (cx, cy) in enumerate(chips):
            slot = out_ref.at[2 * cx + cy, half]
            pltpu.make_async_remote_copy(src_ref=slot, dst_ref=slot, send_sem=send1.at[k], recv_sem=recv1.at[k],
                                         device_id=(cx, cy, c), device_id_type=MESH).wait_recv()
            cp = pltpu.make_async_remote_copy(src_ref=slot, dst_ref=slot, send_sem=send2.at[k], recv_sem=recv2.at[k],
                                              device_id=sib, device_id_type=MESH)
            cp.start()
            passed.append(cp)
        for k, (cx, cy) in enumerate(chips):
            slot = out_ref.at[2 * cx + cy, pl.ds((1 - c) * H, H)]
            pltpu.make_async_remote_copy(src_ref=slot, dst_ref=slot, send_sem=send2.at[k], recv_sem=recv2.at[k],
                                         device_id=sib, device_id_type=MESH).wait_recv()
        for cp in first + passed:
            cp.wait_send()

    return pl.pallas_call(
        body, name="gather_weights", in_specs=[ANY], out_specs=ANY,
        out_shape=jax.ShapeDtypeStruct((4, R, D_MODEL), wb.dtype),
        scratch_shapes=[pltpu.SemaphoreType.DMA((3,)), pltpu.SemaphoreType.DMA((3,)),
                        pltpu.SemaphoreType.DMA((3,)), pltpu.SemaphoreType.DMA((3,))],
    )(wb)


def _swap_halves(g):
    H = g.shape[1] // 2

    def body(g_ref, out_ref, send, recv):
        x, y, c, _ = _place()
        sib = (x, y, 1 - c)
        cps = [pltpu.make_async_remote_copy(
            src_ref=g_ref.at[s, pl.ds((1 - c) * H, H)], dst_ref=out_ref.at[s], send_sem=send.at[s],
            recv_sem=recv.at[s], device_id=sib, device_id_type=MESH) for s in range(4)]
        for cp in cps:
            cp.start()
        for cp in cps:
            cp.wait()

    return pl.pallas_call(
        body, name="swap_halves", in_specs=[ANY], out_specs=ANY,
        out_shape=jax.ShapeDtypeStruct((4, H, D_MODEL), g.dtype),
        scratch_shapes=[pltpu.SemaphoreType.DMA((4,)), pltpu.SemaphoreType.DMA((4,))],
    )(g)


def _scatter_partials(pb):
    def body(p_ref, out_ref, send, recv):
        x, y, c, chips = _place()
        me = 2 * x + y
        cps = [pltpu.make_async_remote_copy(
            src_ref=p_ref.at[2 * cx + cy], dst_ref=out_ref.at[me], send_sem=send.at[k], recv_sem=recv.at[k],
            device_id=(cx, cy, c), device_id_type=MESH) for k, (cx, cy) in enumerate(chips)]
        for cp in cps:
            cp.start()
        for k, (cx, cy) in enumerate(chips):
            slot = out_ref.at[2 * cx + cy]
            pltpu.make_async_remote_copy(src_ref=slot, dst_ref=slot, send_sem=send.at[k], recv_sem=recv.at[k],
                                         device_id=(cx, cy, c), device_id_type=MESH).wait_recv()
        for cp in cps:
            cp.wait_send()

    return pl.pallas_call(
        body, name="scatter_partials", in_specs=[ANY], out_specs=ANY,
        out_shape=jax.ShapeDtypeStruct(pb.shape, pb.dtype),
        scratch_shapes=[pltpu.SemaphoreType.DMA((3,)), pltpu.SemaphoreType.DMA((3,))],
    )(pb)


def _sibling_half(f):
    def body(f_ref, out_ref, send, recv):
        x, y, c, _ = _place()
        cp = pltpu.make_async_remote_copy(src_ref=f_ref, dst_ref=out_ref, send_sem=send, recv_sem=recv,
                                          device_id=(x, y, 1 - c), device_id_type=MESH)
        cp.start()
        cp.wait()

    return pl.pallas_call(
        body, name="sibling_half", in_specs=[ANY], out_specs=ANY,
        out_shape=jax.ShapeDtypeStruct(f.shape, f.dtype),
        scratch_shapes=[pltpu.SemaphoreType.DMA, pltpu.SemaphoreType.DMA],
    )(f)


def _gather_tiles(tile, name):
    m_per = tile.shape[0]

    def body(x_ref, out_ref, send_sems, recv_sems, local_sem):
        x, y, c, chips = _place()
        me, sibling = (x, y, c), (x, y, 1 - c)

        def rows(px, py, pc):
            return out_ref.at[pl.ds((4 * px + 2 * py + pc) * m_per, m_per), :]

        def copy(k, block, to, src=None):
            return pltpu.make_async_remote_copy(
                src_ref=rows(*block) if src is None else src, dst_ref=rows(*block),
                send_sem=send_sems.at[k], recv_sem=recv_sems.at[k], device_id=to, device_id_type=MESH)

        mine = pltpu.make_async_copy(x_ref, rows(*me), local_sem)
        mine.start()
        first = [copy(0, me, sibling, src=x_ref)]
        first += [copy(1 + j, me, (*chip, c), src=x_ref) for j, chip in enumerate(chips)]
        for cp in first:
            cp.start()
        passed = [copy(4 + j, (*chip, c), sibling) for j, chip in enumerate(chips)]
        for j, chip in enumerate(chips):
            copy(1 + j, (*chip, c), me).wait_recv()
            passed[j].start()
        copy(0, sibling, me).wait_recv()
        for j, chip in enumerate(chips):
            copy(4 + j, (*chip, 1 - c), me).wait_recv()
        for cp in first + passed:
            cp.wait_send()
        mine.wait()

    return pl.pallas_call(
        body, name=name, out_shape=jax.ShapeDtypeStruct((8 * m_per, D_MODEL), tile.dtype),
        in_specs=[pl.BlockSpec(memory_space=pltpu.VMEM)], out_specs=pl.BlockSpec(memory_space=pltpu.VMEM),
        scratch_shapes=[pltpu.SemaphoreType.DMA((7,)), pltpu.SemaphoreType.DMA((7,)), pltpu.SemaphoreType.DMA],
    )(tile)


def _pad_rows(a, rows):
    return jnp.pad(a, ((0, rows - a.shape[0]), (0, 0)))


def _pad_row(v):
    v = v.reshape(1, -1)
    return jnp.pad(v, ((0, 0), (0, D_MODEL - v.shape[1])))


def _pack_shard(fwi, fwo, dwi, dwo, pwu, pwg):
    parts = [_pad_rows(fwi.reshape(1028, D_MODEL), 1040), fwo.reshape(256, D_MODEL), dwi.reshape(2560, D_MODEL),
             dwo.reshape(256, D_MODEL), pwu.reshape(128, D_MODEL), pwg.reshape(512, D_MODEL),
             jnp.zeros((FLAT_TOTAL - FLAT_USED, D_MODEL), fwi.dtype)]
    return jnp.concatenate(parts, axis=0)


def _unpack_shard(flat):
    out, r0 = {}, 0
    for name, rows, slot in FLAT_ROWS:
        out[name] = flat[r0:r0 + rows]
        r0 += slot
    return (out["fox_w_in"].reshape(1, D_MODEL, 1028), out["fox_w_out"].reshape(1, 256, D_MODEL),
            out["dil_w_in"].reshape(1, D_MODEL, 2560), out["dil_w_out"].reshape(1, 256, D_MODEL),
            out["ple_w_up"].reshape(2, PLE_DIM, 256), out["ple_w_gate"].reshape(2, 256, D_MODEL))


def _unpack_gathered(wall):
    out, r0 = {}, 0
    for name, rows, slot in FLAT_ROWS:
        out[name] = wall[:, r0:r0 + rows]
        r0 += slot
    cols = lambda a, n: a.reshape(4, D_MODEL, n).transpose(1, 0, 2).reshape(D_MODEL, 4 * n)
    fwi = cols(out["fox_w_in"], 1028)
    dwi = cols(out["dil_w_in"], 2560)
    fwo = out["fox_w_out"].reshape(D_MODEL, D_MODEL)
    dwo = out["dil_w_out"].reshape(D_MODEL, D_MODEL)
    pwu4 = out["ple_w_up"].reshape(4, 2, PLE_DIM, 256)
    pwg4 = out["ple_w_gate"].reshape(4, 2, 256, D_MODEL)
    pwu = [pwu4[:, i].transpose(1, 0, 2).reshape(PLE_DIM, D_MODEL) for i in range(2)]
    pwg = [pwg4[:, i].reshape(D_MODEL, D_MODEL) for i in range(2)]
    return fwi, fwo, dwi, dwo, pwu, pwg


def _pack_grads(gfwi, gfwo, gdwi, gdwo, gpwu, gpwg):
    cols = lambda a, n: a.reshape(D_MODEL, 4, n).transpose(1, 0, 2).reshape(4, n, D_MODEL)
    up = jnp.stack([a.reshape(PLE_DIM, 4, 256).transpose(1, 0, 2) for a in gpwu], axis=1)
    gate = jnp.stack([a.reshape(4, 256, D_MODEL) for a in gpwg], axis=1)
    parts = [jnp.pad(cols(gfwi, 1028), ((0, 0), (0, 12), (0, 0))), gfwo.reshape(4, 256, D_MODEL), cols(gdwi, 2560),
             gdwo.reshape(4, 256, D_MODEL), up.reshape(4, 128, D_MODEL), gate.reshape(4, 512, D_MODEL),
             jnp.zeros((4, FLAT_TOTAL - FLAT_USED, D_MODEL), F32)]
    return jnp.concatenate(parts, axis=1)


def _local_step(x, p0, p1, tgt, fox_g, dil_g, fin_g, b_f, fwi, fwo, dwi, dwo, pwu, pwg):
    S = x.shape[0]
    tm = min(256, S)
    nh = FOX_HEADS // 2
    w_qkv0 = jnp.concatenate([fwi[:, :D_MODEL] * 0.125, fwi[:, D_MODEL:3 * D_MODEL]], axis=1)
    w_z0 = fwi[:, 3 * D_MODEL:4 * D_MODEL]
    swapped = lambda a: jnp.repeat(a.reshape(-1, nh, 2)[:, :, ::-1].reshape(-1, FOX_HEADS), FOX_HEAD_DIM, axis=1)
    w_f0 = swapped(fwi[:, 4 * D_MODEL:])
    b_full = swapped(b_f.reshape(1, FOX_HEADS))
    w_qkv1 = dwi[:, :3 * DIL_QKV]
    w_z1 = dwi[:, 3 * DIL_QKV:]

    n0, = _rows(_rms_fwd, [x], [(D_MODEL, BF16)], name="norm0", tm=tm, bcast=[fox_g])
    qkv0 = _mm(n0, w_qkv0, out_dtype=BF16, name="proj_qkv0")
    z0 = _mm(n0, w_z0, name="proj_z0")
    fl0 = _mm(n0, w_f0, name="proj_f0")
    c_full = _cumsum_rows([fl0], name="forget_cumsum", pre=_log_forget, bcast=[b_full])
    qx0, kx0 = _rows(_fox_extras, [c_full], [(D_MODEL, BF16), (D_MODEL, BF16)], name="fox_extras", tm=tm)
    o0, a0 = _fox_fwd(qkv0, qx0, kx0)
    g0, = _rows(_gate_fwd, [o0, z0], [(D_MODEL, BF16)], name="gate0", tm=tm)
    h1 = _mm(g0, fwo, add=x, name="out_proj0")
    u0 = _mm(p0, pwu[0], name="ple_up0")
    t0 = _mm(h1, pwg[0], name="ple_gate0")
    h2, = _rows(_ple_fwd, [h1, u0, t0], [(D_MODEL, F32)], name="ple_mix0", tm=tm)

    n1, = _rows(_rms_fwd, [h2], [(D_MODEL, BF16)], name="norm1", tm=tm, bcast=[dil_g])
    qkv1 = _mm(n1, w_qkv1, name="proj_qkv1")
    z1 = _mm(n1, w_z1, name="proj_z1")
    o1, g1, lse1 = _dil_fwd(qkv1, z1)
    h3 = _mm(g1, dwo, add=h2, name="out_proj1")
    u1 = _mm(p1, pwu[1], name="ple_up1")
    t1 = _mm(h3, pwg[1], name="ple_gate1")
    h4, = _rows(_ple_fwd, [h3, u1, t1], [(D_MODEL, F32)], name="ple_mix1", tm=tm)

    dh4, g_fin, loss = _rows(_final_stage, [h4, tgt], [(D_MODEL, F32)], name="loss_head", tm=tm, bcast=[fin_g],
                             acc=[((1, D_MODEL), F32), ((1, LANES), F32)])

    du1, dt1 = _rows(_ple_bwd, [dh4, u1, t1], [(D_MODEL, BF16), (D_MODEL, BF16)], name="ple_bwd1", tm=tm)
    g_up1 = _mm(p1, du1, ta=True, name="grad_ple_up1")
    g_gate1 = _mm(h3, dt1, ta=True, name="grad_ple_gate1")
    dh3 = _mm(dt1, pwg[1], tb=True, add=dh4, name="ple_back1")
    dg1 = _mm(dh3, dwo, tb=True, name="out_back1")
    g_dwo = _mm(g1, dh3, ta=True, name="grad_out1")
    do1, dz1, dl1 = _rows(_gate_bwd, [dg1, o1, z1], [(D_MODEL, F32), (D_MODEL, BF16), (D_MODEL, F32)],
                          name="gate_bwd1", tm=tm)
    dq1, dk1, dv1 = (lax.empty((S, DIL_QKV), F32) for _ in range(3))
    for g in range(DIL_GROUPS):
        dq1 = _dil_bwd_q(qkv1, do1, lse1, dl1, dq1, g)
        dk1, dv1 = _dil_bwd_kv(qkv1, do1, lse1, dl1, dk1, dv1, g)
    g_dwi = _mm_tn_parts(n1, [dq1, dk1, dv1, dz1], name="grad_in1")
    dn1 = _mm_nt_parts([dq1, dk1, dv1, dz1], dwi, name="in_back1")
    dh2, g_dil = _rows(_rms_bwd, [dn1, h2, dh3], [(D_MODEL, F32)], name="norm_bwd1", tm=tm, bcast=[dil_g],
                       acc=[((1, D_MODEL), F32)])

    du0, dt0 = _rows(_ple_bwd, [dh2, u0, t0], [(D_MODEL, BF16), (D_MODEL, BF16)], name="ple_bwd0", tm=tm)
    g_up0 = _mm(p0, du0, ta=True, name="grad_ple_up0")
    g_gate0 = _mm(h1, dt0, ta=True, name="grad_ple_gate0")
    dh1 = _mm(dt0, pwg[0], tb=True, add=dh2, name="ple_back0")
    dg0 = _mm(dh1, fwo, tb=True, name="out_back0")
    g_fwo = _mm(g0, dh1, ta=True, name="grad_out0")
    do0, dz0, qxa0, dx0 = _rows(_gate_bwd_fox, [dg0, o0, z0, a0], [(D_MODEL, BF16)] * 4, name="gate_bwd0", tm=tm)
    dq0, drx, dk0, dv0, dkx = _fox_bwd(qkv0, do0, qxa0, dx0, kx0)
    dlogf = _cumsum_rows([drx, dkx], name="forget_cumsum_bwd", reverse=True, pre=_forget_dc)
    df0, g_bf = _rows(_forget_bwd, [dlogf, fl0], [(D_MODEL, BF16)], name="forget_bwd", tm=tm, bcast=[b_full],
                      acc=[((1, D_MODEL), F32)])
    g_fwi_main = _mm_tn_parts(n0, [dq0, dk0, dv0, dz0], name="grad_in0")
    g_fwi_f = _mm(n0, df0, ta=True, name="grad_in0_forget")
    dn0 = _mm_nt_parts([dq0, dk0, dv0, dz0], fwi[:, :4 * D_MODEL], name="in_back0")
    dn0 = _mm(df0, w_f0, tb=True, add=dn0, name="in_back0_forget")
    dx, g_fox = _rows(_rms_bwd, [dn0, x, dh1], [(D_MODEL, F32)], name="norm_bwd0", tm=tm, bcast=[fox_g],
                      acc=[((1, D_MODEL), F32)])
    first_lane = lambda a: a.reshape(-1, nh, 2, FOX_HEAD_DIM)[:, :, ::-1, 0].reshape(-1, FOX_HEADS)
    g_fwi = jnp.concatenate([g_fwi_main, first_lane(g_fwi_f)], axis=1)
    return (loss, dx, (g_fwi, g_fwo, g_dwi, g_dwo, [g_up0, g_up1], [g_gate0, g_gate1]),
            (g_fox, _pad_row(first_lane(g_bf)), g_fin, g_dil))


def kernel(x, p, fox_norm, fox_w_in, fox_b_f, fox_w_out, dil_norm, dil_w_in, dil_w_out, ple_w_up, ple_w_gate, final_norm, loss_target, m_fox_norm, m_fox_w_in, m_fox_b_f, m_fox_w_out, m_dil_norm, m_dil_w_in, m_dil_w_out, m_ple_w_up, m_ple_w_gate, m_final_norm, v_fox_norm, v_fox_w_in, v_fox_b_f, v_fox_w_out, v_dil_norm, v_dil_w_in, v_dil_w_out, v_ple_w_up, v_ple_w_gate, v_final_norm):
    xi, yi, ci = lax.axis_index("x"), lax.axis_index("y"), lax.axis_index("c")
    chip = 2 * xi + yi

    w_flat = _pack_shard(fox_w_in, fox_w_out, dil_w_in, dil_w_out, ple_w_up, ple_w_gate)
    m_flat = _pack_shard(m_fox_w_in, m_fox_w_out, m_dil_w_in, m_dil_w_out, m_ple_w_up, m_ple_w_gate)
    v_flat = _pack_shard(v_fox_w_in, v_fox_w_out, v_dil_w_in, v_dil_w_out, v_ple_w_up, v_ple_w_gate)
    w_bf = w_flat.astype(BF16)
    wall = lax.dynamic_update_slice(_gather_weights(w_bf), w_bf[None], (chip, 0, 0))
    fwi, fwo, dwi, dwo, pwu, pwg = _unpack_gathered(wall)
    gains = _gather_tiles(_pad_rows(_pad_row(dil_norm), 8), "gather_gain").reshape(4, 2, 8, D_MODEL)
    dil_g = gains[:, 0, 0, :256].reshape(1, D_MODEL)

    loss_part, dx, grads, small = _local_step(
        x[0], p[0, 0], p[1, 0], loss_target[0], fox_norm.reshape(1, D_MODEL), dil_g,
        final_norm.reshape(1, D_MODEL), fox_b_f, fwi, fwo, dwi, dwo, pwu, pwg)
    g_fox, g_bf, g_fin, g_dil = small

    g_all = _pack_grads(*grads)
    theirs = _swap_halves(g_all)
    mine = lax.dynamic_slice_in_dim(g_all, ci * FLAT_HALF, FLAT_HALF, axis=1)
    part, = _rows(lambda a, b: a + b, [mine.reshape(4 * FLAT_HALF, D_MODEL), theirs.reshape(4 * FLAT_HALF, D_MODEL)],
                  [(D_MODEL, BF16)], name="pair_sum", tm=FLAT_TILE)
    part = part.reshape(4, FLAT_HALF, D_MODEL)
    own = lax.dynamic_slice_in_dim(part, chip, 1, axis=0)
    by_chip = lax.dynamic_update_slice(_scatter_partials(part), own, (chip, 0, 0))
    half_sum, = _rows(lambda a, b, c, d: ((a.astype(F32) + b.astype(F32)) + c.astype(F32)) + d.astype(F32),
                      [by_chip[s] for s in range(4)], [(D_MODEL, F32)], name="chip_sum", tm=FLAT_TILE)
    other_half = _sibling_half(half_sum)
    g_flat = jnp.where(ci == 0, jnp.concatenate([half_sum, other_half], axis=0),
                       jnp.concatenate([other_half, half_sum], axis=0))

    tile = jnp.concatenate([g_fox, g_bf, g_fin, g_dil, jnp.pad(loss_part, ((0, 0), (0, D_MODEL - LANES))),
                            jnp.zeros((3, D_MODEL), F32)], axis=0)
    tiles = _gather_tiles(tile, "gather_small")
    tot, = _rows(lambda *t: functools.reduce(lambda a, b: a + b, t), [tiles[8 * s:8 * s + 8] for s in range(8)],
                 [(D_MODEL, F32)], name="small_sum", tm=8)
    loss = tot[4, 0]
    g_small = jnp.concatenate([
        tot[0:3],
        _pad_row(lax.dynamic_slice_in_dim(tot[3], chip * 256, 256)), jnp.zeros((4, D_MODEL), F32)], axis=0)
    small_tile = lambda a, b, c, d: jnp.concatenate(
        [_pad_row(a), _pad_row(b), _pad_row(c), _pad_row(d), jnp.zeros((4, D_MODEL), F32)], axis=0)
    w_small = small_tile(fox_norm, fox_b_f, final_norm, dil_norm)
    m_small = small_tile(m_fox_norm, m_fox_b_f, m_final_norm, m_dil_norm)
    v_small = small_tile(v_fox_norm, v_fox_b_f, v_final_norm, v_dil_norm)

    three = [(D_MODEL, F32)] * 3
    d_flat, nm_flat, nv_flat = _rows(_adamw, [w_flat, g_flat, m_flat, v_flat], three, name="adamw", tm=FLAT_TILE)
    d_small, nm_small, nv_small = _rows(_adamw, [w_small, g_small, m_small, v_small], three, name="adamw_small", tm=8)

    def leaves(flat, small_rows):
        fwi_, fwo_, dwi_, dwo_, pwu_, pwg_ = _unpack_shard(flat)
        return (small_rows[0:1], fwi_, small_rows[1:2, :FOX_HEADS], fwo_, small_rows[3:4, :256], dwi_, dwo_,
                pwu_, pwg_, small_rows[2])

    return (loss, dx[None], *leaves(g_flat, g_small), *leaves(d_flat, d_small), *leaves(nm_flat, nm_small),
            *leaves(nv_flat, nv_small))
```

```python
import functools

import numpy as np
import jax
import jax.numpy as jnp
from jax import lax
from jax.experimental import pallas as pl
from jax.experimental.pallas import tpu as pltpu

F32 = jnp.float32
BF16 = jnp.bfloat16

D_MODEL = 1024
PLE_DIM = 256
FOX_HEADS = 16
FOX_HEAD_DIM = 64
DIL_PATTERN = ((128, 1), (512, 4), (2048, 16))
DIL_GROUPS = 3
DIL_HEADS_PER_GROUP = 8
DIL_HEAD_DIM = 128
DIL_WINDOW_STEPS = 128
DIL_QKV = 3072
ALIBI_MAX_EXP = 8.0
RMS_EPS = 1e-6
ADAM_LR, ADAM_B1, ADAM_B2, ADAM_EPS, ADAM_WD, ADAM_STEP = 0.001, 0.9, 0.999, 1e-08, 0.01, 10

LANES = 128
VMEM_LIMIT = 56 * 1024 * 1024
MESH = pl.DeviceIdType.MESH
NEG = -1e30

FLAT_ROWS = (("fox_w_in", 1028, 1040), ("fox_w_out", 256, 256), ("dil_w_in", 2560, 2560), ("dil_w_out", 256, 256),
             ("ple_w_up", 128, 128), ("ple_w_gate", 512, 512))
FLAT_USED = sum(r for _, _, r in FLAT_ROWS)
FLAT_TOTAL = 4864
FLAT_HALF = FLAT_TOTAL // 2
FLAT_TILE = 608


def _cparams(*sem):
    return pltpu.CompilerParams(dimension_semantics=sem, vmem_limit_bytes=VMEM_LIMIT)


def _sigmoid(x):
    return 1.0 / (1.0 + jnp.exp(-x))


def _mm(a, b, *, name, ta=False, tb=False, out_dtype=F32, add=None):
    if ta:
        K, M = a.shape
    else:
        M, K = a.shape
    if tb:
        N, Kb = b.shape
    else:
        Kb, N = b.shape
    assert K == Kb, (a.shape, b.shape)
    tm, tn, tk = min(M, 1024), min(N, 1024), min(K, 1024)
    assert M % tm == 0 and N % tn == 0 and K % tk == 0, (M, N, K)
    nk = K // tk
    dn = (((0 if ta else 1,), (1 if tb else 0,)), ((), ()))

    def body(*refs):
        if add is None:
            a_ref, b_ref, o_ref, acc = refs
        else:
            a_ref, b_ref, add_ref, o_ref, acc = refs
        k = pl.program_id(2)

        @pl.when(k == 0)
        def _():
            acc[...] = jnp.zeros_like(acc)

        acc[...] += lax.dot_general(a_ref[...].astype(BF16), b_ref[...].astype(BF16), dn,
                                    preferred_element_type=F32)

        @pl.when(k == nk - 1)
        def _():
            r = acc[...]
            if add is not None:
                r = r + add_ref[...]
            o_ref[...] = r.astype(out_dtype)

    a_spec = (pl.BlockSpec((tk, tm), lambda i, j, k: (k, i)) if ta
              else pl.BlockSpec((tm, tk), lambda i, j, k: (i, k)))
    b_spec = (pl.BlockSpec((tn, tk), lambda i, j, k: (j, k)) if tb
              else pl.BlockSpec((tk, tn), lambda i, j, k: (k, j)))
    in_specs = [a_spec, b_spec]
    args = [a, b]
    if add is not None:
        in_specs.append(pl.BlockSpec((tm, tn), lambda i, j, k: (i, j)))
        args.append(add)
    return pl.pallas_call(
        body, name=name, grid=(M // tm, N // tn, nk),
        in_specs=in_specs, out_specs=pl.BlockSpec((tm, tn), lambda i, j, k: (i, j)),
        out_shape=jax.ShapeDtypeStruct((M, N), out_dtype),
        scratch_shapes=[pltpu.VMEM((tm, tn), F32)],
        compiler_params=_cparams("parallel", "parallel", "arbitrary"),
    )(*args)


def _mm_nt_parts(a_parts, b, *, name):
    M = a_parts[0].shape[0]
    N, K = b.shape
    tm, tn, tk = min(M, 1024), min(N, 1024), 1024
    nks = [a.shape[1] // tk for a in a_parts]
    offs = [sum(nks[:p]) for p in range(len(nks))]
    nk = sum(nks)
    assert nk * tk == K and M % tm == 0 and N % tn == 0, (M, N, K)
    n_parts = len(a_parts)

    def body(*refs):
        a_refs, b_ref, o_ref, acc = refs[:n_parts], refs[n_parts], refs[n_parts + 1], refs[n_parts + 2]
        k = pl.program_id(2)

        @pl.when(k == 0)
        def _():
            acc[...] = jnp.zeros_like(acc)

        for a_ref, off, n in zip(a_refs, offs, nks):
            @pl.when((k >= off) & (k < off + n))
            def _(a_ref=a_ref):
                acc[...] += lax.dot_general(a_ref[...].astype(BF16), b_ref[...].astype(BF16),
                                            (((1,), (1,)), ((), ())), preferred_element_type=F32)

        @pl.when(k == nk - 1)
        def _():
            o_ref[...] = acc[...]

    a_specs = [pl.BlockSpec((tm, tk), lambda i, j, k, off=off, n=n: (i, jnp.clip(k - off, 0, n - 1)))
               for off, n in zip(offs, nks)]
    return pl.pallas_call(
        body, name=name, grid=(M // tm, N // tn, nk),
        in_specs=a_specs + [pl.BlockSpec((tn, tk), lambda i, j, k: (j, k))],
        out_specs=pl.BlockSpec((tm, tn), lambda i, j, k: (i, j)),
        out_shape=jax.ShapeDtypeStruct((M, N), F32), scratch_shapes=[pltpu.VMEM((tm, tn), F32)],
        compiler_params=_cparams("parallel", "parallel", "arbitrary"),
    )(*a_parts, b)


def _mm_tn_parts(a, b_parts, *, name):
    K, M = a.shape
    tm, tn, tk = min(M, 1024), 1024, min(K, 1024)
    njs = [b.shape[1] // tn for b in b_parts]
    offs = [sum(njs[:p]) for p in range(len(njs))]
    nj, nk = sum(njs), K // tk
    assert M % tm == 0 and K % tk == 0 and all(b.shape[1] % tn == 0 for b in b_parts)
    n_parts = len(b_parts)

    def body(*refs):
        a_ref, b_refs, o_ref, acc = refs[0], refs[1:1 + n_parts], refs[1 + n_parts], refs[2 + n_parts]
        j, k = pl.program_id(1), pl.program_id(2)

        @pl.when(k == 0)
        def _():
            acc[...] = jnp.zeros_like(acc)

        for b_ref, off, n in zip(b_refs, offs, njs):
            @pl.when((j >= off) & (j < off + n))
            def _(b_ref=b_ref):
                acc[...] += lax.dot_general(a_ref[...].astype(BF16), b_ref[...].astype(BF16),
                                            (((0,), (0,)), ((), ())), preferred_element_type=F32)

        @pl.when(k == nk - 1)
        def _():
            o_ref[...] = acc[...]

    def b_spec(off, n):
        def index(i, j, k):
            mine = (j >= off) & (j < off + n)
            return jnp.where(mine, k, 0), jnp.clip(j - off, 0, n - 1)
        return pl.BlockSpec((tk, tn), index)

    return pl.pallas_call(
        body, name=name, grid=(M // tm, nj, nk),
        in_specs=[pl.BlockSpec((tk, tm), lambda i, j, k: (k, i))] + [b_spec(off, n) for off, n in zip(offs, njs)],
        out_specs=pl.BlockSpec((tm, tn), lambda i, j, k: (i, j)),
        out_shape=jax.ShapeDtypeStruct((M, nj * tn), F32), scratch_shapes=[pltpu.VMEM((tm, tn), F32)],
        compiler_params=_cparams("parallel", "parallel", "arbitrary"),
    )(a, *b_parts)


def _rows(fn, ins, outs, *, name, tm, bcast=(), acc=()):
    R = ins[0].shape[0]
    assert R % tm == 0, (R, tm)
    n_in, n_b, n_out, n_acc = len(ins), len(bcast), len(outs), len(acc)

    def body(*refs):
        in_refs = refs[:n_in + n_b]
        out_refs = refs[n_in + n_b:n_in + n_b + n_out]
        acc_refs = refs[n_in + n_b + n_out:]
        res = fn(*[r[...] for r in in_refs])
        if not isinstance(res, (tuple, list)):
            res = (res,)
        for r, v in zip(out_refs, res[:n_out]):
            r[...] = v.astype(r.dtype)
        first = pl.program_id(0) == 0
        for r, v in zip(acc_refs, res[n_out:]):
            @pl.when(first)
            def _(r=r, v=v):
                r[...] = v.astype(r.dtype)

            @pl.when(jnp.logical_not(first))
            def _(r=r, v=v):
                r[...] += v.astype(r.dtype)

    in_specs = [pl.BlockSpec((tm, a.shape[1]), lambda i: (i, 0)) for a in ins]
    in_specs += [pl.BlockSpec(b.shape, lambda i, nd=b.ndim: (0,) * nd) for b in bcast]
    out_specs = [pl.BlockSpec((tm, c), lambda i: (i, 0)) for c, _ in outs]
    out_specs += [pl.BlockSpec(s, lambda i, nd=len(s): (0,) * nd) for s, _ in acc]
    out_shape = [jax.ShapeDtypeStruct((R, c), dt) for c, dt in outs]
    out_shape += [jax.ShapeDtypeStruct(s, dt) for s, dt in acc]
    res = pl.pallas_call(
        body, name=name, grid=(R // tm,), in_specs=in_specs, out_specs=out_specs, out_shape=out_shape,
        compiler_params=_cparams("arbitrary" if acc else "parallel"),
    )(*ins, *bcast)
    return res


def _cumsum_rows(ins, *, name, reverse=False, pre=None, bcast=(), post=None, out_dtypes=(F32,)):
    S, C = ins[0].shape
    tb = 256
    nb = S // tb
    assert S % tb == 0
    n_in = len(ins) + len(bcast)
    n_out = len(out_dtypes)

    def body(*refs):
        in_refs, o_refs, carry = refs[:n_in], refs[n_in:n_in + n_out], refs[n_in + n_out]

        @pl.when(pl.program_id(0) == 0)
        def _():
            carry[...] = jnp.zeros_like(carry)

        xv = in_refs[0][...] if pre is None else pre(*[r[...] for r in in_refs])
        r_ = lax.broadcasted_iota(jnp.int32, (tb, tb), 0)
        c_ = lax.broadcasted_iota(jnp.int32, (tb, tb), 1)
        tri = jnp.where((c_ >= r_) if reverse else (c_ <= r_), 1.0, 0.0).astype(BF16)
        hi = xv.astype(BF16)
        r1 = xv - hi.astype(F32)
        mid = r1.astype(BF16)
        lo = (r1 - mid.astype(F32)).astype(BF16)
        cs = (jnp.dot(tri, hi, preferred_element_type=F32) + jnp.dot(tri, mid, preferred_element_type=F32)
              + jnp.dot(tri, lo, preferred_element_type=F32)) + carry[...]
        outs = (cs,) if post is None else post(cs)
        for o_ref, v in zip(o_refs, outs):
            o_ref[...] = v.astype(o_ref.dtype)
        carry[...] = cs[0:1, :] if reverse else cs[tb - 1:tb, :]

    blk = (lambda i: (nb - 1 - i, 0)) if reverse else (lambda i: (i, 0))
    in_specs = [pl.BlockSpec((tb, C), blk) for _ in ins]
    in_specs += [pl.BlockSpec(b.shape, lambda i, nd=b.ndim: (0,) * nd) for b in bcast]
    return pl.pallas_call(
        body, name=name, grid=(nb,), in_specs=in_specs, out_specs=[pl.BlockSpec((tb, C), blk)] * n_out,
        out_shape=[jax.ShapeDtypeStruct((S, C), dt) for dt in out_dtypes], scratch_shapes=[pltpu.VMEM((1, C), F32)],
        compiler_params=_cparams("arbitrary"),
    )(*ins, *bcast)


def _rms_fwd(x, g):
    r = lax.rsqrt(jnp.mean(x * x, axis=1, keepdims=True) + RMS_EPS)
    return (x * r) * g


def _rms_bwd(dn, x, dres, g):
    r = lax.rsqrt(jnp.mean(x * x, axis=1, keepdims=True) + RMS_EPS)
    xh = x * r
    w = dn * g
    dx = r * (w - xh * jnp.mean(w * xh, axis=1, keepdims=True))
    return dres + dx, jnp.sum(dn * xh, axis=0, keepdims=True)


def _final_stage(h, tgt, g):
    r = lax.rsqrt(jnp.mean(h * h, axis=1, keepdims=True) + RMS_EPS)
    xh = h * r
    diff = xh * g - tgt
    loss = 0.5 * jnp.sum(jnp.mean(diff * diff, axis=1, keepdims=True), axis=0, keepdims=True)
    dy = diff * (1.0 / D_MODEL)
    w = dy * g
    dh = r * (w - xh * jnp.mean(w * xh, axis=1, keepdims=True))
    return dh, jnp.sum(dy * xh, axis=0, keepdims=True), jnp.broadcast_to(loss, (1, LANES))


def _ple_fwd(h, u, t):
    return h + u * _sigmoid(t)


def _ple_bwd(dh, u, t):
    s = _sigmoid(t)
    return dh * s, dh * u * s * (1.0 - s)


def _ple_norm(h, u, t, g):
    h2 = _ple_fwd(h, u, t)
    return h2, _rms_fwd(h2, g)


def _ple_loss(h, u, t, tgt, g):
    dh, g_gain, loss = _final_stage(_ple_fwd(h, u, t), tgt, g)
    du, dt = _ple_bwd(dh, u, t)
    return du, dt, dh, g_gain, loss


def _norm_ple_bwd(dn, x, dres, u, t, g):
    dh, g_gain = _rms_bwd(dn, x, dres, g)
    du, dt = _ple_bwd(dh, u, t)
    return dh, du, dt, g_gain


def _gate_fwd(o, z):
    return o * (z * _sigmoid(z))


def _head_sums(prod):
    tm, width = prod.shape
    cols = [jnp.broadcast_to(jnp.sum(prod[:, b * LANES:(b + 1) * LANES], axis=1, keepdims=True), (tm, LANES))
            for b in range(width // LANES)]
    return jnp.concatenate(cols, axis=1)


def _gate_bwd(dg, o, z, lse):
    s = _sigmoid(z)
    do = dg * (z * s)
    dz = dg * o * (s * (1.0 + z * (1.0 - s)))
    lane = lax.broadcasted_iota(jnp.int32, do.shape, 1) % LANES
    return do, dz, jnp.where(lane < LANES // 2, lse, _head_sums(do * o))


def _log_forget(fl, b):
    u = fl + b
    return jnp.minimum(u, 0.0) - jnp.log(1.0 + jnp.exp(-jnp.abs(u)))


def _adamw(w, g, m, v):
    m = ADAM_B1 * m + (1.0 - ADAM_B1) * g
    v = ADAM_B2 * v + (1.0 - ADAM_B2) * (g * g)
    m_hat = m / (1.0 - ADAM_B1 ** ADAM_STEP)
    v_hat = v / (1.0 - ADAM_B2 ** ADAM_STEP)
    delta = -ADAM_LR * (m_hat / (jnp.sqrt(v_hat) + ADAM_EPS) + ADAM_WD * w)
    return delta, m, v


FOX_TQ = 1024


def _split3(x):
    p1 = x.astype(BF16).astype(F32)
    r = x - p1
    p2 = r.astype(BF16).astype(F32)
    return p1, p2, r - p2


def _lane_in_head(shape):
    return lax.broadcasted_iota(jnp.int32, shape, 1) % FOX_HEAD_DIM


def _query_extras(x):
    lm = _lane_in_head(x.shape)
    p1, p2, p3 = _split3(x)
    return jnp.where(lm == 0, p1, jnp.where(lm == 1, p2, jnp.where(lm == 2, p3, jnp.where(lm < 6, 1.0, 0.0))))


def _key_extras(c):
    lm = _lane_in_head(c.shape)
    p1, p2, p3 = _split3(c)
    return jnp.where(lm < 3, 1.0, jnp.where(lm == 3, -p1, jnp.where(lm == 4, -p2, jnp.where(lm == 5, -p3, 0.0))))


def _fox_extras(c):
    return _query_extras(c), _key_extras(c)


def _swapped_head_sums(prod):
    tm, width = prod.shape
    lane = lax.broadcasted_iota(jnp.int32, (tm, LANES), 1)
    low = lane < FOX_HEAD_DIM
    cols = []
    for b in range(width // LANES):
        blk = prod[:, b * LANES:(b + 1) * LANES]
        sa = jnp.sum(jnp.where(low, blk, 0.0), axis=1, keepdims=True)
        sb = jnp.sum(jnp.where(low, 0.0, blk), axis=1, keepdims=True)
        cols.append(jnp.where(low, sb, sa))
    return jnp.concatenate(cols, axis=1)


def _gate_bwd_fox(dg, o, z, a):
    s = _sigmoid(z)
    do = dg * (z * s)
    dz = dg * o * (s * (1.0 + z * (1.0 - s)))
    lm = _lane_in_head(do.shape)
    d1, d2, d3 = _split3(-_swapped_head_sums(do * o))
    dx = jnp.where(lm == 0, d1, jnp.where(lm == 1, d2, jnp.where(lm == 2, d3, 0.0)))
    return do, dz, _query_extras(a), dx


def _forget_dc(drx, dkx):
    lm = _lane_in_head(drx.shape)
    return jnp.where(lm == 0, drx, 0.0) - jnp.where(lm == 3, dkx, 0.0)


def _forget_bwd(dl, fl, b):
    lm = _lane_in_head(dl.shape)
    width = dl.shape[1]
    both = dl + pltpu.roll(dl, 3, 1) + pltpu.roll(dl, width - 3, 1)
    du = jnp.where(lm == 0, both, 0.0) * (1.0 / (1.0 + jnp.exp(fl + b)))
    return du, jnp.sum(du, axis=0, keepdims=True)


def _fox_fwd(qkv, qx, kx, z):
    S = qkv.shape[0]
    tq = min(FOX_TQ, S)
    nq = S // tq
    nt = (((1,), (1,)), ((), ()))

    def body(q_ref, qx_ref, k_ref, v_ref, kx_ref, z_ref, o_ref, a_ref, g_ref):
        i = pl.program_id(1)
        low = lax.broadcasted_iota(jnp.int32, (tq, LANES), 1) < FOX_HEAD_DIM
        row = lax.broadcasted_iota(jnp.int32, (tq, tq), 0)
        col = lax.broadcasted_iota(jnp.int32, (tq, tq), 1)
        q2, x2 = q_ref[...], qx_ref[...]
        qa = (jnp.where(low, q2, x2), jnp.where(low, x2, q2))

        def step(kb, carry, diag):
            start = pl.multiple_of(kb * tq, tq)
            k2 = k_ref[pl.ds(start, tq), :]
            v2 = v_ref[pl.ds(start, tq), :]
            y2 = kx_ref[pl.ds(start, tq), :]
            one = jnp.ones_like(v2)
            ka = (jnp.where(low, k2, y2), jnp.where(low, y2, k2))
            va = (jnp.where(low, v2, one), jnp.where(low, one, v2))
            out = []
            for hh in range(2):
                m, acc = carry[hh]
                s = lax.dot_general(qa[hh], ka[hh], nt, preferred_element_type=F32)
                if diag:
                    s = jnp.where(col <= row, s, NEG)
                m_new = jnp.maximum(m, jnp.max(s, axis=1, keepdims=True))
                pr = jnp.exp(s - m_new)
                acc = jnp.exp(m - m_new) * acc + jnp.dot(pr.astype(BF16), va[hh], preferred_element_type=F32)
                out.append((m_new, acc))
            return tuple(out)

        init = ((jnp.full((tq, 1), NEG, F32), jnp.zeros((tq, LANES), F32)),) * 2
        carry = lax.fori_loop(0, i, functools.partial(step, diag=False), init)
        (m_a, acc_a), (m_b, acc_b) = step(i, carry, True)
        l_a, l_b = acc_a[:, FOX_HEAD_DIM:FOX_HEAD_DIM + 1], acc_b[:, 0:1]
        xf = x2.astype(F32)
        c_a = xf[:, 64:65] + xf[:, 65:66] + xf[:, 66:67]
        c_b = xf[:, 0:1] + xf[:, 1:2] + xf[:, 2:3]
        o = jnp.where(low, acc_a / l_a, acc_b / l_b)
        o_ref[...] = o
        g_ref[...] = _gate_fwd(o, z_ref[...]).astype(BF16)
        a_ref[...] = jnp.where(low, jnp.broadcast_to(c_b - (m_b + jnp.log(l_b)), (tq, LANES)),
                               jnp.broadcast_to(c_a - (m_a + jnp.log(l_a)), (tq, LANES)))

    blk = lambda cb: pl.BlockSpec((tq, LANES), lambda h, i, cb=cb: (i, cb + h))
    res = lambda cb: pl.BlockSpec((S, LANES), lambda h, i, cb=cb: (0, cb + h))
    return pl.pallas_call(
        body, name="fox_attn_fwd", grid=(FOX_HEADS // 2, nq),
        in_specs=[blk(0), blk(0), res(8), res(16), res(0), blk(0)],
        out_specs=[blk(0), blk(0), blk(0)],
        out_shape=[jax.ShapeDtypeStruct((S, D_MODEL), F32), jax.ShapeDtypeStruct((S, D_MODEL), F32),
                   jax.ShapeDtypeStruct((S, D_MODEL), BF16)],
        compiler_params=_cparams("parallel", "arbitrary"),
    )(qkv, qx, qkv, qkv, kx, z)


def _fox_bwd(qkv, do, qxa, dx, kx):
    S = qkv.shape[0]
    tq = min(FOX_TQ, S)
    nq = S // tq
    nt = (((1,), (1,)), ((), ()))
    tn = (((0,), (0,)), ((), ()))

    def body(q_ref, qx_ref, do_ref, dx_ref, k_ref, v_ref, kx_ref, dq_ref, dr_ref, dk_ref, dv_ref, dkx_ref):
        kb = pl.program_id(1)
        low = lax.broadcasted_iota(jnp.int32, (tq, LANES), 1) < FOX_HEAD_DIM
        row = lax.broadcasted_iota(jnp.int32, (tq, tq), 0)
        col = lax.broadcasted_iota(jnp.int32, (tq, tq), 1)

        @pl.when(kb == 0)
        def _():
            dq_ref[...] = jnp.zeros_like(dq_ref)
            dr_ref[...] = jnp.zeros_like(dr_ref)

        k2, v2, y2 = k_ref[...], v_ref[...], kx_ref[...]
        one = jnp.ones_like(v2)
        ka = (jnp.where(low, k2, y2), jnp.where(low, y2, k2))
        va = (jnp.where(low, v2, one), jnp.where(low, one, v2))

        def step(qb, carry, diag):
            start = pl.multiple_of(qb * tq, tq)
            q2 = q_ref[pl.ds(start, tq), :]
            x2 = qx_ref[pl.ds(start, tq), :]
            d2 = do_ref[pl.ds(start, tq), :]
            e2 = dx_ref[pl.ds(start, tq), :]
            qa = (jnp.where(low, q2, x2), jnp.where(low, x2, q2))
            da = (jnp.where(low, d2, e2), jnp.where(low, e2, d2))
            new, res = [], []
            for hh in range(2):
                dk, dv = carry[hh]
                s = lax.dot_general(qa[hh], ka[hh], nt, preferred_element_type=F32)
                if diag:
                    s = jnp.where(col <= row, s, NEG)
                pr = jnp.exp(s)
                ds = pr * lax.dot_general(da[hh], va[hh], nt, preferred_element_type=F32)
                prb, dsb = pr.astype(BF16), ds.astype(BF16)
                dv = dv + lax.dot_general(prb, da[hh], tn, preferred_element_type=F32)
                dk = dk + lax.dot_general(dsb, qa[hh], tn, preferred_element_type=F32)
                res.append(jnp.dot(dsb, ka[hh], preferred_element_type=F32))
                new.append((dk, dv))
            dq_ref[pl.ds(start, tq), :] += jnp.where(low, res[0], res[1])
            dr_ref[pl.ds(start, tq), :] += jnp.where(low, res[1], res[0])
            return tuple(new)

        init = ((jnp.zeros((tq, LANES), F32), jnp.zeros((tq, LANES), F32)),) * 2
        carry = step(kb, init, True)
        (dk_a, dv_a), (dk_b, dv_b) = lax.fori_loop(kb + 1, nq, functools.partial(step, diag=False), carry)
        dk_ref[...] = jnp.where(low, dk_a, dk_b).astype(BF16)
        dv_ref[...] = jnp.where(low, dv_a, dv_b).astype(BF16)
        dkx_ref[...] = jnp.where(low, dk_b, dk_a)

        @pl.when(kb == nq - 1)
        def _():
            dq_ref[...] = dq_ref[...] * (FOX_HEAD_DIM ** -0.5)

    res = lambda cb: pl.BlockSpec((S, LANES), lambda h, k, cb=cb: (0, cb + h))
    blk = lambda cb: pl.BlockSpec((tq, LANES), lambda h, k, cb=cb: (k, cb + h))
    f32, b16 = jax.ShapeDtypeStruct((S, D_MODEL), F32), jax.ShapeDtypeStruct((S, D_MODEL), BF16)
    return pl.pallas_call(
        body, name="fox_attn_bwd", grid=(FOX_HEADS // 2, nq),
        in_specs=[res(0), res(0), res(0), res(0), blk(8), blk(16), blk(0)],
        out_specs=[res(0), res(0), blk(0), blk(0), blk(0)],
        out_shape=[f32, f32, b16, b16, f32],
        compiler_params=_cparams("parallel", "arbitrary"),
    )(qkv, qxa, do, dx, qkv, qkv, kx)


def _alibi_slopes():
    n = DIL_GROUPS * DIL_HEADS_PER_GROUP
    s = np.float32(2.0) ** (np.float32(-ALIBI_MAX_EXP) * np.arange(1, n + 1, dtype=np.float32) / np.float32(n))
    return s.astype(np.float32).reshape(DIL_GROUPS, DIL_HEADS_PER_GROUP)


W = DIL_WINDOW_STEPS
DIL_SCALE = DIL_HEAD_DIM ** -0.5


DIL_ROWS = 2048
NT = (((1,), (1,)), ((), ()))
TN = (((0,), (0,)), ((), ()))
QKV_BLOCKS = DIL_QKV // LANES


def _slope_table():
    t = np.zeros((DIL_HEADS_PER_GROUP, 8, LANES), np.float32)
    t[:, :DIL_GROUPS, :] = _alibi_slopes().T[:, :, None]
    return jnp.asarray(t)


def _phase_rows(start, d):
    return pl.ds(start, W, stride=d) if d > 1 else pl.ds(start, W)


def _for_phases(d, unit):
    if d == 1:
        unit(0)
    else:
        lax.fori_loop(0, d, lambda r, c: (unit(r), c)[1], 0)


def _window_geometry(shape, q_axis_offset):
    i_ = lax.broadcasted_iota(jnp.int32, shape, 0)
    j_ = lax.broadcasted_iota(jnp.int32, shape, 1)
    dist = q_axis_offset + i_ - j_
    return i_, j_, dist, (dist >= 0) & (dist <= W)


def _dil_fwd(qkv, z):
    S = qkv.shape[0]
    rb = min(DIL_ROWS, S)
    nb = S // rb
    spans = [W * d for _, d in DIL_PATTERN]

    def body(*refs):
        slope_ref = refs[0]
        q, kc, kp, vc, vp = refs[1:4], refs[4:7], refs[7:10], refs[10:13], refs[13:16]
        z_ref, o_ref, g_ref, l_ref = refs[16:20]
        og, lg = refs[20:23], refs[23:26]
        i = pl.program_id(1)
        _, j_, dist, inwin = _window_geometry((W, 2 * W), W)
        distf = dist.astype(F32)
        for g, (_, d) in enumerate(DIL_PATTERN):
            span = spans[g]
            bias = -(slope_ref[0, g:g + 1, 0:1] * float(d)) * distf
            for b in range(rb // span):
                def unit(r, g=g, d=d, span=span, b=b, bias=bias):
                    base = b * span + r
                    rows = _phase_rows(base, d)
                    if b > 0:
                        kprev, vprev = kc[g][_phase_rows(base - span, d), :], vc[g][_phase_rows(base - span, d), :]
                        valid = inwin
                    else:
                        kprev, vprev = kp[g][_phase_rows(r, d), :], vp[g][_phase_rows(r, d), :]
                        valid = inwin & ((j_ >= W) | (i > 0))
                    k2 = jnp.concatenate([kprev, kc[g][rows, :]], axis=0).astype(BF16)
                    v2 = jnp.concatenate([vprev, vc[g][rows, :]], axis=0).astype(BF16)
                    s = lax.dot_general(q[g][rows, :].astype(BF16), k2, NT, preferred_element_type=F32)
                    s = jnp.where(valid, s * DIL_SCALE + bias, NEG)
                    m = jnp.max(s, axis=1, keepdims=True)
                    pr = jnp.exp(s - m)
                    l = jnp.sum(pr, axis=1, keepdims=True)
                    og[g][rows, :] = jnp.dot(pr.astype(BF16), v2, preferred_element_type=F32) / l
                    lg[g][rows, :] = jnp.broadcast_to(m + jnp.log(l), (W, LANES))

                _for_phases(d, unit)

        def mix(cix, c):
            sl = pl.ds(pl.multiple_of(cix * 256, 256), 256)
            l1, l2, l3 = lg[0][sl, :], lg[1][sl, :], lg[2][sl, :]
            m = jnp.maximum(jnp.maximum(l1, l2), l3)
            e1, e2, e3 = jnp.exp(l1 - m), jnp.exp(l2 - m), jnp.exp(l3 - m)
            tot = e1 + e2 + e3
            o = (e1 * og[0][sl, :] + e2 * og[1][sl, :] + e3 * og[2][sl, :]) / tot
            zz = z_ref[sl, :]
            o_ref[sl, :] = o
            g_ref[sl, :] = (o * (zz * _sigmoid(zz))).astype(BF16)
            l_ref[sl, :] = m + jnp.log(tot)
            return c

        lax.fori_loop(0, rb // 256, mix, 0)

    cur = lambda off, g: pl.BlockSpec((rb, LANES), lambda h, i: (i, off * QKV_BLOCKS + g * DIL_HEADS_PER_GROUP + h))
    prev = lambda off, g: pl.BlockSpec(
        (spans[g], LANES),
        lambda h, i: (jnp.maximum(i * (rb // spans[g]) - 1, 0), off * QKV_BLOCKS + g * DIL_HEADS_PER_GROUP + h))
    row = pl.BlockSpec((rb, LANES), lambda h, i: (i, h))
    groups = range(DIL_GROUPS)
    in_specs = [pl.BlockSpec((1, 8, LANES), lambda h, i: (h, 0, 0))]
    in_specs += [cur(0, g) for g in groups] + [cur(1, g) for g in groups] + [prev(1, g) for g in groups]
    in_specs += [cur(2, g) for g in groups] + [prev(2, g) for g in groups] + [row]
    f32 = jax.ShapeDtypeStruct((S, D_MODEL), F32)
    return pl.pallas_call(
        body, name="dil_attn_fwd", grid=(DIL_HEADS_PER_GROUP, nb), in_specs=in_specs,
        out_specs=[row, row, row], out_shape=[f32, jax.ShapeDtypeStruct((S, D_MODEL), BF16), f32],
        scratch_shapes=[pltpu.VMEM((rb, LANES), F32)] * 6,
        compiler_params=_cparams("parallel", "parallel"),
    )(_slope_table(), *([qkv] * 15), z)


def _dil_bwd_q(qkv, do, ld, buf, g):
    S = qkv.shape[0]
    rb = min(DIL_ROWS, S)
    nb = S // rb
    d = DIL_PATTERN[g][1]
    span = W * d

    def body(slope_ref, q_ref, kc, kp, vc, vp, do_ref, ld_ref, buf_ref, dq_ref, dq_f32):
        i = pl.program_id(1)
        _, j_, dist, inwin = _window_geometry((W, 2 * W), W)
        bias = -(slope_ref[0, g:g + 1, 0:1] * float(d)) * dist.astype(F32)
        for b in range(rb // span):
            def unit(r, b=b):
                base = b * span + r
                rows = _phase_rows(base, d)
                if b > 0:
                    kprev, vprev = kc[_phase_rows(base - span, d), :], vc[_phase_rows(base - span, d), :]
                    valid = inwin
                else:
                    kprev, vprev = kp[_phase_rows(r, d), :], vp[_phase_rows(r, d), :]
                    valid = inwin & ((j_ >= W) | (i > 0))
                k2 = jnp.concatenate([kprev, kc[rows, :]], axis=0).astype(BF16)
                v2 = jnp.concatenate([vprev, vc[rows, :]], axis=0).astype(BF16)
                ld = ld_ref[rows, :]
                s = lax.dot_general(q_ref[rows, :].astype(BF16), k2, NT, preferred_element_type=F32)
                s = jnp.where(valid, s * DIL_SCALE + bias - ld[:, 0:1], NEG)
                pr = jnp.exp(s)
                dp = lax.dot_general(do_ref[rows, :].astype(BF16), v2, NT, preferred_element_type=F32)
                ds = pr * (dp - ld[:, LANES // 2:LANES // 2 + 1])
                dq_f32[rows, :] = jnp.dot(ds.astype(BF16), k2, preferred_element_type=F32) * DIL_SCALE

            _for_phases(d, unit)
        dq_ref[...] = dq_f32[...].astype(BF16)

    col = lambda off: off * QKV_BLOCKS + g * DIL_HEADS_PER_GROUP
    cur = lambda off: pl.BlockSpec((rb, LANES), lambda h, i: (i, col(off) + h))
    prev = lambda off: pl.BlockSpec((span, LANES), lambda h, i: (jnp.maximum(i * (rb // span) - 1, 0), col(off) + h))
    row = pl.BlockSpec((rb, LANES), lambda h, i: (i, h))
    return pl.pallas_call(
        body, name=f"dil_attn_bwd_q_g{g}", grid=(DIL_HEADS_PER_GROUP, nb),
        in_specs=[pl.BlockSpec((1, 8, LANES), lambda h, i: (h, 0, 0)), cur(0), cur(1), prev(1), cur(2), prev(2),
                  row, row, pl.BlockSpec(memory_space=pl.ANY)],
        out_specs=pl.BlockSpec((rb, LANES), lambda h, i: (i, g * DIL_HEADS_PER_GROUP + h)),
        out_shape=jax.ShapeDtypeStruct(buf.shape, buf.dtype), input_output_aliases={8: 0},
        scratch_shapes=[pltpu.VMEM((rb, LANES), F32)],
        compiler_params=_cparams("parallel", "parallel"),
    )(_slope_table(), qkv, qkv, qkv, qkv, qkv, do, ld, buf)


def _dil_bwd_kv(qkv, do, ld, bufk, bufv, g):
    S = qkv.shape[0]
    rb = min(DIL_ROWS, S)
    nb = S // rb
    d = DIL_PATTERN[g][1]
    span = W * d
    nub = rb // span

    def body(slope_ref, k_ref, v_ref, qc, qn, doc, don, ldc, ldn, bufk_ref, bufv_ref, dk_ref, dv_ref, dk_f32, dv_f32):
        i = pl.program_id(1)
        i_, _, dist, inwin = _window_geometry((2 * W, W), 0)
        bias = -(slope_ref[0, g:g + 1, 0:1] * float(d)) * dist.astype(F32)
        for b in range(nub):
            def unit(r, b=b):
                base = b * span + r
                rows = _phase_rows(base, d)
                if b < nub - 1:
                    nxt = _phase_rows(base + span, d)
                    qnext, donext, ldnext = qc[nxt, :], doc[nxt, :], ldc[nxt, :]
                    valid = inwin
                else:
                    nxt = _phase_rows(r, d)
                    qnext, donext, ldnext = qn[nxt, :], don[nxt, :], ldn[nxt, :]
                    valid = inwin & ((i_ < W) | (i < nb - 1))
                q2 = jnp.concatenate([qc[rows, :], qnext], axis=0).astype(BF16)
                do2 = jnp.concatenate([doc[rows, :], donext], axis=0).astype(BF16)
                ld2 = jnp.concatenate([ldc[rows, :], ldnext], axis=0)
                s = lax.dot_general(q2, k_ref[rows, :].astype(BF16), NT, preferred_element_type=F32)
                s = jnp.where(valid, s * DIL_SCALE + bias - ld2[:, 0:1], NEG)
                pr = jnp.exp(s)
                dp = lax.dot_general(do2, v_ref[rows, :].astype(BF16), NT, preferred_element_type=F32)
                ds = pr * (dp - ld2[:, LANES // 2:LANES // 2 + 1])
                dv_f32[rows, :] = lax.dot_general(pr.astype(BF16), do2, TN, preferred_element_type=F32)
                dk_f32[rows, :] = lax.dot_general(ds.astype(BF16), q2, TN, preferred_element_type=F32) * DIL_SCALE

            _for_phases(d, unit)
        dk_ref[...] = dk_f32[...].astype(BF16)
        dv_ref[...] = dv_f32[...].astype(BF16)

    col = lambda off: off * QKV_BLOCKS + g * DIL_HEADS_PER_GROUP
    cur = lambda off: pl.BlockSpec((rb, LANES), lambda h, i: (i, col(off) + h))
    nxt_blk = lambda i: jnp.minimum((i + 1) * nub, S // span - 1)
    nxt = lambda off: pl.BlockSpec((span, LANES), lambda h, i: (nxt_blk(i), col(off) + h))
    row = pl.BlockSpec((rb, LANES), lambda h, i: (i, h))
    row_nxt = pl.BlockSpec((span, LANES), lambda h, i: (nxt_blk(i), h))
    any_ = pl.BlockSpec(memory_space=pl.ANY)
    out = pl.BlockSpec((rb, LANES), lambda h, i: (i, g * DIL_HEADS_PER_GROUP + h))
    return pl.pallas_call(
        body, name=f"dil_attn_bwd_kv_g{g}", grid=(DIL_HEADS_PER_GROUP, nb),
        in_specs=[pl.BlockSpec((1, 8, LANES), lambda h, i: (h, 0, 0)), cur(1), cur(2), cur(0), nxt(0),
                  row, row_nxt, row, row_nxt, any_, any_],
        out_specs=[out, out],
        out_shape=[jax.ShapeDtypeStruct(bufk.shape, bufk.dtype), jax.ShapeDtypeStruct(bufv.shape, bufv.dtype)],
        input_output_aliases={9: 0, 10: 1},
        scratch_shapes=[pltpu.VMEM((rb, LANES), F32)] * 2,
        compiler_params=_cparams("parallel", "parallel"),
    )(_slope_table(), qkv, qkv, qkv, qkv, do, do, ld, ld, bufk, bufv)


ANY = pl.BlockSpec(memory_space=pl.ANY)


def _place():
    x, y, c = lax.axis_index("x"), lax.axis_index("y"), lax.axis_index("c")
    chips = [(1 - x, y), (x, 1 - y), (1 - x, 1 - y)]
    return x, y, c, chips


def _gather_weights(wb):
    R = wb.shape[0]
    H = R // 2

    def body(w_ref, out_ref, send1, recv1, send2, recv2):
        x, y, c, chips = _place()
        me = 2 * x + y
        sib = (x, y, 1 - c)
        half = pl.ds(c * H, H)
        first = [pltpu.make_async_remote_copy(
            src_ref=w_ref.at[half], dst_ref=out_ref.at[me, half], send_sem=send1.at[k], recv_sem=recv1.at[k],
            device_id=(*chip, c), device_id_type=MESH) for k, chip in enumerate(chips)]
        for cp in first:
            cp.start()
        passed = []
        for k, (cx, cy) in enumerate(chips):
            slot = out_ref.at[2 * cx + cy, half]
            pltpu.make_async_remote_copy(src_ref=slot, dst_ref=slot, send_sem=send1.at[k], recv_sem=recv1.at[k],
                                         device_id=(cx, cy, c), device_id_type=MESH).wait_recv()
            cp = pltpu.make_async_remote_copy(src_ref=slot, dst_ref=slot, send_sem=send2.at[k], recv_sem=recv2.at[k],
                                              device_id=sib, device_id_type=MESH)
            cp.start()
            passed.append(cp)
        for k, (cx, cy) in enumerate(chips):
            slot = out_ref.at[2 * cx + cy, pl.ds((1 - c) * H, H)]
            pltpu.make_async_remote_copy(src_ref=slot, dst_ref=slot, send_sem=send2.at[k], recv_sem=recv2.at[k],
                                         device_id=sib, device_id_type=MESH).wait_recv()
        for cp in first + passed:
            cp.wait_send()

    return pl.pallas_call(
        body, name="gather_weights", in_specs=[ANY], out_specs=ANY,
        out_shape=jax.ShapeDtypeStruct((4, R, D_MODEL), wb.dtype),
        scratch_shapes=[pltpu.SemaphoreType.DMA((3,)), pltpu.SemaphoreType.DMA((3,)),
                        pltpu.SemaphoreType.DMA((3,)), pltpu.SemaphoreType.DMA((3,))],
    )(wb)


def _swap_halves(g):
    H = g.shape[1] // 2

    def body(g_ref, out_ref, send, recv):
        x, y, c, _ = _place()
        sib = (x, y, 1 - c)
        cps = [pltpu.make_async_remote_copy(
            src_ref=g_ref.at[s, pl.ds((1 - c) * H, H)], dst_ref=out_ref.at[s], send_sem=send.at[s],
            recv_sem=recv.at[s], device_id=sib, device_id_type=MESH) for s in range(4)]
        for cp in cps:
            cp.start()
        for cp in cps:
            cp.wait()

    return pl.pallas_call(
        body, name="swap_halves", in_specs=[ANY], out_specs=ANY,
        out_shape=jax.ShapeDtypeStruct((4, H, D_MODEL), g.dtype),
        scratch_shapes=[pltpu.SemaphoreType.DMA((4,)), pltpu.SemaphoreType.DMA((4,))],
    )(g)


def _scatter_partials(pb):
    def body(p_ref, out_ref, send, recv):
        x, y, c, chips = _place()
        me = 2 * x + y
        cps = [pltpu.make_async_remote_copy(
            src_ref=p_ref.at[2 * cx + cy], dst_ref=out_ref.at[me], send_sem=send.at[k], recv_sem=recv.at[k],
            device_id=(cx, cy, c), device_id_type=MESH) for k, (cx, cy) in enumerate(chips)]
        for cp in cps:
            cp.start()
        for k, (cx, cy) in enumerate(chips):
            slot = out_ref.at[2 * cx + cy]
            pltpu.make_async_remote_copy(src_ref=slot, dst_ref=slot, send_sem=send.at[k], recv_sem=recv.at[k],
                                         device_id=(cx, cy, c), device_id_type=MESH).wait_recv()
        for cp in cps:
            cp.wait_send()

    return pl.pallas_call(
        body, name="scatter_partials", in_specs=[ANY], out_specs=ANY,
        out_shape=jax.ShapeDtypeStruct(pb.shape, pb.dtype),
        scratch_shapes=[pltpu.SemaphoreType.DMA((3,)), pltpu.SemaphoreType.DMA((3,))],
    )(pb)


def _sibling_half(f):
    def body(f_ref, out_ref, send, recv):
        x, y, c, _ = _place()
        cp = pltpu.make_async_remote_copy(src_ref=f_ref, dst_ref=out_ref, send_sem=send, recv_sem=recv,
                                          device_id=(x, y, 1 - c), device_id_type=MESH)
        cp.start()
        cp.wait()

    return pl.pallas_call(
        body, name="sibling_half", in_specs=[ANY], out_specs=ANY,
        out_shape=jax.ShapeDtypeStruct(f.shape, f.dtype),
        scratch_shapes=[pltpu.SemaphoreType.DMA, pltpu.SemaphoreType.DMA],
    )(f)


def _gather_tiles(tile, name):
    m_per = tile.shape[0]

    def body(x_ref, out_ref, send_sems, recv_sems, local_sem):
        x, y, c, chips = _place()
        me, sibling = (x, y, c), (x, y, 1 - c)

        def rows(px, py, pc):
            return out_ref.at[pl.ds((4 * px + 2 * py + pc) * m_per, m_per), :]

        def copy(k, block, to, src=None):
            return pltpu.make_async_remote_copy(
                src_ref=rows(*block) if src is None else src, dst_ref=rows(*block),
                send_sem=send_sems.at[k], recv_sem=recv_sems.at[k], device_id=to, device_id_type=MESH)

        mine = pltpu.make_async_copy(x_ref, rows(*me), local_sem)
        mine.start()
        first = [copy(0, me, sibling, src=x_ref)]
        first += [copy(1 + j, me, (*chip, c), src=x_ref) for j, chip in enumerate(chips)]
        for cp in first:
            cp.start()
        passed = [copy(4 + j, (*chip, c), sibling) for j, chip in enumerate(chips)]
        for j, chip in enumerate(chips):
            copy(1 + j, (*chip, c), me).wait_recv()
            passed[j].start()
        copy(0, sibling, me).wait_recv()
        for j, chip in enumerate(chips):
            copy(4 + j, (*chip, 1 - c), me).wait_recv()
        for cp in first + passed:
            cp.wait_send()
        mine.wait()

    return pl.pallas_call(
        body, name=name, out_shape=jax.ShapeDtypeStruct((8 * m_per, D_MODEL), tile.dtype),
        in_specs=[pl.BlockSpec(memory_space=pltpu.VMEM)], out_specs=pl.BlockSpec(memory_space=pltpu.VMEM),
        scratch_shapes=[pltpu.SemaphoreType.DMA((7,)), pltpu.SemaphoreType.DMA((7,)), pltpu.SemaphoreType.DMA],
    )(tile)


def _pad_rows(a, rows):
    return jnp.pad(a, ((0, rows - a.shape[0]), (0, 0)))


def _pad_row(v):
    v = v.reshape(1, -1)
    return jnp.pad(v, ((0, 0), (0, D_MODEL - v.shape[1])))


def _pack_shard(fwi, fwo, dwi, dwo, pwu, pwg):
    parts = [_pad_rows(fwi.reshape(1028, D_MODEL), 1040), fwo.reshape(256, D_MODEL), dwi.reshape(2560, D_MODEL),
             dwo.reshape(256, D_MODEL), pwu.reshape(128, D_MODEL), pwg.reshape(512, D_MODEL),
             jnp.zeros((FLAT_TOTAL - FLAT_USED, D_MODEL), fwi.dtype)]
    return jnp.concatenate(parts, axis=0)


def _unpack_shard(flat):
    out, r0 = {}, 0
    for name, rows, slot in FLAT_ROWS:
        out[name] = flat[r0:r0 + rows]
        r0 += slot
    return (out["fox_w_in"].reshape(1, D_MODEL, 1028), out["fox_w_out"].reshape(1, 256, D_MODEL),
            out["dil_w_in"].reshape(1, D_MODEL, 2560), out["dil_w_out"].reshape(1, 256, D_MODEL),
            out["ple_w_up"].reshape(2, PLE_DIM, 256), out["ple_w_gate"].reshape(2, 256, D_MODEL))


def _unpack_gathered(wall):
    out, r0 = {}, 0
    for name, rows, slot in FLAT_ROWS:
        out[name] = wall[:, r0:r0 + rows]
        r0 += slot
    cols = lambda a, n: a.reshape(4, D_MODEL, n).transpose(1, 0, 2).reshape(D_MODEL, 4 * n)
    fwi = cols(out["fox_w_in"], 1028)
    dwi = cols(out["dil_w_in"], 2560)
    fwo = out["fox_w_out"].reshape(D_MODEL, D_MODEL)
    dwo = out["dil_w_out"].reshape(D_MODEL, D_MODEL)
    pwu4 = out["ple_w_up"].reshape(4, 2, PLE_DIM, 256)
    pwg4 = out["ple_w_gate"].reshape(4, 2, 256, D_MODEL)
    pwu = [pwu4[:, i].transpose(1, 0, 2).reshape(PLE_DIM, D_MODEL) for i in range(2)]
    pwg = [pwg4[:, i].reshape(D_MODEL, D_MODEL) for i in range(2)]
    return fwi, fwo, dwi, dwo, pwu, pwg


def _pack_grads(gfwi, gfwo, gdwi, gdwo, gpwu, gpwg):
    cols = lambda a, n: a.reshape(D_MODEL, 4, n).transpose(1, 0, 2).reshape(4, n, D_MODEL)
    up = jnp.stack([a.reshape(PLE_DIM, 4, 256).transpose(1, 0, 2) for a in gpwu], axis=1)
    gate = jnp.stack([a.reshape(4, 256, D_MODEL) for a in gpwg], axis=1)
    parts = [jnp.pad(cols(gfwi, 1028), ((0, 0), (0, 12), (0, 0))), gfwo.reshape(4, 256, D_MODEL), cols(gdwi, 2560),
             gdwo.reshape(4, 256, D_MODEL), up.reshape(4, 128, D_MODEL), gate.reshape(4, 512, D_MODEL),
             jnp.zeros((4, FLAT_TOTAL - FLAT_USED, D_MODEL), F32)]
    return jnp.concatenate(parts, axis=1)


def _local_step(x, p0, p1, tgt, fox_g, dil_g, fin_g, b_f, fwi, fwo, dwi, dwo, pwu, pwg):
    S = x.shape[0]
    tm = min(256, S)
    nh = FOX_HEADS // 2
    w_qkv0 = jnp.concatenate([fwi[:, :D_MODEL] * 0.125, fwi[:, D_MODEL:3 * D_MODEL]], axis=1)
    w_z0 = fwi[:, 3 * D_MODEL:4 * D_MODEL]
    swapped = lambda a: jnp.repeat(a.reshape(-1, nh, 2)[:, :, ::-1].reshape(-1, FOX_HEADS), FOX_HEAD_DIM, axis=1)
    w_f0 = swapped(fwi[:, 4 * D_MODEL:])
    b_full = swapped(b_f.reshape(1, FOX_HEADS))
    w_qkv1 = dwi[:, :3 * DIL_QKV]
    w_z1 = dwi[:, 3 * DIL_QKV:]

    n0, = _rows(_rms_fwd, [x], [(D_MODEL, BF16)], name="norm0", tm=tm, bcast=[fox_g])
    qkv0 = _mm(n0, w_qkv0, out_dtype=BF16, name="proj_qkv0")
    z0 = _mm(n0, w_z0, name="proj_z0")
    fl0 = _mm(n0, w_f0, name="proj_f0")
    qx0, kx0 = _cumsum_rows([fl0], name="forget_cumsum", pre=_log_forget, bcast=[b_full], post=_fox_extras,
                            out_dtypes=(BF16, BF16))
    o0, a0, g0 = _fox_fwd(qkv0, qx0, kx0, z0)
    h1 = _mm(g0, fwo, add=x, name="out_proj0")
    u0 = _mm(p0, pwu[0], name="ple_up0")
    t0 = _mm(h1, pwg[0], name="ple_gate0")
    h2, n1 = _rows(_ple_norm, [h1, u0, t0], [(D_MODEL, F32), (D_MODEL, BF16)], name="ple_mix0_norm1", tm=tm,
                   bcast=[dil_g])

    qkv1 = _mm(n1, w_qkv1, name="proj_qkv1")
    z1 = _mm(n1, w_z1, name="proj_z1")
    o1, g1, lse1 = _dil_fwd(qkv1, z1)
    h3 = _mm(g1, dwo, add=h2, name="out_proj1")
    u1 = _mm(p1, pwu[1], name="ple_up1")
    t1 = _mm(h3, pwg[1], name="ple_gate1")

    du1, dt1, dh4, g_fin, loss = _rows(
        _ple_loss, [h3, u1, t1, tgt], [(D_MODEL, BF16), (D_MODEL, BF16), (D_MODEL, F32)], name="ple_mix1_loss_head",
        tm=tm, bcast=[fin_g], acc=[((1, D_MODEL), F32), ((1, LANES), F32)])
    g_up1 = _mm(p1, du1, ta=True, name="grad_ple_up1")
    g_gate1 = _mm(h3, dt1, ta=True, name="grad_ple_gate1")
    dh3 = _mm(dt1, pwg[1], tb=True, add=dh4, name="ple_back1")
    dg1 = _mm(dh3, dwo, tb=True, name="out_back1")
    g_dwo = _mm(g1, dh3, ta=True, name="grad_out1")
    do1, dz1, ld1 = _rows(_gate_bwd, [dg1, o1, z1, lse1], [(D_MODEL, F32), (D_MODEL, BF16), (D_MODEL, F32)],
                          name="gate_bwd1", tm=tm)
    dq1, dk1, dv1 = (lax.empty((S, DIL_QKV), BF16) for _ in range(3))
    for g in range(DIL_GROUPS):
        dq1 = _dil_bwd_q(qkv1, do1, ld1, dq1, g)
        dk1, dv1 = _dil_bwd_kv(qkv1, do1, ld1, dk1, dv1, g)
    g_dwi = _mm_tn_parts(n1, [dq1, dk1, dv1, dz1], name="grad_in1")
    dn1 = _mm_nt_parts([dq1, dk1, dv1, dz1], dwi, name="in_back1")
    dh2, du0, dt0, g_dil = _rows(_norm_ple_bwd, [dn1, h2, dh3, u0, t0], [(D_MODEL, F32), (D_MODEL, BF16), (D_MODEL, BF16)],
                                 name="norm_bwd1_ple_bwd0", tm=tm, bcast=[dil_g], acc=[((1, D_MODEL), F32)])
    g_up0 = _mm(p0, du0, ta=True, name="grad_ple_up0")
    g_gate0 = _mm(h1, dt0, ta=True, name="grad_ple_gate0")
    dh1 = _mm(dt0, pwg[0], tb=True, add=dh2, name="ple_back0")
    dg0 = _mm(dh1, fwo, tb=True, name="out_back0")
    g_fwo = _mm(g0, dh1, ta=True, name="grad_out0")
    do0, dz0, qxa0, dx0 = _rows(_gate_bwd_fox, [dg0, o0, z0, a0], [(D_MODEL, BF16)] * 4, name="gate_bwd0", tm=tm)
    dq0, drx, dk0, dv0, dkx = _fox_bwd(qkv0, do0, qxa0, dx0, kx0)
    dlogf, = _cumsum_rows([drx, dkx], name="forget_cumsum_bwd", reverse=True, pre=_forget_dc)
    df0, g_bf = _rows(_forget_bwd, [dlogf, fl0], [(D_MODEL, BF16)], name="forget_bwd", tm=tm, bcast=[b_full],
                      acc=[((1, D_MODEL), F32)])
    g_fwi_main = _mm_tn_parts(n0, [dq0, dk0, dv0, dz0], name="grad_in0")
    g_fwi_f = _mm(n0, df0, ta=True, name="grad_in0_forget")
    dn0 = _mm_nt_parts([dq0, dk0, dv0, dz0], fwi[:, :4 * D_MODEL], name="in_back0")
    dn0 = _mm(df0, w_f0, tb=True, add=dn0, name="in_back0_forget")
    dx, g_fox = _rows(_rms_bwd, [dn0, x, dh1], [(D_MODEL, F32)], name="norm_bwd0", tm=tm, bcast=[fox_g],
                      acc=[((1, D_MODEL), F32)])
    first_lane = lambda a: a.reshape(-1, nh, 2, FOX_HEAD_DIM)[:, :, ::-1, 0].reshape(-1, FOX_HEADS)
    g_fwi = jnp.concatenate([g_fwi_main, first_lane(g_fwi_f)], axis=1)
    return (loss, dx, (g_fwi, g_fwo, g_dwi, g_dwo, [g_up0, g_up1], [g_gate0, g_gate1]),
            (g_fox, _pad_row(first_lane(g_bf)), g_fin, g_dil))


def kernel(x, p, fox_norm, fox_w_in, fox_b_f, fox_w_out, dil_norm, dil_w_in, dil_w_out, ple_w_up, ple_w_gate, final_norm, loss_target, m_fox_norm, m_fox_w_in, m_fox_b_f, m_fox_w_out, m_dil_norm, m_dil_w_in, m_dil_w_out, m_ple_w_up, m_ple_w_gate, m_final_norm, v_fox_norm, v_fox_w_in, v_fox_b_f, v_fox_w_out, v_dil_norm, v_dil_w_in, v_dil_w_out, v_ple_w_up, v_ple_w_gate, v_final_norm):
    xi, yi, ci = lax.axis_index("x"), lax.axis_index("y"), lax.axis_index("c")
    chip = 2 * xi + yi

    w_flat = _pack_shard(fox_w_in, fox_w_out, dil_w_in, dil_w_out, ple_w_up, ple_w_gate)
    m_flat = _pack_shard(m_fox_w_in, m_fox_w_out, m_dil_w_in, m_dil_w_out, m_ple_w_up, m_ple_w_gate)
    v_flat = _pack_shard(v_fox_w_in, v_fox_w_out, v_dil_w_in, v_dil_w_out, v_ple_w_up, v_ple_w_gate)
    w_bf = w_flat.astype(BF16)
    wall = lax.dynamic_update_slice(_gather_weights(w_bf), w_bf[None], (chip, 0, 0))
    fwi, fwo, dwi, dwo, pwu, pwg = _unpack_gathered(wall)
    gains = _gather_tiles(_pad_rows(_pad_row(dil_norm), 8), "gather_gain").reshape(4, 2, 8, D_MODEL)
    dil_g = gains[:, 0, 0, :256].reshape(1, D_MODEL)

    loss_part, dx, grads, small = _local_step(
        x[0], p[0, 0], p[1, 0], loss_target[0], fox_norm.reshape(1, D_MODEL), dil_g,
        final_norm.reshape(1, D_MODEL), fox_b_f, fwi, fwo, dwi, dwo, pwu, pwg)
    g_fox, g_bf, g_fin, g_dil = small

    g_all = _pack_grads(*grads)
    theirs = _swap_halves(g_all)
    mine = lax.dynamic_slice_in_dim(g_all, ci * FLAT_HALF, FLAT_HALF, axis=1)
    part, = _rows(lambda a, b: a + b, [mine.reshape(4 * FLAT_HALF, D_MODEL), theirs.reshape(4 * FLAT_HALF, D_MODEL)],
                  [(D_MODEL, BF16)], name="pair_sum", tm=FLAT_TILE)
    part = part.reshape(4, FLAT_HALF, D_MODEL)
    own = lax.dynamic_slice_in_dim(part, chip, 1, axis=0)
    by_chip = lax.dynamic_update_slice(_scatter_partials(part), own, (chip, 0, 0))
    half_sum, = _rows(lambda a, b, c, d: ((a.astype(F32) + b.astype(F32)) + c.astype(F32)) + d.astype(F32),
                      [by_chip[s] for s in range(4)], [(D_MODEL, F32)], name="chip_sum", tm=FLAT_TILE)
    other_half = _sibling_half(half_sum)
    g_flat = jnp.where(ci == 0, jnp.concatenate([half_sum, other_half], axis=0),
                       jnp.concatenate([other_half, half_sum], axis=0))

    tile = jnp.concatenate([g_fox, g_bf, g_fin, g_dil, jnp.pad(loss_part, ((0, 0), (0, D_MODEL - LANES))),
                            jnp.zeros((3, D_MODEL), F32)], axis=0)
    tiles = _gather_tiles(tile, "gather_small")
    tot, = _rows(lambda *t: functools.reduce(lambda a, b: a + b, t), [tiles[8 * s:8 * s + 8] for s in range(8)],
                 [(D_MODEL, F32)], name="small_sum", tm=8)
    loss = tot[4, 0]
    g_small = jnp.concatenate([
        tot[0:3],
        _pad_row(lax.dynamic_slice_in_dim(tot[3], chip * 256, 256)), jnp.zeros((4, D_MODEL), F32)], axis=0)
    small_tile = lambda a, b, c, d: jnp.concatenate(
        [_pad_row(a), _pad_row(b), _pad_row(c), _pad_row(d), jnp.zeros((4, D_MODEL), F32)], axis=0)
    w_small = small_tile(fox_norm, fox_b_f, final_norm, dil_norm)
    m_small = small_tile(m_fox_norm, m_fox_b_f, m_final_norm, m_dil_norm)
    v_small = small_tile(v_fox_norm, v_fox_b_f, v_final_norm, v_dil_norm)

    three = [(D_MODEL, F32)] * 3
    d_flat, nm_flat, nv_flat = _rows(_adamw, [w_flat, g_flat, m_flat, v_flat], three, name="adamw", tm=FLAT_TILE)
    d_small, nm_small, nv_small = _rows(_adamw, [w_small, g_small, m_small, v_small], three, name="adamw_small", tm=8)

    def leaves(flat, small_rows):
        fwi_, fwo_, dwi_, dwo_, pwu_, pwg_ = _unpack_shard(flat)
        return (small_rows[0:1], fwi_, small_rows[1:2, :FOX_HEADS], fwo_, small_rows[3:4, :256], dwi_, dwo_,
                pwu_, pwg_, small_rows[2])

    return (loss, dx[None], *leaves(g_flat, g_small), *leaves(d_flat, d_small), *leaves(nm_flat, nm_small),
            *leaves(nv_flat, nv_small))
```

```python
import functools

import numpy as np
import jax
import jax.numpy as jnp
from jax import lax
from jax.experimental import pallas as pl
from jax.experimental.pallas import tpu as pltpu

F32 = jnp.float32
BF16 = jnp.bfloat16

D_MODEL = 1024
PLE_DIM = 256
FOX_HEADS = 16
FOX_HEAD_DIM = 64
DIL_PATTERN = ((128, 1), (512, 4), (2048, 16))
DIL_GROUPS = 3
DIL_HEADS_PER_GROUP = 8
DIL_HEAD_DIM = 128
DIL_WINDOW_STEPS = 128
DIL_QKV = 3072
ALIBI_MAX_EXP = 8.0
RMS_EPS = 1e-6
ADAM_LR, ADAM_B1, ADAM_B2, ADAM_EPS, ADAM_WD, ADAM_STEP = 0.001, 0.9, 0.999, 1e-08, 0.01, 10

LANES = 128
VMEM_LIMIT = 56 * 1024 * 1024
MESH = pl.DeviceIdType.MESH
NEG = -1e30

FLAT_ROWS = (("fox_w_in", 1028, 1040), ("fox_w_out", 256, 256), ("dil_w_in", 2560, 2560), ("dil_w_out", 256, 256),
             ("ple_w_up", 128, 128), ("ple_w_gate", 512, 512))
FLAT_USED = sum(r for _, _, r in FLAT_ROWS)
FLAT_TOTAL = 4864
FLAT_HALF = FLAT_TOTAL // 2
FLAT_TILE = 608


def _cparams(*sem):
    return pltpu.CompilerParams(dimension_semantics=sem, vmem_limit_bytes=VMEM_LIMIT)


def _sigmoid(x):
    return 1.0 / (1.0 + jnp.exp(-x))


def _mm(a, b, *, name, ta=False, tb=False, out_dtype=F32, add=None):
    if ta:
        K, M = a.shape
    else:
        M, K = a.shape
    if tb:
        N, Kb = b.shape
    else:
        Kb, N = b.shape
    assert K == Kb, (a.shape, b.shape)
    tm, tn, tk = min(M, 1024), min(N, 1024), min(K, 1024)
    assert M % tm == 0 and N % tn == 0 and K % tk == 0, (M, N, K)
    nk = K // tk
    dn = (((0 if ta else 1,), (1 if tb else 0,)), ((), ()))

    def body(*refs):
        if add is None:
            a_ref, b_ref, o_ref, acc = refs
        else:
            a_ref, b_ref, add_ref, o_ref, acc = refs
        k = pl.program_id(2)

        @pl.when(k == 0)
        def _():
            acc[...] = jnp.zeros_like(acc)

        acc[...] += lax.dot_general(a_ref[...].astype(BF16), b_ref[...].astype(BF16), dn,
                                    preferred_element_type=F32)

        @pl.when(k == nk - 1)
        def _():
            r = acc[...]
            if add is not None:
                r = r + add_ref[...]
            o_ref[...] = r.astype(out_dtype)

    a_spec = (pl.BlockSpec((tk, tm), lambda i, j, k: (k, i)) if ta
              else pl.BlockSpec((tm, tk), lambda i, j, k: (i, k)))
    b_spec = (pl.BlockSpec((tn, tk), lambda i, j, k: (j, k)) if tb
              else pl.BlockSpec((tk, tn), lambda i, j, k: (k, j)))
    in_specs = [a_spec, b_spec]
    args = [a, b]
    if add is not None:
        in_specs.append(pl.BlockSpec((tm, tn), lambda i, j, k: (i, j)))
        args.append(add)
    return pl.pallas_call(
        body, name=name, grid=(M // tm, N // tn, nk),
        in_specs=in_specs, out_specs=pl.BlockSpec((tm, tn), lambda i, j, k: (i, j)),
        out_shape=jax.ShapeDtypeStruct((M, N), out_dtype),
        scratch_shapes=[pltpu.VMEM((tm, tn), F32)],
        compiler_params=_cparams("parallel", "parallel", "arbitrary"),
    )(*args)


def _mm_nt_parts(a_parts, b, *, name):
    M = a_parts[0].shape[0]
    N, K = b.shape
    tm, tn, tk = min(M, 1024), min(N, 1024), 1024
    nks = [a.shape[1] // tk for a in a_parts]
    offs = [sum(nks[:p]) for p in range(len(nks))]
    nk = sum(nks)
    assert nk * tk == K and M % tm == 0 and N % tn == 0, (M, N, K)
    n_parts = len(a_parts)

    def body(*refs):
        a_refs, b_ref, o_ref, acc = refs[:n_parts], refs[n_parts], refs[n_parts + 1], refs[n_parts + 2]
        k = pl.program_id(2)

        @pl.when(k == 0)
        def _():
            acc[...] = jnp.zeros_like(acc)

        for a_ref, off, n in zip(a_refs, offs, nks):
            @pl.when((k >= off) & (k < off + n))
            def _(a_ref=a_ref):
                acc[...] += lax.dot_general(a_ref[...].astype(BF16), b_ref[...].astype(BF16),
                                            (((1,), (1,)), ((), ())), preferred_element_type=F32)

        @pl.when(k == nk - 1)
        def _():
            o_ref[...] = acc[...]

    a_specs = [pl.BlockSpec((tm, tk), lambda i, j, k, off=off, n=n: (i, jnp.clip(k - off, 0, n - 1)))
               for off, n in zip(offs, nks)]
    return pl.pallas_call(
        body, name=name, grid=(M // tm, N // tn, nk),
        in_specs=a_specs + [pl.BlockSpec((tn, tk), lambda i, j, k: (j, k))],
        out_specs=pl.BlockSpec((tm, tn), lambda i, j, k: (i, j)),
        out_shape=jax.ShapeDtypeStruct((M, N), F32), scratch_shapes=[pltpu.VMEM((tm, tn), F32)],
        compiler_params=_cparams("parallel", "parallel", "arbitrary"),
    )(*a_parts, b)


def _mm_tn_parts(a, b_parts, *, name):
    K, M = a.shape
    tm, tn, tk = min(M, 1024), 1024, min(K, 1024)
    njs = [b.shape[1] // tn for b in b_parts]
    offs = [sum(njs[:p]) for p in range(len(njs))]
    nj, nk = sum(njs), K // tk
    assert M % tm == 0 and K % tk == 0 and all(b.shape[1] % tn == 0 for b in b_parts)
    n_parts = len(b_parts)

    def body(*refs):
        a_ref, b_refs, o_ref, acc = refs[0], refs[1:1 + n_parts], refs[1 + n_parts], refs[2 + n_parts]
        j, k = pl.program_id(1), pl.program_id(2)

        @pl.when(k == 0)
        def _():
            acc[...] = jnp.zeros_like(acc)

        for b_ref, off, n in zip(b_refs, offs, njs):
            @pl.when((j >= off) & (j < off + n))
            def _(b_ref=b_ref):
                acc[...] += lax.dot_general(a_ref[...].astype(BF16), b_ref[...].astype(BF16),
                                            (((0,), (0,)), ((), ())), preferred_element_type=F32)

        @pl.when(k == nk - 1)
        def _():
            o_ref[...] = acc[...]

    def b_spec(off, n):
        def index(i, j, k):
            mine = (j >= off) & (j < off + n)
            return jnp.where(mine, k, 0), jnp.clip(j - off, 0, n - 1)
        return pl.BlockSpec((tk, tn), index)

    return pl.pallas_call(
        body, name=name, grid=(M // tm, nj, nk),
        in_specs=[pl.BlockSpec((tk, tm), lambda i, j, k: (k, i))] + [b_spec(off, n) for off, n in zip(offs, njs)],
        out_specs=pl.BlockSpec((tm, tn), lambda i, j, k: (i, j)),
        out_shape=jax.ShapeDtypeStruct((M, nj * tn), F32), scratch_shapes=[pltpu.VMEM((tm, tn), F32)],
        compiler_params=_cparams("parallel", "parallel", "arbitrary"),
    )(a, *b_parts)


def _rows(fn, ins, outs, *, name, tm, bcast=(), acc=()):
    R = ins[0].shape[0]
    assert R % tm == 0, (R, tm)
    n_in, n_b, n_out, n_acc = len(ins), len(bcast), len(outs), len(acc)

    def body(*refs):
        in_refs = refs[:n_in + n_b]
        out_refs = refs[n_in + n_b:n_in + n_b + n_out]
        acc_refs = refs[n_in + n_b + n_out:]
        res = fn(*[r[...] for r in in_refs])
        if not isinstance(res, (tuple, list)):
            res = (res,)
        for r, v in zip(out_refs, res[:n_out]):
            r[...] = v.astype(r.dtype)
        first = pl.program_id(0) == 0
        for r, v in zip(acc_refs, res[n_out:]):
            @pl.when(first)
            def _(r=r, v=v):
                r[...] = v.astype(r.dtype)

            @pl.when(jnp.logical_not(first))
            def _(r=r, v=v):
                r[...] += v.astype(r.dtype)

    in_specs = [pl.BlockSpec((tm, a.shape[1]), lambda i: (i, 0)) for a in ins]
    in_specs += [pl.BlockSpec(b.shape, lambda i, nd=b.ndim: (0,) * nd) for b in bcast]
    out_specs = [pl.BlockSpec((tm, c), lambda i: (i, 0)) for c, _ in outs]
    out_specs += [pl.BlockSpec(s, lambda i, nd=len(s): (0,) * nd) for s, _ in acc]
    out_shape = [jax.ShapeDtypeStruct((R, c), dt) for c, dt in outs]
    out_shape += [jax.ShapeDtypeStruct(s, dt) for s, dt in acc]
    res = pl.pallas_call(
        body, name=name, grid=(R // tm,), in_specs=in_specs, out_specs=out_specs, out_shape=out_shape,
        compiler_params=_cparams("arbitrary" if acc else "parallel"),
    )(*ins, *bcast)
    return res


def _cumsum_rows(ins, *, name, reverse=False, pre=None, bcast=(), post=None, out_dtypes=(F32,)):
    S, C = ins[0].shape
    tb = 256
    nb = S // tb
    assert S % tb == 0
    n_in = len(ins) + len(bcast)
    n_out = len(out_dtypes)

    def body(*refs):
        in_refs, o_refs, carry = refs[:n_in], refs[n_in:n_in + n_out], refs[n_in + n_out]

        @pl.when(pl.program_id(0) == 0)
        def _():
            carry[...] = jnp.zeros_like(carry)

        xv = in_refs[0][...] if pre is None else pre(*[r[...] for r in in_refs])
        r_ = lax.broadcasted_iota(jnp.int32, (tb, tb), 0)
        c_ = lax.broadcasted_iota(jnp.int32, (tb, tb), 1)
        tri = jnp.where((c_ >= r_) if reverse else (c_ <= r_), 1.0, 0.0).astype(BF16)
        hi = xv.astype(BF16)
        r1 = xv - hi.astype(F32)
        mid = r1.astype(BF16)
        lo = (r1 - mid.astype(F32)).astype(BF16)
        cs = (jnp.dot(tri, hi, preferred_element_type=F32) + jnp.dot(tri, mid, preferred_element_type=F32)
              + jnp.dot(tri, lo, preferred_element_type=F32)) + carry[...]
        outs = (cs,) if post is None else post(cs)
        for o_ref, v in zip(o_refs, outs):
            o_ref[...] = v.astype(o_ref.dtype)
        carry[...] = cs[0:1, :] if reverse else cs[tb - 1:tb, :]

    blk = (lambda i: (nb - 1 - i, 0)) if reverse else (lambda i: (i, 0))
    in_specs = [pl.BlockSpec((tb, C), blk) for _ in ins]
    in_specs += [pl.BlockSpec(b.shape, lambda i, nd=b.ndim: (0,) * nd) for b in bcast]
    return pl.pallas_call(
        body, name=name, grid=(nb,), in_specs=in_specs, out_specs=[pl.BlockSpec((tb, C), blk)] * n_out,
        out_shape=[jax.ShapeDtypeStruct((S, C), dt) for dt in out_dtypes], scratch_shapes=[pltpu.VMEM((1, C), F32)],
        compiler_params=_cparams("arbitrary"),
    )(*ins, *bcast)


def _rms_fwd(x, g):
    r = lax.rsqrt(jnp.mean(x * x, axis=1, keepdims=True) + RMS_EPS)
    return (x * r) * g


def _rms_bwd(dn, x, dres, g):
    r = lax.rsqrt(jnp.mean(x * x, axis=1, keepdims=True) + RMS_EPS)
    xh = x * r
    w = dn * g
    dx = r * (w - xh * jnp.mean(w * xh, axis=1, keepdims=True))
    return dres + dx, jnp.sum(dn * xh, axis=0, keepdims=True)


def _final_stage(h, tgt, g):
    r = lax.rsqrt(jnp.mean(h * h, axis=1, keepdims=True) + RMS_EPS)
    xh = h * r
    diff = xh * g - tgt
    loss = 0.5 * jnp.sum(jnp.mean(diff * diff, axis=1, keepdims=True), axis=0, keepdims=True)
    dy = diff * (1.0 / D_MODEL)
    w = dy * g
    dh = r * (w - xh * jnp.mean(w * xh, axis=1, keepdims=True))
    return dh, jnp.sum(dy * xh, axis=0, keepdims=True), jnp.broadcast_to(loss, (1, LANES))


def _ple_fwd(h, u, t):
    return h + u * _sigmoid(t)


def _ple_bwd(dh, u, t):
    s = _sigmoid(t)
    return dh * s, dh * u * s * (1.0 - s)


def _ple_norm(h, u, t, g):
    h2 = _ple_fwd(h, u, t)
    return h2, _rms_fwd(h2, g)


def _ple_loss(h, u, t, tgt, g):
    dh, g_gain, loss = _final_stage(_ple_fwd(h, u, t), tgt, g)
    du, dt = _ple_bwd(dh, u, t)
    return du, dt, dh, g_gain, loss


def _norm_ple_bwd(dn, x, dres, u, t, g):
    dh, g_gain = _rms_bwd(dn, x, dres, g)
    du, dt = _ple_bwd(dh, u, t)
    return dh, du, dt, g_gain


def _gate_fwd(o, z):
    return o * (z * _sigmoid(z))


def _head_sums(prod):
    tm, width = prod.shape
    cols = [jnp.broadcast_to(jnp.sum(prod[:, b * LANES:(b + 1) * LANES], axis=1, keepdims=True), (tm, LANES))
            for b in range(width // LANES)]
    return jnp.concatenate(cols, axis=1)


def _gate_bwd(dg, o, z, lse):
    s = _sigmoid(z)
    do = dg * (z * s)
    dz = dg * o * (s * (1.0 + z * (1.0 - s)))
    lane = lax.broadcasted_iota(jnp.int32, do.shape, 1) % LANES
    return do, dz, jnp.where(lane < LANES // 2, lse, _head_sums(do * o))


def _log_forget(fl, b):
    u = fl + b
    return jnp.minimum(u, 0.0) - jnp.log(1.0 + jnp.exp(-jnp.abs(u)))


def _adamw(w, g, m, v):
    m = ADAM_B1 * m + (1.0 - ADAM_B1) * g
    v = ADAM_B2 * v + (1.0 - ADAM_B2) * (g * g)
    m_hat = m / (1.0 - ADAM_B1 ** ADAM_STEP)
    v_hat = v / (1.0 - ADAM_B2 ** ADAM_STEP)
    delta = -ADAM_LR * (m_hat / (jnp.sqrt(v_hat) + ADAM_EPS) + ADAM_WD * w)
    return delta, m, v


FOX_TQ = 1024


def _split3(x):
    p1 = x.astype(BF16).astype(F32)
    r = x - p1
    p2 = r.astype(BF16).astype(F32)
    return p1, p2, r - p2


def _lane_in_head(shape):
    return lax.broadcasted_iota(jnp.int32, shape, 1) % FOX_HEAD_DIM


def _query_extras(x):
    lm = _lane_in_head(x.shape)
    p1, p2, p3 = _split3(x)
    return jnp.where(lm == 0, p1, jnp.where(lm == 1, p2, jnp.where(lm == 2, p3, jnp.where(lm < 6, 1.0, 0.0))))


def _key_extras(c):
    lm = _lane_in_head(c.shape)
    p1, p2, p3 = _split3(c)
    return jnp.where(lm < 3, 1.0, jnp.where(lm == 3, -p1, jnp.where(lm == 4, -p2, jnp.where(lm == 5, -p3, 0.0))))


def _fox_extras(c):
    return _query_extras(c), _key_extras(c)


def _swapped_head_sums(prod):
    tm, width = prod.shape
    lane = lax.broadcasted_iota(jnp.int32, (tm, LANES), 1)
    low = lane < FOX_HEAD_DIM
    cols = []
    for b in range(width // LANES):
        blk = prod[:, b * LANES:(b + 1) * LANES]
        sa = jnp.sum(jnp.where(low, blk, 0.0), axis=1, keepdims=True)
        sb = jnp.sum(jnp.where(low, 0.0, blk), axis=1, keepdims=True)
        cols.append(jnp.where(low, sb, sa))
    return jnp.concatenate(cols, axis=1)


def _gate_bwd_fox(dg, o, z, a):
    s = _sigmoid(z)
    do = dg * (z * s)
    dz = dg * o * (s * (1.0 + z * (1.0 - s)))
    lm = _lane_in_head(do.shape)
    d1, d2, d3 = _split3(-_swapped_head_sums(do * o))
    dx = jnp.where(lm == 0, d1, jnp.where(lm == 1, d2, jnp.where(lm == 2, d3, 0.0)))
    return do, dz, _query_extras(a), dx


def _forget_dc(drx, dkx):
    lm = _lane_in_head(drx.shape)
    return jnp.where(lm == 0, drx, 0.0) - jnp.where(lm == 3, dkx, 0.0)


def _forget_bwd(dl, fl, b):
    lm = _lane_in_head(dl.shape)
    width = dl.shape[1]
    both = dl + pltpu.roll(dl, 3, 1) + pltpu.roll(dl, width - 3, 1)
    du = jnp.where(lm == 0, both, 0.0) * (1.0 / (1.0 + jnp.exp(fl + b)))
    return du, jnp.sum(du, axis=0, keepdims=True)


def _fox_fwd(qkv, qx, kx, z):
    S = qkv.shape[0]
    tq = min(FOX_TQ, S)
    nq = S // tq
    nt = (((1,), (1,)), ((), ()))

    def body(q_ref, qx_ref, k_ref, v_ref, kx_ref, z_ref, o_ref, a_ref, g_ref):
        i = pl.program_id(1)
        low = lax.broadcasted_iota(jnp.int32, (tq, LANES), 1) < FOX_HEAD_DIM
        row = lax.broadcasted_iota(jnp.int32, (tq, tq), 0)
        col = lax.broadcasted_iota(jnp.int32, (tq, tq), 1)
        q2, x2 = q_ref[...], qx_ref[...]
        qa = (jnp.where(low, q2, x2), jnp.where(low, x2, q2))

        def step(kb, carry, diag):
            start = pl.multiple_of(kb * tq, tq)
            k2 = k_ref[pl.ds(start, tq), :]
            v2 = v_ref[pl.ds(start, tq), :]
            y2 = kx_ref[pl.ds(start, tq), :]
            one = jnp.ones_like(v2)
            ka = (jnp.where(low, k2, y2), jnp.where(low, y2, k2))
            va = (jnp.where(low, v2, one), jnp.where(low, one, v2))
            out = []
            for hh in range(2):
                m, acc = carry[hh]
                s = lax.dot_general(qa[hh], ka[hh], nt, preferred_element_type=F32)
                if diag:
                    s = jnp.where(col <= row, s, NEG)
                m_new = jnp.maximum(m, jnp.max(s, axis=1, keepdims=True))
                pr = jnp.exp(s - m_new)
                acc = jnp.exp(m - m_new) * acc + jnp.dot(pr.astype(BF16), va[hh], preferred_element_type=F32)
                out.append((m_new, acc))
            return tuple(out)

        init = ((jnp.full((tq, 1), NEG, F32), jnp.zeros((tq, LANES), F32)),) * 2
        carry = lax.fori_loop(0, i, functools.partial(step, diag=False), init)
        (m_a, acc_a), (m_b, acc_b) = step(i, carry, True)
        l_a, l_b = acc_a[:, FOX_HEAD_DIM:FOX_HEAD_DIM + 1], acc_b[:, 0:1]
        xf = x2.astype(F32)
        c_a = xf[:, 64:65] + xf[:, 65:66] + xf[:, 66:67]
        c_b = xf[:, 0:1] + xf[:, 1:2] + xf[:, 2:3]
        o = jnp.where(low, acc_a / l_a, acc_b / l_b)
        o_ref[...] = o
        g_ref[...] = _gate_fwd(o, z_ref[...]).astype(BF16)
        a_ref[...] = jnp.where(low, jnp.broadcast_to(c_b - (m_b + jnp.log(l_b)), (tq, LANES)),
                               jnp.broadcast_to(c_a - (m_a + jnp.log(l_a)), (tq, LANES)))

    blk = lambda cb: pl.BlockSpec((tq, LANES), lambda h, i, cb=cb: (i, cb + h))
    res = lambda cb: pl.BlockSpec((S, LANES), lambda h, i, cb=cb: (0, cb + h))
    return pl.pallas_call(
        body, name="fox_attn_fwd", grid=(FOX_HEADS // 2, nq),
        in_specs=[blk(0), blk(0), res(8), res(16), res(0), blk(0)],
        out_specs=[blk(0), blk(0), blk(0)],
        out_shape=[jax.ShapeDtypeStruct((S, D_MODEL), F32), jax.ShapeDtypeStruct((S, D_MODEL), F32),
                   jax.ShapeDtypeStruct((S, D_MODEL), BF16)],
        compiler_params=_cparams("parallel", "arbitrary"),
    )(qkv, qx, qkv, qkv, kx, z)


def _fox_bwd(qkv, do, qxa, dx, kx):
    S = qkv.shape[0]
    tq = min(FOX_TQ, S)
    nq = S // tq
    nt = (((1,), (1,)), ((), ()))
    tn = (((0,), (0,)), ((), ()))

    def body(q_ref, qx_ref, do_ref, dx_ref, k_ref, v_ref, kx_ref, dq_ref, dr_ref, dk_ref, dv_ref, dkx_ref):
        kb = pl.program_id(1)
        low = lax.broadcasted_iota(jnp.int32, (tq, LANES), 1) < FOX_HEAD_DIM
        row = lax.broadcasted_iota(jnp.int32, (tq, tq), 0)
        col = lax.broadcasted_iota(jnp.int32, (tq, tq), 1)

        @pl.when(kb == 0)
        def _():
            dq_ref[...] = jnp.zeros_like(dq_ref)
            dr_ref[...] = jnp.zeros_like(dr_ref)

        k2, v2, y2 = k_ref[...], v_ref[...], kx_ref[...]
        one = jnp.ones_like(v2)
        ka = (jnp.where(low, k2, y2), jnp.where(low, y2, k2))
        va = (jnp.where(low, v2, one), jnp.where(low, one, v2))

        def step(qb, carry, diag):
            start = pl.multiple_of(qb * tq, tq)
            q2 = q_ref[pl.ds(start, tq), :]
            x2 = qx_ref[pl.ds(start, tq), :]
            d2 = do_ref[pl.ds(start, tq), :]
            e2 = dx_ref[pl.ds(start, tq), :]
            qa = (jnp.where(low, q2, x2), jnp.where(low, x2, q2))
            da = (jnp.where(low, d2, e2), jnp.where(low, e2, d2))
            new, res = [], []
            for hh in range(2):
                dk, dv = carry[hh]
                s = lax.dot_general(qa[hh], ka[hh], nt, preferred_element_type=F32)
                if diag:
                    s = jnp.where(col <= row, s, NEG)
                pr = jnp.exp(s)
                ds = pr * lax.dot_general(da[hh], va[hh], nt, preferred_element_type=F32)
                prb, dsb = pr.astype(BF16), ds.astype(BF16)
                dv = dv + lax.dot_general(prb, da[hh], tn, preferred_element_type=F32)
                dk = dk + lax.dot_general(dsb, qa[hh], tn, preferred_element_type=F32)
                res.append(jnp.dot(dsb, ka[hh], preferred_element_type=F32))
                new.append((dk, dv))
            dq_ref[pl.ds(start, tq), :] += jnp.where(low, res[0], res[1])
            dr_ref[pl.ds(start, tq), :] += jnp.where(low, res[1], res[0])
            return tuple(new)

        init = ((jnp.zeros((tq, LANES), F32), jnp.zeros((tq, LANES), F32)),) * 2
        carry = step(kb, init, True)
        (dk_a, dv_a), (dk_b, dv_b) = lax.fori_loop(kb + 1, nq, functools.partial(step, diag=False), carry)
        dk_ref[...] = jnp.where(low, dk_a, dk_b).astype(BF16)
        dv_ref[...] = jnp.where(low, dv_a, dv_b).astype(BF16)
        dkx_ref[...] = jnp.where(low, dk_b, dk_a)

        @pl.when(kb == nq - 1)
        def _():
            dq_ref[...] = dq_ref[...] * (FOX_HEAD_DIM ** -0.5)

    res = lambda cb: pl.BlockSpec((S, LANES), lambda h, k, cb=cb: (0, cb + h))
    blk = lambda cb: pl.BlockSpec((tq, LANES), lambda h, k, cb=cb: (k, cb + h))
    f32, b16 = jax.ShapeDtypeStruct((S, D_MODEL), F32), jax.ShapeDtypeStruct((S, D_MODEL), BF16)
    return pl.pallas_call(
        body, name="fox_attn_bwd", grid=(FOX_HEADS // 2, nq),
        in_specs=[res(0), res(0), res(0), res(0), blk(8), blk(16), blk(0)],
        out_specs=[res(0), res(0), blk(0), blk(0), blk(0)],
        out_shape=[f32, f32, b16, b16, f32],
        compiler_params=_cparams("parallel", "arbitrary"),
    )(qkv, qxa, do, dx, qkv, qkv, kx)


def _alibi_slopes():
    n = DIL_GROUPS * DIL_HEADS_PER_GROUP
    s = np.float32(2.0) ** (np.float32(-ALIBI_MAX_EXP) * np.arange(1, n + 1, dtype=np.float32) / np.float32(n))
    return s.astype(np.float32).reshape(DIL_GROUPS, DIL_HEADS_PER_GROUP)


W = DIL_WINDOW_STEPS
DIL_SCALE = DIL_HEAD_DIM ** -0.5


DIL_ROWS = 2048
NT = (((1,), (1,)), ((), ()))
TN = (((0,), (0,)), ((), ()))
QKV_BLOCKS = DIL_QKV // LANES


def _slope_table():
    t = np.zeros((DIL_HEADS_PER_GROUP, 8, LANES), np.float32)
    t[:, :DIL_GROUPS, :] = _alibi_slopes().T[:, :, None]
    return jnp.asarray(t)


def _to_phases(x, d):
    if d == 1:
        return x
    n = x.shape[0] // d
    return jnp.swapaxes(x.reshape(n, d, LANES), 0, 1).reshape(n * d, LANES)


def _from_phases(y, d):
    if d == 1:
        return y
    n = y.shape[0] // d
    return jnp.swapaxes(y.reshape(d, n, LANES), 0, 1).reshape(n * d, LANES)


def _for_phases(d, unit):
    if d == 1:
        unit(0)
    else:
        lax.fori_loop(0, d, lambda r, c: (unit(r), c)[1], 0)


def _unit_rows(r, n, b):
    return pl.ds(pl.multiple_of(r * n + b * W, W), W)


def _window_geometry(shape, q_axis_offset):
    i_ = lax.broadcasted_iota(jnp.int32, shape, 0)
    j_ = lax.broadcasted_iota(jnp.int32, shape, 1)
    dist = q_axis_offset + i_ - j_
    return i_, j_, dist, (dist >= 0) & (dist <= W)


def _dil_fwd(qkv, z):
    S = qkv.shape[0]
    rb = min(DIL_ROWS, S)
    nb = S // rb
    spans = [W * d for _, d in DIL_PATTERN]

    def body(*refs):
        slope_ref = refs[0]
        q, kc, kp, vc, vp = refs[1:4], refs[4:7], refs[7:10], refs[10:13], refs[13:16]
        z_ref, o_ref, g_ref, l_ref = refs[16:20]
        og, lg = refs[20:23], refs[23:26]
        qP, kcP, vcP = refs[26:29], refs[29:32], refs[32:35]
        kpP, vpP = refs[35:38], refs[38:41]
        i = pl.program_id(1)
        _, j_, dist, inwin = _window_geometry((W, 2 * W), W)
        distf = dist.astype(F32)
        for g, (_, d) in enumerate(DIL_PATTERN):
            n = rb // d
            bias = -(slope_ref[0, g:g + 1, 0:1] * float(d)) * distf
            for src, dst in ((q[g], qP[g]), (kc[g], kcP[g]), (vc[g], vcP[g]), (kp[g], kpP[g]), (vp[g], vpP[g])):
                dst[...] = _to_phases(src[...], d).astype(BF16)
            for b in range(n // W):
                def unit(r, g=g, n=n, b=b, bias=bias):
                    rows = _unit_rows(r, n, b)
                    if b > 0:
                        before = _unit_rows(r, n, b - 1)
                        kprev, vprev, valid = kcP[g][before, :], vcP[g][before, :], inwin
                    else:
                        before = _unit_rows(r, W, 0)
                        kprev, vprev, valid = kpP[g][before, :], vpP[g][before, :], inwin & ((j_ >= W) | (i > 0))
                    k2 = jnp.concatenate([kprev, kcP[g][rows, :]], axis=0)
                    v2 = jnp.concatenate([vprev, vcP[g][rows, :]], axis=0)
                    s = lax.dot_general(qP[g][rows, :], k2, NT, preferred_element_type=F32)
                    s = jnp.where(valid, s * DIL_SCALE + bias, NEG)
                    m = jnp.max(s, axis=1, keepdims=True)
                    pr = jnp.exp(s - m)
                    l = jnp.sum(pr, axis=1, keepdims=True)
                    og[g][rows, :] = jnp.dot(pr.astype(BF16), v2, preferred_element_type=F32) / l
                    lg[g][rows, :] = jnp.broadcast_to(m + jnp.log(l), (W, LANES))

                _for_phases(d, unit)
            if d > 1:
                og[g][...] = _from_phases(og[g][...], d)
                lg[g][...] = _from_phases(lg[g][...], d)

        def mix(cix, c):
            sl = pl.ds(pl.multiple_of(cix * 256, 256), 256)
            l1, l2, l3 = lg[0][sl, :], lg[1][sl, :], lg[2][sl, :]
            m = jnp.maximum(jnp.maximum(l1, l2), l3)
            e1, e2, e3 = jnp.exp(l1 - m), jnp.exp(l2 - m), jnp.exp(l3 - m)
            tot = e1 + e2 + e3
            o = (e1 * og[0][sl, :] + e2 * og[1][sl, :] + e3 * og[2][sl, :]) / tot
            o_ref[sl, :] = o
            g_ref[sl, :] = _gate_fwd(o, z_ref[sl, :]).astype(BF16)
            l_ref[sl, :] = m + jnp.log(tot)
            return c

        lax.fori_loop(0, rb // 256, mix, 0)

    cur = lambda off, g: pl.BlockSpec((rb, LANES), lambda h, i: (i, off * QKV_BLOCKS + g * DIL_HEADS_PER_GROUP + h))
    prev = lambda off, g: pl.BlockSpec(
        (spans[g], LANES),
        lambda h, i: (jnp.maximum(i * (rb // spans[g]) - 1, 0), off * QKV_BLOCKS + g * DIL_HEADS_PER_GROUP + h))
    row = pl.BlockSpec((rb, LANES), lambda h, i: (i, h))
    groups = range(DIL_GROUPS)
    in_specs = [pl.BlockSpec((1, 8, LANES), lambda h, i: (h, 0, 0))]
    in_specs += [cur(0, g) for g in groups] + [cur(1, g) for g in groups] + [prev(1, g) for g in groups]
    in_specs += [cur(2, g) for g in groups] + [prev(2, g) for g in groups] + [row]
    f32 = jax.ShapeDtypeStruct((S, D_MODEL), F32)
    scratch = [pltpu.VMEM((rb, LANES), F32)] * 6 + [pltpu.VMEM((rb, LANES), BF16)] * 9
    scratch += [pltpu.VMEM((spans[g], LANES), BF16) for g in groups] * 2
    return pl.pallas_call(
        body, name="dil_attn_fwd", grid=(DIL_HEADS_PER_GROUP, nb), in_specs=in_specs,
        out_specs=[row, row, row], out_shape=[f32, jax.ShapeDtypeStruct((S, D_MODEL), BF16), f32],
        scratch_shapes=scratch, compiler_params=_cparams("parallel", "parallel"),
    )(_slope_table(), *([qkv] * 15), z)


def _dil_bwd_q(qkv, do, ld, buf, g):
    S = qkv.shape[0]
    rb = min(DIL_ROWS, S)
    nb = S // rb
    d = DIL_PATTERN[g][1]
    span = W * d
    n = rb // d

    def body(slope_ref, q_ref, kc, kp, vc, vp, do_ref, ld_ref, buf_ref, dq_ref, qP, kcP, vcP, doP, kpP, vpP, ldP, dqP):
        i = pl.program_id(1)
        _, j_, dist, inwin = _window_geometry((W, 2 * W), W)
        bias = -(slope_ref[0, g:g + 1, 0:1] * float(d)) * dist.astype(F32)
        for src, dst in ((q_ref, qP), (kc, kcP), (vc, vcP), (do_ref, doP), (kp, kpP), (vp, vpP)):
            dst[...] = _to_phases(src[...], d).astype(BF16)
        ldP[...] = _to_phases(ld_ref[...], d)
        for b in range(n // W):
            def unit(r, b=b):
                rows = _unit_rows(r, n, b)
                if b > 0:
                    before = _unit_rows(r, n, b - 1)
                    kprev, vprev, valid = kcP[before, :], vcP[before, :], inwin
                else:
                    before = _unit_rows(r, W, 0)
                    kprev, vprev, valid = kpP[before, :], vpP[before, :], inwin & ((j_ >= W) | (i > 0))
                k2 = jnp.concatenate([kprev, kcP[rows, :]], axis=0)
                v2 = jnp.concatenate([vprev, vcP[rows, :]], axis=0)
                ldu = ldP[rows, :]
                s = lax.dot_general(qP[rows, :], k2, NT, preferred_element_type=F32)
                s = jnp.where(valid, s * DIL_SCALE + bias - ldu[:, 0:1], NEG)
                pr = jnp.exp(s)
                dp = lax.dot_general(doP[rows, :], v2, NT, preferred_element_type=F32)
                ds = pr * (dp - ldu[:, LANES // 2:LANES // 2 + 1])
                dqP[rows, :] = jnp.dot(ds.astype(BF16), k2, preferred_element_type=F32) * DIL_SCALE

            _for_phases(d, unit)
        dq_ref[...] = _from_phases(dqP[...], d).astype(BF16)

    col = lambda off: off * QKV_BLOCKS + g * DIL_HEADS_PER_GROUP
    cur = lambda off: pl.BlockSpec((rb, LANES), lambda h, i: (i, col(off) + h))
    prev = lambda off: pl.BlockSpec((span, LANES), lambda h, i: (jnp.maximum(i * (rb // span) - 1, 0), col(off) + h))
    row = pl.BlockSpec((rb, LANES), lambda h, i: (i, h))
    return pl.pallas_call(
        body, name=f"dil_attn_bwd_q_g{g}", grid=(DIL_HEADS_PER_GROUP, nb),
        in_specs=[pl.BlockSpec((1, 8, LANES), lambda h, i: (h, 0, 0)), cur(0), cur(1), prev(1), cur(2), prev(2),
                  row, row, pl.BlockSpec(memory_space=pl.ANY)],
        out_specs=pl.BlockSpec((rb, LANES), lambda h, i: (i, g * DIL_HEADS_PER_GROUP + h)),
        out_shape=jax.ShapeDtypeStruct(buf.shape, buf.dtype), input_output_aliases={8: 0},
        scratch_shapes=[pltpu.VMEM((rb, LANES), BF16)] * 4 + [pltpu.VMEM((span, LANES), BF16)] * 2
        + [pltpu.VMEM((rb, LANES), F32)] * 2,
        compiler_params=_cparams("parallel", "parallel"),
    )(_slope_table(), qkv, qkv, qkv, qkv, qkv, do, ld, buf)


def _dil_bwd_kv(qkv, do, ld, bufk, bufv, g):
    S = qkv.shape[0]
    rb = min(DIL_ROWS, S)
    nb = S // rb
    d = DIL_PATTERN[g][1]
    span = W * d
    n = rb // d
    nub = n // W

    def body(slope_ref, k_ref, v_ref, qc, qn, doc, don, ldc, ldn, bufk_ref, bufv_ref, dk_ref, dv_ref,
             kP, vP, qcP, docP, qnP, donP, ldcP, ldnP, dkP, dvP):
        i = pl.program_id(1)
        i_, _, dist, inwin = _window_geometry((2 * W, W), 0)
        bias = -(slope_ref[0, g:g + 1, 0:1] * float(d)) * dist.astype(F32)
        for src, dst in ((k_ref, kP), (v_ref, vP), (qc, qcP), (doc, docP), (qn, qnP), (don, donP)):
            dst[...] = _to_phases(src[...], d).astype(BF16)
        ldcP[...] = _to_phases(ldc[...], d)
        ldnP[...] = _to_phases(ldn[...], d)
        for b in range(nub):
            def unit(r, b=b):
                rows = _unit_rows(r, n, b)
                if b < nub - 1:
                    after = _unit_rows(r, n, b + 1)
                    qnext, donext, ldnext, valid = qcP[after, :], docP[after, :], ldcP[after, :], inwin
                else:
                    after = _unit_rows(r, W, 0)
                    qnext, donext, ldnext = qnP[after, :], donP[after, :], ldnP[after, :]
                    valid = inwin & ((i_ < W) | (i < nb - 1))
                q2 = jnp.concatenate([qcP[rows, :], qnext], axis=0)
                do2 = jnp.concatenate([docP[rows, :], donext], axis=0)
                ld2 = jnp.concatenate([ldcP[rows, :], ldnext], axis=0)
                s = lax.dot_general(q2, kP[rows, :], NT, preferred_element_type=F32)
                s = jnp.where(valid, s * DIL_SCALE + bias - ld2[:, 0:1], NEG)
                pr = jnp.exp(s)
                dp = lax.dot_general(do2, vP[rows, :], NT, preferred_element_type=F32)
                ds = pr * (dp - ld2[:, LANES // 2:LANES // 2 + 1])
                dvP[rows, :] = lax.dot_general(pr.astype(BF16), do2, TN, preferred_element_type=F32)
                dkP[rows, :] = lax.dot_general(ds.astype(BF16), q2, TN, preferred_element_type=F32) * DIL_SCALE

            _for_phases(d, unit)
        dk_ref[...] = _from_phases(dkP[...], d).astype(BF16)
        dv_ref[...] = _from_phases(dvP[...], d).astype(BF16)

    col = lambda off: off * QKV_BLOCKS + g * DIL_HEADS_PER_GROUP
    cur = lambda off: pl.BlockSpec((rb, LANES), lambda h, i: (i, col(off) + h))
    nxt_blk = lambda i: jnp.minimum((i + 1) * nub, S // span - 1)
    nxt = lambda off: pl.BlockSpec((span, LANES), lambda h, i: (nxt_blk(i), col(off) + h))
    row = pl.BlockSpec((rb, LANES), lambda h, i: (i, h))
    row_nxt = pl.BlockSpec((span, LANES), lambda h, i: (nxt_blk(i), h))
    any_ = pl.BlockSpec(memory_space=pl.ANY)
    out = pl.BlockSpec((rb, LANES), lambda h, i: (i, g * DIL_HEADS_PER_GROUP + h))
    return pl.pallas_call(
        body, name=f"dil_attn_bwd_kv_g{g}", grid=(DIL_HEADS_PER_GROUP, nb),
        in_specs=[pl.BlockSpec((1, 8, LANES), lambda h, i: (h, 0, 0)), cur(1), cur(2), cur(0), nxt(0),
                  row, row_nxt, row, row_nxt, any_, any_],
        out_specs=[out, out],
        out_shape=[jax.ShapeDtypeStruct(bufk.shape, bufk.dtype), jax.ShapeDtypeStruct(bufv.shape, bufv.dtype)],
        input_output_aliases={9: 0, 10: 1},
        scratch_shapes=[pltpu.VMEM((rb, LANES), BF16)] * 4 + [pltpu.VMEM((span, LANES), BF16)] * 2
        + [pltpu.VMEM((rb, LANES), F32), pltpu.VMEM((span, LANES), F32)] + [pltpu.VMEM((rb, LANES), F32)] * 2,
        compiler_params=_cparams("parallel", "parallel"),
    )(_slope_table(), qkv, qkv, qkv, qkv, do, do, ld, ld, bufk, bufv)


ANY = pl.BlockSpec(memory_space=pl.ANY)


def _place():
    x, y, c = lax.axis_index("x"), lax.axis_index("y"), lax.axis_index("c")
    chips = [(1 - x, y), (x, 1 - y), (1 - x, 1 - y)]
    return x, y, c, chips


def _gather_weights(wb):
    R = wb.shape[0]
    H = R // 2

    def body(w_ref, out_ref, send1, recv1, send2, recv2):
        x, y, c, chips = _place()
        me = 2 * x + y
        sib = (x, y, 1 - c)
        half = pl.ds(c * H, H)
        first = [pltpu.make_async_remote_copy(
            src_ref=w_ref.at[half], dst_ref=out_ref.at[me, half], send_sem=send1.at[k], recv_sem=recv1.at[k],
            device_id=(*chip, c), device_id_type=MESH) for k, chip in enumerate(chips)]
        for cp in first:
            cp.start()
        passed = []
        for k, (cx, cy) in enumerate(chips):
            slot = out_ref.at[2 * cx + cy, half]
            pltpu.make_async_remote_copy(src_ref=slot, dst_ref=slot, send_sem=send1.at[k], recv_sem=recv1.at[k],
                                         device_id=(cx, cy, c), device_id_type=MESH).wait_recv()
            cp = pltpu.make_async_remote_copy(src_ref=slot, dst_ref=slot, send_sem=send2.at[k], recv_sem=recv2.at[k],
                                              device_id=sib, device_id_type=MESH)
            cp.start()
            passed.append(cp)
        for k, (cx, cy) in enumerate(chips):
            slot = out_ref.at[2 * cx + cy, pl.ds((1 - c) * H, H)]
            pltpu.make_async_remote_copy(src_ref=slot, dst_ref=slot, send_sem=send2.at[k], recv_sem=recv2.at[k],
                                         device_id=sib, device_id_type=MESH).wait_recv()
        for cp in first + passed:
            cp.wait_send()

    return pl.pallas_call(
        body, name="gather_weights", in_specs=[ANY], out_specs=ANY,
        out_shape=jax.ShapeDtypeStruct((4, R, D_MODEL), wb.dtype),
        scratch_shapes=[pltpu.SemaphoreType.DMA((3,)), pltpu.SemaphoreType.DMA((3,)),
                        pltpu.SemaphoreType.DMA((3,)), pltpu.SemaphoreType.DMA((3,))],
    )(wb)


def _swap_halves(g):
    H = g.shape[1] // 2

    def body(g_ref, out_ref, send, recv):
        x, y, c, _ = _place()
        sib = (x, y, 1 - c)
        cps = [pltpu.make_async_remote_copy(
            src_ref=g_ref.at[s, pl.ds((1 - c) * H, H)], dst_ref=out_ref.at[s], send_sem=send.at[s],
            recv_sem=recv.at[s], device_id=sib, device_id_type=MESH) for s in range(4)]
        for cp in cps:
            cp.start()
        for cp in cps:
            cp.wait()

    return pl.pallas_call(
        body, name="swap_halves", in_specs=[ANY], out_specs=ANY,
        out_shape=jax.ShapeDtypeStruct((4, H, D_MODEL), g.dtype),
        scratch_shapes=[pltpu.SemaphoreType.DMA((4,)), pltpu.SemaphoreType.DMA((4,))],
    )(g)


def _scatter_partials(pb):
    def body(p_ref, out_ref, send, recv):
        x, y, c, chips = _place()
        me = 2 * x + y
        cps = [pltpu.make_async_remote_copy(
            src_ref=p_ref.at[2 * cx + cy], dst_ref=out_ref.at[me], send_sem=send.at[k], recv_sem=recv.at[k],
            device_id=(cx, cy, c), device_id_type=MESH) for k, (cx, cy) in enumerate(chips)]
        for cp in cps:
            cp.start()
        for k, (cx, cy) in enumerate(chips):
            slot = out_ref.at[2 * cx + cy]
            pltpu.make_async_remote_copy(src_ref=slot, dst_ref=slot, send_sem=send.at[k], recv_sem=recv.at[k],
                                         device_id=(cx, cy, c), device_id_type=MESH).wait_recv()
        for cp in cps:
            cp.wait_send()

    return pl.pallas_call(
        body, name="scatter_partials", in_specs=[ANY], out_specs=ANY,
        out_shape=jax.ShapeDtypeStruct(pb.shape, pb.dtype),
        scratch_shapes=[pltpu.SemaphoreType.DMA((3,)), pltpu.SemaphoreType.DMA((3,))],
    )(pb)


def _sibling_half(f):
    def body(f_ref, out_ref, send, recv):
        x, y, c, _ = _place()
        cp = pltpu.make_async_remote_copy(src_ref=f_ref, dst_ref=out_ref, send_sem=send, recv_sem=recv,
                                          device_id=(x, y, 1 - c), device_id_type=MESH)
        cp.start()
        cp.wait()

    return pl.pallas_call(
        body, name="sibling_half", in_specs=[ANY], out_specs=ANY,
        out_shape=jax.ShapeDtypeStruct(f.shape, f.dtype),
        scratch_shapes=[pltpu.SemaphoreType.DMA, pltpu.SemaphoreType.DMA],
    )(f)


def _gather_tiles(tile, name):
    m_per = tile.shape[0]

    def body(x_ref, out_ref, send_sems, recv_sems, local_sem):
        x, y, c, chips = _place()
        me, sibling = (x, y, c), (x, y, 1 - c)

        def rows(px, py, pc):
            return out_ref.at[pl.ds((4 * px + 2 * py + pc) * m_per, m_per), :]

        def copy(k, block, to, src=None):
            return pltpu.make_async_remote_copy(
                src_ref=rows(*block) if src is None else src, dst_ref=rows(*block),
                send_sem=send_sems.at[k], recv_sem=recv_sems.at[k], device_id=to, device_id_type=MESH)

        mine = pltpu.make_async_copy(x_ref, rows(*me), local_sem)
        mine.start()
        first = [copy(0, me, sibling, src=x_ref)]
        first += [copy(1 + j, me, (*chip, c), src=x_ref) for j, chip in enumerate(chips)]
        for cp in first:
            cp.start()
        passed = [copy(4 + j, (*chip, c), sibling) for j, chip in enumerate(chips)]
        for j, chip in enumerate(chips):
            copy(1 + j, (*chip, c), me).wait_recv()
            passed[j].start()
        copy(0, sibling, me).wait_recv()
        for j, chip in enumerate(chips):
            copy(4 + j, (*chip, 1 - c), me).wait_recv()
        for cp in first + passed:
            cp.wait_send()
        mine.wait()

    return pl.pallas_call(
        body, name=name, out_shape=jax.ShapeDtypeStruct((8 * m_per, D_MODEL), tile.dtype),
        in_specs=[pl.BlockSpec(memory_space=pltpu.VMEM)], out_specs=pl.BlockSpec(memory_space=pltpu.VMEM),
        scratch_shapes=[pltpu.SemaphoreType.DMA((7,)), pltpu.SemaphoreType.DMA((7,)), pltpu.SemaphoreType.DMA],
    )(tile)


def _pad_rows(a, rows):
    return jnp.pad(a, ((0, rows - a.shape[0]), (0, 0)))


def _pad_row(v):
    v = v.reshape(1, -1)
    return jnp.pad(v, ((0, 0), (0, D_MODEL - v.shape[1])))


def _pack_shard(fwi, fwo, dwi, dwo, pwu, pwg):
    parts = [_pad_rows(fwi.reshape(1028, D_MODEL), 1040), fwo.reshape(256, D_MODEL), dwi.reshape(2560, D_MODEL),
             dwo.reshape(256, D_MODEL), pwu.reshape(128, D_MODEL), pwg.reshape(512, D_MODEL),
             jnp.zeros((FLAT_TOTAL - FLAT_USED, D_MODEL), fwi.dtype)]
    return jnp.concatenate(parts, axis=0)


def _unpack_shard(flat):
    out, r0 = {}, 0
    for name, rows, slot in FLAT_ROWS:
        out[name] = flat[r0:r0 + rows]
        r0 += slot
    return (out["fox_w_in"].reshape(1, D_MODEL, 1028), out["fox_w_out"].reshape(1, 256, D_MODEL),
            out["dil_w_in"].reshape(1, D_MODEL, 2560), out["dil_w_out"].reshape(1, 256, D_MODEL),
            out["ple_w_up"].reshape(2, PLE_DIM, 256), out["ple_w_gate"].reshape(2, 256, D_MODEL))


def _unpack_gathered(wall):
    out, r0 = {}, 0
    for name, rows, slot in FLAT_ROWS:
        out[name] = wall[:, r0:r0 + rows]
        r0 += slot
    cols = lambda a, n: a.reshape(4, D_MODEL, n).transpose(1, 0, 2).reshape(D_MODEL, 4 * n)
    fwi = cols(out["fox_w_in"], 1028)
    dwi = cols(out["dil_w_in"], 2560)
    fwo = out["fox_w_out"].reshape(D_MODEL, D_MODEL)
    dwo = out["dil_w_out"].reshape(D_MODEL, D_MODEL)
    pwu4 = out["ple_w_up"].reshape(4, 2, PLE_DIM, 256)
    pwg4 = out["ple_w_gate"].reshape(4, 2, 256, D_MODEL)
    pwu = [pwu4[:, i].transpose(1, 0, 2).reshape(PLE_DIM, D_MODEL) for i in range(2)]
    pwg = [pwg4[:, i].reshape(D_MODEL, D_MODEL) for i in range(2)]
    return fwi, fwo, dwi, dwo, pwu, pwg


def _pack_grads(gfwi, gfwo, gdwi, gdwo, gpwu, gpwg):
    cols = lambda a, n: a.reshape(D_MODEL, 4, n).transpose(1, 0, 2).reshape(4, n, D_MODEL)
    up = jnp.stack([a.reshape(PLE_DIM, 4, 256).transpose(1, 0, 2) for a in gpwu], axis=1)
    gate = jnp.stack([a.reshape(4, 256, D_MODEL) for a in gpwg], axis=1)
    parts = [jnp.pad(cols(gfwi, 1028), ((0, 0), (0, 12), (0, 0))), gfwo.reshape(4, 256, D_MODEL), cols(gdwi, 2560),
             gdwo.reshape(4, 256, D_MODEL), up.reshape(4, 128, D_MODEL), gate.reshape(4, 512, D_MODEL),
             jnp.zeros((4, FLAT_TOTAL - FLAT_USED, D_MODEL), F32)]
    return jnp.concatenate(parts, axis=1)


def _local_step(x, p0, p1, tgt, fox_g, dil_g, fin_g, b_f, fwi, fwo, dwi, dwo, pwu, pwg):
    S = x.shape[0]
    tm = min(256, S)
    nh = FOX_HEADS // 2
    w_qkv0 = jnp.concatenate([fwi[:, :D_MODEL] * 0.125, fwi[:, D_MODEL:3 * D_MODEL]], axis=1)
    w_z0 = fwi[:, 3 * D_MODEL:4 * D_MODEL]
    swapped = lambda a: jnp.repeat(a.reshape(-1, nh, 2)[:, :, ::-1].reshape(-1, FOX_HEADS), FOX_HEAD_DIM, axis=1)
    w_f0 = swapped(fwi[:, 4 * D_MODEL:])
    b_full = swapped(b_f.reshape(1, FOX_HEADS))
    w_qkv1 = dwi[:, :3 * DIL_QKV]
    w_z1 = dwi[:, 3 * DIL_QKV:]

    n0, = _rows(_rms_fwd, [x], [(D_MODEL, BF16)], name="norm0", tm=tm, bcast=[fox_g])
    qkv0 = _mm(n0, w_qkv0, out_dtype=BF16, name="proj_qkv0")
    z0 = _mm(n0, w_z0, name="proj_z0")
    fl0 = _mm(n0, w_f0, name="proj_f0")
    qx0, kx0 = _cumsum_rows([fl0], name="forget_cumsum", pre=_log_forget, bcast=[b_full], post=_fox_extras,
                            out_dtypes=(BF16, BF16))
    o0, a0, g0 = _fox_fwd(qkv0, qx0, kx0, z0)
    h1 = _mm(g0, fwo, add=x, name="out_proj0")
    u0 = _mm(p0, pwu[0], name="ple_up0")
    t0 = _mm(h1, pwg[0], name="ple_gate0")
    h2, n1 = _rows(_ple_norm, [h1, u0, t0], [(D_MODEL, F32), (D_MODEL, BF16)], name="ple_mix0_norm1", tm=tm,
                   bcast=[dil_g])

    qkv1 = _mm(n1, w_qkv1, name="proj_qkv1")
    z1 = _mm(n1, w_z1, name="proj_z1")
    o1, g1, lse1 = _dil_fwd(qkv1, z1)
    h3 = _mm(g1, dwo, add=h2, name="out_proj1")
    u1 = _mm(p1, pwu[1], name="ple_up1")
    t1 = _mm(h3, pwg[1], name="ple_gate1")

    du1, dt1, dh4, g_fin, loss = _rows(
        _ple_loss, [h3, u1, t1, tgt], [(D_MODEL, BF16), (D_MODEL, BF16), (D_MODEL, F32)], name="ple_mix1_loss_head",
        tm=tm, bcast=[fin_g], acc=[((1, D_MODEL), F32), ((1, LANES), F32)])
    g_up1 = _mm(p1, du1, ta=True, name="grad_ple_up1")
    g_gate1 = _mm(h3, dt1, ta=True, name="grad_ple_gate1")
    dh3 = _mm(dt1, pwg[1], tb=True, add=dh4, name="ple_back1")
    dg1 = _mm(dh3, dwo, tb=True, name="out_back1")
    g_dwo = _mm(g1, dh3, ta=True, name="grad_out1")
    do1, dz1, ld1 = _rows(_gate_bwd, [dg1, o1, z1, lse1], [(D_MODEL, F32), (D_MODEL, BF16), (D_MODEL, F32)],
                          name="gate_bwd1", tm=tm)
    dq1, dk1, dv1 = (lax.empty((S, DIL_QKV), BF16) for _ in range(3))
    for g in range(DIL_GROUPS):
        dq1 = _dil_bwd_q(qkv1, do1, ld1, dq1, g)
        dk1, dv1 = _dil_bwd_kv(qkv1, do1, ld1, dk1, dv1, g)
    g_dwi = _mm_tn_parts(n1, [dq1, dk1, dv1, dz1], name="grad_in1")
    dn1 = _mm_nt_parts([dq1, dk1, dv1, dz1], dwi, name="in_back1")
    dh2, du0, dt0, g_dil = _rows(_norm_ple_bwd, [dn1, h2, dh3, u0, t0], [(D_MODEL, F32), (D_MODEL, BF16), (D_MODEL, BF16)],
                                 name="norm_bwd1_ple_bwd0", tm=tm, bcast=[dil_g], acc=[((1, D_MODEL), F32)])
    g_up0 = _mm(p0, du0, ta=True, name="grad_ple_up0")
    g_gate0 = _mm(h1, dt0, ta=True, name="grad_ple_gate0")
    dh1 = _mm(dt0, pwg[0], tb=True, add=dh2, name="ple_back0")
    dg0 = _mm(dh1, fwo, tb=True, name="out_back0")
    g_fwo = _mm(g0, dh1, ta=True, name="grad_out0")
    do0, dz0, qxa0, dx0 = _rows(_gate_bwd_fox, [dg0, o0, z0, a0], [(D_MODEL, BF16)] * 4, name="gate_bwd0", tm=tm)
    dq0, drx, dk0, dv0, dkx = _fox_bwd(qkv0, do0, qxa0, dx0, kx0)
    dlogf, = _cumsum_rows([drx, dkx], name="forget_cumsum_bwd", reverse=True, pre=_forget_dc)
    df0, g_bf = _rows(_forget_bwd, [dlogf, fl0], [(D_MODEL, BF16)], name="forget_bwd", tm=tm, bcast=[b_full],
                      acc=[((1, D_MODEL), F32)])
    g_fwi_main = _mm_tn_parts(n0, [dq0, dk0, dv0, dz0], name="grad_in0")
    g_fwi_f = _mm(n0, df0, ta=True, name="grad_in0_forget")
    dn0 = _mm_nt_parts([dq0, dk0, dv0, dz0], fwi[:, :4 * D_MODEL], name="in_back0")
    dn0 = _mm(df0, w_f0, tb=True, add=dn0, name="in_back0_forget")
    dx, g_fox = _rows(_rms_bwd, [dn0, x, dh1], [(D_MODEL, F32)], name="norm_bwd0", tm=tm, bcast=[fox_g],
                      acc=[((1, D_MODEL), F32)])
    first_lane = lambda a: a.reshape(-1, nh, 2, FOX_HEAD_DIM)[:, :, ::-1, 0].reshape(-1, FOX_HEADS)
    g_fwi = jnp.concatenate([g_fwi_main, first_lane(g_fwi_f)], axis=1)
    return (loss, dx, (g_fwi, g_fwo, g_dwi, g_dwo, [g_up0, g_up1], [g_gate0, g_gate1]),
            (g_fox, _pad_row(first_lane(g_bf)), g_fin, g_dil))


def kernel(x, p, fox_norm, fox_w_in, fox_b_f, fox_w_out, dil_norm, dil_w_in, dil_w_out, ple_w_up, ple_w_gate, final_norm, loss_target, m_fox_norm, m_fox_w_in, m_fox_b_f, m_fox_w_out, m_dil_norm, m_dil_w_in, m_dil_w_out, m_ple_w_up, m_ple_w_gate, m_final_norm, v_fox_norm, v_fox_w_in, v_fox_b_f, v_fox_w_out, v_dil_norm, v_dil_w_in, v_dil_w_out, v_ple_w_up, v_ple_w_gate, v_final_norm):
    xi, yi, ci = lax.axis_index("x"), lax.axis_index("y"), lax.axis_index("c")
    chip = 2 * xi + yi

    w_flat = _pack_shard(fox_w_in, fox_w_out, dil_w_in, dil_w_out, ple_w_up, ple_w_gate)
    m_flat = _pack_shard(m_fox_w_in, m_fox_w_out, m_dil_w_in, m_dil_w_out, m_ple_w_up, m_ple_w_gate)
    v_flat = _pack_shard(v_fox_w_in, v_fox_w_out, v_dil_w_in, v_dil_w_out, v_ple_w_up, v_ple_w_gate)
    w_bf = w_flat.astype(BF16)
    wall = lax.dynamic_update_slice(_gather_weights(w_bf), w_bf[None], (chip, 0, 0))
    fwi, fwo, dwi, dwo, pwu, pwg = _unpack_gathered(wall)
    gains = _gather_tiles(_pad_rows(_pad_row(dil_norm), 8), "gather_gain").reshape(4, 2, 8, D_MODEL)
    dil_g = gains[:, 0, 0, :256].reshape(1, D_MODEL)

    loss_part, dx, grads, small = _local_step(
        x[0], p[0, 0], p[1, 0], loss_target[0], fox_norm.reshape(1, D_MODEL), dil_g,
        final_norm.reshape(1, D_MODEL), fox_b_f, fwi, fwo, dwi, dwo, pwu, pwg)
    g_fox, g_bf, g_fin, g_dil = small

    g_all = _pack_grads(*grads)
    theirs = _swap_halves(g_all)
    mine = lax.dynamic_slice_in_dim(g_all, ci * FLAT_HALF, FLAT_HALF, axis=1)
    part, = _rows(lambda a, b: a + b, [mine.reshape(4 * FLAT_HALF, D_MODEL), theirs.reshape(4 * FLAT_HALF, D_MODEL)],
                  [(D_MODEL, BF16)], name="pair_sum", tm=FLAT_TILE)
    part = part.reshape(4, FLAT_HALF, D_MODEL)
    own = lax.dynamic_slice_in_dim(part, chip, 1, axis=0)
    by_chip = lax.dynamic_update_slice(_scatter_partials(part), own, (chip, 0, 0))
    half_sum, = _rows(lambda a, b, c, d: ((a.astype(F32) + b.astype(F32)) + c.astype(F32)) + d.astype(F32),
                      [by_chip[s] for s in range(4)], [(D_MODEL, F32)], name="chip_sum", tm=FLAT_TILE)
    other_half = _sibling_half(half_sum)
    g_flat = jnp.where(ci == 0, jnp.concatenate([half_sum, other_half], axis=0),
                       jnp.concatenate([other_half, half_sum], axis=0))

    tile = jnp.concatenate([g_fox, g_bf, g_fin, g_dil, jnp.pad(loss_part, ((0, 0), (0, D_MODEL - LANES))),
                            jnp.zeros((3, D_MODEL), F32)], axis=0)
    tiles = _gather_tiles(tile, "gather_small")
    tot, = _rows(lambda *t: functools.reduce(lambda a, b: a + b, t), [tiles[8 * s:8 * s + 8] for s in range(8)],
                 [(D_MODEL, F32)], name="small_sum", tm=8)
    loss = tot[4, 0]
    g_small = jnp.concatenate([
        tot[0:3],
        _pad_row(lax.dynamic_slice_in_dim(tot[3], chip * 256, 256)), jnp.zeros((4, D_MODEL), F32)], axis=0)
    small_tile = lambda a, b, c, d: jnp.concatenate(
        [_pad_row(a), _pad_row(b), _pad_row(c), _pad_row(d), jnp.zeros((4, D_MODEL), F32)], axis=0)
    w_small = small_tile(fox_norm, fox_b_f, final_norm, dil_norm)
    m_small = small_tile(m_fox_norm, m_fox_b_f, m_final_norm, m_dil_norm)
    v_small = small_tile(v_fox_norm, v_fox_b_f, v_final_norm, v_dil_norm)

    three = [(D_MODEL, F32)] * 3
    d_flat, nm_flat, nv_flat = _rows(_adamw, [w_flat, g_flat, m_flat, v_flat], three, name="adamw", tm=FLAT_TILE)
    d_small, nm_small, nv_small = _rows(_adamw, [w_small, g_small, m_small, v_small], three, name="adamw_small", tm=8)

    def leaves(flat, small_rows):
        fwi_, fwo_, dwi_, dwo_, pwu_, pwg_ = _unpack_shard(flat)
        return (small_rows[0:1], fwi_, small_rows[1:2, :FOX_HEADS], fwo_, small_rows[3:4, :256], dwi_, dwo_,
                pwu_, pwg_, small_rows[2])

    return (loss, dx[None], *leaves(g_flat, g_small), *leaves(d_flat, d_small), *leaves(nm_flat, nm_small),
            *leaves(nv_flat, nv_small))
```

```python
import functools

import numpy as np
import jax
import jax.numpy as jnp
from jax import lax
from jax.experimental import pallas as pl
from jax.experimental.pallas import tpu as pltpu

F32 = jnp.float32
BF16 = jnp.bfloat16

D_MODEL = 1024
PLE_DIM = 256
FOX_HEADS = 16
FOX_HEAD_DIM = 64
DIL_PATTERN = ((128, 1), (512, 4), (2048, 16))
DIL_GROUPS = 3
DIL_HEADS_PER_GROUP = 8
DIL_HEAD_DIM = 128
DIL_WINDOW_STEPS = 128
DIL_QKV = 3072
ALIBI_MAX_EXP = 8.0
RMS_EPS = 1e-6
ADAM_LR, ADAM_B1, ADAM_B2, ADAM_EPS, ADAM_WD, ADAM_STEP = 0.001, 0.9, 0.999, 1e-08, 0.01, 10

LANES = 128
VMEM_LIMIT = 56 * 1024 * 1024
MESH = pl.DeviceIdType.MESH
ANY = pl.BlockSpec(memory_space=pl.ANY)
NEG = -1e30

FLAT_ROWS = (("fox_w_in", 1028, 1056), ("fox_w_out", 256, 256), ("dil_w_in", 2560, 2560), ("dil_w_out", 256, 256),
             ("ple_w_up", 128, 128), ("ple_w_gate", 512, 512))
FLAT_USED = sum(r for _, _, r in FLAT_ROWS)
FLAT_TOTAL = 4864
FLAT_TILE = 608
FLAT_A = FLAT_ROWS[0][2]
FLAT_B = FLAT_USED - FLAT_A


def _cparams(*sem):
    return pltpu.CompilerParams(dimension_semantics=sem, vmem_limit_bytes=VMEM_LIMIT)


def _sigmoid(x):
    return 1.0 / (1.0 + jnp.exp(-x))


def _mm(a, b, *, name, ta=False, tb=False, out_dtype=F32, add=None):
    if ta:
        K, M = a.shape
    else:
        M, K = a.shape
    if tb:
        N, Kb = b.shape
    else:
        Kb, N = b.shape
    assert K == Kb, (a.shape, b.shape)
    tm, tn, tk = min(M, 1024), min(N, 1024), min(K, 1024)
    assert M % tm == 0 and N % tn == 0 and K % tk == 0, (M, N, K)
    nk = K // tk
    dn = (((0 if ta else 1,), (1 if tb else 0,)), ((), ()))

    def body(*refs):
        if add is None:
            a_ref, b_ref, o_ref, acc = refs
        else:
            a_ref, b_ref, add_ref, o_ref, acc = refs
        k = pl.program_id(2)

        @pl.when(k == 0)
        def _():
            acc[...] = jnp.zeros_like(acc)

        acc[...] += lax.dot_general(a_ref[...].astype(BF16), b_ref[...].astype(BF16), dn,
                                    preferred_element_type=F32)

        @pl.when(k == nk - 1)
        def _():
            r = acc[...]
            if add is not None:
                r = r + add_ref[...]
            o_ref[...] = r.astype(out_dtype)

    a_spec = (pl.BlockSpec((tk, tm), lambda i, j, k: (k, i)) if ta
              else pl.BlockSpec((tm, tk), lambda i, j, k: (i, k)))
    b_spec = (pl.BlockSpec((tn, tk), lambda i, j, k: (j, k)) if tb
              else pl.BlockSpec((tk, tn), lambda i, j, k: (k, j)))
    in_specs = [a_spec, b_spec]
    args = [a, b]
    if add is not None:
        in_specs.append(pl.BlockSpec((tm, tn), lambda i, j, k: (i, j)))
        args.append(add)
    return pl.pallas_call(
        body, name=name, grid=(M // tm, N // tn, nk),
        in_specs=in_specs, out_specs=pl.BlockSpec((tm, tn), lambda i, j, k: (i, j)),
        out_shape=jax.ShapeDtypeStruct((M, N), out_dtype),
        scratch_shapes=[pltpu.VMEM((tm, tn), F32)],
        compiler_params=_cparams("parallel", "parallel", "arbitrary"),
    )(*args)


def _mm_nt_parts(a_parts, b, *, name):
    M = a_parts[0].shape[0]
    N, K = b.shape
    tm, tn, tk = min(M, 1024), min(N, 1024), 1024
    nks = [a.shape[1] // tk for a in a_parts]
    offs = [sum(nks[:p]) for p in range(len(nks))]
    nk = sum(nks)
    assert nk * tk == K and M % tm == 0 and N % tn == 0, (M, N, K)
    n_parts = len(a_parts)

    def body(*refs):
        a_refs, b_ref, o_ref, acc = refs[:n_parts], refs[n_parts], refs[n_parts + 1], refs[n_parts + 2]
        k = pl.program_id(2)

        @pl.when(k == 0)
        def _():
            acc[...] = jnp.zeros_like(acc)

        for a_ref, off, n in zip(a_refs, offs, nks):
            @pl.when((k >= off) & (k < off + n))
            def _(a_ref=a_ref):
                acc[...] += lax.dot_general(a_ref[...].astype(BF16), b_ref[...].astype(BF16),
                                            (((1,), (1,)), ((), ())), preferred_element_type=F32)

        @pl.when(k == nk - 1)
        def _():
            o_ref[...] = acc[...]

    a_specs = [pl.BlockSpec((tm, tk), lambda i, j, k, off=off, n=n: (i, jnp.clip(k - off, 0, n - 1)))
               for off, n in zip(offs, nks)]
    return pl.pallas_call(
        body, name=name, grid=(M // tm, N // tn, nk),
        in_specs=a_specs + [pl.BlockSpec((tn, tk), lambda i, j, k: (j, k))],
        out_specs=pl.BlockSpec((tm, tn), lambda i, j, k: (i, j)),
        out_shape=jax.ShapeDtypeStruct((M, N), F32), scratch_shapes=[pltpu.VMEM((tm, tn), F32)],
        compiler_params=_cparams("parallel", "parallel", "arbitrary"),
    )(*a_parts, b)


def _mm_tn_parts(a, b_parts, *, name):
    K, M = a.shape
    tm, tn, tk = min(M, 1024), 1024, min(K, 1024)
    njs = [b.shape[1] // tn for b in b_parts]
    offs = [sum(njs[:p]) for p in range(len(njs))]
    nj, nk = sum(njs), K // tk
    assert M % tm == 0 and K % tk == 0 and all(b.shape[1] % tn == 0 for b in b_parts)
    n_parts = len(b_parts)

    def body(*refs):
        a_ref, b_refs, o_ref, acc = refs[0], refs[1:1 + n_parts], refs[1 + n_parts], refs[2 + n_parts]
        j, k = pl.program_id(1), pl.program_id(2)

        @pl.when(k == 0)
        def _():
            acc[...] = jnp.zeros_like(acc)

        for b_ref, off, n in zip(b_refs, offs, njs):
            @pl.when((j >= off) & (j < off + n))
            def _(b_ref=b_ref):
                acc[...] += lax.dot_general(a_ref[...].astype(BF16), b_ref[...].astype(BF16),
                                            (((0,), (0,)), ((), ())), preferred_element_type=F32)

        @pl.when(k == nk - 1)
        def _():
            o_ref[...] = acc[...]

    def b_spec(off, n):
        def index(i, j, k):
            mine = (j >= off) & (j < off + n)
            return jnp.where(mine, k, 0), jnp.clip(j - off, 0, n - 1)
        return pl.BlockSpec((tk, tn), index)

    return pl.pallas_call(
        body, name=name, grid=(M // tm, nj, nk),
        in_specs=[pl.BlockSpec((tk, tm), lambda i, j, k: (k, i))] + [b_spec(off, n) for off, n in zip(offs, njs)],
        out_specs=pl.BlockSpec((tm, tn), lambda i, j, k: (i, j)),
        out_shape=jax.ShapeDtypeStruct((M, nj * tn), F32), scratch_shapes=[pltpu.VMEM((tm, tn), F32)],
        compiler_params=_cparams("parallel", "parallel", "arbitrary"),
    )(a, *b_parts)


def _rows(fn, ins, outs, *, name, tm, bcast=(), acc=()):
    R = ins[0].shape[0]
    assert R % tm == 0, (R, tm)
    n_in, n_b, n_out, n_acc = len(ins), len(bcast), len(outs), len(acc)

    def body(*refs):
        in_refs = refs[:n_in + n_b]
        out_refs = refs[n_in + n_b:n_in + n_b + n_out]
        acc_refs = refs[n_in + n_b + n_out:]
        res = fn(*[r[...] for r in in_refs])
        if not isinstance(res, (tuple, list)):
            res = (res,)
        for r, v in zip(out_refs, res[:n_out]):
            r[...] = v.astype(r.dtype)
        first = pl.program_id(0) == 0
        for r, v in zip(acc_refs, res[n_out:]):
            @pl.when(first)
            def _(r=r, v=v):
                r[...] = v.astype(r.dtype)

            @pl.when(jnp.logical_not(first))
            def _(r=r, v=v):
                r[...] += v.astype(r.dtype)

    in_specs = [pl.BlockSpec((tm, a.shape[1]), lambda i: (i, 0)) for a in ins]
    in_specs += [pl.BlockSpec(b.shape, lambda i, nd=b.ndim: (0,) * nd) for b in bcast]
    out_specs = [pl.BlockSpec((tm, c), lambda i: (i, 0)) for c, _ in outs]
    out_specs += [pl.BlockSpec(s, lambda i, nd=len(s): (0,) * nd) for s, _ in acc]
    out_shape = [jax.ShapeDtypeStruct((R, c), dt) for c, dt in outs]
    out_shape += [jax.ShapeDtypeStruct(s, dt) for s, dt in acc]
    res = pl.pallas_call(
        body, name=name, grid=(R // tm,), in_specs=in_specs, out_specs=out_specs, out_shape=out_shape,
        compiler_params=_cparams("arbitrary" if acc else "parallel"),
    )(*ins, *bcast)
    return res


def _cumsum_rows(ins, *, name, reverse=False, pre=None, bcast=(), post=None, out_dtypes=(F32,)):
    S, C = ins[0].shape
    tb = 256
    nb = S // tb
    assert S % tb == 0
    n_in = len(ins) + len(bcast)
    n_out = len(out_dtypes)

    def body(*refs):
        in_refs, o_refs, carry = refs[:n_in], refs[n_in:n_in + n_out], refs[n_in + n_out]

        @pl.when(pl.program_id(0) == 0)
        def _():
            carry[...] = jnp.zeros_like(carry)

        xv = in_refs[0][...] if pre is None else pre(*[r[...] for r in in_refs])
        r_ = lax.broadcasted_iota(jnp.int32, (tb, tb), 0)
        c_ = lax.broadcasted_iota(jnp.int32, (tb, tb), 1)
        tri = jnp.where((c_ >= r_) if reverse else (c_ <= r_), 1.0, 0.0).astype(BF16)
        hi = xv.astype(BF16)
        r1 = xv - hi.astype(F32)
        mid = r1.astype(BF16)
        lo = (r1 - mid.astype(F32)).astype(BF16)
        cs = (jnp.dot(tri, hi, preferred_element_type=F32) + jnp.dot(tri, mid, preferred_element_type=F32)
              + jnp.dot(tri, lo, preferred_element_type=F32)) + carry[...]
        outs = (cs,) if post is None else post(cs)
        for o_ref, v in zip(o_refs, outs):
            o_ref[...] = v.astype(o_ref.dtype)
        carry[...] = cs[0:1, :] if reverse else cs[tb - 1:tb, :]

    blk = (lambda i: (nb - 1 - i, 0)) if reverse else (lambda i: (i, 0))
    in_specs = [pl.BlockSpec((tb, C), blk) for _ in ins]
    in_specs += [pl.BlockSpec(b.shape, lambda i, nd=b.ndim: (0,) * nd) for b in bcast]
    return pl.pallas_call(
        body, name=name, grid=(nb,), in_specs=in_specs, out_specs=[pl.BlockSpec((tb, C), blk)] * n_out,
        out_shape=[jax.ShapeDtypeStruct((S, C), dt) for dt in out_dtypes], scratch_shapes=[pltpu.VMEM((1, C), F32)],
        compiler_params=_cparams("arbitrary"),
    )(*ins, *bcast)


def _rms_fwd(x, g):
    r = lax.rsqrt(jnp.mean(x * x, axis=1, keepdims=True) + RMS_EPS)
    return (x * r) * g


def _rms_bwd(dn, x, dres, g):
    r = lax.rsqrt(jnp.mean(x * x, axis=1, keepdims=True) + RMS_EPS)
    xh = x * r
    w = dn * g
    dx = r * (w - xh * jnp.mean(w * xh, axis=1, keepdims=True))
    return dres + dx, jnp.sum(dn * xh, axis=0, keepdims=True)


def _final_stage(h, tgt, g):
    r = lax.rsqrt(jnp.mean(h * h, axis=1, keepdims=True) + RMS_EPS)
    xh = h * r
    diff = xh * g - tgt
    loss = 0.5 * jnp.sum(jnp.mean(diff * diff, axis=1, keepdims=True), axis=0, keepdims=True)
    dy = diff * (1.0 / D_MODEL)
    w = dy * g
    dh = r * (w - xh * jnp.mean(w * xh, axis=1, keepdims=True))
    return dh, jnp.sum(dy * xh, axis=0, keepdims=True), jnp.broadcast_to(loss, (1, LANES))


def _ple_fwd(h, u, t):
    return h + u * _sigmoid(t)


def _ple_bwd(dh, u, t):
    s = _sigmoid(t)
    return dh * s, dh * u * s * (1.0 - s)


def _ple_norm(h, u, t, g):
    h2 = _ple_fwd(h, u, t)
    return h2, _rms_fwd(h2, g)


def _ple_loss(h, u, t, tgt, g):
    dh, g_gain, loss = _final_stage(_ple_fwd(h, u, t), tgt, g)
    du, dt = _ple_bwd(dh, u, t)
    return du, dt, dh, g_gain, loss


def _norm_ple_bwd(dn, x, dres, u, t, g):
    dh, g_gain = _rms_bwd(dn, x, dres, g)
    du, dt = _ple_bwd(dh, u, t)
    return dh, du, dt, g_gain


def _gate_fwd(o, z):
    return o * (z * _sigmoid(z))


def _head_sums(prod):
    tm, width = prod.shape
    cols = [jnp.broadcast_to(jnp.sum(prod[:, b * LANES:(b + 1) * LANES], axis=1, keepdims=True), (tm, LANES))
            for b in range(width // LANES)]
    return jnp.concatenate(cols, axis=1)


def _gate_bwd(dg, o, z, lse):
    s = _sigmoid(z)
    do = dg * (z * s)
    dz = dg * o * (s * (1.0 + z * (1.0 - s)))
    lane = lax.broadcasted_iota(jnp.int32, do.shape, 1) % LANES
    return do, dz, jnp.where(lane < LANES // 2, lse, _head_sums(do * o))


def _log_forget(fl, b):
    u = fl + b
    return jnp.minimum(u, 0.0) - jnp.log(1.0 + jnp.exp(-jnp.abs(u)))


def _adamw(w, g, m, v):
    m = ADAM_B1 * m + (1.0 - ADAM_B1) * g
    v = ADAM_B2 * v + (1.0 - ADAM_B2) * (g * g)
    m_hat = m / (1.0 - ADAM_B1 ** ADAM_STEP)
    v_hat = v / (1.0 - ADAM_B2 ** ADAM_STEP)
    delta = -ADAM_LR * (m_hat / (jnp.sqrt(v_hat) + ADAM_EPS) + ADAM_WD * w)
    return delta, m, v


FOX_TQ = 1024


def _split3(x):
    p1 = x.astype(BF16).astype(F32)
    r = x - p1
    p2 = r.astype(BF16).astype(F32)
    return p1, p2, r - p2


def _lane_in_head(shape):
    return lax.broadcasted_iota(jnp.int32, shape, 1) % FOX_HEAD_DIM


def _query_extras(x):
    lm = _lane_in_head(x.shape)
    p1, p2, p3 = _split3(x)
    return jnp.where(lm == 0, p1, jnp.where(lm == 1, p2, jnp.where(lm == 2, p3, jnp.where(lm < 6, 1.0, 0.0))))


def _key_extras(c):
    lm = _lane_in_head(c.shape)
    p1, p2, p3 = _split3(c)
    return jnp.where(lm < 3, 1.0, jnp.where(lm == 3, -p1, jnp.where(lm == 4, -p2, jnp.where(lm == 5, -p3, 0.0))))


def _fox_extras(c):
    return _query_extras(c), _key_extras(c)


def _swapped_head_sums(prod):
    tm, width = prod.shape
    lane = lax.broadcasted_iota(jnp.int32, (tm, LANES), 1)
    low = lane < FOX_HEAD_DIM
    cols = []
    for b in range(width // LANES):
        blk = prod[:, b * LANES:(b + 1) * LANES]
        sa = jnp.sum(jnp.where(low, blk, 0.0), axis=1, keepdims=True)
        sb = jnp.sum(jnp.where(low, 0.0, blk), axis=1, keepdims=True)
        cols.append(jnp.where(low, sb, sa))
    return jnp.concatenate(cols, axis=1)


def _gate_bwd_fox(dg, o, z, a):
    s = _sigmoid(z)
    do = dg * (z * s)
    dz = dg * o * (s * (1.0 + z * (1.0 - s)))
    lm = _lane_in_head(do.shape)
    d1, d2, d3 = _split3(-_swapped_head_sums(do * o))
    dx = jnp.where(lm == 0, d1, jnp.where(lm == 1, d2, jnp.where(lm == 2, d3, 0.0)))
    return do, dz, _query_extras(a), dx


def _forget_dc(drx, dkx):
    lm = _lane_in_head(drx.shape)
    return jnp.where(lm == 0, drx, 0.0) - jnp.where(lm == 3, dkx, 0.0)


def _forget_bwd(dl, fl, b):
    lm = _lane_in_head(dl.shape)
    width = dl.shape[1]
    both = dl + pltpu.roll(dl, 3, 1) + pltpu.roll(dl, width - 3, 1)
    du = jnp.where(lm == 0, both, 0.0) * (1.0 / (1.0 + jnp.exp(fl + b)))
    return du, jnp.sum(du, axis=0, keepdims=True)


def _chip_exchange(src_of, out_ref, send, recv):
    x, y, c = lax.axis_index("x"), lax.axis_index("y"), lax.axis_index("c")
    chips = [(1 - x, y), (x, 1 - y), (1 - x, 1 - y)]
    me = 2 * x + y
    sends = [pltpu.make_async_remote_copy(src_ref=src_of(cx, cy), dst_ref=out_ref.at[me], send_sem=send.at[k],
                                          recv_sem=recv.at[k], device_id=(cx, cy, c), device_id_type=MESH)
             for k, (cx, cy) in enumerate(chips)]

    def start():
        for cp in sends:
            cp.start()

    def wait():
        for k, (cx, cy) in enumerate(chips):
            slot = out_ref.at[2 * cx + cy]
            pltpu.make_async_remote_copy(src_ref=slot, dst_ref=slot, send_sem=send.at[k], recv_sem=recv.at[k],
                                         device_id=(cx, cy, c), device_id_type=MESH).wait_recv()
        for cp in sends:
            cp.wait_send()

    return start, wait


def _fox_fwd(qkv, qx, kx, z, w_b):
    S = qkv.shape[0]
    tq = min(FOX_TQ, S)
    nq = S // tq
    nt = (((1,), (1,)), ((), ()))

    def body(q_ref, qx_ref, k_ref, v_ref, kx_ref, z_ref, wb_ref, o_ref, a_ref, g_ref, wall_ref, send, recv):
        i = pl.program_id(1)
        start, wait = _chip_exchange(lambda cx, cy: wb_ref, wall_ref, send, recv)
        pl.when((pl.program_id(0) == 0) & (i == 0))(start)
        low = lax.broadcasted_iota(jnp.int32, (tq, LANES), 1) < FOX_HEAD_DIM
        row = lax.broadcasted_iota(jnp.int32, (tq, tq), 0)
        col = lax.broadcasted_iota(jnp.int32, (tq, tq), 1)
        q2, x2 = q_ref[...], qx_ref[...]
        qa = (jnp.where(low, q2, x2), jnp.where(low, x2, q2))

        def step(kb, carry, diag):
            start = pl.multiple_of(kb * tq, tq)
            k2 = k_ref[pl.ds(start, tq), :]
            v2 = v_ref[pl.ds(start, tq), :]
            y2 = kx_ref[pl.ds(start, tq), :]
            one = jnp.ones_like(v2)
            ka = (jnp.where(low, k2, y2), jnp.where(low, y2, k2))
            va = (jnp.where(low, v2, one), jnp.where(low, one, v2))
            out = []
            for hh in range(2):
                m, acc = carry[hh]
                s = lax.dot_general(qa[hh], ka[hh], nt, preferred_element_type=F32)
                if diag:
                    s = jnp.where(col <= row, s, NEG)
                m_new = jnp.maximum(m, jnp.max(s, axis=1, keepdims=True))
                pr = jnp.exp(s - m_new)
                acc = jnp.exp(m - m_new) * acc + jnp.dot(pr.astype(BF16), va[hh], preferred_element_type=F32)
                out.append((m_new, acc))
            return tuple(out)

        init = ((jnp.full((tq, 1), NEG, F32), jnp.zeros((tq, LANES), F32)),) * 2
        carry = lax.fori_loop(0, i, functools.partial(step, diag=False), init)
        (m_a, acc_a), (m_b, acc_b) = step(i, carry, True)
        l_a, l_b = acc_a[:, FOX_HEAD_DIM:FOX_HEAD_DIM + 1], acc_b[:, 0:1]
        xf = x2.astype(F32)
        c_a = xf[:, 64:65] + xf[:, 65:66] + xf[:, 66:67]
        c_b = xf[:, 0:1] + xf[:, 1:2] + xf[:, 2:3]
        o = jnp.where(low, acc_a / l_a, acc_b / l_b)
        o_ref[...] = o
        g_ref[...] = _gate_fwd(o, z_ref[...]).astype(BF16)
        a_ref[...] = jnp.where(low, jnp.broadcast_to(c_b - (m_b + jnp.log(l_b)), (tq, LANES)),
                               jnp.broadcast_to(c_a - (m_a + jnp.log(l_a)), (tq, LANES)))
        pl.when((pl.program_id(0) == FOX_HEADS // 2 - 1) & (i == nq - 1))(wait)

    blk = lambda cb: pl.BlockSpec((tq, LANES), lambda h, i, cb=cb: (i, cb + h))
    res = lambda cb: pl.BlockSpec((S, LANES), lambda h, i, cb=cb: (0, cb + h))
    return pl.pallas_call(
        body, name="fox_attn_fwd", grid=(FOX_HEADS // 2, nq),
        in_specs=[blk(0), blk(0), res(8), res(16), res(0), blk(0), ANY],
        out_specs=[blk(0), blk(0), blk(0), ANY],
        out_shape=[jax.ShapeDtypeStruct((S, D_MODEL), F32), jax.ShapeDtypeStruct((S, D_MODEL), F32),
                   jax.ShapeDtypeStruct((S, D_MODEL), BF16), jax.ShapeDtypeStruct((4,) + w_b.shape, w_b.dtype)],
        scratch_shapes=[pltpu.SemaphoreType.DMA((3,)), pltpu.SemaphoreType.DMA((3,))],
        compiler_params=_cparams("arbitrary", "arbitrary"),
    )(qkv, qx, qkv, qkv, kx, z, w_b)


def _fox_bwd(qkv, do, qxa, dx, kx, g_b):
    S = qkv.shape[0]
    tq = min(FOX_TQ, S)
    nq = S // tq
    nt = (((1,), (1,)), ((), ()))
    tn = (((0,), (0,)), ((), ()))

    def body(q_ref, qx_ref, do_ref, dx_ref, k_ref, v_ref, kx_ref, gb_ref, dq_ref, dr_ref, dk_ref, dv_ref, dkx_ref,
             got_ref, send, recv):
        kb = pl.program_id(1)
        start, wait = _chip_exchange(lambda cx, cy: gb_ref.at[2 * cx + cy], got_ref, send, recv)
        pl.when((pl.program_id(0) == 0) & (kb == 0))(start)
        low = lax.broadcasted_iota(jnp.int32, (tq, LANES), 1) < FOX_HEAD_DIM
        row = lax.broadcasted_iota(jnp.int32, (tq, tq), 0)
        col = lax.broadcasted_iota(jnp.int32, (tq, tq), 1)

        @pl.when(kb == 0)
        def _():
            dq_ref[...] = jnp.zeros_like(dq_ref)
            dr_ref[...] = jnp.zeros_like(dr_ref)

        k2, v2, y2 = k_ref[...], v_ref[...], kx_ref[...]
        one = jnp.ones_like(v2)
        ka = (jnp.where(low, k2, y2), jnp.where(low, y2, k2))
        va = (jnp.where(low, v2, one), jnp.where(low, one, v2))

        def step(qb, carry, diag):
            start = pl.multiple_of(qb * tq, tq)
            q2 = q_ref[pl.ds(start, tq), :]
            x2 = qx_ref[pl.ds(start, tq), :]
            d2 = do_ref[pl.ds(start, tq), :]
            e2 = dx_ref[pl.ds(start, tq), :]
            qa = (jnp.where(low, q2, x2), jnp.where(low, x2, q2))
            da = (jnp.where(low, d2, e2), jnp.where(low, e2, d2))
            new, res = [], []
            for hh in range(2):
                dk, dv = carry[hh]
                s = lax.dot_general(qa[hh], ka[hh], nt, preferred_element_type=F32)
                if diag:
                    s = jnp.where(col <= row, s, NEG)
                pr = jnp.exp(s)
                ds = pr * lax.dot_general(da[hh], va[hh], nt, preferred_element_type=F32)
                prb, dsb = pr.astype(BF16), ds.astype(BF16)
                dv = dv + lax.dot_general(prb, da[hh], tn, preferred_element_type=F32)
                dk = dk + lax.dot_general(dsb, qa[hh], tn, preferred_element_type=F32)
                res.append(jnp.dot(dsb, ka[hh], preferred_element_type=F32))
                new.append((dk, dv))
            dq_ref[pl.ds(start, tq), :] += jnp.where(low, res[0], res[1])
            dr_ref[pl.ds(start, tq), :] += jnp.where(low, res[1], res[0])
            return tuple(new)

        init = ((jnp.zeros((tq, LANES), F32), jnp.zeros((tq, LANES), F32)),) * 2
        carry = step(kb, init, True)
        (dk_a, dv_a), (dk_b, dv_b) = lax.fori_loop(kb + 1, nq, functools.partial(step, diag=False), carry)
        dk_ref[...] = jnp.where(low, dk_a, dk_b).astype(BF16)
        dv_ref[...] = jnp.where(low, dv_a, dv_b).astype(BF16)
        dkx_ref[...] = jnp.where(low, dk_b, dk_a)

        @pl.when(kb == nq - 1)
        def _():
            dq_ref[...] = dq_ref[...] * (FOX_HEAD_DIM ** -0.5)

        pl.when((pl.program_id(0) == FOX_HEADS // 2 - 1) & (kb == nq - 1))(wait)

    res = lambda cb: pl.BlockSpec((S, LANES), lambda h, k, cb=cb: (0, cb + h))
    blk = lambda cb: pl.BlockSpec((tq, LANES), lambda h, k, cb=cb: (k, cb + h))
    f32, b16 = jax.ShapeDtypeStruct((S, D_MODEL), F32), jax.ShapeDtypeStruct((S, D_MODEL), BF16)
    return pl.pallas_call(
        body, name="fox_attn_bwd", grid=(FOX_HEADS // 2, nq),
        in_specs=[res(0), res(0), res(0), res(0), blk(8), blk(16), blk(0), ANY],
        out_specs=[res(0), res(0), blk(0), blk(0), blk(0), ANY],
        out_shape=[f32, f32, b16, b16, f32, jax.ShapeDtypeStruct(g_b.shape, g_b.dtype)],
        scratch_shapes=[pltpu.SemaphoreType.DMA((3,)), pltpu.SemaphoreType.DMA((3,))],
        compiler_params=_cparams("arbitrary", "arbitrary"),
    )(qkv, qxa, do, dx, qkv, qkv, kx, g_b)


def _alibi_slopes():
    n = DIL_GROUPS * DIL_HEADS_PER_GROUP
    s = np.float32(2.0) ** (np.float32(-ALIBI_MAX_EXP) * np.arange(1, n + 1, dtype=np.float32) / np.float32(n))
    return s.astype(np.float32).reshape(DIL_GROUPS, DIL_HEADS_PER_GROUP)


W = DIL_WINDOW_STEPS
DIL_SCALE = DIL_HEAD_DIM ** -0.5


DIL_ROWS = 2048
NT = (((1,), (1,)), ((), ()))
TN = (((0,), (0,)), ((), ()))
QKV_BLOCKS = DIL_QKV // LANES


def _slope_table():
    t = np.zeros((DIL_HEADS_PER_GROUP, 8, LANES), np.float32)
    t[:, :DIL_GROUPS, :] = _alibi_slopes().T[:, :, None]
    return jnp.asarray(t)


def _phase_rows(start, d):
    return pl.ds(start, W, stride=d) if d > 1 else pl.ds(start, W)


def _for_phases(d, unit):
    if d == 1:
        unit(0)
    else:
        lax.fori_loop(0, d, lambda r, c: (unit(r), c)[1], 0)


def _window_geometry(shape, q_axis_offset):
    i_ = lax.broadcasted_iota(jnp.int32, shape, 0)
    j_ = lax.broadcasted_iota(jnp.int32, shape, 1)
    dist = q_axis_offset + i_ - j_
    return i_, j_, dist, (dist >= 0) & (dist <= W)


def _dil_fwd(qkv, z):
    S = qkv.shape[0]
    rb = min(DIL_ROWS, S)
    nb = S // rb
    spans = [W * d for _, d in DIL_PATTERN]

    def body(*refs):
        slope_ref = refs[0]
        q, kc, kp, vc, vp = refs[1:4], refs[4:7], refs[7:10], refs[10:13], refs[13:16]
        z_ref, o_ref, g_ref, l_ref = refs[16:20]
        og, lg = refs[20:23], refs[23:26]
        i = pl.program_id(1)
        _, j_, dist, inwin = _window_geometry((W, 2 * W), W)
        distf = dist.astype(F32)
        for g, (_, d) in enumerate(DIL_PATTERN):
            span = spans[g]
            bias = -(slope_ref[0, g:g + 1, 0:1] * float(d)) * distf
            for b in range(rb // span):
                def unit(r, g=g, d=d, span=span, b=b, bias=bias):
                    base = b * span + r
                    rows = _phase_rows(base, d)
                    if b > 0:
                        kprev, vprev = kc[g][_phase_rows(base - span, d), :], vc[g][_phase_rows(base - span, d), :]
                        valid = inwin
                    else:
                        kprev, vprev = kp[g][_phase_rows(r, d), :], vp[g][_phase_rows(r, d), :]
                        valid = inwin & ((j_ >= W) | (i > 0))
                    k2 = jnp.concatenate([kprev, kc[g][rows, :]], axis=0).astype(BF16)
                    v2 = jnp.concatenate([vprev, vc[g][rows, :]], axis=0).astype(BF16)
                    s = lax.dot_general(q[g][rows, :].astype(BF16), k2, NT, preferred_element_type=F32)
                    s = jnp.where(valid, s * DIL_SCALE + bias, NEG)
                    m = jnp.max(s, axis=1, keepdims=True)
                    pr = jnp.exp(s - m)
                    l = jnp.sum(pr, axis=1, keepdims=True)
                    og[g][rows, :] = jnp.dot(pr.astype(BF16), v2, preferred_element_type=F32) / l
                    lg[g][rows, :] = jnp.broadcast_to(m + jnp.log(l), (W, LANES))

                _for_phases(d, unit)

        def mix(cix, c):
            sl = pl.ds(pl.multiple_of(cix * 256, 256), 256)
            l1, l2, l3 = lg[0][sl, :], lg[1][sl, :], lg[2][sl, :]
            m = jnp.maximum(jnp.maximum(l1, l2), l3)
            e1, e2, e3 = jnp.exp(l1 - m), jnp.exp(l2 - m), jnp.exp(l3 - m)
            tot = e1 + e2 + e3
            o = (e1 * og[0][sl, :] + e2 * og[1][sl, :] + e3 * og[2][sl, :]) / tot
            o_ref[sl, :] = o
            g_ref[sl, :] = _gate_fwd(o, z_ref[sl, :]).astype(BF16)
            l_ref[sl, :] = m + jnp.log(tot)
            return c

        lax.fori_loop(0, rb // 256, mix, 0)

    cur = lambda off, g: pl.BlockSpec((rb, LANES), lambda h, i: (i, off * QKV_BLOCKS + g * DIL_HEADS_PER_GROUP + h))
    prev = lambda off, g: pl.BlockSpec(
        (spans[g], LANES),
        lambda h, i: (jnp.maximum(i * (rb // spans[g]) - 1, 0), off * QKV_BLOCKS + g * DIL_HEADS_PER_GROUP + h))
    row = pl.BlockSpec((rb, LANES), lambda h, i: (i, h))
    groups = range(DIL_GROUPS)
    in_specs = [pl.BlockSpec((1, 8, LANES), lambda h, i: (h, 0, 0))]
    in_specs += [cur(0, g) for g in groups] + [cur(1, g) for g in groups] + [prev(1, g) for g in groups]
    in_specs += [cur(2, g) for g in groups] + [prev(2, g) for g in groups] + [row]
    f32 = jax.ShapeDtypeStruct((S, D_MODEL), F32)
    return pl.pallas_call(
        body, name="dil_attn_fwd", grid=(DIL_HEADS_PER_GROUP, nb), in_specs=in_specs,
        out_specs=[row, row, row], out_shape=[f32, jax.ShapeDtypeStruct((S, D_MODEL), BF16), f32],
        scratch_shapes=[pltpu.VMEM((rb, LANES), F32)] * 6,
        compiler_params=_cparams("parallel", "parallel"),
    )(_slope_table(), *([qkv] * 15), z)


def _dil_bwd_q(qkv, do, ld, buf, g):
    S = qkv.shape[0]
    rb = min(DIL_ROWS, S)
    nb = S // rb
    d = DIL_PATTERN[g][1]
    span = W * d

    def body(slope_ref, q_ref, kc, kp, vc, vp, do_ref, ld_ref, buf_ref, dq_ref, dq_f32):
        i = pl.program_id(1)
        _, j_, dist, inwin = _window_geometry((W, 2 * W), W)
        bias = -(slope_ref[0, g:g + 1, 0:1] * float(d)) * dist.astype(F32)
        for b in range(rb // span):
            def unit(r, b=b):
                base = b * span + r
                rows = _phase_rows(base, d)
                if b > 0:
                    kprev, vprev = kc[_phase_rows(base - span, d), :], vc[_phase_rows(base - span, d), :]
                    valid = inwin
                else:
                    kprev, vprev = kp[_phase_rows(r, d), :], vp[_phase_rows(r, d), :]
                    valid = inwin & ((j_ >= W) | (i > 0))
                k2 = jnp.concatenate([kprev, kc[rows, :]], axis=0).astype(BF16)
                v2 = jnp.concatenate([vprev, vc[rows, :]], axis=0).astype(BF16)
                ld = ld_ref[rows, :]
                s = lax.dot_general(q_ref[rows, :].astype(BF16), k2, NT, preferred_element_type=F32)
                s = jnp.where(valid, s * DIL_SCALE + bias - ld[:, 0:1], NEG)
                pr = jnp.exp(s)
                dp = lax.dot_general(do_ref[rows, :].astype(BF16), v2, NT, preferred_element_type=F32)
                ds = pr * (dp - ld[:, LANES // 2:LANES // 2 + 1])
                dq_f32[rows, :] = jnp.dot(ds.astype(BF16), k2, preferred_element_type=F32) * DIL_SCALE

            _for_phases(d, unit)
        dq_ref[...] = dq_f32[...].astype(BF16)

    col = lambda off: off * QKV_BLOCKS + g * DIL_HEADS_PER_GROUP
    cur = lambda off: pl.BlockSpec((rb, LANES), lambda h, i: (i, col(off) + h))
    prev = lambda off: pl.BlockSpec((span, LANES), lambda h, i: (jnp.maximum(i * (rb // span) - 1, 0), col(off) + h))
    row = pl.BlockSpec((rb, LANES), lambda h, i: (i, h))
    return pl.pallas_call(
        body, name=f"dil_attn_bwd_q_g{g}", grid=(DIL_HEADS_PER_GROUP, nb),
        in_specs=[pl.BlockSpec((1, 8, LANES), lambda h, i: (h, 0, 0)), cur(0), cur(1), prev(1), cur(2), prev(2),
                  row, row, pl.BlockSpec(memory_space=pl.ANY)],
        out_specs=pl.BlockSpec((rb, LANES), lambda h, i: (i, g * DIL_HEADS_PER_GROUP + h)),
        out_shape=jax.ShapeDtypeStruct(buf.shape, buf.dtype), input_output_aliases={8: 0},
        scratch_shapes=[pltpu.VMEM((rb, LANES), F32)],
        compiler_params=_cparams("parallel", "parallel"),
    )(_slope_table(), qkv, qkv, qkv, qkv, qkv, do, ld, buf)


def _dil_bwd_kv(qkv, do, ld, bufk, bufv, g):
    S = qkv.shape[0]
    rb = min(DIL_ROWS, S)
    nb = S // rb
    d = DIL_PATTERN[g][1]
    span = W * d
    nub = rb // span

    def body(slope_ref, k_ref, v_ref, qc, qn, doc, don, ldc, ldn, bufk_ref, bufv_ref, dk_ref, dv_ref, dk_f32, dv_f32):
        i = pl.program_id(1)
        i_, _, dist, inwin = _window_geometry((2 * W, W), 0)
        bias = -(slope_ref[0, g:g + 1, 0:1] * float(d)) * dist.astype(F32)
        for b in range(nub):
            def unit(r, b=b):
                base = b * span + r
                rows = _phase_rows(base, d)
                if b < nub - 1:
                    nxt = _phase_rows(base + span, d)
                    qnext, donext, ldnext = qc[nxt, :], doc[nxt, :], ldc[nxt, :]
                    valid = inwin
                else:
                    nxt = _phase_rows(r, d)
                    qnext, donext, ldnext = qn[nxt, :], don[nxt, :], ldn[nxt, :]
                    valid = inwin & ((i_ < W) | (i < nb - 1))
                q2 = jnp.concatenate([qc[rows, :], qnext], axis=0).astype(BF16)
                do2 = jnp.concatenate([doc[rows, :], donext], axis=0).astype(BF16)
                ld2 = jnp.concatenate([ldc[rows, :], ldnext], axis=0)
                s = lax.dot_general(q2, k_ref[rows, :].astype(BF16), NT, preferred_element_type=F32)
                s = jnp.where(valid, s * DIL_SCALE + bias - ld2[:, 0:1], NEG)
                pr = jnp.exp(s)
                dp = lax.dot_general(do2, v_ref[rows, :].astype(BF16), NT, preferred_element_type=F32)
                ds = pr * (dp - ld2[:, LANES // 2:LANES // 2 + 1])
                dv_f32[rows, :] = lax.dot_general(pr.astype(BF16), do2, TN, preferred_element_type=F32)
                dk_f32[rows, :] = lax.dot_general(ds.astype(BF16), q2, TN, preferred_element_type=F32) * DIL_SCALE

            _for_phases(d, unit)
        dk_ref[...] = dk_f32[...].astype(BF16)
        dv_ref[...] = dv_f32[...].astype(BF16)

    col = lambda off: off * QKV_BLOCKS + g * DIL_HEADS_PER_GROUP
    cur = lambda off: pl.BlockSpec((rb, LANES), lambda h, i: (i, col(off) + h))
    nxt_blk = lambda i: jnp.minimum((i + 1) * nub, S // span - 1)
    nxt = lambda off: pl.BlockSpec((span, LANES), lambda h, i: (nxt_blk(i), col(off) + h))
    row = pl.BlockSpec((rb, LANES), lambda h, i: (i, h))
    row_nxt = pl.BlockSpec((span, LANES), lambda h, i: (nxt_blk(i), h))
    any_ = pl.BlockSpec(memory_space=pl.ANY)
    out = pl.BlockSpec((rb, LANES), lambda h, i: (i, g * DIL_HEADS_PER_GROUP + h))
    return pl.pallas_call(
        body, name=f"dil_attn_bwd_kv_g{g}", grid=(DIL_HEADS_PER_GROUP, nb),
        in_specs=[pl.BlockSpec((1, 8, LANES), lambda h, i: (h, 0, 0)), cur(1), cur(2), cur(0), nxt(0),
                  row, row_nxt, row, row_nxt, any_, any_],
        out_specs=[out, out],
        out_shape=[jax.ShapeDtypeStruct(bufk.shape, bufk.dtype), jax.ShapeDtypeStruct(bufv.shape, bufv.dtype)],
        input_output_aliases={9: 0, 10: 1},
        scratch_shapes=[pltpu.VMEM((rb, LANES), F32)] * 2,
        compiler_params=_cparams("parallel", "parallel"),
    )(_slope_table(), qkv, qkv, qkv, qkv, do, do, ld, ld, bufk, bufv)


def _place():
    x, y, c = lax.axis_index("x"), lax.axis_index("y"), lax.axis_index("c")
    chips = [(1 - x, y), (x, 1 - y), (1 - x, 1 - y)]
    return x, y, c, chips


def _gather_weights(wb):
    R = wb.shape[0]
    H = R // 2

    def body(w_ref, out_ref, send1, recv1, send2, recv2):
        x, y, c, chips = _place()
        me = 2 * x + y
        sib = (x, y, 1 - c)
        half = pl.ds(c * H, H)
        first = [pltpu.make_async_remote_copy(
            src_ref=w_ref.at[half], dst_ref=out_ref.at[me, half], send_sem=send1.at[k], recv_sem=recv1.at[k],
            device_id=(*chip, c), device_id_type=MESH) for k, chip in enumerate(chips)]
        for cp in first:
            cp.start()
        passed = []
        for k, (cx, cy) in enumerate(chips):
            slot = out_ref.at[2 * cx + cy, half]
            pltpu.make_async_remote_copy(src_ref=slot, dst_ref=slot, send_sem=send1.at[k], recv_sem=recv1.at[k],
                                         device_id=(cx, cy, c), device_id_type=MESH).wait_recv()
            cp = pltpu.make_async_remote_copy(src_ref=slot, dst_ref=slot, send_sem=send2.at[k], recv_sem=recv2.at[k],
                                              device_id=sib, device_id_type=MESH)
            cp.start()
            passed.append(cp)
        for k, (cx, cy) in enumerate(chips):
            slot = out_ref.at[2 * cx + cy, pl.ds((1 - c) * H, H)]
            pltpu.make_async_remote_copy(src_ref=slot, dst_ref=slot, send_sem=send2.at[k], recv_sem=recv2.at[k],
                                         device_id=sib, device_id_type=MESH).wait_recv()
        for cp in first + passed:
            cp.wait_send()

    return pl.pallas_call(
        body, name="gather_weights", in_specs=[ANY], out_specs=ANY,
        out_shape=jax.ShapeDtypeStruct((4, R, D_MODEL), wb.dtype),
        scratch_shapes=[pltpu.SemaphoreType.DMA((3,)), pltpu.SemaphoreType.DMA((3,)),
                        pltpu.SemaphoreType.DMA((3,)), pltpu.SemaphoreType.DMA((3,))],
    )(wb)


def _swap_halves(g):
    H = g.shape[1] // 2

    def body(g_ref, out_ref, send, recv):
        x, y, c, _ = _place()
        sib = (x, y, 1 - c)
        cps = [pltpu.make_async_remote_copy(
            src_ref=g_ref.at[s, pl.ds((1 - c) * H, H)], dst_ref=out_ref.at[s], send_sem=send.at[s],
            recv_sem=recv.at[s], device_id=sib, device_id_type=MESH) for s in range(4)]
        for cp in cps:
            cp.start()
        for cp in cps:
            cp.wait()

    return pl.pallas_call(
        body, name="swap_halves", in_specs=[ANY], out_specs=ANY,
        out_shape=jax.ShapeDtypeStruct((4, H, D_MODEL), g.dtype),
        scratch_shapes=[pltpu.SemaphoreType.DMA((4,)), pltpu.SemaphoreType.DMA((4,))],
    )(g)


def _scatter_partials(pb):
    def body(p_ref, out_ref, send, recv):
        x, y, c, chips = _place()
        me = 2 * x + y
        cps = [pltpu.make_async_remote_copy(
            src_ref=p_ref.at[2 * cx + cy], dst_ref=out_ref.at[me], send_sem=send.at[k], recv_sem=recv.at[k],
            device_id=(cx, cy, c), device_id_type=MESH) for k, (cx, cy) in enumerate(chips)]
        for cp in cps:
            cp.start()
        for k, (cx, cy) in enumerate(chips):
            slot = out_ref.at[2 * cx + cy]
            pltpu.make_async_remote_copy(src_ref=slot, dst_ref=slot, send_sem=send.at[k], recv_sem=recv.at[k],
                                         device_id=(cx, cy, c), device_id_type=MESH).wait_recv()
        for cp in cps:
            cp.wait_send()

    return pl.pallas_call(
        body, name="scatter_partials", in_specs=[ANY], out_specs=ANY,
        out_shape=jax.ShapeDtypeStruct(pb.shape, pb.dtype),
        scratch_shapes=[pltpu.SemaphoreType.DMA((3,)), pltpu.SemaphoreType.DMA((3,))],
    )(pb)


def _sibling_half(f, name):
    def body(f_ref, out_ref, send, recv):
        x, y, c, _ = _place()
        cp = pltpu.make_async_remote_copy(src_ref=f_ref, dst_ref=out_ref, send_sem=send, recv_sem=recv,
                                          device_id=(x, y, 1 - c), device_id_type=MESH)
        cp.start()
        cp.wait()

    return pl.pallas_call(
        body, name=name, in_specs=[ANY], out_specs=ANY,
        out_shape=jax.ShapeDtypeStruct(f.shape, f.dtype),
        scratch_shapes=[pltpu.SemaphoreType.DMA, pltpu.SemaphoreType.DMA],
    )(f)


def _gather_tiles(tile, name):
    m_per = tile.shape[0]

    def body(x_ref, out_ref, send_sems, recv_sems, local_sem):
        x, y, c, chips = _place()
        me, sibling = (x, y, c), (x, y, 1 - c)

        def rows(px, py, pc):
            return out_ref.at[pl.ds((4 * px + 2 * py + pc) * m_per, m_per), :]

        def copy(k, block, to, src=None):
            return pltpu.make_async_remote_copy(
                src_ref=rows(*block) if src is None else src, dst_ref=rows(*block),
                send_sem=send_sems.at[k], recv_sem=recv_sems.at[k], device_id=to, device_id_type=MESH)

        mine = pltpu.make_async_copy(x_ref, rows(*me), local_sem)
        mine.start()
        first = [copy(0, me, sibling, src=x_ref)]
        first += [copy(1 + j, me, (*chip, c), src=x_ref) for j, chip in enumerate(chips)]
        for cp in first:
            cp.start()
        passed = [copy(4 + j, (*chip, c), sibling) for j, chip in enumerate(chips)]
        for j, chip in enumerate(chips):
            copy(1 + j, (*chip, c), me).wait_recv()
            passed[j].start()
        copy(0, sibling, me).wait_recv()
        for j, chip in enumerate(chips):
            copy(4 + j, (*chip, 1 - c), me).wait_recv()
        for cp in first + passed:
            cp.wait_send()
        mine.wait()

    return pl.pallas_call(
        body, name=name, out_shape=jax.ShapeDtypeStruct((8 * m_per, D_MODEL), tile.dtype),
        in_specs=[pl.BlockSpec(memory_space=pltpu.VMEM)], out_specs=pl.BlockSpec(memory_space=pltpu.VMEM),
        scratch_shapes=[pltpu.SemaphoreType.DMA((7,)), pltpu.SemaphoreType.DMA((7,)), pltpu.SemaphoreType.DMA],
    )(tile)


def _pad_rows(a, rows):
    return jnp.pad(a, ((0, rows - a.shape[0]), (0, 0)))


def _pad_row(v):
    v = v.reshape(1, -1)
    return jnp.pad(v, ((0, 0), (0, D_MODEL - v.shape[1])))


def _pack_shard(fwi, fwo, dwi, dwo, pwu, pwg):
    parts = [_pad_rows(fwi.reshape(1028, D_MODEL), FLAT_A), fwo.reshape(256, D_MODEL), dwi.reshape(2560, D_MODEL),
             dwo.reshape(256, D_MODEL), pwu.reshape(128, D_MODEL), pwg.reshape(512, D_MODEL),
             jnp.zeros((FLAT_TOTAL - FLAT_USED, D_MODEL), fwi.dtype)]
    return jnp.concatenate(parts, axis=0)


def _unpack_shard(flat):
    out, r0 = {}, 0
    for name, rows, slot in FLAT_ROWS:
        out[name] = flat[r0:r0 + rows]
        r0 += slot
    return (out["fox_w_in"].reshape(1, D_MODEL, 1028), out["fox_w_out"].reshape(1, 256, D_MODEL),
            out["dil_w_in"].reshape(1, D_MODEL, 2560), out["dil_w_out"].reshape(1, 256, D_MODEL),
            out["ple_w_up"].reshape(2, PLE_DIM, 256), out["ple_w_gate"].reshape(2, 256, D_MODEL))


def _from_shard_columns(a, n):
    return a.reshape(4, D_MODEL, n).transpose(1, 0, 2).reshape(D_MODEL, 4 * n)


def _to_shard_columns(a, n):
    return a.reshape(D_MODEL, 4, n).transpose(1, 0, 2).reshape(4, n, D_MODEL)


def _unpack_a(wall_a):
    return _from_shard_columns(wall_a[:, :1028], 1028)


def _unpack_b(wall_b):
    out, r0 = {}, 0
    for name, rows, slot in FLAT_ROWS[1:]:
        out[name] = wall_b[:, r0:r0 + rows]
        r0 += slot
    pwu4 = out["ple_w_up"].reshape(4, 2, PLE_DIM, 256)
    pwg4 = out["ple_w_gate"].reshape(4, 2, 256, D_MODEL)
    pwu = [pwu4[:, i].transpose(1, 0, 2).reshape(PLE_DIM, D_MODEL) for i in range(2)]
    pwg = [pwg4[:, i].reshape(D_MODEL, D_MODEL) for i in range(2)]
    return (out["fox_w_out"].reshape(D_MODEL, D_MODEL), _from_shard_columns(out["dil_w_in"], 2560),
            out["dil_w_out"].reshape(D_MODEL, D_MODEL), pwu, pwg)


def _gathered_b_weights(wall_b, w_b, chip):
    return _unpack_b(lax.dynamic_update_slice(wall_b, w_b[None], (chip, 0, 0)))


def _pack_grads_a(gfwi):
    return jnp.pad(_to_shard_columns(gfwi, 1028), ((0, 0), (0, FLAT_A - 1028), (0, 0)))


def _pack_grads_b(gfwo, gdwi, gdwo, gpwu, gpwg):
    up = jnp.stack([a.reshape(PLE_DIM, 4, 256).transpose(1, 0, 2) for a in gpwu], axis=1)
    gate = jnp.stack([a.reshape(4, 256, D_MODEL) for a in gpwg], axis=1)
    parts = [gfwo.reshape(4, 256, D_MODEL), _to_shard_columns(gdwi, 2560), gdwo.reshape(4, 256, D_MODEL),
             up.reshape(4, 128, D_MODEL), gate.reshape(4, 512, D_MODEL)]
    return jnp.concatenate(parts, axis=1).astype(BF16)


def _local_step(x, p0, p1, tgt, fox_g, dil_g, fin_g, b_f, fwi, w_b, chip):
    S = x.shape[0]
    tm = min(256, S)
    nh = FOX_HEADS // 2
    w_qkv0 = jnp.concatenate([fwi[:, :D_MODEL] * 0.125, fwi[:, D_MODEL:3 * D_MODEL]], axis=1)
    w_z0 = fwi[:, 3 * D_MODEL:4 * D_MODEL]
    swapped = lambda a: jnp.repeat(a.reshape(-1, nh, 2)[:, :, ::-1].reshape(-1, FOX_HEADS), FOX_HEAD_DIM, axis=1)
    w_f0 = swapped(fwi[:, 4 * D_MODEL:])
    b_full = swapped(b_f.reshape(1, FOX_HEADS))

    n0, = _rows(_rms_fwd, [x], [(D_MODEL, BF16)], name="norm0", tm=tm, bcast=[fox_g])
    qkv0 = _mm(n0, w_qkv0, out_dtype=BF16, name="proj_qkv0")
    z0 = _mm(n0, w_z0, name="proj_z0")
    fl0 = _mm(n0, w_f0, name="proj_f0")
    qx0, kx0 = _cumsum_rows([fl0], name="forget_cumsum", pre=_log_forget, bcast=[b_full], post=_fox_extras,
                            out_dtypes=(BF16, BF16))
    o0, a0, g0, wall_b = _fox_fwd(qkv0, qx0, kx0, z0, w_b)
    fwo, dwi, dwo, pwu, pwg = _gathered_b_weights(wall_b, w_b, chip)
    w_qkv1 = dwi[:, :3 * DIL_QKV]
    w_z1 = dwi[:, 3 * DIL_QKV:]
    h1 = _mm(g0, fwo, add=x, name="out_proj0")
    u0 = _mm(p0, pwu[0], name="ple_up0")
    t0 = _mm(h1, pwg[0], name="ple_gate0")
    h2, n1 = _rows(_ple_norm, [h1, u0, t0], [(D_MODEL, F32), (D_MODEL, BF16)], name="ple_mix0_norm1", tm=tm,
                   bcast=[dil_g])

    qkv1 = _mm(n1, w_qkv1, name="proj_qkv1")
    z1 = _mm(n1, w_z1, name="proj_z1")
    o1, g1, lse1 = _dil_fwd(qkv1, z1)
    h3 = _mm(g1, dwo, add=h2, name="out_proj1")
    u1 = _mm(p1, pwu[1], name="ple_up1")
    t1 = _mm(h3, pwg[1], name="ple_gate1")

    du1, dt1, dh4, g_fin, loss = _rows(
        _ple_loss, [h3, u1, t1, tgt], [(D_MODEL, BF16), (D_MODEL, BF16), (D_MODEL, F32)], name="ple_mix1_loss_head",
        tm=tm, bcast=[fin_g], acc=[((1, D_MODEL), F32), ((1, LANES), F32)])
    g_up1 = _mm(p1, du1, ta=True, name="grad_ple_up1")
    g_gate1 = _mm(h3, dt1, ta=True, name="grad_ple_gate1")
    dh3 = _mm(dt1, pwg[1], tb=True, add=dh4, name="ple_back1")
    dg1 = _mm(dh3, dwo, tb=True, name="out_back1")
    g_dwo = _mm(g1, dh3, ta=True, name="grad_out1")
    do1, dz1, ld1 = _rows(_gate_bwd, [dg1, o1, z1, lse1], [(D_MODEL, F32), (D_MODEL, BF16), (D_MODEL, F32)],
                          name="gate_bwd1", tm=tm)
    dq1, dk1, dv1 = (lax.empty((S, DIL_QKV), BF16) for _ in range(3))
    for g in range(DIL_GROUPS):
        dq1 = _dil_bwd_q(qkv1, do1, ld1, dq1, g)
        dk1, dv1 = _dil_bwd_kv(qkv1, do1, ld1, dk1, dv1, g)
    g_dwi = _mm_tn_parts(n1, [dq1, dk1, dv1, dz1], name="grad_in1")
    dn1 = _mm_nt_parts([dq1, dk1, dv1, dz1], dwi, name="in_back1")
    dh2, du0, dt0, g_dil = _rows(_norm_ple_bwd, [dn1, h2, dh3, u0, t0], [(D_MODEL, F32), (D_MODEL, BF16), (D_MODEL, BF16)],
                                 name="norm_bwd1_ple_bwd0", tm=tm, bcast=[dil_g], acc=[((1, D_MODEL), F32)])
    g_up0 = _mm(p0, du0, ta=True, name="grad_ple_up0")
    g_gate0 = _mm(h1, dt0, ta=True, name="grad_ple_gate0")
    dh1 = _mm(dt0, pwg[0], tb=True, add=dh2, name="ple_back0")
    dg0 = _mm(dh1, fwo, tb=True, name="out_back0")
    g_fwo = _mm(g0, dh1, ta=True, name="grad_out0")
    do0, dz0, qxa0, dx0 = _rows(_gate_bwd_fox, [dg0, o0, z0, a0], [(D_MODEL, BF16)] * 4, name="gate_bwd0", tm=tm)
    g_b = _pack_grads_b(g_fwo, g_dwi, g_dwo, [g_up0, g_up1], [g_gate0, g_gate1])
    dq0, drx, dk0, dv0, dkx, got_b = _fox_bwd(qkv0, do0, qxa0, dx0, kx0, g_b)
    dlogf, = _cumsum_rows([drx, dkx], name="forget_cumsum_bwd", reverse=True, pre=_forget_dc)
    df0, g_bf = _rows(_forget_bwd, [dlogf, fl0], [(D_MODEL, BF16)], name="forget_bwd", tm=tm, bcast=[b_full],
                      acc=[((1, D_MODEL), F32)])
    g_fwi_main = _mm_tn_parts(n0, [dq0, dk0, dv0, dz0], name="grad_in0")
    g_fwi_f = _mm(n0, df0, ta=True, name="grad_in0_forget")
    dn0 = _mm_nt_parts([dq0, dk0, dv0, dz0], fwi[:, :4 * D_MODEL], name="in_back0")
    dn0 = _mm(df0, w_f0, tb=True, add=dn0, name="in_back0_forget")
    dx, g_fox = _rows(_rms_bwd, [dn0, x, dh1], [(D_MODEL, F32)], name="norm_bwd0", tm=tm, bcast=[fox_g],
                      acc=[((1, D_MODEL), F32)])
    first_lane = lambda a: a.reshape(-1, nh, 2, FOX_HEAD_DIM)[:, :, ::-1, 0].reshape(-1, FOX_HEADS)
    g_fwi = jnp.concatenate([g_fwi_main, first_lane(g_fwi_f)], axis=1)
    return loss, dx, g_fwi, g_b, got_b, (g_fox, _pad_row(first_lane(g_bf)), g_fin, g_dil)


def kernel(x, p, fox_norm, fox_w_in, fox_b_f, fox_w_out, dil_norm, dil_w_in, dil_w_out, ple_w_up, ple_w_gate, final_norm, loss_target, m_fox_norm, m_fox_w_in, m_fox_b_f, m_fox_w_out, m_dil_norm, m_dil_w_in, m_dil_w_out, m_ple_w_up, m_ple_w_gate, m_final_norm, v_fox_norm, v_fox_w_in, v_fox_b_f, v_fox_w_out, v_dil_norm, v_dil_w_in, v_dil_w_out, v_ple_w_up, v_ple_w_gate, v_final_norm):
    xi, yi, ci = lax.axis_index("x"), lax.axis_index("y"), lax.axis_index("c")
    chip = 2 * xi + yi

    w_flat = _pack_shard(fox_w_in, fox_w_out, dil_w_in, dil_w_out, ple_w_up, ple_w_gate)
    m_flat = _pack_shard(m_fox_w_in, m_fox_w_out, m_dil_w_in, m_dil_w_out, m_ple_w_up, m_ple_w_gate)
    v_flat = _pack_shard(v_fox_w_in, v_fox_w_out, v_dil_w_in, v_dil_w_out, v_ple_w_up, v_ple_w_gate)
    w_a, w_b = w_flat[:FLAT_A].astype(BF16), w_flat[FLAT_A:FLAT_USED].astype(BF16)
    fwi = _unpack_a(lax.dynamic_update_slice(_gather_weights(w_a), w_a[None], (chip, 0, 0)))
    gains = _gather_tiles(_pad_rows(_pad_row(dil_norm), 8), "gather_gain").reshape(4, 2, 8, D_MODEL)
    dil_g = gains[:, 0, 0, :256].reshape(1, D_MODEL)

    loss_part, dx, g_fwi, g_b, got_b, small = _local_step(
        x[0], p[0, 0], p[1, 0], loss_target[0], fox_norm.reshape(1, D_MODEL), dil_g,
        final_norm.reshape(1, D_MODEL), fox_b_f, fwi, w_b, chip)
    g_fox, g_bf, g_fin, g_dil = small
    sum4 = lambda a, b, c, d: ((a.astype(F32) + b.astype(F32)) + c.astype(F32)) + d.astype(F32)

    got_b = lax.dynamic_update_slice(got_b, lax.dynamic_slice_in_dim(g_b, chip, 1, axis=0), (chip, 0, 0))
    sum_b, = _rows(sum4, [got_b[s] for s in range(4)], [(D_MODEL, F32)], name="chip_sum_b", tm=FLAT_B // 8)
    g_flat_b, = _rows(lambda a, b: a + b, [sum_b, _sibling_half(sum_b, "sibling_sum_b")], [(D_MODEL, F32)],
                      name="pair_sum_b", tm=FLAT_B // 8)

    half = FLAT_A // 2
    g_a = _pack_grads_a(g_fwi)
    theirs = _swap_halves(g_a)
    mine = lax.dynamic_slice_in_dim(g_a, ci * half, half, axis=1)
    part, = _rows(lambda a, b: a + b, [mine.reshape(4 * half, D_MODEL), theirs.reshape(4 * half, D_MODEL)],
                  [(D_MODEL, BF16)], name="pair_sum", tm=half)
    part = part.reshape(4, half, D_MODEL)
    own = lax.dynamic_slice_in_dim(part, chip, 1, axis=0)
    by_chip = lax.dynamic_update_slice(_scatter_partials(part), own, (chip, 0, 0))
    half_sum, = _rows(sum4, [by_chip[s] for s in range(4)], [(D_MODEL, F32)], name="chip_sum", tm=half)
    other_half = _sibling_half(half_sum, "sibling_half")
    g_flat_a = jnp.where(ci == 0, jnp.concatenate([half_sum, other_half], axis=0),
                         jnp.concatenate([other_half, half_sum], axis=0))
    g_flat = jnp.concatenate([g_flat_a, g_flat_b, jnp.zeros((FLAT_TOTAL - FLAT_USED, D_MODEL), F32)], axis=0)

    tile = jnp.concatenate([g_fox, g_bf, g_fin, g_dil, jnp.pad(loss_part, ((0, 0), (0, D_MODEL - LANES))),
                            jnp.zeros((3, D_MODEL), F32)], axis=0)
    tiles = _gather_tiles(tile, "gather_small")
    tot, = _rows(lambda *t: functools.reduce(lambda a, b: a + b, t), [tiles[8 * s:8 * s + 8] for s in range(8)],
                 [(D_MODEL, F32)], name="small_sum", tm=8)
    loss = tot[4, 0]
    g_small = jnp.concatenate([
        tot[0:3],
        _pad_row(lax.dynamic_slice_in_dim(tot[3], chip * 256, 256)), jnp.zeros((4, D_MODEL), F32)], axis=0)
    small_tile = lambda a, b, c, d: jnp.concatenate(
        [_pad_row(a), _pad_row(b), _pad_row(c), _pad_row(d), jnp.zeros((4, D_MODEL), F32)], axis=0)
    w_small = small_tile(fox_norm, fox_b_f, final_norm, dil_norm)
    m_small = small_tile(m_fox_norm, m_fox_b_f, m_final_norm, m_dil_norm)
    v_small = small_tile(v_fox_norm, v_fox_b_f, v_final_norm, v_dil_norm)

    three = [(D_MODEL, F32)] * 3
    d_flat, nm_flat, nv_flat = _rows(_adamw, [w_flat, g_flat, m_flat, v_flat], three, name="adamw", tm=FLAT_TILE)
    d_small, nm_small, nv_small = _rows(_adamw, [w_small, g_small, m_small, v_small], three, name="adamw_small", tm=8)

    def leaves(flat, small_rows):
        fwi_, fwo_, dwi_, dwo_, pwu_, pwg_ = _unpack_shard(flat)
        return (small_rows[0:1], fwi_, small_rows[1:2, :FOX_HEADS], fwo_, small_rows[3:4, :256], dwi_, dwo_,
                pwu_, pwg_, small_rows[2])

    return (loss, dx[None], *leaves(g_flat, g_small), *leaves(d_flat, d_small), *leaves(nm_flat, nm_small),
            *leaves(nv_flat, nv_small))
```

```python
import functools

import numpy as np
import jax
import jax.numpy as jnp
from jax import lax
from jax.experimental import pallas as pl
from jax.experimental.pallas import tpu as pltpu

F32 = jnp.float32
BF16 = jnp.bfloat16

D_MODEL = 1024
PLE_DIM = 256
FOX_HEADS = 16
FOX_HEAD_DIM = 64
DIL_PATTERN = ((128, 1), (512, 4), (2048, 16))
DIL_GROUPS = 3
DIL_HEADS_PER_GROUP = 8
DIL_HEAD_DIM = 128
DIL_WINDOW_STEPS = 128
DIL_QKV = 3072
ALIBI_MAX_EXP = 8.0
RMS_EPS = 1e-6
ADAM_LR, ADAM_B1, ADAM_B2, ADAM_EPS, ADAM_WD, ADAM_STEP = 0.001, 0.9, 0.999, 1e-08, 0.01, 10

LANES = 128
VMEM_LIMIT = 56 * 1024 * 1024
MESH = pl.DeviceIdType.MESH
ANY = pl.BlockSpec(memory_space=pl.ANY)
NEG = -1e30

FLAT_ROWS = (("fox_w_in", 1028, 1056), ("fox_w_out", 256, 256), ("dil_w_in", 2560, 2560), ("dil_w_out", 256, 256),
             ("ple_w_up", 128, 128), ("ple_w_gate", 512, 512))
FLAT_USED = sum(r for _, _, r in FLAT_ROWS)
FLAT_TOTAL = 4864
FLAT_TILE = 608
FLAT_A = FLAT_ROWS[0][2]
FLAT_B = FLAT_USED - FLAT_A


def _cparams(*sem):
    return pltpu.CompilerParams(dimension_semantics=sem, vmem_limit_bytes=VMEM_LIMIT)


def _sigmoid(x):
    return 1.0 / (1.0 + jnp.exp(-x))


def _mm(a, b, *, name, ta=False, tb=False, out_dtype=F32, add=None):
    if ta:
        K, M = a.shape
    else:
        M, K = a.shape
    if tb:
        N, Kb = b.shape
    else:
        Kb, N = b.shape
    assert K == Kb, (a.shape, b.shape)
    tm, tn, tk = min(M, 1024), min(N, 1024), min(K, 1024)
    assert M % tm == 0 and N % tn == 0 and K % tk == 0, (M, N, K)
    nk = K // tk
    dn = (((0 if ta else 1,), (1 if tb else 0,)), ((), ()))

    def body(*refs):
        if add is None:
            a_ref, b_ref, o_ref, acc = refs
        else:
            a_ref, b_ref, add_ref, o_ref, acc = refs
        k = pl.program_id(2)

        @pl.when(k == 0)
        def _():
            acc[...] = jnp.zeros_like(acc)

        acc[...] += lax.dot_general(a_ref[...].astype(BF16), b_ref[...].astype(BF16), dn,
                                    preferred_element_type=F32)

        @pl.when(k == nk - 1)
        def _():
            r = acc[...]
            if add is not None:
                r = r + add_ref[...]
            o_ref[...] = r.astype(out_dtype)

    a_spec = (pl.BlockSpec((tk, tm), lambda i, j, k: (k, i)) if ta
              else pl.BlockSpec((tm, tk), lambda i, j, k: (i, k)))
    b_spec = (pl.BlockSpec((tn, tk), lambda i, j, k: (j, k)) if tb
              else pl.BlockSpec((tk, tn), lambda i, j, k: (k, j)))
    in_specs = [a_spec, b_spec]
    args = [a, b]
    if add is not None:
        in_specs.append(pl.BlockSpec((tm, tn), lambda i, j, k: (i, j)))
        args.append(add)
    return pl.pallas_call(
        body, name=name, grid=(M // tm, N // tn, nk),
        in_specs=in_specs, out_specs=pl.BlockSpec((tm, tn), lambda i, j, k: (i, j)),
        out_shape=jax.ShapeDtypeStruct((M, N), out_dtype),
        scratch_shapes=[pltpu.VMEM((tm, tn), F32)],
        compiler_params=_cparams("parallel", "parallel", "arbitrary"),
    )(*args)


def _mm_nt_parts(a_parts, b, *, name):
    M = a_parts[0].shape[0]
    N, K = b.shape
    tm, tn, tk = min(M, 1024), min(N, 1024), 1024
    nks = [a.shape[1] // tk for a in a_parts]
    offs = [sum(nks[:p]) for p in range(len(nks))]
    nk = sum(nks)
    assert nk * tk == K and M % tm == 0 and N % tn == 0, (M, N, K)
    n_parts = len(a_parts)

    def body(*refs):
        a_refs, b_ref, o_ref, acc = refs[:n_parts], refs[n_parts], refs[n_parts + 1], refs[n_parts + 2]
        k = pl.program_id(2)

        @pl.when(k == 0)
        def _():
            acc[...] = jnp.zeros_like(acc)

        for a_ref, off, n in zip(a_refs, offs, nks):
            @pl.when((k >= off) & (k < off + n))
            def _(a_ref=a_ref):
                acc[...] += lax.dot_general(a_ref[...].astype(BF16), b_ref[...].astype(BF16),
                                            (((1,), (1,)), ((), ())), preferred_element_type=F32)

        @pl.when(k == nk - 1)
        def _():
            o_ref[...] = acc[...]

    a_specs = [pl.BlockSpec((tm, tk), lambda i, j, k, off=off, n=n: (i, jnp.clip(k - off, 0, n - 1)))
               for off, n in zip(offs, nks)]
    return pl.pallas_call(
        body, name=name, grid=(M // tm, N // tn, nk),
        in_specs=a_specs + [pl.BlockSpec((tn, tk), lambda i, j, k: (j, k))],
        out_specs=pl.BlockSpec((tm, tn), lambda i, j, k: (i, j)),
        out_shape=jax.ShapeDtypeStruct((M, N), F32), scratch_shapes=[pltpu.VMEM((tm, tn), F32)],
        compiler_params=_cparams("parallel", "parallel", "arbitrary"),
    )(*a_parts, b)


def _mm_tn_parts(a, b_parts, *, name):
    K, M = a.shape
    tm, tn, tk = min(M, 1024), 1024, min(K, 1024)
    njs = [b.shape[1] // tn for b in b_parts]
    offs = [sum(njs[:p]) for p in range(len(njs))]
    nj, nk = sum(njs), K // tk
    assert M % tm == 0 and K % tk == 0 and all(b.shape[1] % tn == 0 for b in b_parts)
    n_parts = len(b_parts)

    def body(*refs):
        a_ref, b_refs, o_ref, acc = refs[0], refs[1:1 + n_parts], refs[1 + n_parts], refs[2 + n_parts]
        j, k = pl.program_id(1), pl.program_id(2)

        @pl.when(k == 0)
        def _():
            acc[...] = jnp.zeros_like(acc)

        for b_ref, off, n in zip(b_refs, offs, njs):
            @pl.when((j >= off) & (j < off + n))
            def _(b_ref=b_ref):
                acc[...] += lax.dot_general(a_ref[...].astype(BF16), b_ref[...].astype(BF16),
                                            (((0,), (0,)), ((), ())), preferred_element_type=F32)

        @pl.when(k == nk - 1)
        def _():
            o_ref[...] = acc[...]

    def b_spec(off, n):
        def index(i, j, k):
            mine = (j >= off) & (j < off + n)
            return jnp.where(mine, k, 0), jnp.clip(j - off, 0, n - 1)
        return pl.BlockSpec((tk, tn), index)

    return pl.pallas_call(
        body, name=name, grid=(M // tm, nj, nk),
        in_specs=[pl.BlockSpec((tk, tm), lambda i, j, k: (k, i))] + [b_spec(off, n) for off, n in zip(offs, njs)],
        out_specs=pl.BlockSpec((tm, tn), lambda i, j, k: (i, j)),
        out_shape=jax.ShapeDtypeStruct((M, nj * tn), F32), scratch_shapes=[pltpu.VMEM((tm, tn), F32)],
        compiler_params=_cparams("parallel", "parallel", "arbitrary"),
    )(a, *b_parts)


def _rows(fn, ins, outs, *, name, tm, bcast=(), acc=()):
    R = ins[0].shape[0]
    assert R % tm == 0, (R, tm)
    n_in, n_b, n_out, n_acc = len(ins), len(bcast), len(outs), len(acc)

    def body(*refs):
        in_refs = refs[:n_in + n_b]
        out_refs = refs[n_in + n_b:n_in + n_b + n_out]
        acc_refs = refs[n_in + n_b + n_out:]
        res = fn(*[r[...] for r in in_refs])
        if not isinstance(res, (tuple, list)):
            res = (res,)
        for r, v in zip(out_refs, res[:n_out]):
            r[...] = v.astype(r.dtype)
        first = pl.program_id(0) == 0
        for r, v in zip(acc_refs, res[n_out:]):
            @pl.when(first)
            def _(r=r, v=v):
                r[...] = v.astype(r.dtype)

            @pl.when(jnp.logical_not(first))
            def _(r=r, v=v):
                r[...] += v.astype(r.dtype)

    in_specs = [pl.BlockSpec((tm, a.shape[1]), lambda i: (i, 0)) for a in ins]
    in_specs += [pl.BlockSpec(b.shape, lambda i, nd=b.ndim: (0,) * nd) for b in bcast]
    out_specs = [pl.BlockSpec((tm, c), lambda i: (i, 0)) for c, _ in outs]
    out_specs += [pl.BlockSpec(s, lambda i, nd=len(s): (0,) * nd) for s, _ in acc]
    out_shape = [jax.ShapeDtypeStruct((R, c), dt) for c, dt in outs]
    out_shape += [jax.ShapeDtypeStruct(s, dt) for s, dt in acc]
    res = pl.pallas_call(
        body, name=name, grid=(R // tm,), in_specs=in_specs, out_specs=out_specs, out_shape=out_shape,
        compiler_params=_cparams("arbitrary" if acc else "parallel"),
    )(*ins, *bcast)
    return res


def _cumsum_rows(ins, *, name, reverse=False, pre=None, bcast=(), post=None, out_dtypes=(F32,)):
    S, C = ins[0].shape
    tb = 256
    nb = S // tb
    assert S % tb == 0
    n_in = len(ins) + len(bcast)
    n_out = len(out_dtypes)

    def body(*refs):
        in_refs, o_refs, carry = refs[:n_in], refs[n_in:n_in + n_out], refs[n_in + n_out]

        @pl.when(pl.program_id(0) == 0)
        def _():
            carry[...] = jnp.zeros_like(carry)

        xv = in_refs[0][...] if pre is None else pre(*[r[...] for r in in_refs])
        r_ = lax.broadcasted_iota(jnp.int32, (tb, tb), 0)
        c_ = lax.broadcasted_iota(jnp.int32, (tb, tb), 1)
        tri = jnp.where((c_ >= r_) if reverse else (c_ <= r_), 1.0, 0.0).astype(BF16)
        hi = xv.astype(BF16)
        r1 = xv - hi.astype(F32)
        mid = r1.astype(BF16)
        lo = (r1 - mid.astype(F32)).astype(BF16)
        cs = (jnp.dot(tri, hi, preferred_element_type=F32) + jnp.dot(tri, mid, preferred_element_type=F32)
              + jnp.dot(tri, lo, preferred_element_type=F32)) + carry[...]
        outs = (cs,) if post is None else post(cs)
        for o_ref, v in zip(o_refs, outs):
            o_ref[...] = v.astype(o_ref.dtype)
        carry[...] = cs[0:1, :] if reverse else cs[tb - 1:tb, :]

    blk = (lambda i: (nb - 1 - i, 0)) if reverse else (lambda i: (i, 0))
    in_specs = [pl.BlockSpec((tb, C), blk) for _ in ins]
    in_specs += [pl.BlockSpec(b.shape, lambda i, nd=b.ndim: (0,) * nd) for b in bcast]
    return pl.pallas_call(
        body, name=name, grid=(nb,), in_specs=in_specs, out_specs=[pl.BlockSpec((tb, C), blk)] * n_out,
        out_shape=[jax.ShapeDtypeStruct((S, C), dt) for dt in out_dtypes], scratch_shapes=[pltpu.VMEM((1, C), F32)],
        compiler_params=_cparams("arbitrary"),
    )(*ins, *bcast)


def _rms_fwd(x, g):
    r = lax.rsqrt(jnp.mean(x * x, axis=1, keepdims=True) + RMS_EPS)
    return (x * r) * g


def _rms_bwd(dn, x, dres, g):
    r = lax.rsqrt(jnp.mean(x * x, axis=1, keepdims=True) + RMS_EPS)
    xh = x * r
    w = dn * g
    dx = r * (w - xh * jnp.mean(w * xh, axis=1, keepdims=True))
    return dres + dx, jnp.sum(dn * xh, axis=0, keepdims=True)


def _final_stage(h, tgt, g):
    r = lax.rsqrt(jnp.mean(h * h, axis=1, keepdims=True) + RMS_EPS)
    xh = h * r
    diff = xh * g - tgt
    loss = 0.5 * jnp.sum(jnp.mean(diff * diff, axis=1, keepdims=True), axis=0, keepdims=True)
    dy = diff * (1.0 / D_MODEL)
    w = dy * g
    dh = r * (w - xh * jnp.mean(w * xh, axis=1, keepdims=True))
    return dh, jnp.sum(dy * xh, axis=0, keepdims=True), jnp.broadcast_to(loss, (1, LANES))


def _ple_fwd(h, u, t):
    return h + u * _sigmoid(t)


def _ple_bwd(dh, u, t):
    s = _sigmoid(t)
    return dh * s, dh * u * s * (1.0 - s)


def _ple_norm(h, u, t, g):
    h2 = _ple_fwd(h, u, t)
    return h2, _rms_fwd(h2, g)


def _ple_loss(h, u, t, tgt, g):
    dh, g_gain, loss = _final_stage(_ple_fwd(h, u, t), tgt, g)
    du, dt = _ple_bwd(dh, u, t)
    return du, dt, dh, g_gain, loss


def _norm_ple_bwd(dn, x, dres, u, t, g):
    dh, g_gain = _rms_bwd(dn, x, dres, g)
    du, dt = _ple_bwd(dh, u, t)
    return dh, du, dt, g_gain


def _gate_fwd(o, z):
    return o * (z * _sigmoid(z))


def _head_sums(prod):
    tm, width = prod.shape
    cols = [jnp.broadcast_to(jnp.sum(prod[:, b * LANES:(b + 1) * LANES], axis=1, keepdims=True), (tm, LANES))
            for b in range(width // LANES)]
    return jnp.concatenate(cols, axis=1)


def _gate_bwd(dg, o, z, lse):
    s = _sigmoid(z)
    do = dg * (z * s)
    dz = dg * o * (s * (1.0 + z * (1.0 - s)))
    lane = lax.broadcasted_iota(jnp.int32, do.shape, 1) % LANES
    return do, dz, jnp.where(lane < LANES // 2, lse, _head_sums(do * o))


def _log_forget(fl, b):
    u = fl + b
    return jnp.minimum(u, 0.0) - jnp.log(1.0 + jnp.exp(-jnp.abs(u)))


def _adamw(w, g, m, v):
    m = ADAM_B1 * m + (1.0 - ADAM_B1) * g
    v = ADAM_B2 * v + (1.0 - ADAM_B2) * (g * g)
    m_hat = m / (1.0 - ADAM_B1 ** ADAM_STEP)
    v_hat = v / (1.0 - ADAM_B2 ** ADAM_STEP)
    delta = -ADAM_LR * (m_hat / (jnp.sqrt(v_hat) + ADAM_EPS) + ADAM_WD * w)
    return delta, m, v


FOX_TQ = 1024


def _split3(x):
    p1 = x.astype(BF16).astype(F32)
    r = x - p1
    p2 = r.astype(BF16).astype(F32)
    return p1, p2, r - p2


def _lane_in_head(shape):
    return lax.broadcasted_iota(jnp.int32, shape, 1) % FOX_HEAD_DIM


def _query_extras(x):
    lm = _lane_in_head(x.shape)
    p1, p2, p3 = _split3(x)
    return jnp.where(lm == 0, p1, jnp.where(lm == 1, p2, jnp.where(lm == 2, p3, jnp.where(lm < 6, 1.0, 0.0))))


def _key_extras(c):
    lm = _lane_in_head(c.shape)
    p1, p2, p3 = _split3(c)
    return jnp.where(lm < 3, 1.0, jnp.where(lm == 3, -p1, jnp.where(lm == 4, -p2, jnp.where(lm == 5, -p3, 0.0))))


def _fox_extras(c):
    return _query_extras(c), _key_extras(c)


def _swapped_head_sums(prod):
    tm, width = prod.shape
    lane = lax.broadcasted_iota(jnp.int32, (tm, LANES), 1)
    low = lane < FOX_HEAD_DIM
    cols = []
    for b in range(width // LANES):
        blk = prod[:, b * LANES:(b + 1) * LANES]
        sa = jnp.sum(jnp.where(low, blk, 0.0), axis=1, keepdims=True)
        sb = jnp.sum(jnp.where(low, 0.0, blk), axis=1, keepdims=True)
        cols.append(jnp.where(low, sb, sa))
    return jnp.concatenate(cols, axis=1)


def _gate_bwd_fox(dg, o, z, a):
    s = _sigmoid(z)
    do = dg * (z * s)
    dz = dg * o * (s * (1.0 + z * (1.0 - s)))
    lm = _lane_in_head(do.shape)
    d1, d2, d3 = _split3(-_swapped_head_sums(do * o))
    dx = jnp.where(lm == 0, d1, jnp.where(lm == 1, d2, jnp.where(lm == 2, d3, 0.0)))
    return do, dz, _query_extras(a), dx


def _forget_dc(drx, dkx):
    lm = _lane_in_head(drx.shape)
    return jnp.where(lm == 0, drx, 0.0) - jnp.where(lm == 3, dkx, 0.0)


def _forget_bwd(dl, fl, b):
    lm = _lane_in_head(dl.shape)
    width = dl.shape[1]
    both = dl + pltpu.roll(dl, 3, 1) + pltpu.roll(dl, width - 3, 1)
    du = jnp.where(lm == 0, both, 0.0) * (1.0 / (1.0 + jnp.exp(fl + b)))
    return du, jnp.sum(du, axis=0, keepdims=True)


def _chip_exchange(src_of, out_ref, send, recv):
    x, y, c = lax.axis_index("x"), lax.axis_index("y"), lax.axis_index("c")
    chips = [(1 - x, y), (x, 1 - y), (1 - x, 1 - y)]
    me = 2 * x + y
    sends = [pltpu.make_async_remote_copy(src_ref=src_of(cx, cy), dst_ref=out_ref.at[me], send_sem=send.at[k],
                                          recv_sem=recv.at[k], device_id=(cx, cy, c), device_id_type=MESH)
             for k, (cx, cy) in enumerate(chips)]

    def start():
        for cp in sends:
            cp.start()

    def wait():
        for k, (cx, cy) in enumerate(chips):
            slot = out_ref.at[2 * cx + cy]
            pltpu.make_async_remote_copy(src_ref=slot, dst_ref=slot, send_sem=send.at[k], recv_sem=recv.at[k],
                                         device_id=(cx, cy, c), device_id_type=MESH).wait_recv()
        for cp in sends:
            cp.wait_send()

    return start, wait


def _fox_fwd(qkv, qx, kx, z, w_b):
    S = qkv.shape[0]
    tq = min(FOX_TQ, S)
    nq = S // tq
    nt = (((1,), (1,)), ((), ()))

    def body(q_ref, qx_ref, k_ref, v_ref, kx_ref, z_ref, wb_ref, o_ref, a_ref, g_ref, wall_ref, send, recv):
        i = pl.program_id(1)
        start, wait = _chip_exchange(lambda cx, cy: wb_ref, wall_ref, send, recv)
        pl.when((pl.program_id(0) == 0) & (i == 0))(start)
        low = lax.broadcasted_iota(jnp.int32, (tq, LANES), 1) < FOX_HEAD_DIM
        row = lax.broadcasted_iota(jnp.int32, (tq, tq), 0)
        col = lax.broadcasted_iota(jnp.int32, (tq, tq), 1)
        q2, x2 = q_ref[...], qx_ref[...]
        qa = (jnp.where(low, q2, x2), jnp.where(low, x2, q2))

        def step(kb, carry, diag):
            start = pl.multiple_of(kb * tq, tq)
            k2 = k_ref[pl.ds(start, tq), :]
            v2 = v_ref[pl.ds(start, tq), :]
            y2 = kx_ref[pl.ds(start, tq), :]
            one = jnp.ones_like(v2)
            ka = (jnp.where(low, k2, y2), jnp.where(low, y2, k2))
            va = (jnp.where(low, v2, one), jnp.where(low, one, v2))
            out = []
            for hh in range(2):
                m, acc = carry[hh]
                s = lax.dot_general(qa[hh], ka[hh], nt, preferred_element_type=F32)
                if diag:
                    s = jnp.where(col <= row, s, NEG)
                m_new = jnp.maximum(m, jnp.max(s, axis=1, keepdims=True))
                pr = jnp.exp(s - m_new)
                acc = jnp.exp(m - m_new) * acc + jnp.dot(pr.astype(BF16), va[hh], preferred_element_type=F32)
                out.append((m_new, acc))
            return tuple(out)

        init = ((jnp.full((tq, 1), NEG, F32), jnp.zeros((tq, LANES), F32)),) * 2
        carry = lax.fori_loop(0, i, functools.partial(step, diag=False), init)
        (m_a, acc_a), (m_b, acc_b) = step(i, carry, True)
        l_a, l_b = acc_a[:, FOX_HEAD_DIM:FOX_HEAD_DIM + 1], acc_b[:, 0:1]
        xf = x2.astype(F32)
        c_a = xf[:, 64:65] + xf[:, 65:66] + xf[:, 66:67]
        c_b = xf[:, 0:1] + xf[:, 1:2] + xf[:, 2:3]
        o = jnp.where(low, acc_a / l_a, acc_b / l_b)
        o_ref[...] = o
        g_ref[...] = _gate_fwd(o, z_ref[...]).astype(BF16)
        a_ref[...] = jnp.where(low, jnp.broadcast_to(c_b - (m_b + jnp.log(l_b)), (tq, LANES)),
                               jnp.broadcast_to(c_a - (m_a + jnp.log(l_a)), (tq, LANES)))
        pl.when((pl.program_id(0) == FOX_HEADS // 2 - 1) & (i == nq - 1))(wait)

    blk = lambda cb: pl.BlockSpec((tq, LANES), lambda h, i, cb=cb: (i, cb + h))
    res = lambda cb: pl.BlockSpec((S, LANES), lambda h, i, cb=cb: (0, cb + h))
    return pl.pallas_call(
        body, name="fox_attn_fwd", grid=(FOX_HEADS // 2, nq),
        in_specs=[blk(0), blk(0), res(8), res(16), res(0), blk(0), ANY],
        out_specs=[blk(0), blk(0), blk(0), ANY],
        out_shape=[jax.ShapeDtypeStruct((S, D_MODEL), F32), jax.ShapeDtypeStruct((S, D_MODEL), F32),
                   jax.ShapeDtypeStruct((S, D_MODEL), BF16), jax.ShapeDtypeStruct((4,) + w_b.shape, w_b.dtype)],
        scratch_shapes=[pltpu.SemaphoreType.DMA((3,)), pltpu.SemaphoreType.DMA((3,))],
        compiler_params=_cparams("arbitrary", "arbitrary"),
    )(qkv, qx, qkv, qkv, kx, z, w_b)


def _fox_bwd(qkv, do, qxa, dx, kx, g_b):
    S = qkv.shape[0]
    tq = min(FOX_TQ, S)
    nq = S // tq
    nt = (((1,), (1,)), ((), ()))
    tn = (((0,), (0,)), ((), ()))

    def body(q_ref, qx_ref, do_ref, dx_ref, k_ref, v_ref, kx_ref, gb_ref, dq_ref, dr_ref, dk_ref, dv_ref, dkx_ref,
             got_ref, send, recv):
        kb = pl.program_id(1)
        start, wait = _chip_exchange(lambda cx, cy: gb_ref.at[2 * cx + cy], got_ref, send, recv)
        pl.when((pl.program_id(0) == 0) & (kb == 0))(start)
        low = lax.broadcasted_iota(jnp.int32, (tq, LANES), 1) < FOX_HEAD_DIM
        row = lax.broadcasted_iota(jnp.int32, (tq, tq), 0)
        col = lax.broadcasted_iota(jnp.int32, (tq, tq), 1)

        @pl.when(kb == 0)
        def _():
            dq_ref[...] = jnp.zeros_like(dq_ref)
            dr_ref[...] = jnp.zeros_like(dr_ref)

        k2, v2, y2 = k_ref[...], v_ref[...], kx_ref[...]
        one = jnp.ones_like(v2)
        ka = (jnp.where(low, k2, y2), jnp.where(low, y2, k2))
        va = (jnp.where(low, v2, one), jnp.where(low, one, v2))

        def step(qb, carry, diag):
            start = pl.multiple_of(qb * tq, tq)
            q2 = q_ref[pl.ds(start, tq), :]
            x2 = qx_ref[pl.ds(start, tq), :]
            d2 = do_ref[pl.ds(start, tq), :]
            e2 = dx_ref[pl.ds(start, tq), :]
            qa = (jnp.where(low, q2, x2), jnp.where(low, x2, q2))
            da = (jnp.where(low, d2, e2), jnp.where(low, e2, d2))
            new, res = [], []
            for hh in range(2):
                dk, dv = carry[hh]
                s = lax.dot_general(qa[hh], ka[hh], nt, preferred_element_type=F32)
                if diag:
                    s = jnp.where(col <= row, s, NEG)
                pr = jnp.exp(s)
                ds = pr * lax.dot_general(da[hh], va[hh], nt, preferred_element_type=F32)
                prb, dsb = pr.astype(BF16), ds.astype(BF16)
                dv = dv + lax.dot_general(prb, da[hh], tn, preferred_element_type=F32)
                dk = dk + lax.dot_general(dsb, qa[hh], tn, preferred_element_type=F32)
                res.append(jnp.dot(dsb, ka[hh], preferred_element_type=F32))
                new.append((dk, dv))
            dq_ref[pl.ds(start, tq), :] += jnp.where(low, res[0], res[1])
            dr_ref[pl.ds(start, tq), :] += jnp.where(low, res[1], res[0])
            return tuple(new)

        init = ((jnp.zeros((tq, LANES), F32), jnp.zeros((tq, LANES), F32)),) * 2
        carry = step(kb, init, True)
        (dk_a, dv_a), (dk_b, dv_b) = lax.fori_loop(kb + 1, nq, functools.partial(step, diag=False), carry)
        dk_ref[...] = jnp.where(low, dk_a, dk_b).astype(BF16)
        dv_ref[...] = jnp.where(low, dv_a, dv_b).astype(BF16)
        dkx_ref[...] = jnp.where(low, dk_b, dk_a)

        @pl.when(kb == nq - 1)
        def _():
            dq_ref[...] = dq_ref[...] * (FOX_HEAD_DIM ** -0.5)

        pl.when((pl.program_id(0) == FOX_HEADS // 2 - 1) & (kb == nq - 1))(wait)

    res = lambda cb: pl.BlockSpec((S, LANES), lambda h, k, cb=cb: (0, cb + h))
    blk = lambda cb: pl.BlockSpec((tq, LANES), lambda h, k, cb=cb: (k, cb + h))
    f32, b16 = jax.ShapeDtypeStruct((S, D_MODEL), F32), jax.ShapeDtypeStruct((S, D_MODEL), BF16)
    return pl.pallas_call(
        body, name="fox_attn_bwd", grid=(FOX_HEADS // 2, nq),
        in_specs=[res(0), res(0), res(0), res(0), blk(8), blk(16), blk(0), ANY],
        out_specs=[res(0), res(0), blk(0), blk(0), blk(0), ANY],
        out_shape=[f32, f32, b16, b16, f32, jax.ShapeDtypeStruct(g_b.shape, g_b.dtype)],
        scratch_shapes=[pltpu.SemaphoreType.DMA((3,)), pltpu.SemaphoreType.DMA((3,))],
        compiler_params=_cparams("arbitrary", "arbitrary"),
    )(qkv, qxa, do, dx, qkv, qkv, kx, g_b)


def _alibi_slopes():
    n = DIL_GROUPS * DIL_HEADS_PER_GROUP
    s = np.float32(2.0) ** (np.float32(-ALIBI_MAX_EXP) * np.arange(1, n + 1, dtype=np.float32) / np.float32(n))
    return s.astype(np.float32).reshape(DIL_GROUPS, DIL_HEADS_PER_GROUP)


W = DIL_WINDOW_STEPS
DIL_SCALE = DIL_HEAD_DIM ** -0.5


DIL_ROWS = 2048
NT = (((1,), (1,)), ((), ()))
TN = (((0,), (0,)), ((), ()))
QKV_BLOCKS = DIL_QKV // LANES


def _slope_table():
    t = np.zeros((DIL_HEADS_PER_GROUP, 8, LANES), np.float32)
    t[:, :DIL_GROUPS, :] = _alibi_slopes().T[:, :, None]
    return jnp.asarray(t)


def _phase_rows(start, d):
    return pl.ds(start, W, stride=d) if d > 1 else pl.ds(start, W)


def _for_phases(d, unit):
    if d == 1:
        unit(0)
    else:
        lax.fori_loop(0, d, lambda r, c: (unit(r), c)[1], 0)


def _window_geometry(shape, q_axis_offset):
    i_ = lax.broadcasted_iota(jnp.int32, shape, 0)
    j_ = lax.broadcasted_iota(jnp.int32, shape, 1)
    dist = q_axis_offset + i_ - j_
    return i_, j_, dist, (dist >= 0) & (dist <= W)


def _dil_fwd(qkv, z):
    S = qkv.shape[0]
    rb = min(DIL_ROWS, S)
    nb = S // rb
    spans = [W * d for _, d in DIL_PATTERN]

    def body(*refs):
        slope_ref = refs[0]
        q, kc, kp, vc, vp = refs[1:4], refs[4:7], refs[7:10], refs[10:13], refs[13:16]
        z_ref, o_ref, g_ref, l_ref = refs[16:20]
        og, lg = refs[20:23], refs[23:26]
        i = pl.program_id(1)
        _, j_, dist, inwin = _window_geometry((W, 2 * W), W)
        distf = dist.astype(F32)
        for g, (_, d) in enumerate(DIL_PATTERN):
            span = spans[g]
            bias = -(slope_ref[0, g:g + 1, 0:1] * float(d)) * distf
            for b in range(rb // span):
                def unit(r, g=g, d=d, span=span, b=b, bias=bias):
                    base = b * span + r
                    rows = _phase_rows(base, d)
                    if b > 0:
                        kprev, vprev = kc[g][_phase_rows(base - span, d), :], vc[g][_phase_rows(base - span, d), :]
                        valid = inwin
                    else:
                        kprev, vprev = kp[g][_phase_rows(r, d), :], vp[g][_phase_rows(r, d), :]
                        valid = inwin & ((j_ >= W) | (i > 0))
                    k2 = jnp.concatenate([kprev, kc[g][rows, :]], axis=0).astype(BF16)
                    v2 = jnp.concatenate([vprev, vc[g][rows, :]], axis=0).astype(BF16)
                    s = lax.dot_general(q[g][rows, :].astype(BF16), k2, NT, preferred_element_type=F32)
                    s = jnp.where(valid, s * DIL_SCALE + bias, NEG)
                    m = jnp.max(s, axis=1, keepdims=True)
                    pr = jnp.exp(s - m)
                    l = jnp.sum(pr, axis=1, keepdims=True)
                    og[g][rows, :] = jnp.dot(pr.astype(BF16), v2, preferred_element_type=F32) / l
                    lg[g][rows, :] = jnp.broadcast_to(m + jnp.log(l), (W, LANES))

                _for_phases(d, unit)

        def mix(cix, c):
            sl = pl.ds(pl.multiple_of(cix * 256, 256), 256)
            l1, l2, l3 = lg[0][sl, :], lg[1][sl, :], lg[2][sl, :]
            m = jnp.maximum(jnp.maximum(l1, l2), l3)
            e1, e2, e3 = jnp.exp(l1 - m), jnp.exp(l2 - m), jnp.exp(l3 - m)
            tot = e1 + e2 + e3
            o = (e1 * og[0][sl, :] + e2 * og[1][sl, :] + e3 * og[2][sl, :]) / tot
            o_ref[sl, :] = o
            g_ref[sl, :] = _gate_fwd(o, z_ref[sl, :]).astype(BF16)
            l_ref[sl, :] = m + jnp.log(tot)
            return c

        lax.fori_loop(0, rb // 256, mix, 0)

    cur = lambda off, g: pl.BlockSpec((rb, LANES), lambda h, i: (i, off * QKV_BLOCKS + g * DIL_HEADS_PER_GROUP + h))
    prev = lambda off, g: pl.BlockSpec(
        (spans[g], LANES),
        lambda h, i: (jnp.maximum(i * (rb // spans[g]) - 1, 0), off * QKV_BLOCKS + g * DIL_HEADS_PER_GROUP + h))
    row = pl.BlockSpec((rb, LANES), lambda h, i: (i, h))
    groups = range(DIL_GROUPS)
    in_specs = [pl.BlockSpec((1, 8, LANES), lambda h, i: (h, 0, 0))]
    in_specs += [cur(0, g) for g in groups] + [cur(1, g) for g in groups] + [prev(1, g) for g in groups]
    in_specs += [cur(2, g) for g in groups] + [prev(2, g) for g in groups] + [row]
    f32 = jax.ShapeDtypeStruct((S, D_MODEL), F32)
    return pl.pallas_call(
        body, name="dil_attn_fwd", grid=(DIL_HEADS_PER_GROUP, nb), in_specs=in_specs,
        out_specs=[row, row, row], out_shape=[f32, jax.ShapeDtypeStruct((S, D_MODEL), BF16), f32],
        scratch_shapes=[pltpu.VMEM((rb, LANES), F32)] * 6,
        compiler_params=_cparams("parallel", "parallel"),
    )(_slope_table(), *([qkv] * 15), z)


def _dil_bwd_q(qkv, do, ld, buf, g):
    S = qkv.shape[0]
    rb = min(DIL_ROWS, S)
    nb = S // rb
    d = DIL_PATTERN[g][1]
    span = W * d

    def body(slope_ref, q_ref, kc, kp, vc, vp, do_ref, ld_ref, buf_ref, dq_ref, dq_f32):
        i = pl.program_id(1)
        _, j_, dist, inwin = _window_geometry((W, 2 * W), W)
        bias = -(slope_ref[0, g:g + 1, 0:1] * float(d)) * dist.astype(F32)
        for b in range(rb // span):
            def unit(r, b=b):
                base = b * span + r
                rows = _phase_rows(base, d)
                if b > 0:
                    kprev, vprev = kc[_phase_rows(base - span, d), :], vc[_phase_rows(base - span, d), :]
                    valid = inwin
                else:
                    kprev, vprev = kp[_phase_rows(r, d), :], vp[_phase_rows(r, d), :]
                    valid = inwin & ((j_ >= W) | (i > 0))
                k2 = jnp.concatenate([kprev, kc[rows, :]], axis=0).astype(BF16)
                v2 = jnp.concatenate([vprev, vc[rows, :]], axis=0).astype(BF16)
                ld = ld_ref[rows, :]
                s = lax.dot_general(q_ref[rows, :].astype(BF16), k2, NT, preferred_element_type=F32)
                s = jnp.where(valid, s * DIL_SCALE + bias - ld[:, 0:1], NEG)
                pr = jnp.exp(s)
                dp = lax.dot_general(do_ref[rows, :].astype(BF16), v2, NT, preferred_element_type=F32)
                ds = pr * (dp - ld[:, LANES // 2:LANES // 2 + 1])
                dq_f32[rows, :] = jnp.dot(ds.astype(BF16), k2, preferred_element_type=F32) * DIL_SCALE

            _for_phases(d, unit)
        dq_ref[...] = dq_f32[...].astype(BF16)

    col = lambda off: off * QKV_BLOCKS + g * DIL_HEADS_PER_GROUP
    cur = lambda off: pl.BlockSpec((rb, LANES), lambda h, i: (i, col(off) + h))
    prev = lambda off: pl.BlockSpec((span, LANES), lambda h, i: (jnp.maximum(i * (rb // span) - 1, 0), col(off) + h))
    row = pl.BlockSpec((rb, LANES), lambda h, i: (i, h))
    return pl.pallas_call(
        body, name=f"dil_attn_bwd_q_g{g}", grid=(DIL_HEADS_PER_GROUP, nb),
        in_specs=[pl.BlockSpec((1, 8, LANES), lambda h, i: (h, 0, 0)), cur(0), cur(1), prev(1), cur(2), prev(2),
                  row, row, pl.BlockSpec(memory_space=pl.ANY)],
        out_specs=pl.BlockSpec((rb, LANES), lambda h, i: (i, g * DIL_HEADS_PER_GROUP + h)),
        out_shape=jax.ShapeDtypeStruct(buf.shape, buf.dtype), input_output_aliases={8: 0},
        scratch_shapes=[pltpu.VMEM((rb, LANES), F32)],
        compiler_params=_cparams("parallel", "parallel"),
    )(_slope_table(), qkv, qkv, qkv, qkv, qkv, do, ld, buf)


def _dil_bwd_kv(qkv, do, ld, bufk, bufv, g):
    S = qkv.shape[0]
    rb = min(DIL_ROWS, S)
    nb = S // rb
    d = DIL_PATTERN[g][1]
    span = W * d
    nub = rb // span

    def body(slope_ref, k_ref, v_ref, qc, qn, doc, don, ldc, ldn, bufk_ref, bufv_ref, dk_ref, dv_ref, dk_f32, dv_f32):
        i = pl.program_id(1)
        i_, _, dist, inwin = _window_geometry((2 * W, W), 0)
        bias = -(slope_ref[0, g:g + 1, 0:1] * float(d)) * dist.astype(F32)
        for b in range(nub):
            def unit(r, b=b):
                base = b * span + r
                rows = _phase_rows(base, d)
                if b < nub - 1:
                    nxt = _phase_rows(base + span, d)
                    qnext, donext, ldnext = qc[nxt, :], doc[nxt, :], ldc[nxt, :]
                    valid = inwin
                else:
                    nxt = _phase_rows(r, d)
                    qnext, donext, ldnext = qn[nxt, :], don[nxt, :], ldn[nxt, :]
                    valid = inwin & ((i_ < W) | (i < nb - 1))
                q2 = jnp.concatenate([qc[rows, :], qnext], axis=0).astype(BF16)
                do2 = jnp.concatenate([doc[rows, :], donext], axis=0).astype(BF16)
                ld2 = jnp.concatenate([ldc[rows, :], ldnext], axis=0)
                s = lax.dot_general(q2, k_ref[rows, :].astype(BF16), NT, preferred_element_type=F32)
                s = jnp.where(valid, s * DIL_SCALE + bias - ld2[:, 0:1], NEG)
                pr = jnp.exp(s)
                dp = lax.dot_general(do2, v_ref[rows, :].astype(BF16), NT, preferred_element_type=F32)
                ds = pr * (dp - ld2[:, LANES // 2:LANES // 2 + 1])
                dv_f32[rows, :] = lax.dot_general(pr.astype(BF16), do2, TN, preferred_element_type=F32)
                dk_f32[rows, :] = lax.dot_general(ds.astype(BF16), q2, TN, preferred_element_type=F32) * DIL_SCALE

            _for_phases(d, unit)
        dk_ref[...] = dk_f32[...].astype(BF16)
        dv_ref[...] = dv_f32[...].astype(BF16)

    col = lambda off: off * QKV_BLOCKS + g * DIL_HEADS_PER_GROUP
    cur = lambda off: pl.BlockSpec((rb, LANES), lambda h, i: (i, col(off) + h))
    nxt_blk = lambda i: jnp.minimum((i + 1) * nub, S // span - 1)
    nxt = lambda off: pl.BlockSpec((span, LANES), lambda h, i: (nxt_blk(i), col(off) + h))
    row = pl.BlockSpec((rb, LANES), lambda h, i: (i, h))
    row_nxt = pl.BlockSpec((span, LANES), lambda h, i: (nxt_blk(i), h))
    any_ = pl.BlockSpec(memory_space=pl.ANY)
    out = pl.BlockSpec((rb, LANES), lambda h, i: (i, g * DIL_HEADS_PER_GROUP + h))
    return pl.pallas_call(
        body, name=f"dil_attn_bwd_kv_g{g}", grid=(DIL_HEADS_PER_GROUP, nb),
        in_specs=[pl.BlockSpec((1, 8, LANES), lambda h, i: (h, 0, 0)), cur(1), cur(2), cur(0), nxt(0),
                  row, row_nxt, row, row_nxt, any_, any_],
        out_specs=[out, out],
        out_shape=[jax.ShapeDtypeStruct(bufk.shape, bufk.dtype), jax.ShapeDtypeStruct(bufv.shape, bufv.dtype)],
        input_output_aliases={9: 0, 10: 1},
        scratch_shapes=[pltpu.VMEM((rb, LANES), F32)] * 2,
        compiler_params=_cparams("parallel", "parallel"),
    )(_slope_table(), qkv, qkv, qkv, qkv, do, do, ld, ld, bufk, bufv)


def _place():
    x, y, c = lax.axis_index("x"), lax.axis_index("y"), lax.axis_index("c")
    chips = [(1 - x, y), (x, 1 - y), (1 - x, 1 - y)]
    return x, y, c, chips


def _gather_weights(wb):
    R = wb.shape[0]
    H = R // 2

    def body(w_ref, out_ref, send1, recv1, send2, recv2):
        x, y, c, chips = _place()
        me = 2 * x + y
        sib = (x, y, 1 - c)
        half = pl.ds(c * H, H)
        first = [pltpu.make_async_remote_copy(
            src_ref=w_ref.at[half], dst_ref=out_ref.at[me, half], send_sem=send1.at[k], recv_sem=recv1.at[k],
            device_id=(*chip, c), device_id_type=MESH) for k, chip in enumerate(chips)]
        for cp in first:
            cp.start()
        passed = []
        for k, (cx, cy) in enumerate(chips):
            slot = out_ref.at[2 * cx + cy, half]
            pltpu.make_async_remote_copy(src_ref=slot, dst_ref=slot, send_sem=send1.at[k], recv_sem=recv1.at[k],
                                         device_id=(cx, cy, c), device_id_type=MESH).wait_recv()
            cp = pltpu.make_async_remote_copy(src_ref=slot, dst_ref=slot, send_sem=send2.at[k], recv_sem=recv2.at[k],
                                              device_id=sib, device_id_type=MESH)
            cp.start()
            passed.append(cp)
        for k, (cx, cy) in enumerate(chips):
            slot = out_ref.at[2 * cx + cy, pl.ds((1 - c) * H, H)]
            pltpu.make_async_remote_copy(src_ref=slot, dst_ref=slot, send_sem=send2.at[k], recv_sem=recv2.at[k],
                                         device_id=sib, device_id_type=MESH).wait_recv()
        for cp in first + passed:
            cp.wait_send()

    return pl.pallas_call(
        body, name="gather_weights", in_specs=[ANY], out_specs=ANY,
        out_shape=jax.ShapeDtypeStruct((4, R, D_MODEL), wb.dtype),
        scratch_shapes=[pltpu.SemaphoreType.DMA((3,)), pltpu.SemaphoreType.DMA((3,)),
                        pltpu.SemaphoreType.DMA((3,)), pltpu.SemaphoreType.DMA((3,))],
    )(wb)


def _swap_halves(g):
    H = g.shape[1] // 2

    def body(g_ref, out_ref, send, recv):
        x, y, c, _ = _place()
        sib = (x, y, 1 - c)
        cps = [pltpu.make_async_remote_copy(
            src_ref=g_ref.at[s, pl.ds((1 - c) * H, H)], dst_ref=out_ref.at[s], send_sem=send.at[s],
            recv_sem=recv.at[s], device_id=sib, device_id_type=MESH) for s in range(4)]
        for cp in cps:
            cp.start()
        for cp in cps:
            cp.wait()

    return pl.pallas_call(
        body, name="swap_halves", in_specs=[ANY], out_specs=ANY,
        out_shape=jax.ShapeDtypeStruct((4, H, D_MODEL), g.dtype),
        scratch_shapes=[pltpu.SemaphoreType.DMA((4,)), pltpu.SemaphoreType.DMA((4,))],
    )(g)


def _scatter_partials(pb):
    def body(p_ref, out_ref, send, recv):
        x, y, c, chips = _place()
        me = 2 * x + y
        cps = [pltpu.make_async_remote_copy(
            src_ref=p_ref.at[2 * cx + cy], dst_ref=out_ref.at[me], send_sem=send.at[k], recv_sem=recv.at[k],
            device_id=(cx, cy, c), device_id_type=MESH) for k, (cx, cy) in enumerate(chips)]
        for cp in cps:
            cp.start()
        for k, (cx, cy) in enumerate(chips):
            slot = out_ref.at[2 * cx + cy]
            pltpu.make_async_remote_copy(src_ref=slot, dst_ref=slot, send_sem=send.at[k], recv_sem=recv.at[k],
                                         device_id=(cx, cy, c), device_id_type=MESH).wait_recv()
        for cp in cps:
            cp.wait_send()

    return pl.pallas_call(
        body, name="scatter_partials", in_specs=[ANY], out_specs=ANY,
        out_shape=jax.ShapeDtypeStruct(pb.shape, pb.dtype),
        scratch_shapes=[pltpu.SemaphoreType.DMA((3,)), pltpu.SemaphoreType.DMA((3,))],
    )(pb)


def _sibling_half(f, name):
    def body(f_ref, out_ref, send, recv):
        x, y, c, _ = _place()
        cp = pltpu.make_async_remote_copy(src_ref=f_ref, dst_ref=out_ref, send_sem=send, recv_sem=recv,
                                          device_id=(x, y, 1 - c), device_id_type=MESH)
        cp.start()
        cp.wait()

    return pl.pallas_call(
        body, name=name, in_specs=[ANY], out_specs=ANY,
        out_shape=jax.ShapeDtypeStruct(f.shape, f.dtype),
        scratch_shapes=[pltpu.SemaphoreType.DMA, pltpu.SemaphoreType.DMA],
    )(f)


def _gather_tiles(tile, name):
    m_per = tile.shape[0]

    def body(x_ref, out_ref, send_sems, recv_sems, local_sem):
        x, y, c, chips = _place()
        me, sibling = (x, y, c), (x, y, 1 - c)

        def rows(px, py, pc):
            return out_ref.at[pl.ds((4 * px + 2 * py + pc) * m_per, m_per), :]

        def copy(k, block, to, src=None):
            return pltpu.make_async_remote_copy(
                src_ref=rows(*block) if src is None else src, dst_ref=rows(*block),
                send_sem=send_sems.at[k], recv_sem=recv_sems.at[k], device_id=to, device_id_type=MESH)

        mine = pltpu.make_async_copy(x_ref, rows(*me), local_sem)
        mine.start()
        first = [copy(0, me, sibling, src=x_ref)]
        first += [copy(1 + j, me, (*chip, c), src=x_ref) for j, chip in enumerate(chips)]
        for cp in first:
            cp.start()
        passed = [copy(4 + j, (*chip, c), sibling) for j, chip in enumerate(chips)]
        for j, chip in enumerate(chips):
            copy(1 + j, (*chip, c), me).wait_recv()
            passed[j].start()
        copy(0, sibling, me).wait_recv()
        for j, chip in enumerate(chips):
            copy(4 + j, (*chip, 1 - c), me).wait_recv()
        for cp in first + passed:
            cp.wait_send()
        mine.wait()

    return pl.pallas_call(
        body, name=name, out_shape=jax.ShapeDtypeStruct((8 * m_per, D_MODEL), tile.dtype),
        in_specs=[pl.BlockSpec(memory_space=pltpu.VMEM)], out_specs=pl.BlockSpec(memory_space=pltpu.VMEM),
        scratch_shapes=[pltpu.SemaphoreType.DMA((7,)), pltpu.SemaphoreType.DMA((7,)), pltpu.SemaphoreType.DMA],
    )(tile)


def _pad_rows(a, rows):
    return jnp.pad(a, ((0, rows - a.shape[0]), (0, 0)))


def _pad_row(v):
    v = v.reshape(1, -1)
    return jnp.pad(v, ((0, 0), (0, D_MODEL - v.shape[1])))


def _pack_shard(fwi, fwo, dwi, dwo, pwu, pwg):
    parts = [_pad_rows(fwi.reshape(1028, D_MODEL), FLAT_A), fwo.reshape(256, D_MODEL), dwi.reshape(2560, D_MODEL),
             dwo.reshape(256, D_MODEL), pwu.reshape(128, D_MODEL), pwg.reshape(512, D_MODEL),
             jnp.zeros((FLAT_TOTAL - FLAT_USED, D_MODEL), fwi.dtype)]
    return jnp.concatenate(parts, axis=0)


def _unpack_shard(flat):
    out, r0 = {}, 0
    for name, rows, slot in FLAT_ROWS:
        out[name] = flat[r0:r0 + rows]
        r0 += slot
    return (out["fox_w_in"].reshape(1, D_MODEL, 1028), out["fox_w_out"].reshape(1, 256, D_MODEL),
            out["dil_w_in"].reshape(1, D_MODEL, 2560), out["dil_w_out"].reshape(1, 256, D_MODEL),
            out["ple_w_up"].reshape(2, PLE_DIM, 256), out["ple_w_gate"].reshape(2, 256, D_MODEL))


def _from_shard_columns(a, n):
    return a.reshape(4, D_MODEL, n).transpose(1, 0, 2).reshape(D_MODEL, 4 * n)


def _to_shard_columns(a, n):
    return a.reshape(D_MODEL, 4, n).transpose(1, 0, 2).reshape(4, n, D_MODEL)


def _unpack_a(wall_a):
    return _from_shard_columns(wall_a[:, :1028], 1028)


def _unpack_b(wall_b):
    out, r0 = {}, 0
    for name, rows, slot in FLAT_ROWS[1:]:
        out[name] = wall_b[:, r0:r0 + rows]
        r0 += slot
    pwu4 = out["ple_w_up"].reshape(4, 2, PLE_DIM, 256)
    pwg4 = out["ple_w_gate"].reshape(4, 2, 256, D_MODEL)
    pwu = [pwu4[:, i].transpose(1, 0, 2).reshape(PLE_DIM, D_MODEL) for i in range(2)]
    pwg = [pwg4[:, i].reshape(D_MODEL, D_MODEL) for i in range(2)]
    return (out["fox_w_out"].reshape(D_MODEL, D_MODEL), _from_shard_columns(out["dil_w_in"], 2560),
            out["dil_w_out"].reshape(D_MODEL, D_MODEL), pwu, pwg)


def _gathered_b_weights(wall_b, w_b, chip):
    return _unpack_b(lax.dynamic_update_slice(wall_b, w_b[None], (chip, 0, 0)))


def _pack_grads_a(gfwi):
    return jnp.pad(_to_shard_columns(gfwi, 1028), ((0, 0), (0, FLAT_A - 1028), (0, 0)))


def _pack_grads_b(gfwo, gdwi, gdwo, gpwu, gpwg):
    up = jnp.stack([a.reshape(PLE_DIM, 4, 256).transpose(1, 0, 2) for a in gpwu], axis=1)
    gate = jnp.stack([a.reshape(4, 256, D_MODEL) for a in gpwg], axis=1)
    parts = [gfwo.reshape(4, 256, D_MODEL), _to_shard_columns(gdwi, 2560), gdwo.reshape(4, 256, D_MODEL),
             up.reshape(4, 128, D_MODEL), gate.reshape(4, 512, D_MODEL)]
    return jnp.concatenate(parts, axis=1).astype(BF16)


def _local_step(x, p0, p1, tgt, fox_g, dil_g, fin_g, b_f, fwi, w_b, chip):
    S = x.shape[0]
    tm = min(256, S)
    nh = FOX_HEADS // 2
    w_qkv0 = jnp.concatenate([fwi[:, :D_MODEL] * 0.125, fwi[:, D_MODEL:3 * D_MODEL]], axis=1)
    w_z0 = fwi[:, 3 * D_MODEL:4 * D_MODEL]
    swapped = lambda a: jnp.repeat(a.reshape(-1, nh, 2)[:, :, ::-1].reshape(-1, FOX_HEADS), FOX_HEAD_DIM, axis=1)
    w_f0 = swapped(fwi[:, 4 * D_MODEL:])
    b_full = swapped(b_f.reshape(1, FOX_HEADS))

    n0, = _rows(_rms_fwd, [x], [(D_MODEL, BF16)], name="norm0", tm=tm, bcast=[fox_g])
    qkv0 = _mm(n0, w_qkv0, out_dtype=BF16, name="proj_qkv0")
    z0 = _mm(n0, w_z0, name="proj_z0")
    fl0 = _mm(n0, w_f0, name="proj_f0")
    qx0, kx0 = _cumsum_rows([fl0], name="forget_cumsum", pre=_log_forget, bcast=[b_full], post=_fox_extras,
                            out_dtypes=(BF16, BF16))
    o0, a0, g0, wall_b = _fox_fwd(qkv0, qx0, kx0, z0, w_b)
    fwo, dwi, dwo, pwu, pwg = _gathered_b_weights(wall_b, w_b, chip)
    w_qkv1 = dwi[:, :3 * DIL_QKV]
    w_z1 = dwi[:, 3 * DIL_QKV:]
    h1 = _mm(g0, fwo, add=x, name="out_proj0")
    u0 = _mm(p0, pwu[0], name="ple_up0")
    t0 = _mm(h1, pwg[0], name="ple_gate0")
    h2, n1 = _rows(_ple_norm, [h1, u0, t0], [(D_MODEL, F32), (D_MODEL, BF16)], name="ple_mix0_norm1", tm=tm,
                   bcast=[dil_g])

    qkv1 = _mm(n1, w_qkv1, name="proj_qkv1")
    z1 = _mm(n1, w_z1, name="proj_z1")
    o1, g1, lse1 = _dil_fwd(qkv1, z1)
    h3 = _mm(g1, dwo, add=h2, name="out_proj1")
    u1 = _mm(p1, pwu[1], name="ple_up1")
    t1 = _mm(h3, pwg[1], name="ple_gate1")

    du1, dt1, dh4, g_fin, loss = _rows(
        _ple_loss, [h3, u1, t1, tgt], [(D_MODEL, BF16), (D_MODEL, BF16), (D_MODEL, F32)], name="ple_mix1_loss_head",
        tm=tm, bcast=[fin_g], acc=[((1, D_MODEL), F32), ((1, LANES), F32)])
    g_up1 = _mm(p1, du1, ta=True, name="grad_ple_up1")
    g_gate1 = _mm(h3, dt1, ta=True, name="grad_ple_gate1")
    dh3 = _mm(dt1, pwg[1], tb=True, add=dh4, name="ple_back1")
    dg1 = _mm(dh3, dwo, tb=True, name="out_back1")
    g_dwo = _mm(g1, dh3, ta=True, name="grad_out1")
    do1, dz1, ld1 = _rows(_gate_bwd, [dg1, o1, z1, lse1], [(D_MODEL, F32), (D_MODEL, BF16), (D_MODEL, F32)],
                          name="gate_bwd1", tm=tm)
    dq1, dk1, dv1 = (lax.empty((S, DIL_QKV), BF16) for _ in range(3))
    for g in range(DIL_GROUPS):
        dq1 = _dil_bwd_q(qkv1, do1, ld1, dq1, g)
        dk1, dv1 = _dil_bwd_kv(qkv1, do1, ld1, dk1, dv1, g)
    g_dwi = _mm_tn_parts(n1, [dq1, dk1, dv1, dz1], name="grad_in1")
    dn1 = _mm_nt_parts([dq1, dk1, dv1, dz1], dwi, name="in_back1")
    dh2, du0, dt0, g_dil = _rows(_norm_ple_bwd, [dn1, h2, dh3, u0, t0], [(D_MODEL, F32), (D_MODEL, BF16), (D_MODEL, BF16)],
                                 name="norm_bwd1_ple_bwd0", tm=tm, bcast=[dil_g], acc=[((1, D_MODEL), F32)])
    g_up0 = _mm(p0, du0, ta=True, name="grad_ple_up0")
    g_gate0 = _mm(h1, dt0, ta=True, name="grad_ple_gate0")
    dh1 = _mm(dt0, pwg[0], tb=True, add=dh2, name="ple_back0")
    dg0 = _mm(dh1, fwo, tb=True, name="out_back0")
    g_fwo = _mm(g0, dh1, ta=True, name="grad_out0")
    do0, dz0, qxa0, dx0 = _rows(_gate_bwd_fox, [dg0, o0, z0, a0], [(D_MODEL, BF16)] * 4, name="gate_bwd0", tm=tm)
    g_b = _pack_grads_b(g_fwo, g_dwi, g_dwo, [g_up0, g_up1], [g_gate0, g_gate1])
    dq0, drx, dk0, dv0, dkx, got_b = _fox_bwd(qkv0, do0, qxa0, dx0, kx0, g_b)
    dlogf, = _cumsum_rows([drx, dkx], name="forget_cumsum_bwd", reverse=True, pre=_forget_dc)
    df0, g_bf = _rows(_forget_bwd, [dlogf, fl0], [(D_MODEL, BF16)], name="forget_bwd", tm=tm, bcast=[b_full],
                      acc=[((1, D_MODEL), F32)])
    g_fwi_main = _mm_tn_parts(n0, [dq0, dk0, dv0, dz0], name="grad_in0")
    g_fwi_f = _mm(n0, df0, ta=True, name="grad_in0_forget")
    dn0 = _mm_nt_parts([dq0, dk0, dv0, dz0], fwi[:, :4 * D_MODEL], name="in_back0")
    dn0 = _mm(df0, w_f0, tb=True, add=dn0, name="in_back0_forget")
    dx, g_fox = _rows(_rms_bwd, [dn0, x, dh1], [(D_MODEL, F32)], name="norm_bwd0", tm=tm, bcast=[fox_g],
                      acc=[((1, D_MODEL), F32)])
    first_lane = lambda a: a.reshape(-1, nh, 2, FOX_HEAD_DIM)[:, :, ::-1, 0].reshape(-1, FOX_HEADS)
    g_fwi = jnp.concatenate([g_fwi_main, first_lane(g_fwi_f)], axis=1)
    return loss, dx, g_fwi, g_b, got_b, (g_fox, _pad_row(first_lane(g_bf)), g_fin, g_dil)


def kernel(x, p, fox_norm, fox_w_in, fox_b_f, fox_w_out, dil_norm, dil_w_in, dil_w_out, ple_w_up, ple_w_gate, final_norm, loss_target, m_fox_norm, m_fox_w_in, m_fox_b_f, m_fox_w_out, m_dil_norm, m_dil_w_in, m_dil_w_out, m_ple_w_up, m_ple_w_gate, m_final_norm, v_fox_norm, v_fox_w_in, v_fox_b_f, v_fox_w_out, v_dil_norm, v_dil_w_in, v_dil_w_out, v_ple_w_up, v_ple_w_gate, v_final_norm):
    xi, yi, ci = lax.axis_index("x"), lax.axis_index("y"), lax.axis_index("c")
    chip = 2 * xi + yi

    w_flat = _pack_shard(*(w.astype(BF16) for w in (fox_w_in, fox_w_out, dil_w_in, dil_w_out, ple_w_up, ple_w_gate)))
    w_a, w_b = w_flat[:FLAT_A], w_flat[FLAT_A:FLAT_USED]
    fwi = _unpack_a(lax.dynamic_update_slice(_gather_weights(w_a), w_a[None], (chip, 0, 0)))
    gains = _gather_tiles(_pad_rows(_pad_row(dil_norm), 8), "gather_gain").reshape(4, 2, 8, D_MODEL)
    dil_g = gains[:, 0, 0, :256].reshape(1, D_MODEL)

    loss_part, dx, g_fwi, g_b, got_b, small = _local_step(
        x[0], p[0, 0], p[1, 0], loss_target[0], fox_norm.reshape(1, D_MODEL), dil_g,
        final_norm.reshape(1, D_MODEL), fox_b_f, fwi, w_b, chip)
    g_fox, g_bf, g_fin, g_dil = small
    sum4 = lambda a, b, c, d: ((a.astype(F32) + b.astype(F32)) + c.astype(F32)) + d.astype(F32)

    got_b = lax.dynamic_update_slice(got_b, lax.dynamic_slice_in_dim(g_b, chip, 1, axis=0), (chip, 0, 0))
    sum_b, = _rows(sum4, [got_b[s] for s in range(4)], [(D_MODEL, F32)], name="chip_sum_b", tm=FLAT_B // 8)
    g_flat_b, = _rows(lambda a, b: a + b, [sum_b, _sibling_half(sum_b, "sibling_sum_b")], [(D_MODEL, F32)],
                      name="pair_sum_b", tm=FLAT_B // 8)

    half = FLAT_A // 2
    g_a = _pack_grads_a(g_fwi)
    theirs = _swap_halves(g_a)
    mine = lax.dynamic_slice_in_dim(g_a, ci * half, half, axis=1)
    part, = _rows(lambda a, b: a + b, [mine.reshape(4 * half, D_MODEL), theirs.reshape(4 * half, D_MODEL)],
                  [(D_MODEL, BF16)], name="pair_sum", tm=half)
    part = part.reshape(4, half, D_MODEL)
    own = lax.dynamic_slice_in_dim(part, chip, 1, axis=0)
    by_chip = lax.dynamic_update_slice(_scatter_partials(part), own, (chip, 0, 0))
    half_sum, = _rows(sum4, [by_chip[s] for s in range(4)], [(D_MODEL, F32)], name="chip_sum", tm=half)
    other_half = _sibling_half(half_sum, "sibling_half")
    g_flat_a = jnp.where(ci == 0, jnp.concatenate([half_sum, other_half], axis=0),
                         jnp.concatenate([other_half, half_sum], axis=0))
    g_flat = jnp.concatenate([g_flat_a, g_flat_b, jnp.zeros((FLAT_TOTAL - FLAT_USED, D_MODEL), F32)], axis=0)

    tile = jnp.concatenate([g_fox, g_bf, g_fin, g_dil, jnp.pad(loss_part, ((0, 0), (0, D_MODEL - LANES))),
                            jnp.zeros((3, D_MODEL), F32)], axis=0)
    tiles = _gather_tiles(tile, "gather_small")
    tot, = _rows(lambda *t: functools.reduce(lambda a, b: a + b, t), [tiles[8 * s:8 * s + 8] for s in range(8)],
                 [(D_MODEL, F32)], name="small_sum", tm=8)
    loss = tot[4, 0]
    g_small = jnp.concatenate([
        tot[0:3],
        _pad_row(lax.dynamic_slice_in_dim(tot[3], chip * 256, 256)), jnp.zeros((4, D_MODEL), F32)], axis=0)
    small_tile = lambda a, b, c, d: jnp.concatenate(
        [_pad_row(a), _pad_row(b), _pad_row(c), _pad_row(d), jnp.zeros((4, D_MODEL), F32)], axis=0)
    w_small = small_tile(fox_norm, fox_b_f, final_norm, dil_norm)
    m_small = small_tile(m_fox_norm, m_fox_b_f, m_final_norm, m_dil_norm)
    v_small = small_tile(v_fox_norm, v_fox_b_f, v_final_norm, v_dil_norm)

    d_small, nm_small, nv_small = _rows(_adamw, [w_small, g_small, m_small, v_small], [(D_MODEL, F32)] * 3,
                                        name="adamw_small", tm=8)
    grads = _unpack_shard(g_flat)
    weights = (fox_w_in, fox_w_out, dil_w_in, dil_w_out, ple_w_up, ple_w_gate)
    firsts = (m_fox_w_in, m_fox_w_out, m_dil_w_in, m_dil_w_out, m_ple_w_up, m_ple_w_gate)
    seconds = (v_fox_w_in, v_fox_w_out, v_dil_w_in, v_dil_w_out, v_ple_w_up, v_ple_w_gate)
    big = []
    for (name, _, _), w, g, m, v in zip(FLAT_ROWS, weights, grads, firsts, seconds):
        width = w.shape[-1]
        flat2 = lambda a: a.reshape(-1, width)
        res = _rows(_adamw, [flat2(w), flat2(g), flat2(m), flat2(v)], [(width, F32)] * 3, name=f"adamw_{name}",
                    tm=128 if width > D_MODEL + LANES else 256)
        big.append([r.reshape(w.shape) for r in res])

    def leaves(shards, small_rows):
        fwi_, fwo_, dwi_, dwo_, pwu_, pwg_ = shards
        return (small_rows[0:1], fwi_, small_rows[1:2, :FOX_HEADS], fwo_, small_rows[3:4, :256], dwi_, dwo_,
                pwu_, pwg_, small_rows[2])

    per_kind = lambda k: [b[k] for b in big]
    return (loss, dx[None], *leaves(grads, g_small), *leaves(per_kind(0), d_small), *leaves(per_kind(1), nm_small),
            *leaves(per_kind(2), nv_small))
```

```python
import functools

import numpy as np
import jax
import jax.numpy as jnp
from jax import lax
from jax.experimental import pallas as pl
from jax.experimental.pallas import tpu as pltpu

F32 = jnp.float32
BF16 = jnp.bfloat16

D_MODEL = 1024
PLE_DIM = 256
FOX_HEADS = 16
FOX_HEAD_DIM = 64
DIL_PATTERN = ((128, 1), (512, 4), (2048, 16))
DIL_GROUPS = 3
DIL_HEADS_PER_GROUP = 8
DIL_HEAD_DIM = 128
DIL_WINDOW_STEPS = 128
DIL_QKV = 3072
ALIBI_MAX_EXP = 8.0
RMS_EPS = 1e-6
ADAM_LR, ADAM_B1, ADAM_B2, ADAM_EPS, ADAM_WD, ADAM_STEP = 0.001, 0.9, 0.999, 1e-08, 0.01, 10

LANES = 128
VMEM_LIMIT = 56 * 1024 * 1024
MESH = pl.DeviceIdType.MESH
ANY = pl.BlockSpec(memory_space=pl.ANY)
NEG = -1e30

FLAT_ROWS = (("fox_w_in", 1028, 1056), ("fox_w_out", 256, 256), ("dil_w_in", 2560, 2560), ("dil_w_out", 256, 256),
             ("ple_w_up", 128, 128), ("ple_w_gate", 512, 512))
FLAT_USED = sum(r for _, _, r in FLAT_ROWS)
FLAT_TOTAL = 4864
FLAT_TILE = 608
FLAT_A = FLAT_ROWS[0][2]
FLAT_B = FLAT_USED - FLAT_A


def _cparams(*sem):
    return pltpu.CompilerParams(dimension_semantics=sem, vmem_limit_bytes=VMEM_LIMIT)


def _sigmoid(x):
    return 1.0 / (1.0 + jnp.exp(-x))


def _mm(a, b, *, name, ta=False, tb=False, out_dtype=F32, add=None):
    if ta:
        K, M = a.shape
    else:
        M, K = a.shape
    if tb:
        N, Kb = b.shape
    else:
        Kb, N = b.shape
    assert K == Kb, (a.shape, b.shape)
    tm, tn, tk = min(M, 1024), min(N, 1024), min(K, 1024)
    assert M % tm == 0 and N % tn == 0 and K % tk == 0, (M, N, K)
    nk = K // tk
    dn = (((0 if ta else 1,), (1 if tb else 0,)), ((), ()))

    def body(*refs):
        if add is None:
            a_ref, b_ref, o_ref, acc = refs
        else:
            a_ref, b_ref, add_ref, o_ref, acc = refs
        k = pl.program_id(2)

        @pl.when(k == 0)
        def _():
            acc[...] = jnp.zeros_like(acc)

        acc[...] += lax.dot_general(a_ref[...].astype(BF16), b_ref[...].astype(BF16), dn,
                                    preferred_element_type=F32)

        @pl.when(k == nk - 1)
        def _():
            r = acc[...]
            if add is not None:
                r = r + add_ref[...]
            o_ref[...] = r.astype(out_dtype)

    a_spec = (pl.BlockSpec((tk, tm), lambda i, j, k: (k, i)) if ta
              else pl.BlockSpec((tm, tk), lambda i, j, k: (i, k)))
    b_spec = (pl.BlockSpec((tn, tk), lambda i, j, k: (j, k)) if tb
              else pl.BlockSpec((tk, tn), lambda i, j, k: (k, j)))
    in_specs = [a_spec, b_spec]
    args = [a, b]
    if add is not None:
        in_specs.append(pl.BlockSpec((tm, tn), lambda i, j, k: (i, j)))
        args.append(add)
    return pl.pallas_call(
        body, name=name, grid=(M // tm, N // tn, nk),
        in_specs=in_specs, out_specs=pl.BlockSpec((tm, tn), lambda i, j, k: (i, j)),
        out_shape=jax.ShapeDtypeStruct((M, N), out_dtype),
        scratch_shapes=[pltpu.VMEM((tm, tn), F32)],
        compiler_params=_cparams("parallel", "parallel", "arbitrary"),
    )(*args)


def _mm_nt_parts(a_parts, b, *, name):
    M = a_parts[0].shape[0]
    N, K = b.shape
    tm, tn, tk = min(M, 1024), min(N, 1024), 1024
    nks = [a.shape[1] // tk for a in a_parts]
    offs = [sum(nks[:p]) for p in range(len(nks))]
    nk = sum(nks)
    assert nk * tk == K and M % tm == 0 and N % tn == 0, (M, N, K)
    n_parts = len(a_parts)

    def body(*refs):
        a_refs, b_ref, o_ref, acc = refs[:n_parts], refs[n_parts], refs[n_parts + 1], refs[n_parts + 2]
        k = pl.program_id(2)

        @pl.when(k == 0)
        def _():
            acc[...] = jnp.zeros_like(acc)

        for a_ref, off, n in zip(a_refs, offs, nks):
            @pl.when((k >= off) & (k < off + n))
            def _(a_ref=a_ref):
                acc[...] += lax.dot_general(a_ref[...].astype(BF16), b_ref[...].astype(BF16),
                                            (((1,), (1,)), ((), ())), preferred_element_type=F32)

        @pl.when(k == nk - 1)
        def _():
            o_ref[...] = acc[...]

    a_specs = [pl.BlockSpec((tm, tk), lambda i, j, k, off=off, n=n: (i, jnp.clip(k - off, 0, n - 1)))
               for off, n in zip(offs, nks)]
    return pl.pallas_call(
        body, name=name, grid=(M // tm, N // tn, nk),
        in_specs=a_specs + [pl.BlockSpec((tn, tk), lambda i, j, k: (j, k))],
        out_specs=pl.BlockSpec((tm, tn), lambda i, j, k: (i, j)),
        out_shape=jax.ShapeDtypeStruct((M, N), F32), scratch_shapes=[pltpu.VMEM((tm, tn), F32)],
        compiler_params=_cparams("parallel", "parallel", "arbitrary"),
    )(*a_parts, b)


def _mm_tn_parts(a, b_parts, *, name):
    K, M = a.shape
    tm, tn, tk = min(M, 1024), 1024, min(K, 1024)
    njs = [b.shape[1] // tn for b in b_parts]
    offs = [sum(njs[:p]) for p in range(len(njs))]
    nj, nk = sum(njs), K // tk
    assert M % tm == 0 and K % tk == 0 and all(b.shape[1] % tn == 0 for b in b_parts)
    n_parts = len(b_parts)

    def body(*refs):
        a_ref, b_refs, o_ref, acc = refs[0], refs[1:1 + n_parts], refs[1 + n_parts], refs[2 + n_parts]
        j, k = pl.program_id(1), pl.program_id(2)

        @pl.when(k == 0)
        def _():
            acc[...] = jnp.zeros_like(acc)

        for b_ref, off, n in zip(b_refs, offs, njs):
            @pl.when((j >= off) & (j < off + n))
            def _(b_ref=b_ref):
                acc[...] += lax.dot_general(a_ref[...].astype(BF16), b_ref[...].astype(BF16),
                                            (((0,), (0,)), ((), ())), preferred_element_type=F32)

        @pl.when(k == nk - 1)
        def _():
            o_ref[...] = acc[...]

    def b_spec(off, n):
        def index(i, j, k):
            mine = (j >= off) & (j < off + n)
            return jnp.where(mine, k, 0), jnp.clip(j - off, 0, n - 1)
        return pl.BlockSpec((tk, tn), index)

    return pl.pallas_call(
        body, name=name, grid=(M // tm, nj, nk),
        in_specs=[pl.BlockSpec((tk, tm), lambda i, j, k: (k, i))] + [b_spec(off, n) for off, n in zip(offs, njs)],
        out_specs=pl.BlockSpec((tm, tn), lambda i, j, k: (i, j)),
        out_shape=jax.ShapeDtypeStruct((M, nj * tn), F32), scratch_shapes=[pltpu.VMEM((tm, tn), F32)],
        compiler_params=_cparams("parallel", "parallel", "arbitrary"),
    )(a, *b_parts)


def _rows(fn, ins, outs, *, name, tm, bcast=(), acc=()):
    R = ins[0].shape[0]
    assert R % tm == 0, (R, tm)
    n_in, n_b, n_out, n_acc = len(ins), len(bcast), len(outs), len(acc)

    def body(*refs):
        in_refs = refs[:n_in + n_b]
        out_refs = refs[n_in + n_b:n_in + n_b + n_out]
        acc_refs = refs[n_in + n_b + n_out:]
        res = fn(*[r[...] for r in in_refs])
        if not isinstance(res, (tuple, list)):
            res = (res,)
        for r, v in zip(out_refs, res[:n_out]):
            r[...] = v.astype(r.dtype)
        first = pl.program_id(0) == 0
        for r, v in zip(acc_refs, res[n_out:]):
            @pl.when(first)
            def _(r=r, v=v):
                r[...] = v.astype(r.dtype)

            @pl.when(jnp.logical_not(first))
            def _(r=r, v=v):
                r[...] += v.astype(r.dtype)

    in_specs = [pl.BlockSpec((tm, a.shape[1]), lambda i: (i, 0)) for a in ins]
    in_specs += [pl.BlockSpec(b.shape, lambda i, nd=b.ndim: (0,) * nd) for b in bcast]
    out_specs = [pl.BlockSpec((tm, c), lambda i: (i, 0)) for c, _ in outs]
    out_specs += [pl.BlockSpec(s, lambda i, nd=len(s): (0,) * nd) for s, _ in acc]
    out_shape = [jax.ShapeDtypeStruct((R, c), dt) for c, dt in outs]
    out_shape += [jax.ShapeDtypeStruct(s, dt) for s, dt in acc]
    res = pl.pallas_call(
        body, name=name, grid=(R // tm,), in_specs=in_specs, out_specs=out_specs, out_shape=out_shape,
        compiler_params=_cparams("arbitrary" if acc else "parallel"),
    )(*ins, *bcast)
    return res


def _exact_dot(sel, x, left):
    hi = x.astype(BF16)
    r1 = x - hi.astype(F32)
    mid = r1.astype(BF16)
    lo = (r1 - mid.astype(F32)).astype(BF16)
    dot = (lambda p: jnp.dot(sel, p, preferred_element_type=F32)) if left else (
        lambda p: jnp.dot(p, sel, preferred_element_type=F32))
    return dot(hi) + dot(mid) + dot(lo)


def _cumsum_rows(ins, *, name, width, reverse=False, pre=None, bcast=(), post=None, outs=None):
    S = ins[0].shape[0]
    outs = outs or [(width, F32)]
    out_dtypes = [dt for _, dt in outs]
    tb = 256
    nb = S // tb
    assert S % tb == 0
    n_in = len(ins) + len(bcast)
    n_out = len(out_dtypes)

    def body(*refs):
        in_refs, o_refs, carry = refs[:n_in], refs[n_in:n_in + n_out], refs[n_in + n_out]

        @pl.when(pl.program_id(0) == 0)
        def _():
            carry[...] = jnp.zeros_like(carry)

        xv = in_refs[0][...] if pre is None else pre(*[r[...] for r in in_refs])
        r_ = lax.broadcasted_iota(jnp.int32, (tb, tb), 0)
        c_ = lax.broadcasted_iota(jnp.int32, (tb, tb), 1)
        tri = jnp.where((c_ >= r_) if reverse else (c_ <= r_), 1.0, 0.0).astype(BF16)
        cs = _exact_dot(tri, xv, left=True) + carry[...]
        outs = (cs,) if post is None else post(cs)
        for o_ref, v in zip(o_refs, outs):
            o_ref[...] = v.astype(o_ref.dtype)
        carry[...] = cs[0:1, :] if reverse else cs[tb - 1:tb, :]

    blk = (lambda i: (nb - 1 - i, 0)) if reverse else (lambda i: (i, 0))
    in_specs = [pl.BlockSpec((tb, a.shape[1]), blk) for a in ins]
    in_specs += [pl.BlockSpec(b.shape, lambda i, nd=b.ndim: (0,) * nd) for b in bcast]
    return pl.pallas_call(
        body, name=name, grid=(nb,), in_specs=in_specs, out_specs=[pl.BlockSpec((tb, c), blk) for c, _ in outs],
        out_shape=[jax.ShapeDtypeStruct((S, c), dt) for c, dt in outs], scratch_shapes=[pltpu.VMEM((1, width), F32)],
        compiler_params=_cparams("arbitrary"),
    )(*ins, *bcast)


def _rms_fwd(x, g):
    r = lax.rsqrt(jnp.mean(x * x, axis=1, keepdims=True) + RMS_EPS)
    return (x * r) * g


def _rms_bwd(dn, x, dres, g):
    r = lax.rsqrt(jnp.mean(x * x, axis=1, keepdims=True) + RMS_EPS)
    xh = x * r
    w = dn * g
    dx = r * (w - xh * jnp.mean(w * xh, axis=1, keepdims=True))
    return dres + dx, jnp.sum(dn * xh, axis=0, keepdims=True)


def _final_stage(h, tgt, g):
    r = lax.rsqrt(jnp.mean(h * h, axis=1, keepdims=True) + RMS_EPS)
    xh = h * r
    diff = xh * g - tgt
    loss = 0.5 * jnp.sum(jnp.mean(diff * diff, axis=1, keepdims=True), axis=0, keepdims=True)
    dy = diff * (1.0 / D_MODEL)
    w = dy * g
    dh = r * (w - xh * jnp.mean(w * xh, axis=1, keepdims=True))
    return dh, jnp.sum(dy * xh, axis=0, keepdims=True), jnp.broadcast_to(loss, (1, LANES))


def _ple_fwd(h, u, t):
    return h + u * _sigmoid(t)


def _ple_bwd(dh, u, t):
    s = _sigmoid(t)
    return dh * s, dh * u * s * (1.0 - s)


def _ple_norm(h, u, t, g):
    h2 = _ple_fwd(h, u, t)
    return h2, _rms_fwd(h2, g)


def _ple_loss(h, u, t, tgt, g):
    dh, g_gain, loss = _final_stage(_ple_fwd(h, u, t), tgt, g)
    du, dt = _ple_bwd(dh, u, t)
    return du, dt, dh, g_gain, loss


def _norm_ple_bwd(dn, x, dres, u, t, g):
    dh, g_gain = _rms_bwd(dn, x, dres, g)
    du, dt = _ple_bwd(dh, u, t)
    return dh, du, dt, g_gain


def _gate_fwd(o, z):
    return o * (z * _sigmoid(z))


def _head_sums(prod):
    tm, width = prod.shape
    cols = [jnp.broadcast_to(jnp.sum(prod[:, b * LANES:(b + 1) * LANES], axis=1, keepdims=True), (tm, LANES))
            for b in range(width // LANES)]
    return jnp.concatenate(cols, axis=1)


def _gate_bwd(dg, o, z, lse):
    s = _sigmoid(z)
    do = dg * (z * s)
    dz = dg * o * (s * (1.0 + z * (1.0 - s)))
    lane = lax.broadcasted_iota(jnp.int32, do.shape, 1) % LANES
    return do, dz, jnp.where(lane < LANES // 2, lse, _head_sums(do * o))


def _log_forget(fl, b):
    u = fl + b
    return jnp.minimum(u, 0.0) - jnp.log(1.0 + jnp.exp(-jnp.abs(u)))


def _adamw(w, g, m, v):
    m = ADAM_B1 * m + (1.0 - ADAM_B1) * g
    v = ADAM_B2 * v + (1.0 - ADAM_B2) * (g * g)
    m_hat = m / (1.0 - ADAM_B1 ** ADAM_STEP)
    v_hat = v / (1.0 - ADAM_B2 ** ADAM_STEP)
    delta = -ADAM_LR * (m_hat / (jnp.sqrt(v_hat) + ADAM_EPS) + ADAM_WD * w)
    return delta, m, v


FOX_TQ = 1024


def _split3(x):
    p1 = x.astype(BF16).astype(F32)
    r = x - p1
    p2 = r.astype(BF16).astype(F32)
    return p1, p2, r - p2


def _lane_in_head(shape):
    return lax.broadcasted_iota(jnp.int32, shape, 1) % FOX_HEAD_DIM


def _query_extras(x):
    lm = _lane_in_head(x.shape)
    p1, p2, p3 = _split3(x)
    return jnp.where(lm == 0, p1, jnp.where(lm == 1, p2, jnp.where(lm == 2, p3, jnp.where(lm < 6, 1.0, 0.0))))


def _key_extras(c):
    lm = _lane_in_head(c.shape)
    p1, p2, p3 = _split3(c)
    return jnp.where(lm < 3, 1.0, jnp.where(lm == 3, -p1, jnp.where(lm == 4, -p2, jnp.where(lm == 5, -p3, 0.0))))


def _head_lane_base(head):
    return LANES * (head // 2) + FOX_HEAD_DIM * (1 - head % 2)


def _fox_extras(c):
    head = lax.broadcasted_iota(jnp.int32, (LANES, D_MODEL), 0)
    lane = lax.broadcasted_iota(jnp.int32, (LANES, D_MODEL), 1)
    owner = 2 * (lane // LANES) + 1 - (lane % LANES) // FOX_HEAD_DIM
    wide = _exact_dot(jnp.where(head == owner, 1.0, 0.0).astype(BF16), c, left=False)
    return _query_extras(wide), _key_extras(wide)


def _swapped_head_sums(prod):
    tm, width = prod.shape
    lane = lax.broadcasted_iota(jnp.int32, (tm, LANES), 1)
    low = lane < FOX_HEAD_DIM
    cols = []
    for b in range(width // LANES):
        blk = prod[:, b * LANES:(b + 1) * LANES]
        sa = jnp.sum(jnp.where(low, blk, 0.0), axis=1, keepdims=True)
        sb = jnp.sum(jnp.where(low, 0.0, blk), axis=1, keepdims=True)
        cols.append(jnp.where(low, sb, sa))
    return jnp.concatenate(cols, axis=1)


def _gate_bwd_fox(dg, o, z, a):
    s = _sigmoid(z)
    do = dg * (z * s)
    dz = dg * o * (s * (1.0 + z * (1.0 - s)))
    lm = _lane_in_head(do.shape)
    d1, d2, d3 = _split3(-_swapped_head_sums(do * o))
    dx = jnp.where(lm == 0, d1, jnp.where(lm == 1, d2, jnp.where(lm == 2, d3, 0.0)))
    return do, dz, _query_extras(a), dx


def _forget_dc(drx, dkx):
    lane = lax.broadcasted_iota(jnp.int32, (D_MODEL, LANES), 0)
    head = lax.broadcasted_iota(jnp.int32, (D_MODEL, LANES), 1)
    base = _head_lane_base(head)
    pick = lambda l: jnp.where((lane == base + l) & (head < FOX_HEADS), 1.0, 0.0).astype(BF16)
    return _exact_dot(pick(0), drx, left=False) - _exact_dot(pick(3), dkx, left=False)


def _forget_bwd(dl, fl, b):
    du = dl * (1.0 / (1.0 + jnp.exp(fl + b)))
    return du, jnp.sum(du, axis=0, keepdims=True)


def _chip_exchange(src_of, out_ref, send, recv):
    x, y, c = lax.axis_index("x"), lax.axis_index("y"), lax.axis_index("c")
    chips = [(1 - x, y), (x, 1 - y), (1 - x, 1 - y)]
    me = 2 * x + y
    sends = [pltpu.make_async_remote_copy(src_ref=src_of(cx, cy), dst_ref=out_ref.at[me], send_sem=send.at[k],
                                          recv_sem=recv.at[k], device_id=(cx, cy, c), device_id_type=MESH)
             for k, (cx, cy) in enumerate(chips)]

    def start():
        for cp in sends:
            cp.start()

    def wait():
        for k, (cx, cy) in enumerate(chips):
            slot = out_ref.at[2 * cx + cy]
            pltpu.make_async_remote_copy(src_ref=slot, dst_ref=slot, send_sem=send.at[k], recv_sem=recv.at[k],
                                         device_id=(cx, cy, c), device_id_type=MESH).wait_recv()
        for cp in sends:
            cp.wait_send()

    return start, wait


def _fox_fwd(qkv, qx, kx, z, w_b):
    S = qkv.shape[0]
    tq = min(FOX_TQ, S)
    nq = S // tq
    nt = (((1,), (1,)), ((), ()))

    def body(q_ref, qx_ref, k_ref, v_ref, kx_ref, z_ref, wb_ref, o_ref, a_ref, g_ref, wall_ref, send, recv):
        i = pl.program_id(1)
        start, wait = _chip_exchange(lambda cx, cy: wb_ref, wall_ref, send, recv)
        pl.when((pl.program_id(0) == 0) & (i == 0))(start)
        low = lax.broadcasted_iota(jnp.int32, (tq, LANES), 1) < FOX_HEAD_DIM
        row = lax.broadcasted_iota(jnp.int32, (tq, tq), 0)
        col = lax.broadcasted_iota(jnp.int32, (tq, tq), 1)
        q2, x2 = q_ref[...], qx_ref[...]
        qa = (jnp.where(low, q2, x2), jnp.where(low, x2, q2))

        def step(kb, carry, diag):
            start = pl.multiple_of(kb * tq, tq)
            k2 = k_ref[pl.ds(start, tq), :]
            v2 = v_ref[pl.ds(start, tq), :]
            y2 = kx_ref[pl.ds(start, tq), :]
            one = jnp.ones_like(v2)
            ka = (jnp.where(low, k2, y2), jnp.where(low, y2, k2))
            va = (jnp.where(low, v2, one), jnp.where(low, one, v2))
            out = []
            for hh in range(2):
                m, acc = carry[hh]
                s = lax.dot_general(qa[hh], ka[hh], nt, preferred_element_type=F32)
                if diag:
                    s = jnp.where(col <= row, s, NEG)
                m_new = jnp.maximum(m, jnp.max(s, axis=1, keepdims=True))
                pr = jnp.exp(s - m_new)
                acc = jnp.exp(m - m_new) * acc + jnp.dot(pr.astype(BF16), va[hh], preferred_element_type=F32)
                out.append((m_new, acc))
            return tuple(out)

        init = ((jnp.full((tq, 1), NEG, F32), jnp.zeros((tq, LANES), F32)),) * 2
        carry = lax.fori_loop(0, i, functools.partial(step, diag=False), init)
        (m_a, acc_a), (m_b, acc_b) = step(i, carry, True)
        l_a, l_b = acc_a[:, FOX_HEAD_DIM:FOX_HEAD_DIM + 1], acc_b[:, 0:1]
        xf = x2.astype(F32)
        c_a = xf[:, 64:65] + xf[:, 65:66] + xf[:, 66:67]
        c_b = xf[:, 0:1] + xf[:, 1:2] + xf[:, 2:3]
        o = jnp.where(low, acc_a / l_a, acc_b / l_b)
        o_ref[...] = o
        g_ref[...] = _gate_fwd(o, z_ref[...]).astype(BF16)
        a_ref[...] = jnp.where(low, jnp.broadcast_to(c_b - (m_b + jnp.log(l_b)), (tq, LANES)),
                               jnp.broadcast_to(c_a - (m_a + jnp.log(l_a)), (tq, LANES)))
        pl.when((pl.program_id(0) == FOX_HEADS // 2 - 1) & (i == nq - 1))(wait)

    blk = lambda cb: pl.BlockSpec((tq, LANES), lambda h, i, cb=cb: (i, cb + h))
    res = lambda cb: pl.BlockSpec((S, LANES), lambda h, i, cb=cb: (0, cb + h))
    return pl.pallas_call(
        body, name="fox_attn_fwd", grid=(FOX_HEADS // 2, nq),
        in_specs=[blk(0), blk(0), res(8), res(16), res(0), blk(0), ANY],
        out_specs=[blk(0), blk(0), blk(0), ANY],
        out_shape=[jax.ShapeDtypeStruct((S, D_MODEL), F32), jax.ShapeDtypeStruct((S, D_MODEL), F32),
                   jax.ShapeDtypeStruct((S, D_MODEL), BF16), jax.ShapeDtypeStruct((4,) + w_b.shape, w_b.dtype)],
        scratch_shapes=[pltpu.SemaphoreType.DMA((3,)), pltpu.SemaphoreType.DMA((3,))],
        compiler_params=_cparams("arbitrary", "arbitrary"),
    )(qkv, qx, qkv, qkv, kx, z, w_b)


def _fox_bwd(qkv, do, qxa, dx, kx, g_b):
    S = qkv.shape[0]
    tq = min(FOX_TQ, S)
    nq = S // tq
    nt = (((1,), (1,)), ((), ()))
    tn = (((0,), (0,)), ((), ()))

    def body(q_ref, qx_ref, do_ref, dx_ref, k_ref, v_ref, kx_ref, gb_ref, dq_ref, dr_ref, dk_ref, dv_ref, dkx_ref,
             got_ref, send, recv):
        kb = pl.program_id(1)
        start, wait = _chip_exchange(lambda cx, cy: gb_ref.at[2 * cx + cy], got_ref, send, recv)
        pl.when((pl.program_id(0) == 0) & (kb == 0))(start)
        low = lax.broadcasted_iota(jnp.int32, (tq, LANES), 1) < FOX_HEAD_DIM
        row = lax.broadcasted_iota(jnp.int32, (tq, tq), 0)
        col = lax.broadcasted_iota(jnp.int32, (tq, tq), 1)

        @pl.when(kb == 0)
        def _():
            dq_ref[...] = jnp.zeros_like(dq_ref)
            dr_ref[...] = jnp.zeros_like(dr_ref)

        k2, v2, y2 = k_ref[...], v_ref[...], kx_ref[...]
        one = jnp.ones_like(v2)
        ka = (jnp.where(low, k2, y2), jnp.where(low, y2, k2))
        va = (jnp.where(low, v2, one), jnp.where(low, one, v2))

        def step(qb, carry, diag):
            start = pl.multiple_of(qb * tq, tq)
            q2 = q_ref[pl.ds(start, tq), :]
            x2 = qx_ref[pl.ds(start, tq), :]
            d2 = do_ref[pl.ds(start, tq), :]
            e2 = dx_ref[pl.ds(start, tq), :]
            qa = (jnp.where(low, q2, x2), jnp.where(low, x2, q2))
            da = (jnp.where(low, d2, e2), jnp.where(low, e2, d2))
            new, res = [], []
            for hh in range(2):
                dk, dv = carry[hh]
                s = lax.dot_general(qa[hh], ka[hh], nt, preferred_element_type=F32)
                if diag:
                    s = jnp.where(col <= row, s, NEG)
                pr = jnp.exp(s)
                ds = pr * lax.dot_general(da[hh], va[hh], nt, preferred_element_type=F32)
                prb, dsb = pr.astype(BF16), ds.astype(BF16)
                dv = dv + lax.dot_general(prb, da[hh], tn, preferred_element_type=F32)
                dk = dk + lax.dot_general(dsb, qa[hh], tn, preferred_element_type=F32)
                res.append(jnp.dot(dsb, ka[hh], preferred_element_type=F32))
                new.append((dk, dv))
            dq_ref[pl.ds(start, tq), :] += jnp.where(low, res[0], res[1])
            dr_ref[pl.ds(start, tq), :] += jnp.where(low, res[1], res[0])
            return tuple(new)

        init = ((jnp.zeros((tq, LANES), F32), jnp.zeros((tq, LANES), F32)),) * 2
        carry = step(kb, init, True)
        (dk_a, dv_a), (dk_b, dv_b) = lax.fori_loop(kb + 1, nq, functools.partial(step, diag=False), carry)
        dk_ref[...] = jnp.where(low, dk_a, dk_b).astype(BF16)
        dv_ref[...] = jnp.where(low, dv_a, dv_b).astype(BF16)
        dkx_ref[...] = jnp.where(low, dk_b, dk_a)

        @pl.when(kb == nq - 1)
        def _():
            dq_ref[...] = dq_ref[...] * (FOX_HEAD_DIM ** -0.5)

        pl.when((pl.program_id(0) == FOX_HEADS // 2 - 1) & (kb == nq - 1))(wait)

    res = lambda cb: pl.BlockSpec((S, LANES), lambda h, k, cb=cb: (0, cb + h))
    blk = lambda cb: pl.BlockSpec((tq, LANES), lambda h, k, cb=cb: (k, cb + h))
    f32, b16 = jax.ShapeDtypeStruct((S, D_MODEL), F32), jax.ShapeDtypeStruct((S, D_MODEL), BF16)
    return pl.pallas_call(
        body, name="fox_attn_bwd", grid=(FOX_HEADS // 2, nq),
        in_specs=[res(0), res(0), res(0), res(0), blk(8), blk(16), blk(0), ANY],
        out_specs=[res(0), res(0), blk(0), blk(0), blk(0), ANY],
        out_shape=[f32, f32, b16, b16, f32, jax.ShapeDtypeStruct(g_b.shape, g_b.dtype)],
        scratch_shapes=[pltpu.SemaphoreType.DMA((3,)), pltpu.SemaphoreType.DMA((3,))],
        compiler_params=_cparams("arbitrary", "arbitrary"),
    )(qkv, qxa, do, dx, qkv, qkv, kx, g_b)


def _alibi_slopes():
    n = DIL_GROUPS * DIL_HEADS_PER_GROUP
    s = np.float32(2.0) ** (np.float32(-ALIBI_MAX_EXP) * np.arange(1, n + 1, dtype=np.float32) / np.float32(n))
    return s.astype(np.float32).reshape(DIL_GROUPS, DIL_HEADS_PER_GROUP)


W = DIL_WINDOW_STEPS
DIL_SCALE = DIL_HEAD_DIM ** -0.5


DIL_ROWS = 2048
NT = (((1,), (1,)), ((), ()))
TN = (((0,), (0,)), ((), ()))
QKV_BLOCKS = DIL_QKV // LANES


def _slope_table():
    t = np.zeros((DIL_HEADS_PER_GROUP, 8, LANES), np.float32)
    t[:, :DIL_GROUPS, :] = _alibi_slopes().T[:, :, None]
    return jnp.asarray(t)


def _phase_rows(start, d):
    return pl.ds(start, W, stride=d) if d > 1 else pl.ds(start, W)


def _for_phases(d, unit):
    if d == 1:
        unit(0)
    else:
        lax.fori_loop(0, d, lambda r, c: (unit(r), c)[1], 0)


def _window_geometry(shape, q_axis_offset):
    i_ = lax.broadcasted_iota(jnp.int32, shape, 0)
    j_ = lax.broadcasted_iota(jnp.int32, shape, 1)
    dist = q_axis_offset + i_ - j_
    return i_, j_, dist, (dist >= 0) & (dist <= W)


def _dil_fwd(qkv, z):
    S = qkv.shape[0]
    rb = min(DIL_ROWS, S)
    nb = S // rb
    spans = [W * d for _, d in DIL_PATTERN]

    def body(*refs):
        slope_ref = refs[0]
        q, kc, kp, vc, vp = refs[1:4], refs[4:7], refs[7:10], refs[10:13], refs[13:16]
        z_ref, o_ref, g_ref, l_ref = refs[16:20]
        og, lg = refs[20:23], refs[23:26]
        i = pl.program_id(1)
        _, j_, dist, inwin = _window_geometry((W, 2 * W), W)
        distf = dist.astype(F32)
        for g, (_, d) in enumerate(DIL_PATTERN):
            span = spans[g]
            bias = -(slope_ref[0, g:g + 1, 0:1] * float(d)) * distf
            for b in range(rb // span):
                def unit(r, g=g, d=d, span=span, b=b, bias=bias):
                    base = b * span + r
                    rows = _phase_rows(base, d)
                    if b > 0:
                        kprev, vprev = kc[g][_phase_rows(base - span, d), :], vc[g][_phase_rows(base - span, d), :]
                        valid = inwin
                    else:
                        kprev, vprev = kp[g][_phase_rows(r, d), :], vp[g][_phase_rows(r, d), :]
                        valid = inwin & ((j_ >= W) | (i > 0))
                    k2 = jnp.concatenate([kprev, kc[g][rows, :]], axis=0).astype(BF16)
                    v2 = jnp.concatenate([vprev, vc[g][rows, :]], axis=0).astype(BF16)
                    s = lax.dot_general(q[g][rows, :].astype(BF16), k2, NT, preferred_element_type=F32)
                    s = jnp.where(valid, s * DIL_SCALE + bias, NEG)
                    m = jnp.max(s, axis=1, keepdims=True)
                    pr = jnp.exp(s - m)
                    l = jnp.sum(pr, axis=1, keepdims=True)
                    og[g][rows, :] = jnp.dot(pr.astype(BF16), v2, preferred_element_type=F32) / l
                    lg[g][rows, :] = jnp.broadcast_to(m + jnp.log(l), (W, LANES))

                _for_phases(d, unit)

        def mix(cix, c):
            sl = pl.ds(pl.multiple_of(cix * 256, 256), 256)
            l1, l2, l3 = lg[0][sl, :], lg[1][sl, :], lg[2][sl, :]
            m = jnp.maximum(jnp.maximum(l1, l2), l3)
            e1, e2, e3 = jnp.exp(l1 - m), jnp.exp(l2 - m), jnp.exp(l3 - m)
            tot = e1 + e2 + e3
            o = (e1 * og[0][sl, :] + e2 * og[1][sl, :] + e3 * og[2][sl, :]) / tot
            o_ref[sl, :] = o
            g_ref[sl, :] = _gate_fwd(o, z_ref[sl, :]).astype(BF16)
            l_ref[sl, :] = m + jnp.log(tot)
            return c

        lax.fori_loop(0, rb // 256, mix, 0)

    cur = lambda off, g: pl.BlockSpec((rb, LANES), lambda h, i: (i, off * QKV_BLOCKS + g * DIL_HEADS_PER_GROUP + h))
    prev = lambda off, g: pl.BlockSpec(
        (spans[g], LANES),
        lambda h, i: (jnp.maximum(i * (rb // spans[g]) - 1, 0), off * QKV_BLOCKS + g * DIL_HEADS_PER_GROUP + h))
    row = pl.BlockSpec((rb, LANES), lambda h, i: (i, h))
    groups = range(DIL_GROUPS)
    in_specs = [pl.BlockSpec((1, 8, LANES), lambda h, i: (h, 0, 0))]
    in_specs += [cur(0, g) for g in groups] + [cur(1, g) for g in groups] + [prev(1, g) for g in groups]
    in_specs += [cur(2, g) for g in groups] + [prev(2, g) for g in groups] + [row]
    f32 = jax.ShapeDtypeStruct((S, D_MODEL), F32)
    return pl.pallas_call(
        body, name="dil_attn_fwd", grid=(DIL_HEADS_PER_GROUP, nb), in_specs=in_specs,
        out_specs=[row, row, row], out_shape=[f32, jax.ShapeDtypeStruct((S, D_MODEL), BF16), f32],
        scratch_shapes=[pltpu.VMEM((rb, LANES), F32)] * 6,
        compiler_params=_cparams("parallel", "parallel"),
    )(_slope_table(), *([qkv] * 15), z)


def _dil_bwd_q(qkv, do, ld, buf, g):
    S = qkv.shape[0]
    rb = min(DIL_ROWS, S)
    nb = S // rb
    d = DIL_PATTERN[g][1]
    span = W * d

    def body(slope_ref, q_ref, kc, kp, vc, vp, do_ref, ld_ref, buf_ref, dq_ref, dq_f32):
        i = pl.program_id(1)
        _, j_, dist, inwin = _window_geometry((W, 2 * W), W)
        bias = -(slope_ref[0, g:g + 1, 0:1] * float(d)) * dist.astype(F32)
        for b in range(rb // span):
            def unit(r, b=b):
                base = b * span + r
                rows = _phase_rows(base, d)
                if b > 0:
                    kprev, vprev = kc[_phase_rows(base - span, d), :], vc[_phase_rows(base - span, d), :]
                    valid = inwin
                else:
                    kprev, vprev = kp[_phase_rows(r, d), :], vp[_phase_rows(r, d), :]
                    valid = inwin & ((j_ >= W) | (i > 0))
                k2 = jnp.concatenate([kprev, kc[rows, :]], axis=0).astype(BF16)
                v2 = jnp.concatenate([vprev, vc[rows, :]], axis=0).astype(BF16)
                ld = ld_ref[rows, :]
                s = lax.dot_general(q_ref[rows, :].astype(BF16), k2, NT, preferred_element_type=F32)
                s = jnp.where(valid, s * DIL_SCALE + bias - ld[:, 0:1], NEG)
                pr = jnp.exp(s)
                dp = lax.dot_general(do_ref[rows, :].astype(BF16), v2, NT, preferred_element_type=F32)
                ds = pr * (dp - ld[:, LANES // 2:LANES // 2 + 1])
                dq_f32[rows, :] = jnp.dot(ds.astype(BF16), k2, preferred_element_type=F32) * DIL_SCALE

            _for_phases(d, unit)
        dq_ref[...] = dq_f32[...].astype(BF16)

    col = lambda off: off * QKV_BLOCKS + g * DIL_HEADS_PER_GROUP
    cur = lambda off: pl.BlockSpec((rb, LANES), lambda h, i: (i, col(off) + h))
    prev = lambda off: pl.BlockSpec((span, LANES), lambda h, i: (jnp.maximum(i * (rb // span) - 1, 0), col(off) + h))
    row = pl.BlockSpec((rb, LANES), lambda h, i: (i, h))
    return pl.pallas_call(
        body, name=f"dil_attn_bwd_q_g{g}", grid=(DIL_HEADS_PER_GROUP, nb),
        in_specs=[pl.BlockSpec((1, 8, LANES), lambda h, i: (h, 0, 0)), cur(0), cur(1), prev(1), cur(2), prev(2),
                  row, row, pl.BlockSpec(memory_space=pl.ANY)],
        out_specs=pl.BlockSpec((rb, LANES), lambda h, i: (i, g * DIL_HEADS_PER_GROUP + h)),
        out_shape=jax.ShapeDtypeStruct(buf.shape, buf.dtype), input_output_aliases={8: 0},
        scratch_shapes=[pltpu.VMEM((rb, LANES), F32)],
        compiler_params=_cparams("parallel", "parallel"),
    )(_slope_table(), qkv, qkv, qkv, qkv, qkv, do, ld, buf)


def _dil_bwd_kv(qkv, do, ld, bufk, bufv, g):
    S = qkv.shape[0]
    rb = min(DIL_ROWS, S)
    nb = S // rb
    d = DIL_PATTERN[g][1]
    span = W * d
    nub = rb // span

    def body(slope_ref, k_ref, v_ref, qc, qn, doc, don, ldc, ldn, bufk_ref, bufv_ref, dk_ref, dv_ref, dk_f32, dv_f32):
        i = pl.program_id(1)
        i_, _, dist, inwin = _window_geometry((2 * W, W), 0)
        bias = -(slope_ref[0, g:g + 1, 0:1] * float(d)) * dist.astype(F32)
        for b in range(nub):
            def unit(r, b=b):
                base = b * span + r
                rows = _phase_rows(base, d)
                if b < nub - 1:
                    nxt = _phase_rows(base + span, d)
                    qnext, donext, ldnext = qc[nxt, :], doc[nxt, :], ldc[nxt, :]
                    valid = inwin
                else:
                    nxt = _phase_rows(r, d)
                    qnext, donext, ldnext = qn[nxt, :], don[nxt, :], ldn[nxt, :]
                    valid = inwin & ((i_ < W) | (i < nb - 1))
                q2 = jnp.concatenate([qc[rows, :], qnext], axis=0).astype(BF16)
                do2 = jnp.concatenate([doc[rows, :], donext], axis=0).astype(BF16)
                ld2 = jnp.concatenate([ldc[rows, :], ldnext], axis=0)
                s = lax.dot_general(q2, k_ref[rows, :].astype(BF16), NT, preferred_element_type=F32)
                s = jnp.where(valid, s * DIL_SCALE + bias - ld2[:, 0:1], NEG)
                pr = jnp.exp(s)
                dp = lax.dot_general(do2, v_ref[rows, :].astype(BF16), NT, preferred_element_type=F32)
                ds = pr * (dp - ld2[:, LANES // 2:LANES // 2 + 1])
                dv_f32[rows, :] = lax.dot_general(pr.astype(BF16), do2, TN, preferred_element_type=F32)
                dk_f32[rows, :] = lax.dot_general(ds.astype(BF16), q2, TN, preferred_element_type=F32) * DIL_SCALE

            _for_phases(d, unit)
        dk_ref[...] = dk_f32[...].astype(BF16)
        dv_ref[...] = dv_f32[...].astype(BF16)

    col = lambda off: off * QKV_BLOCKS + g * DIL_HEADS_PER_GROUP
    cur = lambda off: pl.BlockSpec((rb, LANES), lambda h, i: (i, col(off) + h))
    nxt_blk = lambda i: jnp.minimum((i + 1) * nub, S // span - 1)
    nxt = lambda off: pl.BlockSpec((span, LANES), lambda h, i: (nxt_blk(i), col(off) + h))
    row = pl.BlockSpec((rb, LANES), lambda h, i: (i, h))
    row_nxt = pl.BlockSpec((span, LANES), lambda h, i: (nxt_blk(i), h))
    any_ = pl.BlockSpec(memory_space=pl.ANY)
    out = pl.BlockSpec((rb, LANES), lambda h, i: (i, g * DIL_HEADS_PER_GROUP + h))
    return pl.pallas_call(
        body, name=f"dil_attn_bwd_kv_g{g}", grid=(DIL_HEADS_PER_GROUP, nb),
        in_specs=[pl.BlockSpec((1, 8, LANES), lambda h, i: (h, 0, 0)), cur(1), cur(2), cur(0), nxt(0),
                  row, row_nxt, row, row_nxt, any_, any_],
        out_specs=[out, out],
        out_shape=[jax.ShapeDtypeStruct(bufk.shape, bufk.dtype), jax.ShapeDtypeStruct(bufv.shape, bufv.dtype)],
        input_output_aliases={9: 0, 10: 1},
        scratch_shapes=[pltpu.VMEM((rb, LANES), F32)] * 2,
        compiler_params=_cparams("parallel", "parallel"),
    )(_slope_table(), qkv, qkv, qkv, qkv, do, do, ld, ld, bufk, bufv)


def _place():
    x, y, c = lax.axis_index("x"), lax.axis_index("y"), lax.axis_index("c")
    chips = [(1 - x, y), (x, 1 - y), (1 - x, 1 - y)]
    return x, y, c, chips


def _gather_weights(wb):
    R = wb.shape[0]
    H = R // 2

    def body(w_ref, out_ref, send1, recv1, send2, recv2):
        x, y, c, chips = _place()
        me = 2 * x + y
        sib = (x, y, 1 - c)
        half = pl.ds(c * H, H)
        first = [pltpu.make_async_remote_copy(
            src_ref=w_ref.at[half], dst_ref=out_ref.at[me, half], send_sem=send1.at[k], recv_sem=recv1.at[k],
            device_id=(*chip, c), device_id_type=MESH) for k, chip in enumerate(chips)]
        for cp in first:
            cp.start()
        passed = []
        for k, (cx, cy) in enumerate(chips):
            slot = out_ref.at[2 * cx + cy, half]
            pltpu.make_async_remote_copy(src_ref=slot, dst_ref=slot, send_sem=send1.at[k], recv_sem=recv1.at[k],
                                         device_id=(cx, cy, c), device_id_type=MESH).wait_recv()
            cp = pltpu.make_async_remote_copy(src_ref=slot, dst_ref=slot, send_sem=send2.at[k], recv_sem=recv2.at[k],
                                              device_id=sib, device_id_type=MESH)
            cp.start()
            passed.append(cp)
        for k, (cx, cy) in enumerate(chips):
            slot = out_ref.at[2 * cx + cy, pl.ds((1 - c) * H, H)]
            pltpu.make_async_remote_copy(src_ref=slot, dst_ref=slot, send_sem=send2.at[k], recv_sem=recv2.at[k],
                                         device_id=sib, device_id_type=MESH).wait_recv()
        for cp in first + passed:
            cp.wait_send()

    return pl.pallas_call(
        body, name="gather_weights", in_specs=[ANY], out_specs=ANY,
        out_shape=jax.ShapeDtypeStruct((4, R, D_MODEL), wb.dtype),
        scratch_shapes=[pltpu.SemaphoreType.DMA((3,)), pltpu.SemaphoreType.DMA((3,)),
                        pltpu.SemaphoreType.DMA((3,)), pltpu.SemaphoreType.DMA((3,))],
    )(wb)


def _swap_halves(g):
    H = g.shape[1] // 2

    def body(g_ref, out_ref, send, recv):
        x, y, c, _ = _place()
        sib = (x, y, 1 - c)
        cps = [pltpu.make_async_remote_copy(
            src_ref=g_ref.at[s, pl.ds((1 - c) * H, H)], dst_ref=out_ref.at[s], send_sem=send.at[s],
            recv_sem=recv.at[s], device_id=sib, device_id_type=MESH) for s in range(4)]
        for cp in cps:
            cp.start()
        for cp in cps:
            cp.wait()

    return pl.pallas_call(
        body, name="swap_halves", in_specs=[ANY], out_specs=ANY,
        out_shape=jax.ShapeDtypeStruct((4, H, D_MODEL), g.dtype),
        scratch_shapes=[pltpu.SemaphoreType.DMA((4,)), pltpu.SemaphoreType.DMA((4,))],
    )(g)


def _scatter_partials(pb):
    def body(p_ref, out_ref, send, recv):
        x, y, c, chips = _place()
        me = 2 * x + y
        cps = [pltpu.make_async_remote_copy(
            src_ref=p_ref.at[2 * cx + cy], dst_ref=out_ref.at[me], send_sem=send.at[k], recv_sem=recv.at[k],
            device_id=(cx, cy, c), device_id_type=MESH) for k, (cx, cy) in enumerate(chips)]
        for cp in cps:
            cp.start()
        for k, (cx, cy) in enumerate(chips):
            slot = out_ref.at[2 * cx + cy]
            pltpu.make_async_remote_copy(src_ref=slot, dst_ref=slot, send_sem=send.at[k], recv_sem=recv.at[k],
                                         device_id=(cx, cy, c), device_id_type=MESH).wait_recv()
        for cp in cps:
            cp.wait_send()

    return pl.pallas_call(
        body, name="scatter_partials", in_specs=[ANY], out_specs=ANY,
        out_shape=jax.ShapeDtypeStruct(pb.shape, pb.dtype),
        scratch_shapes=[pltpu.SemaphoreType.DMA((3,)), pltpu.SemaphoreType.DMA((3,))],
    )(pb)


def _sibling_half(f, name):
    def body(f_ref, out_ref, send, recv):
        x, y, c, _ = _place()
        cp = pltpu.make_async_remote_copy(src_ref=f_ref, dst_ref=out_ref, send_sem=send, recv_sem=recv,
                                          device_id=(x, y, 1 - c), device_id_type=MESH)
        cp.start()
        cp.wait()

    return pl.pallas_call(
        body, name=name, in_specs=[ANY], out_specs=ANY,
        out_shape=jax.ShapeDtypeStruct(f.shape, f.dtype),
        scratch_shapes=[pltpu.SemaphoreType.DMA, pltpu.SemaphoreType.DMA],
    )(f)


def _gather_tiles(tile, name):
    m_per = tile.shape[0]

    def body(x_ref, out_ref, send_sems, recv_sems, local_sem):
        x, y, c, chips = _place()
        me, sibling = (x, y, c), (x, y, 1 - c)

        def rows(px, py, pc):
            return out_ref.at[pl.ds((4 * px + 2 * py + pc) * m_per, m_per), :]

        def copy(k, block, to, src=None):
            return pltpu.make_async_remote_copy(
                src_ref=rows(*block) if src is None else src, dst_ref=rows(*block),
                send_sem=send_sems.at[k], recv_sem=recv_sems.at[k], device_id=to, device_id_type=MESH)

        mine = pltpu.make_async_copy(x_ref, rows(*me), local_sem)
        mine.start()
        first = [copy(0, me, sibling, src=x_ref)]
        first += [copy(1 + j, me, (*chip, c), src=x_ref) for j, chip in enumerate(chips)]
        for cp in first:
            cp.start()
        passed = [copy(4 + j, (*chip, c), sibling) for j, chip in enumerate(chips)]
        for j, chip in enumerate(chips):
            copy(1 + j, (*chip, c), me).wait_recv()
            passed[j].start()
        copy(0, sibling, me).wait_recv()
        for j, chip in enumerate(chips):
            copy(4 + j, (*chip, 1 - c), me).wait_recv()
        for cp in first + passed:
            cp.wait_send()
        mine.wait()

    return pl.pallas_call(
        body, name=name, out_shape=jax.ShapeDtypeStruct((8 * m_per, D_MODEL), tile.dtype),
        in_specs=[pl.BlockSpec(memory_space=pltpu.VMEM)], out_specs=pl.BlockSpec(memory_space=pltpu.VMEM),
        scratch_shapes=[pltpu.SemaphoreType.DMA((7,)), pltpu.SemaphoreType.DMA((7,)), pltpu.SemaphoreType.DMA],
    )(tile)


def _pad_rows(a, rows):
    return jnp.pad(a, ((0, rows - a.shape[0]), (0, 0)))


def _pad_row(v):
    v = v.reshape(1, -1)
    return jnp.pad(v, ((0, 0), (0, D_MODEL - v.shape[1])))


def _pack_shard(fwi, fwo, dwi, dwo, pwu, pwg):
    parts = [_pad_rows(fwi.reshape(1028, D_MODEL), FLAT_A), fwo.reshape(256, D_MODEL), dwi.reshape(2560, D_MODEL),
             dwo.reshape(256, D_MODEL), pwu.reshape(128, D_MODEL), pwg.reshape(512, D_MODEL),
             jnp.zeros((FLAT_TOTAL - FLAT_USED, D_MODEL), fwi.dtype)]
    return jnp.concatenate(parts, axis=0)


def _unpack_shard(flat):
    out, r0 = {}, 0
    for name, rows, slot in FLAT_ROWS:
        out[name] = flat[r0:r0 + rows]
        r0 += slot
    return (out["fox_w_in"].reshape(1, D_MODEL, 1028), out["fox_w_out"].reshape(1, 256, D_MODEL),
            out["dil_w_in"].reshape(1, D_MODEL, 2560), out["dil_w_out"].reshape(1, 256, D_MODEL),
            out["ple_w_up"].reshape(2, PLE_DIM, 256), out["ple_w_gate"].reshape(2, 256, D_MODEL))


def _from_shard_columns(a, n):
    return a.reshape(4, D_MODEL, n).transpose(1, 0, 2).reshape(D_MODEL, 4 * n)


def _to_shard_columns(a, n):
    return a.reshape(D_MODEL, 4, n).transpose(1, 0, 2).reshape(4, n, D_MODEL)


def _unpack_a(wall_a):
    return _from_shard_columns(wall_a[:, :1028], 1028)


def _unpack_b(wall_b):
    out, r0 = {}, 0
    for name, rows, slot in FLAT_ROWS[1:]:
        out[name] = wall_b[:, r0:r0 + rows]
        r0 += slot
    pwu4 = out["ple_w_up"].reshape(4, 2, PLE_DIM, 256)
    pwg4 = out["ple_w_gate"].reshape(4, 2, 256, D_MODEL)
    pwu = [pwu4[:, i].transpose(1, 0, 2).reshape(PLE_DIM, D_MODEL) for i in range(2)]
    pwg = [pwg4[:, i].reshape(D_MODEL, D_MODEL) for i in range(2)]
    return (out["fox_w_out"].reshape(D_MODEL, D_MODEL), _from_shard_columns(out["dil_w_in"], 2560),
            out["dil_w_out"].reshape(D_MODEL, D_MODEL), pwu, pwg)


def _gathered_b_weights(wall_b, w_b, chip):
    return _unpack_b(lax.dynamic_update_slice(wall_b, w_b[None], (chip, 0, 0)))


def _pack_grads_a(gfwi):
    return jnp.pad(_to_shard_columns(gfwi, 1028), ((0, 0), (0, FLAT_A - 1028), (0, 0)))


def _pack_grads_b(gfwo, gdwi, gdwo, gpwu, gpwg):
    up = jnp.stack([a.reshape(PLE_DIM, 4, 256).transpose(1, 0, 2) for a in gpwu], axis=1)
    gate = jnp.stack([a.reshape(4, 256, D_MODEL) for a in gpwg], axis=1)
    parts = [gfwo.reshape(4, 256, D_MODEL), _to_shard_columns(gdwi, 2560), gdwo.reshape(4, 256, D_MODEL),
             up.reshape(4, 128, D_MODEL), gate.reshape(4, 512, D_MODEL)]
    return jnp.concatenate(parts, axis=1).astype(BF16)


def _local_step(x, p0, p1, tgt, fox_g, dil_g, fin_g, b_f, fwi, w_b, chip):
    S = x.shape[0]
    tm = min(256, S)
    nh = FOX_HEADS // 2
    w_qkv0 = jnp.concatenate([fwi[:, :D_MODEL] * 0.125, fwi[:, D_MODEL:3 * D_MODEL]], axis=1)
    w_z0 = fwi[:, 3 * D_MODEL:4 * D_MODEL]
    w_f0 = jnp.pad(fwi[:, 4 * D_MODEL:], ((0, 0), (0, LANES - FOX_HEADS)))
    b_full = jnp.pad(b_f.reshape(1, FOX_HEADS), ((0, 0), (0, LANES - FOX_HEADS)))

    n0, = _rows(_rms_fwd, [x], [(D_MODEL, BF16)], name="norm0", tm=tm, bcast=[fox_g])
    qkv0 = _mm(n0, w_qkv0, out_dtype=BF16, name="proj_qkv0")
    z0 = _mm(n0, w_z0, name="proj_z0")
    fl0 = _mm(n0, w_f0, name="proj_f0")
    qx0, kx0 = _cumsum_rows([fl0], name="forget_cumsum", width=LANES, pre=_log_forget, bcast=[b_full],
                            post=_fox_extras, outs=[(D_MODEL, BF16), (D_MODEL, BF16)])
    o0, a0, g0, wall_b = _fox_fwd(qkv0, qx0, kx0, z0, w_b)
    fwo, dwi, dwo, pwu, pwg = _gathered_b_weights(wall_b, w_b, chip)
    w_qkv1 = dwi[:, :3 * DIL_QKV]
    w_z1 = dwi[:, 3 * DIL_QKV:]
    h1 = _mm(g0, fwo, add=x, name="out_proj0")
    u0 = _mm(p0, pwu[0], name="ple_up0")
    t0 = _mm(h1, pwg[0], name="ple_gate0")
    h2, n1 = _rows(_ple_norm, [h1, u0, t0], [(D_MODEL, F32), (D_MODEL, BF16)], name="ple_mix0_norm1", tm=tm,
                   bcast=[dil_g])

    qkv1 = _mm(n1, w_qkv1, name="proj_qkv1")
    z1 = _mm(n1, w_z1, name="proj_z1")
    o1, g1, lse1 = _dil_fwd(qkv1, z1)
    h3 = _mm(g1, dwo, add=h2, name="out_proj1")
    u1 = _mm(p1, pwu[1], name="ple_up1")
    t1 = _mm(h3, pwg[1], name="ple_gate1")

    du1, dt1, dh4, g_fin, loss = _rows(
        _ple_loss, [h3, u1, t1, tgt], [(D_MODEL, BF16), (D_MODEL, BF16), (D_MODEL, F32)], name="ple_mix1_loss_head",
        tm=tm, bcast=[fin_g], acc=[((1, D_MODEL), F32), ((1, LANES), F32)])
    g_up1 = _mm(p1, du1, ta=True, name="grad_ple_up1")
    g_gate1 = _mm(h3, dt1, ta=True, name="grad_ple_gate1")
    dh3 = _mm(dt1, pwg[1], tb=True, add=dh4, name="ple_back1")
    dg1 = _mm(dh3, dwo, tb=True, name="out_back1")
    g_dwo = _mm(g1, dh3, ta=True, name="grad_out1")
    do1, dz1, ld1 = _rows(_gate_bwd, [dg1, o1, z1, lse1], [(D_MODEL, F32), (D_MODEL, BF16), (D_MODEL, F32)],
                          name="gate_bwd1", tm=tm)
    dq1, dk1, dv1 = (lax.empty((S, DIL_QKV), BF16) for _ in range(3))
    for g in range(DIL_GROUPS):
        dq1 = _dil_bwd_q(qkv1, do1, ld1, dq1, g)
        dk1, dv1 = _dil_bwd_kv(qkv1, do1, ld1, dk1, dv1, g)
    g_dwi = _mm_tn_parts(n1, [dq1, dk1, dv1, dz1], name="grad_in1")
    dn1 = _mm_nt_parts([dq1, dk1, dv1, dz1], dwi, name="in_back1")
    dh2, du0, dt0, g_dil = _rows(_norm_ple_bwd, [dn1, h2, dh3, u0, t0], [(D_MODEL, F32), (D_MODEL, BF16), (D_MODEL, BF16)],
                                 name="norm_bwd1_ple_bwd0", tm=tm, bcast=[dil_g], acc=[((1, D_MODEL), F32)])
    g_up0 = _mm(p0, du0, ta=True, name="grad_ple_up0")
    g_gate0 = _mm(h1, dt0, ta=True, name="grad_ple_gate0")
    dh1 = _mm(dt0, pwg[0], tb=True, add=dh2, name="ple_back0")
    dg0 = _mm(dh1, fwo, tb=True, name="out_back0")
    g_fwo = _mm(g0, dh1, ta=True, name="grad_out0")
    do0, dz0, qxa0, dx0 = _rows(_gate_bwd_fox, [dg0, o0, z0, a0], [(D_MODEL, BF16)] * 4, name="gate_bwd0", tm=tm)
    g_b = _pack_grads_b(g_fwo, g_dwi, g_dwo, [g_up0, g_up1], [g_gate0, g_gate1])
    dq0, drx, dk0, dv0, dkx, got_b = _fox_bwd(qkv0, do0, qxa0, dx0, kx0, g_b)
    dlogf, = _cumsum_rows([drx, dkx], name="forget_cumsum_bwd", width=LANES, reverse=True, pre=_forget_dc)
    df0, g_bf = _rows(_forget_bwd, [dlogf, fl0], [(LANES, BF16)], name="forget_bwd", tm=tm, bcast=[b_full],
                      acc=[((1, LANES), F32)])
    g_fwi_main = _mm_tn_parts(n0, [dq0, dk0, dv0, dz0], name="grad_in0")
    g_fwi_f = _mm(n0, df0, ta=True, name="grad_in0_forget")
    dn0 = _mm_nt_parts([dq0, dk0, dv0, dz0], fwi[:, :4 * D_MODEL], name="in_back0")
    dn0 = _mm(df0, w_f0, tb=True, add=dn0, name="in_back0_forget")
    dx, g_fox = _rows(_rms_bwd, [dn0, x, dh1], [(D_MODEL, F32)], name="norm_bwd0", tm=tm, bcast=[fox_g],
                      acc=[((1, D_MODEL), F32)])
    g_fwi = jnp.concatenate([g_fwi_main, g_fwi_f[:, :FOX_HEADS]], axis=1)
    return loss, dx, g_fwi, g_b, got_b, (g_fox, _pad_row(g_bf[:, :FOX_HEADS]), g_fin, g_dil)


def kernel(x, p, fox_norm, fox_w_in, fox_b_f, fox_w_out, dil_norm, dil_w_in, dil_w_out, ple_w_up, ple_w_gate, final_norm, loss_target, m_fox_norm, m_fox_w_in, m_fox_b_f, m_fox_w_out, m_dil_norm, m_dil_w_in, m_dil_w_out, m_ple_w_up, m_ple_w_gate, m_final_norm, v_fox_norm, v_fox_w_in, v_fox_b_f, v_fox_w_out, v_dil_norm, v_dil_w_in, v_dil_w_out, v_ple_w_up, v_ple_w_gate, v_final_norm):
    xi, yi, ci = lax.axis_index("x"), lax.axis_index("y"), lax.axis_index("c")
    chip = 2 * xi + yi

    w_flat = _pack_shard(*(w.astype(BF16) for w in (fox_w_in, fox_w_out, dil_w_in, dil_w_out, ple_w_up, ple_w_gate)))
    w_a, w_b = w_flat[:FLAT_A], w_flat[FLAT_A:FLAT_USED]
    fwi = _unpack_a(lax.dynamic_update_slice(_gather_weights(w_a), w_a[None], (chip, 0, 0)))
    gains = _gather_tiles(_pad_rows(_pad_row(dil_norm), 8), "gather_gain").reshape(4, 2, 8, D_MODEL)
    dil_g = gains[:, 0, 0, :256].reshape(1, D_MODEL)

    loss_part, dx, g_fwi, g_b, got_b, small = _local_step(
        x[0], p[0, 0], p[1, 0], loss_target[0], fox_norm.reshape(1, D_MODEL), dil_g,
        final_norm.reshape(1, D_MODEL), fox_b_f, fwi, w_b, chip)
    g_fox, g_bf, g_fin, g_dil = small
    sum4 = lambda a, b, c, d: ((a.astype(F32) + b.astype(F32)) + c.astype(F32)) + d.astype(F32)

    got_b = lax.dynamic_update_slice(got_b, lax.dynamic_slice_in_dim(g_b, chip, 1, axis=0), (chip, 0, 0))
    sum_b, = _rows(sum4, [got_b[s] for s in range(4)], [(D_MODEL, F32)], name="chip_sum_b", tm=FLAT_B // 8)
    g_flat_b, = _rows(lambda a, b: a + b, [sum_b, _sibling_half(sum_b, "sibling_sum_b")], [(D_MODEL, F32)],
                      name="pair_sum_b", tm=FLAT_B // 8)

    half = FLAT_A // 2
    g_a = _pack_grads_a(g_fwi)
    theirs = _swap_halves(g_a)
    mine = lax.dynamic_slice_in_dim(g_a, ci * half, half, axis=1)
    part, = _rows(lambda a, b: a + b, [mine.reshape(4 * half, D_MODEL), theirs.reshape(4 * half, D_MODEL)],
                  [(D_MODEL, BF16)], name="pair_sum", tm=half)
    part = part.reshape(4, half, D_MODEL)
    own = lax.dynamic_slice_in_dim(part, chip, 1, axis=0)
    by_chip = lax.dynamic_update_slice(_scatter_partials(part), own, (chip, 0, 0))
    half_sum, = _rows(sum4, [by_chip[s] for s in range(4)], [(D_MODEL, F32)], name="chip_sum", tm=half)
    other_half = _sibling_half(half_sum, "sibling_half")
    g_flat_a = jnp.where(ci == 0, jnp.concatenate([half_sum, other_half], axis=0),
                         jnp.concatenate([other_half, half_sum], axis=0))
    g_flat = jnp.concatenate([g_flat_a, g_flat_b, jnp.zeros((FLAT_TOTAL - FLAT_USED, D_MODEL), F32)], axis=0)

    tile = jnp.concatenate([g_fox, g_bf, g_fin, g_dil, jnp.pad(loss_part, ((0, 0), (0, D_MODEL - LANES))),
                            jnp.zeros((3, D_MODEL), F32)], axis=0)
    tiles = _gather_tiles(tile, "gather_small")
    tot, = _rows(lambda *t: functools.reduce(lambda a, b: a + b, t), [tiles[8 * s:8 * s + 8] for s in range(8)],
                 [(D_MODEL, F32)], name="small_sum", tm=8)
    loss = tot[4, 0]
    g_small = jnp.concatenate([
        tot[0:3],
        _pad_row(lax.dynamic_slice_in_dim(tot[3], chip * 256, 256)), jnp.zeros((4, D_MODEL), F32)], axis=0)
    small_tile = lambda a, b, c, d: jnp.concatenate(
        [_pad_row(a), _pad_row(b), _pad_row(c), _pad_row(d), jnp.zeros((4, D_MODEL), F32)], axis=0)
    w_small = small_tile(fox_norm, fox_b_f, final_norm, dil_norm)
    m_small = small_tile(m_fox_norm, m_fox_b_f, m_final_norm, m_dil_norm)
    v_small = small_tile(v_fox_norm, v_fox_b_f, v_final_norm, v_dil_norm)

    d_small, nm_small, nv_small = _rows(_adamw, [w_small, g_small, m_small, v_small], [(D_MODEL, F32)] * 3,
                                        name="adamw_small", tm=8)
    grads = _unpack_shard(g_flat)
    weights = (fox_w_in, fox_w_out, dil_w_in, dil_w_out, ple_w_up, ple_w_gate)
    firsts = (m_fox_w_in, m_fox_w_out, m_dil_w_in, m_dil_w_out, m_ple_w_up, m_ple_w_gate)
    seconds = (v_fox_w_in, v_fox_w_out, v_dil_w_in, v_dil_w_out, v_ple_w_up, v_ple_w_gate)
    big = []
    for (name, _, _), w, g, m, v in zip(FLAT_ROWS, weights, grads, firsts, seconds):
        width = w.shape[-1]
        flat2 = lambda a: a.reshape(-1, width)
        res = _rows(_adamw, [flat2(w), flat2(g), flat2(m), flat2(v)], [(width, F32)] * 3, name=f"adamw_{name}",
                    tm=128 if width > D_MODEL + LANES else 256)
        big.append([r.reshape(w.shape) for r in res])

    def leaves(shards, small_rows):
        fwi_, fwo_, dwi_, dwo_, pwu_, pwg_ = shards
        return (small_rows[0:1], fwi_, small_rows[1:2, :FOX_HEADS], fwo_, small_rows[3:4, :256], dwi_, dwo_,
                pwu_, pwg_, small_rows[2])

    per_kind = lambda k: [b[k] for b in big]
    return (loss, dx[None], *leaves(grads, g_small), *leaves(per_kind(0), d_small), *leaves(per_kind(1), nm_small),
            *leaves(per_kind(2), nv_small))
```

```python
import functools

import numpy as np
import jax
import jax.numpy as jnp
from jax import lax
from jax.experimental import pallas as pl
from jax.experimental.pallas import tpu as pltpu

F32 = jnp.float32
BF16 = jnp.bfloat16

D_MODEL = 1024
PLE_DIM = 256
FOX_HEADS = 16
FOX_HEAD_DIM = 64
DIL_PATTERN = ((128, 1), (512, 4), (2048, 16))
DIL_GROUPS = 3
DIL_HEADS_PER_GROUP = 8
DIL_HEAD_DIM = 128
DIL_WINDOW_STEPS = 128
DIL_QKV = 3072
ALIBI_MAX_EXP = 8.0
RMS_EPS = 1e-6
ADAM_LR, ADAM_B1, ADAM_B2, ADAM_EPS, ADAM_WD, ADAM_STEP = 0.001, 0.9, 0.999, 1e-08, 0.01, 10

LANES = 128
VMEM_LIMIT = 56 * 1024 * 1024
MESH = pl.DeviceIdType.MESH
ANY = pl.BlockSpec(memory_space=pl.ANY)
NEG = -1e30

FLAT_ROWS = (("fox_w_in", 1028, 1056), ("fox_w_out", 256, 256), ("dil_w_in", 2560, 2560), ("dil_w_out", 256, 256),
             ("ple_w_up", 128, 128), ("ple_w_gate", 512, 512))
FLAT_USED = sum(r for _, _, r in FLAT_ROWS)
FLAT_TOTAL = 4864
FLAT_TILE = 608
FLAT_A = FLAT_ROWS[0][2]
FLAT_B = FLAT_USED - FLAT_A


def _cparams(*sem):
    return pltpu.CompilerParams(dimension_semantics=sem, vmem_limit_bytes=VMEM_LIMIT)


def _sigmoid(x):
    return 1.0 / (1.0 + jnp.exp(-x))


def _mm(a, b, *, name, ta=False, tb=False, out_dtype=F32, add=None):
    if ta:
        K, M = a.shape
    else:
        M, K = a.shape
    if tb:
        N, Kb = b.shape
    else:
        Kb, N = b.shape
    assert K == Kb, (a.shape, b.shape)
    tm, tn, tk = min(M, 1024), min(N, 1024), min(K, 1024)
    assert M % tm == 0 and N % tn == 0 and K % tk == 0, (M, N, K)
    nk = K // tk
    dn = (((0 if ta else 1,), (1 if tb else 0,)), ((), ()))

    def body(*refs):
        if add is None:
            a_ref, b_ref, o_ref, acc = refs
        else:
            a_ref, b_ref, add_ref, o_ref, acc = refs
        k = pl.program_id(2)

        @pl.when(k == 0)
        def _():
            acc[...] = jnp.zeros_like(acc)

        acc[...] += lax.dot_general(a_ref[...].astype(BF16), b_ref[...].astype(BF16), dn,
                                    preferred_element_type=F32)

        @pl.when(k == nk - 1)
        def _():
            r = acc[...]
            if add is not None:
                r = r + add_ref[...]
            o_ref[...] = r.astype(out_dtype)

    a_spec = (pl.BlockSpec((tk, tm), lambda i, j, k: (k, i)) if ta
              else pl.BlockSpec((tm, tk), lambda i, j, k: (i, k)))
    b_spec = (pl.BlockSpec((tn, tk), lambda i, j, k: (j, k)) if tb
              else pl.BlockSpec((tk, tn), lambda i, j, k: (k, j)))
    in_specs = [a_spec, b_spec]
    args = [a, b]
    if add is not None:
        in_specs.append(pl.BlockSpec((tm, tn), lambda i, j, k: (i, j)))
        args.append(add)
    return pl.pallas_call(
        body, name=name, grid=(M // tm, N // tn, nk),
        in_specs=in_specs, out_specs=pl.BlockSpec((tm, tn), lambda i, j, k: (i, j)),
        out_shape=jax.ShapeDtypeStruct((M, N), out_dtype),
        scratch_shapes=[pltpu.VMEM((tm, tn), F32)],
        compiler_params=_cparams("parallel", "parallel", "arbitrary"),
    )(*args)


def _mm_nt_parts(a_parts, b, *, name):
    M = a_parts[0].shape[0]
    N, K = b.shape
    tm, tn, tk = min(M, 1024), min(N, 1024), 1024
    nks = [a.shape[1] // tk for a in a_parts]
    offs = [sum(nks[:p]) for p in range(len(nks))]
    nk = sum(nks)
    assert nk * tk == K and M % tm == 0 and N % tn == 0, (M, N, K)
    n_parts = len(a_parts)

    def body(*refs):
        a_refs, b_ref, o_ref, acc = refs[:n_parts], refs[n_parts], refs[n_parts + 1], refs[n_parts + 2]
        k = pl.program_id(2)

        @pl.when(k == 0)
        def _():
            acc[...] = jnp.zeros_like(acc)

        for a_ref, off, n in zip(a_refs, offs, nks):
            @pl.when((k >= off) & (k < off + n))
            def _(a_ref=a_ref):
                acc[...] += lax.dot_general(a_ref[...].astype(BF16), b_ref[...].astype(BF16),
                                            (((1,), (1,)), ((), ())), preferred_element_type=F32)

        @pl.when(k == nk - 1)
        def _():
            o_ref[...] = acc[...]

    a_specs = [pl.BlockSpec((tm, tk), lambda i, j, k, off=off, n=n: (i, jnp.clip(k - off, 0, n - 1)))
               for off, n in zip(offs, nks)]
    return pl.pallas_call(
        body, name=name, grid=(M // tm, N // tn, nk),
        in_specs=a_specs + [pl.BlockSpec((tn, tk), lambda i, j, k: (j, k))],
        out_specs=pl.BlockSpec((tm, tn), lambda i, j, k: (i, j)),
        out_shape=jax.ShapeDtypeStruct((M, N), F32), scratch_shapes=[pltpu.VMEM((tm, tn), F32)],
        compiler_params=_cparams("parallel", "parallel", "arbitrary"),
    )(*a_parts, b)


def _mm_tn_parts(a, b_parts, *, name, out_dtype=F32):
    K, M = a.shape
    tm, tn, tk = min(M, 1024), 1024, min(K, 1024)
    njs = [b.shape[1] // tn for b in b_parts]
    offs = [sum(njs[:p]) for p in range(len(njs))]
    nj, nk = sum(njs), K // tk
    assert M % tm == 0 and K % tk == 0 and all(b.shape[1] % tn == 0 for b in b_parts)
    n_parts = len(b_parts)

    def body(*refs):
        a_ref, b_refs, o_ref, acc = refs[0], refs[1:1 + n_parts], refs[1 + n_parts], refs[2 + n_parts]
        j, k = pl.program_id(1), pl.program_id(2)

        @pl.when(k == 0)
        def _():
            acc[...] = jnp.zeros_like(acc)

        for b_ref, off, n in zip(b_refs, offs, njs):
            @pl.when((j >= off) & (j < off + n))
            def _(b_ref=b_ref):
                acc[...] += lax.dot_general(a_ref[...].astype(BF16), b_ref[...].astype(BF16),
                                            (((0,), (0,)), ((), ())), preferred_element_type=F32)

        @pl.when(k == nk - 1)
        def _():
            o_ref[...] = acc[...].astype(out_dtype)

    def b_spec(off, n):
        def index(i, j, k):
            mine = (j >= off) & (j < off + n)
            return jnp.where(mine, k, 0), jnp.clip(j - off, 0, n - 1)
        return pl.BlockSpec((tk, tn), index)

    return pl.pallas_call(
        body, name=name, grid=(M // tm, nj, nk),
        in_specs=[pl.BlockSpec((tk, tm), lambda i, j, k: (k, i))] + [b_spec(off, n) for off, n in zip(offs, njs)],
        out_specs=pl.BlockSpec((tm, tn), lambda i, j, k: (i, j)),
        out_shape=jax.ShapeDtypeStruct((M, nj * tn), out_dtype), scratch_shapes=[pltpu.VMEM((tm, tn), F32)],
        compiler_params=_cparams("parallel", "parallel", "arbitrary"),
    )(a, *b_parts)


def _rows(fn, ins, outs, *, name, tm, bcast=(), acc=()):
    R = ins[0].shape[0]
    assert R % tm == 0, (R, tm)
    n_in, n_b, n_out, n_acc = len(ins), len(bcast), len(outs), len(acc)

    def body(*refs):
        in_refs = refs[:n_in + n_b]
        out_refs = refs[n_in + n_b:n_in + n_b + n_out]
        acc_refs = refs[n_in + n_b + n_out:]
        res = fn(*[r[...] for r in in_refs])
        if not isinstance(res, (tuple, list)):
            res = (res,)
        for r, v in zip(out_refs, res[:n_out]):
            r[...] = v.astype(r.dtype)
        first = pl.program_id(0) == 0
        for r, v in zip(acc_refs, res[n_out:]):
            @pl.when(first)
            def _(r=r, v=v):
                r[...] = v.astype(r.dtype)

            @pl.when(jnp.logical_not(first))
            def _(r=r, v=v):
                r[...] += v.astype(r.dtype)

    in_specs = [pl.BlockSpec((tm, a.shape[1]), lambda i: (i, 0)) for a in ins]
    in_specs += [pl.BlockSpec(b.shape, lambda i, nd=b.ndim: (0,) * nd) for b in bcast]
    out_specs = [pl.BlockSpec((tm, c), lambda i: (i, 0)) for c, _ in outs]
    out_specs += [pl.BlockSpec(s, lambda i, nd=len(s): (0,) * nd) for s, _ in acc]
    out_shape = [jax.ShapeDtypeStruct((R, c), dt) for c, dt in outs]
    out_shape += [jax.ShapeDtypeStruct(s, dt) for s, dt in acc]
    res = pl.pallas_call(
        body, name=name, grid=(R // tm,), in_specs=in_specs, out_specs=out_specs, out_shape=out_shape,
        compiler_params=_cparams("arbitrary" if acc else "parallel"),
    )(*ins, *bcast)
    return res


def _exact_dot(sel, x, left):
    hi = x.astype(BF16)
    r1 = x - hi.astype(F32)
    mid = r1.astype(BF16)
    lo = (r1 - mid.astype(F32)).astype(BF16)
    dot = (lambda p: jnp.dot(sel, p, preferred_element_type=F32)) if left else (
        lambda p: jnp.dot(p, sel, preferred_element_type=F32))
    return dot(hi) + dot(mid) + dot(lo)


def _cumsum_rows(ins, *, name, width, reverse=False, pre=None, bcast=(), post=None, outs=None):
    S = ins[0].shape[0]
    outs = outs or [(width, F32)]
    out_dtypes = [dt for _, dt in outs]
    tb = 256
    nb = S // tb
    assert S % tb == 0
    n_in = len(ins) + len(bcast)
    n_out = len(out_dtypes)

    def body(*refs):
        in_refs, o_refs, carry = refs[:n_in], refs[n_in:n_in + n_out], refs[n_in + n_out]

        @pl.when(pl.program_id(0) == 0)
        def _():
            carry[...] = jnp.zeros_like(carry)

        xv = in_refs[0][...] if pre is None else pre(*[r[...] for r in in_refs])
        r_ = lax.broadcasted_iota(jnp.int32, (tb, tb), 0)
        c_ = lax.broadcasted_iota(jnp.int32, (tb, tb), 1)
        tri = jnp.where((c_ >= r_) if reverse else (c_ <= r_), 1.0, 0.0).astype(BF16)
        cs = _exact_dot(tri, xv, left=True) + carry[...]
        outs = (cs,) if post is None else post(cs)
        for o_ref, v in zip(o_refs, outs):
            o_ref[...] = v.astype(o_ref.dtype)
        carry[...] = cs[0:1, :] if reverse else cs[tb - 1:tb, :]

    blk = (lambda i: (nb - 1 - i, 0)) if reverse else (lambda i: (i, 0))
    in_specs = [pl.BlockSpec((tb, a.shape[1]), blk) for a in ins]
    in_specs += [pl.BlockSpec(b.shape, lambda i, nd=b.ndim: (0,) * nd) for b in bcast]
    return pl.pallas_call(
        body, name=name, grid=(nb,), in_specs=in_specs, out_specs=[pl.BlockSpec((tb, c), blk) for c, _ in outs],
        out_shape=[jax.ShapeDtypeStruct((S, c), dt) for c, dt in outs], scratch_shapes=[pltpu.VMEM((1, width), F32)],
        compiler_params=_cparams("arbitrary"),
    )(*ins, *bcast)


def _rms_fwd(x, g):
    r = lax.rsqrt(jnp.mean(x * x, axis=1, keepdims=True) + RMS_EPS)
    return (x * r) * g


def _rms_bwd(dn, x, dres, g):
    r = lax.rsqrt(jnp.mean(x * x, axis=1, keepdims=True) + RMS_EPS)
    xh = x * r
    w = dn * g
    dx = r * (w - xh * jnp.mean(w * xh, axis=1, keepdims=True))
    return dres + dx, jnp.sum(dn * xh, axis=0, keepdims=True)


def _final_stage(h, tgt, g):
    r = lax.rsqrt(jnp.mean(h * h, axis=1, keepdims=True) + RMS_EPS)
    xh = h * r
    diff = xh * g - tgt
    loss = 0.5 * jnp.sum(jnp.mean(diff * diff, axis=1, keepdims=True), axis=0, keepdims=True)
    dy = diff * (1.0 / D_MODEL)
    w = dy * g
    dh = r * (w - xh * jnp.mean(w * xh, axis=1, keepdims=True))
    return dh, jnp.sum(dy * xh, axis=0, keepdims=True), jnp.broadcast_to(loss, (1, LANES))


def _ple_fwd(h, u, t):
    return h + u * _sigmoid(t)


def _ple_bwd(dh, u, t):
    s = _sigmoid(t)
    return dh * s, dh * u * s * (1.0 - s)


def _ple_norm(h, u, t, g):
    h2 = _ple_fwd(h, u, t)
    return h2, _rms_fwd(h2, g)


def _ple_loss(h, u, t, tgt, g):
    dh, g_gain, loss = _final_stage(_ple_fwd(h, u, t), tgt, g)
    du, dt = _ple_bwd(dh, u, t)
    return du, dt, dh, g_gain, loss


def _norm_ple_bwd(dn, x, dres, u, t, g):
    dh, g_gain = _rms_bwd(dn, x, dres, g)
    du, dt = _ple_bwd(dh, u, t)
    return dh, du, dt, g_gain


def _gate_fwd(o, z):
    return o * (z * _sigmoid(z))


def _head_sums(prod):
    tm, width = prod.shape
    cols = [jnp.broadcast_to(jnp.sum(prod[:, b * LANES:(b + 1) * LANES], axis=1, keepdims=True), (tm, LANES))
            for b in range(width // LANES)]
    return jnp.concatenate(cols, axis=1)


def _gate_bwd(dg, o, z, lse):
    s = _sigmoid(z)
    do = dg * (z * s)
    dz = dg * o * (s * (1.0 + z * (1.0 - s)))
    lane = lax.broadcasted_iota(jnp.int32, do.shape, 1) % LANES
    return do, dz, jnp.where(lane < LANES // 2, lse, _head_sums(do * o))


def _log_forget(fl, b):
    u = fl + b
    return jnp.minimum(u, 0.0) - jnp.log(1.0 + jnp.exp(-jnp.abs(u)))


def _adamw(w, g, m, v):
    m = ADAM_B1 * m + (1.0 - ADAM_B1) * g
    v = ADAM_B2 * v + (1.0 - ADAM_B2) * (g * g)
    m_hat = m / (1.0 - ADAM_B1 ** ADAM_STEP)
    v_hat = v / (1.0 - ADAM_B2 ** ADAM_STEP)
    delta = -ADAM_LR * (m_hat / (jnp.sqrt(v_hat) + ADAM_EPS) + ADAM_WD * w)
    return delta, m, v


FOX_TQ = 1024


def _split3(x):
    p1 = x.astype(BF16).astype(F32)
    r = x - p1
    p2 = r.astype(BF16).astype(F32)
    return p1, p2, r - p2


def _lane_in_head(shape):
    return lax.broadcasted_iota(jnp.int32, shape, 1) % FOX_HEAD_DIM


def _query_extras(x):
    lm = _lane_in_head(x.shape)
    p1, p2, p3 = _split3(x)
    return jnp.where(lm == 0, p1, jnp.where(lm == 1, p2, jnp.where(lm == 2, p3, jnp.where(lm < 6, 1.0, 0.0))))


def _key_extras(c):
    lm = _lane_in_head(c.shape)
    p1, p2, p3 = _split3(c)
    return jnp.where(lm < 3, 1.0, jnp.where(lm == 3, -p1, jnp.where(lm == 4, -p2, jnp.where(lm == 5, -p3, 0.0))))


def _head_lane_base(head):
    return LANES * (head // 2) + FOX_HEAD_DIM * (1 - head % 2)


def _fox_extras(c):
    head = lax.broadcasted_iota(jnp.int32, (LANES, D_MODEL), 0)
    lane = lax.broadcasted_iota(jnp.int32, (LANES, D_MODEL), 1)
    owner = 2 * (lane // LANES) + 1 - (lane % LANES) // FOX_HEAD_DIM
    wide = _exact_dot(jnp.where(head == owner, 1.0, 0.0).astype(BF16), c, left=False)
    return _query_extras(wide), _key_extras(wide)


def _swapped_head_sums(prod):
    tm, width = prod.shape
    lane = lax.broadcasted_iota(jnp.int32, (tm, LANES), 1)
    low = lane < FOX_HEAD_DIM
    cols = []
    for b in range(width // LANES):
        blk = prod[:, b * LANES:(b + 1) * LANES]
        sa = jnp.sum(jnp.where(low, blk, 0.0), axis=1, keepdims=True)
        sb = jnp.sum(jnp.where(low, 0.0, blk), axis=1, keepdims=True)
        cols.append(jnp.where(low, sb, sa))
    return jnp.concatenate(cols, axis=1)


def _gate_bwd_fox(dg, o, z, a):
    s = _sigmoid(z)
    do = dg * (z * s)
    dz = dg * o * (s * (1.0 + z * (1.0 - s)))
    lm = _lane_in_head(do.shape)
    d1, d2, d3 = _split3(-_swapped_head_sums(do * o))
    dx = jnp.where(lm == 0, d1, jnp.where(lm == 1, d2, jnp.where(lm == 2, d3, 0.0)))
    return do, dz, _query_extras(a), dx


def _forget_dc(drx, dkx):
    lane = lax.broadcasted_iota(jnp.int32, (D_MODEL, LANES), 0)
    head = lax.broadcasted_iota(jnp.int32, (D_MODEL, LANES), 1)
    base = _head_lane_base(head)
    pick = lambda l: jnp.where((lane == base + l) & (head < FOX_HEADS), 1.0, 0.0).astype(BF16)
    return _exact_dot(pick(0), drx, left=False) - _exact_dot(pick(3), dkx, left=False)


def _forget_bwd(dl, fl, b):
    du = dl * (1.0 / (1.0 + jnp.exp(fl + b)))
    return du, jnp.sum(du, axis=0, keepdims=True)


def _chip_exchange(src_of, out_ref, send, recv):
    x, y, c = lax.axis_index("x"), lax.axis_index("y"), lax.axis_index("c")
    chips = [(1 - x, y), (x, 1 - y), (1 - x, 1 - y)]
    me = 2 * x + y
    sends = [pltpu.make_async_remote_copy(src_ref=src_of(cx, cy), dst_ref=out_ref.at[me], send_sem=send.at[k],
                                          recv_sem=recv.at[k], device_id=(cx, cy, c), device_id_type=MESH)
             for k, (cx, cy) in enumerate(chips)]

    def start():
        for cp in sends:
            cp.start()

    def wait():
        for k, (cx, cy) in enumerate(chips):
            slot = out_ref.at[2 * cx + cy]
            pltpu.make_async_remote_copy(src_ref=slot, dst_ref=slot, send_sem=send.at[k], recv_sem=recv.at[k],
                                         device_id=(cx, cy, c), device_id_type=MESH).wait_recv()
        for cp in sends:
            cp.wait_send()

    return start, wait


def _fox_fwd(qkv, qx, kx, z, w_b):
    S = qkv.shape[0]
    tq = min(FOX_TQ, S)
    nq = S // tq
    nt = (((1,), (1,)), ((), ()))

    def body(q_ref, qx_ref, k_ref, v_ref, kx_ref, z_ref, wb_ref, o_ref, a_ref, g_ref, wall_ref, send, recv):
        i = pl.program_id(1)
        start, wait = _chip_exchange(lambda cx, cy: wb_ref, wall_ref, send, recv)
        pl.when((pl.program_id(0) == 0) & (i == 0))(start)
        low = lax.broadcasted_iota(jnp.int32, (tq, LANES), 1) < FOX_HEAD_DIM
        row = lax.broadcasted_iota(jnp.int32, (tq, tq), 0)
        col = lax.broadcasted_iota(jnp.int32, (tq, tq), 1)
        q2, x2 = q_ref[...], qx_ref[...]
        qa = (jnp.where(low, q2, x2), jnp.where(low, x2, q2))

        def step(kb, carry, diag):
            start = pl.multiple_of(kb * tq, tq)
            k2 = k_ref[pl.ds(start, tq), :]
            v2 = v_ref[pl.ds(start, tq), :]
            y2 = kx_ref[pl.ds(start, tq), :]
            one = jnp.ones_like(v2)
            ka = (jnp.where(low, k2, y2), jnp.where(low, y2, k2))
            va = (jnp.where(low, v2, one), jnp.where(low, one, v2))
            out = []
            for hh in range(2):
                m, acc = carry[hh]
                s = lax.dot_general(qa[hh], ka[hh], nt, preferred_element_type=F32)
                if diag:
                    s = jnp.where(col <= row, s, NEG)
                m_new = jnp.maximum(m, jnp.max(s, axis=1, keepdims=True))
                pr = jnp.exp(s - m_new)
                acc = jnp.exp(m - m_new) * acc + jnp.dot(pr.astype(BF16), va[hh], preferred_element_type=F32)
                out.append((m_new, acc))
            return tuple(out)

        init = ((jnp.full((tq, 1), NEG, F32), jnp.zeros((tq, LANES), F32)),) * 2
        carry = lax.fori_loop(0, i, functools.partial(step, diag=False), init)
        (m_a, acc_a), (m_b, acc_b) = step(i, carry, True)
        l_a, l_b = acc_a[:, FOX_HEAD_DIM:FOX_HEAD_DIM + 1], acc_b[:, 0:1]
        xf = x2.astype(F32)
        c_a = xf[:, 64:65] + xf[:, 65:66] + xf[:, 66:67]
        c_b = xf[:, 0:1] + xf[:, 1:2] + xf[:, 2:3]
        o = jnp.where(low, acc_a / l_a, acc_b / l_b)
        o_ref[...] = o
        g_ref[...] = _gate_fwd(o, z_ref[...]).astype(BF16)
        a_ref[...] = jnp.where(low, jnp.broadcast_to(c_b - (m_b + jnp.log(l_b)), (tq, LANES)),
                               jnp.broadcast_to(c_a - (m_a + jnp.log(l_a)), (tq, LANES)))
        pl.when((pl.program_id(0) == FOX_HEADS // 2 - 1) & (i == nq - 1))(wait)

    blk = lambda cb: pl.BlockSpec((tq, LANES), lambda h, i, cb=cb: (i, cb + h))
    res = lambda cb: pl.BlockSpec((S, LANES), lambda h, i, cb=cb: (0, cb + h))
    return pl.pallas_call(
        body, name="fox_attn_fwd", grid=(FOX_HEADS // 2, nq),
        in_specs=[blk(0), blk(0), res(8), res(16), res(0), blk(0), ANY],
        out_specs=[blk(0), blk(0), blk(0), ANY],
        out_shape=[jax.ShapeDtypeStruct((S, D_MODEL), F32), jax.ShapeDtypeStruct((S, D_MODEL), F32),
                   jax.ShapeDtypeStruct((S, D_MODEL), BF16), jax.ShapeDtypeStruct((4,) + w_b.shape, w_b.dtype)],
        scratch_shapes=[pltpu.SemaphoreType.DMA((3,)), pltpu.SemaphoreType.DMA((3,))],
        compiler_params=_cparams("arbitrary", "arbitrary"),
    )(qkv, qx, qkv, qkv, kx, z, w_b)


def _fox_bwd(qkv, do, qxa, dx, kx, g_b):
    S = qkv.shape[0]
    tq = min(FOX_TQ, S)
    nq = S // tq
    nt = (((1,), (1,)), ((), ()))
    tn = (((0,), (0,)), ((), ()))

    def body(q_ref, qx_ref, do_ref, dx_ref, k_ref, v_ref, kx_ref, gb_ref, dq_ref, dr_ref, dk_ref, dv_ref, dkx_ref,
             got_ref, send, recv):
        kb = pl.program_id(1)
        start, wait = _chip_exchange(lambda cx, cy: gb_ref.at[2 * cx + cy], got_ref, send, recv)
        pl.when((pl.program_id(0) == 0) & (kb == 0))(start)
        low = lax.broadcasted_iota(jnp.int32, (tq, LANES), 1) < FOX_HEAD_DIM
        row = lax.broadcasted_iota(jnp.int32, (tq, tq), 0)
        col = lax.broadcasted_iota(jnp.int32, (tq, tq), 1)

        @pl.when(kb == 0)
        def _():
            dq_ref[...] = jnp.zeros_like(dq_ref)
            dr_ref[...] = jnp.zeros_like(dr_ref)

        k2, v2, y2 = k_ref[...], v_ref[...], kx_ref[...]
        one = jnp.ones_like(v2)
        ka = (jnp.where(low, k2, y2), jnp.where(low, y2, k2))
        va = (jnp.where(low, v2, one), jnp.where(low, one, v2))

        def step(qb, carry, diag):
            start = pl.multiple_of(qb * tq, tq)
            q2 = q_ref[pl.ds(start, tq), :]
            x2 = qx_ref[pl.ds(start, tq), :]
            d2 = do_ref[pl.ds(start, tq), :]
            e2 = dx_ref[pl.ds(start, tq), :]
            qa = (jnp.where(low, q2, x2), jnp.where(low, x2, q2))
            da = (jnp.where(low, d2, e2), jnp.where(low, e2, d2))
            new, res = [], []
            for hh in range(2):
                dk, dv = carry[hh]
                s = lax.dot_general(qa[hh], ka[hh], nt, preferred_element_type=F32)
                if diag:
                    s = jnp.where(col <= row, s, NEG)
                pr = jnp.exp(s)
                ds = pr * lax.dot_general(da[hh], va[hh], nt, preferred_element_type=F32)
                prb, dsb = pr.astype(BF16), ds.astype(BF16)
                dv = dv + lax.dot_general(prb, da[hh], tn, preferred_element_type=F32)
                dk = dk + lax.dot_general(dsb, qa[hh], tn, preferred_element_type=F32)
                res.append(jnp.dot(dsb, ka[hh], preferred_element_type=F32))
                new.append((dk, dv))
            dq_ref[pl.ds(start, tq), :] += jnp.where(low, res[0], res[1])
            dr_ref[pl.ds(start, tq), :] += jnp.where(low, res[1], res[0])
            return tuple(new)

        init = ((jnp.zeros((tq, LANES), F32), jnp.zeros((tq, LANES), F32)),) * 2
        carry = step(kb, init, True)
        (dk_a, dv_a), (dk_b, dv_b) = lax.fori_loop(kb + 1, nq, functools.partial(step, diag=False), carry)
        dk_ref[...] = jnp.where(low, dk_a, dk_b).astype(BF16)
        dv_ref[...] = jnp.where(low, dv_a, dv_b).astype(BF16)
        dkx_ref[...] = jnp.where(low, dk_b, dk_a)

        @pl.when(kb == nq - 1)
        def _():
            dq_ref[...] = dq_ref[...] * (FOX_HEAD_DIM ** -0.5)

        pl.when((pl.program_id(0) == FOX_HEADS // 2 - 1) & (kb == nq - 1))(wait)

    res = lambda cb: pl.BlockSpec((S, LANES), lambda h, k, cb=cb: (0, cb + h))
    blk = lambda cb: pl.BlockSpec((tq, LANES), lambda h, k, cb=cb: (k, cb + h))
    f32, b16 = jax.ShapeDtypeStruct((S, D_MODEL), F32), jax.ShapeDtypeStruct((S, D_MODEL), BF16)
    return pl.pallas_call(
        body, name="fox_attn_bwd", grid=(FOX_HEADS // 2, nq),
        in_specs=[res(0), res(0), res(0), res(0), blk(8), blk(16), blk(0), ANY],
        out_specs=[res(0), res(0), blk(0), blk(0), blk(0), ANY],
        out_shape=[f32, f32, b16, b16, f32, jax.ShapeDtypeStruct(g_b.shape, g_b.dtype)],
        scratch_shapes=[pltpu.SemaphoreType.DMA((3,)), pltpu.SemaphoreType.DMA((3,))],
        compiler_params=_cparams("arbitrary", "arbitrary"),
    )(qkv, qxa, do, dx, qkv, qkv, kx, g_b)


def _alibi_slopes():
    n = DIL_GROUPS * DIL_HEADS_PER_GROUP
    s = np.float32(2.0) ** (np.float32(-ALIBI_MAX_EXP) * np.arange(1, n + 1, dtype=np.float32) / np.float32(n))
    return s.astype(np.float32).reshape(DIL_GROUPS, DIL_HEADS_PER_GROUP)


W = DIL_WINDOW_STEPS
DIL_SCALE = DIL_HEAD_DIM ** -0.5


DIL_ROWS = 2048
NT = (((1,), (1,)), ((), ()))
TN = (((0,), (0,)), ((), ()))
QKV_BLOCKS = DIL_QKV // LANES


def _slope_table():
    t = np.zeros((DIL_HEADS_PER_GROUP, 8, LANES), np.float32)
    t[:, :DIL_GROUPS, :] = _alibi_slopes().T[:, :, None]
    return jnp.asarray(t)


def _phase_rows(start, d):
    return pl.ds(start, W, stride=d) if d > 1 else pl.ds(start, W)


def _for_phases(d, unit):
    if d == 1:
        unit(0)
    else:
        lax.fori_loop(0, d, lambda r, c: (unit(r), c)[1], 0)


def _window_geometry(shape, q_axis_offset):
    i_ = lax.broadcasted_iota(jnp.int32, shape, 0)
    j_ = lax.broadcasted_iota(jnp.int32, shape, 1)
    dist = q_axis_offset + i_ - j_
    return i_, j_, dist, (dist >= 0) & (dist <= W)


def _dil_fwd(qkv, z):
    S = qkv.shape[0]
    rb = min(DIL_ROWS, S)
    nb = S // rb
    spans = [W * d for _, d in DIL_PATTERN]

    def body(*refs):
        slope_ref = refs[0]
        q, kc, kp, vc, vp = refs[1:4], refs[4:7], refs[7:10], refs[10:13], refs[13:16]
        z_ref, o_ref, g_ref, l_ref = refs[16:20]
        og, lg = refs[20:23], refs[23:26]
        i = pl.program_id(1)
        _, j_, dist, inwin = _window_geometry((W, 2 * W), W)
        distf = dist.astype(F32)
        for g, (_, d) in enumerate(DIL_PATTERN):
            span = spans[g]
            bias = -(slope_ref[0, g:g + 1, 0:1] * float(d)) * distf
            def phase(r, g=g, d=d, span=span, bias=bias):
                kprev, vprev = kp[g][_phase_rows(r, d), :].astype(BF16), vp[g][_phase_rows(r, d), :].astype(BF16)
                for b in range(rb // span):
                    rows = _phase_rows(b * span + r, d)
                    kcur, vcur = kc[g][rows, :].astype(BF16), vc[g][rows, :].astype(BF16)
                    valid = inwin if b > 0 else inwin & ((j_ >= W) | (i > 0))
                    k2 = jnp.concatenate([kprev, kcur], axis=0)
                    v2 = jnp.concatenate([vprev, vcur], axis=0)
                    s = lax.dot_general(q[g][rows, :].astype(BF16), k2, NT, preferred_element_type=F32)
                    s = jnp.where(valid, s * DIL_SCALE + bias, NEG)
                    m = jnp.max(s, axis=1, keepdims=True)
                    pr = jnp.exp(s - m)
                    l = jnp.sum(pr, axis=1, keepdims=True)
                    og[g][rows, :] = jnp.dot(pr.astype(BF16), v2, preferred_element_type=F32) / l
                    lg[g][rows, :] = jnp.broadcast_to(m + jnp.log(l), (W, LANES))
                    kprev, vprev = kcur, vcur

            _for_phases(d, phase)

        def mix(cix, c):
            sl = pl.ds(pl.multiple_of(cix * 256, 256), 256)
            l1, l2, l3 = lg[0][sl, :], lg[1][sl, :], lg[2][sl, :]
            m = jnp.maximum(jnp.maximum(l1, l2), l3)
            e1, e2, e3 = jnp.exp(l1 - m), jnp.exp(l2 - m), jnp.exp(l3 - m)
            tot = e1 + e2 + e3
            o = (e1 * og[0][sl, :] + e2 * og[1][sl, :] + e3 * og[2][sl, :]) / tot
            o_ref[sl, :] = o
            g_ref[sl, :] = _gate_fwd(o, z_ref[sl, :]).astype(BF16)
            l_ref[sl, :] = m + jnp.log(tot)
            return c

        lax.fori_loop(0, rb // 256, mix, 0)

    cur = lambda off, g: pl.BlockSpec((rb, LANES), lambda h, i: (i, off * QKV_BLOCKS + g * DIL_HEADS_PER_GROUP + h))
    prev = lambda off, g: pl.BlockSpec(
        (spans[g], LANES),
        lambda h, i: (jnp.maximum(i * (rb // spans[g]) - 1, 0), off * QKV_BLOCKS + g * DIL_HEADS_PER_GROUP + h))
    row = pl.BlockSpec((rb, LANES), lambda h, i: (i, h))
    groups = range(DIL_GROUPS)
    in_specs = [pl.BlockSpec((1, 8, LANES), lambda h, i: (h, 0, 0))]
    in_specs += [cur(0, g) for g in groups] + [cur(1, g) for g in groups] + [prev(1, g) for g in groups]
    in_specs += [cur(2, g) for g in groups] + [prev(2, g) for g in groups] + [row]
    f32 = jax.ShapeDtypeStruct((S, D_MODEL), F32)
    return pl.pallas_call(
        body, name="dil_attn_fwd", grid=(DIL_HEADS_PER_GROUP, nb), in_specs=in_specs,
        out_specs=[row, row, row], out_shape=[f32, jax.ShapeDtypeStruct((S, D_MODEL), BF16), f32],
        scratch_shapes=[pltpu.VMEM((rb, LANES), F32)] * 6,
        compiler_params=_cparams("parallel", "parallel"),
    )(_slope_table(), *([qkv] * 15), z)


def _dil_bwd_q(qkv, do, ld, buf, g):
    S = qkv.shape[0]
    rb = min(DIL_ROWS, S)
    nb = S // rb
    d = DIL_PATTERN[g][1]
    span = W * d

    def body(slope_ref, q_ref, kc, kp, vc, vp, do_ref, ld_ref, buf_ref, dq_ref, dq_f32):
        i = pl.program_id(1)
        _, j_, dist, inwin = _window_geometry((W, 2 * W), W)
        bias = -(slope_ref[0, g:g + 1, 0:1] * float(d)) * dist.astype(F32)
        def phase(r):
            kprev, vprev = kp[_phase_rows(r, d), :].astype(BF16), vp[_phase_rows(r, d), :].astype(BF16)
            for b in range(rb // span):
                rows = _phase_rows(b * span + r, d)
                kcur, vcur = kc[rows, :].astype(BF16), vc[rows, :].astype(BF16)
                valid = inwin if b > 0 else inwin & ((j_ >= W) | (i > 0))
                k2 = jnp.concatenate([kprev, kcur], axis=0)
                v2 = jnp.concatenate([vprev, vcur], axis=0)
                ld = ld_ref[rows, :]
                s = lax.dot_general(q_ref[rows, :].astype(BF16), k2, NT, preferred_element_type=F32)
                s = jnp.where(valid, s * DIL_SCALE + bias - ld[:, 0:1], NEG)
                pr = jnp.exp(s)
                dp = lax.dot_general(do_ref[rows, :].astype(BF16), v2, NT, preferred_element_type=F32)
                ds = pr * (dp - ld[:, LANES // 2:LANES // 2 + 1])
                dq_f32[rows, :] = jnp.dot(ds.astype(BF16), k2, preferred_element_type=F32) * DIL_SCALE
                kprev, vprev = kcur, vcur

        _for_phases(d, phase)
        dq_ref[...] = dq_f32[...].astype(BF16)

    col = lambda off: off * QKV_BLOCKS + g * DIL_HEADS_PER_GROUP
    cur = lambda off: pl.BlockSpec((rb, LANES), lambda h, i: (i, col(off) + h))
    prev = lambda off: pl.BlockSpec((span, LANES), lambda h, i: (jnp.maximum(i * (rb // span) - 1, 0), col(off) + h))
    row = pl.BlockSpec((rb, LANES), lambda h, i: (i, h))
    return pl.pallas_call(
        body, name=f"dil_attn_bwd_q_g{g}", grid=(DIL_HEADS_PER_GROUP, nb),
        in_specs=[pl.BlockSpec((1, 8, LANES), lambda h, i: (h, 0, 0)), cur(0), cur(1), prev(1), cur(2), prev(2),
                  row, row, pl.BlockSpec(memory_space=pl.ANY)],
        out_specs=pl.BlockSpec((rb, LANES), lambda h, i: (i, g * DIL_HEADS_PER_GROUP + h)),
        out_shape=jax.ShapeDtypeStruct(buf.shape, buf.dtype), input_output_aliases={8: 0},
        scratch_shapes=[pltpu.VMEM((rb, LANES), F32)],
        compiler_params=_cparams("parallel", "parallel"),
    )(_slope_table(), qkv, qkv, qkv, qkv, qkv, do, ld, buf)


def _dil_bwd_kv(qkv, do, ld, bufk, bufv, g):
    S = qkv.shape[0]
    rb = min(DIL_ROWS, S)
    nb = S // rb
    d = DIL_PATTERN[g][1]
    span = W * d
    nub = rb // span

    def body(slope_ref, k_ref, v_ref, qc, qn, doc, don, ldc, ldn, bufk_ref, bufv_ref, dk_ref, dv_ref, dk_f32, dv_f32):
        i = pl.program_id(1)
        i_, _, dist, inwin = _window_geometry((2 * W, W), 0)
        bias = -(slope_ref[0, g:g + 1, 0:1] * float(d)) * dist.astype(F32)
        def phase(r):
            first = _phase_rows(r, d)
            qcur, docur, ldcur = qc[first, :].astype(BF16), doc[first, :].astype(BF16), ldc[first, :]
            for b in range(nub):
                rows = _phase_rows(b * span + r, d)
                if b < nub - 1:
                    nxt = _phase_rows((b + 1) * span + r, d)
                    qnext, donext, ldnext, valid = qc[nxt, :].astype(BF16), doc[nxt, :].astype(BF16), ldc[nxt, :], inwin
                else:
                    qnext, donext, ldnext = qn[first, :].astype(BF16), don[first, :].astype(BF16), ldn[first, :]
                    valid = inwin & ((i_ < W) | (i < nb - 1))
                q2 = jnp.concatenate([qcur, qnext], axis=0)
                do2 = jnp.concatenate([docur, donext], axis=0)
                ld2 = jnp.concatenate([ldcur, ldnext], axis=0)
                s = lax.dot_general(q2, k_ref[rows, :].astype(BF16), NT, preferred_element_type=F32)
                s = jnp.where(valid, s * DIL_SCALE + bias - ld2[:, 0:1], NEG)
                pr = jnp.exp(s)
                dp = lax.dot_general(do2, v_ref[rows, :].astype(BF16), NT, preferred_element_type=F32)
                ds = pr * (dp - ld2[:, LANES // 2:LANES // 2 + 1])
                dv_f32[rows, :] = lax.dot_general(pr.astype(BF16), do2, TN, preferred_element_type=F32)
                dk_f32[rows, :] = lax.dot_general(ds.astype(BF16), q2, TN, preferred_element_type=F32) * DIL_SCALE
                qcur, docur, ldcur = qnext, donext, ldnext

        _for_phases(d, phase)
        dk_ref[...] = dk_f32[...].astype(BF16)
        dv_ref[...] = dv_f32[...].astype(BF16)

    col = lambda off: off * QKV_BLOCKS + g * DIL_HEADS_PER_GROUP
    cur = lambda off: pl.BlockSpec((rb, LANES), lambda h, i: (i, col(off) + h))
    nxt_blk = lambda i: jnp.minimum((i + 1) * nub, S // span - 1)
    nxt = lambda off: pl.BlockSpec((span, LANES), lambda h, i: (nxt_blk(i), col(off) + h))
    row = pl.BlockSpec((rb, LANES), lambda h, i: (i, h))
    row_nxt = pl.BlockSpec((span, LANES), lambda h, i: (nxt_blk(i), h))
    any_ = pl.BlockSpec(memory_space=pl.ANY)
    out = pl.BlockSpec((rb, LANES), lambda h, i: (i, g * DIL_HEADS_PER_GROUP + h))
    return pl.pallas_call(
        body, name=f"dil_attn_bwd_kv_g{g}", grid=(DIL_HEADS_PER_GROUP, nb),
        in_specs=[pl.BlockSpec((1, 8, LANES), lambda h, i: (h, 0, 0)), cur(1), cur(2), cur(0), nxt(0),
                  row, row_nxt, row, row_nxt, any_, any_],
        out_specs=[out, out],
        out_shape=[jax.ShapeDtypeStruct(bufk.shape, bufk.dtype), jax.ShapeDtypeStruct(bufv.shape, bufv.dtype)],
        input_output_aliases={9: 0, 10: 1},
        scratch_shapes=[pltpu.VMEM((rb, LANES), F32)] * 2,
        compiler_params=_cparams("parallel", "parallel"),
    )(_slope_table(), qkv, qkv, qkv, qkv, do, do, ld, ld, bufk, bufv)


def _place():
    x, y, c = lax.axis_index("x"), lax.axis_index("y"), lax.axis_index("c")
    chips = [(1 - x, y), (x, 1 - y), (1 - x, 1 - y)]
    return x, y, c, chips


def _gather_weights(wb):
    R = wb.shape[0]
    H = R // 2

    def body(w_ref, out_ref, send1, recv1, send2, recv2):
        x, y, c, chips = _place()
        me = 2 * x + y
        sib = (x, y, 1 - c)
        half = pl.ds(c * H, H)
        first = [pltpu.make_async_remote_copy(
            src_ref=w_ref.at[half], dst_ref=out_ref.at[me, half], send_sem=send1.at[k], recv_sem=recv1.at[k],
            device_id=(*chip, c), device_id_type=MESH) for k, chip in enumerate(chips)]
        for cp in first:
            cp.start()
        passed = []
        for k, (cx, cy) in enumerate(chips):
            slot = out_ref.at[2 * cx + cy, half]
            pltpu.make_async_remote_copy(src_ref=slot, dst_ref=slot, send_sem=send1.at[k], recv_sem=recv1.at[k],
                                         device_id=(cx, cy, c), device_id_type=MESH).wait_recv()
            cp = pltpu.make_async_remote_copy(src_ref=slot, dst_ref=slot, send_sem=send2.at[k], recv_sem=recv2.at[k],
                                              device_id=sib, device_id_type=MESH)
            cp.start()
            passed.append(cp)
        for k, (cx, cy) in enumerate(chips):
            slot = out_ref.at[2 * cx + cy, pl.ds((1 - c) * H, H)]
            pltpu.make_async_remote_copy(src_ref=slot, dst_ref=slot, send_sem=send2.at[k], recv_sem=recv2.at[k],
                                         device_id=sib, device_id_type=MESH).wait_recv()
        for cp in first + passed:
            cp.wait_send()

    return pl.pallas_call(
        body, name="gather_weights", in_specs=[ANY], out_specs=ANY,
        out_shape=jax.ShapeDtypeStruct((4, R, D_MODEL), wb.dtype),
        scratch_shapes=[pltpu.SemaphoreType.DMA((3,)), pltpu.SemaphoreType.DMA((3,)),
                        pltpu.SemaphoreType.DMA((3,)), pltpu.SemaphoreType.DMA((3,))],
    )(wb)


def _swap_halves(g):
    H = g.shape[1] // 2

    def body(g_ref, out_ref, send, recv):
        x, y, c, _ = _place()
        sib = (x, y, 1 - c)
        cps = [pltpu.make_async_remote_copy(
            src_ref=g_ref.at[s, pl.ds((1 - c) * H, H)], dst_ref=out_ref.at[s], send_sem=send.at[s],
            recv_sem=recv.at[s], device_id=sib, device_id_type=MESH) for s in range(4)]
        for cp in cps:
            cp.start()
        for cp in cps:
            cp.wait()

    return pl.pallas_call(
        body, name="swap_halves", in_specs=[ANY], out_specs=ANY,
        out_shape=jax.ShapeDtypeStruct((4, H, D_MODEL), g.dtype),
        scratch_shapes=[pltpu.SemaphoreType.DMA((4,)), pltpu.SemaphoreType.DMA((4,))],
    )(g)


def _scatter_partials(pb):
    def body(p_ref, out_ref, send, recv):
        x, y, c, chips = _place()
        me = 2 * x + y
        cps = [pltpu.make_async_remote_copy(
            src_ref=p_ref.at[2 * cx + cy], dst_ref=out_ref.at[me], send_sem=send.at[k], recv_sem=recv.at[k],
            device_id=(cx, cy, c), device_id_type=MESH) for k, (cx, cy) in enumerate(chips)]
        for cp in cps:
            cp.start()
        for k, (cx, cy) in enumerate(chips):
            slot = out_ref.at[2 * cx + cy]
            pltpu.make_async_remote_copy(src_ref=slot, dst_ref=slot, send_sem=send.at[k], recv_sem=recv.at[k],
                                         device_id=(cx, cy, c), device_id_type=MESH).wait_recv()
        for cp in cps:
            cp.wait_send()

    return pl.pallas_call(
        body, name="scatter_partials", in_specs=[ANY], out_specs=ANY,
        out_shape=jax.ShapeDtypeStruct(pb.shape, pb.dtype),
        scratch_shapes=[pltpu.SemaphoreType.DMA((3,)), pltpu.SemaphoreType.DMA((3,))],
    )(pb)


def _sibling_half(f, name):
    def body(f_ref, out_ref, send, recv):
        x, y, c, _ = _place()
        cp = pltpu.make_async_remote_copy(src_ref=f_ref, dst_ref=out_ref, send_sem=send, recv_sem=recv,
                                          device_id=(x, y, 1 - c), device_id_type=MESH)
        cp.start()
        cp.wait()

    return pl.pallas_call(
        body, name=name, in_specs=[ANY], out_specs=ANY,
        out_shape=jax.ShapeDtypeStruct(f.shape, f.dtype),
        scratch_shapes=[pltpu.SemaphoreType.DMA, pltpu.SemaphoreType.DMA],
    )(f)


def _gather_tiles(tile, name):
    m_per = tile.shape[0]

    def body(x_ref, out_ref, send_sems, recv_sems, local_sem):
        x, y, c, chips = _place()
        me, sibling = (x, y, c), (x, y, 1 - c)

        def rows(px, py, pc):
            return out_ref.at[pl.ds((4 * px + 2 * py + pc) * m_per, m_per), :]

        def copy(k, block, to, src=None):
            return pltpu.make_async_remote_copy(
                src_ref=rows(*block) if src is None else src, dst_ref=rows(*block),
                send_sem=send_sems.at[k], recv_sem=recv_sems.at[k], device_id=to, device_id_type=MESH)

        mine = pltpu.make_async_copy(x_ref, rows(*me), local_sem)
        mine.start()
        first = [copy(0, me, sibling, src=x_ref)]
        first += [copy(1 + j, me, (*chip, c), src=x_ref) for j, chip in enumerate(chips)]
        for cp in first:
            cp.start()
        passed = [copy(4 + j, (*chip, c), sibling) for j, chip in enumerate(chips)]
        for j, chip in enumerate(chips):
            copy(1 + j, (*chip, c), me).wait_recv()
            passed[j].start()
        copy(0, sibling, me).wait_recv()
        for j, chip in enumerate(chips):
            copy(4 + j, (*chip, 1 - c), me).wait_recv()
        for cp in first + passed:
            cp.wait_send()
        mine.wait()

    return pl.pallas_call(
        body, name=name, out_shape=jax.ShapeDtypeStruct((8 * m_per, D_MODEL), tile.dtype),
        in_specs=[pl.BlockSpec(memory_space=pltpu.VMEM)], out_specs=pl.BlockSpec(memory_space=pltpu.VMEM),
        scratch_shapes=[pltpu.SemaphoreType.DMA((7,)), pltpu.SemaphoreType.DMA((7,)), pltpu.SemaphoreType.DMA],
    )(tile)


def _pad_rows(a, rows):
    return jnp.pad(a, ((0, rows - a.shape[0]), (0, 0)))


def _pad_row(v):
    v = v.reshape(1, -1)
    return jnp.pad(v, ((0, 0), (0, D_MODEL - v.shape[1])))


def _pack_shard(fwi, fwo, dwi, dwo, pwu, pwg):
    parts = [_pad_rows(fwi.reshape(1028, D_MODEL), FLAT_A), fwo.reshape(256, D_MODEL), dwi.reshape(2560, D_MODEL),
             dwo.reshape(256, D_MODEL), pwu.reshape(128, D_MODEL), pwg.reshape(512, D_MODEL),
             jnp.zeros((FLAT_TOTAL - FLAT_USED, D_MODEL), fwi.dtype)]
    return jnp.concatenate(parts, axis=0)


def _unpack_shard(flat):
    out, r0 = {}, 0
    for name, rows, slot in FLAT_ROWS:
        out[name] = flat[r0:r0 + rows]
        r0 += slot
    return (out["fox_w_in"].reshape(1, D_MODEL, 1028), out["fox_w_out"].reshape(1, 256, D_MODEL),
            out["dil_w_in"].reshape(1, D_MODEL, 2560), out["dil_w_out"].reshape(1, 256, D_MODEL),
            out["ple_w_up"].reshape(2, PLE_DIM, 256), out["ple_w_gate"].reshape(2, 256, D_MODEL))


def _from_shard_columns(a, n):
    return a.reshape(4, D_MODEL, n).transpose(1, 0, 2).reshape(D_MODEL, 4 * n)


def _to_shard_columns(a, n):
    return a.reshape(D_MODEL, 4, n).transpose(1, 0, 2).reshape(4, n, D_MODEL)


def _unpack_a(wall_a):
    return _from_shard_columns(wall_a[:, :1028], 1028)


def _unpack_b(wall_b):
    out, r0 = {}, 0
    for name, rows, slot in FLAT_ROWS[1:]:
        out[name] = wall_b[:, r0:r0 + rows]
        r0 += slot
    pwu4 = out["ple_w_up"].reshape(4, 2, PLE_DIM, 256)
    pwg4 = out["ple_w_gate"].reshape(4, 2, 256, D_MODEL)
    pwu = [pwu4[:, i].transpose(1, 0, 2).reshape(PLE_DIM, D_MODEL) for i in range(2)]
    pwg = [pwg4[:, i].reshape(D_MODEL, D_MODEL) for i in range(2)]
    return (out["fox_w_out"].reshape(D_MODEL, D_MODEL), _from_shard_columns(out["dil_w_in"], 2560),
            out["dil_w_out"].reshape(D_MODEL, D_MODEL), pwu, pwg)


def _gathered_b_weights(wall_b, w_b, chip):
    return _unpack_b(lax.dynamic_update_slice(wall_b, w_b[None], (chip, 0, 0)))


def _pack_grads_a(gfwi):
    return jnp.pad(_to_shard_columns(gfwi, 1028), ((0, 0), (0, FLAT_A - 1028), (0, 0)))


def _pack_grads_b(gfwo, gdwi, gdwo, gpwu, gpwg):
    up = jnp.stack([a.reshape(PLE_DIM, 4, 256).transpose(1, 0, 2) for a in gpwu], axis=1)
    gate = jnp.stack([a.reshape(4, 256, D_MODEL) for a in gpwg], axis=1)
    parts = [gfwo.reshape(4, 256, D_MODEL), _to_shard_columns(gdwi, 2560), gdwo.reshape(4, 256, D_MODEL),
             up.reshape(4, 128, D_MODEL), gate.reshape(4, 512, D_MODEL)]
    return jnp.concatenate(parts, axis=1).astype(BF16)


def _local_step(x, p0, p1, tgt, fox_g, dil_g, fin_g, b_f, fwi, w_b, chip):
    S = x.shape[0]
    tm = min(256, S)
    nh = FOX_HEADS // 2
    w_qkv0 = jnp.concatenate([fwi[:, :D_MODEL] * 0.125, fwi[:, D_MODEL:3 * D_MODEL]], axis=1)
    w_z0 = fwi[:, 3 * D_MODEL:4 * D_MODEL]
    w_f0 = jnp.pad(fwi[:, 4 * D_MODEL:], ((0, 0), (0, LANES - FOX_HEADS)))
    b_full = jnp.pad(b_f.reshape(1, FOX_HEADS), ((0, 0), (0, LANES - FOX_HEADS)))

    n0, = _rows(_rms_fwd, [x], [(D_MODEL, BF16)], name="norm0", tm=tm, bcast=[fox_g])
    qkv0 = _mm(n0, w_qkv0, out_dtype=BF16, name="proj_qkv0")
    z0 = _mm(n0, w_z0, name="proj_z0")
    fl0 = _mm(n0, w_f0, name="proj_f0")
    qx0, kx0 = _cumsum_rows([fl0], name="forget_cumsum", width=LANES, pre=_log_forget, bcast=[b_full],
                            post=_fox_extras, outs=[(D_MODEL, BF16), (D_MODEL, BF16)])
    o0, a0, g0, wall_b = _fox_fwd(qkv0, qx0, kx0, z0, w_b)
    fwo, dwi, dwo, pwu, pwg = _gathered_b_weights(wall_b, w_b, chip)
    w_qkv1 = dwi[:, :3 * DIL_QKV]
    w_z1 = dwi[:, 3 * DIL_QKV:]
    h1 = _mm(g0, fwo, add=x, name="out_proj0")
    u0 = _mm(p0, pwu[0], name="ple_up0")
    t0 = _mm(h1, pwg[0], name="ple_gate0")
    h2, n1 = _rows(_ple_norm, [h1, u0, t0], [(D_MODEL, F32), (D_MODEL, BF16)], name="ple_mix0_norm1", tm=tm,
                   bcast=[dil_g])

    qkv1 = _mm(n1, w_qkv1, name="proj_qkv1")
    z1 = _mm(n1, w_z1, name="proj_z1")
    o1, g1, lse1 = _dil_fwd(qkv1, z1)
    h3 = _mm(g1, dwo, add=h2, name="out_proj1")
    u1 = _mm(p1, pwu[1], name="ple_up1")
    t1 = _mm(h3, pwg[1], name="ple_gate1")

    du1, dt1, dh4, g_fin, loss = _rows(
        _ple_loss, [h3, u1, t1, tgt], [(D_MODEL, BF16), (D_MODEL, BF16), (D_MODEL, F32)], name="ple_mix1_loss_head",
        tm=tm, bcast=[fin_g], acc=[((1, D_MODEL), F32), ((1, LANES), F32)])
    g_up1 = _mm(p1, du1, ta=True, out_dtype=BF16, name="grad_ple_up1")
    g_gate1 = _mm(h3, dt1, ta=True, out_dtype=BF16, name="grad_ple_gate1")
    dh3 = _mm(dt1, pwg[1], tb=True, add=dh4, name="ple_back1")
    dg1 = _mm(dh3, dwo, tb=True, name="out_back1")
    g_dwo = _mm(g1, dh3, ta=True, out_dtype=BF16, name="grad_out1")
    do1, dz1, ld1 = _rows(_gate_bwd, [dg1, o1, z1, lse1], [(D_MODEL, F32), (D_MODEL, BF16), (D_MODEL, F32)],
                          name="gate_bwd1", tm=tm)
    dq1, dk1, dv1 = (lax.empty((S, DIL_QKV), BF16) for _ in range(3))
    for g in range(DIL_GROUPS):
        dq1 = _dil_bwd_q(qkv1, do1, ld1, dq1, g)
        dk1, dv1 = _dil_bwd_kv(qkv1, do1, ld1, dk1, dv1, g)
    g_dwi = _mm_tn_parts(n1, [dq1, dk1, dv1, dz1], name="grad_in1", out_dtype=BF16)
    dn1 = _mm_nt_parts([dq1, dk1, dv1, dz1], dwi, name="in_back1")
    dh2, du0, dt0, g_dil = _rows(_norm_ple_bwd, [dn1, h2, dh3, u0, t0], [(D_MODEL, F32), (D_MODEL, BF16), (D_MODEL, BF16)],
                                 name="norm_bwd1_ple_bwd0", tm=tm, bcast=[dil_g], acc=[((1, D_MODEL), F32)])
    g_up0 = _mm(p0, du0, ta=True, out_dtype=BF16, name="grad_ple_up0")
    g_gate0 = _mm(h1, dt0, ta=True, out_dtype=BF16, name="grad_ple_gate0")
    dh1 = _mm(dt0, pwg[0], tb=True, add=dh2, name="ple_back0")
    dg0 = _mm(dh1, fwo, tb=True, name="out_back0")
    g_fwo = _mm(g0, dh1, ta=True, out_dtype=BF16, name="grad_out0")
    do0, dz0, qxa0, dx0 = _rows(_gate_bwd_fox, [dg0, o0, z0, a0], [(D_MODEL, BF16)] * 4, name="gate_bwd0", tm=tm)
    g_b = _pack_grads_b(g_fwo, g_dwi, g_dwo, [g_up0, g_up1], [g_gate0, g_gate1])
    dq0, drx, dk0, dv0, dkx, got_b = _fox_bwd(qkv0, do0, qxa0, dx0, kx0, g_b)
    dlogf, = _cumsum_rows([drx, dkx], name="forget_cumsum_bwd", width=LANES, reverse=True, pre=_forget_dc)
    df0, g_bf = _rows(_forget_bwd, [dlogf, fl0], [(LANES, BF16)], name="forget_bwd", tm=tm, bcast=[b_full],
                      acc=[((1, LANES), F32)])
    g_fwi_main = _mm_tn_parts(n0, [dq0, dk0, dv0, dz0], name="grad_in0")
    g_fwi_f = _mm(n0, df0, ta=True, name="grad_in0_forget")
    dn0 = _mm_nt_parts([dq0, dk0, dv0, dz0], fwi[:, :4 * D_MODEL], name="in_back0")
    dn0 = _mm(df0, w_f0, tb=True, add=dn0, name="in_back0_forget")
    dx, g_fox = _rows(_rms_bwd, [dn0, x, dh1], [(D_MODEL, F32)], name="norm_bwd0", tm=tm, bcast=[fox_g],
                      acc=[((1, D_MODEL), F32)])
    g_fwi = jnp.concatenate([g_fwi_main, g_fwi_f[:, :FOX_HEADS]], axis=1)
    return loss, dx, g_fwi, g_b, got_b, (g_fox, _pad_row(g_bf[:, :FOX_HEADS]), g_fin, g_dil)


def kernel(x, p, fox_norm, fox_w_in, fox_b_f, fox_w_out, dil_norm, dil_w_in, dil_w_out, ple_w_up, ple_w_gate, final_norm, loss_target, m_fox_norm, m_fox_w_in, m_fox_b_f, m_fox_w_out, m_dil_norm, m_dil_w_in, m_dil_w_out, m_ple_w_up, m_ple_w_gate, m_final_norm, v_fox_norm, v_fox_w_in, v_fox_b_f, v_fox_w_out, v_dil_norm, v_dil_w_in, v_dil_w_out, v_ple_w_up, v_ple_w_gate, v_final_norm):
    xi, yi, ci = lax.axis_index("x"), lax.axis_index("y"), lax.axis_index("c")
    chip = 2 * xi + yi

    w_flat = _pack_shard(*(w.astype(BF16) for w in (fox_w_in, fox_w_out, dil_w_in, dil_w_out, ple_w_up, ple_w_gate)))
    w_a, w_b = w_flat[:FLAT_A], w_flat[FLAT_A:FLAT_USED]
    fwi = _unpack_a(lax.dynamic_update_slice(_gather_weights(w_a), w_a[None], (chip, 0, 0)))
    gains = _gather_tiles(_pad_rows(_pad_row(dil_norm), 8), "gather_gain").reshape(4, 2, 8, D_MODEL)
    dil_g = gains[:, 0, 0, :256].reshape(1, D_MODEL)

    loss_part, dx, g_fwi, g_b, got_b, small = _local_step(
        x[0], p[0, 0], p[1, 0], loss_target[0], fox_norm.reshape(1, D_MODEL), dil_g,
        final_norm.reshape(1, D_MODEL), fox_b_f, fwi, w_b, chip)
    g_fox, g_bf, g_fin, g_dil = small
    sum4 = lambda a, b, c, d: ((a.astype(F32) + b.astype(F32)) + c.astype(F32)) + d.astype(F32)

    got_b = lax.dynamic_update_slice(got_b, lax.dynamic_slice_in_dim(g_b, chip, 1, axis=0), (chip, 0, 0))
    sum_b, = _rows(sum4, [got_b[s] for s in range(4)], [(D_MODEL, F32)], name="chip_sum_b", tm=FLAT_B // 8)
    g_flat_b, = _rows(lambda a, b: a + b, [sum_b, _sibling_half(sum_b, "sibling_sum_b")], [(D_MODEL, F32)],
                      name="pair_sum_b", tm=FLAT_B // 8)

    half = FLAT_A // 2
    g_a = _pack_grads_a(g_fwi)
    theirs = _swap_halves(g_a)
    mine = lax.dynamic_slice_in_dim(g_a, ci * half, half, axis=1)
    part, = _rows(lambda a, b: a + b, [mine.reshape(4 * half, D_MODEL), theirs.reshape(4 * half, D_MODEL)],
                  [(D_MODEL, BF16)], name="pair_sum", tm=half)
    part = part.reshape(4, half, D_MODEL)
    own = lax.dynamic_slice_in_dim(part, chip, 1, axis=0)
    by_chip = lax.dynamic_update_slice(_scatter_partials(part), own, (chip, 0, 0))
    half_sum, = _rows(sum4, [by_chip[s] for s in range(4)], [(D_MODEL, F32)], name="chip_sum", tm=half)
    other_half = _sibling_half(half_sum, "sibling_half")
    g_flat_a = jnp.where(ci == 0, jnp.concatenate([half_sum, other_half], axis=0),
                         jnp.concatenate([other_half, half_sum], axis=0))
    g_flat = jnp.concatenate([g_flat_a, g_flat_b, jnp.zeros((FLAT_TOTAL - FLAT_USED, D_MODEL), F32)], axis=0)

    tile = jnp.concatenate([g_fox, g_bf, g_fin, g_dil, jnp.pad(loss_part, ((0, 0), (0, D_MODEL - LANES))),
                            jnp.zeros((3, D_MODEL), F32)], axis=0)
    tiles = _gather_tiles(tile, "gather_small")
    tot, = _rows(lambda *t: functools.reduce(lambda a, b: a + b, t), [tiles[8 * s:8 * s + 8] for s in range(8)],
                 [(D_MODEL, F32)], name="small_sum", tm=8)
    loss = tot[4, 0]
    g_small = jnp.concatenate([
        tot[0:3],
        _pad_row(lax.dynamic_slice_in_dim(tot[3], chip * 256, 256)), jnp.zeros((4, D_MODEL), F32)], axis=0)
    small_tile = lambda a, b, c, d: jnp.concatenate(
        [_pad_row(a), _pad_row(b), _pad_row(c), _pad_row(d), jnp.zeros((4, D_MODEL), F32)], axis=0)
    w_small = small_tile(fox_norm, fox_b_f, final_norm, dil_norm)
    m_small = small_tile(m_fox_norm, m_fox_b_f, m_final_norm, m_dil_norm)
    v_small = small_tile(v_fox_norm, v_fox_b_f, v_final_norm, v_dil_norm)

    d_small, nm_small, nv_small = _rows(_adamw, [w_small, g_small, m_small, v_small], [(D_MODEL, F32)] * 3,
                                        name="adamw_small", tm=8)
    grads = _unpack_shard(g_flat)
    weights = (fox_w_in, fox_w_out, dil_w_in, dil_w_out, ple_w_up, ple_w_gate)
    firsts = (m_fox_w_in, m_fox_w_out, m_dil_w_in, m_dil_w_out, m_ple_w_up, m_ple_w_gate)
    seconds = (v_fox_w_in, v_fox_w_out, v_dil_w_in, v_dil_w_out, v_ple_w_up, v_ple_w_gate)
    big = []
    for (name, _, _), w, g, m, v in zip(FLAT_ROWS, weights, grads, firsts, seconds):
        width = w.shape[-1]
        flat2 = lambda a: a.reshape(-1, width)
        res = _rows(_adamw, [flat2(w), flat2(g), flat2(m), flat2(v)], [(width, F32)] * 3, name=f"adamw_{name}",
                    tm=128 if width > D_MODEL + LANES else 256)
        big.append([r.reshape(w.shape) for r in res])

    def leaves(shards, small_rows):
        fwi_, fwo_, dwi_, dwo_, pwu_, pwg_ = shards
        return (small_rows[0:1], fwi_, small_rows[1:2, :FOX_HEADS], fwo_, small_rows[3:4, :256], dwi_, dwo_,
                pwu_, pwg_, small_rows[2])

    per_kind = lambda k: [b[k] for b in big]
    return (loss, dx[None], *leaves(grads, g_small), *leaves(per_kind(0), d_small), *leaves(per_kind(1), nm_small),
            *leaves(per_kind(2), nv_small))
```

```python
import functools

import numpy as np
import jax
import jax.numpy as jnp
from jax import lax
from jax.experimental import pallas as pl
from jax.experimental.pallas import tpu as pltpu

F32 = jnp.float32
BF16 = jnp.bfloat16

D_MODEL = 1024
PLE_DIM = 256
FOX_HEADS = 16
FOX_HEAD_DIM = 64
DIL_PATTERN = ((128, 1), (512, 4), (2048, 16))
DIL_GROUPS = 3
DIL_HEADS_PER_GROUP = 8
DIL_HEAD_DIM = 128
DIL_WINDOW_STEPS = 128
DIL_QKV = 3072
ALIBI_MAX_EXP = 8.0
RMS_EPS = 1e-6
ADAM_LR, ADAM_B1, ADAM_B2, ADAM_EPS, ADAM_WD, ADAM_STEP = 0.001, 0.9, 0.999, 1e-08, 0.01, 10

LANES = 128
VMEM_LIMIT = 56 * 1024 * 1024
MESH = pl.DeviceIdType.MESH
ANY = pl.BlockSpec(memory_space=pl.ANY)
NEG = -1e30

FLAT_ROWS = (("fox_w_in", 1028, 1056), ("fox_w_out", 256, 256), ("dil_w_in", 2560, 2560), ("dil_w_out", 256, 256),
             ("ple_w_up", 128, 128), ("ple_w_gate", 512, 512))
FLAT_USED = sum(r for _, _, r in FLAT_ROWS)
FLAT_TOTAL = 4864
FLAT_TILE = 608
FLAT_A = FLAT_ROWS[0][2]
FLAT_B = FLAT_USED - FLAT_A


def _cparams(*sem):
    return pltpu.CompilerParams(dimension_semantics=sem, vmem_limit_bytes=VMEM_LIMIT)


def _sigmoid(x):
    return 1.0 / (1.0 + jnp.exp(-x))


def _mm(a, b, *, name, ta=False, tb=False, out_dtype=F32, add=None):
    if ta:
        K, M = a.shape
    else:
        M, K = a.shape
    if tb:
        N, Kb = b.shape
    else:
        Kb, N = b.shape
    assert K == Kb, (a.shape, b.shape)
    tm, tn, tk = min(M, 1024), min(N, 1024), min(K, 1024)
    assert M % tm == 0 and N % tn == 0 and K % tk == 0, (M, N, K)
    nk = K // tk
    dn = (((0 if ta else 1,), (1 if tb else 0,)), ((), ()))

    def body(*refs):
        if add is None:
            a_ref, b_ref, o_ref, acc = refs
        else:
            a_ref, b_ref, add_ref, o_ref, acc = refs
        k = pl.program_id(2)

        @pl.when(k == 0)
        def _():
            acc[...] = jnp.zeros_like(acc)

        acc[...] += lax.dot_general(a_ref[...].astype(BF16), b_ref[...].astype(BF16), dn,
                                    preferred_element_type=F32)

        @pl.when(k == nk - 1)
        def _():
            r = acc[...]
            if add is not None:
                r = r + add_ref[...]
            o_ref[...] = r.astype(out_dtype)

    a_spec = (pl.BlockSpec((tk, tm), lambda i, j, k: (k, i)) if ta
              else pl.BlockSpec((tm, tk), lambda i, j, k: (i, k)))
    b_spec = (pl.BlockSpec((tn, tk), lambda i, j, k: (j, k)) if tb
              else pl.BlockSpec((tk, tn), lambda i, j, k: (k, j)))
    in_specs = [a_spec, b_spec]
    args = [a, b]
    if add is not None:
        in_specs.append(pl.BlockSpec((tm, tn), lambda i, j, k: (i, j)))
        args.append(add)
    return pl.pallas_call(
        body, name=name, grid=(M // tm, N // tn, nk),
        in_specs=in_specs, out_specs=pl.BlockSpec((tm, tn), lambda i, j, k: (i, j)),
        out_shape=jax.ShapeDtypeStruct((M, N), out_dtype),
        scratch_shapes=[pltpu.VMEM((tm, tn), F32)],
        compiler_params=_cparams("parallel", "parallel", "arbitrary"),
    )(*args)


def _mm_nt_parts(a_parts, b, *, name):
    M = a_parts[0].shape[0]
    N, K = b.shape
    tm, tn, tk = min(M, 1024), min(N, 1024), 1024
    nks = [a.shape[1] // tk for a in a_parts]
    offs = [sum(nks[:p]) for p in range(len(nks))]
    nk = sum(nks)
    assert nk * tk == K and M % tm == 0 and N % tn == 0, (M, N, K)
    n_parts = len(a_parts)

    def body(*refs):
        a_refs, b_ref, o_ref, acc = refs[:n_parts], refs[n_parts], refs[n_parts + 1], refs[n_parts + 2]
        k = pl.program_id(2)

        @pl.when(k == 0)
        def _():
            acc[...] = jnp.zeros_like(acc)

        for a_ref, off, n in zip(a_refs, offs, nks):
            @pl.when((k >= off) & (k < off + n))
            def _(a_ref=a_ref):
                acc[...] += lax.dot_general(a_ref[...].astype(BF16), b_ref[...].astype(BF16),
                                            (((1,), (1,)), ((), ())), preferred_element_type=F32)

        @pl.when(k == nk - 1)
        def _():
            o_ref[...] = acc[...]

    a_specs = [pl.BlockSpec((tm, tk), lambda i, j, k, off=off, n=n: (i, jnp.clip(k - off, 0, n - 1)))
               for off, n in zip(offs, nks)]
    return pl.pallas_call(
        body, name=name, grid=(M // tm, N // tn, nk),
        in_specs=a_specs + [pl.BlockSpec((tn, tk), lambda i, j, k: (j, k))],
        out_specs=pl.BlockSpec((tm, tn), lambda i, j, k: (i, j)),
        out_shape=jax.ShapeDtypeStruct((M, N), F32), scratch_shapes=[pltpu.VMEM((tm, tn), F32)],
        compiler_params=_cparams("parallel", "parallel", "arbitrary"),
    )(*a_parts, b)


def _mm_tn_parts(a, b_parts, *, name, out_dtype=F32):
    K, M = a.shape
    tm, tn, tk = min(M, 1024), 1024, min(K, 1024)
    njs = [b.shape[1] // tn for b in b_parts]
    offs = [sum(njs[:p]) for p in range(len(njs))]
    nj, nk = sum(njs), K // tk
    assert M % tm == 0 and K % tk == 0 and all(b.shape[1] % tn == 0 for b in b_parts)
    n_parts = len(b_parts)

    def body(*refs):
        a_ref, b_refs, o_ref, acc = refs[0], refs[1:1 + n_parts], refs[1 + n_parts], refs[2 + n_parts]
        j, k = pl.program_id(1), pl.program_id(2)

        @pl.when(k == 0)
        def _():
            acc[...] = jnp.zeros_like(acc)

        for b_ref, off, n in zip(b_refs, offs, njs):
            @pl.when((j >= off) & (j < off + n))
            def _(b_ref=b_ref):
                acc[...] += lax.dot_general(a_ref[...].astype(BF16), b_ref[...].astype(BF16),
                                            (((0,), (0,)), ((), ())), preferred_element_type=F32)

        @pl.when(k == nk - 1)
        def _():
            o_ref[...] = acc[...].astype(out_dtype)

    def b_spec(off, n):
        def index(i, j, k):
            mine = (j >= off) & (j < off + n)
            return jnp.where(mine, k, 0), jnp.clip(j - off, 0, n - 1)
        return pl.BlockSpec((tk, tn), index)

    return pl.pallas_call(
        body, name=name, grid=(M // tm, nj, nk),
        in_specs=[pl.BlockSpec((tk, tm), lambda i, j, k: (k, i))] + [b_spec(off, n) for off, n in zip(offs, njs)],
        out_specs=pl.BlockSpec((tm, tn), lambda i, j, k: (i, j)),
        out_shape=jax.ShapeDtypeStruct((M, nj * tn), out_dtype), scratch_shapes=[pltpu.VMEM((tm, tn), F32)],
        compiler_params=_cparams("parallel", "parallel", "arbitrary"),
    )(a, *b_parts)


def _rows(fn, ins, outs, *, name, tm, bcast=(), acc=()):
    R = ins[0].shape[0]
    assert R % tm == 0, (R, tm)
    n_in, n_b, n_out, n_acc = len(ins), len(bcast), len(outs), len(acc)

    def body(*refs):
        in_refs = refs[:n_in + n_b]
        out_refs = refs[n_in + n_b:n_in + n_b + n_out]
        acc_refs = refs[n_in + n_b + n_out:]
        res = fn(*[r[...] for r in in_refs])
        if not isinstance(res, (tuple, list)):
            res = (res,)
        for r, v in zip(out_refs, res[:n_out]):
            r[...] = v.astype(r.dtype)
        first = pl.program_id(0) == 0
        for r, v in zip(acc_refs, res[n_out:]):
            @pl.when(first)
            def _(r=r, v=v):
                r[...] = v.astype(r.dtype)

            @pl.when(jnp.logical_not(first))
            def _(r=r, v=v):
                r[...] += v.astype(r.dtype)

    in_specs = [pl.BlockSpec((tm, a.shape[1]), lambda i: (i, 0)) for a in ins]
    in_specs += [pl.BlockSpec(b.shape, lambda i, nd=b.ndim: (0,) * nd) for b in bcast]
    out_specs = [pl.BlockSpec((tm, c), lambda i: (i, 0)) for c, _ in outs]
    out_specs += [pl.BlockSpec(s, lambda i, nd=len(s): (0,) * nd) for s, _ in acc]
    out_shape = [jax.ShapeDtypeStruct((R, c), dt) for c, dt in outs]
    out_shape += [jax.ShapeDtypeStruct(s, dt) for s, dt in acc]
    res = pl.pallas_call(
        body, name=name, grid=(R // tm,), in_specs=in_specs, out_specs=out_specs, out_shape=out_shape,
        compiler_params=_cparams("arbitrary" if acc else "parallel"),
    )(*ins, *bcast)
    return res


def _exact_dot(sel, x, left):
    hi = x.astype(BF16)
    r1 = x - hi.astype(F32)
    mid = r1.astype(BF16)
    lo = (r1 - mid.astype(F32)).astype(BF16)
    dot = (lambda p: jnp.dot(sel, p, preferred_element_type=F32)) if left else (
        lambda p: jnp.dot(p, sel, preferred_element_type=F32))
    return dot(hi) + dot(mid) + dot(lo)


def _cumsum_rows(ins, *, name, width, reverse=False, pre=None, bcast=(), post=None, outs=None):
    S = ins[0].shape[0]
    outs = outs or [(width, F32)]
    out_dtypes = [dt for _, dt in outs]
    tb = 256
    nb = S // tb
    assert S % tb == 0
    n_in = len(ins) + len(bcast)
    n_out = len(out_dtypes)

    def body(*refs):
        in_refs, o_refs, carry = refs[:n_in], refs[n_in:n_in + n_out], refs[n_in + n_out]

        @pl.when(pl.program_id(0) == 0)
        def _():
            carry[...] = jnp.zeros_like(carry)

        xv = in_refs[0][...] if pre is None else pre(*[r[...] for r in in_refs])
        r_ = lax.broadcasted_iota(jnp.int32, (tb, tb), 0)
        c_ = lax.broadcasted_iota(jnp.int32, (tb, tb), 1)
        tri = jnp.where((c_ >= r_) if reverse else (c_ <= r_), 1.0, 0.0).astype(BF16)
        cs = _exact_dot(tri, xv, left=True) + carry[...]
        outs = (cs,) if post is None else post(cs)
        for o_ref, v in zip(o_refs, outs):
            o_ref[...] = v.astype(o_ref.dtype)
        carry[...] = cs[0:1, :] if reverse else cs[tb - 1:tb, :]

    blk = (lambda i: (nb - 1 - i, 0)) if reverse else (lambda i: (i, 0))
    in_specs = [pl.BlockSpec((tb, a.shape[1]), blk) for a in ins]
    in_specs += [pl.BlockSpec(b.shape, lambda i, nd=b.ndim: (0,) * nd) for b in bcast]
    return pl.pallas_call(
        body, name=name, grid=(nb,), in_specs=in_specs, out_specs=[pl.BlockSpec((tb, c), blk) for c, _ in outs],
        out_shape=[jax.ShapeDtypeStruct((S, c), dt) for c, dt in outs], scratch_shapes=[pltpu.VMEM((1, width), F32)],
        compiler_params=_cparams("arbitrary"),
    )(*ins, *bcast)


def _rms_fwd(x, g):
    r = lax.rsqrt(jnp.mean(x * x, axis=1, keepdims=True) + RMS_EPS)
    return (x * r) * g


def _rms_bwd(dn, x, dres, g):
    r = lax.rsqrt(jnp.mean(x * x, axis=1, keepdims=True) + RMS_EPS)
    xh = x * r
    w = dn * g
    dx = r * (w - xh * jnp.mean(w * xh, axis=1, keepdims=True))
    return dres + dx, jnp.sum(dn * xh, axis=0, keepdims=True)


def _final_stage(h, tgt, g):
    r = lax.rsqrt(jnp.mean(h * h, axis=1, keepdims=True) + RMS_EPS)
    xh = h * r
    diff = xh * g - tgt
    loss = 0.5 * jnp.sum(jnp.mean(diff * diff, axis=1, keepdims=True), axis=0, keepdims=True)
    dy = diff * (1.0 / D_MODEL)
    w = dy * g
    dh = r * (w - xh * jnp.mean(w * xh, axis=1, keepdims=True))
    return dh, jnp.sum(dy * xh, axis=0, keepdims=True), jnp.broadcast_to(loss, (1, LANES))


def _ple_fwd(h, u, t):
    return h + u * _sigmoid(t)


def _ple_bwd(dh, u, t):
    s = _sigmoid(t)
    return dh * s, dh * u * s * (1.0 - s)


def _ple_norm(h, u, t, g):
    h2 = _ple_fwd(h, u, t)
    return h2, _rms_fwd(h2, g)


def _ple_loss(h, u, t, tgt, g):
    dh, g_gain, loss = _final_stage(_ple_fwd(h, u, t), tgt, g)
    du, dt = _ple_bwd(dh, u, t)
    return du, dt, dh, g_gain, loss


def _norm_ple_bwd(dn, x, dres, u, t, g):
    dh, g_gain = _rms_bwd(dn, x, dres, g)
    du, dt = _ple_bwd(dh, u, t)
    return dh, du, dt, g_gain


def _gate_fwd(o, z):
    return o * (z * _sigmoid(z))


def _head_sums(prod):
    tm, width = prod.shape
    cols = [jnp.broadcast_to(jnp.sum(prod[:, b * LANES:(b + 1) * LANES], axis=1, keepdims=True), (tm, LANES))
            for b in range(width // LANES)]
    return jnp.concatenate(cols, axis=1)


def _gate_bwd(dg, o, z, lse):
    s = _sigmoid(z)
    do = dg * (z * s)
    dz = dg * o * (s * (1.0 + z * (1.0 - s)))
    lane = lax.broadcasted_iota(jnp.int32, do.shape, 1) % LANES
    return do, dz, jnp.where(lane < LANES // 2, lse, _head_sums(do * o))


def _log_forget(fl, b):
    u = fl + b
    return jnp.minimum(u, 0.0) - jnp.log(1.0 + jnp.exp(-jnp.abs(u)))


def _adamw(w, g, m, v):
    m = ADAM_B1 * m + (1.0 - ADAM_B1) * g
    v = ADAM_B2 * v + (1.0 - ADAM_B2) * (g * g)
    m_hat = m / (1.0 - ADAM_B1 ** ADAM_STEP)
    v_hat = v / (1.0 - ADAM_B2 ** ADAM_STEP)
    delta = -ADAM_LR * (m_hat / (jnp.sqrt(v_hat) + ADAM_EPS) + ADAM_WD * w)
    return delta, m, v


FOX_TQ = 1024


def _split3(x):
    p1 = x.astype(BF16).astype(F32)
    r = x - p1
    p2 = r.astype(BF16).astype(F32)
    return p1, p2, r - p2


def _lane_in_head(shape):
    return lax.broadcasted_iota(jnp.int32, shape, 1) % FOX_HEAD_DIM


def _query_extras(x):
    lm = _lane_in_head(x.shape)
    p1, p2, p3 = _split3(x)
    return jnp.where(lm == 0, p1, jnp.where(lm == 1, p2, jnp.where(lm == 2, p3, jnp.where(lm < 6, 1.0, 0.0))))


def _key_extras(c):
    lm = _lane_in_head(c.shape)
    p1, p2, p3 = _split3(c)
    return jnp.where(lm < 3, 1.0, jnp.where(lm == 3, -p1, jnp.where(lm == 4, -p2, jnp.where(lm == 5, -p3, 0.0))))


def _head_lane_base(head):
    return LANES * (head // 2) + FOX_HEAD_DIM * (1 - head % 2)


def _fox_extras(c):
    head = lax.broadcasted_iota(jnp.int32, (LANES, D_MODEL), 0)
    lane = lax.broadcasted_iota(jnp.int32, (LANES, D_MODEL), 1)
    owner = 2 * (lane // LANES) + 1 - (lane % LANES) // FOX_HEAD_DIM
    wide = _exact_dot(jnp.where(head == owner, 1.0, 0.0).astype(BF16), c, left=False)
    return _query_extras(wide), _key_extras(wide)


def _swapped_head_sums(prod):
    tm, width = prod.shape
    lane = lax.broadcasted_iota(jnp.int32, (tm, LANES), 1)
    low = lane < FOX_HEAD_DIM
    cols = []
    for b in range(width // LANES):
        blk = prod[:, b * LANES:(b + 1) * LANES]
        sa = jnp.sum(jnp.where(low, blk, 0.0), axis=1, keepdims=True)
        sb = jnp.sum(jnp.where(low, 0.0, blk), axis=1, keepdims=True)
        cols.append(jnp.where(low, sb, sa))
    return jnp.concatenate(cols, axis=1)


def _gate_bwd_fox(dg, o, z, a):
    s = _sigmoid(z)
    do = dg * (z * s)
    dz = dg * o * (s * (1.0 + z * (1.0 - s)))
    lm = _lane_in_head(do.shape)
    d1, d2, d3 = _split3(-_swapped_head_sums(do * o))
    dx = jnp.where(lm == 0, d1, jnp.where(lm == 1, d2, jnp.where(lm == 2, d3, 0.0)))
    return do, dz, _query_extras(a), dx


def _forget_dc(drx, dkx):
    lane = lax.broadcasted_iota(jnp.int32, (D_MODEL, LANES), 0)
    head = lax.broadcasted_iota(jnp.int32, (D_MODEL, LANES), 1)
    base = _head_lane_base(head)
    pick = lambda l: jnp.where((lane == base + l) & (head < FOX_HEADS), 1.0, 0.0).astype(BF16)
    return _exact_dot(pick(0), drx, left=False) - _exact_dot(pick(3), dkx, left=False)


def _forget_bwd(dl, fl, b):
    du = dl * (1.0 / (1.0 + jnp.exp(fl + b)))
    return du, jnp.sum(du, axis=0, keepdims=True)


def _chip_exchange(src_of, out_ref, send, recv):
    x, y, c = lax.axis_index("x"), lax.axis_index("y"), lax.axis_index("c")
    chips = [(1 - x, y), (x, 1 - y), (1 - x, 1 - y)]
    me = 2 * x + y
    sends = [pltpu.make_async_remote_copy(src_ref=src_of(cx, cy), dst_ref=out_ref.at[me], send_sem=send.at[k],
                                          recv_sem=recv.at[k], device_id=(cx, cy, c), device_id_type=MESH)
             for k, (cx, cy) in enumerate(chips)]

    def start():
        for cp in sends:
            cp.start()

    def wait():
        for k, (cx, cy) in enumerate(chips):
            slot = out_ref.at[2 * cx + cy]
            pltpu.make_async_remote_copy(src_ref=slot, dst_ref=slot, send_sem=send.at[k], recv_sem=recv.at[k],
                                         device_id=(cx, cy, c), device_id_type=MESH).wait_recv()
        for cp in sends:
            cp.wait_send()

    return start, wait


def _fox_fwd(qkv, qx, kx, z, w_b):
    S = qkv.shape[0]
    tq = min(FOX_TQ, S)
    nq = S // tq
    nt = (((1,), (1,)), ((), ()))

    def body(q_ref, qx_ref, k_ref, v_ref, kx_ref, z_ref, wb_ref, o_ref, a_ref, g_ref, wall_ref, send, recv):
        i = pl.program_id(1)
        start, wait = _chip_exchange(lambda cx, cy: wb_ref, wall_ref, send, recv)
        pl.when((pl.program_id(0) == 0) & (i == 0))(start)
        low = lax.broadcasted_iota(jnp.int32, (tq, LANES), 1) < FOX_HEAD_DIM
        row = lax.broadcasted_iota(jnp.int32, (tq, tq), 0)
        col = lax.broadcasted_iota(jnp.int32, (tq, tq), 1)
        q2, x2 = q_ref[...], qx_ref[...]
        qa = (jnp.where(low, q2, x2), jnp.where(low, x2, q2))

        def step(kb, carry, diag):
            start = pl.multiple_of(kb * tq, tq)
            k2 = k_ref[pl.ds(start, tq), :]
            v2 = v_ref[pl.ds(start, tq), :]
            y2 = kx_ref[pl.ds(start, tq), :]
            one = jnp.ones_like(v2)
            ka = (jnp.where(low, k2, y2), jnp.where(low, y2, k2))
            va = (jnp.where(low, v2, one), jnp.where(low, one, v2))
            out = []
            for hh in range(2):
                m, acc = carry[hh]
                s = lax.dot_general(qa[hh], ka[hh], nt, preferred_element_type=F32)
                if diag:
                    s = jnp.where(col <= row, s, NEG)
                m_new = jnp.maximum(m, jnp.max(s, axis=1, keepdims=True))
                pr = jnp.exp(s - m_new)
                acc = jnp.exp(m - m_new) * acc + jnp.dot(pr.astype(BF16), va[hh], preferred_element_type=F32)
                out.append((m_new, acc))
            return tuple(out)

        init = ((jnp.full((tq, 1), NEG, F32), jnp.zeros((tq, LANES), F32)),) * 2
        carry = lax.fori_loop(0, i, functools.partial(step, diag=False), init)
        (m_a, acc_a), (m_b, acc_b) = step(i, carry, True)
        l_a, l_b = acc_a[:, FOX_HEAD_DIM:FOX_HEAD_DIM + 1], acc_b[:, 0:1]
        xf = x2.astype(F32)
        c_a = xf[:, 64:65] + xf[:, 65:66] + xf[:, 66:67]
        c_b = xf[:, 0:1] + xf[:, 1:2] + xf[:, 2:3]
        o = jnp.where(low, acc_a / l_a, acc_b / l_b)
        o_ref[...] = o
        g_ref[...] = _gate_fwd(o, z_ref[...]).astype(BF16)
        a_ref[...] = jnp.where(low, jnp.broadcast_to(c_b - (m_b + jnp.log(l_b)), (tq, LANES)),
                               jnp.broadcast_to(c_a - (m_a + jnp.log(l_a)), (tq, LANES)))
        pl.when((pl.program_id(0) == FOX_HEADS // 2 - 1) & (i == nq - 1))(wait)

    blk = lambda cb: pl.BlockSpec((tq, LANES), lambda h, i, cb=cb: (i, cb + h))
    res = lambda cb: pl.BlockSpec((S, LANES), lambda h, i, cb=cb: (0, cb + h))
    return pl.pallas_call(
        body, name="fox_attn_fwd", grid=(FOX_HEADS // 2, nq),
        in_specs=[blk(0), blk(0), res(8), res(16), res(0), blk(0), ANY],
        out_specs=[blk(0), blk(0), blk(0), ANY],
        out_shape=[jax.ShapeDtypeStruct((S, D_MODEL), F32), jax.ShapeDtypeStruct((S, D_MODEL), F32),
                   jax.ShapeDtypeStruct((S, D_MODEL), BF16), jax.ShapeDtypeStruct((4,) + w_b.shape, w_b.dtype)],
        scratch_shapes=[pltpu.SemaphoreType.DMA((3,)), pltpu.SemaphoreType.DMA((3,))],
        compiler_params=_cparams("arbitrary", "arbitrary"),
    )(qkv, qx, qkv, qkv, kx, z, w_b)


def _fox_bwd(qkv, do, qxa, dx, kx, g_b):
    S = qkv.shape[0]
    tq = min(FOX_TQ, S)
    nq = S // tq
    nt = (((1,), (1,)), ((), ()))
    tn = (((0,), (0,)), ((), ()))

    def body(q_ref, qx_ref, do_ref, dx_ref, k_ref, v_ref, kx_ref, gb_ref, dq_ref, dr_ref, dk_ref, dv_ref, dkx_ref,
             got_ref, send, recv):
        kb = pl.program_id(1)
        start, wait = _chip_exchange(lambda cx, cy: gb_ref.at[2 * cx + cy], got_ref, send, recv)
        pl.when((pl.program_id(0) == 0) & (kb == 0))(start)
        low = lax.broadcasted_iota(jnp.int32, (tq, LANES), 1) < FOX_HEAD_DIM
        row = lax.broadcasted_iota(jnp.int32, (tq, tq), 0)
        col = lax.broadcasted_iota(jnp.int32, (tq, tq), 1)

        @pl.when(kb == 0)
        def _():
            dq_ref[...] = jnp.zeros_like(dq_ref)
            dr_ref[...] = jnp.zeros_like(dr_ref)

        k2, v2, y2 = k_ref[...], v_ref[...], kx_ref[...]
        one = jnp.ones_like(v2)
        ka = (jnp.where(low, k2, y2), jnp.where(low, y2, k2))
        va = (jnp.where(low, v2, one), jnp.where(low, one, v2))

        def step(qb, carry, diag):
            start = pl.multiple_of(qb * tq, tq)
            q2 = q_ref[pl.ds(start, tq), :]
            x2 = qx_ref[pl.ds(start, tq), :]
            d2 = do_ref[pl.ds(start, tq), :]
            e2 = dx_ref[pl.ds(start, tq), :]
            qa = (jnp.where(low, q2, x2), jnp.where(low, x2, q2))
            da = (jnp.where(low, d2, e2), jnp.where(low, e2, d2))
            new, res = [], []
            for hh in range(2):
                dk, dv = carry[hh]
                s = lax.dot_general(qa[hh], ka[hh], nt, preferred_element_type=F32)
                if diag:
                    s = jnp.where(col <= row, s, NEG)
                pr = jnp.exp(s)
                ds = pr * lax.dot_general(da[hh], va[hh], nt, preferred_element_type=F32)
                prb, dsb = pr.astype(BF16), ds.astype(BF16)
                dv = dv + lax.dot_general(prb, da[hh], tn, preferred_element_type=F32)
                dk = dk + lax.dot_general(dsb, qa[hh], tn, preferred_element_type=F32)
                res.append(jnp.dot(dsb, ka[hh], preferred_element_type=F32))
                new.append((dk, dv))
            dq_ref[pl.ds(start, tq), :] += jnp.where(low, res[0], res[1])
            dr_ref[pl.ds(start, tq), :] += jnp.where(low, res[1], res[0])
            return tuple(new)

        init = ((jnp.zeros((tq, LANES), F32), jnp.zeros((tq, LANES), F32)),) * 2
        carry = step(kb, init, True)
        (dk_a, dv_a), (dk_b, dv_b) = lax.fori_loop(kb + 1, nq, functools.partial(step, diag=False), carry)
        dk_ref[...] = jnp.where(low, dk_a, dk_b).astype(BF16)
        dv_ref[...] = jnp.where(low, dv_a, dv_b).astype(BF16)
        dkx_ref[...] = jnp.where(low, dk_b, dk_a)

        @pl.when(kb == nq - 1)
        def _():
            dq_ref[...] = dq_ref[...] * (FOX_HEAD_DIM ** -0.5)

        pl.when((pl.program_id(0) == FOX_HEADS // 2 - 1) & (kb == nq - 1))(wait)

    res = lambda cb: pl.BlockSpec((S, LANES), lambda h, k, cb=cb: (0, cb + h))
    blk = lambda cb: pl.BlockSpec((tq, LANES), lambda h, k, cb=cb: (k, cb + h))
    f32, b16 = jax.ShapeDtypeStruct((S, D_MODEL), F32), jax.ShapeDtypeStruct((S, D_MODEL), BF16)
    return pl.pallas_call(
        body, name="fox_attn_bwd", grid=(FOX_HEADS // 2, nq),
        in_specs=[res(0), res(0), res(0), res(0), blk(8), blk(16), blk(0), ANY],
        out_specs=[res(0), res(0), blk(0), blk(0), blk(0), ANY],
        out_shape=[f32, f32, b16, b16, f32, jax.ShapeDtypeStruct(g_b.shape, g_b.dtype)],
        scratch_shapes=[pltpu.SemaphoreType.DMA((3,)), pltpu.SemaphoreType.DMA((3,))],
        compiler_params=_cparams("arbitrary", "arbitrary"),
    )(qkv, qxa, do, dx, qkv, qkv, kx, g_b)


def _alibi_slopes():
    n = DIL_GROUPS * DIL_HEADS_PER_GROUP
    s = np.float32(2.0) ** (np.float32(-ALIBI_MAX_EXP) * np.arange(1, n + 1, dtype=np.float32) / np.float32(n))
    return s.astype(np.float32).reshape(DIL_GROUPS, DIL_HEADS_PER_GROUP)


W = DIL_WINDOW_STEPS
DIL_SCALE = DIL_HEAD_DIM ** -0.5


DIL_ROWS = 2048
NT = (((1,), (1,)), ((), ()))
TN = (((0,), (0,)), ((), ()))
QKV_BLOCKS = DIL_QKV // LANES


def _slope_table():
    t = np.zeros((DIL_HEADS_PER_GROUP, 8, LANES), np.float32)
    t[:, :DIL_GROUPS, :] = _alibi_slopes().T[:, :, None]
    return jnp.asarray(t)


def _phase_rows(start, d):
    return pl.ds(start, W, stride=d) if d > 1 else pl.ds(start, W)


def _for_phases(d, unit):
    if d == 1:
        unit(0)
    else:
        lax.fori_loop(0, d, lambda r, c: (unit(r), c)[1], 0)


def _window_geometry(shape, q_axis_offset):
    i_ = lax.broadcasted_iota(jnp.int32, shape, 0)
    j_ = lax.broadcasted_iota(jnp.int32, shape, 1)
    dist = q_axis_offset + i_ - j_
    return i_, j_, dist, (dist >= 0) & (dist <= W)


def _dil_fwd(qkv, z):
    S = qkv.shape[0]
    rb = min(DIL_ROWS, S)
    nb = S // rb
    spans = [W * d for _, d in DIL_PATTERN]

    def body(*refs):
        slope_ref = refs[0]
        q, kc, kp, vc, vp = refs[1:4], refs[4:7], refs[7:10], refs[10:13], refs[13:16]
        z_ref, o_ref, g_ref, l_ref = refs[16:20]
        og, lg = refs[20:23], refs[23:26]
        i = pl.program_id(1)
        _, j_, dist, inwin = _window_geometry((W, 2 * W), W)
        distf = dist.astype(F32)
        for g, (_, d) in enumerate(DIL_PATTERN):
            span = spans[g]
            bias = -(slope_ref[0, g:g + 1, 0:1] * float(d)) * distf
            def phase(r, g=g, d=d, span=span, bias=bias):
                kprev, vprev = kp[g][_phase_rows(r, d), :].astype(BF16), vp[g][_phase_rows(r, d), :].astype(BF16)
                for b in range(rb // span):
                    rows = _phase_rows(b * span + r, d)
                    kcur, vcur = kc[g][rows, :].astype(BF16), vc[g][rows, :].astype(BF16)
                    valid = inwin if b > 0 else inwin & ((j_ >= W) | (i > 0))
                    k2 = jnp.concatenate([kprev, kcur], axis=0)
                    v2 = jnp.concatenate([vprev, vcur], axis=0)
                    s = lax.dot_general(q[g][rows, :].astype(BF16), k2, NT, preferred_element_type=F32)
                    s = jnp.where(valid, s * DIL_SCALE + bias, NEG)
                    m = jnp.max(s, axis=1, keepdims=True)
                    pr = jnp.exp(s - m)
                    l = jnp.sum(pr, axis=1, keepdims=True)
                    og[g][rows, :] = jnp.dot(pr.astype(BF16), v2, preferred_element_type=F32) / l
                    lg[g][rows, :] = jnp.broadcast_to(m + jnp.log(l), (W, LANES))
                    kprev, vprev = kcur, vcur

            _for_phases(d, phase)

        def mix(cix, c):
            sl = pl.ds(pl.multiple_of(cix * 256, 256), 256)
            l1, l2, l3 = lg[0][sl, :], lg[1][sl, :], lg[2][sl, :]
            m = jnp.maximum(jnp.maximum(l1, l2), l3)
            e1, e2, e3 = jnp.exp(l1 - m), jnp.exp(l2 - m), jnp.exp(l3 - m)
            tot = e1 + e2 + e3
            o = (e1 * og[0][sl, :] + e2 * og[1][sl, :] + e3 * og[2][sl, :]) / tot
            o_ref[sl, :] = o
            g_ref[sl, :] = _gate_fwd(o, z_ref[sl, :]).astype(BF16)
            l_ref[sl, :] = m + jnp.log(tot)
            return c

        lax.fori_loop(0, rb // 256, mix, 0)

    cur = lambda off, g: pl.BlockSpec((rb, LANES), lambda h, i: (i, off * QKV_BLOCKS + g * DIL_HEADS_PER_GROUP + h))
    prev = lambda off, g: pl.BlockSpec(
        (spans[g], LANES),
        lambda h, i: (jnp.maximum(i * (rb // spans[g]) - 1, 0), off * QKV_BLOCKS + g * DIL_HEADS_PER_GROUP + h))
    row = pl.BlockSpec((rb, LANES), lambda h, i: (i, h))
    groups = range(DIL_GROUPS)
    in_specs = [pl.BlockSpec((1, 8, LANES), lambda h, i: (h, 0, 0))]
    in_specs += [cur(0, g) for g in groups] + [cur(1, g) for g in groups] + [prev(1, g) for g in groups]
    in_specs += [cur(2, g) for g in groups] + [prev(2, g) for g in groups] + [row]
    f32 = jax.ShapeDtypeStruct((S, D_MODEL), F32)
    return pl.pallas_call(
        body, name="dil_attn_fwd", grid=(DIL_HEADS_PER_GROUP, nb), in_specs=in_specs,
        out_specs=[row, row, row], out_shape=[f32, jax.ShapeDtypeStruct((S, D_MODEL), BF16), f32],
        scratch_shapes=[pltpu.VMEM((rb, LANES), F32)] * 6,
        compiler_params=_cparams("parallel", "parallel"),
    )(_slope_table(), *([qkv] * 15), z)


def _dil_bwd_q(qkv, do, ld, buf, g):
    S = qkv.shape[0]
    d = DIL_PATTERN[g][1]
    span = W * d
    rb = min(max(DIL_ROWS, 2 * span), S)
    nb = S // rb

    def body(slope_ref, q_ref, kc, kp, vc, vp, do_ref, ld_ref, buf_ref, dq_ref, dq_f32):
        i = pl.program_id(1)
        _, j_, dist, inwin = _window_geometry((W, 2 * W), W)
        bias = -(slope_ref[0, g:g + 1, 0:1] * float(d)) * dist.astype(F32)
        def phase(r):
            kprev, vprev = kp[_phase_rows(r, d), :].astype(BF16), vp[_phase_rows(r, d), :].astype(BF16)
            for b in range(rb // span):
                rows = _phase_rows(b * span + r, d)
                kcur, vcur = kc[rows, :].astype(BF16), vc[rows, :].astype(BF16)
                valid = inwin if b > 0 else inwin & ((j_ >= W) | (i > 0))
                k2 = jnp.concatenate([kprev, kcur], axis=0)
                v2 = jnp.concatenate([vprev, vcur], axis=0)
                ld = ld_ref[rows, :]
                s = lax.dot_general(q_ref[rows, :].astype(BF16), k2, NT, preferred_element_type=F32)
                s = jnp.where(valid, s * DIL_SCALE + bias - ld[:, 0:1], NEG)
                pr = jnp.exp(s)
                dp = lax.dot_general(do_ref[rows, :].astype(BF16), v2, NT, preferred_element_type=F32)
                ds = pr * (dp - ld[:, LANES // 2:LANES // 2 + 1])
                dq_f32[rows, :] = jnp.dot(ds.astype(BF16), k2, preferred_element_type=F32) * DIL_SCALE
                kprev, vprev = kcur, vcur

        _for_phases(d, phase)
        dq_ref[...] = dq_f32[...].astype(BF16)

    col = lambda off: off * QKV_BLOCKS + g * DIL_HEADS_PER_GROUP
    cur = lambda off: pl.BlockSpec((rb, LANES), lambda h, i: (i, col(off) + h))
    prev = lambda off: pl.BlockSpec((span, LANES), lambda h, i: (jnp.maximum(i * (rb // span) - 1, 0), col(off) + h))
    row = pl.BlockSpec((rb, LANES), lambda h, i: (i, h))
    return pl.pallas_call(
        body, name=f"dil_attn_bwd_q_g{g}", grid=(DIL_HEADS_PER_GROUP, nb),
        in_specs=[pl.BlockSpec((1, 8, LANES), lambda h, i: (h, 0, 0)), cur(0), cur(1), prev(1), cur(2), prev(2),
                  row, row, pl.BlockSpec(memory_space=pl.ANY)],
        out_specs=pl.BlockSpec((rb, LANES), lambda h, i: (i, g * DIL_HEADS_PER_GROUP + h)),
        out_shape=jax.ShapeDtypeStruct(buf.shape, buf.dtype), input_output_aliases={8: 0},
        scratch_shapes=[pltpu.VMEM((rb, LANES), F32)],
        compiler_params=_cparams("parallel", "parallel"),
    )(_slope_table(), qkv, qkv, qkv, qkv, qkv, do, ld, buf)


def _dil_bwd_kv(qkv, do, ld, bufk, bufv, g):
    S = qkv.shape[0]
    d = DIL_PATTERN[g][1]
    span = W * d
    rb = min(max(DIL_ROWS, 2 * span), S)
    nb = S // rb
    nub = rb // span

    def body(slope_ref, k_ref, v_ref, qc, qn, doc, don, ldc, ldn, bufk_ref, bufv_ref, dk_ref, dv_ref, dk_f32, dv_f32):
        i = pl.program_id(1)
        i_, _, dist, inwin = _window_geometry((2 * W, W), 0)
        bias = -(slope_ref[0, g:g + 1, 0:1] * float(d)) * dist.astype(F32)
        def phase(r):
            first = _phase_rows(r, d)
            qcur, docur, ldcur = qc[first, :].astype(BF16), doc[first, :].astype(BF16), ldc[first, :]
            for b in range(nub):
                rows = _phase_rows(b * span + r, d)
                if b < nub - 1:
                    nxt = _phase_rows((b + 1) * span + r, d)
                    qnext, donext, ldnext, valid = qc[nxt, :].astype(BF16), doc[nxt, :].astype(BF16), ldc[nxt, :], inwin
                else:
                    qnext, donext, ldnext = qn[first, :].astype(BF16), don[first, :].astype(BF16), ldn[first, :]
                    valid = inwin & ((i_ < W) | (i < nb - 1))
                q2 = jnp.concatenate([qcur, qnext], axis=0)
                do2 = jnp.concatenate([docur, donext], axis=0)
                ld2 = jnp.concatenate([ldcur, ldnext], axis=0)
                s = lax.dot_general(q2, k_ref[rows, :].astype(BF16), NT, preferred_element_type=F32)
                s = jnp.where(valid, s * DIL_SCALE + bias - ld2[:, 0:1], NEG)
                pr = jnp.exp(s)
                dp = lax.dot_general(do2, v_ref[rows, :].astype(BF16), NT, preferred_element_type=F32)
                ds = pr * (dp - ld2[:, LANES // 2:LANES // 2 + 1])
                dv_f32[rows, :] = lax.dot_general(pr.astype(BF16), do2, TN, preferred_element_type=F32)
                dk_f32[rows, :] = lax.dot_general(ds.astype(BF16), q2, TN, preferred_element_type=F32) * DIL_SCALE
                qcur, docur, ldcur = qnext, donext, ldnext

        _for_phases(d, phase)
        dk_ref[...] = dk_f32[...].astype(BF16)
        dv_ref[...] = dv_f32[...].astype(BF16)

    col = lambda off: off * QKV_BLOCKS + g * DIL_HEADS_PER_GROUP
    cur = lambda off: pl.BlockSpec((rb, LANES), lambda h, i: (i, col(off) + h))
    nxt_blk = lambda i: jnp.minimum((i + 1) * nub, S // span - 1)
    nxt = lambda off: pl.BlockSpec((span, LANES), lambda h, i: (nxt_blk(i), col(off) + h))
    row = pl.BlockSpec((rb, LANES), lambda h, i: (i, h))
    row_nxt = pl.BlockSpec((span, LANES), lambda h, i: (nxt_blk(i), h))
    any_ = pl.BlockSpec(memory_space=pl.ANY)
    out = pl.BlockSpec((rb, LANES), lambda h, i: (i, g * DIL_HEADS_PER_GROUP + h))
    return pl.pallas_call(
        body, name=f"dil_attn_bwd_kv_g{g}", grid=(DIL_HEADS_PER_GROUP, nb),
        in_specs=[pl.BlockSpec((1, 8, LANES), lambda h, i: (h, 0, 0)), cur(1), cur(2), cur(0), nxt(0),
                  row, row_nxt, row, row_nxt, any_, any_],
        out_specs=[out, out],
        out_shape=[jax.ShapeDtypeStruct(bufk.shape, bufk.dtype), jax.ShapeDtypeStruct(bufv.shape, bufv.dtype)],
        input_output_aliases={9: 0, 10: 1},
        scratch_shapes=[pltpu.VMEM((rb, LANES), F32)] * 2,
        compiler_params=_cparams("parallel", "parallel"),
    )(_slope_table(), qkv, qkv, qkv, qkv, do, do, ld, ld, bufk, bufv)


def _place():
    x, y, c = lax.axis_index("x"), lax.axis_index("y"), lax.axis_index("c")
    chips = [(1 - x, y), (x, 1 - y), (1 - x, 1 - y)]
    return x, y, c, chips


def _gather_weights(wb):
    R = wb.shape[0]
    H = R // 2

    def body(w_ref, out_ref, send1, recv1, send2, recv2):
        x, y, c, chips = _place()
        me = 2 * x + y
        sib = (x, y, 1 - c)
        half = pl.ds(c * H, H)
        first = [pltpu.make_async_remote_copy(
            src_ref=w_ref.at[half], dst_ref=out_ref.at[me, half], send_sem=send1.at[k], recv_sem=recv1.at[k],
            device_id=(*chip, c), device_id_type=MESH) for k, chip in enumerate(chips)]
        for cp in first:
            cp.start()
        passed = []
        for k, (cx, cy) in enumerate(chips):
            slot = out_ref.at[2 * cx + cy, half]
            pltpu.make_async_remote_copy(src_ref=slot, dst_ref=slot, send_sem=send1.at[k], recv_sem=recv1.at[k],
                                         device_id=(cx, cy, c), device_id_type=MESH).wait_recv()
            cp = pltpu.make_async_remote_copy(src_ref=slot, dst_ref=slot, send_sem=send2.at[k], recv_sem=recv2.at[k],
                                              device_id=sib, device_id_type=MESH)
            cp.start()
            passed.append(cp)
        for k, (cx, cy) in enumerate(chips):
            slot = out_ref.at[2 * cx + cy, pl.ds((1 - c) * H, H)]
            pltpu.make_async_remote_copy(src_ref=slot, dst_ref=slot, send_sem=send2.at[k], recv_sem=recv2.at[k],
                                         device_id=sib, device_id_type=MESH).wait_recv()
        for cp in first + passed:
            cp.wait_send()

    return pl.pallas_call(
        body, name="gather_weights", in_specs=[ANY], out_specs=ANY,
        out_shape=jax.ShapeDtypeStruct((4, R, D_MODEL), wb.dtype),
        scratch_shapes=[pltpu.SemaphoreType.DMA((3,)), pltpu.SemaphoreType.DMA((3,)),
                        pltpu.SemaphoreType.DMA((3,)), pltpu.SemaphoreType.DMA((3,))],
    )(wb)


def _swap_halves(g):
    H = g.shape[1] // 2

    def body(g_ref, out_ref, send, recv):
        x, y, c, _ = _place()
        sib = (x, y, 1 - c)
        cps = [pltpu.make_async_remote_copy(
            src_ref=g_ref.at[s, pl.ds((1 - c) * H, H)], dst_ref=out_ref.at[s], send_sem=send.at[s],
            recv_sem=recv.at[s], device_id=sib, device_id_type=MESH) for s in range(4)]
        for cp in cps:
            cp.start()
        for cp in cps:
            cp.wait()

    return pl.pallas_call(
        body, name="swap_halves", in_specs=[ANY], out_specs=ANY,
        out_shape=jax.ShapeDtypeStruct((4, H, D_MODEL), g.dtype),
        scratch_shapes=[pltpu.SemaphoreType.DMA((4,)), pltpu.SemaphoreType.DMA((4,))],
    )(g)


def _scatter_partials(pb):
    def body(p_ref, out_ref, send, recv):
        x, y, c, chips = _place()
        me = 2 * x + y
        cps = [pltpu.make_async_remote_copy(
            src_ref=p_ref.at[2 * cx + cy], dst_ref=out_ref.at[me], send_sem=send.at[k], recv_sem=recv.at[k],
            device_id=(cx, cy, c), device_id_type=MESH) for k, (cx, cy) in enumerate(chips)]
        for cp in cps:
            cp.start()
        for k, (cx, cy) in enumerate(chips):
            slot = out_ref.at[2 * cx + cy]
            pltpu.make_async_remote_copy(src_ref=slot, dst_ref=slot, send_sem=send.at[k], recv_sem=recv.at[k],
                                         device_id=(cx, cy, c), device_id_type=MESH).wait_recv()
        for cp in cps:
            cp.wait_send()

    return pl.pallas_call(
        body, name="scatter_partials", in_specs=[ANY], out_specs=ANY,
        out_shape=jax.ShapeDtypeStruct(pb.shape, pb.dtype),
        scratch_shapes=[pltpu.SemaphoreType.DMA((3,)), pltpu.SemaphoreType.DMA((3,))],
    )(pb)


def _sibling_half(f, name):
    def body(f_ref, out_ref, send, recv):
        x, y, c, _ = _place()
        cp = pltpu.make_async_remote_copy(src_ref=f_ref, dst_ref=out_ref, send_sem=send, recv_sem=recv,
                                          device_id=(x, y, 1 - c), device_id_type=MESH)
        cp.start()
        cp.wait()

    return pl.pallas_call(
        body, name=name, in_specs=[ANY], out_specs=ANY,
        out_shape=jax.ShapeDtypeStruct(f.shape, f.dtype),
        scratch_shapes=[pltpu.SemaphoreType.DMA, pltpu.SemaphoreType.DMA],
    )(f)


def _gather_tiles(tile, name):
    m_per = tile.shape[0]

    def body(x_ref, out_ref, send_sems, recv_sems, local_sem):
        x, y, c, chips = _place()
        me, sibling = (x, y, c), (x, y, 1 - c)

        def rows(px, py, pc):
            return out_ref.at[pl.ds((4 * px + 2 * py + pc) * m_per, m_per), :]

        def copy(k, block, to, src=None):
            return pltpu.make_async_remote_copy(
                src_ref=rows(*block) if src is None else src, dst_ref=rows(*block),
                send_sem=send_sems.at[k], recv_sem=recv_sems.at[k], device_id=to, device_id_type=MESH)

        mine = pltpu.make_async_copy(x_ref, rows(*me), local_sem)
        mine.start()
        first = [copy(0, me, sibling, src=x_ref)]
        first += [copy(1 + j, me, (*chip, c), src=x_ref) for j, chip in enumerate(chips)]
        for cp in first:
            cp.start()
        passed = [copy(4 + j, (*chip, c), sibling) for j, chip in enumerate(chips)]
        for j, chip in enumerate(chips):
            copy(1 + j, (*chip, c), me).wait_recv()
            passed[j].start()
        copy(0, sibling, me).wait_recv()
        for j, chip in enumerate(chips):
            copy(4 + j, (*chip, 1 - c), me).wait_recv()
        for cp in first + passed:
            cp.wait_send()
        mine.wait()

    return pl.pallas_call(
        body, name=name, out_shape=jax.ShapeDtypeStruct((8 * m_per, D_MODEL), tile.dtype),
        in_specs=[pl.BlockSpec(memory_space=pltpu.VMEM)], out_specs=pl.BlockSpec(memory_space=pltpu.VMEM),
        scratch_shapes=[pltpu.SemaphoreType.DMA((7,)), pltpu.SemaphoreType.DMA((7,)), pltpu.SemaphoreType.DMA],
    )(tile)


def _pad_rows(a, rows):
    return jnp.pad(a, ((0, rows - a.shape[0]), (0, 0)))


def _pad_row(v):
    v = v.reshape(1, -1)
    return jnp.pad(v, ((0, 0), (0, D_MODEL - v.shape[1])))


def _pack_shard(fwi, fwo, dwi, dwo, pwu, pwg):
    parts = [_pad_rows(fwi.reshape(1028, D_MODEL), FLAT_A), fwo.reshape(256, D_MODEL), dwi.reshape(2560, D_MODEL),
             dwo.reshape(256, D_MODEL), pwu.reshape(128, D_MODEL), pwg.reshape(512, D_MODEL),
             jnp.zeros((FLAT_TOTAL - FLAT_USED, D_MODEL), fwi.dtype)]
    return jnp.concatenate(parts, axis=0)


def _unpack_shard(flat):
    out, r0 = {}, 0
    for name, rows, slot in FLAT_ROWS:
        out[name] = flat[r0:r0 + rows]
        r0 += slot
    return (out["fox_w_in"].reshape(1, D_MODEL, 1028), out["fox_w_out"].reshape(1, 256, D_MODEL),
            out["dil_w_in"].reshape(1, D_MODEL, 2560), out["dil_w_out"].reshape(1, 256, D_MODEL),
            out["ple_w_up"].reshape(2, PLE_DIM, 256), out["ple_w_gate"].reshape(2, 256, D_MODEL))


def _from_shard_columns(a, n):
    return a.reshape(4, D_MODEL, n).transpose(1, 0, 2).reshape(D_MODEL, 4 * n)


def _to_shard_columns(a, n):
    return a.reshape(D_MODEL, 4, n).transpose(1, 0, 2).reshape(4, n, D_MODEL)


def _unpack_a(wall_a):
    return _from_shard_columns(wall_a[:, :1028], 1028)


def _unpack_b(wall_b):
    out, r0 = {}, 0
    for name, rows, slot in FLAT_ROWS[1:]:
        out[name] = wall_b[:, r0:r0 + rows]
        r0 += slot
    pwu4 = out["ple_w_up"].reshape(4, 2, PLE_DIM, 256)
    pwg4 = out["ple_w_gate"].reshape(4, 2, 256, D_MODEL)
    pwu = [pwu4[:, i].transpose(1, 0, 2).reshape(PLE_DIM, D_MODEL) for i in range(2)]
    pwg = [pwg4[:, i].reshape(D_MODEL, D_MODEL) for i in range(2)]
    return (out["fox_w_out"].reshape(D_MODEL, D_MODEL), _from_shard_columns(out["dil_w_in"], 2560),
            out["dil_w_out"].reshape(D_MODEL, D_MODEL), pwu, pwg)


def _gathered_b_weights(wall_b, w_b, chip):
    return _unpack_b(lax.dynamic_update_slice(wall_b, w_b[None], (chip, 0, 0)))


def _pack_grads_a(gfwi):
    return jnp.pad(_to_shard_columns(gfwi, 1028), ((0, 0), (0, FLAT_A - 1028), (0, 0)))


def _pack_grads_b(gfwo, gdwi, gdwo, gpwu, gpwg):
    up = jnp.stack([a.reshape(PLE_DIM, 4, 256).transpose(1, 0, 2) for a in gpwu], axis=1)
    gate = jnp.stack([a.reshape(4, 256, D_MODEL) for a in gpwg], axis=1)
    parts = [gfwo.reshape(4, 256, D_MODEL), _to_shard_columns(gdwi, 2560), gdwo.reshape(4, 256, D_MODEL),
             up.reshape(4, 128, D_MODEL), gate.reshape(4, 512, D_MODEL)]
    return jnp.concatenate(parts, axis=1).astype(BF16)


def _local_step(x, p0, p1, tgt, fox_g, dil_g, fin_g, b_f, fwi, w_b, chip):
    S = x.shape[0]
    tm = min(256, S)
    nh = FOX_HEADS // 2
    w_qkv0 = jnp.concatenate([fwi[:, :D_MODEL] * 0.125, fwi[:, D_MODEL:3 * D_MODEL]], axis=1)
    w_z0 = fwi[:, 3 * D_MODEL:4 * D_MODEL]
    w_f0 = jnp.pad(fwi[:, 4 * D_MODEL:], ((0, 0), (0, LANES - FOX_HEADS)))
    b_full = jnp.pad(b_f.reshape(1, FOX_HEADS), ((0, 0), (0, LANES - FOX_HEADS)))

    n0, = _rows(_rms_fwd, [x], [(D_MODEL, BF16)], name="norm0", tm=tm, bcast=[fox_g])
    qkv0 = _mm(n0, w_qkv0, out_dtype=BF16, name="proj_qkv0")
    z0 = _mm(n0, w_z0, name="proj_z0")
    fl0 = _mm(n0, w_f0, name="proj_f0")
    qx0, kx0 = _cumsum_rows([fl0], name="forget_cumsum", width=LANES, pre=_log_forget, bcast=[b_full],
                            post=_fox_extras, outs=[(D_MODEL, BF16), (D_MODEL, BF16)])
    o0, a0, g0, wall_b = _fox_fwd(qkv0, qx0, kx0, z0, w_b)
    fwo, dwi, dwo, pwu, pwg = _gathered_b_weights(wall_b, w_b, chip)
    w_qkv1 = dwi[:, :3 * DIL_QKV]
    w_z1 = dwi[:, 3 * DIL_QKV:]
    h1 = _mm(g0, fwo, add=x, name="out_proj0")
    u0 = _mm(p0, pwu[0], name="ple_up0")
    t0 = _mm(h1, pwg[0], name="ple_gate0")
    h2, n1 = _rows(_ple_norm, [h1, u0, t0], [(D_MODEL, F32), (D_MODEL, BF16)], name="ple_mix0_norm1", tm=tm,
                   bcast=[dil_g])

    qkv1 = _mm(n1, w_qkv1, name="proj_qkv1")
    z1 = _mm(n1, w_z1, name="proj_z1")
    o1, g1, lse1 = _dil_fwd(qkv1, z1)
    h3 = _mm(g1, dwo, add=h2, name="out_proj1")
    u1 = _mm(p1, pwu[1], name="ple_up1")
    t1 = _mm(h3, pwg[1], name="ple_gate1")

    du1, dt1, dh4, g_fin, loss = _rows(
        _ple_loss, [h3, u1, t1, tgt], [(D_MODEL, BF16), (D_MODEL, BF16), (D_MODEL, F32)], name="ple_mix1_loss_head",
        tm=tm, bcast=[fin_g], acc=[((1, D_MODEL), F32), ((1, LANES), F32)])
    g_up1 = _mm(p1, du1, ta=True, out_dtype=BF16, name="grad_ple_up1")
    g_gate1 = _mm(h3, dt1, ta=True, out_dtype=BF16, name="grad_ple_gate1")
    dh3 = _mm(dt1, pwg[1], tb=True, add=dh4, name="ple_back1")
    dg1 = _mm(dh3, dwo, tb=True, name="out_back1")
    g_dwo = _mm(g1, dh3, ta=True, out_dtype=BF16, name="grad_out1")
    do1, dz1, ld1 = _rows(_gate_bwd, [dg1, o1, z1, lse1], [(D_MODEL, F32), (D_MODEL, BF16), (D_MODEL, F32)],
                          name="gate_bwd1", tm=tm)
    dq1, dk1, dv1 = (lax.empty((S, DIL_QKV), BF16) for _ in range(3))
    for g in range(DIL_GROUPS):
        dq1 = _dil_bwd_q(qkv1, do1, ld1, dq1, g)
        dk1, dv1 = _dil_bwd_kv(qkv1, do1, ld1, dk1, dv1, g)
    g_dwi = _mm_tn_parts(n1, [dq1, dk1, dv1, dz1], name="grad_in1", out_dtype=BF16)
    dn1 = _mm_nt_parts([dq1, dk1, dv1, dz1], dwi, name="in_back1")
    dh2, du0, dt0, g_dil = _rows(_norm_ple_bwd, [dn1, h2, dh3, u0, t0], [(D_MODEL, F32), (D_MODEL, BF16), (D_MODEL, BF16)],
                                 name="norm_bwd1_ple_bwd0", tm=tm, bcast=[dil_g], acc=[((1, D_MODEL), F32)])
    g_up0 = _mm(p0, du0, ta=True, out_dtype=BF16, name="grad_ple_up0")
    g_gate0 = _mm(h1, dt0, ta=True, out_dtype=BF16, name="grad_ple_gate0")
    dh1 = _mm(dt0, pwg[0], tb=True, add=dh2, name="ple_back0")
    dg0 = _mm(dh1, fwo, tb=True, name="out_back0")
    g_fwo = _mm(g0, dh1, ta=True, out_dtype=BF16, name="grad_out0")
    do0, dz0, qxa0, dx0 = _rows(_gate_bwd_fox, [dg0, o0, z0, a0], [(D_MODEL, BF16)] * 4, name="gate_bwd0", tm=tm)
    g_b = _pack_grads_b(g_fwo, g_dwi, g_dwo, [g_up0, g_up1], [g_gate0, g_gate1])
    dq0, drx, dk0, dv0, dkx, got_b = _fox_bwd(qkv0, do0, qxa0, dx0, kx0, g_b)
    dlogf, = _cumsum_rows([drx, dkx], name="forget_cumsum_bwd", width=LANES, reverse=True, pre=_forget_dc)
    df0, g_bf = _rows(_forget_bwd, [dlogf, fl0], [(LANES, BF16)], name="forget_bwd", tm=tm, bcast=[b_full],
                      acc=[((1, LANES), F32)])
    g_fwi_main = _mm_tn_parts(n0, [dq0, dk0, dv0, dz0], name="grad_in0")
    g_fwi_f = _mm(n0, df0, ta=True, name="grad_in0_forget")
    dn0 = _mm_nt_parts([dq0, dk0, dv0, dz0], fwi[:, :4 * D_MODEL], name="in_back0")
    dn0 = _mm(df0, w_f0, tb=True, add=dn0, name="in_back0_forget")
    dx, g_fox = _rows(_rms_bwd, [dn0, x, dh1], [(D_MODEL, F32)], name="norm_bwd0", tm=tm, bcast=[fox_g],
                      acc=[((1, D_MODEL), F32)])
    g_fwi = jnp.concatenate([g_fwi_main, g_fwi_f[:, :FOX_HEADS]], axis=1)
    return loss, dx, g_fwi, g_b, got_b, (g_fox, _pad_row(g_bf[:, :FOX_HEADS]), g_fin, g_dil)


def kernel(x, p, fox_norm, fox_w_in, fox_b_f, fox_w_out, dil_norm, dil_w_in, dil_w_out, ple_w_up, ple_w_gate, final_norm, loss_target, m_fox_norm, m_fox_w_in, m_fox_b_f, m_fox_w_out, m_dil_norm, m_dil_w_in, m_dil_w_out, m_ple_w_up, m_ple_w_gate, m_final_norm, v_fox_norm, v_fox_w_in, v_fox_b_f, v_fox_w_out, v_dil_norm, v_dil_w_in, v_dil_w_out, v_ple_w_up, v_ple_w_gate, v_final_norm):
    xi, yi, ci = lax.axis_index("x"), lax.axis_index("y"), lax.axis_index("c")
    chip = 2 * xi + yi

    w_flat = _pack_shard(*(w.astype(BF16) for w in (fox_w_in, fox_w_out, dil_w_in, dil_w_out, ple_w_up, ple_w_gate)))
    w_a, w_b = w_flat[:FLAT_A], w_flat[FLAT_A:FLAT_USED]
    fwi = _unpack_a(lax.dynamic_update_slice(_gather_weights(w_a), w_a[None], (chip, 0, 0)))
    gains = _gather_tiles(_pad_rows(_pad_row(dil_norm), 8), "gather_gain").reshape(4, 2, 8, D_MODEL)
    dil_g = gains[:, 0, 0, :256].reshape(1, D_MODEL)

    loss_part, dx, g_fwi, g_b, got_b, small = _local_step(
        x[0], p[0, 0], p[1, 0], loss_target[0], fox_norm.reshape(1, D_MODEL), dil_g,
        final_norm.reshape(1, D_MODEL), fox_b_f, fwi, w_b, chip)
    g_fox, g_bf, g_fin, g_dil = small
    sum4 = lambda a, b, c, d: ((a.astype(F32) + b.astype(F32)) + c.astype(F32)) + d.astype(F32)

    got_b = lax.dynamic_update_slice(got_b, lax.dynamic_slice_in_dim(g_b, chip, 1, axis=0), (chip, 0, 0))
    sum_b, = _rows(sum4, [got_b[s] for s in range(4)], [(D_MODEL, F32)], name="chip_sum_b", tm=FLAT_B // 8)
    g_flat_b, = _rows(lambda a, b: a + b, [sum_b, _sibling_half(sum_b, "sibling_sum_b")], [(D_MODEL, F32)],
                      name="pair_sum_b", tm=FLAT_B // 8)

    half = FLAT_A // 2
    g_a = _pack_grads_a(g_fwi)
    theirs = _swap_halves(g_a)
    mine = lax.dynamic_slice_in_dim(g_a, ci * half, half, axis=1)
    part, = _rows(lambda a, b: a + b, [mine.reshape(4 * half, D_MODEL), theirs.reshape(4 * half, D_MODEL)],
                  [(D_MODEL, BF16)], name="pair_sum", tm=half)
    part = part.reshape(4, half, D_MODEL)
    own = lax.dynamic_slice_in_dim(part, chip, 1, axis=0)
    by_chip = lax.dynamic_update_slice(_scatter_partials(part), own, (chip, 0, 0))
    half_sum, = _rows(sum4, [by_chip[s] for s in range(4)], [(D_MODEL, F32)], name="chip_sum", tm=half)
    other_half = _sibling_half(half_sum, "sibling_half")
    g_flat_a = jnp.where(ci == 0, jnp.concatenate([half_sum, other_half], axis=0),
                         jnp.concatenate([other_half, half_sum], axis=0))
    g_flat = jnp.concatenate([g_flat_a, g_flat_b, jnp.zeros((FLAT_TOTAL - FLAT_USED, D_MODEL), F32)], axis=0)

    tile = jnp.concatenate([g_fox, g_bf, g_fin, g_dil, jnp.pad(loss_part, ((0, 0), (0, D_MODEL - LANES))),
                            jnp.zeros((3, D_MODEL), F32)], axis=0)
    tiles = _gather_tiles(tile, "gather_small")
    tot, = _rows(lambda *t: functools.reduce(lambda a, b: a + b, t), [tiles[8 * s:8 * s + 8] for s in range(8)],
                 [(D_MODEL, F32)], name="small_sum", tm=8)
    loss = tot[4, 0]
    g_small = jnp.concatenate([
        tot[0:3],
        _pad_row(lax.dynamic_slice_in_dim(tot[3], chip * 256, 256)), jnp.zeros((4, D_MODEL), F32)], axis=0)
    small_tile = lambda a, b, c, d: jnp.concatenate(
        [_pad_row(a), _pad_row(b), _pad_row(c), _pad_row(d), jnp.zeros((4, D_MODEL), F32)], axis=0)
    w_small = small_tile(fox_norm, fox_b_f, final_norm, dil_norm)
    m_small = small_tile(m_fox_norm, m_fox_b_f, m_final_norm, m_dil_norm)
    v_small = small_tile(v_fox_norm, v_fox_b_f, v_final_norm, v_dil_norm)

    d_small, nm_small, nv_small = _rows(_adamw, [w_small, g_small, m_small, v_small], [(D_MODEL, F32)] * 3,
                                        name="adamw_small", tm=8)
    grads = _unpack_shard(g_flat)
    weights = (fox_w_in, fox_w_out, dil_w_in, dil_w_out, ple_w_up, ple_w_gate)
    firsts = (m_fox_w_in, m_fox_w_out, m_dil_w_in, m_dil_w_out, m_ple_w_up, m_ple_w_gate)
    seconds = (v_fox_w_in, v_fox_w_out, v_dil_w_in, v_dil_w_out, v_ple_w_up, v_ple_w_gate)
    big = []
    for (name, _, _), w, g, m, v in zip(FLAT_ROWS, weights, grads, firsts, seconds):
        width = w.shape[-1]
        flat2 = lambda a: a.reshape(-1, width)
        res = _rows(_adamw, [flat2(w), flat2(g), flat2(m), flat2(v)], [(width, F32)] * 3, name=f"adamw_{name}",
                    tm=128 if width > D_MODEL + LANES else 256)
        big.append([r.reshape(w.shape) for r in res])

    def leaves(shards, small_rows):
        fwi_, fwo_, dwi_, dwo_, pwu_, pwg_ = shards
        return (small_rows[0:1], fwi_, small_rows[1:2, :FOX_HEADS], fwo_, small_rows[3:4, :256], dwi_, dwo_,
                pwu_, pwg_, small_rows[2])

    per_kind = lambda k: [b[k] for b in big]
    return (loss, dx[None], *leaves(grads, g_small), *leaves(per_kind(0), d_small), *leaves(per_kind(1), nm_small),
            *leaves(per_kind(2), nv_small))
```

```python
import functools

import numpy as np
import jax
import jax.numpy as jnp
from jax import lax
from jax.experimental import pallas as pl
from jax.experimental.pallas import tpu as pltpu

F32 = jnp.float32
BF16 = jnp.bfloat16

D_MODEL = 1024
PLE_DIM = 256
FOX_HEADS = 16
FOX_HEAD_DIM = 64
DIL_PATTERN = ((128, 1), (512, 4), (2048, 16))
DIL_GROUPS = 3
DIL_HEADS_PER_GROUP = 8
DIL_HEAD_DIM = 128
DIL_WINDOW_STEPS = 128
DIL_QKV = 3072
ALIBI_MAX_EXP = 8.0
RMS_EPS = 1e-6
ADAM_LR, ADAM_B1, ADAM_B2, ADAM_EPS, ADAM_WD, ADAM_STEP = 0.001, 0.9, 0.999, 1e-08, 0.01, 10

LANES = 128
VMEM_LIMIT = 56 * 1024 * 1024
MESH = pl.DeviceIdType.MESH
ANY = pl.BlockSpec(memory_space=pl.ANY)
NEG = -1e30

FLAT_ROWS = (("fox_w_in", 1028, 1056), ("fox_w_out", 256, 256), ("dil_w_in", 2560, 2560), ("dil_w_out", 256, 256),
             ("ple_w_up", 128, 128), ("ple_w_gate", 512, 512))
FLAT_USED = sum(r for _, _, r in FLAT_ROWS)
FLAT_TOTAL = 4864
FLAT_TILE = 608
FLAT_A = FLAT_ROWS[0][2]
FLAT_B = FLAT_USED - FLAT_A


def _cparams(*sem):
    return pltpu.CompilerParams(dimension_semantics=sem, vmem_limit_bytes=VMEM_LIMIT)


def _sigmoid(x):
    return 1.0 / (1.0 + jnp.exp(-x))


def _mm(a, b, *, name, ta=False, tb=False, out_dtype=F32, add=None):
    if ta:
        K, M = a.shape
    else:
        M, K = a.shape
    if tb:
        N, Kb = b.shape
    else:
        Kb, N = b.shape
    assert K == Kb, (a.shape, b.shape)
    tm, tn, tk = min(M, 1024), min(N, 1024), min(K, 1024)
    assert M % tm == 0 and N % tn == 0 and K % tk == 0, (M, N, K)
    nk = K // tk
    dn = (((0 if ta else 1,), (1 if tb else 0,)), ((), ()))

    def body(*refs):
        if add is None:
            a_ref, b_ref, o_ref, acc = refs
        else:
            a_ref, b_ref, add_ref, o_ref, acc = refs
        k = pl.program_id(2)

        @pl.when(k == 0)
        def _():
            acc[...] = jnp.zeros_like(acc)

        acc[...] += lax.dot_general(a_ref[...].astype(BF16), b_ref[...].astype(BF16), dn,
                                    preferred_element_type=F32)

        @pl.when(k == nk - 1)
        def _():
            r = acc[...]
            if add is not None:
                r = r + add_ref[...]
            o_ref[...] = r.astype(out_dtype)

    a_spec = (pl.BlockSpec((tk, tm), lambda i, j, k: (k, i)) if ta
              else pl.BlockSpec((tm, tk), lambda i, j, k: (i, k)))
    b_spec = (pl.BlockSpec((tn, tk), lambda i, j, k: (j, k)) if tb
              else pl.BlockSpec((tk, tn), lambda i, j, k: (k, j)))
    in_specs = [a_spec, b_spec]
    args = [a, b]
    if add is not None:
        in_specs.append(pl.BlockSpec((tm, tn), lambda i, j, k: (i, j)))
        args.append(add)
    return pl.pallas_call(
        body, name=name, grid=(M // tm, N // tn, nk),
        in_specs=in_specs, out_specs=pl.BlockSpec((tm, tn), lambda i, j, k: (i, j)),
        out_shape=jax.ShapeDtypeStruct((M, N), out_dtype),
        scratch_shapes=[pltpu.VMEM((tm, tn), F32)],
        compiler_params=_cparams("parallel", "parallel", "arbitrary"),
    )(*args)


def _mm_nt_parts(a_parts, b, *, name):
    M = a_parts[0].shape[0]
    N, K = b.shape
    tm, tn, tk = min(M, 1024), min(N, 1024), 1024
    nks = [a.shape[1] // tk for a in a_parts]
    offs = [sum(nks[:p]) for p in range(len(nks))]
    nk = sum(nks)
    assert nk * tk == K and M % tm == 0 and N % tn == 0, (M, N, K)
    n_parts = len(a_parts)

    def body(*refs):
        a_refs, b_ref, o_ref, acc = refs[:n_parts], refs[n_parts], refs[n_parts + 1], refs[n_parts + 2]
        k = pl.program_id(2)

        @pl.when(k == 0)
        def _():
            acc[...] = jnp.zeros_like(acc)

        for a_ref, off, n in zip(a_refs, offs, nks):
            @pl.when((k >= off) & (k < off + n))
            def _(a_ref=a_ref):
                acc[...] += lax.dot_general(a_ref[...].astype(BF16), b_ref[...].astype(BF16),
                                            (((1,), (1,)), ((), ())), preferred_element_type=F32)

        @pl.when(k == nk - 1)
        def _():
            o_ref[...] = acc[...]

    a_specs = [pl.BlockSpec((tm, tk), lambda i, j, k, off=off, n=n: (i, jnp.clip(k - off, 0, n - 1)))
               for off, n in zip(offs, nks)]
    return pl.pallas_call(
        body, name=name, grid=(M // tm, N // tn, nk),
        in_specs=a_specs + [pl.BlockSpec((tn, tk), lambda i, j, k: (j, k))],
        out_specs=pl.BlockSpec((tm, tn), lambda i, j, k: (i, j)),
        out_shape=jax.ShapeDtypeStruct((M, N), F32), scratch_shapes=[pltpu.VMEM((tm, tn), F32)],
        compiler_params=_cparams("parallel", "parallel", "arbitrary"),
    )(*a_parts, b)


def _mm_tn_parts(a, b_parts, *, name, out_dtype=F32):
    K, M = a.shape
    tm, tn, tk = min(M, 1024), 1024, min(K, 1024)
    njs = [b.shape[1] // tn for b in b_parts]
    offs = [sum(njs[:p]) for p in range(len(njs))]
    nj, nk = sum(njs), K // tk
    assert M % tm == 0 and K % tk == 0 and all(b.shape[1] % tn == 0 for b in b_parts)
    n_parts = len(b_parts)

    def body(*refs):
        a_ref, b_refs, o_ref, acc = refs[0], refs[1:1 + n_parts], refs[1 + n_parts], refs[2 + n_parts]
        j, k = pl.program_id(1), pl.program_id(2)

        @pl.when(k == 0)
        def _():
            acc[...] = jnp.zeros_like(acc)

        for b_ref, off, n in zip(b_refs, offs, njs):
            @pl.when((j >= off) & (j < off + n))
            def _(b_ref=b_ref):
                acc[...] += lax.dot_general(a_ref[...].astype(BF16), b_ref[...].astype(BF16),
                                            (((0,), (0,)), ((), ())), preferred_element_type=F32)

        @pl.when(k == nk - 1)
        def _():
            o_ref[...] = acc[...].astype(out_dtype)

    def b_spec(off, n):
        def index(i, j, k):
            mine = (j >= off) & (j < off + n)
            return jnp.where(mine, k, 0), jnp.clip(j - off, 0, n - 1)
        return pl.BlockSpec((tk, tn), index)

    return pl.pallas_call(
        body, name=name, grid=(M // tm, nj, nk),
        in_specs=[pl.BlockSpec((tk, tm), lambda i, j, k: (k, i))] + [b_spec(off, n) for off, n in zip(offs, njs)],
        out_specs=pl.BlockSpec((tm, tn), lambda i, j, k: (i, j)),
        out_shape=jax.ShapeDtypeStruct((M, nj * tn), out_dtype), scratch_shapes=[pltpu.VMEM((tm, tn), F32)],
        compiler_params=_cparams("parallel", "parallel", "arbitrary"),
    )(a, *b_parts)


def _rows(fn, ins, outs, *, name, tm, bcast=(), acc=()):
    R = ins[0].shape[0]
    assert R % tm == 0, (R, tm)
    n_in, n_b, n_out, n_acc = len(ins), len(bcast), len(outs), len(acc)

    def body(*refs):
        in_refs = refs[:n_in + n_b]
        out_refs = refs[n_in + n_b:n_in + n_b + n_out]
        acc_refs = refs[n_in + n_b + n_out:]
        res = fn(*[r[...] for r in in_refs])
        if not isinstance(res, (tuple, list)):
            res = (res,)
        for r, v in zip(out_refs, res[:n_out]):
            r[...] = v.astype(r.dtype)
        first = pl.program_id(0) == 0
        for r, v in zip(acc_refs, res[n_out:]):
            @pl.when(first)
            def _(r=r, v=v):
                r[...] = v.astype(r.dtype)

            @pl.when(jnp.logical_not(first))
            def _(r=r, v=v):
                r[...] += v.astype(r.dtype)

    in_specs = [pl.BlockSpec((tm, a.shape[1]), lambda i: (i, 0)) for a in ins]
    in_specs += [pl.BlockSpec(b.shape, lambda i, nd=b.ndim: (0,) * nd) for b in bcast]
    out_specs = [pl.BlockSpec((tm, c), lambda i: (i, 0)) for c, _ in outs]
    out_specs += [pl.BlockSpec(s, lambda i, nd=len(s): (0,) * nd) for s, _ in acc]
    out_shape = [jax.ShapeDtypeStruct((R, c), dt) for c, dt in outs]
    out_shape += [jax.ShapeDtypeStruct(s, dt) for s, dt in acc]
    res = pl.pallas_call(
        body, name=name, grid=(R // tm,), in_specs=in_specs, out_specs=out_specs, out_shape=out_shape,
        compiler_params=_cparams("arbitrary" if acc else "parallel"),
    )(*ins, *bcast)
    return res


def _exact_dot(sel, x, left):
    hi = x.astype(BF16)
    r1 = x - hi.astype(F32)
    mid = r1.astype(BF16)
    lo = (r1 - mid.astype(F32)).astype(BF16)
    dot = (lambda p: jnp.dot(sel, p, preferred_element_type=F32)) if left else (
        lambda p: jnp.dot(p, sel, preferred_element_type=F32))
    return dot(hi) + dot(mid) + dot(lo)


def _cumsum_rows(ins, *, name, width, reverse=False, pre=None, bcast=(), post=None, outs=None):
    S = ins[0].shape[0]
    outs = outs or [(width, F32)]
    out_dtypes = [dt for _, dt in outs]
    tb = 256
    nb = S // tb
    assert S % tb == 0
    n_in = len(ins) + len(bcast)
    n_out = len(out_dtypes)

    def body(*refs):
        in_refs, o_refs, carry = refs[:n_in], refs[n_in:n_in + n_out], refs[n_in + n_out]

        @pl.when(pl.program_id(0) == 0)
        def _():
            carry[...] = jnp.zeros_like(carry)

        xv = in_refs[0][...] if pre is None else pre(*[r[...] for r in in_refs])
        r_ = lax.broadcasted_iota(jnp.int32, (tb, tb), 0)
        c_ = lax.broadcasted_iota(jnp.int32, (tb, tb), 1)
        tri = jnp.where((c_ >= r_) if reverse else (c_ <= r_), 1.0, 0.0).astype(BF16)
        cs = _exact_dot(tri, xv, left=True) + carry[...]
        outs = (cs,) if post is None else post(cs)
        for o_ref, v in zip(o_refs, outs):
            o_ref[...] = v.astype(o_ref.dtype)
        carry[...] = cs[0:1, :] if reverse else cs[tb - 1:tb, :]

    blk = (lambda i: (nb - 1 - i, 0)) if reverse else (lambda i: (i, 0))
    in_specs = [pl.BlockSpec((tb, a.shape[1]), blk) for a in ins]
    in_specs += [pl.BlockSpec(b.shape, lambda i, nd=b.ndim: (0,) * nd) for b in bcast]
    return pl.pallas_call(
        body, name=name, grid=(nb,), in_specs=in_specs, out_specs=[pl.BlockSpec((tb, c), blk) for c, _ in outs],
        out_shape=[jax.ShapeDtypeStruct((S, c), dt) for c, dt in outs], scratch_shapes=[pltpu.VMEM((1, width), F32)],
        compiler_params=_cparams("arbitrary"),
    )(*ins, *bcast)


def _rms_fwd(x, g):
    r = lax.rsqrt(jnp.mean(x * x, axis=1, keepdims=True) + RMS_EPS)
    return (x * r) * g


def _rms_bwd(dn, x, dres, g):
    r = lax.rsqrt(jnp.mean(x * x, axis=1, keepdims=True) + RMS_EPS)
    xh = x * r
    w = dn * g
    dx = r * (w - xh * jnp.mean(w * xh, axis=1, keepdims=True))
    return dres + dx, jnp.sum(dn * xh, axis=0, keepdims=True)


def _final_stage(h, tgt, g):
    r = lax.rsqrt(jnp.mean(h * h, axis=1, keepdims=True) + RMS_EPS)
    xh = h * r
    diff = xh * g - tgt
    loss = 0.5 * jnp.sum(jnp.mean(diff * diff, axis=1, keepdims=True), axis=0, keepdims=True)
    dy = diff * (1.0 / D_MODEL)
    w = dy * g
    dh = r * (w - xh * jnp.mean(w * xh, axis=1, keepdims=True))
    return dh, jnp.sum(dy * xh, axis=0, keepdims=True), jnp.broadcast_to(loss, (1, LANES))


def _ple_fwd(h, u, t):
    return h + u * _sigmoid(t)


def _ple_bwd(dh, u, t):
    s = _sigmoid(t)
    return dh * s, dh * u * s * (1.0 - s)


def _ple_norm(h, u, t, g):
    h2 = _ple_fwd(h, u, t)
    return h2, _rms_fwd(h2, g)


def _ple_loss(h, u, t, tgt, g):
    dh, g_gain, loss = _final_stage(_ple_fwd(h, u, t), tgt, g)
    du, dt = _ple_bwd(dh, u, t)
    return du, dt, dh, g_gain, loss


def _norm_ple_bwd(dn, x, dres, u, t, g):
    dh, g_gain = _rms_bwd(dn, x, dres, g)
    du, dt = _ple_bwd(dh, u, t)
    return dh, du, dt, g_gain


def _gate_fwd(o, z):
    return o * (z * _sigmoid(z))


def _head_sums(prod):
    tm, width = prod.shape
    cols = [jnp.broadcast_to(jnp.sum(prod[:, b * LANES:(b + 1) * LANES], axis=1, keepdims=True), (tm, LANES))
            for b in range(width // LANES)]
    return jnp.concatenate(cols, axis=1)


def _gate_bwd(dg, o, z, lse):
    s = _sigmoid(z)
    do = dg * (z * s)
    dz = dg * o * (s * (1.0 + z * (1.0 - s)))
    lane = lax.broadcasted_iota(jnp.int32, do.shape, 1) % LANES
    return do, dz, jnp.where(lane < LANES // 2, lse, _head_sums(do * o))


def _log_forget(fl, b):
    u = fl + b
    return jnp.minimum(u, 0.0) - jnp.log(1.0 + jnp.exp(-jnp.abs(u)))


def _adamw(w, g, m, v):
    m = ADAM_B1 * m + (1.0 - ADAM_B1) * g
    v = ADAM_B2 * v + (1.0 - ADAM_B2) * (g * g)
    m_hat = m / (1.0 - ADAM_B1 ** ADAM_STEP)
    v_hat = v / (1.0 - ADAM_B2 ** ADAM_STEP)
    delta = -ADAM_LR * (m_hat / (jnp.sqrt(v_hat) + ADAM_EPS) + ADAM_WD * w)
    return delta, m, v


FOX_TQ = 1024


def _split3(x):
    p1 = x.astype(BF16).astype(F32)
    r = x - p1
    p2 = r.astype(BF16).astype(F32)
    return p1, p2, r - p2


def _lane_in_head(shape):
    return lax.broadcasted_iota(jnp.int32, shape, 1) % FOX_HEAD_DIM


def _query_extras(x):
    lm = _lane_in_head(x.shape)
    p1, p2, p3 = _split3(x)
    return jnp.where(lm == 0, p1, jnp.where(lm == 1, p2, jnp.where(lm == 2, p3, jnp.where(lm < 6, 1.0, 0.0))))


def _key_extras(c):
    lm = _lane_in_head(c.shape)
    p1, p2, p3 = _split3(c)
    return jnp.where(lm < 3, 1.0, jnp.where(lm == 3, -p1, jnp.where(lm == 4, -p2, jnp.where(lm == 5, -p3, 0.0))))


def _head_lane_base(head):
    return LANES * (head // 2) + FOX_HEAD_DIM * (1 - head % 2)


def _fox_extras(c):
    head = lax.broadcasted_iota(jnp.int32, (LANES, D_MODEL), 0)
    lane = lax.broadcasted_iota(jnp.int32, (LANES, D_MODEL), 1)
    owner = 2 * (lane // LANES) + 1 - (lane % LANES) // FOX_HEAD_DIM
    wide = _exact_dot(jnp.where(head == owner, 1.0, 0.0).astype(BF16), c, left=False)
    return _query_extras(wide), _key_extras(wide)


def _swapped_head_sums(prod):
    tm, width = prod.shape
    lane = lax.broadcasted_iota(jnp.int32, (tm, LANES), 1)
    low = lane < FOX_HEAD_DIM
    cols = []
    for b in range(width // LANES):
        blk = prod[:, b * LANES:(b + 1) * LANES]
        sa = jnp.sum(jnp.where(low, blk, 0.0), axis=1, keepdims=True)
        sb = jnp.sum(jnp.where(low, 0.0, blk), axis=1, keepdims=True)
        cols.append(jnp.where(low, sb, sa))
    return jnp.concatenate(cols, axis=1)


def _gate_bwd_fox(dg, o, z, a):
    s = _sigmoid(z)
    do = dg * (z * s)
    dz = dg * o * (s * (1.0 + z * (1.0 - s)))
    lm = _lane_in_head(do.shape)
    d1, d2, d3 = _split3(-_swapped_head_sums(do * o))
    dx = jnp.where(lm == 0, d1, jnp.where(lm == 1, d2, jnp.where(lm == 2, d3, 0.0)))
    return do, dz, _query_extras(a), dx


def _forget_dc(drx, dkx):
    lane = lax.broadcasted_iota(jnp.int32, (D_MODEL, LANES), 0)
    head = lax.broadcasted_iota(jnp.int32, (D_MODEL, LANES), 1)
    base = _head_lane_base(head)
    pick = lambda l: jnp.where((lane == base + l) & (head < FOX_HEADS), 1.0, 0.0).astype(BF16)
    return _exact_dot(pick(0), drx, left=False) - _exact_dot(pick(3), dkx, left=False)


def _forget_bwd(dl, fl, b):
    du = dl * (1.0 / (1.0 + jnp.exp(fl + b)))
    return du, jnp.sum(du, axis=0, keepdims=True)


def _chip_exchange(src_of, out_ref, send, recv):
    x, y, c = lax.axis_index("x"), lax.axis_index("y"), lax.axis_index("c")
    chips = [(1 - x, y), (x, 1 - y), (1 - x, 1 - y)]
    me = 2 * x + y
    sends = [pltpu.make_async_remote_copy(src_ref=src_of(cx, cy), dst_ref=out_ref.at[me], send_sem=send.at[k],
                                          recv_sem=recv.at[k], device_id=(cx, cy, c), device_id_type=MESH)
             for k, (cx, cy) in enumerate(chips)]

    def start():
        for cp in sends:
            cp.start()

    def wait():
        for k, (cx, cy) in enumerate(chips):
            slot = out_ref.at[2 * cx + cy]
            pltpu.make_async_remote_copy(src_ref=slot, dst_ref=slot, send_sem=send.at[k], recv_sem=recv.at[k],
                                         device_id=(cx, cy, c), device_id_type=MESH).wait_recv()
        for cp in sends:
            cp.wait_send()

    return start, wait


def _fox_fwd(qkv, qx, kx, z, w_b):
    S = qkv.shape[0]
    tq = min(FOX_TQ, S)
    nq = S // tq
    nt = (((1,), (1,)), ((), ()))

    def body(q_ref, qx_ref, k_ref, v_ref, kx_ref, z_ref, wb_ref, o_ref, a_ref, g_ref, wall_ref, send, recv):
        i = pl.program_id(1)
        start, wait = _chip_exchange(lambda cx, cy: wb_ref, wall_ref, send, recv)
        pl.when((pl.program_id(0) == 0) & (i == 0))(start)
        low = lax.broadcasted_iota(jnp.int32, (tq, LANES), 1) < FOX_HEAD_DIM
        row = lax.broadcasted_iota(jnp.int32, (tq, tq), 0)
        col = lax.broadcasted_iota(jnp.int32, (tq, tq), 1)
        q2, x2 = q_ref[...], qx_ref[...]
        qa = (jnp.where(low, q2, x2), jnp.where(low, x2, q2))

        def step(kb, carry, diag):
            start = pl.multiple_of(kb * tq, tq)
            k2 = k_ref[pl.ds(start, tq), :]
            v2 = v_ref[pl.ds(start, tq), :]
            y2 = kx_ref[pl.ds(start, tq), :]
            one = jnp.ones_like(v2)
            ka = (jnp.where(low, k2, y2), jnp.where(low, y2, k2))
            va = (jnp.where(low, v2, one), jnp.where(low, one, v2))
            out = []
            for hh in range(2):
                m, acc = carry[hh]
                s = lax.dot_general(qa[hh], ka[hh], nt, preferred_element_type=F32)
                if diag:
                    s = jnp.where(col <= row, s, NEG)
                m_new = jnp.maximum(m, jnp.max(s, axis=1, keepdims=True))
                pr = jnp.exp(s - m_new)
                acc = jnp.exp(m - m_new) * acc + jnp.dot(pr.astype(BF16), va[hh], preferred_element_type=F32)
                out.append((m_new, acc))
            return tuple(out)

        init = ((jnp.full((tq, 1), NEG, F32), jnp.zeros((tq, LANES), F32)),) * 2
        carry = lax.fori_loop(0, i, functools.partial(step, diag=False), init)
        (m_a, acc_a), (m_b, acc_b) = step(i, carry, True)
        l_a, l_b = acc_a[:, FOX_HEAD_DIM:FOX_HEAD_DIM + 1], acc_b[:, 0:1]
        xf = x2.astype(F32)
        c_a = xf[:, 64:65] + xf[:, 65:66] + xf[:, 66:67]
        c_b = xf[:, 0:1] + xf[:, 1:2] + xf[:, 2:3]
        o = jnp.where(low, acc_a / l_a, acc_b / l_b)
        o_ref[...] = o
        g_ref[...] = _gate_fwd(o, z_ref[...]).astype(BF16)
        a_ref[...] = jnp.where(low, jnp.broadcast_to(c_b - (m_b + jnp.log(l_b)), (tq, LANES)),
                               jnp.broadcast_to(c_a - (m_a + jnp.log(l_a)), (tq, LANES)))
        pl.when((pl.program_id(0) == FOX_HEADS // 2 - 1) & (i == nq - 1))(wait)

    blk = lambda cb: pl.BlockSpec((tq, LANES), lambda h, i, cb=cb: (i, cb + h))
    res = lambda cb: pl.BlockSpec((S, LANES), lambda h, i, cb=cb: (0, cb + h))
    return pl.pallas_call(
        body, name="fox_attn_fwd", grid=(FOX_HEADS // 2, nq),
        in_specs=[blk(0), blk(0), res(8), res(16), res(0), blk(0), ANY],
        out_specs=[blk(0), blk(0), blk(0), ANY],
        out_shape=[jax.ShapeDtypeStruct((S, D_MODEL), F32), jax.ShapeDtypeStruct((S, D_MODEL), F32),
                   jax.ShapeDtypeStruct((S, D_MODEL), BF16), jax.ShapeDtypeStruct((4,) + w_b.shape, w_b.dtype)],
        scratch_shapes=[pltpu.SemaphoreType.DMA((3,)), pltpu.SemaphoreType.DMA((3,))],
        compiler_params=_cparams("arbitrary", "arbitrary"),
    )(qkv, qx, qkv, qkv, kx, z, w_b)


def _fox_bwd(qkv, do, qxa, dx, kx, g_b):
    S = qkv.shape[0]
    tq = min(FOX_TQ, S)
    nq = S // tq
    nt = (((1,), (1,)), ((), ()))
    tn = (((0,), (0,)), ((), ()))

    def body(q_ref, qx_ref, do_ref, dx_ref, k_ref, v_ref, kx_ref, gb_ref, dq_ref, dr_ref, dk_ref, dv_ref, dkx_ref,
             got_ref, send, recv):
        kb = pl.program_id(1)
        start, wait = _chip_exchange(lambda cx, cy: gb_ref.at[2 * cx + cy], got_ref, send, recv)
        pl.when((pl.program_id(0) == 0) & (kb == 0))(start)
        low = lax.broadcasted_iota(jnp.int32, (tq, LANES), 1) < FOX_HEAD_DIM
        row = lax.broadcasted_iota(jnp.int32, (tq, tq), 0)
        col = lax.broadcasted_iota(jnp.int32, (tq, tq), 1)

        @pl.when(kb == 0)
        def _():
            dq_ref[...] = jnp.zeros_like(dq_ref)
            dr_ref[...] = jnp.zeros_like(dr_ref)

        k2, v2, y2 = k_ref[...], v_ref[...], kx_ref[...]
        one = jnp.ones_like(v2)
        ka = (jnp.where(low, k2, y2), jnp.where(low, y2, k2))
        va = (jnp.where(low, v2, one), jnp.where(low, one, v2))

        def step(qb, carry, diag):
            start = pl.multiple_of(qb * tq, tq)
            q2 = q_ref[pl.ds(start, tq), :]
            x2 = qx_ref[pl.ds(start, tq), :]
            d2 = do_ref[pl.ds(start, tq), :]
            e2 = dx_ref[pl.ds(start, tq), :]
            qa = (jnp.where(low, q2, x2), jnp.where(low, x2, q2))
            da = (jnp.where(low, d2, e2), jnp.where(low, e2, d2))
            new, res = [], []
            for hh in range(2):
                dk, dv = carry[hh]
                s = lax.dot_general(qa[hh], ka[hh], nt, preferred_element_type=F32)
                if diag:
                    s = jnp.where(col <= row, s, NEG)
                pr = jnp.exp(s)
                ds = pr * lax.dot_general(da[hh], va[hh], nt, preferred_element_type=F32)
                prb, dsb = pr.astype(BF16), ds.astype(BF16)
                dv = dv + lax.dot_general(prb, da[hh], tn, preferred_element_type=F32)
                dk = dk + lax.dot_general(dsb, qa[hh], tn, preferred_element_type=F32)
                res.append(jnp.dot(dsb, ka[hh], preferred_element_type=F32))
                new.append((dk, dv))
            dq_ref[pl.ds(start, tq), :] += jnp.where(low, res[0], res[1])
            dr_ref[pl.ds(start, tq), :] += jnp.where(low, res[1], res[0])
            return tuple(new)

        init = ((jnp.zeros((tq, LANES), F32), jnp.zeros((tq, LANES), F32)),) * 2
        carry = step(kb, init, True)
        (dk_a, dv_a), (dk_b, dv_b) = lax.fori_loop(kb + 1, nq, functools.partial(step, diag=False), carry)
        dk_ref[...] = jnp.where(low, dk_a, dk_b).astype(BF16)
        dv_ref[...] = jnp.where(low, dv_a, dv_b).astype(BF16)
        dkx_ref[...] = jnp.where(low, dk_b, dk_a)

        @pl.when(kb == nq - 1)
        def _():
            dq_ref[...] = dq_ref[...] * (FOX_HEAD_DIM ** -0.5)

        pl.when((pl.program_id(0) == FOX_HEADS // 2 - 1) & (kb == nq - 1))(wait)

    res = lambda cb: pl.BlockSpec((S, LANES), lambda h, k, cb=cb: (0, cb + h))
    blk = lambda cb: pl.BlockSpec((tq, LANES), lambda h, k, cb=cb: (k, cb + h))
    f32, b16 = jax.ShapeDtypeStruct((S, D_MODEL), F32), jax.ShapeDtypeStruct((S, D_MODEL), BF16)
    return pl.pallas_call(
        body, name="fox_attn_bwd", grid=(FOX_HEADS // 2, nq),
        in_specs=[res(0), res(0), res(0), res(0), blk(8), blk(16), blk(0), ANY],
        out_specs=[res(0), res(0), blk(0), blk(0), blk(0), ANY],
        out_shape=[f32, f32, b16, b16, f32, jax.ShapeDtypeStruct(g_b.shape, g_b.dtype)],
        scratch_shapes=[pltpu.SemaphoreType.DMA((3,)), pltpu.SemaphoreType.DMA((3,))],
        compiler_params=_cparams("arbitrary", "arbitrary"),
    )(qkv, qxa, do, dx, qkv, qkv, kx, g_b)


def _alibi_slopes():
    n = DIL_GROUPS * DIL_HEADS_PER_GROUP
    s = np.float32(2.0) ** (np.float32(-ALIBI_MAX_EXP) * np.arange(1, n + 1, dtype=np.float32) / np.float32(n))
    return s.astype(np.float32).reshape(DIL_GROUPS, DIL_HEADS_PER_GROUP)


W = DIL_WINDOW_STEPS
DIL_SCALE = DIL_HEAD_DIM ** -0.5


DIL_ROWS = 2048
DIL_BWD_ROWS = 4096
NT = (((1,), (1,)), ((), ()))
TN = (((0,), (0,)), ((), ()))
QKV_BLOCKS = DIL_QKV // LANES


def _slope_table():
    t = np.zeros((DIL_HEADS_PER_GROUP, 8, LANES), np.float32)
    t[:, :DIL_GROUPS, :] = _alibi_slopes().T[:, :, None]
    return jnp.asarray(t)


def _phase_rows(start, d):
    return pl.ds(start, W, stride=d) if d > 1 else pl.ds(start, W)


def _for_phases(d, unit):
    if d == 1:
        unit(0)
    else:
        lax.fori_loop(0, d, lambda r, c: (unit(r), c)[1], 0)


def _window_geometry(shape, q_axis_offset):
    i_ = lax.broadcasted_iota(jnp.int32, shape, 0)
    j_ = lax.broadcasted_iota(jnp.int32, shape, 1)
    dist = q_axis_offset + i_ - j_
    return i_, j_, dist, (dist >= 0) & (dist <= W)


def _dil_fwd(qkv, z):
    S = qkv.shape[0]
    rb = min(DIL_ROWS, S)
    nb = S // rb
    spans = [W * d for _, d in DIL_PATTERN]

    def body(*refs):
        slope_ref = refs[0]
        q, kc, kp, vc, vp = refs[1:4], refs[4:7], refs[7:10], refs[10:13], refs[13:16]
        z_ref, o_ref, g_ref, l_ref = refs[16:20]
        og, lg = refs[20:23], refs[23:26]
        i = pl.program_id(1)
        _, j_, dist, inwin = _window_geometry((W, 2 * W), W)
        distf = dist.astype(F32)
        for g, (_, d) in enumerate(DIL_PATTERN):
            span = spans[g]
            bias = -(slope_ref[0, g:g + 1, 0:1] * float(d)) * distf
            def phase(r, g=g, d=d, span=span, bias=bias):
                kprev, vprev = kp[g][_phase_rows(r, d), :].astype(BF16), vp[g][_phase_rows(r, d), :].astype(BF16)
                for b in range(rb // span):
                    rows = _phase_rows(b * span + r, d)
                    kcur, vcur = kc[g][rows, :].astype(BF16), vc[g][rows, :].astype(BF16)
                    valid = inwin if b > 0 else inwin & ((j_ >= W) | (i > 0))
                    k2 = jnp.concatenate([kprev, kcur], axis=0)
                    v2 = jnp.concatenate([vprev, vcur], axis=0)
                    s = lax.dot_general(q[g][rows, :].astype(BF16), k2, NT, preferred_element_type=F32)
                    s = jnp.where(valid, s * DIL_SCALE + bias, NEG)
                    m = jnp.max(s, axis=1, keepdims=True)
                    pr = jnp.exp(s - m)
                    l = jnp.sum(pr, axis=1, keepdims=True)
                    og[g][rows, :] = jnp.dot(pr.astype(BF16), v2, preferred_element_type=F32) / l
                    lg[g][rows, :] = jnp.broadcast_to(m + jnp.log(l), (W, LANES))
                    kprev, vprev = kcur, vcur

            _for_phases(d, phase)

        def mix(cix, c):
            sl = pl.ds(pl.multiple_of(cix * 256, 256), 256)
            l1, l2, l3 = lg[0][sl, :], lg[1][sl, :], lg[2][sl, :]
            m = jnp.maximum(jnp.maximum(l1, l2), l3)
            e1, e2, e3 = jnp.exp(l1 - m), jnp.exp(l2 - m), jnp.exp(l3 - m)
            tot = e1 + e2 + e3
            o = (e1 * og[0][sl, :] + e2 * og[1][sl, :] + e3 * og[2][sl, :]) / tot
            o_ref[sl, :] = o
            g_ref[sl, :] = _gate_fwd(o, z_ref[sl, :]).astype(BF16)
            l_ref[sl, :] = m + jnp.log(tot)
            return c

        lax.fori_loop(0, rb // 256, mix, 0)

    cur = lambda off, g: pl.BlockSpec((rb, LANES), lambda h, i: (i, off * QKV_BLOCKS + g * DIL_HEADS_PER_GROUP + h))
    prev = lambda off, g: pl.BlockSpec(
        (spans[g], LANES),
        lambda h, i: (jnp.maximum(i * (rb // spans[g]) - 1, 0), off * QKV_BLOCKS + g * DIL_HEADS_PER_GROUP + h))
    row = pl.BlockSpec((rb, LANES), lambda h, i: (i, h))
    groups = range(DIL_GROUPS)
    in_specs = [pl.BlockSpec((1, 8, LANES), lambda h, i: (h, 0, 0))]
    in_specs += [cur(0, g) for g in groups] + [cur(1, g) for g in groups] + [prev(1, g) for g in groups]
    in_specs += [cur(2, g) for g in groups] + [prev(2, g) for g in groups] + [row]
    f32 = jax.ShapeDtypeStruct((S, D_MODEL), F32)
    return pl.pallas_call(
        body, name="dil_attn_fwd", grid=(DIL_HEADS_PER_GROUP, nb), in_specs=in_specs,
        out_specs=[row, row, row], out_shape=[f32, jax.ShapeDtypeStruct((S, D_MODEL), BF16), f32],
        scratch_shapes=[pltpu.VMEM((rb, LANES), F32)] * 6,
        compiler_params=_cparams("parallel", "parallel"),
    )(_slope_table(), *([qkv] * 15), z)


def _dil_bwd_q(qkv, do, ld, buf, g):
    S = qkv.shape[0]
    d = DIL_PATTERN[g][1]
    span = W * d
    rb = min(max(DIL_ROWS, min(8 * span, DIL_BWD_ROWS)), S)
    nb = S // rb

    def body(slope_ref, q_ref, kc, kp, vc, vp, do_ref, ld_ref, buf_ref, dq_ref, dq_f32):
        i = pl.program_id(1)
        _, j_, dist, inwin = _window_geometry((W, 2 * W), W)
        bias = -(slope_ref[0, g:g + 1, 0:1] * float(d)) * dist.astype(F32)
        def phase(r):
            kprev, vprev = kp[_phase_rows(r, d), :].astype(BF16), vp[_phase_rows(r, d), :].astype(BF16)
            for b in range(rb // span):
                rows = _phase_rows(b * span + r, d)
                kcur, vcur = kc[rows, :].astype(BF16), vc[rows, :].astype(BF16)
                valid = inwin if b > 0 else inwin & ((j_ >= W) | (i > 0))
                k2 = jnp.concatenate([kprev, kcur], axis=0)
                v2 = jnp.concatenate([vprev, vcur], axis=0)
                ld = ld_ref[rows, :]
                s = lax.dot_general(q_ref[rows, :].astype(BF16), k2, NT, preferred_element_type=F32)
                s = jnp.where(valid, s * DIL_SCALE + bias - ld[:, 0:1], NEG)
                pr = jnp.exp(s)
                dp = lax.dot_general(do_ref[rows, :].astype(BF16), v2, NT, preferred_element_type=F32)
                ds = pr * (dp - ld[:, LANES // 2:LANES // 2 + 1])
                dq_f32[rows, :] = jnp.dot(ds.astype(BF16), k2, preferred_element_type=F32) * DIL_SCALE
                kprev, vprev = kcur, vcur

        _for_phases(d, phase)
        dq_ref[...] = dq_f32[...].astype(BF16)

    col = lambda off: off * QKV_BLOCKS + g * DIL_HEADS_PER_GROUP
    cur = lambda off: pl.BlockSpec((rb, LANES), lambda h, i: (i, col(off) + h))
    prev = lambda off: pl.BlockSpec((span, LANES), lambda h, i: (jnp.maximum(i * (rb // span) - 1, 0), col(off) + h))
    row = pl.BlockSpec((rb, LANES), lambda h, i: (i, h))
    return pl.pallas_call(
        body, name=f"dil_attn_bwd_q_g{g}", grid=(DIL_HEADS_PER_GROUP, nb),
        in_specs=[pl.BlockSpec((1, 8, LANES), lambda h, i: (h, 0, 0)), cur(0), cur(1), prev(1), cur(2), prev(2),
                  row, row, pl.BlockSpec(memory_space=pl.ANY)],
        out_specs=pl.BlockSpec((rb, LANES), lambda h, i: (i, g * DIL_HEADS_PER_GROUP + h)),
        out_shape=jax.ShapeDtypeStruct(buf.shape, buf.dtype), input_output_aliases={8: 0},
        scratch_shapes=[pltpu.VMEM((rb, LANES), F32)],
        compiler_params=_cparams("parallel", "parallel"),
    )(_slope_table(), qkv, qkv, qkv, qkv, qkv, do, ld, buf)


def _dil_bwd_kv(qkv, do, ld, bufk, bufv, g):
    S = qkv.shape[0]
    d = DIL_PATTERN[g][1]
    span = W * d
    rb = min(max(DIL_ROWS, min(8 * span, DIL_BWD_ROWS)), S)
    nb = S // rb
    nub = rb // span

    def body(slope_ref, k_ref, v_ref, qc, qn, doc, don, ldc, ldn, bufk_ref, bufv_ref, dk_ref, dv_ref, dk_f32, dv_f32):
        i = pl.program_id(1)
        i_, _, dist, inwin = _window_geometry((2 * W, W), 0)
        bias = -(slope_ref[0, g:g + 1, 0:1] * float(d)) * dist.astype(F32)
        def phase(r):
            first = _phase_rows(r, d)
            qcur, docur, ldcur = qc[first, :].astype(BF16), doc[first, :].astype(BF16), ldc[first, :]
            for b in range(nub):
                rows = _phase_rows(b * span + r, d)
                if b < nub - 1:
                    nxt = _phase_rows((b + 1) * span + r, d)
                    qnext, donext, ldnext, valid = qc[nxt, :].astype(BF16), doc[nxt, :].astype(BF16), ldc[nxt, :], inwin
                else:
                    qnext, donext, ldnext = qn[first, :].astype(BF16), don[first, :].astype(BF16), ldn[first, :]
                    valid = inwin & ((i_ < W) | (i < nb - 1))
                q2 = jnp.concatenate([qcur, qnext], axis=0)
                do2 = jnp.concatenate([docur, donext], axis=0)
                ld2 = jnp.concatenate([ldcur, ldnext], axis=0)
                s = lax.dot_general(q2, k_ref[rows, :].astype(BF16), NT, preferred_element_type=F32)
                s = jnp.where(valid, s * DIL_SCALE + bias - ld2[:, 0:1], NEG)
                pr = jnp.exp(s)
                dp = lax.dot_general(do2, v_ref[rows, :].astype(BF16), NT, preferred_element_type=F32)
                ds = pr * (dp - ld2[:, LANES // 2:LANES // 2 + 1])
                dv_f32[rows, :] = lax.dot_general(pr.astype(BF16), do2, TN, preferred_element_type=F32)
                dk_f32[rows, :] = lax.dot_general(ds.astype(BF16), q2, TN, preferred_element_type=F32) * DIL_SCALE
                qcur, docur, ldcur = qnext, donext, ldnext

        _for_phases(d, phase)
        dk_ref[...] = dk_f32[...].astype(BF16)
        dv_ref[...] = dv_f32[...].astype(BF16)

    col = lambda off: off * QKV_BLOCKS + g * DIL_HEADS_PER_GROUP
    cur = lambda off: pl.BlockSpec((rb, LANES), lambda h, i: (i, col(off) + h))
    nxt_blk = lambda i: jnp.minimum((i + 1) * nub, S // span - 1)
    nxt = lambda off: pl.BlockSpec((span, LANES), lambda h, i: (nxt_blk(i), col(off) + h))
    row = pl.BlockSpec((rb, LANES), lambda h, i: (i, h))
    row_nxt = pl.BlockSpec((span, LANES), lambda h, i: (nxt_blk(i), h))
    any_ = pl.BlockSpec(memory_space=pl.ANY)
    out = pl.BlockSpec((rb, LANES), lambda h, i: (i, g * DIL_HEADS_PER_GROUP + h))
    return pl.pallas_call(
        body, name=f"dil_attn_bwd_kv_g{g}", grid=(DIL_HEADS_PER_GROUP, nb),
        in_specs=[pl.BlockSpec((1, 8, LANES), lambda h, i: (h, 0, 0)), cur(1), cur(2), cur(0), nxt(0),
                  row, row_nxt, row, row_nxt, any_, any_],
        out_specs=[out, out],
        out_shape=[jax.ShapeDtypeStruct(bufk.shape, bufk.dtype), jax.ShapeDtypeStruct(bufv.shape, bufv.dtype)],
        input_output_aliases={9: 0, 10: 1},
        scratch_shapes=[pltpu.VMEM((rb, LANES), F32)] * 2,
        compiler_params=_cparams("parallel", "parallel"),
    )(_slope_table(), qkv, qkv, qkv, qkv, do, do, ld, ld, bufk, bufv)


def _place():
    x, y, c = lax.axis_index("x"), lax.axis_index("y"), lax.axis_index("c")
    chips = [(1 - x, y), (x, 1 - y), (1 - x, 1 - y)]
    return x, y, c, chips


def _gather_weights(wb):
    R = wb.shape[0]
    H = R // 2

    def body(w_ref, out_ref, send1, recv1, send2, recv2):
        x, y, c, chips = _place()
        me = 2 * x + y
        sib = (x, y, 1 - c)
        half = pl.ds(c * H, H)
        first = [pltpu.make_async_remote_copy(
            src_ref=w_ref.at[half], dst_ref=out_ref.at[me, half], send_sem=send1.at[k], recv_sem=recv1.at[k],
            device_id=(*chip, c), device_id_type=MESH) for k, chip in enumerate(chips)]
        for cp in first:
            cp.start()
        passed = []
        for k, (cx, cy) in enumerate(chips):
            slot = out_ref.at[2 * cx + cy, half]
            pltpu.make_async_remote_copy(src_ref=slot, dst_ref=slot, send_sem=send1.at[k], recv_sem=recv1.at[k],
                                         device_id=(cx, cy, c), device_id_type=MESH).wait_recv()
            cp = pltpu.make_async_remote_copy(src_ref=slot, dst_ref=slot, send_sem=send2.at[k], recv_sem=recv2.at[k],
                                              device_id=sib, device_id_type=MESH)
            cp.start()
            passed.append(cp)
        for k, (cx, cy) in enumerate(chips):
            slot = out_ref.at[2 * cx + cy, pl.ds((1 - c) * H, H)]
            pltpu.make_async_remote_copy(src_ref=slot, dst_ref=slot, send_sem=send2.at[k], recv_sem=recv2.at[k],
                                         device_id=sib, device_id_type=MESH).wait_recv()
        for cp in first + passed:
            cp.wait_send()

    return pl.pallas_call(
        body, name="gather_weights", in_specs=[ANY], out_specs=ANY,
        out_shape=jax.ShapeDtypeStruct((4, R, D_MODEL), wb.dtype),
        scratch_shapes=[pltpu.SemaphoreType.DMA((3,)), pltpu.SemaphoreType.DMA((3,)),
                        pltpu.SemaphoreType.DMA((3,)), pltpu.SemaphoreType.DMA((3,))],
    )(wb)


def _swap_halves(g):
    H = g.shape[1] // 2

    def body(g_ref, out_ref, send, recv):
        x, y, c, _ = _place()
        sib = (x, y, 1 - c)
        cps = [pltpu.make_async_remote_copy(
            src_ref=g_ref.at[s, pl.ds((1 - c) * H, H)], dst_ref=out_ref.at[s], send_sem=send.at[s],
            recv_sem=recv.at[s], device_id=sib, device_id_type=MESH) for s in range(4)]
        for cp in cps:
            cp.start()
        for cp in cps:
            cp.wait()

    return pl.pallas_call(
        body, name="swap_halves", in_specs=[ANY], out_specs=ANY,
        out_shape=jax.ShapeDtypeStruct((4, H, D_MODEL), g.dtype),
        scratch_shapes=[pltpu.SemaphoreType.DMA((4,)), pltpu.SemaphoreType.DMA((4,))],
    )(g)


def _scatter_partials(pb):
    def body(p_ref, out_ref, send, recv):
        x, y, c, chips = _place()
        me = 2 * x + y
        cps = [pltpu.make_async_remote_copy(
            src_ref=p_ref.at[2 * cx + cy], dst_ref=out_ref.at[me], send_sem=send.at[k], recv_sem=recv.at[k],
            device_id=(cx, cy, c), device_id_type=MESH) for k, (cx, cy) in enumerate(chips)]
        for cp in cps:
            cp.start()
        for k, (cx, cy) in enumerate(chips):
            slot = out_ref.at[2 * cx + cy]
            pltpu.make_async_remote_copy(src_ref=slot, dst_ref=slot, send_sem=send.at[k], recv_sem=recv.at[k],
                                         device_id=(cx, cy, c), device_id_type=MESH).wait_recv()
        for cp in cps:
            cp.wait_send()

    return pl.pallas_call(
        body, name="scatter_partials", in_specs=[ANY], out_specs=ANY,
        out_shape=jax.ShapeDtypeStruct(pb.shape, pb.dtype),
        scratch_shapes=[pltpu.SemaphoreType.DMA((3,)), pltpu.SemaphoreType.DMA((3,))],
    )(pb)


def _sibling_half(f, name):
    def body(f_ref, out_ref, send, recv):
        x, y, c, _ = _place()
        cp = pltpu.make_async_remote_copy(src_ref=f_ref, dst_ref=out_ref, send_sem=send, recv_sem=recv,
                                          device_id=(x, y, 1 - c), device_id_type=MESH)
        cp.start()
        cp.wait()

    return pl.pallas_call(
        body, name=name, in_specs=[ANY], out_specs=ANY,
        out_shape=jax.ShapeDtypeStruct(f.shape, f.dtype),
        scratch_shapes=[pltpu.SemaphoreType.DMA, pltpu.SemaphoreType.DMA],
    )(f)


def _gather_tiles(tile, name):
    m_per = tile.shape[0]

    def body(x_ref, out_ref, send_sems, recv_sems, local_sem):
        x, y, c, chips = _place()
        me, sibling = (x, y, c), (x, y, 1 - c)

        def rows(px, py, pc):
            return out_ref.at[pl.ds((4 * px + 2 * py + pc) * m_per, m_per), :]

        def copy(k, block, to, src=None):
            return pltpu.make_async_remote_copy(
                src_ref=rows(*block) if src is None else src, dst_ref=rows(*block),
                send_sem=send_sems.at[k], recv_sem=recv_sems.at[k], device_id=to, device_id_type=MESH)

        mine = pltpu.make_async_copy(x_ref, rows(*me), local_sem)
        mine.start()
        first = [copy(0, me, sibling, src=x_ref)]
        first += [copy(1 + j, me, (*chip, c), src=x_ref) for j, chip in enumerate(chips)]
        for cp in first:
            cp.start()
        passed = [copy(4 + j, (*chip, c), sibling) for j, chip in enumerate(chips)]
        for j, chip in enumerate(chips):
            copy(1 + j, (*chip, c), me).wait_recv()
            passed[j].start()
        copy(0, sibling, me).wait_recv()
        for j, chip in enumerate(chips):
            copy(4 + j, (*chip, 1 - c), me).wait_recv()
        for cp in first + passed:
            cp.wait_send()
        mine.wait()

    return pl.pallas_call(
        body, name=name, out_shape=jax.ShapeDtypeStruct((8 * m_per, D_MODEL), tile.dtype),
        in_specs=[pl.BlockSpec(memory_space=pltpu.VMEM)], out_specs=pl.BlockSpec(memory_space=pltpu.VMEM),
        scratch_shapes=[pltpu.SemaphoreType.DMA((7,)), pltpu.SemaphoreType.DMA((7,)), pltpu.SemaphoreType.DMA],
    )(tile)


def _pad_rows(a, rows):
    return jnp.pad(a, ((0, rows - a.shape[0]), (0, 0)))


def _pad_row(v):
    v = v.reshape(1, -1)
    return jnp.pad(v, ((0, 0), (0, D_MODEL - v.shape[1])))


def _pack_shard(fwi, fwo, dwi, dwo, pwu, pwg):
    parts = [_pad_rows(fwi.reshape(1028, D_MODEL), FLAT_A), fwo.reshape(256, D_MODEL), dwi.reshape(2560, D_MODEL),
             dwo.reshape(256, D_MODEL), pwu.reshape(128, D_MODEL), pwg.reshape(512, D_MODEL),
             jnp.zeros((FLAT_TOTAL - FLAT_USED, D_MODEL), fwi.dtype)]
    return jnp.concatenate(parts, axis=0)


def _unpack_shard(flat):
    out, r0 = {}, 0
    for name, rows, slot in FLAT_ROWS:
        out[name] = flat[r0:r0 + rows]
        r0 += slot
    return (out["fox_w_in"].reshape(1, D_MODEL, 1028), out["fox_w_out"].reshape(1, 256, D_MODEL),
            out["dil_w_in"].reshape(1, D_MODEL, 2560), out["dil_w_out"].reshape(1, 256, D_MODEL),
            out["ple_w_up"].reshape(2, PLE_DIM, 256), out["ple_w_gate"].reshape(2, 256, D_MODEL))


def _from_shard_columns(a, n):
    return a.reshape(4, D_MODEL, n).transpose(1, 0, 2).reshape(D_MODEL, 4 * n)


def _to_shard_columns(a, n):
    return a.reshape(D_MODEL, 4, n).transpose(1, 0, 2).reshape(4, n, D_MODEL)


def _unpack_a(wall_a):
    return _from_shard_columns(wall_a[:, :1028], 1028)


def _unpack_b(wall_b):
    out, r0 = {}, 0
    for name, rows, slot in FLAT_ROWS[1:]:
        out[name] = wall_b[:, r0:r0 + rows]
        r0 += slot
    pwu4 = out["ple_w_up"].reshape(4, 2, PLE_DIM, 256)
    pwg4 = out["ple_w_gate"].reshape(4, 2, 256, D_MODEL)
    pwu = [pwu4[:, i].transpose(1, 0, 2).reshape(PLE_DIM, D_MODEL) for i in range(2)]
    pwg = [pwg4[:, i].reshape(D_MODEL, D_MODEL) for i in range(2)]
    return (out["fox_w_out"].reshape(D_MODEL, D_MODEL), _from_shard_columns(out["dil_w_in"], 2560),
            out["dil_w_out"].reshape(D_MODEL, D_MODEL), pwu, pwg)


def _gathered_b_weights(wall_b, w_b, chip):
    return _unpack_b(lax.dynamic_update_slice(wall_b, w_b[None], (chip, 0, 0)))


def _pack_grads_a(gfwi):
    return jnp.pad(_to_shard_columns(gfwi, 1028), ((0, 0), (0, FLAT_A - 1028), (0, 0)))


def _pack_grads_b(gfwo, gdwi, gdwo, gpwu, gpwg):
    up = jnp.stack([a.reshape(PLE_DIM, 4, 256).transpose(1, 0, 2) for a in gpwu], axis=1)
    gate = jnp.stack([a.reshape(4, 256, D_MODEL) for a in gpwg], axis=1)
    parts = [gfwo.reshape(4, 256, D_MODEL), _to_shard_columns(gdwi, 2560), gdwo.reshape(4, 256, D_MODEL),
             up.reshape(4, 128, D_MODEL), gate.reshape(4, 512, D_MODEL)]
    return jnp.concatenate(parts, axis=1).astype(BF16)


def _local_step(x, p0, p1, tgt, fox_g, dil_g, fin_g, b_f, fwi, w_b, chip):
    S = x.shape[0]
    tm = min(512, S)
    w_qkv0 = jnp.concatenate([fwi[:, :D_MODEL] * 0.125, fwi[:, D_MODEL:3 * D_MODEL]], axis=1)
    w_z0 = fwi[:, 3 * D_MODEL:4 * D_MODEL]
    w_f0 = jnp.pad(fwi[:, 4 * D_MODEL:], ((0, 0), (0, LANES - FOX_HEADS)))
    b_full = jnp.pad(b_f.reshape(1, FOX_HEADS), ((0, 0), (0, LANES - FOX_HEADS)))

    n0, = _rows(_rms_fwd, [x], [(D_MODEL, BF16)], name="norm0", tm=tm, bcast=[fox_g])
    qkv0 = _mm(n0, w_qkv0, out_dtype=BF16, name="proj_qkv0")
    z0 = _mm(n0, w_z0, name="proj_z0")
    fl0 = _mm(n0, w_f0, name="proj_f0")
    qx0, kx0 = _cumsum_rows([fl0], name="forget_cumsum", width=LANES, pre=_log_forget, bcast=[b_full],
                            post=_fox_extras, outs=[(D_MODEL, BF16), (D_MODEL, BF16)])
    o0, a0, g0, wall_b = _fox_fwd(qkv0, qx0, kx0, z0, w_b)
    fwo, dwi, dwo, pwu, pwg = _gathered_b_weights(wall_b, w_b, chip)
    w_qkv1 = dwi[:, :3 * DIL_QKV]
    w_z1 = dwi[:, 3 * DIL_QKV:]
    h1 = _mm(g0, fwo, add=x, name="out_proj0")
    u0 = _mm(p0, pwu[0], name="ple_up0")
    t0 = _mm(h1, pwg[0], name="ple_gate0")
    h2, n1 = _rows(_ple_norm, [h1, u0, t0], [(D_MODEL, F32), (D_MODEL, BF16)], name="ple_mix0_norm1", tm=tm,
                   bcast=[dil_g])

    qkv1 = _mm(n1, w_qkv1, name="proj_qkv1")
    z1 = _mm(n1, w_z1, name="proj_z1")
    o1, g1, lse1 = _dil_fwd(qkv1, z1)
    h3 = _mm(g1, dwo, add=h2, name="out_proj1")
    u1 = _mm(p1, pwu[1], name="ple_up1")
    t1 = _mm(h3, pwg[1], name="ple_gate1")

    du1, dt1, dh4, g_fin, loss = _rows(
        _ple_loss, [h3, u1, t1, tgt], [(D_MODEL, BF16), (D_MODEL, BF16), (D_MODEL, F32)], name="ple_mix1_loss_head",
        tm=tm, bcast=[fin_g], acc=[((1, D_MODEL), F32), ((1, LANES), F32)])
    g_up1 = _mm(p1, du1, ta=True, out_dtype=BF16, name="grad_ple_up1")
    g_gate1 = _mm(h3, dt1, ta=True, out_dtype=BF16, name="grad_ple_gate1")
    dh3 = _mm(dt1, pwg[1], tb=True, add=dh4, name="ple_back1")
    dg1 = _mm(dh3, dwo, tb=True, name="out_back1")
    g_dwo = _mm(g1, dh3, ta=True, out_dtype=BF16, name="grad_out1")
    do1, dz1, ld1 = _rows(_gate_bwd, [dg1, o1, z1, lse1], [(D_MODEL, F32), (D_MODEL, BF16), (D_MODEL, F32)],
                          name="gate_bwd1", tm=tm)
    dq1, dk1, dv1 = (lax.empty((S, DIL_QKV), BF16) for _ in range(3))
    for g in range(DIL_GROUPS):
        dq1 = _dil_bwd_q(qkv1, do1, ld1, dq1, g)
        dk1, dv1 = _dil_bwd_kv(qkv1, do1, ld1, dk1, dv1, g)
    g_dwi = _mm_tn_parts(n1, [dq1, dk1, dv1, dz1], name="grad_in1", out_dtype=BF16)
    dn1 = _mm_nt_parts([dq1, dk1, dv1, dz1], dwi, name="in_back1")
    dh2, du0, dt0, g_dil = _rows(_norm_ple_bwd, [dn1, h2, dh3, u0, t0], [(D_MODEL, F32), (D_MODEL, BF16), (D_MODEL, BF16)],
                                 name="norm_bwd1_ple_bwd0", tm=tm, bcast=[dil_g], acc=[((1, D_MODEL), F32)])
    g_up0 = _mm(p0, du0, ta=True, out_dtype=BF16, name="grad_ple_up0")
    g_gate0 = _mm(h1, dt0, ta=True, out_dtype=BF16, name="grad_ple_gate0")
    dh1 = _mm(dt0, pwg[0], tb=True, add=dh2, name="ple_back0")
    dg0 = _mm(dh1, fwo, tb=True, name="out_back0")
    g_fwo = _mm(g0, dh1, ta=True, out_dtype=BF16, name="grad_out0")
    do0, dz0, qxa0, dx0 = _rows(_gate_bwd_fox, [dg0, o0, z0, a0], [(D_MODEL, BF16)] * 4, name="gate_bwd0", tm=tm)
    g_b = _pack_grads_b(g_fwo, g_dwi, g_dwo, [g_up0, g_up1], [g_gate0, g_gate1])
    dq0, drx, dk0, dv0, dkx, got_b = _fox_bwd(qkv0, do0, qxa0, dx0, kx0, g_b)
    dlogf, = _cumsum_rows([drx, dkx], name="forget_cumsum_bwd", width=LANES, reverse=True, pre=_forget_dc)
    df0, g_bf = _rows(_forget_bwd, [dlogf, fl0], [(LANES, BF16)], name="forget_bwd", tm=tm, bcast=[b_full],
                      acc=[((1, LANES), F32)])
    g_fwi_main = _mm_tn_parts(n0, [dq0, dk0, dv0, dz0], name="grad_in0")
    g_fwi_f = _mm(n0, df0, ta=True, name="grad_in0_forget")
    dn0 = _mm_nt_parts([dq0, dk0, dv0, dz0], fwi[:, :4 * D_MODEL], name="in_back0")
    dn0 = _mm(df0, w_f0, tb=True, add=dn0, name="in_back0_forget")
    dx, g_fox = _rows(_rms_bwd, [dn0, x, dh1], [(D_MODEL, F32)], name="norm_bwd0", tm=tm, bcast=[fox_g],
                      acc=[((1, D_MODEL), F32)])
    g_fwi = jnp.concatenate([g_fwi_main, g_fwi_f[:, :FOX_HEADS]], axis=1)
    return loss, dx, g_fwi, g_b, got_b, (g_fox, _pad_row(g_bf[:, :FOX_HEADS]), g_fin, g_dil)


def kernel(x, p, fox_norm, fox_w_in, fox_b_f, fox_w_out, dil_norm, dil_w_in, dil_w_out, ple_w_up, ple_w_gate, final_norm, loss_target, m_fox_norm, m_fox_w_in, m_fox_b_f, m_fox_w_out, m_dil_norm, m_dil_w_in, m_dil_w_out, m_ple_w_up, m_ple_w_gate, m_final_norm, v_fox_norm, v_fox_w_in, v_fox_b_f, v_fox_w_out, v_dil_norm, v_dil_w_in, v_dil_w_out, v_ple_w_up, v_ple_w_gate, v_final_norm):
    xi, yi, ci = lax.axis_index("x"), lax.axis_index("y"), lax.axis_index("c")
    chip = 2 * xi + yi

    w_flat = _pack_shard(*(w.astype(BF16) for w in (fox_w_in, fox_w_out, dil_w_in, dil_w_out, ple_w_up, ple_w_gate)))
    w_a, w_b = w_flat[:FLAT_A], w_flat[FLAT_A:FLAT_USED]
    fwi = _unpack_a(lax.dynamic_update_slice(_gather_weights(w_a), w_a[None], (chip, 0, 0)))
    gains = _gather_tiles(_pad_rows(_pad_row(dil_norm), 8), "gather_gain").reshape(4, 2, 8, D_MODEL)
    dil_g = gains[:, 0, 0, :256].reshape(1, D_MODEL)

    loss_part, dx, g_fwi, g_b, got_b, small = _local_step(
        x[0], p[0, 0], p[1, 0], loss_target[0], fox_norm.reshape(1, D_MODEL), dil_g,
        final_norm.reshape(1, D_MODEL), fox_b_f, fwi, w_b, chip)
    g_fox, g_bf, g_fin, g_dil = small
    sum4 = lambda a, b, c, d: ((a.astype(F32) + b.astype(F32)) + c.astype(F32)) + d.astype(F32)

    got_b = lax.dynamic_update_slice(got_b, lax.dynamic_slice_in_dim(g_b, chip, 1, axis=0), (chip, 0, 0))
    sum_b, = _rows(sum4, [got_b[s] for s in range(4)], [(D_MODEL, F32)], name="chip_sum_b", tm=FLAT_B // 8)
    g_flat_b, = _rows(lambda a, b: a + b, [sum_b, _sibling_half(sum_b, "sibling_sum_b")], [(D_MODEL, F32)],
                      name="pair_sum_b", tm=FLAT_B // 8)

    half = FLAT_A // 2
    g_a = _pack_grads_a(g_fwi)
    theirs = _swap_halves(g_a)
    mine = lax.dynamic_slice_in_dim(g_a, ci * half, half, axis=1)
    part, = _rows(lambda a, b: a + b, [mine.reshape(4 * half, D_MODEL), theirs.reshape(4 * half, D_MODEL)],
                  [(D_MODEL, BF16)], name="pair_sum", tm=half)
    part = part.reshape(4, half, D_MODEL)
    own = lax.dynamic_slice_in_dim(part, chip, 1, axis=0)
    by_chip = lax.dynamic_update_slice(_scatter_partials(part), own, (chip, 0, 0))
    half_sum, = _rows(sum4, [by_chip[s] for s in range(4)], [(D_MODEL, F32)], name="chip_sum", tm=half)
    other_half = _sibling_half(half_sum, "sibling_half")
    g_flat_a = jnp.where(ci == 0, jnp.concatenate([half_sum, other_half], axis=0),
                         jnp.concatenate([other_half, half_sum], axis=0))
    g_flat = jnp.concatenate([g_flat_a, g_flat_b, jnp.zeros((FLAT_TOTAL - FLAT_USED, D_MODEL), F32)], axis=0)

    tile = jnp.concatenate([g_fox, g_bf, g_fin, g_dil, jnp.pad(loss_part, ((0, 0), (0, D_MODEL - LANES))),
                            jnp.zeros((3, D_MODEL), F32)], axis=0)
    tiles = _gather_tiles(tile, "gather_small")
    tot, = _rows(lambda *t: functools.reduce(lambda a, b: a + b, t), [tiles[8 * s:8 * s + 8] for s in range(8)],
                 [(D_MODEL, F32)], name="small_sum", tm=8)
    loss = tot[4, 0]
    g_small = jnp.concatenate([
        tot[0:3],
        _pad_row(lax.dynamic_slice_in_dim(tot[3], chip * 256, 256)), jnp.zeros((4, D_MODEL), F32)], axis=0)
    small_tile = lambda a, b, c, d: jnp.concatenate(
        [_pad_row(a), _pad_row(b), _pad_row(c), _pad_row(d), jnp.zeros((4, D_MODEL), F32)], axis=0)
    w_small = small_tile(fox_norm, fox_b_f, final_norm, dil_norm)
    m_small = small_tile(m_fox_norm, m_fox_b_f, m_final_norm, m_dil_norm)
    v_small = small_tile(v_fox_norm, v_fox_b_f, v_final_norm, v_dil_norm)

    d_small, nm_small, nv_small = _rows(_adamw, [w_small, g_small, m_small, v_small], [(D_MODEL, F32)] * 3,
                                        name="adamw_small", tm=8)
    grads = _unpack_shard(g_flat)
    weights = (fox_w_in, fox_w_out, dil_w_in, dil_w_out, ple_w_up, ple_w_gate)
    firsts = (m_fox_w_in, m_fox_w_out, m_dil_w_in, m_dil_w_out, m_ple_w_up, m_ple_w_gate)
    seconds = (v_fox_w_in, v_fox_w_out, v_dil_w_in, v_dil_w_out, v_ple_w_up, v_ple_w_gate)
    big = []
    for (name, _, _), w, g, m, v in zip(FLAT_ROWS, weights, grads, firsts, seconds):
        width = w.shape[-1]
        flat2 = lambda a: a.reshape(-1, width)
        res = _rows(_adamw, [flat2(w), flat2(g), flat2(m), flat2(v)], [(width, F32)] * 3, name=f"adamw_{name}",
                    tm=128 if width > D_MODEL + LANES else 256)
        big.append([r.reshape(w.shape) for r in res])

    def leaves(shards, small_rows):
        fwi_, fwo_, dwi_, dwo_, pwu_, pwg_ = shards
        return (small_rows[0:1], fwi_, small_rows[1:2, :FOX_HEADS], fwo_, small_rows[3:4, :256], dwi_, dwo_,
                pwu_, pwg_, small_rows[2])

    per_kind = lambda k: [b[k] for b in big]
    return (loss, dx[None], *leaves(grads, g_small), *leaves(per_kind(0), d_small), *leaves(per_kind(1), nm_small),
            *leaves(per_kind(2), nv_small))
```

```python
import functools

import numpy as np
import jax
import jax.numpy as jnp
from jax import lax
from jax.experimental import pallas as pl
from jax.experimental.pallas import tpu as pltpu

F32 = jnp.float32
BF16 = jnp.bfloat16

D_MODEL = 1024
PLE_DIM = 256
FOX_HEADS = 16
FOX_HEAD_DIM = 64
DIL_PATTERN = ((128, 1), (512, 4), (2048, 16))
DIL_GROUPS = 3
DIL_HEADS_PER_GROUP = 8
DIL_HEAD_DIM = 128
DIL_WINDOW_STEPS = 128
DIL_QKV = 3072
ALIBI_MAX_EXP = 8.0
RMS_EPS = 1e-6
ADAM_LR, ADAM_B1, ADAM_B2, ADAM_EPS, ADAM_WD, ADAM_STEP = 0.001, 0.9, 0.999, 1e-08, 0.01, 10

LANES = 128
VMEM_LIMIT = 56 * 1024 * 1024
MESH = pl.DeviceIdType.MESH
ANY = pl.BlockSpec(memory_space=pl.ANY)
NEG = -1e30

FLAT_ROWS = (("fox_w_in", 1028, 1056), ("fox_w_out", 256, 256), ("dil_w_in", 2560, 2560), ("dil_w_out", 256, 256),
             ("ple_w_up", 128, 128), ("ple_w_gate", 512, 512))
FLAT_USED = sum(r for _, _, r in FLAT_ROWS)
FLAT_TOTAL = 4864
FLAT_TILE = 608
FLAT_A = FLAT_ROWS[0][2]
FLAT_B = FLAT_USED - FLAT_A


def _cparams(*sem):
    return pltpu.CompilerParams(dimension_semantics=sem, vmem_limit_bytes=VMEM_LIMIT)


def _sigmoid(x):
    return 1.0 / (1.0 + jnp.exp(-x))


def _mm(a, b, *, name, ta=False, tb=False, out_dtype=F32, add=None):
    if ta:
        K, M = a.shape
    else:
        M, K = a.shape
    if tb:
        N, Kb = b.shape
    else:
        Kb, N = b.shape
    assert K == Kb, (a.shape, b.shape)
    tm, tn, tk = min(M, 1024), min(N, 1024), min(K, 1024)
    assert M % tm == 0 and N % tn == 0 and K % tk == 0, (M, N, K)
    nk = K // tk
    dn = (((0 if ta else 1,), (1 if tb else 0,)), ((), ()))

    def body(*refs):
        if add is None:
            a_ref, b_ref, o_ref, acc = refs
        else:
            a_ref, b_ref, add_ref, o_ref, acc = refs
        k = pl.program_id(2)

        @pl.when(k == 0)
        def _():
            acc[...] = jnp.zeros_like(acc)

        acc[...] += lax.dot_general(a_ref[...].astype(BF16), b_ref[...].astype(BF16), dn,
                                    preferred_element_type=F32)

        @pl.when(k == nk - 1)
        def _():
            r = acc[...]
            if add is not None:
                r = r + add_ref[...]
            o_ref[...] = r.astype(out_dtype)

    a_spec = (pl.BlockSpec((tk, tm), lambda i, j, k: (k, i)) if ta
              else pl.BlockSpec((tm, tk), lambda i, j, k: (i, k)))
    b_spec = (pl.BlockSpec((tn, tk), lambda i, j, k: (j, k)) if tb
              else pl.BlockSpec((tk, tn), lambda i, j, k: (k, j)))
    in_specs = [a_spec, b_spec]
    args = [a, b]
    if add is not None:
        in_specs.append(pl.BlockSpec((tm, tn), lambda i, j, k: (i, j)))
        args.append(add)
    return pl.pallas_call(
        body, name=name, grid=(M // tm, N // tn, nk),
        in_specs=in_specs, out_specs=pl.BlockSpec((tm, tn), lambda i, j, k: (i, j)),
        out_shape=jax.ShapeDtypeStruct((M, N), out_dtype),
        scratch_shapes=[pltpu.VMEM((tm, tn), F32)],
        compiler_params=_cparams("parallel", "parallel", "arbitrary"),
    )(*args)


def _mm_nt_parts(a_parts, b, *, name, scatter=None):
    M = a_parts[0].shape[0]
    N, K = b.shape
    tm, tn, tk = min(M, 1024), min(N, 1024), 1024
    nks = [a.shape[1] // tk for a in a_parts]
    offs = [sum(nks[:p]) for p in range(len(nks))]
    nk = sum(nks)
    assert nk * tk == K and M % tm == 0 and N % tn == 0, (M, N, K)
    n_parts = len(a_parts)

    grid = (M // tm, N // tn, nk)

    def body(*refs):
        a_refs, b_ref = refs[:n_parts], refs[n_parts]
        if scatter is None:
            o_ref, acc = refs[n_parts + 1:]
        else:
            part_ref, o_ref, got_ref, acc, send, recv = refs[n_parts + 1:]
            start, wait = _chip_exchange(lambda cx, cy: part_ref.at[2 * cx + cy], got_ref, send, recv)
            at = lambda step: functools.reduce(
                lambda x, y: x & y, [pl.program_id(ax) == (0 if step == 0 else n - 1) for ax, n in enumerate(grid)])
            pl.when(at(0))(start)
        k = pl.program_id(2)

        @pl.when(k == 0)
        def _():
            acc[...] = jnp.zeros_like(acc)

        for a_ref, off, n in zip(a_refs, offs, nks):
            @pl.when((k >= off) & (k < off + n))
            def _(a_ref=a_ref):
                acc[...] += lax.dot_general(a_ref[...].astype(BF16), b_ref[...].astype(BF16),
                                            (((1,), (1,)), ((), ())), preferred_element_type=F32)

        @pl.when(k == nk - 1)
        def _():
            o_ref[...] = acc[...]

        if scatter is not None:
            pl.when(at(-1))(wait)

    a_specs = [pl.BlockSpec((tm, tk), lambda i, j, k, off=off, n=n: (i, jnp.clip(k - off, 0, n - 1)))
               for off, n in zip(offs, nks)]
    in_specs = a_specs + [pl.BlockSpec((tn, tk), lambda i, j, k: (j, k))]
    out_specs = [pl.BlockSpec((tm, tn), lambda i, j, k: (i, j))]
    out_shape = [jax.ShapeDtypeStruct((M, N), F32)]
    scratch = [pltpu.VMEM((tm, tn), F32)]
    args = [*a_parts, b]
    if scatter is not None:
        in_specs, out_specs = in_specs + [ANY], out_specs + [ANY]
        out_shape.append(jax.ShapeDtypeStruct(scatter.shape, scatter.dtype))
        scratch += [pltpu.SemaphoreType.DMA((3,)), pltpu.SemaphoreType.DMA((3,))]
        args.append(scatter)
    res = pl.pallas_call(
        body, name=name, grid=grid, in_specs=in_specs, out_specs=out_specs, out_shape=out_shape,
        scratch_shapes=scratch,
        compiler_params=_cparams(*(("arbitrary",) * 3 if scatter is not None else ("parallel", "parallel", "arbitrary"))),
    )(*args)
    return res[0] if scatter is None else res


def _mm_tn_parts(a, b_parts, *, name, out_dtype=F32):
    K, M = a.shape
    tm, tn, tk = min(M, 1024), 1024, min(K, 1024)
    njs = [b.shape[1] // tn for b in b_parts]
    offs = [sum(njs[:p]) for p in range(len(njs))]
    nj, nk = sum(njs), K // tk
    assert M % tm == 0 and K % tk == 0 and all(b.shape[1] % tn == 0 for b in b_parts)
    n_parts = len(b_parts)

    def body(*refs):
        a_ref, b_refs, o_ref, acc = refs[0], refs[1:1 + n_parts], refs[1 + n_parts], refs[2 + n_parts]
        j, k = pl.program_id(1), pl.program_id(2)

        @pl.when(k == 0)
        def _():
            acc[...] = jnp.zeros_like(acc)

        for b_ref, off, n in zip(b_refs, offs, njs):
            @pl.when((j >= off) & (j < off + n))
            def _(b_ref=b_ref):
                acc[...] += lax.dot_general(a_ref[...].astype(BF16), b_ref[...].astype(BF16),
                                            (((0,), (0,)), ((), ())), preferred_element_type=F32)

        @pl.when(k == nk - 1)
        def _():
            o_ref[...] = acc[...].astype(out_dtype)

    def b_spec(off, n):
        def index(i, j, k):
            mine = (j >= off) & (j < off + n)
            return jnp.where(mine, k, 0), jnp.clip(j - off, 0, n - 1)
        return pl.BlockSpec((tk, tn), index)

    return pl.pallas_call(
        body, name=name, grid=(M // tm, nj, nk),
        in_specs=[pl.BlockSpec((tk, tm), lambda i, j, k: (k, i))] + [b_spec(off, n) for off, n in zip(offs, njs)],
        out_specs=pl.BlockSpec((tm, tn), lambda i, j, k: (i, j)),
        out_shape=jax.ShapeDtypeStruct((M, nj * tn), out_dtype), scratch_shapes=[pltpu.VMEM((tm, tn), F32)],
        compiler_params=_cparams("parallel", "parallel", "arbitrary"),
    )(a, *b_parts)


def _rows(fn, ins, outs, *, name, tm, bcast=(), acc=()):
    R = ins[0].shape[0]
    assert R % tm == 0, (R, tm)
    n_in, n_b, n_out, n_acc = len(ins), len(bcast), len(outs), len(acc)

    def body(*refs):
        in_refs = refs[:n_in + n_b]
        out_refs = refs[n_in + n_b:n_in + n_b + n_out]
        acc_refs = refs[n_in + n_b + n_out:]
        res = fn(*[r[...] for r in in_refs])
        if not isinstance(res, (tuple, list)):
            res = (res,)
        for r, v in zip(out_refs, res[:n_out]):
            r[...] = v.astype(r.dtype)
        first = pl.program_id(0) == 0
        for r, v in zip(acc_refs, res[n_out:]):
            @pl.when(first)
            def _(r=r, v=v):
                r[...] = v.astype(r.dtype)

            @pl.when(jnp.logical_not(first))
            def _(r=r, v=v):
                r[...] += v.astype(r.dtype)

    in_specs = [pl.BlockSpec((tm, a.shape[1]), lambda i: (i, 0)) for a in ins]
    in_specs += [pl.BlockSpec(b.shape, lambda i, nd=b.ndim: (0,) * nd) for b in bcast]
    out_specs = [pl.BlockSpec((tm, c), lambda i: (i, 0)) for c, _ in outs]
    out_specs += [pl.BlockSpec(s, lambda i, nd=len(s): (0,) * nd) for s, _ in acc]
    out_shape = [jax.ShapeDtypeStruct((R, c), dt) for c, dt in outs]
    out_shape += [jax.ShapeDtypeStruct(s, dt) for s, dt in acc]
    res = pl.pallas_call(
        body, name=name, grid=(R // tm,), in_specs=in_specs, out_specs=out_specs, out_shape=out_shape,
        compiler_params=_cparams("arbitrary" if acc else "parallel"),
    )(*ins, *bcast)
    return res


def _exact_dot(sel, x, left):
    hi = x.astype(BF16)
    r1 = x - hi.astype(F32)
    mid = r1.astype(BF16)
    lo = (r1 - mid.astype(F32)).astype(BF16)
    dot = (lambda p: jnp.dot(sel, p, preferred_element_type=F32)) if left else (
        lambda p: jnp.dot(p, sel, preferred_element_type=F32))
    return dot(hi) + dot(mid) + dot(lo)


def _cumsum_rows(ins, *, name, width, reverse=False, pre=None, bcast=(), post=None, outs=None):
    S = ins[0].shape[0]
    outs = outs or [(width, F32)]
    out_dtypes = [dt for _, dt in outs]
    tb = 256
    nb = S // tb
    assert S % tb == 0
    n_in = len(ins) + len(bcast)
    n_out = len(out_dtypes)

    def body(*refs):
        in_refs, o_refs, carry = refs[:n_in], refs[n_in:n_in + n_out], refs[n_in + n_out]

        @pl.when(pl.program_id(0) == 0)
        def _():
            carry[...] = jnp.zeros_like(carry)

        xv = in_refs[0][...] if pre is None else pre(*[r[...] for r in in_refs])
        r_ = lax.broadcasted_iota(jnp.int32, (tb, tb), 0)
        c_ = lax.broadcasted_iota(jnp.int32, (tb, tb), 1)
        tri = jnp.where((c_ >= r_) if reverse else (c_ <= r_), 1.0, 0.0).astype(BF16)
        cs = _exact_dot(tri, xv, left=True) + carry[...]
        outs = (cs,) if post is None else post(cs)
        for o_ref, v in zip(o_refs, outs):
            o_ref[...] = v.astype(o_ref.dtype)
        carry[...] = cs[0:1, :] if reverse else cs[tb - 1:tb, :]

    blk = (lambda i: (nb - 1 - i, 0)) if reverse else (lambda i: (i, 0))
    in_specs = [pl.BlockSpec((tb, a.shape[1]), blk) for a in ins]
    in_specs += [pl.BlockSpec(b.shape, lambda i, nd=b.ndim: (0,) * nd) for b in bcast]
    return pl.pallas_call(
        body, name=name, grid=(nb,), in_specs=in_specs, out_specs=[pl.BlockSpec((tb, c), blk) for c, _ in outs],
        out_shape=[jax.ShapeDtypeStruct((S, c), dt) for c, dt in outs], scratch_shapes=[pltpu.VMEM((1, width), F32)],
        compiler_params=_cparams("arbitrary"),
    )(*ins, *bcast)


def _rms_fwd(x, g):
    r = lax.rsqrt(jnp.mean(x * x, axis=1, keepdims=True) + RMS_EPS)
    return (x * r) * g


def _rms_bwd(dn, x, dres, g):
    r = lax.rsqrt(jnp.mean(x * x, axis=1, keepdims=True) + RMS_EPS)
    xh = x * r
    w = dn * g
    dx = r * (w - xh * jnp.mean(w * xh, axis=1, keepdims=True))
    return dres + dx, jnp.sum(dn * xh, axis=0, keepdims=True)


def _final_stage(h, tgt, g):
    r = lax.rsqrt(jnp.mean(h * h, axis=1, keepdims=True) + RMS_EPS)
    xh = h * r
    diff = xh * g - tgt
    loss = 0.5 * jnp.sum(jnp.mean(diff * diff, axis=1, keepdims=True), axis=0, keepdims=True)
    dy = diff * (1.0 / D_MODEL)
    w = dy * g
    dh = r * (w - xh * jnp.mean(w * xh, axis=1, keepdims=True))
    return dh, jnp.sum(dy * xh, axis=0, keepdims=True), jnp.broadcast_to(loss, (1, LANES))


def _ple_fwd(h, u, t):
    return h + u * _sigmoid(t)


def _ple_bwd(dh, u, t):
    s = _sigmoid(t)
    return dh * s, dh * u * s * (1.0 - s)


def _ple_norm(h, u, t, g):
    h2 = _ple_fwd(h, u, t)
    return h2, _rms_fwd(h2, g)


def _ple_loss(h, u, t, tgt, g):
    dh, g_gain, loss = _final_stage(_ple_fwd(h, u, t), tgt, g)
    du, dt = _ple_bwd(dh, u, t)
    return du, dt, dh, g_gain, loss


def _norm_ple_bwd(dn, x, dres, u, t, g):
    dh, g_gain = _rms_bwd(dn, x, dres, g)
    du, dt = _ple_bwd(dh, u, t)
    return dh, du, dt, g_gain


def _gate_fwd(o, z):
    return o * (z * _sigmoid(z))


def _head_sums(prod):
    tm, width = prod.shape
    cols = [jnp.broadcast_to(jnp.sum(prod[:, b * LANES:(b + 1) * LANES], axis=1, keepdims=True), (tm, LANES))
            for b in range(width // LANES)]
    return jnp.concatenate(cols, axis=1)


def _gate_bwd(dg, o, z, lse):
    s = _sigmoid(z)
    do = dg * (z * s)
    dz = dg * o * (s * (1.0 + z * (1.0 - s)))
    lane = lax.broadcasted_iota(jnp.int32, do.shape, 1) % LANES
    return do, dz, jnp.where(lane < LANES // 2, lse, _head_sums(do * o))


def _log_forget(fl, b):
    u = fl + b
    return jnp.minimum(u, 0.0) - jnp.log(1.0 + jnp.exp(-jnp.abs(u)))


def _adamw(w, g, m, v):
    m = ADAM_B1 * m + (1.0 - ADAM_B1) * g
    v = ADAM_B2 * v + (1.0 - ADAM_B2) * (g * g)
    m_hat = m / (1.0 - ADAM_B1 ** ADAM_STEP)
    v_hat = v / (1.0 - ADAM_B2 ** ADAM_STEP)
    delta = -ADAM_LR * (m_hat / (jnp.sqrt(v_hat) + ADAM_EPS) + ADAM_WD * w)
    return delta, m, v


FOX_TQ = 1024


def _split3(x):
    p1 = x.astype(BF16).astype(F32)
    r = x - p1
    p2 = r.astype(BF16).astype(F32)
    return p1, p2, r - p2


def _lane_in_head(shape):
    return lax.broadcasted_iota(jnp.int32, shape, 1) % FOX_HEAD_DIM


def _query_extras(x):
    lm = _lane_in_head(x.shape)
    p1, p2, p3 = _split3(x)
    return jnp.where(lm == 0, p1, jnp.where(lm == 1, p2, jnp.where(lm == 2, p3, jnp.where(lm < 6, 1.0, 0.0))))


def _key_extras(c):
    lm = _lane_in_head(c.shape)
    p1, p2, p3 = _split3(c)
    return jnp.where(lm < 3, 1.0, jnp.where(lm == 3, -p1, jnp.where(lm == 4, -p2, jnp.where(lm == 5, -p3, 0.0))))


def _head_lane_base(head):
    return LANES * (head // 2) + FOX_HEAD_DIM * (1 - head % 2)


def _fox_extras(c):
    head = lax.broadcasted_iota(jnp.int32, (LANES, D_MODEL), 0)
    lane = lax.broadcasted_iota(jnp.int32, (LANES, D_MODEL), 1)
    owner = 2 * (lane // LANES) + 1 - (lane % LANES) // FOX_HEAD_DIM
    wide = _exact_dot(jnp.where(head == owner, 1.0, 0.0).astype(BF16), c, left=False)
    return _query_extras(wide), _key_extras(wide)


def _swapped_head_sums(prod):
    tm, width = prod.shape
    lane = lax.broadcasted_iota(jnp.int32, (tm, LANES), 1)
    low = lane < FOX_HEAD_DIM
    cols = []
    for b in range(width // LANES):
        blk = prod[:, b * LANES:(b + 1) * LANES]
        sa = jnp.sum(jnp.where(low, blk, 0.0), axis=1, keepdims=True)
        sb = jnp.sum(jnp.where(low, 0.0, blk), axis=1, keepdims=True)
        cols.append(jnp.where(low, sb, sa))
    return jnp.concatenate(cols, axis=1)


def _gate_bwd_fox(dg, o, z, a):
    s = _sigmoid(z)
    do = dg * (z * s)
    dz = dg * o * (s * (1.0 + z * (1.0 - s)))
    lm = _lane_in_head(do.shape)
    d1, d2, d3 = _split3(-_swapped_head_sums(do * o))
    dx = jnp.where(lm == 0, d1, jnp.where(lm == 1, d2, jnp.where(lm == 2, d3, 0.0)))
    return do, dz, _query_extras(a), dx


def _forget_dc(drx, dkx):
    lane = lax.broadcasted_iota(jnp.int32, (D_MODEL, LANES), 0)
    head = lax.broadcasted_iota(jnp.int32, (D_MODEL, LANES), 1)
    base = _head_lane_base(head)
    pick = lambda l: jnp.where((lane == base + l) & (head < FOX_HEADS), 1.0, 0.0).astype(BF16)
    return _exact_dot(pick(0), drx, left=False) - _exact_dot(pick(3), dkx, left=False)


def _forget_bwd(dl, fl, b):
    du = dl * (1.0 / (1.0 + jnp.exp(fl + b)))
    return du, jnp.sum(du, axis=0, keepdims=True)


def _chip_exchange(src_of, out_ref, send, recv):
    x, y, c = lax.axis_index("x"), lax.axis_index("y"), lax.axis_index("c")
    chips = [(1 - x, y), (x, 1 - y), (1 - x, 1 - y)]
    me = 2 * x + y
    sends = [pltpu.make_async_remote_copy(src_ref=src_of(cx, cy), dst_ref=out_ref.at[me], send_sem=send.at[k],
                                          recv_sem=recv.at[k], device_id=(cx, cy, c), device_id_type=MESH)
             for k, (cx, cy) in enumerate(chips)]

    def start():
        for cp in sends:
            cp.start()

    def wait():
        for k, (cx, cy) in enumerate(chips):
            slot = out_ref.at[2 * cx + cy]
            pltpu.make_async_remote_copy(src_ref=slot, dst_ref=slot, send_sem=send.at[k], recv_sem=recv.at[k],
                                         device_id=(cx, cy, c), device_id_type=MESH).wait_recv()
        for cp in sends:
            cp.wait_send()

    return start, wait


def _fox_fwd(qkv, qx, kx, z, w_b):
    S = qkv.shape[0]
    tq = min(FOX_TQ, S)
    nq = S // tq
    nt = (((1,), (1,)), ((), ()))

    def body(q_ref, qx_ref, k_ref, v_ref, kx_ref, z_ref, wb_ref, o_ref, a_ref, g_ref, wall_ref, send, recv):
        i = pl.program_id(1)
        start, wait = _chip_exchange(lambda cx, cy: wb_ref, wall_ref, send, recv)
        pl.when((pl.program_id(0) == 0) & (i == 0))(start)
        low = lax.broadcasted_iota(jnp.int32, (tq, LANES), 1) < FOX_HEAD_DIM
        row = lax.broadcasted_iota(jnp.int32, (tq, tq), 0)
        col = lax.broadcasted_iota(jnp.int32, (tq, tq), 1)
        q2, x2 = q_ref[...], qx_ref[...]
        qa = (jnp.where(low, q2, x2), jnp.where(low, x2, q2))

        def step(kb, carry, diag):
            start = pl.multiple_of(kb * tq, tq)
            k2 = k_ref[pl.ds(start, tq), :]
            v2 = v_ref[pl.ds(start, tq), :]
            y2 = kx_ref[pl.ds(start, tq), :]
            one = jnp.ones_like(v2)
            ka = (jnp.where(low, k2, y2), jnp.where(low, y2, k2))
            va = (jnp.where(low, v2, one), jnp.where(low, one, v2))
            out = []
            for hh in range(2):
                m, acc = carry[hh]
                s = lax.dot_general(qa[hh], ka[hh], nt, preferred_element_type=F32)
                if diag:
                    s = jnp.where(col <= row, s, NEG)
                m_new = jnp.maximum(m, jnp.max(s, axis=1, keepdims=True))
                pr = jnp.exp(s - m_new)
                acc = jnp.exp(m - m_new) * acc + jnp.dot(pr.astype(BF16), va[hh], preferred_element_type=F32)
                out.append((m_new, acc))
            return tuple(out)

        init = ((jnp.full((tq, 1), NEG, F32), jnp.zeros((tq, LANES), F32)),) * 2
        carry = lax.fori_loop(0, i, functools.partial(step, diag=False), init)
        (m_a, acc_a), (m_b, acc_b) = step(i, carry, True)
        l_a, l_b = acc_a[:, FOX_HEAD_DIM:FOX_HEAD_DIM + 1], acc_b[:, 0:1]
        xf = x2.astype(F32)
        c_a = xf[:, 64:65] + xf[:, 65:66] + xf[:, 66:67]
        c_b = xf[:, 0:1] + xf[:, 1:2] + xf[:, 2:3]
        o = jnp.where(low, acc_a / l_a, acc_b / l_b)
        o_ref[...] = o
        g_ref[...] = _gate_fwd(o, z_ref[...]).astype(BF16)
        a_ref[...] = jnp.where(low, jnp.broadcast_to(c_b - (m_b + jnp.log(l_b)), (tq, LANES)),
                               jnp.broadcast_to(c_a - (m_a + jnp.log(l_a)), (tq, LANES)))
        pl.when((pl.program_id(0) == FOX_HEADS // 2 - 1) & (i == nq - 1))(wait)

    blk = lambda cb: pl.BlockSpec((tq, LANES), lambda h, i, cb=cb: (i, cb + h))
    res = lambda cb: pl.BlockSpec((S, LANES), lambda h, i, cb=cb: (0, cb + h))
    return pl.pallas_call(
        body, name="fox_attn_fwd", grid=(FOX_HEADS // 2, nq),
        in_specs=[blk(0), blk(0), res(8), res(16), res(0), blk(0), ANY],
        out_specs=[blk(0), blk(0), blk(0), ANY],
        out_shape=[jax.ShapeDtypeStruct((S, D_MODEL), F32), jax.ShapeDtypeStruct((S, D_MODEL), F32),
                   jax.ShapeDtypeStruct((S, D_MODEL), BF16), jax.ShapeDtypeStruct((4,) + w_b.shape, w_b.dtype)],
        scratch_shapes=[pltpu.SemaphoreType.DMA((3,)), pltpu.SemaphoreType.DMA((3,))],
        compiler_params=_cparams("arbitrary", "arbitrary"),
    )(qkv, qx, qkv, qkv, kx, z, w_b)


def _fox_bwd(qkv, do, qxa, dx, kx, g_b):
    S = qkv.shape[0]
    tq = min(FOX_TQ, S)
    nq = S // tq
    nt = (((1,), (1,)), ((), ()))
    tn = (((0,), (0,)), ((), ()))

    def body(q_ref, qx_ref, do_ref, dx_ref, k_ref, v_ref, kx_ref, gb_ref, dq_ref, dr_ref, dk_ref, dv_ref, dkx_ref,
             got_ref, send, recv):
        kb = pl.program_id(1)
        start, wait = _chip_exchange(lambda cx, cy: gb_ref.at[2 * cx + cy], got_ref, send, recv)
        pl.when((pl.program_id(0) == 0) & (kb == 0))(start)
        low = lax.broadcasted_iota(jnp.int32, (tq, LANES), 1) < FOX_HEAD_DIM
        row = lax.broadcasted_iota(jnp.int32, (tq, tq), 0)
        col = lax.broadcasted_iota(jnp.int32, (tq, tq), 1)

        @pl.when(kb == 0)
        def _():
            dq_ref[...] = jnp.zeros_like(dq_ref)
            dr_ref[...] = jnp.zeros_like(dr_ref)

        k2, v2, y2 = k_ref[...], v_ref[...], kx_ref[...]
        one = jnp.ones_like(v2)
        ka = (jnp.where(low, k2, y2), jnp.where(low, y2, k2))
        va = (jnp.where(low, v2, one), jnp.where(low, one, v2))

        def step(qb, carry, diag):
            start = pl.multiple_of(qb * tq, tq)
            q2 = q_ref[pl.ds(start, tq), :]
            x2 = qx_ref[pl.ds(start, tq), :]
            d2 = do_ref[pl.ds(start, tq), :]
            e2 = dx_ref[pl.ds(start, tq), :]
            qa = (jnp.where(low, q2, x2), jnp.where(low, x2, q2))
            da = (jnp.where(low, d2, e2), jnp.where(low, e2, d2))
            new, res = [], []
            for hh in range(2):
                dk, dv = carry[hh]
                s = lax.dot_general(qa[hh], ka[hh], nt, preferred_element_type=F32)
                if diag:
                    s = jnp.where(col <= row, s, NEG)
                pr = jnp.exp(s)
                ds = pr * lax.dot_general(da[hh], va[hh], nt, preferred_element_type=F32)
                prb, dsb = pr.astype(BF16), ds.astype(BF16)
                dv = dv + lax.dot_general(prb, da[hh], tn, preferred_element_type=F32)
                dk = dk + lax.dot_general(dsb, qa[hh], tn, preferred_element_type=F32)
                res.append(jnp.dot(dsb, ka[hh], preferred_element_type=F32))
                new.append((dk, dv))
            dq_ref[pl.ds(start, tq), :] += jnp.where(low, res[0], res[1])
            dr_ref[pl.ds(start, tq), :] += jnp.where(low, res[1], res[0])
            return tuple(new)

        init = ((jnp.zeros((tq, LANES), F32), jnp.zeros((tq, LANES), F32)),) * 2
        carry = step(kb, init, True)
        (dk_a, dv_a), (dk_b, dv_b) = lax.fori_loop(kb + 1, nq, functools.partial(step, diag=False), carry)
        dk_ref[...] = jnp.where(low, dk_a, dk_b).astype(BF16)
        dv_ref[...] = jnp.where(low, dv_a, dv_b).astype(BF16)
        dkx_ref[...] = jnp.where(low, dk_b, dk_a)

        @pl.when(kb == nq - 1)
        def _():
            dq_ref[...] = dq_ref[...] * (FOX_HEAD_DIM ** -0.5)

        pl.when((pl.program_id(0) == FOX_HEADS // 2 - 1) & (kb == nq - 1))(wait)

    res = lambda cb: pl.BlockSpec((S, LANES), lambda h, k, cb=cb: (0, cb + h))
    blk = lambda cb: pl.BlockSpec((tq, LANES), lambda h, k, cb=cb: (k, cb + h))
    f32, b16 = jax.ShapeDtypeStruct((S, D_MODEL), F32), jax.ShapeDtypeStruct((S, D_MODEL), BF16)
    return pl.pallas_call(
        body, name="fox_attn_bwd", grid=(FOX_HEADS // 2, nq),
        in_specs=[res(0), res(0), res(0), res(0), blk(8), blk(16), blk(0), ANY],
        out_specs=[res(0), res(0), blk(0), blk(0), blk(0), ANY],
        out_shape=[f32, f32, b16, b16, f32, jax.ShapeDtypeStruct(g_b.shape, g_b.dtype)],
        scratch_shapes=[pltpu.SemaphoreType.DMA((3,)), pltpu.SemaphoreType.DMA((3,))],
        compiler_params=_cparams("arbitrary", "arbitrary"),
    )(qkv, qxa, do, dx, qkv, qkv, kx, g_b)


def _alibi_slopes():
    n = DIL_GROUPS * DIL_HEADS_PER_GROUP
    s = np.float32(2.0) ** (np.float32(-ALIBI_MAX_EXP) * np.arange(1, n + 1, dtype=np.float32) / np.float32(n))
    return s.astype(np.float32).reshape(DIL_GROUPS, DIL_HEADS_PER_GROUP)


W = DIL_WINDOW_STEPS
DIL_SCALE = DIL_HEAD_DIM ** -0.5


DIL_ROWS = 2048
DIL_BWD_ROWS = 4096
NT = (((1,), (1,)), ((), ()))
TN = (((0,), (0,)), ((), ()))
QKV_BLOCKS = DIL_QKV // LANES


def _slope_table():
    t = np.zeros((DIL_HEADS_PER_GROUP, 8, LANES), np.float32)
    t[:, :DIL_GROUPS, :] = _alibi_slopes().T[:, :, None]
    return jnp.asarray(t)


def _phase_rows(start, d):
    return pl.ds(start, W, stride=d) if d > 1 else pl.ds(start, W)


def _for_phases(d, unit):
    if d == 1:
        unit(0)
    else:
        lax.fori_loop(0, d, lambda r, c: (unit(r), c)[1], 0)


def _window_geometry(shape, q_axis_offset):
    i_ = lax.broadcasted_iota(jnp.int32, shape, 0)
    j_ = lax.broadcasted_iota(jnp.int32, shape, 1)
    dist = q_axis_offset + i_ - j_
    return i_, j_, dist, (dist >= 0) & (dist <= W)


def _dil_fwd(qkv, z):
    S = qkv.shape[0]
    rb = min(DIL_ROWS, S)
    nb = S // rb
    spans = [W * d for _, d in DIL_PATTERN]

    def body(*refs):
        slope_ref = refs[0]
        q, kc, kp, vc, vp = refs[1:4], refs[4:7], refs[7:10], refs[10:13], refs[13:16]
        z_ref, o_ref, g_ref, l_ref = refs[16:20]
        og, lg = refs[20:23], refs[23:26]
        i = pl.program_id(1)
        _, j_, dist, inwin = _window_geometry((W, 2 * W), W)
        distf = dist.astype(F32)
        for g, (_, d) in enumerate(DIL_PATTERN):
            span = spans[g]
            bias = -(slope_ref[0, g:g + 1, 0:1] * float(d)) * distf
            def phase(r, g=g, d=d, span=span, bias=bias):
                kprev, vprev = kp[g][_phase_rows(r, d), :].astype(BF16), vp[g][_phase_rows(r, d), :].astype(BF16)
                for b in range(rb // span):
                    rows = _phase_rows(b * span + r, d)
                    kcur, vcur = kc[g][rows, :].astype(BF16), vc[g][rows, :].astype(BF16)
                    valid = inwin if b > 0 else inwin & ((j_ >= W) | (i > 0))
                    k2 = jnp.concatenate([kprev, kcur], axis=0)
                    v2 = jnp.concatenate([vprev, vcur], axis=0)
                    s = lax.dot_general(q[g][rows, :].astype(BF16), k2, NT, preferred_element_type=F32)
                    s = jnp.where(valid, s * DIL_SCALE + bias, NEG)
                    m = jnp.max(s, axis=1, keepdims=True)
                    pr = jnp.exp(s - m)
                    l = jnp.sum(pr, axis=1, keepdims=True)
                    og[g][rows, :] = jnp.dot(pr.astype(BF16), v2, preferred_element_type=F32) / l
                    lg[g][rows, :] = jnp.broadcast_to(m + jnp.log(l), (W, LANES))
                    kprev, vprev = kcur, vcur

            _for_phases(d, phase)

        def mix(cix, c):
            sl = pl.ds(pl.multiple_of(cix * 256, 256), 256)
            l1, l2, l3 = lg[0][sl, :], lg[1][sl, :], lg[2][sl, :]
            m = jnp.maximum(jnp.maximum(l1, l2), l3)
            e1, e2, e3 = jnp.exp(l1 - m), jnp.exp(l2 - m), jnp.exp(l3 - m)
            tot = e1 + e2 + e3
            o = (e1 * og[0][sl, :] + e2 * og[1][sl, :] + e3 * og[2][sl, :]) / tot
            o_ref[sl, :] = o
            g_ref[sl, :] = _gate_fwd(o, z_ref[sl, :]).astype(BF16)
            l_ref[sl, :] = m + jnp.log(tot)
            return c

        lax.fori_loop(0, rb // 256, mix, 0)

    cur = lambda off, g: pl.BlockSpec((rb, LANES), lambda h, i: (i, off * QKV_BLOCKS + g * DIL_HEADS_PER_GROUP + h))
    prev = lambda off, g: pl.BlockSpec(
        (spans[g], LANES),
        lambda h, i: (jnp.maximum(i * (rb // spans[g]) - 1, 0), off * QKV_BLOCKS + g * DIL_HEADS_PER_GROUP + h))
    row = pl.BlockSpec((rb, LANES), lambda h, i: (i, h))
    groups = range(DIL_GROUPS)
    in_specs = [pl.BlockSpec((1, 8, LANES), lambda h, i: (h, 0, 0))]
    in_specs += [cur(0, g) for g in groups] + [cur(1, g) for g in groups] + [prev(1, g) for g in groups]
    in_specs += [cur(2, g) for g in groups] + [prev(2, g) for g in groups] + [row]
    f32 = jax.ShapeDtypeStruct((S, D_MODEL), F32)
    return pl.pallas_call(
        body, name="dil_attn_fwd", grid=(DIL_HEADS_PER_GROUP, nb), in_specs=in_specs,
        out_specs=[row, row, row], out_shape=[f32, jax.ShapeDtypeStruct((S, D_MODEL), BF16), f32],
        scratch_shapes=[pltpu.VMEM((rb, LANES), F32)] * 6,
        compiler_params=_cparams("parallel", "parallel"),
    )(_slope_table(), *([qkv] * 15), z)


def _dil_bwd_q(qkv, do, ld, buf, g):
    S = qkv.shape[0]
    d = DIL_PATTERN[g][1]
    span = W * d
    rb = min(max(DIL_ROWS, min(8 * span, DIL_BWD_ROWS)), S)
    nb = S // rb

    def body(slope_ref, q_ref, kc, kp, vc, vp, do_ref, ld_ref, buf_ref, dq_ref, dq_f32):
        i = pl.program_id(1)
        _, j_, dist, inwin = _window_geometry((W, 2 * W), W)
        bias = -(slope_ref[0, g:g + 1, 0:1] * float(d)) * dist.astype(F32)
        def phase(r):
            kprev, vprev = kp[_phase_rows(r, d), :].astype(BF16), vp[_phase_rows(r, d), :].astype(BF16)
            for b in range(rb // span):
                rows = _phase_rows(b * span + r, d)
                kcur, vcur = kc[rows, :].astype(BF16), vc[rows, :].astype(BF16)
                valid = inwin if b > 0 else inwin & ((j_ >= W) | (i > 0))
                k2 = jnp.concatenate([kprev, kcur], axis=0)
                v2 = jnp.concatenate([vprev, vcur], axis=0)
                ld = ld_ref[rows, :]
                s = lax.dot_general(q_ref[rows, :].astype(BF16), k2, NT, preferred_element_type=F32)
                s = jnp.where(valid, s * DIL_SCALE + bias - ld[:, 0:1], NEG)
                pr = jnp.exp(s)
                dp = lax.dot_general(do_ref[rows, :].astype(BF16), v2, NT, preferred_element_type=F32)
                ds = pr * (dp - ld[:, LANES // 2:LANES // 2 + 1])
                dq_f32[rows, :] = jnp.dot(ds.astype(BF16), k2, preferred_element_type=F32) * DIL_SCALE
                kprev, vprev = kcur, vcur

        _for_phases(d, phase)
        dq_ref[...] = dq_f32[...].astype(BF16)

    col = lambda off: off * QKV_BLOCKS + g * DIL_HEADS_PER_GROUP
    cur = lambda off: pl.BlockSpec((rb, LANES), lambda h, i: (i, col(off) + h))
    prev = lambda off: pl.BlockSpec((span, LANES), lambda h, i: (jnp.maximum(i * (rb // span) - 1, 0), col(off) + h))
    row = pl.BlockSpec((rb, LANES), lambda h, i: (i, h))
    return pl.pallas_call(
        body, name=f"dil_attn_bwd_q_g{g}", grid=(DIL_HEADS_PER_GROUP, nb),
        in_specs=[pl.BlockSpec((1, 8, LANES), lambda h, i: (h, 0, 0)), cur(0), cur(1), prev(1), cur(2), prev(2),
                  row, row, pl.BlockSpec(memory_space=pl.ANY)],
        out_specs=pl.BlockSpec((rb, LANES), lambda h, i: (i, g * DIL_HEADS_PER_GROUP + h)),
        out_shape=jax.ShapeDtypeStruct(buf.shape, buf.dtype), input_output_aliases={8: 0},
        scratch_shapes=[pltpu.VMEM((rb, LANES), F32)],
        compiler_params=_cparams("parallel", "parallel"),
    )(_slope_table(), qkv, qkv, qkv, qkv, qkv, do, ld, buf)


def _dil_bwd_kv(qkv, do, ld, bufk, bufv, g):
    S = qkv.shape[0]
    d = DIL_PATTERN[g][1]
    span = W * d
    rb = min(max(DIL_ROWS, min(8 * span, DIL_BWD_ROWS)), S)
    nb = S // rb
    nub = rb // span

    def body(slope_ref, k_ref, v_ref, qc, qn, doc, don, ldc, ldn, bufk_ref, bufv_ref, dk_ref, dv_ref, dk_f32, dv_f32):
        i = pl.program_id(1)
        i_, _, dist, inwin = _window_geometry((2 * W, W), 0)
        bias = -(slope_ref[0, g:g + 1, 0:1] * float(d)) * dist.astype(F32)
        def phase(r):
            first = _phase_rows(r, d)
            qcur, docur, ldcur = qc[first, :].astype(BF16), doc[first, :].astype(BF16), ldc[first, :]
            for b in range(nub):
                rows = _phase_rows(b * span + r, d)
                if b < nub - 1:
                    nxt = _phase_rows((b + 1) * span + r, d)
                    qnext, donext, ldnext, valid = qc[nxt, :].astype(BF16), doc[nxt, :].astype(BF16), ldc[nxt, :], inwin
                else:
                    qnext, donext, ldnext = qn[first, :].astype(BF16), don[first, :].astype(BF16), ldn[first, :]
                    valid = inwin & ((i_ < W) | (i < nb - 1))
                q2 = jnp.concatenate([qcur, qnext], axis=0)
                do2 = jnp.concatenate([docur, donext], axis=0)
                ld2 = jnp.concatenate([ldcur, ldnext], axis=0)
                s = lax.dot_general(q2, k_ref[rows, :].astype(BF16), NT, preferred_element_type=F32)
                s = jnp.where(valid, s * DIL_SCALE + bias - ld2[:, 0:1], NEG)
                pr = jnp.exp(s)
                dp = lax.dot_general(do2, v_ref[rows, :].astype(BF16), NT, preferred_element_type=F32)
                ds = pr * (dp - ld2[:, LANES // 2:LANES // 2 + 1])
                dv_f32[rows, :] = lax.dot_general(pr.astype(BF16), do2, TN, preferred_element_type=F32)
                dk_f32[rows, :] = lax.dot_general(ds.astype(BF16), q2, TN, preferred_element_type=F32) * DIL_SCALE
                qcur, docur, ldcur = qnext, donext, ldnext

        _for_phases(d, phase)
        dk_ref[...] = dk_f32[...].astype(BF16)
        dv_ref[...] = dv_f32[...].astype(BF16)

    col = lambda off: off * QKV_BLOCKS + g * DIL_HEADS_PER_GROUP
    cur = lambda off: pl.BlockSpec((rb, LANES), lambda h, i: (i, col(off) + h))
    nxt_blk = lambda i: jnp.minimum((i + 1) * nub, S // span - 1)
    nxt = lambda off: pl.BlockSpec((span, LANES), lambda h, i: (nxt_blk(i), col(off) + h))
    row = pl.BlockSpec((rb, LANES), lambda h, i: (i, h))
    row_nxt = pl.BlockSpec((span, LANES), lambda h, i: (nxt_blk(i), h))
    any_ = pl.BlockSpec(memory_space=pl.ANY)
    out = pl.BlockSpec((rb, LANES), lambda h, i: (i, g * DIL_HEADS_PER_GROUP + h))
    return pl.pallas_call(
        body, name=f"dil_attn_bwd_kv_g{g}", grid=(DIL_HEADS_PER_GROUP, nb),
        in_specs=[pl.BlockSpec((1, 8, LANES), lambda h, i: (h, 0, 0)), cur(1), cur(2), cur(0), nxt(0),
                  row, row_nxt, row, row_nxt, any_, any_],
        out_specs=[out, out],
        out_shape=[jax.ShapeDtypeStruct(bufk.shape, bufk.dtype), jax.ShapeDtypeStruct(bufv.shape, bufv.dtype)],
        input_output_aliases={9: 0, 10: 1},
        scratch_shapes=[pltpu.VMEM((rb, LANES), F32)] * 2,
        compiler_params=_cparams("parallel", "parallel"),
    )(_slope_table(), qkv, qkv, qkv, qkv, do, do, ld, ld, bufk, bufv)


def _place():
    x, y, c = lax.axis_index("x"), lax.axis_index("y"), lax.axis_index("c")
    chips = [(1 - x, y), (x, 1 - y), (1 - x, 1 - y)]
    return x, y, c, chips


def _gather_weights(wb):
    R = wb.shape[0]
    H = R // 2

    def body(w_ref, out_ref, send1, recv1, send2, recv2):
        x, y, c, chips = _place()
        me = 2 * x + y
        sib = (x, y, 1 - c)
        half = pl.ds(c * H, H)
        first = [pltpu.make_async_remote_copy(
            src_ref=w_ref.at[half], dst_ref=out_ref.at[me, half], send_sem=send1.at[k], recv_sem=recv1.at[k],
            device_id=(*chip, c), device_id_type=MESH) for k, chip in enumerate(chips)]
        for cp in first:
            cp.start()
        passed = []
        for k, (cx, cy) in enumerate(chips):
            slot = out_ref.at[2 * cx + cy, half]
            pltpu.make_async_remote_copy(src_ref=slot, dst_ref=slot, send_sem=send1.at[k], recv_sem=recv1.at[k],
                                         device_id=(cx, cy, c), device_id_type=MESH).wait_recv()
            cp = pltpu.make_async_remote_copy(src_ref=slot, dst_ref=slot, send_sem=send2.at[k], recv_sem=recv2.at[k],
                                              device_id=sib, device_id_type=MESH)
            cp.start()
            passed.append(cp)
        for k, (cx, cy) in enumerate(chips):
            slot = out_ref.at[2 * cx + cy, pl.ds((1 - c) * H, H)]
            pltpu.make_async_remote_copy(src_ref=slot, dst_ref=slot, send_sem=send2.at[k], recv_sem=recv2.at[k],
                                         device_id=sib, device_id_type=MESH).wait_recv()
        for cp in first + passed:
            cp.wait_send()

    return pl.pallas_call(
        body, name="gather_weights", in_specs=[ANY], out_specs=ANY,
        out_shape=jax.ShapeDtypeStruct((4, R, D_MODEL), wb.dtype),
        scratch_shapes=[pltpu.SemaphoreType.DMA((3,)), pltpu.SemaphoreType.DMA((3,)),
                        pltpu.SemaphoreType.DMA((3,)), pltpu.SemaphoreType.DMA((3,))],
    )(wb)


def _swap_halves(g):
    H = g.shape[1] // 2

    def body(g_ref, out_ref, send, recv):
        x, y, c, _ = _place()
        sib = (x, y, 1 - c)
        cps = [pltpu.make_async_remote_copy(
            src_ref=g_ref.at[s, pl.ds((1 - c) * H, H)], dst_ref=out_ref.at[s], send_sem=send.at[s],
            recv_sem=recv.at[s], device_id=sib, device_id_type=MESH) for s in range(4)]
        for cp in cps:
            cp.start()
        for cp in cps:
            cp.wait()

    return pl.pallas_call(
        body, name="swap_halves", in_specs=[ANY], out_specs=ANY,
        out_shape=jax.ShapeDtypeStruct((4, H, D_MODEL), g.dtype),
        scratch_shapes=[pltpu.SemaphoreType.DMA((4,)), pltpu.SemaphoreType.DMA((4,))],
    )(g)


def _sibling_half(f, name):
    def body(f_ref, out_ref, send, recv):
        x, y, c, _ = _place()
        cp = pltpu.make_async_remote_copy(src_ref=f_ref, dst_ref=out_ref, send_sem=send, recv_sem=recv,
                                          device_id=(x, y, 1 - c), device_id_type=MESH)
        cp.start()
        cp.wait()

    return pl.pallas_call(
        body, name=name, in_specs=[ANY], out_specs=ANY,
        out_shape=jax.ShapeDtypeStruct(f.shape, f.dtype),
        scratch_shapes=[pltpu.SemaphoreType.DMA, pltpu.SemaphoreType.DMA],
    )(f)


def _gather_tiles(tile, name):
    m_per = tile.shape[0]

    def body(x_ref, out_ref, send_sems, recv_sems, local_sem):
        x, y, c, chips = _place()
        me, sibling = (x, y, c), (x, y, 1 - c)

        def rows(px, py, pc):
            return out_ref.at[pl.ds((4 * px + 2 * py + pc) * m_per, m_per), :]

        def copy(k, block, to, src=None):
            return pltpu.make_async_remote_copy(
                src_ref=rows(*block) if src is None else src, dst_ref=rows(*block),
                send_sem=send_sems.at[k], recv_sem=recv_sems.at[k], device_id=to, device_id_type=MESH)

        mine = pltpu.make_async_copy(x_ref, rows(*me), local_sem)
        mine.start()
        first = [copy(0, me, sibling, src=x_ref)]
        first += [copy(1 + j, me, (*chip, c), src=x_ref) for j, chip in enumerate(chips)]
        for cp in first:
            cp.start()
        passed = [copy(4 + j, (*chip, c), sibling) for j, chip in enumerate(chips)]
        for j, chip in enumerate(chips):
            copy(1 + j, (*chip, c), me).wait_recv()
            passed[j].start()
        copy(0, sibling, me).wait_recv()
        for j, chip in enumerate(chips):
            copy(4 + j, (*chip, 1 - c), me).wait_recv()
        for cp in first + passed:
            cp.wait_send()
        mine.wait()

    return pl.pallas_call(
        body, name=name, out_shape=jax.ShapeDtypeStruct((8 * m_per, D_MODEL), tile.dtype),
        in_specs=[pl.BlockSpec(memory_space=pltpu.VMEM)], out_specs=pl.BlockSpec(memory_space=pltpu.VMEM),
        scratch_shapes=[pltpu.SemaphoreType.DMA((7,)), pltpu.SemaphoreType.DMA((7,)), pltpu.SemaphoreType.DMA],
    )(tile)


def _pad_rows(a, rows):
    return jnp.pad(a, ((0, rows - a.shape[0]), (0, 0)))


def _pad_row(v):
    v = v.reshape(1, -1)
    return jnp.pad(v, ((0, 0), (0, D_MODEL - v.shape[1])))


def _pack_shard(fwi, fwo, dwi, dwo, pwu, pwg):
    parts = [_pad_rows(fwi.reshape(1028, D_MODEL), FLAT_A), fwo.reshape(256, D_MODEL), dwi.reshape(2560, D_MODEL),
             dwo.reshape(256, D_MODEL), pwu.reshape(128, D_MODEL), pwg.reshape(512, D_MODEL),
             jnp.zeros((FLAT_TOTAL - FLAT_USED, D_MODEL), fwi.dtype)]
    return jnp.concatenate(parts, axis=0)


def _unpack_shard(flat):
    out, r0 = {}, 0
    for name, rows, slot in FLAT_ROWS:
        out[name] = flat[r0:r0 + rows]
        r0 += slot
    return (out["fox_w_in"].reshape(1, D_MODEL, 1028), out["fox_w_out"].reshape(1, 256, D_MODEL),
            out["dil_w_in"].reshape(1, D_MODEL, 2560), out["dil_w_out"].reshape(1, 256, D_MODEL),
            out["ple_w_up"].reshape(2, PLE_DIM, 256), out["ple_w_gate"].reshape(2, 256, D_MODEL))


def _from_shard_columns(a, n):
    return a.reshape(4, D_MODEL, n).transpose(1, 0, 2).reshape(D_MODEL, 4 * n)


def _to_shard_columns(a, n):
    return a.reshape(D_MODEL, 4, n).transpose(1, 0, 2).reshape(4, n, D_MODEL)


def _unpack_a(wall_a):
    return _from_shard_columns(wall_a[:, :1028], 1028)


def _unpack_b(wall_b):
    out, r0 = {}, 0
    for name, rows, slot in FLAT_ROWS[1:]:
        out[name] = wall_b[:, r0:r0 + rows]
        r0 += slot
    pwu4 = out["ple_w_up"].reshape(4, 2, PLE_DIM, 256)
    pwg4 = out["ple_w_gate"].reshape(4, 2, 256, D_MODEL)
    pwu = [pwu4[:, i].transpose(1, 0, 2).reshape(PLE_DIM, D_MODEL) for i in range(2)]
    pwg = [pwg4[:, i].reshape(D_MODEL, D_MODEL) for i in range(2)]
    return (out["fox_w_out"].reshape(D_MODEL, D_MODEL), _from_shard_columns(out["dil_w_in"], 2560),
            out["dil_w_out"].reshape(D_MODEL, D_MODEL), pwu, pwg)


def _gathered_b_weights(wall_b, w_b, chip):
    return _unpack_b(lax.dynamic_update_slice(wall_b, w_b[None], (chip, 0, 0)))


def _pack_grads_a(gfwi):
    return jnp.pad(_to_shard_columns(gfwi, 1028), ((0, 0), (0, FLAT_A - 1028), (0, 0)))


def _pack_grads_b(gfwo, gdwi, gdwo, gpwu, gpwg):
    up = jnp.stack([a.reshape(PLE_DIM, 4, 256).transpose(1, 0, 2) for a in gpwu], axis=1)
    gate = jnp.stack([a.reshape(4, 256, D_MODEL) for a in gpwg], axis=1)
    parts = [gfwo.reshape(4, 256, D_MODEL), _to_shard_columns(gdwi, 2560), gdwo.reshape(4, 256, D_MODEL),
             up.reshape(4, 128, D_MODEL), gate.reshape(4, 512, D_MODEL)]
    return jnp.concatenate(parts, axis=1).astype(BF16)


def _local_step(x, p0, p1, tgt, fox_g, dil_g, fin_g, b_f, fwi, w_b, chip, core):
    S = x.shape[0]
    tm = min(512, S)
    w_qkv0 = jnp.concatenate([fwi[:, :D_MODEL] * 0.125, fwi[:, D_MODEL:3 * D_MODEL]], axis=1)
    w_z0 = fwi[:, 3 * D_MODEL:4 * D_MODEL]
    w_f0 = jnp.pad(fwi[:, 4 * D_MODEL:], ((0, 0), (0, LANES - FOX_HEADS)))
    b_full = jnp.pad(b_f.reshape(1, FOX_HEADS), ((0, 0), (0, LANES - FOX_HEADS)))

    n0, = _rows(_rms_fwd, [x], [(D_MODEL, BF16)], name="norm0", tm=tm, bcast=[fox_g])
    qkv0 = _mm(n0, w_qkv0, out_dtype=BF16, name="proj_qkv0")
    z0 = _mm(n0, w_z0, name="proj_z0")
    fl0 = _mm(n0, w_f0, name="proj_f0")
    qx0, kx0 = _cumsum_rows([fl0], name="forget_cumsum", width=LANES, pre=_log_forget, bcast=[b_full],
                            post=_fox_extras, outs=[(D_MODEL, BF16), (D_MODEL, BF16)])
    o0, a0, g0, wall_b = _fox_fwd(qkv0, qx0, kx0, z0, w_b)
    fwo, dwi, dwo, pwu, pwg = _gathered_b_weights(wall_b, w_b, chip)
    w_qkv1 = dwi[:, :3 * DIL_QKV]
    w_z1 = dwi[:, 3 * DIL_QKV:]
    h1 = _mm(g0, fwo, add=x, name="out_proj0")
    u0 = _mm(p0, pwu[0], name="ple_up0")
    t0 = _mm(h1, pwg[0], name="ple_gate0")
    h2, n1 = _rows(_ple_norm, [h1, u0, t0], [(D_MODEL, F32), (D_MODEL, BF16)], name="ple_mix0_norm1", tm=tm,
                   bcast=[dil_g])

    qkv1 = _mm(n1, w_qkv1, name="proj_qkv1")
    z1 = _mm(n1, w_z1, name="proj_z1")
    o1, g1, lse1 = _dil_fwd(qkv1, z1)
    h3 = _mm(g1, dwo, add=h2, name="out_proj1")
    u1 = _mm(p1, pwu[1], name="ple_up1")
    t1 = _mm(h3, pwg[1], name="ple_gate1")

    du1, dt1, dh4, g_fin, loss = _rows(
        _ple_loss, [h3, u1, t1, tgt], [(D_MODEL, BF16), (D_MODEL, BF16), (D_MODEL, F32)], name="ple_mix1_loss_head",
        tm=tm, bcast=[fin_g], acc=[((1, D_MODEL), F32), ((1, LANES), F32)])
    g_up1 = _mm(p1, du1, ta=True, out_dtype=BF16, name="grad_ple_up1")
    g_gate1 = _mm(h3, dt1, ta=True, out_dtype=BF16, name="grad_ple_gate1")
    dh3 = _mm(dt1, pwg[1], tb=True, add=dh4, name="ple_back1")
    dg1 = _mm(dh3, dwo, tb=True, name="out_back1")
    g_dwo = _mm(g1, dh3, ta=True, out_dtype=BF16, name="grad_out1")
    do1, dz1, ld1 = _rows(_gate_bwd, [dg1, o1, z1, lse1], [(D_MODEL, F32), (D_MODEL, BF16), (D_MODEL, F32)],
                          name="gate_bwd1", tm=tm)
    dq1, dk1, dv1 = (lax.empty((S, DIL_QKV), BF16) for _ in range(3))
    for g in range(DIL_GROUPS):
        dq1 = _dil_bwd_q(qkv1, do1, ld1, dq1, g)
        dk1, dv1 = _dil_bwd_kv(qkv1, do1, ld1, dk1, dv1, g)
    g_dwi = _mm_tn_parts(n1, [dq1, dk1, dv1, dz1], name="grad_in1", out_dtype=BF16)
    dn1 = _mm_nt_parts([dq1, dk1, dv1, dz1], dwi, name="in_back1")
    dh2, du0, dt0, g_dil = _rows(_norm_ple_bwd, [dn1, h2, dh3, u0, t0], [(D_MODEL, F32), (D_MODEL, BF16), (D_MODEL, BF16)],
                                 name="norm_bwd1_ple_bwd0", tm=tm, bcast=[dil_g], acc=[((1, D_MODEL), F32)])
    g_up0 = _mm(p0, du0, ta=True, out_dtype=BF16, name="grad_ple_up0")
    g_gate0 = _mm(h1, dt0, ta=True, out_dtype=BF16, name="grad_ple_gate0")
    dh1 = _mm(dt0, pwg[0], tb=True, add=dh2, name="ple_back0")
    dg0 = _mm(dh1, fwo, tb=True, name="out_back0")
    g_fwo = _mm(g0, dh1, ta=True, out_dtype=BF16, name="grad_out0")
    do0, dz0, qxa0, dx0 = _rows(_gate_bwd_fox, [dg0, o0, z0, a0], [(D_MODEL, BF16)] * 4, name="gate_bwd0", tm=tm)
    g_b = _pack_grads_b(g_fwo, g_dwi, g_dwo, [g_up0, g_up1], [g_gate0, g_gate1])
    dq0, drx, dk0, dv0, dkx, got_b = _fox_bwd(qkv0, do0, qxa0, dx0, kx0, g_b)
    dlogf, = _cumsum_rows([drx, dkx], name="forget_cumsum_bwd", width=LANES, reverse=True, pre=_forget_dc)
    df0, g_bf = _rows(_forget_bwd, [dlogf, fl0], [(LANES, BF16)], name="forget_bwd", tm=tm, bcast=[b_full],
                      acc=[((1, LANES), F32)])
    g_fwi_main = _mm_tn_parts(n0, [dq0, dk0, dv0, dz0], name="grad_in0")
    g_fwi_f = _mm(n0, df0, ta=True, name="grad_in0_forget")
    g_fwi = jnp.concatenate([g_fwi_main, g_fwi_f[:, :FOX_HEADS]], axis=1)
    half = FLAT_A // 2
    g_a = _pack_grads_a(g_fwi)
    theirs = _swap_halves(g_a)
    mine = lax.dynamic_slice_in_dim(g_a, core * half, half, axis=1)
    part_a, = _rows(lambda a, b: a + b, [mine.reshape(4 * half, D_MODEL), theirs.reshape(4 * half, D_MODEL)],
                    [(D_MODEL, BF16)], name="pair_sum", tm=half)
    part_a = part_a.reshape(4, half, D_MODEL)
    dn0, got_a = _mm_nt_parts([dq0, dk0, dv0, dz0], fwi[:, :4 * D_MODEL], name="in_back0", scatter=part_a)
    dn0 = _mm(df0, w_f0, tb=True, add=dn0, name="in_back0_forget")
    dx, g_fox = _rows(_rms_bwd, [dn0, x, dh1], [(D_MODEL, F32)], name="norm_bwd0", tm=tm, bcast=[fox_g],
                      acc=[((1, D_MODEL), F32)])
    return loss, dx, part_a, got_a, g_b, got_b, (g_fox, _pad_row(g_bf[:, :FOX_HEADS]), g_fin, g_dil)


def kernel(x, p, fox_norm, fox_w_in, fox_b_f, fox_w_out, dil_norm, dil_w_in, dil_w_out, ple_w_up, ple_w_gate, final_norm, loss_target, m_fox_norm, m_fox_w_in, m_fox_b_f, m_fox_w_out, m_dil_norm, m_dil_w_in, m_dil_w_out, m_ple_w_up, m_ple_w_gate, m_final_norm, v_fox_norm, v_fox_w_in, v_fox_b_f, v_fox_w_out, v_dil_norm, v_dil_w_in, v_dil_w_out, v_ple_w_up, v_ple_w_gate, v_final_norm):
    xi, yi, ci = lax.axis_index("x"), lax.axis_index("y"), lax.axis_index("c")
    chip = 2 * xi + yi

    w_flat = _pack_shard(*(w.astype(BF16) for w in (fox_w_in, fox_w_out, dil_w_in, dil_w_out, ple_w_up, ple_w_gate)))
    w_a, w_b = w_flat[:FLAT_A], w_flat[FLAT_A:FLAT_USED]
    fwi = _unpack_a(lax.dynamic_update_slice(_gather_weights(w_a), w_a[None], (chip, 0, 0)))
    gains = _gather_tiles(_pad_rows(_pad_row(dil_norm), 8), "gather_gain").reshape(4, 2, 8, D_MODEL)
    dil_g = gains[:, 0, 0, :256].reshape(1, D_MODEL)

    loss_part, dx, part_a, got_a, g_b, got_b, small = _local_step(
        x[0], p[0, 0], p[1, 0], loss_target[0], fox_norm.reshape(1, D_MODEL), dil_g,
        final_norm.reshape(1, D_MODEL), fox_b_f, fwi, w_b, chip, ci)
    g_fox, g_bf, g_fin, g_dil = small
    sum4 = lambda a, b, c, d: ((a.astype(F32) + b.astype(F32)) + c.astype(F32)) + d.astype(F32)

    got_b = lax.dynamic_update_slice(got_b, lax.dynamic_slice_in_dim(g_b, chip, 1, axis=0), (chip, 0, 0))
    sum_b, = _rows(sum4, [got_b[s] for s in range(4)], [(D_MODEL, F32)], name="chip_sum_b", tm=FLAT_B // 8)
    g_flat_b, = _rows(lambda a, b: a + b, [sum_b, _sibling_half(sum_b, "sibling_sum_b")], [(D_MODEL, F32)],
                      name="pair_sum_b", tm=FLAT_B // 8)

    half = FLAT_A // 2
    by_chip = lax.dynamic_update_slice(got_a, lax.dynamic_slice_in_dim(part_a, chip, 1, axis=0), (chip, 0, 0))
    half_sum, = _rows(sum4, [by_chip[s] for s in range(4)], [(D_MODEL, F32)], name="chip_sum", tm=half)
    other_half = _sibling_half(half_sum, "sibling_half")
    g_flat_a = jnp.where(ci == 0, jnp.concatenate([half_sum, other_half], axis=0),
                         jnp.concatenate([other_half, half_sum], axis=0))
    g_flat = jnp.concatenate([g_flat_a, g_flat_b, jnp.zeros((FLAT_TOTAL - FLAT_USED, D_MODEL), F32)], axis=0)

    tile = jnp.concatenate([g_fox, g_bf, g_fin, g_dil, jnp.pad(loss_part, ((0, 0), (0, D_MODEL - LANES))),
                            jnp.zeros((3, D_MODEL), F32)], axis=0)
    tiles = _gather_tiles(tile, "gather_small")
    tot, = _rows(lambda *t: functools.reduce(lambda a, b: a + b, t), [tiles[8 * s:8 * s + 8] for s in range(8)],
                 [(D_MODEL, F32)], name="small_sum", tm=8)
    loss = tot[4, 0]
    g_small = jnp.concatenate([
        tot[0:3],
        _pad_row(lax.dynamic_slice_in_dim(tot[3], chip * 256, 256)), jnp.zeros((4, D_MODEL), F32)], axis=0)
    small_tile = lambda a, b, c, d: jnp.concatenate(
        [_pad_row(a), _pad_row(b), _pad_row(c), _pad_row(d), jnp.zeros((4, D_MODEL), F32)], axis=0)
    w_small = small_tile(fox_norm, fox_b_f, final_norm, dil_norm)
    m_small = small_tile(m_fox_norm, m_fox_b_f, m_final_norm, m_dil_norm)
    v_small = small_tile(v_fox_norm, v_fox_b_f, v_final_norm, v_dil_norm)

    d_small, nm_small, nv_small = _rows(_adamw, [w_small, g_small, m_small, v_small], [(D_MODEL, F32)] * 3,
                                        name="adamw_small", tm=8)
    grads = _unpack_shard(g_flat)
    weights = (fox_w_in, fox_w_out, dil_w_in, dil_w_out, ple_w_up, ple_w_gate)
    firsts = (m_fox_w_in, m_fox_w_out, m_dil_w_in, m_dil_w_out, m_ple_w_up, m_ple_w_gate)
    seconds = (v_fox_w_in, v_fox_w_out, v_dil_w_in, v_dil_w_out, v_ple_w_up, v_ple_w_gate)
    big = []
    for (name, _, _), w, g, m, v in zip(FLAT_ROWS, weights, grads, firsts, seconds):
        width = w.shape[-1]
        flat2 = lambda a: a.reshape(-1, width)
        res = _rows(_adamw, [flat2(w), flat2(g), flat2(m), flat2(v)], [(width, F32)] * 3, name=f"adamw_{name}",
                    tm=128 if width > D_MODEL + LANES else 256)
        big.append([r.reshape(w.shape) for r in res])

    def leaves(shards, small_rows):
        fwi_, fwo_, dwi_, dwo_, pwu_, pwg_ = shards
        return (small_rows[0:1], fwi_, small_rows[1:2, :FOX_HEADS], fwo_, small_rows[3:4, :256], dwi_, dwo_,
                pwu_, pwg_, small_rows[2])

    per_kind = lambda k: [b[k] for b in big]
    return (loss, dx[None], *leaves(grads, g_small), *leaves(per_kind(0), d_small), *leaves(per_kind(1), nm_small),
            *leaves(per_kind(2), nv_small))
```

```python
import functools

import numpy as np
import jax
import jax.numpy as jnp
from jax import lax
from jax.experimental import pallas as pl
from jax.experimental.pallas import tpu as pltpu

F32 = jnp.float32
BF16 = jnp.bfloat16

D_MODEL = 1024
PLE_DIM = 256
FOX_HEADS = 16
FOX_HEAD_DIM = 64
DIL_PATTERN = ((128, 1), (512, 4), (2048, 16))
DIL_GROUPS = 3
DIL_HEADS_PER_GROUP = 8
DIL_HEAD_DIM = 128
DIL_WINDOW_STEPS = 128
DIL_QKV = 3072
ALIBI_MAX_EXP = 8.0
RMS_EPS = 1e-6
ADAM_LR, ADAM_B1, ADAM_B2, ADAM_EPS, ADAM_WD, ADAM_STEP = 0.001, 0.9, 0.999, 1e-08, 0.01, 10

LANES = 128
VMEM_LIMIT = 56 * 1024 * 1024
MESH = pl.DeviceIdType.MESH
ANY = pl.BlockSpec(memory_space=pl.ANY)
NEG = -1e30

FLAT_ROWS = (("fox_w_in", 1028, 1056), ("fox_w_out", 256, 256), ("dil_w_in", 2560, 2560), ("dil_w_out", 256, 256),
             ("ple_w_up", 128, 128), ("ple_w_gate", 512, 512))
FLAT_USED = sum(r for _, _, r in FLAT_ROWS)
FLAT_TOTAL = 4864
FLAT_A = FLAT_ROWS[0][2]
FLAT_B = FLAT_USED - FLAT_A


def _cparams(*sem):
    return pltpu.CompilerParams(dimension_semantics=sem, vmem_limit_bytes=VMEM_LIMIT)


def _sigmoid(x):
    return 1.0 / (1.0 + jnp.exp(-x))


def _mm(a, b, *, name, ta=False, tb=False, out_dtype=F32, add=None, n_cols=None):
    if ta:
        K, M = a.shape
    else:
        M, K = a.shape
    if tb:
        N, Kb = b.shape
    else:
        Kb, N = b.shape
        N = n_cols or N
    assert K == Kb, (a.shape, b.shape)
    tm, tn, tk = min(M, 1024), min(N, 1024), min(K, 1024)
    assert M % tm == 0 and N % tn == 0 and K % tk == 0, (M, N, K)
    nk = K // tk
    dn = (((0 if ta else 1,), (1 if tb else 0,)), ((), ()))

    def body(*refs):
        if add is None:
            a_ref, b_ref, o_ref, acc = refs
        else:
            a_ref, b_ref, add_ref, o_ref, acc = refs
        k = pl.program_id(2)

        @pl.when(k == 0)
        def _():
            acc[...] = jnp.zeros_like(acc)

        acc[...] += lax.dot_general(a_ref[...].astype(BF16), b_ref[...].astype(BF16), dn,
                                    preferred_element_type=F32)

        @pl.when(k == nk - 1)
        def _():
            r = acc[...]
            if add is not None:
                r = r + add_ref[...]
            o_ref[...] = r.astype(out_dtype)

    a_spec = (pl.BlockSpec((tk, tm), lambda i, j, k: (k, i)) if ta
              else pl.BlockSpec((tm, tk), lambda i, j, k: (i, k)))
    b_spec = (pl.BlockSpec((tn, tk), lambda i, j, k: (j, k)) if tb
              else pl.BlockSpec((tk, tn), lambda i, j, k: (k, j)))
    in_specs = [a_spec, b_spec]
    args = [a, b]
    if add is not None:
        in_specs.append(pl.BlockSpec((tm, tn), lambda i, j, k: (i, j)))
        args.append(add)
    return pl.pallas_call(
        body, name=name, grid=(M // tm, N // tn, nk),
        in_specs=in_specs, out_specs=pl.BlockSpec((tm, tn), lambda i, j, k: (i, j)),
        out_shape=jax.ShapeDtypeStruct((M, N), out_dtype),
        scratch_shapes=[pltpu.VMEM((tm, tn), F32)],
        compiler_params=_cparams("parallel", "parallel", "arbitrary"),
    )(*args)


def _mm_nt_parts(a_parts, b, *, name, scatter=None):
    M = a_parts[0].shape[0]
    N, K = b.shape
    tm, tn, tk = min(M, 1024), min(N, 1024), 1024
    nks = [a.shape[1] // tk for a in a_parts]
    offs = [sum(nks[:p]) for p in range(len(nks))]
    nk = sum(nks)
    assert nk * tk <= K and M % tm == 0 and N % tn == 0, (M, N, K)
    n_parts = len(a_parts)

    grid = (M // tm, N // tn, nk)

    def body(*refs):
        a_refs, b_ref = refs[:n_parts], refs[n_parts]
        if scatter is None:
            o_ref, acc = refs[n_parts + 1:]
        else:
            part_ref, o_ref, got_ref, acc, send, recv = refs[n_parts + 1:]
            start, wait = _chip_exchange(lambda cx, cy: part_ref.at[2 * cx + cy], got_ref, send, recv)
            at = lambda step: functools.reduce(
                lambda x, y: x & y, [pl.program_id(ax) == (0 if step == 0 else n - 1) for ax, n in enumerate(grid)])
            pl.when(at(0))(start)
        k = pl.program_id(2)

        @pl.when(k == 0)
        def _():
            acc[...] = jnp.zeros_like(acc)

        for a_ref, off, n in zip(a_refs, offs, nks):
            @pl.when((k >= off) & (k < off + n))
            def _(a_ref=a_ref):
                acc[...] += lax.dot_general(a_ref[...].astype(BF16), b_ref[...].astype(BF16),
                                            (((1,), (1,)), ((), ())), preferred_element_type=F32)

        @pl.when(k == nk - 1)
        def _():
            o_ref[...] = acc[...]

        if scatter is not None:
            pl.when(at(-1))(wait)

    a_specs = [pl.BlockSpec((tm, tk), lambda i, j, k, off=off, n=n: (i, jnp.clip(k - off, 0, n - 1)))
               for off, n in zip(offs, nks)]
    in_specs = a_specs + [pl.BlockSpec((tn, tk), lambda i, j, k: (j, k))]
    out_specs = [pl.BlockSpec((tm, tn), lambda i, j, k: (i, j))]
    out_shape = [jax.ShapeDtypeStruct((M, N), F32)]
    scratch = [pltpu.VMEM((tm, tn), F32)]
    args = [*a_parts, b]
    if scatter is not None:
        in_specs, out_specs = in_specs + [ANY], out_specs + [ANY]
        out_shape.append(jax.ShapeDtypeStruct(scatter.shape, scatter.dtype))
        scratch += [pltpu.SemaphoreType.DMA((3,)), pltpu.SemaphoreType.DMA((3,))]
        args.append(scatter)
    res = pl.pallas_call(
        body, name=name, grid=grid, in_specs=in_specs, out_specs=out_specs, out_shape=out_shape,
        scratch_shapes=scratch,
        compiler_params=_cparams(*(("arbitrary",) * 3 if scatter is not None else ("parallel", "parallel", "arbitrary"))),
    )(*args)
    return res[0] if scatter is None else res


def _mm_tn_parts(a, b_parts, *, name, out_dtype=F32):
    K, M = a.shape
    tm, tn, tk = min(M, 1024), 1024, min(K, 1024)
    njs = [b.shape[1] // tn for b in b_parts]
    offs = [sum(njs[:p]) for p in range(len(njs))]
    nj, nk = sum(njs), K // tk
    assert M % tm == 0 and K % tk == 0 and all(b.shape[1] % tn == 0 for b in b_parts)
    n_parts = len(b_parts)

    def body(*refs):
        a_ref, b_refs, o_ref, acc = refs[0], refs[1:1 + n_parts], refs[1 + n_parts], refs[2 + n_parts]
        j, k = pl.program_id(1), pl.program_id(2)

        @pl.when(k == 0)
        def _():
            acc[...] = jnp.zeros_like(acc)

        for b_ref, off, n in zip(b_refs, offs, njs):
            @pl.when((j >= off) & (j < off + n))
            def _(b_ref=b_ref):
                acc[...] += lax.dot_general(a_ref[...].astype(BF16), b_ref[...].astype(BF16),
                                            (((0,), (0,)), ((), ())), preferred_element_type=F32)

        @pl.when(k == nk - 1)
        def _():
            o_ref[...] = acc[...].astype(out_dtype)

    def b_spec(off, n):
        def index(i, j, k):
            mine = (j >= off) & (j < off + n)
            return jnp.where(mine, k, 0), jnp.clip(j - off, 0, n - 1)
        return pl.BlockSpec((tk, tn), index)

    return pl.pallas_call(
        body, name=name, grid=(M // tm, nj, nk),
        in_specs=[pl.BlockSpec((tk, tm), lambda i, j, k: (k, i))] + [b_spec(off, n) for off, n in zip(offs, njs)],
        out_specs=pl.BlockSpec((tm, tn), lambda i, j, k: (i, j)),
        out_shape=jax.ShapeDtypeStruct((M, nj * tn), out_dtype), scratch_shapes=[pltpu.VMEM((tm, tn), F32)],
        compiler_params=_cparams("parallel", "parallel", "arbitrary"),
    )(a, *b_parts)


def _rows(fn, ins, outs, *, name, tm, bcast=(), acc=()):
    R = ins[0].shape[0]
    assert R % tm == 0, (R, tm)
    n_in, n_b, n_out, n_acc = len(ins), len(bcast), len(outs), len(acc)

    def body(*refs):
        in_refs = refs[:n_in + n_b]
        out_refs = refs[n_in + n_b:n_in + n_b + n_out]
        acc_refs = refs[n_in + n_b + n_out:]
        res = fn(*[r[...] for r in in_refs])
        if not isinstance(res, (tuple, list)):
            res = (res,)
        for r, v in zip(out_refs, res[:n_out]):
            r[...] = v.astype(r.dtype)
        first = pl.program_id(0) == 0
        for r, v in zip(acc_refs, res[n_out:]):
            @pl.when(first)
            def _(r=r, v=v):
                r[...] = v.astype(r.dtype)

            @pl.when(jnp.logical_not(first))
            def _(r=r, v=v):
                r[...] += v.astype(r.dtype)

    in_specs = [pl.BlockSpec((tm, a.shape[1]), lambda i: (i, 0)) for a in ins]
    in_specs += [pl.BlockSpec(b.shape, lambda i, nd=b.ndim: (0,) * nd) for b in bcast]
    out_specs = [pl.BlockSpec((tm, c), lambda i: (i, 0)) for c, _ in outs]
    out_specs += [pl.BlockSpec(s, lambda i, nd=len(s): (0,) * nd) for s, _ in acc]
    out_shape = [jax.ShapeDtypeStruct((R, c), dt) for c, dt in outs]
    out_shape += [jax.ShapeDtypeStruct(s, dt) for s, dt in acc]
    res = pl.pallas_call(
        body, name=name, grid=(R // tm,), in_specs=in_specs, out_specs=out_specs, out_shape=out_shape,
        compiler_params=_cparams("arbitrary" if acc else "parallel"),
    )(*ins, *bcast)
    return res


def _exact_dot(sel, x, left):
    hi = x.astype(BF16)
    r1 = x - hi.astype(F32)
    mid = r1.astype(BF16)
    lo = (r1 - mid.astype(F32)).astype(BF16)
    dot = (lambda p: jnp.dot(sel, p, preferred_element_type=F32)) if left else (
        lambda p: jnp.dot(p, sel, preferred_element_type=F32))
    return dot(hi) + dot(mid) + dot(lo)


def _cumsum_rows(ins, *, name, width, reverse=False, pre=None, bcast=(), post=None, outs=None):
    S = ins[0].shape[0]
    outs = outs or [(width, F32)]
    out_dtypes = [dt for _, dt in outs]
    tb = 256
    nb = S // tb
    assert S % tb == 0
    n_in = len(ins) + len(bcast)
    n_out = len(out_dtypes)

    def body(*refs):
        in_refs, o_refs, carry = refs[:n_in], refs[n_in:n_in + n_out], refs[n_in + n_out]

        @pl.when(pl.program_id(0) == 0)
        def _():
            carry[...] = jnp.zeros_like(carry)

        xv = in_refs[0][...] if pre is None else pre(*[r[...] for r in in_refs])
        r_ = lax.broadcasted_iota(jnp.int32, (tb, tb), 0)
        c_ = lax.broadcasted_iota(jnp.int32, (tb, tb), 1)
        tri = jnp.where((c_ >= r_) if reverse else (c_ <= r_), 1.0, 0.0).astype(BF16)
        cs = _exact_dot(tri, xv, left=True) + carry[...]
        outs = (cs,) if post is None else post(cs)
        for o_ref, v in zip(o_refs, outs):
            o_ref[...] = v.astype(o_ref.dtype)
        carry[...] = cs[0:1, :] if reverse else cs[tb - 1:tb, :]

    blk = (lambda i: (nb - 1 - i, 0)) if reverse else (lambda i: (i, 0))
    in_specs = [pl.BlockSpec((tb, a.shape[1]), blk) for a in ins]
    in_specs += [pl.BlockSpec(b.shape, lambda i, nd=b.ndim: (0,) * nd) for b in bcast]
    return pl.pallas_call(
        body, name=name, grid=(nb,), in_specs=in_specs, out_specs=[pl.BlockSpec((tb, c), blk) for c, _ in outs],
        out_shape=[jax.ShapeDtypeStruct((S, c), dt) for c, dt in outs], scratch_shapes=[pltpu.VMEM((1, width), F32)],
        compiler_params=_cparams("arbitrary"),
    )(*ins, *bcast)


def _rms_fwd(x, g):
    r = lax.rsqrt(jnp.mean(x * x, axis=1, keepdims=True) + RMS_EPS)
    return (x * r) * g


def _rms_bwd(dn, x, dres, g):
    r = lax.rsqrt(jnp.mean(x * x, axis=1, keepdims=True) + RMS_EPS)
    xh = x * r
    w = dn * g
    dx = r * (w - xh * jnp.mean(w * xh, axis=1, keepdims=True))
    return dres + dx, jnp.sum(dn * xh, axis=0, keepdims=True)


def _final_stage(h, tgt, g):
    r = lax.rsqrt(jnp.mean(h * h, axis=1, keepdims=True) + RMS_EPS)
    xh = h * r
    diff = xh * g - tgt
    loss = 0.5 * jnp.sum(jnp.mean(diff * diff, axis=1, keepdims=True), axis=0, keepdims=True)
    dy = diff * (1.0 / D_MODEL)
    w = dy * g
    dh = r * (w - xh * jnp.mean(w * xh, axis=1, keepdims=True))
    return dh, jnp.sum(dy * xh, axis=0, keepdims=True), jnp.broadcast_to(loss, (1, LANES))


def _ple_fwd(h, u, t):
    return h + u * _sigmoid(t)


def _ple_bwd(dh, u, t):
    s = _sigmoid(t)
    return dh * s, dh * u * s * (1.0 - s)


def _ple_norm(h, u, t, g):
    h2 = _ple_fwd(h, u, t)
    return h2, _rms_fwd(h2, g)


def _ple_loss(h, u, t, tgt, g):
    dh, g_gain, loss = _final_stage(_ple_fwd(h, u, t), tgt, g)
    du, dt = _ple_bwd(dh, u, t)
    return du, dt, dh, g_gain, loss


def _norm_ple_bwd(dn, x, dres, u, t, g):
    dh, g_gain = _rms_bwd(dn, x, dres, g)
    du, dt = _ple_bwd(dh, u, t)
    return dh, du, dt, g_gain


def _gate_fwd(o, z):
    return o * (z * _sigmoid(z))


def _head_sums(prod):
    tm, width = prod.shape
    cols = [jnp.broadcast_to(jnp.sum(prod[:, b * LANES:(b + 1) * LANES], axis=1, keepdims=True), (tm, LANES))
            for b in range(width // LANES)]
    return jnp.concatenate(cols, axis=1)


def _gate_bwd(dg, o, z, lse):
    s = _sigmoid(z)
    do = dg * (z * s)
    dz = dg * o * (s * (1.0 + z * (1.0 - s)))
    lane = lax.broadcasted_iota(jnp.int32, do.shape, 1) % LANES
    return do, dz, jnp.where(lane < LANES // 2, lse, _head_sums(do * o))


def _log_forget(fl, b):
    u = fl + b
    return jnp.minimum(u, 0.0) - jnp.log(1.0 + jnp.exp(-jnp.abs(u)))


def _adamw(w, g, m, v):
    m = ADAM_B1 * m + (1.0 - ADAM_B1) * g
    v = ADAM_B2 * v + (1.0 - ADAM_B2) * (g * g)
    m_hat = m / (1.0 - ADAM_B1 ** ADAM_STEP)
    v_hat = v / (1.0 - ADAM_B2 ** ADAM_STEP)
    delta = -ADAM_LR * (m_hat / (jnp.sqrt(v_hat) + ADAM_EPS) + ADAM_WD * w)
    return delta, m, v


FOX_TQ = 1024


def _split3(x):
    p1 = x.astype(BF16).astype(F32)
    r = x - p1
    p2 = r.astype(BF16).astype(F32)
    return p1, p2, r - p2


def _lane_in_head(shape):
    return lax.broadcasted_iota(jnp.int32, shape, 1) % FOX_HEAD_DIM


def _query_extras(x):
    lm = _lane_in_head(x.shape)
    p1, p2, p3 = _split3(x)
    return jnp.where(lm == 0, p1, jnp.where(lm == 1, p2, jnp.where(lm == 2, p3, jnp.where(lm < 6, 1.0, 0.0))))


def _key_extras(c):
    lm = _lane_in_head(c.shape)
    p1, p2, p3 = _split3(c)
    return jnp.where(lm < 3, 1.0, jnp.where(lm == 3, -p1, jnp.where(lm == 4, -p2, jnp.where(lm == 5, -p3, 0.0))))


def _head_lane_base(head):
    return LANES * (head // 2) + FOX_HEAD_DIM * (1 - head % 2)


def _fox_extras(c):
    head = lax.broadcasted_iota(jnp.int32, (LANES, D_MODEL), 0)
    lane = lax.broadcasted_iota(jnp.int32, (LANES, D_MODEL), 1)
    owner = 2 * (lane // LANES) + 1 - (lane % LANES) // FOX_HEAD_DIM
    wide = _exact_dot(jnp.where(head == owner, 1.0, 0.0).astype(BF16), c, left=False)
    return _query_extras(wide), _key_extras(wide)


def _swapped_head_sums(prod):
    tm, width = prod.shape
    lane = lax.broadcasted_iota(jnp.int32, (tm, LANES), 1)
    low = lane < FOX_HEAD_DIM
    cols = []
    for b in range(width // LANES):
        blk = prod[:, b * LANES:(b + 1) * LANES]
        sa = jnp.sum(jnp.where(low, blk, 0.0), axis=1, keepdims=True)
        sb = jnp.sum(jnp.where(low, 0.0, blk), axis=1, keepdims=True)
        cols.append(jnp.where(low, sb, sa))
    return jnp.concatenate(cols, axis=1)


def _gate_bwd_fox(dg, o, z, a):
    s = _sigmoid(z)
    do = dg * (z * s)
    dz = dg * o * (s * (1.0 + z * (1.0 - s)))
    lm = _lane_in_head(do.shape)
    d1, d2, d3 = _split3(-_swapped_head_sums(do * o))
    dx = jnp.where(lm == 0, d1, jnp.where(lm == 1, d2, jnp.where(lm == 2, d3, 0.0)))
    return do, dz, _query_extras(a), dx


def _forget_dc(drx, dkx):
    lane = lax.broadcasted_iota(jnp.int32, (D_MODEL, LANES), 0)
    head = lax.broadcasted_iota(jnp.int32, (D_MODEL, LANES), 1)
    base = _head_lane_base(head)
    pick = lambda l: jnp.where((lane == base + l) & (head < FOX_HEADS), 1.0, 0.0).astype(BF16)
    return _exact_dot(pick(0), drx, left=False) - _exact_dot(pick(3), dkx, left=False)


def _forget_bwd(dl, fl, b):
    du = dl * (1.0 / (1.0 + jnp.exp(fl + b)))
    return du, jnp.sum(du, axis=0, keepdims=True)


def _chip_exchange(src_of, out_ref, send, recv):
    x, y, c = lax.axis_index("x"), lax.axis_index("y"), lax.axis_index("c")
    chips = [(1 - x, y), (x, 1 - y), (1 - x, 1 - y)]
    me = 2 * x + y
    sends = [pltpu.make_async_remote_copy(src_ref=src_of(cx, cy), dst_ref=out_ref.at[me], send_sem=send.at[k],
                                          recv_sem=recv.at[k], device_id=(cx, cy, c), device_id_type=MESH)
             for k, (cx, cy) in enumerate(chips)]

    def start():
        for cp in sends:
            cp.start()

    def wait():
        for k, (cx, cy) in enumerate(chips):
            slot = out_ref.at[2 * cx + cy]
            pltpu.make_async_remote_copy(src_ref=slot, dst_ref=slot, send_sem=send.at[k], recv_sem=recv.at[k],
                                         device_id=(cx, cy, c), device_id_type=MESH).wait_recv()
        for cp in sends:
            cp.wait_send()

    return start, wait


def _fox_fwd(qkv, qx, kx, z, w_b):
    S = qkv.shape[0]
    tq = min(FOX_TQ, S)
    nq = S // tq
    nt = (((1,), (1,)), ((), ()))

    def body(q_ref, qx_ref, k_ref, v_ref, kx_ref, z_ref, wb_ref, o_ref, a_ref, g_ref, wall_ref, send, recv):
        i = pl.program_id(1)
        start, wait = _chip_exchange(lambda cx, cy: wb_ref, wall_ref, send, recv)
        pl.when((pl.program_id(0) == 0) & (i == 0))(start)
        low = lax.broadcasted_iota(jnp.int32, (tq, LANES), 1) < FOX_HEAD_DIM
        row = lax.broadcasted_iota(jnp.int32, (tq, tq), 0)
        col = lax.broadcasted_iota(jnp.int32, (tq, tq), 1)
        q2, x2 = q_ref[...], qx_ref[...]
        qa = (jnp.where(low, q2, x2), jnp.where(low, x2, q2))

        def step(kb, carry, diag):
            start = pl.multiple_of(kb * tq, tq)
            k2 = k_ref[pl.ds(start, tq), :]
            v2 = v_ref[pl.ds(start, tq), :]
            y2 = kx_ref[pl.ds(start, tq), :]
            one = jnp.ones_like(v2)
            ka = (jnp.where(low, k2, y2), jnp.where(low, y2, k2))
            va = (jnp.where(low, v2, one), jnp.where(low, one, v2))
            out = []
            for hh in range(2):
                m, acc = carry[hh]
                s = lax.dot_general(qa[hh], ka[hh], nt, preferred_element_type=F32)
                if diag:
                    s = jnp.where(col <= row, s, NEG)
                m_new = jnp.maximum(m, jnp.max(s, axis=1, keepdims=True))
                pr = jnp.exp(s - m_new)
                acc = jnp.exp(m - m_new) * acc + jnp.dot(pr.astype(BF16), va[hh], preferred_element_type=F32)
                out.append((m_new, acc))
            return tuple(out)

        init = ((jnp.full((tq, 1), NEG, F32), jnp.zeros((tq, LANES), F32)),) * 2
        carry = lax.fori_loop(0, i, functools.partial(step, diag=False), init)
        (m_a, acc_a), (m_b, acc_b) = step(i, carry, True)
        l_a, l_b = acc_a[:, FOX_HEAD_DIM:FOX_HEAD_DIM + 1], acc_b[:, 0:1]
        xf = x2.astype(F32)
        c_a = xf[:, 64:65] + xf[:, 65:66] + xf[:, 66:67]
        c_b = xf[:, 0:1] + xf[:, 1:2] + xf[:, 2:3]
        o = jnp.where(low, acc_a / l_a, acc_b / l_b)
        o_ref[...] = o
        g_ref[...] = _gate_fwd(o, z_ref[...]).astype(BF16)
        a_ref[...] = jnp.where(low, jnp.broadcast_to(c_b - (m_b + jnp.log(l_b)), (tq, LANES)),
                               jnp.broadcast_to(c_a - (m_a + jnp.log(l_a)), (tq, LANES)))
        pl.when((pl.program_id(0) == FOX_HEADS // 2 - 1) & (i == nq - 1))(wait)

    blk = lambda cb: pl.BlockSpec((tq, LANES), lambda h, i, cb=cb: (i, cb + h))
    res = lambda cb: pl.BlockSpec((S, LANES), lambda h, i, cb=cb: (0, cb + h))
    return pl.pallas_call(
        body, name="fox_attn_fwd", grid=(FOX_HEADS // 2, nq),
        in_specs=[blk(0), blk(0), res(8), res(16), res(0), blk(0), ANY],
        out_specs=[blk(0), blk(0), blk(0), ANY],
        out_shape=[jax.ShapeDtypeStruct((S, D_MODEL), F32), jax.ShapeDtypeStruct((S, D_MODEL), F32),
                   jax.ShapeDtypeStruct((S, D_MODEL), BF16), jax.ShapeDtypeStruct((4,) + w_b.shape, w_b.dtype)],
        scratch_shapes=[pltpu.SemaphoreType.DMA((3,)), pltpu.SemaphoreType.DMA((3,))],
        compiler_params=_cparams("arbitrary", "arbitrary"),
    )(qkv, qx, qkv, qkv, kx, z, w_b)


def _fox_bwd(qkv, do, qxa, dx, kx, g_b):
    S = qkv.shape[0]
    tq = min(FOX_TQ, S)
    nq = S // tq
    nt = (((1,), (1,)), ((), ()))
    tn = (((0,), (0,)), ((), ()))

    def body(q_ref, qx_ref, do_ref, dx_ref, k_ref, v_ref, kx_ref, gb_ref, dq_ref, dr_ref, dk_ref, dv_ref, dkx_ref,
             got_ref, send, recv):
        kb = pl.program_id(1)
        start, wait = _chip_exchange(lambda cx, cy: gb_ref.at[2 * cx + cy], got_ref, send, recv)
        pl.when((pl.program_id(0) == 0) & (kb == 0))(start)
        low = lax.broadcasted_iota(jnp.int32, (tq, LANES), 1) < FOX_HEAD_DIM
        row = lax.broadcasted_iota(jnp.int32, (tq, tq), 0)
        col = lax.broadcasted_iota(jnp.int32, (tq, tq), 1)

        @pl.when(kb == 0)
        def _():
            dq_ref[...] = jnp.zeros_like(dq_ref)
            dr_ref[...] = jnp.zeros_like(dr_ref)

        k2, v2, y2 = k_ref[...], v_ref[...], kx_ref[...]
        one = jnp.ones_like(v2)
        ka = (jnp.where(low, k2, y2), jnp.where(low, y2, k2))
        va = (jnp.where(low, v2, one), jnp.where(low, one, v2))

        def step(qb, carry, diag):
            start = pl.multiple_of(qb * tq, tq)
            q2 = q_ref[pl.ds(start, tq), :]
            x2 = qx_ref[pl.ds(start, tq), :]
            d2 = do_ref[pl.ds(start, tq), :]
            e2 = dx_ref[pl.ds(start, tq), :]
            qa = (jnp.where(low, q2, x2), jnp.where(low, x2, q2))
            da = (jnp.where(low, d2, e2), jnp.where(low, e2, d2))
            new, res = [], []
            for hh in range(2):
                dk, dv = carry[hh]
                s = lax.dot_general(qa[hh], ka[hh], nt, preferred_element_type=F32)
                if diag:
                    s = jnp.where(col <= row, s, NEG)
                pr = jnp.exp(s)
                ds = pr * lax.dot_general(da[hh], va[hh], nt, preferred_element_type=F32)
                prb, dsb = pr.astype(BF16), ds.astype(BF16)
                dv = dv + lax.dot_general(prb, da[hh], tn, preferred_element_type=F32)
                dk = dk + lax.dot_general(dsb, qa[hh], tn, preferred_element_type=F32)
                res.append(jnp.dot(dsb, ka[hh], preferred_element_type=F32))
                new.append((dk, dv))
            dq_ref[pl.ds(start, tq), :] += jnp.where(low, res[0], res[1])
            dr_ref[pl.ds(start, tq), :] += jnp.where(low, res[1], res[0])
            return tuple(new)

        init = ((jnp.zeros((tq, LANES), F32), jnp.zeros((tq, LANES), F32)),) * 2
        carry = step(kb, init, True)
        (dk_a, dv_a), (dk_b, dv_b) = lax.fori_loop(kb + 1, nq, functools.partial(step, diag=False), carry)
        dk_ref[...] = jnp.where(low, dk_a, dk_b).astype(BF16)
        dv_ref[...] = jnp.where(low, dv_a, dv_b).astype(BF16)
        dkx_ref[...] = jnp.where(low, dk_b, dk_a)

        @pl.when(kb == nq - 1)
        def _():
            dq_ref[...] = dq_ref[...] * (FOX_HEAD_DIM ** -0.5)

        pl.when((pl.program_id(0) == FOX_HEADS // 2 - 1) & (kb == nq - 1))(wait)

    res = lambda cb: pl.BlockSpec((S, LANES), lambda h, k, cb=cb: (0, cb + h))
    blk = lambda cb: pl.BlockSpec((tq, LANES), lambda h, k, cb=cb: (k, cb + h))
    f32, b16 = jax.ShapeDtypeStruct((S, D_MODEL), F32), jax.ShapeDtypeStruct((S, D_MODEL), BF16)
    return pl.pallas_call(
        body, name="fox_attn_bwd", grid=(FOX_HEADS // 2, nq),
        in_specs=[res(0), res(0), res(0), res(0), blk(8), blk(16), blk(0), ANY],
        out_specs=[res(0), res(0), blk(0), blk(0), blk(0), ANY],
        out_shape=[f32, f32, b16, b16, f32, jax.ShapeDtypeStruct(g_b.shape, g_b.dtype)],
        scratch_shapes=[pltpu.SemaphoreType.DMA((3,)), pltpu.SemaphoreType.DMA((3,))],
        compiler_params=_cparams("arbitrary", "arbitrary"),
    )(qkv, qxa, do, dx, qkv, qkv, kx, g_b)


def _alibi_slopes():
    n = DIL_GROUPS * DIL_HEADS_PER_GROUP
    s = np.float32(2.0) ** (np.float32(-ALIBI_MAX_EXP) * np.arange(1, n + 1, dtype=np.float32) / np.float32(n))
    return s.astype(np.float32).reshape(DIL_GROUPS, DIL_HEADS_PER_GROUP)


W = DIL_WINDOW_STEPS
DIL_SCALE = DIL_HEAD_DIM ** -0.5


DIL_ROWS = 2048
DIL_BWD_ROWS = 4096
NT = (((1,), (1,)), ((), ()))
TN = (((0,), (0,)), ((), ()))
QKV_BLOCKS = DIL_QKV // LANES


def _slope_table():
    t = np.zeros((DIL_HEADS_PER_GROUP, 8, LANES), np.float32)
    t[:, :DIL_GROUPS, :] = _alibi_slopes().T[:, :, None]
    return jnp.asarray(t)


def _phase_rows(start, d):
    return pl.ds(start, W, stride=d) if d > 1 else pl.ds(start, W)


def _for_phases(d, unit):
    if d == 1:
        unit(0)
    else:
        lax.fori_loop(0, d, lambda r, c: (unit(r), c)[1], 0)


def _window_geometry(shape, q_axis_offset):
    i_ = lax.broadcasted_iota(jnp.int32, shape, 0)
    j_ = lax.broadcasted_iota(jnp.int32, shape, 1)
    dist = q_axis_offset + i_ - j_
    return i_, j_, dist, (dist >= 0) & (dist <= W)


def _dil_fwd(qkv, z):
    S = qkv.shape[0]
    rb = min(DIL_ROWS, S)
    nb = S // rb
    spans = [W * d for _, d in DIL_PATTERN]

    def body(*refs):
        slope_ref = refs[0]
        q, kc, kp, vc, vp = refs[1:4], refs[4:7], refs[7:10], refs[10:13], refs[13:16]
        z_ref, o_ref, g_ref, l_ref = refs[16:20]
        og, lg = refs[20:23], refs[23:26]
        i = pl.program_id(1)
        _, j_, dist, inwin = _window_geometry((W, 2 * W), W)
        distf = dist.astype(F32)
        for g, (_, d) in enumerate(DIL_PATTERN):
            span = spans[g]
            bias = -(slope_ref[0, g:g + 1, 0:1] * float(d)) * distf
            def phase(r, g=g, d=d, span=span, bias=bias):
                kprev, vprev = kp[g][_phase_rows(r, d), :].astype(BF16), vp[g][_phase_rows(r, d), :].astype(BF16)
                for b in range(rb // span):
                    rows = _phase_rows(b * span + r, d)
                    kcur, vcur = kc[g][rows, :].astype(BF16), vc[g][rows, :].astype(BF16)
                    valid = inwin if b > 0 else inwin & ((j_ >= W) | (i > 0))
                    k2 = jnp.concatenate([kprev, kcur], axis=0)
                    v2 = jnp.concatenate([vprev, vcur], axis=0)
                    s = lax.dot_general(q[g][rows, :].astype(BF16), k2, NT, preferred_element_type=F32)
                    s = jnp.where(valid, s * DIL_SCALE + bias, NEG)
                    m = jnp.max(s, axis=1, keepdims=True)
                    pr = jnp.exp(s - m)
                    l = jnp.sum(pr, axis=1, keepdims=True)
                    og[g][rows, :] = jnp.dot(pr.astype(BF16), v2, preferred_element_type=F32) / l
                    lg[g][rows, :] = jnp.broadcast_to(m + jnp.log(l), (W, LANES))
                    kprev, vprev = kcur, vcur

            _for_phases(d, phase)

        def mix(cix, c):
            sl = pl.ds(pl.multiple_of(cix * 256, 256), 256)
            l1, l2, l3 = lg[0][sl, :], lg[1][sl, :], lg[2][sl, :]
            m = jnp.maximum(jnp.maximum(l1, l2), l3)
            e1, e2, e3 = jnp.exp(l1 - m), jnp.exp(l2 - m), jnp.exp(l3 - m)
            tot = e1 + e2 + e3
            o = (e1 * og[0][sl, :] + e2 * og[1][sl, :] + e3 * og[2][sl, :]) / tot
            o_ref[sl, :] = o
            g_ref[sl, :] = _gate_fwd(o, z_ref[sl, :]).astype(BF16)
            l_ref[sl, :] = m + jnp.log(tot)
            return c

        lax.fori_loop(0, rb // 256, mix, 0)

    cur = lambda off, g: pl.BlockSpec((rb, LANES), lambda h, i: (i, off * QKV_BLOCKS + g * DIL_HEADS_PER_GROUP + h))
    prev = lambda off, g: pl.BlockSpec(
        (spans[g], LANES),
        lambda h, i: (jnp.maximum(i * (rb // spans[g]) - 1, 0), off * QKV_BLOCKS + g * DIL_HEADS_PER_GROUP + h))
    row = pl.BlockSpec((rb, LANES), lambda h, i: (i, h))
    groups = range(DIL_GROUPS)
    in_specs = [pl.BlockSpec((1, 8, LANES), lambda h, i: (h, 0, 0))]
    in_specs += [cur(0, g) for g in groups] + [cur(1, g) for g in groups] + [prev(1, g) for g in groups]
    in_specs += [cur(2, g) for g in groups] + [prev(2, g) for g in groups] + [row]
    f32 = jax.ShapeDtypeStruct((S, D_MODEL), F32)
    return pl.pallas_call(
        body, name="dil_attn_fwd", grid=(DIL_HEADS_PER_GROUP, nb), in_specs=in_specs,
        out_specs=[row, row, row], out_shape=[f32, jax.ShapeDtypeStruct((S, D_MODEL), BF16), f32],
        scratch_shapes=[pltpu.VMEM((rb, LANES), F32)] * 6,
        compiler_params=_cparams("parallel", "parallel"),
    )(_slope_table(), *([qkv] * 15), z)


def _dil_bwd_q(qkv, do, ld, buf, g):
    S = qkv.shape[0]
    d = DIL_PATTERN[g][1]
    span = W * d
    rb = min(max(DIL_ROWS, min(8 * span, DIL_BWD_ROWS)), S)
    nb = S // rb

    def body(slope_ref, q_ref, kc, kp, vc, vp, do_ref, ld_ref, buf_ref, dq_ref, dq_f32):
        i = pl.program_id(1)
        _, j_, dist, inwin = _window_geometry((W, 2 * W), W)
        bias = -(slope_ref[0, g:g + 1, 0:1] * float(d)) * dist.astype(F32)
        def phase(r):
            kprev, vprev = kp[_phase_rows(r, d), :].astype(BF16), vp[_phase_rows(r, d), :].astype(BF16)
            for b in range(rb // span):
                rows = _phase_rows(b * span + r, d)
                kcur, vcur = kc[rows, :].astype(BF16), vc[rows, :].astype(BF16)
                valid = inwin if b > 0 else inwin & ((j_ >= W) | (i > 0))
                k2 = jnp.concatenate([kprev, kcur], axis=0)
                v2 = jnp.concatenate([vprev, vcur], axis=0)
                ld = ld_ref[rows, :]
                s = lax.dot_general(q_ref[rows, :].astype(BF16), k2, NT, preferred_element_type=F32)
                s = jnp.where(valid, s * DIL_SCALE + bias - ld[:, 0:1], NEG)
                pr = jnp.exp(s)
                dp = lax.dot_general(do_ref[rows, :].astype(BF16), v2, NT, preferred_element_type=F32)
                ds = pr * (dp - ld[:, LANES // 2:LANES // 2 + 1])
                dq_f32[rows, :] = jnp.dot(ds.astype(BF16), k2, preferred_element_type=F32) * DIL_SCALE
                kprev, vprev = kcur, vcur

        _for_phases(d, phase)
        dq_ref[...] = dq_f32[...].astype(BF16)

    col = lambda off: off * QKV_BLOCKS + g * DIL_HEADS_PER_GROUP
    cur = lambda off: pl.BlockSpec((rb, LANES), lambda h, i: (i, col(off) + h))
    prev = lambda off: pl.BlockSpec((span, LANES), lambda h, i: (jnp.maximum(i * (rb // span) - 1, 0), col(off) + h))
    row = pl.BlockSpec((rb, LANES), lambda h, i: (i, h))
    return pl.pallas_call(
        body, name=f"dil_attn_bwd_q_g{g}", grid=(DIL_HEADS_PER_GROUP, nb),
        in_specs=[pl.BlockSpec((1, 8, LANES), lambda h, i: (h, 0, 0)), cur(0), cur(1), prev(1), cur(2), prev(2),
                  row, row, pl.BlockSpec(memory_space=pl.ANY)],
        out_specs=pl.BlockSpec((rb, LANES), lambda h, i: (i, g * DIL_HEADS_PER_GROUP + h)),
        out_shape=jax.ShapeDtypeStruct(buf.shape, buf.dtype), input_output_aliases={8: 0},
        scratch_shapes=[pltpu.VMEM((rb, LANES), F32)],
        compiler_params=_cparams("parallel", "parallel"),
    )(_slope_table(), qkv, qkv, qkv, qkv, qkv, do, ld, buf)


def _dil_bwd_kv(qkv, do, ld, bufk, bufv, g):
    S = qkv.shape[0]
    d = DIL_PATTERN[g][1]
    span = W * d
    rb = min(max(DIL_ROWS, min(8 * span, DIL_BWD_ROWS)), S)
    nb = S // rb
    nub = rb // span

    def body(slope_ref, k_ref, v_ref, qc, qn, doc, don, ldc, ldn, bufk_ref, bufv_ref, dk_ref, dv_ref, dk_f32, dv_f32):
        i = pl.program_id(1)
        i_, _, dist, inwin = _window_geometry((2 * W, W), 0)
        bias = -(slope_ref[0, g:g + 1, 0:1] * float(d)) * dist.astype(F32)
        def phase(r):
            first = _phase_rows(r, d)
            qcur, docur, ldcur = qc[first, :].astype(BF16), doc[first, :].astype(BF16), ldc[first, :]
            for b in range(nub):
                rows = _phase_rows(b * span + r, d)
                if b < nub - 1:
                    nxt = _phase_rows((b + 1) * span + r, d)
                    qnext, donext, ldnext, valid = qc[nxt, :].astype(BF16), doc[nxt, :].astype(BF16), ldc[nxt, :], inwin
                else:
                    qnext, donext, ldnext = qn[first, :].astype(BF16), don[first, :].astype(BF16), ldn[first, :]
                    valid = inwin & ((i_ < W) | (i < nb - 1))
                q2 = jnp.concatenate([qcur, qnext], axis=0)
                do2 = jnp.concatenate([docur, donext], axis=0)
                ld2 = jnp.concatenate([ldcur, ldnext], axis=0)
                s = lax.dot_general(q2, k_ref[rows, :].astype(BF16), NT, preferred_element_type=F32)
                s = jnp.where(valid, s * DIL_SCALE + bias - ld2[:, 0:1], NEG)
                pr = jnp.exp(s)
                dp = lax.dot_general(do2, v_ref[rows, :].astype(BF16), NT, preferred_element_type=F32)
                ds = pr * (dp - ld2[:, LANES // 2:LANES // 2 + 1])
                dv_f32[rows, :] = lax.dot_general(pr.astype(BF16), do2, TN, preferred_element_type=F32)
                dk_f32[rows, :] = lax.dot_general(ds.astype(BF16), q2, TN, preferred_element_type=F32) * DIL_SCALE
                qcur, docur, ldcur = qnext, donext, ldnext

        _for_phases(d, phase)
        dk_ref[...] = dk_f32[...].astype(BF16)
        dv_ref[...] = dv_f32[...].astype(BF16)

    col = lambda off: off * QKV_BLOCKS + g * DIL_HEADS_PER_GROUP
    cur = lambda off: pl.BlockSpec((rb, LANES), lambda h, i: (i, col(off) + h))
    nxt_blk = lambda i: jnp.minimum((i + 1) * nub, S // span - 1)
    nxt = lambda off: pl.BlockSpec((span, LANES), lambda h, i: (nxt_blk(i), col(off) + h))
    row = pl.BlockSpec((rb, LANES), lambda h, i: (i, h))
    row_nxt = pl.BlockSpec((span, LANES), lambda h, i: (nxt_blk(i), h))
    any_ = pl.BlockSpec(memory_space=pl.ANY)
    out = pl.BlockSpec((rb, LANES), lambda h, i: (i, g * DIL_HEADS_PER_GROUP + h))
    return pl.pallas_call(
        body, name=f"dil_attn_bwd_kv_g{g}", grid=(DIL_HEADS_PER_GROUP, nb),
        in_specs=[pl.BlockSpec((1, 8, LANES), lambda h, i: (h, 0, 0)), cur(1), cur(2), cur(0), nxt(0),
                  row, row_nxt, row, row_nxt, any_, any_],
        out_specs=[out, out],
        out_shape=[jax.ShapeDtypeStruct(bufk.shape, bufk.dtype), jax.ShapeDtypeStruct(bufv.shape, bufv.dtype)],
        input_output_aliases={9: 0, 10: 1},
        scratch_shapes=[pltpu.VMEM((rb, LANES), F32)] * 2,
        compiler_params=_cparams("parallel", "parallel"),
    )(_slope_table(), qkv, qkv, qkv, qkv, do, do, ld, ld, bufk, bufv)


def _place():
    x, y, c = lax.axis_index("x"), lax.axis_index("y"), lax.axis_index("c")
    chips = [(1 - x, y), (x, 1 - y), (1 - x, 1 - y)]
    return x, y, c, chips


def _gather_weights(wb):
    R = wb.shape[0]
    H = R // 2

    def body(w_ref, out_ref, send1, recv1, send2, recv2):
        x, y, c, chips = _place()
        me = 2 * x + y
        sib = (x, y, 1 - c)
        half = pl.ds(c * H, H)
        first = [pltpu.make_async_remote_copy(
            src_ref=w_ref.at[half], dst_ref=out_ref.at[me, half], send_sem=send1.at[k], recv_sem=recv1.at[k],
            device_id=(*chip, c), device_id_type=MESH) for k, chip in enumerate(chips)]
        for cp in first:
            cp.start()
        passed = []
        for k, (cx, cy) in enumerate(chips):
            slot = out_ref.at[2 * cx + cy, half]
            pltpu.make_async_remote_copy(src_ref=slot, dst_ref=slot, send_sem=send1.at[k], recv_sem=recv1.at[k],
                                         device_id=(cx, cy, c), device_id_type=MESH).wait_recv()
            cp = pltpu.make_async_remote_copy(src_ref=slot, dst_ref=slot, send_sem=send2.at[k], recv_sem=recv2.at[k],
                                              device_id=sib, device_id_type=MESH)
            cp.start()
            passed.append(cp)
        for k, (cx, cy) in enumerate(chips):
            slot = out_ref.at[2 * cx + cy, pl.ds((1 - c) * H, H)]
            pltpu.make_async_remote_copy(src_ref=slot, dst_ref=slot, send_sem=send2.at[k], recv_sem=recv2.at[k],
                                         device_id=sib, device_id_type=MESH).wait_recv()
        for cp in first + passed:
            cp.wait_send()

    return pl.pallas_call(
        body, name="gather_weights", in_specs=[ANY], out_specs=ANY,
        out_shape=jax.ShapeDtypeStruct((4, R, D_MODEL), wb.dtype),
        scratch_shapes=[pltpu.SemaphoreType.DMA((3,)), pltpu.SemaphoreType.DMA((3,)),
                        pltpu.SemaphoreType.DMA((3,)), pltpu.SemaphoreType.DMA((3,))],
    )(wb)


def _swap_halves(g):
    H = g.shape[1] // 2

    def body(g_ref, out_ref, send, recv):
        x, y, c, _ = _place()
        sib = (x, y, 1 - c)
        cps = [pltpu.make_async_remote_copy(
            src_ref=g_ref.at[s, pl.ds((1 - c) * H, H)], dst_ref=out_ref.at[s], send_sem=send.at[s],
            recv_sem=recv.at[s], device_id=sib, device_id_type=MESH) for s in range(4)]
        for cp in cps:
            cp.start()
        for cp in cps:
            cp.wait()

    return pl.pallas_call(
        body, name="swap_halves", in_specs=[ANY], out_specs=ANY,
        out_shape=jax.ShapeDtypeStruct((4, H, D_MODEL), g.dtype),
        scratch_shapes=[pltpu.SemaphoreType.DMA((4,)), pltpu.SemaphoreType.DMA((4,))],
    )(g)


def _sibling_half(f, name):
    def body(f_ref, out_ref, send, recv):
        x, y, c, _ = _place()
        cp = pltpu.make_async_remote_copy(src_ref=f_ref, dst_ref=out_ref, send_sem=send, recv_sem=recv,
                                          device_id=(x, y, 1 - c), device_id_type=MESH)
        cp.start()
        cp.wait()

    return pl.pallas_call(
        body, name=name, in_specs=[ANY], out_specs=ANY,
        out_shape=jax.ShapeDtypeStruct(f.shape, f.dtype),
        scratch_shapes=[pltpu.SemaphoreType.DMA, pltpu.SemaphoreType.DMA],
    )(f)


def _gather_tiles(tile, name):
    m_per = tile.shape[0]

    def body(x_ref, out_ref, send_sems, recv_sems, local_sem):
        x, y, c, chips = _place()
        me, sibling = (x, y, c), (x, y, 1 - c)

        def rows(px, py, pc):
            return out_ref.at[pl.ds((4 * px + 2 * py + pc) * m_per, m_per), :]

        def copy(k, block, to, src=None):
            return pltpu.make_async_remote_copy(
                src_ref=rows(*block) if src is None else src, dst_ref=rows(*block),
                send_sem=send_sems.at[k], recv_sem=recv_sems.at[k], device_id=to, device_id_type=MESH)

        mine = pltpu.make_async_copy(x_ref, rows(*me), local_sem)
        mine.start()
        first = [copy(0, me, sibling, src=x_ref)]
        first += [copy(1 + j, me, (*chip, c), src=x_ref) for j, chip in enumerate(chips)]
        for cp in first:
            cp.start()
        passed = [copy(4 + j, (*chip, c), sibling) for j, chip in enumerate(chips)]
        for j, chip in enumerate(chips):
            copy(1 + j, (*chip, c), me).wait_recv()
            passed[j].start()
        copy(0, sibling, me).wait_recv()
        for j, chip in enumerate(chips):
            copy(4 + j, (*chip, 1 - c), me).wait_recv()
        for cp in first + passed:
            cp.wait_send()
        mine.wait()

    return pl.pallas_call(
        body, name=name, out_shape=jax.ShapeDtypeStruct((8 * m_per, D_MODEL), tile.dtype),
        in_specs=[pl.BlockSpec(memory_space=pltpu.VMEM)], out_specs=pl.BlockSpec(memory_space=pltpu.VMEM),
        scratch_shapes=[pltpu.SemaphoreType.DMA((7,)), pltpu.SemaphoreType.DMA((7,)), pltpu.SemaphoreType.DMA],
    )(tile)


def _pad_rows(a, rows):
    return jnp.pad(a, ((0, rows - a.shape[0]), (0, 0)))


def _pad_row(v):
    v = v.reshape(1, -1)
    return jnp.pad(v, ((0, 0), (0, D_MODEL - v.shape[1])))


def _pack_shard(fwi, fwo, dwi, dwo, pwu, pwg):
    parts = [_pad_rows(fwi.reshape(1028, D_MODEL), FLAT_A), fwo.reshape(256, D_MODEL), dwi.reshape(2560, D_MODEL),
             dwo.reshape(256, D_MODEL), pwu.reshape(128, D_MODEL), pwg.reshape(512, D_MODEL),
             jnp.zeros((FLAT_TOTAL - FLAT_USED, D_MODEL), fwi.dtype)]
    return jnp.concatenate(parts, axis=0)


def _unpack_shard(flat):
    out, r0 = {}, 0
    for name, rows, slot in FLAT_ROWS:
        out[name] = flat[r0:r0 + rows]
        r0 += slot
    return (out["fox_w_in"].reshape(1, D_MODEL, 1028), out["fox_w_out"].reshape(1, 256, D_MODEL),
            out["dil_w_in"].reshape(1, D_MODEL, 2560), out["dil_w_out"].reshape(1, 256, D_MODEL),
            out["ple_w_up"].reshape(2, PLE_DIM, 256), out["ple_w_gate"].reshape(2, 256, D_MODEL))


def _from_shard_columns(a, n):
    return a.reshape(4, D_MODEL, n).transpose(1, 0, 2).reshape(D_MODEL, 4 * n)


def _to_shard_columns(a, n):
    return a.reshape(D_MODEL, 4, n).transpose(1, 0, 2).reshape(4, n, D_MODEL)


def _unpack_a(wall_a):
    return _from_shard_columns(wall_a[:, :1028], 1028)


def _unpack_b(wall_b):
    out, r0 = {}, 0
    for name, rows, slot in FLAT_ROWS[1:]:
        out[name] = wall_b[:, r0:r0 + rows]
        r0 += slot
    pwu4 = out["ple_w_up"].reshape(4, 2, PLE_DIM, 256)
    pwg4 = out["ple_w_gate"].reshape(4, 2, 256, D_MODEL)
    pwu = [pwu4[:, i].transpose(1, 0, 2).reshape(PLE_DIM, D_MODEL) for i in range(2)]
    pwg = [pwg4[:, i].reshape(D_MODEL, D_MODEL) for i in range(2)]
    return (out["fox_w_out"].reshape(D_MODEL, D_MODEL), _from_shard_columns(out["dil_w_in"], 2560),
            out["dil_w_out"].reshape(D_MODEL, D_MODEL), pwu, pwg)


def _gathered_b_weights(wall_b, w_b, chip):
    return _unpack_b(lax.dynamic_update_slice(wall_b, w_b[None], (chip, 0, 0)))


def _pack_grads_a(gfwi):
    return jnp.pad(_to_shard_columns(gfwi, 1028), ((0, 0), (0, FLAT_A - 1028), (0, 0)))


def _pack_grads_b(gfwo, gdwi, gdwo, gpwu, gpwg):
    up = jnp.stack([a.reshape(PLE_DIM, 4, 256).transpose(1, 0, 2) for a in gpwu], axis=1)
    gate = jnp.stack([a.reshape(4, 256, D_MODEL) for a in gpwg], axis=1)
    parts = [gfwo.reshape(4, 256, D_MODEL), _to_shard_columns(gdwi, 2560), gdwo.reshape(4, 256, D_MODEL),
             up.reshape(4, 128, D_MODEL), gate.reshape(4, 512, D_MODEL)]
    return jnp.concatenate(parts, axis=1).astype(BF16)


def _local_step(x, p0, p1, tgt, fox_g, dil_g, fin_g, b_f, fwi, w_b, chip, core):
    S = x.shape[0]
    tm = min(512, S)
    w_qkv0 = jnp.concatenate([fwi[:, :D_MODEL] * 0.125, fwi[:, D_MODEL:3 * D_MODEL]], axis=1)
    w_z0 = fwi[:, 3 * D_MODEL:4 * D_MODEL]
    w_f0 = jnp.pad(fwi[:, 4 * D_MODEL:], ((0, 0), (0, LANES - FOX_HEADS)))
    b_full = jnp.pad(b_f.reshape(1, FOX_HEADS), ((0, 0), (0, LANES - FOX_HEADS)))

    n0, = _rows(_rms_fwd, [x], [(D_MODEL, BF16)], name="norm0", tm=tm, bcast=[fox_g])
    qkv0 = _mm(n0, w_qkv0, out_dtype=BF16, name="proj_qkv0")
    z0 = _mm(n0, w_z0, name="proj_z0")
    fl0 = _mm(n0, w_f0, name="proj_f0")
    qx0, kx0 = _cumsum_rows([fl0], name="forget_cumsum", width=LANES, pre=_log_forget, bcast=[b_full],
                            post=_fox_extras, outs=[(D_MODEL, BF16), (D_MODEL, BF16)])
    o0, a0, g0, wall_b = _fox_fwd(qkv0, qx0, kx0, z0, w_b)
    fwo, dwi, dwo, pwu, pwg = _gathered_b_weights(wall_b, w_b, chip)
    w_z1 = dwi[:, 3 * DIL_QKV:]
    h1 = _mm(g0, fwo, add=x, name="out_proj0")
    u0 = _mm(p0, pwu[0], name="ple_up0")
    t0 = _mm(h1, pwg[0], name="ple_gate0")
    h2, n1 = _rows(_ple_norm, [h1, u0, t0], [(D_MODEL, F32), (D_MODEL, BF16)], name="ple_mix0_norm1", tm=tm,
                   bcast=[dil_g])

    qkv1 = _mm(n1, dwi, n_cols=3 * DIL_QKV, name="proj_qkv1")
    z1 = _mm(n1, w_z1, name="proj_z1")
    o1, g1, lse1 = _dil_fwd(qkv1, z1)
    h3 = _mm(g1, dwo, add=h2, name="out_proj1")
    u1 = _mm(p1, pwu[1], name="ple_up1")
    t1 = _mm(h3, pwg[1], name="ple_gate1")

    du1, dt1, dh4, g_fin, loss = _rows(
        _ple_loss, [h3, u1, t1, tgt], [(D_MODEL, BF16), (D_MODEL, BF16), (D_MODEL, F32)], name="ple_mix1_loss_head",
        tm=tm, bcast=[fin_g], acc=[((1, D_MODEL), F32), ((1, LANES), F32)])
    g_up1 = _mm(p1, du1, ta=True, out_dtype=BF16, name="grad_ple_up1")
    g_gate1 = _mm(h3, dt1, ta=True, out_dtype=BF16, name="grad_ple_gate1")
    dh3 = _mm(dt1, pwg[1], tb=True, add=dh4, name="ple_back1")
    dg1 = _mm(dh3, dwo, tb=True, name="out_back1")
    g_dwo = _mm(g1, dh3, ta=True, out_dtype=BF16, name="grad_out1")
    do1, dz1, ld1 = _rows(_gate_bwd, [dg1, o1, z1, lse1], [(D_MODEL, F32), (D_MODEL, BF16), (D_MODEL, F32)],
                          name="gate_bwd1", tm=tm)
    dq1, dk1, dv1 = (lax.empty((S, DIL_QKV), BF16) for _ in range(3))
    for g in range(DIL_GROUPS):
        dq1 = _dil_bwd_q(qkv1, do1, ld1, dq1, g)
        dk1, dv1 = _dil_bwd_kv(qkv1, do1, ld1, dk1, dv1, g)
    g_dwi = _mm_tn_parts(n1, [dq1, dk1, dv1, dz1], name="grad_in1", out_dtype=BF16)
    dn1 = _mm_nt_parts([dq1, dk1, dv1, dz1], dwi, name="in_back1")
    dh2, du0, dt0, g_dil = _rows(_norm_ple_bwd, [dn1, h2, dh3, u0, t0], [(D_MODEL, F32), (D_MODEL, BF16), (D_MODEL, BF16)],
                                 name="norm_bwd1_ple_bwd0", tm=tm, bcast=[dil_g], acc=[((1, D_MODEL), F32)])
    g_up0 = _mm(p0, du0, ta=True, out_dtype=BF16, name="grad_ple_up0")
    g_gate0 = _mm(h1, dt0, ta=True, out_dtype=BF16, name="grad_ple_gate0")
    dh1 = _mm(dt0, pwg[0], tb=True, add=dh2, name="ple_back0")
    dg0 = _mm(dh1, fwo, tb=True, name="out_back0")
    g_fwo = _mm(g0, dh1, ta=True, out_dtype=BF16, name="grad_out0")
    do0, dz0, qxa0, dx0 = _rows(_gate_bwd_fox, [dg0, o0, z0, a0], [(D_MODEL, BF16)] * 4, name="gate_bwd0", tm=tm)
    g_b = _pack_grads_b(g_fwo, g_dwi, g_dwo, [g_up0, g_up1], [g_gate0, g_gate1])
    dq0, drx, dk0, dv0, dkx, got_b = _fox_bwd(qkv0, do0, qxa0, dx0, kx0, g_b)
    dlogf, = _cumsum_rows([drx, dkx], name="forget_cumsum_bwd", width=LANES, reverse=True, pre=_forget_dc)
    df0, g_bf = _rows(_forget_bwd, [dlogf, fl0], [(LANES, BF16)], name="forget_bwd", tm=tm, bcast=[b_full],
                      acc=[((1, LANES), F32)])
    g_fwi_main = _mm_tn_parts(n0, [dq0, dk0, dv0, dz0], name="grad_in0")
    g_fwi_f = _mm(n0, df0, ta=True, name="grad_in0_forget")
    g_fwi = jnp.concatenate([g_fwi_main, g_fwi_f[:, :FOX_HEADS]], axis=1)
    half = FLAT_A // 2
    g_a = _pack_grads_a(g_fwi)
    theirs = _swap_halves(g_a)
    mine = lax.dynamic_slice_in_dim(g_a, core * half, half, axis=1)
    part_a, = _rows(lambda a, b: a + b, [mine.reshape(4 * half, D_MODEL), theirs.reshape(4 * half, D_MODEL)],
                    [(D_MODEL, BF16)], name="pair_sum", tm=half)
    part_a = part_a.reshape(4, half, D_MODEL)
    dn0, got_a = _mm_nt_parts([dq0, dk0, dv0, dz0], fwi, name="in_back0", scatter=part_a)
    dn0 = _mm(df0, w_f0, tb=True, add=dn0, name="in_back0_forget")
    dx, g_fox = _rows(_rms_bwd, [dn0, x, dh1], [(D_MODEL, F32)], name="norm_bwd0", tm=tm, bcast=[fox_g],
                      acc=[((1, D_MODEL), F32)])
    return loss, dx, part_a, got_a, g_b, got_b, (g_fox, _pad_row(g_bf[:, :FOX_HEADS]), g_fin, g_dil)


def kernel(x, p, fox_norm, fox_w_in, fox_b_f, fox_w_out, dil_norm, dil_w_in, dil_w_out, ple_w_up, ple_w_gate, final_norm, loss_target, m_fox_norm, m_fox_w_in, m_fox_b_f, m_fox_w_out, m_dil_norm, m_dil_w_in, m_dil_w_out, m_ple_w_up, m_ple_w_gate, m_final_norm, v_fox_norm, v_fox_w_in, v_fox_b_f, v_fox_w_out, v_dil_norm, v_dil_w_in, v_dil_w_out, v_ple_w_up, v_ple_w_gate, v_final_norm):
    xi, yi, ci = lax.axis_index("x"), lax.axis_index("y"), lax.axis_index("c")
    chip = 2 * xi + yi

    w_flat = _pack_shard(*(w.astype(BF16) for w in (fox_w_in, fox_w_out, dil_w_in, dil_w_out, ple_w_up, ple_w_gate)))
    w_a, w_b = w_flat[:FLAT_A], w_flat[FLAT_A:FLAT_USED]
    fwi = _unpack_a(lax.dynamic_update_slice(_gather_weights(w_a), w_a[None], (chip, 0, 0)))
    gains = _gather_tiles(_pad_rows(_pad_row(dil_norm), 8), "gather_gain").reshape(4, 2, 8, D_MODEL)
    dil_g = gains[:, 0, 0, :256].reshape(1, D_MODEL)

    loss_part, dx, part_a, got_a, g_b, got_b, small = _local_step(
        x[0], p[0, 0], p[1, 0], loss_target[0], fox_norm.reshape(1, D_MODEL), dil_g,
        final_norm.reshape(1, D_MODEL), fox_b_f, fwi, w_b, chip, ci)
    g_fox, g_bf, g_fin, g_dil = small
    sum4 = lambda a, b, c, d: ((a.astype(F32) + b.astype(F32)) + c.astype(F32)) + d.astype(F32)

    got_b = lax.dynamic_update_slice(got_b, lax.dynamic_slice_in_dim(g_b, chip, 1, axis=0), (chip, 0, 0))
    sum_b, = _rows(sum4, [got_b[s] for s in range(4)], [(D_MODEL, F32)], name="chip_sum_b", tm=FLAT_B // 8)
    g_flat_b, = _rows(lambda a, b: a + b, [sum_b, _sibling_half(sum_b, "sibling_sum_b")], [(D_MODEL, F32)],
                      name="pair_sum_b", tm=FLAT_B // 8)

    half = FLAT_A // 2
    by_chip = lax.dynamic_update_slice(got_a, lax.dynamic_slice_in_dim(part_a, chip, 1, axis=0), (chip, 0, 0))
    half_sum, = _rows(sum4, [by_chip[s] for s in range(4)], [(D_MODEL, F32)], name="chip_sum", tm=half)
    other_half = _sibling_half(half_sum, "sibling_half")
    g_flat_a = jnp.where(ci == 0, jnp.concatenate([half_sum, other_half], axis=0),
                         jnp.concatenate([other_half, half_sum], axis=0))
    g_flat = jnp.concatenate([g_flat_a, g_flat_b, jnp.zeros((FLAT_TOTAL - FLAT_USED, D_MODEL), F32)], axis=0)

    tile = jnp.concatenate([g_fox, g_bf, g_fin, g_dil, jnp.pad(loss_part, ((0, 0), (0, D_MODEL - LANES))),
                            jnp.zeros((3, D_MODEL), F32)], axis=0)
    tiles = _gather_tiles(tile, "gather_small")
    tot, = _rows(lambda *t: functools.reduce(lambda a, b: a + b, t), [tiles[8 * s:8 * s + 8] for s in range(8)],
                 [(D_MODEL, F32)], name="small_sum", tm=8)
    loss = tot[4, 0]
    g_small = jnp.concatenate([
        tot[0:3],
        _pad_row(lax.dynamic_slice_in_dim(tot[3], chip * 256, 256)), jnp.zeros((4, D_MODEL), F32)], axis=0)
    small_tile = lambda a, b, c, d: jnp.concatenate(
        [_pad_row(a), _pad_row(b), _pad_row(c), _pad_row(d), jnp.zeros((4, D_MODEL), F32)], axis=0)
    w_small = small_tile(fox_norm, fox_b_f, final_norm, dil_norm)
    m_small = small_tile(m_fox_norm, m_fox_b_f, m_final_norm, m_dil_norm)
    v_small = small_tile(v_fox_norm, v_fox_b_f, v_final_norm, v_dil_norm)

    d_small, nm_small, nv_small = _rows(_adamw, [w_small, g_small, m_small, v_small], [(D_MODEL, F32)] * 3,
                                        name="adamw_small", tm=8)
    grads = _unpack_shard(g_flat)
    weights = (fox_w_in, fox_w_out, dil_w_in, dil_w_out, ple_w_up, ple_w_gate)
    firsts = (m_fox_w_in, m_fox_w_out, m_dil_w_in, m_dil_w_out, m_ple_w_up, m_ple_w_gate)
    seconds = (v_fox_w_in, v_fox_w_out, v_dil_w_in, v_dil_w_out, v_ple_w_up, v_ple_w_gate)
    big = []
    for (name, _, _), w, g, m, v in zip(FLAT_ROWS, weights, grads, firsts, seconds):
        width = w.shape[-1]
        flat2 = lambda a: a.reshape(-1, width)
        res = _rows(_adamw, [flat2(w), flat2(g), flat2(m), flat2(v)], [(width, F32)] * 3, name=f"adamw_{name}",
                    tm=128 if width > D_MODEL + LANES else 256)
        big.append([r.reshape(w.shape) for r in res])

    def leaves(shards, small_rows):
        fwi_, fwo_, dwi_, dwo_, pwu_, pwg_ = shards
        return (small_rows[0:1], fwi_, small_rows[1:2, :FOX_HEADS], fwo_, small_rows[3:4, :256], dwi_, dwo_,
                pwu_, pwg_, small_rows[2])

    per_kind = lambda k: [b[k] for b in big]
    return (loss, dx[None], *leaves(grads, g_small), *leaves(per_kind(0), d_small), *leaves(per_kind(1), nm_small),
            *leaves(per_kind(2), nv_small))
```

```python
import functools

import numpy as np
import jax
import jax.numpy as jnp
from jax import lax
from jax.experimental import pallas as pl
from jax.experimental.pallas import tpu as pltpu

F32 = jnp.float32
BF16 = jnp.bfloat16

D_MODEL = 1024
PLE_DIM = 256
FOX_HEADS = 16
FOX_HEAD_DIM = 64
DIL_PATTERN = ((128, 1), (512, 4), (2048, 16))
DIL_GROUPS = 3
DIL_HEADS_PER_GROUP = 8
DIL_HEAD_DIM = 128
DIL_WINDOW_STEPS = 128
DIL_QKV = 3072
ALIBI_MAX_EXP = 8.0
RMS_EPS = 1e-6
ADAM_LR, ADAM_B1, ADAM_B2, ADAM_EPS, ADAM_WD, ADAM_STEP = 0.001, 0.9, 0.999, 1e-08, 0.01, 10

LANES = 128
VMEM_LIMIT = 56 * 1024 * 1024
MESH = pl.DeviceIdType.MESH
ANY = pl.BlockSpec(memory_space=pl.ANY)
NEG = -1e30

FLAT_ROWS = (("fox_w_in", 1028, 1056), ("fox_w_out", 256, 256), ("dil_w_in", 2560, 2560), ("dil_w_out", 256, 256),
             ("ple_w_up", 128, 128), ("ple_w_gate", 512, 512))
FLAT_USED = sum(r for _, _, r in FLAT_ROWS)
FLAT_TOTAL = 4864
FLAT_A = FLAT_ROWS[0][2]
FLAT_B = FLAT_USED - FLAT_A


def _cparams(*sem):
    return pltpu.CompilerParams(dimension_semantics=sem, vmem_limit_bytes=VMEM_LIMIT)


def _sigmoid(x):
    return 1.0 / (1.0 + jnp.exp(-x))


def _mm(a, b, *, name, ta=False, tb=False, out_dtype=F32, add=None, n_cols=None):
    if ta:
        K, M = a.shape
    else:
        M, K = a.shape
    if tb:
        N, Kb = b.shape
    else:
        Kb, N = b.shape
        N = n_cols or N
    assert K == Kb, (a.shape, b.shape)
    tm, tn, tk = min(M, 1024), min(N, 1024), min(K, 1024)
    assert M % tm == 0 and N % tn == 0 and K % tk == 0, (M, N, K)
    nk = K // tk
    dn = (((0 if ta else 1,), (1 if tb else 0,)), ((), ()))

    def body(*refs):
        if add is None:
            a_ref, b_ref, o_ref, acc = refs
        else:
            a_ref, b_ref, add_ref, o_ref, acc = refs
        k = pl.program_id(2)

        @pl.when(k == 0)
        def _():
            acc[...] = jnp.zeros_like(acc)

        acc[...] += lax.dot_general(a_ref[...].astype(BF16), b_ref[...].astype(BF16), dn,
                                    preferred_element_type=F32)

        @pl.when(k == nk - 1)
        def _():
            r = acc[...]
            if add is not None:
                r = r + add_ref[...]
            o_ref[...] = r.astype(out_dtype)

    a_spec = (pl.BlockSpec((tk, tm), lambda i, j, k: (k, i)) if ta
              else pl.BlockSpec((tm, tk), lambda i, j, k: (i, k)))
    b_spec = (pl.BlockSpec((tn, tk), lambda i, j, k: (j, k)) if tb
              else pl.BlockSpec((tk, tn), lambda i, j, k: (k, j)))
    in_specs = [a_spec, b_spec]
    args = [a, b]
    if add is not None:
        in_specs.append(pl.BlockSpec((tm, tn), lambda i, j, k: (i, j)))
        args.append(add)
    return pl.pallas_call(
        body, name=name, grid=(M // tm, N // tn, nk),
        in_specs=in_specs, out_specs=pl.BlockSpec((tm, tn), lambda i, j, k: (i, j)),
        out_shape=jax.ShapeDtypeStruct((M, N), out_dtype),
        scratch_shapes=[pltpu.VMEM((tm, tn), F32)],
        compiler_params=_cparams("parallel", "parallel", "arbitrary"),
    )(*args)


def _mm_nt_parts(a_parts, b, *, name, scatter=None):
    M = a_parts[0].shape[0]
    N, K = b.shape
    tm, tn, tk = min(M, 1024), min(N, 1024), 1024
    nks = [a.shape[1] // tk for a in a_parts]
    offs = [sum(nks[:p]) for p in range(len(nks))]
    nk = sum(nks)
    assert nk * tk <= K and M % tm == 0 and N % tn == 0, (M, N, K)
    n_parts = len(a_parts)

    grid = (M // tm, N // tn, nk)

    def body(*refs):
        a_refs, b_ref = refs[:n_parts], refs[n_parts]
        if scatter is None:
            o_ref, acc = refs[n_parts + 1:]
        else:
            part_ref, o_ref, got_ref, acc, send, recv = refs[n_parts + 1:]
            start, wait = _chip_exchange(lambda cx, cy: part_ref.at[2 * cx + cy], got_ref, send, recv)
            at = lambda step: functools.reduce(
                lambda x, y: x & y, [pl.program_id(ax) == (0 if step == 0 else n - 1) for ax, n in enumerate(grid)])
            pl.when(at(0))(start)
        k = pl.program_id(2)

        @pl.when(k == 0)
        def _():
            acc[...] = jnp.zeros_like(acc)

        for a_ref, off, n in zip(a_refs, offs, nks):
            @pl.when((k >= off) & (k < off + n))
            def _(a_ref=a_ref):
                acc[...] += lax.dot_general(a_ref[...].astype(BF16), b_ref[...].astype(BF16),
                                            (((1,), (1,)), ((), ())), preferred_element_type=F32)

        @pl.when(k == nk - 1)
        def _():
            o_ref[...] = acc[...]

        if scatter is not None:
            pl.when(at(-1))(wait)

    a_specs = [pl.BlockSpec((tm, tk), lambda i, j, k, off=off, n=n: (i, jnp.clip(k - off, 0, n - 1)))
               for off, n in zip(offs, nks)]
    in_specs = a_specs + [pl.BlockSpec((tn, tk), lambda i, j, k: (j, k))]
    out_specs = [pl.BlockSpec((tm, tn), lambda i, j, k: (i, j))]
    out_shape = [jax.ShapeDtypeStruct((M, N), F32)]
    scratch = [pltpu.VMEM((tm, tn), F32)]
    args = [*a_parts, b]
    if scatter is not None:
        in_specs, out_specs = in_specs + [ANY], out_specs + [ANY]
        out_shape.append(jax.ShapeDtypeStruct(scatter.shape, scatter.dtype))
        scratch += [pltpu.SemaphoreType.DMA((3,)), pltpu.SemaphoreType.DMA((3,))]
        args.append(scatter)
    res = pl.pallas_call(
        body, name=name, grid=grid, in_specs=in_specs, out_specs=out_specs, out_shape=out_shape,
        scratch_shapes=scratch,
        compiler_params=_cparams(*(("arbitrary",) * 3 if scatter is not None else ("parallel", "parallel", "arbitrary"))),
    )(*args)
    return res[0] if scatter is None else res


def _mm_tn_parts(a, b_parts, *, name, out_dtype=F32):
    K, M = a.shape
    tm, tn, tk = min(M, 1024), 1024, min(K, 1024)
    njs = [b.shape[1] // tn for b in b_parts]
    offs = [sum(njs[:p]) for p in range(len(njs))]
    nj, nk = sum(njs), K // tk
    assert M % tm == 0 and K % tk == 0 and all(b.shape[1] % tn == 0 for b in b_parts)
    n_parts = len(b_parts)

    def body(*refs):
        a_ref, b_refs, o_ref, acc = refs[0], refs[1:1 + n_parts], refs[1 + n_parts], refs[2 + n_parts]
        j, k = pl.program_id(1), pl.program_id(2)

        @pl.when(k == 0)
        def _():
            acc[...] = jnp.zeros_like(acc)

        for b_ref, off, n in zip(b_refs, offs, njs):
            @pl.when((j >= off) & (j < off + n))
            def _(b_ref=b_ref):
                acc[...] += lax.dot_general(a_ref[...].astype(BF16), b_ref[...].astype(BF16),
                                            (((0,), (0,)), ((), ())), preferred_element_type=F32)

        @pl.when(k == nk - 1)
        def _():
            o_ref[...] = acc[...].astype(out_dtype)

    def b_spec(off, n):
        def index(i, j, k):
            mine = (j >= off) & (j < off + n)
            return jnp.where(mine, k, 0), jnp.clip(j - off, 0, n - 1)
        return pl.BlockSpec((tk, tn), index)

    return pl.pallas_call(
        body, name=name, grid=(M // tm, nj, nk),
        in_specs=[pl.BlockSpec((tk, tm), lambda i, j, k: (k, i))] + [b_spec(off, n) for off, n in zip(offs, njs)],
        out_specs=pl.BlockSpec((tm, tn), lambda i, j, k: (i, j)),
        out_shape=jax.ShapeDtypeStruct((M, nj * tn), out_dtype), scratch_shapes=[pltpu.VMEM((tm, tn), F32)],
        compiler_params=_cparams("parallel", "parallel", "arbitrary"),
    )(a, *b_parts)


def _rows(fn, ins, outs, *, name, tm, bcast=(), acc=()):
    R = ins[0].shape[0]
    assert R % tm == 0, (R, tm)
    n_in, n_b, n_out, n_acc = len(ins), len(bcast), len(outs), len(acc)

    def body(*refs):
        in_refs = refs[:n_in + n_b]
        out_refs = refs[n_in + n_b:n_in + n_b + n_out]
        acc_refs = refs[n_in + n_b + n_out:]
        res = fn(*[r[...] for r in in_refs])
        if not isinstance(res, (tuple, list)):
            res = (res,)
        for r, v in zip(out_refs, res[:n_out]):
            r[...] = v.astype(r.dtype)
        first = pl.program_id(0) == 0
        for r, v in zip(acc_refs, res[n_out:]):
            @pl.when(first)
            def _(r=r, v=v):
                r[...] = v.astype(r.dtype)

            @pl.when(jnp.logical_not(first))
            def _(r=r, v=v):
                r[...] += v.astype(r.dtype)

    in_specs = [pl.BlockSpec((tm, a.shape[1]), lambda i: (i, 0)) for a in ins]
    in_specs += [pl.BlockSpec(b.shape, lambda i, nd=b.ndim: (0,) * nd) for b in bcast]
    out_specs = [pl.BlockSpec((tm, c), lambda i: (i, 0)) for c, _ in outs]
    out_specs += [pl.BlockSpec(s, lambda i, nd=len(s): (0,) * nd) for s, _ in acc]
    out_shape = [jax.ShapeDtypeStruct((R, c), dt) for c, dt in outs]
    out_shape += [jax.ShapeDtypeStruct(s, dt) for s, dt in acc]
    res = pl.pallas_call(
        body, name=name, grid=(R // tm,), in_specs=in_specs, out_specs=out_specs, out_shape=out_shape,
        compiler_params=_cparams("arbitrary" if acc else "parallel"),
    )(*ins, *bcast)
    return res


def _exact_dot(sel, x, left):
    hi = x.astype(BF16)
    r1 = x - hi.astype(F32)
    mid = r1.astype(BF16)
    lo = (r1 - mid.astype(F32)).astype(BF16)
    dot = (lambda p: jnp.dot(sel, p, preferred_element_type=F32)) if left else (
        lambda p: jnp.dot(p, sel, preferred_element_type=F32))
    return dot(hi) + dot(mid) + dot(lo)


def _cumsum_rows(ins, *, name, width, reverse=False, pre=None, bcast=(), post=None, outs=None):
    S = ins[0].shape[0]
    outs = outs or [(width, F32)]
    out_dtypes = [dt for _, dt in outs]
    tb = min(512, S)
    nb = S // tb
    assert S % tb == 0
    n_in = len(ins) + len(bcast)
    n_out = len(out_dtypes)

    def body(*refs):
        in_refs, o_refs, carry = refs[:n_in], refs[n_in:n_in + n_out], refs[n_in + n_out]

        @pl.when(pl.program_id(0) == 0)
        def _():
            carry[...] = jnp.zeros_like(carry)

        xv = in_refs[0][...] if pre is None else pre(*[r[...] for r in in_refs])
        r_ = lax.broadcasted_iota(jnp.int32, (tb, tb), 0)
        c_ = lax.broadcasted_iota(jnp.int32, (tb, tb), 1)
        tri = jnp.where((c_ >= r_) if reverse else (c_ <= r_), 1.0, 0.0).astype(BF16)
        cs = _exact_dot(tri, xv, left=True) + carry[...]
        outs = (cs,) if post is None else post(cs)
        for o_ref, v in zip(o_refs, outs):
            o_ref[...] = v.astype(o_ref.dtype)
        carry[...] = cs[0:1, :] if reverse else cs[tb - 1:tb, :]

    blk = (lambda i: (nb - 1 - i, 0)) if reverse else (lambda i: (i, 0))
    in_specs = [pl.BlockSpec((tb, a.shape[1]), blk) for a in ins]
    in_specs += [pl.BlockSpec(b.shape, lambda i, nd=b.ndim: (0,) * nd) for b in bcast]
    return pl.pallas_call(
        body, name=name, grid=(nb,), in_specs=in_specs, out_specs=[pl.BlockSpec((tb, c), blk) for c, _ in outs],
        out_shape=[jax.ShapeDtypeStruct((S, c), dt) for c, dt in outs], scratch_shapes=[pltpu.VMEM((1, width), F32)],
        compiler_params=_cparams("arbitrary"),
    )(*ins, *bcast)


def _rms_fwd(x, g):
    r = lax.rsqrt(jnp.mean(x * x, axis=1, keepdims=True) + RMS_EPS)
    return (x * r) * g


def _rms_bwd(dn, x, dres, g):
    r = lax.rsqrt(jnp.mean(x * x, axis=1, keepdims=True) + RMS_EPS)
    xh = x * r
    w = dn * g
    dx = r * (w - xh * jnp.mean(w * xh, axis=1, keepdims=True))
    return dres + dx, jnp.sum(dn * xh, axis=0, keepdims=True)


def _final_stage(h, tgt, g):
    r = lax.rsqrt(jnp.mean(h * h, axis=1, keepdims=True) + RMS_EPS)
    xh = h * r
    diff = xh * g - tgt
    loss = 0.5 * jnp.sum(jnp.mean(diff * diff, axis=1, keepdims=True), axis=0, keepdims=True)
    dy = diff * (1.0 / D_MODEL)
    w = dy * g
    dh = r * (w - xh * jnp.mean(w * xh, axis=1, keepdims=True))
    return dh, jnp.sum(dy * xh, axis=0, keepdims=True), jnp.broadcast_to(loss, (1, LANES))


def _ple_fwd(h, u, t):
    return h + u * _sigmoid(t)


def _ple_bwd(dh, u, t):
    s = _sigmoid(t)
    return dh * s, dh * u * s * (1.0 - s)


def _ple_norm(h, u, t, g):
    h2 = _ple_fwd(h, u, t)
    return h2, _rms_fwd(h2, g)


def _ple_loss(h, u, t, tgt, g):
    dh, g_gain, loss = _final_stage(_ple_fwd(h, u, t), tgt, g)
    du, dt = _ple_bwd(dh, u, t)
    return du, dt, dh, g_gain, loss


def _norm_ple_bwd(dn, x, dres, u, t, g):
    dh, g_gain = _rms_bwd(dn, x, dres, g)
    du, dt = _ple_bwd(dh, u, t)
    return dh, du, dt, g_gain


def _gate_fwd(o, z):
    return o * (z * _sigmoid(z))


def _head_sums(prod):
    tm, width = prod.shape
    cols = [jnp.broadcast_to(jnp.sum(prod[:, b * LANES:(b + 1) * LANES], axis=1, keepdims=True), (tm, LANES))
            for b in range(width // LANES)]
    return jnp.concatenate(cols, axis=1)


def _gate_bwd(dg, o, z, lse):
    s = _sigmoid(z)
    do = dg * (z * s)
    dz = dg * o * (s * (1.0 + z * (1.0 - s)))
    lane = lax.broadcasted_iota(jnp.int32, do.shape, 1) % LANES
    return do, dz, jnp.where(lane < LANES // 2, lse, _head_sums(do * o))


def _log_forget(fl, b):
    u = fl + b
    return jnp.minimum(u, 0.0) - jnp.log(1.0 + jnp.exp(-jnp.abs(u)))


def _adamw(w, g, m, v):
    m = ADAM_B1 * m + (1.0 - ADAM_B1) * g
    v = ADAM_B2 * v + (1.0 - ADAM_B2) * (g * g)
    m_hat = m / (1.0 - ADAM_B1 ** ADAM_STEP)
    v_hat = v / (1.0 - ADAM_B2 ** ADAM_STEP)
    delta = -ADAM_LR * (m_hat / (jnp.sqrt(v_hat) + ADAM_EPS) + ADAM_WD * w)
    return delta, m, v


FOX_TQ = 1024


def _split3(x):
    p1 = x.astype(BF16).astype(F32)
    r = x - p1
    p2 = r.astype(BF16).astype(F32)
    return p1, p2, r - p2


def _lane_in_head(shape):
    return lax.broadcasted_iota(jnp.int32, shape, 1) % FOX_HEAD_DIM


def _query_extras(x):
    lm = _lane_in_head(x.shape)
    p1, p2, p3 = _split3(x)
    return jnp.where(lm == 0, p1, jnp.where(lm == 1, p2, jnp.where(lm == 2, p3, jnp.where(lm < 6, 1.0, 0.0))))


def _key_extras(c):
    lm = _lane_in_head(c.shape)
    p1, p2, p3 = _split3(c)
    return jnp.where(lm < 3, 1.0, jnp.where(lm == 3, -p1, jnp.where(lm == 4, -p2, jnp.where(lm == 5, -p3, 0.0))))


def _head_lane_base(head):
    return LANES * (head // 2) + FOX_HEAD_DIM * (1 - head % 2)


def _fox_extras(c):
    head = lax.broadcasted_iota(jnp.int32, (LANES, D_MODEL), 0)
    lane = lax.broadcasted_iota(jnp.int32, (LANES, D_MODEL), 1)
    owner = 2 * (lane // LANES) + 1 - (lane % LANES) // FOX_HEAD_DIM
    wide = _exact_dot(jnp.where(head == owner, 1.0, 0.0).astype(BF16), c, left=False)
    return _query_extras(wide), _key_extras(wide)


def _swapped_head_sums(prod):
    tm, width = prod.shape
    lane = lax.broadcasted_iota(jnp.int32, (tm, LANES), 1)
    low = lane < FOX_HEAD_DIM
    cols = []
    for b in range(width // LANES):
        blk = prod[:, b * LANES:(b + 1) * LANES]
        sa = jnp.sum(jnp.where(low, blk, 0.0), axis=1, keepdims=True)
        sb = jnp.sum(jnp.where(low, 0.0, blk), axis=1, keepdims=True)
        cols.append(jnp.where(low, sb, sa))
    return jnp.concatenate(cols, axis=1)


def _gate_bwd_fox(dg, o, z, a):
    s = _sigmoid(z)
    do = dg * (z * s)
    dz = dg * o * (s * (1.0 + z * (1.0 - s)))
    lm = _lane_in_head(do.shape)
    d1, d2, d3 = _split3(-_swapped_head_sums(do * o))
    dx = jnp.where(lm == 0, d1, jnp.where(lm == 1, d2, jnp.where(lm == 2, d3, 0.0)))
    return do, dz, _query_extras(a), dx


def _forget_dc(drx, dkx):
    lane = lax.broadcasted_iota(jnp.int32, (D_MODEL, LANES), 0)
    head = lax.broadcasted_iota(jnp.int32, (D_MODEL, LANES), 1)
    base = _head_lane_base(head)
    pick = lambda l: jnp.where((lane == base + l) & (head < FOX_HEADS), 1.0, 0.0).astype(BF16)
    return _exact_dot(pick(0), drx, left=False) - _exact_dot(pick(3), dkx, left=False)


def _forget_bwd(dl, fl, b):
    du = dl * (1.0 / (1.0 + jnp.exp(fl + b)))
    return du, jnp.sum(du, axis=0, keepdims=True)


def _chip_exchange(src_of, out_ref, send, recv):
    x, y, c = lax.axis_index("x"), lax.axis_index("y"), lax.axis_index("c")
    chips = [(1 - x, y), (x, 1 - y), (1 - x, 1 - y)]
    me = 2 * x + y
    sends = [pltpu.make_async_remote_copy(src_ref=src_of(cx, cy), dst_ref=out_ref.at[me], send_sem=send.at[k],
                                          recv_sem=recv.at[k], device_id=(cx, cy, c), device_id_type=MESH)
             for k, (cx, cy) in enumerate(chips)]

    def start():
        for cp in sends:
            cp.start()

    def wait():
        for k, (cx, cy) in enumerate(chips):
            slot = out_ref.at[2 * cx + cy]
            pltpu.make_async_remote_copy(src_ref=slot, dst_ref=slot, send_sem=send.at[k], recv_sem=recv.at[k],
                                         device_id=(cx, cy, c), device_id_type=MESH).wait_recv()
        for cp in sends:
            cp.wait_send()

    return start, wait


def _fox_fwd(qkv, qx, kx, z, w_b):
    S = qkv.shape[0]
    tq = min(FOX_TQ, S)
    nq = S // tq
    nt = (((1,), (1,)), ((), ()))

    def body(q_ref, qx_ref, k_ref, v_ref, kx_ref, z_ref, wb_ref, o_ref, a_ref, g_ref, wall_ref, send, recv):
        i = pl.program_id(1)
        start, wait = _chip_exchange(lambda cx, cy: wb_ref, wall_ref, send, recv)
        pl.when((pl.program_id(0) == 0) & (i == 0))(start)
        low = lax.broadcasted_iota(jnp.int32, (tq, LANES), 1) < FOX_HEAD_DIM
        row = lax.broadcasted_iota(jnp.int32, (tq, tq), 0)
        col = lax.broadcasted_iota(jnp.int32, (tq, tq), 1)
        q2, x2 = q_ref[...], qx_ref[...]
        qa = (jnp.where(low, q2, x2), jnp.where(low, x2, q2))

        def step(kb, carry, diag):
            start = pl.multiple_of(kb * tq, tq)
            k2 = k_ref[pl.ds(start, tq), :]
            v2 = v_ref[pl.ds(start, tq), :]
            y2 = kx_ref[pl.ds(start, tq), :]
            one = jnp.ones_like(v2)
            ka = (jnp.where(low, k2, y2), jnp.where(low, y2, k2))
            va = (jnp.where(low, v2, one), jnp.where(low, one, v2))
            out = []
            for hh in range(2):
                m, acc = carry[hh]
                s = lax.dot_general(qa[hh], ka[hh], nt, preferred_element_type=F32)
                if diag:
                    s = jnp.where(col <= row, s, NEG)
                m_new = jnp.maximum(m, jnp.max(s, axis=1, keepdims=True))
                pr = jnp.exp(s - m_new)
                acc = jnp.exp(m - m_new) * acc + jnp.dot(pr.astype(BF16), va[hh], preferred_element_type=F32)
                out.append((m_new, acc))
            return tuple(out)

        init = ((jnp.full((tq, 1), NEG, F32), jnp.zeros((tq, LANES), F32)),) * 2
        carry = lax.fori_loop(0, i, functools.partial(step, diag=False), init)
        (m_a, acc_a), (m_b, acc_b) = step(i, carry, True)
        l_a, l_b = acc_a[:, FOX_HEAD_DIM:FOX_HEAD_DIM + 1], acc_b[:, 0:1]
        xf = x2.astype(F32)
        c_a = xf[:, 64:65] + xf[:, 65:66] + xf[:, 66:67]
        c_b = xf[:, 0:1] + xf[:, 1:2] + xf[:, 2:3]
        o = jnp.where(low, acc_a / l_a, acc_b / l_b)
        o_ref[...] = o
        g_ref[...] = _gate_fwd(o, z_ref[...]).astype(BF16)
        a_ref[...] = jnp.where(low, jnp.broadcast_to(c_b - (m_b + jnp.log(l_b)), (tq, LANES)),
                               jnp.broadcast_to(c_a - (m_a + jnp.log(l_a)), (tq, LANES)))
        pl.when((pl.program_id(0) == FOX_HEADS // 2 - 1) & (i == nq - 1))(wait)

    blk = lambda cb: pl.BlockSpec((tq, LANES), lambda h, i, cb=cb: (i, cb + h))
    res = lambda cb: pl.BlockSpec((S, LANES), lambda h, i, cb=cb: (0, cb + h))
    return pl.pallas_call(
        body, name="fox_attn_fwd", grid=(FOX_HEADS // 2, nq),
        in_specs=[blk(0), blk(0), res(8), res(16), res(0), blk(0), ANY],
        out_specs=[blk(0), blk(0), blk(0), ANY],
        out_shape=[jax.ShapeDtypeStruct((S, D_MODEL), F32), jax.ShapeDtypeStruct((S, D_MODEL), F32),
                   jax.ShapeDtypeStruct((S, D_MODEL), BF16), jax.ShapeDtypeStruct((4,) + w_b.shape, w_b.dtype)],
        scratch_shapes=[pltpu.SemaphoreType.DMA((3,)), pltpu.SemaphoreType.DMA((3,))],
        compiler_params=_cparams("arbitrary", "arbitrary"),
    )(qkv, qx, qkv, qkv, kx, z, w_b)


def _fox_bwd(qkv, do, qxa, dx, kx, g_b):
    S = qkv.shape[0]
    tq = min(FOX_TQ, S)
    nq = S // tq
    nt = (((1,), (1,)), ((), ()))
    tn = (((0,), (0,)), ((), ()))

    def body(q_ref, qx_ref, do_ref, dx_ref, k_ref, v_ref, kx_ref, gb_ref, dq_ref, dr_ref, dk_ref, dv_ref, dkx_ref,
             got_ref, send, recv):
        kb = pl.program_id(1)
        start, wait = _chip_exchange(lambda cx, cy: gb_ref.at[2 * cx + cy], got_ref, send, recv)
        pl.when((pl.program_id(0) == 0) & (kb == 0))(start)
        low = lax.broadcasted_iota(jnp.int32, (tq, LANES), 1) < FOX_HEAD_DIM
        row = lax.broadcasted_iota(jnp.int32, (tq, tq), 0)
        col = lax.broadcasted_iota(jnp.int32, (tq, tq), 1)

        @pl.when(kb == 0)
        def _():
            dq_ref[...] = jnp.zeros_like(dq_ref)
            dr_ref[...] = jnp.zeros_like(dr_ref)

        k2, v2, y2 = k_ref[...], v_ref[...], kx_ref[...]
        one = jnp.ones_like(v2)
        ka = (jnp.where(low, k2, y2), jnp.where(low, y2, k2))
        va = (jnp.where(low, v2, one), jnp.where(low, one, v2))

        def step(qb, carry, diag):
            start = pl.multiple_of(qb * tq, tq)
            q2 = q_ref[pl.ds(start, tq), :]
            x2 = qx_ref[pl.ds(start, tq), :]
            d2 = do_ref[pl.ds(start, tq), :]
            e2 = dx_ref[pl.ds(start, tq), :]
            qa = (jnp.where(low, q2, x2), jnp.where(low, x2, q2))
            da = (jnp.where(low, d2, e2), jnp.where(low, e2, d2))
            new, res = [], []
            for hh in range(2):
                dk, dv = carry[hh]
                s = lax.dot_general(qa[hh], ka[hh], nt, preferred_element_type=F32)
                if diag:
                    s = jnp.where(col <= row, s, NEG)
                pr = jnp.exp(s)
                ds = pr * lax.dot_general(da[hh], va[hh], nt, preferred_element_type=F32)
                prb, dsb = pr.astype(BF16), ds.astype(BF16)
                dv = dv + lax.dot_general(prb, da[hh], tn, preferred_element_type=F32)
                dk = dk + lax.dot_general(dsb, qa[hh], tn, preferred_element_type=F32)
                res.append(jnp.dot(dsb, ka[hh], preferred_element_type=F32))
                new.append((dk, dv))
            dq_ref[pl.ds(start, tq), :] += jnp.where(low, res[0], res[1])
            dr_ref[pl.ds(start, tq), :] += jnp.where(low, res[1], res[0])
            return tuple(new)

        init = ((jnp.zeros((tq, LANES), F32), jnp.zeros((tq, LANES), F32)),) * 2
        carry = step(kb, init, True)
        (dk_a, dv_a), (dk_b, dv_b) = lax.fori_loop(kb + 1, nq, functools.partial(step, diag=False), carry)
        dk_ref[...] = jnp.where(low, dk_a, dk_b).astype(BF16)
        dv_ref[...] = jnp.where(low, dv_a, dv_b).astype(BF16)
        dkx_ref[...] = jnp.where(low, dk_b, dk_a)

        @pl.when(kb == nq - 1)
        def _():
            dq_ref[...] = dq_ref[...] * (FOX_HEAD_DIM ** -0.5)

        pl.when((pl.program_id(0) == FOX_HEADS // 2 - 1) & (kb == nq - 1))(wait)

    res = lambda cb: pl.BlockSpec((S, LANES), lambda h, k, cb=cb: (0, cb + h))
    blk = lambda cb: pl.BlockSpec((tq, LANES), lambda h, k, cb=cb: (k, cb + h))
    f32, b16 = jax.ShapeDtypeStruct((S, D_MODEL), F32), jax.ShapeDtypeStruct((S, D_MODEL), BF16)
    return pl.pallas_call(
        body, name="fox_attn_bwd", grid=(FOX_HEADS // 2, nq),
        in_specs=[res(0), res(0), res(0), res(0), blk(8), blk(16), blk(0), ANY],
        out_specs=[res(0), res(0), blk(0), blk(0), blk(0), ANY],
        out_shape=[f32, f32, b16, b16, f32, jax.ShapeDtypeStruct(g_b.shape, g_b.dtype)],
        scratch_shapes=[pltpu.SemaphoreType.DMA((3,)), pltpu.SemaphoreType.DMA((3,))],
        compiler_params=_cparams("arbitrary", "arbitrary"),
    )(qkv, qxa, do, dx, qkv, qkv, kx, g_b)


def _alibi_slopes():
    n = DIL_GROUPS * DIL_HEADS_PER_GROUP
    s = np.float32(2.0) ** (np.float32(-ALIBI_MAX_EXP) * np.arange(1, n + 1, dtype=np.float32) / np.float32(n))
    return s.astype(np.float32).reshape(DIL_GROUPS, DIL_HEADS_PER_GROUP)


W = DIL_WINDOW_STEPS
DIL_SCALE = DIL_HEAD_DIM ** -0.5


DIL_ROWS = 2048
DIL_BWD_ROWS = 4096
NT = (((1,), (1,)), ((), ()))
TN = (((0,), (0,)), ((), ()))
QKV_BLOCKS = DIL_QKV // LANES


def _slope_table():
    t = np.zeros((DIL_HEADS_PER_GROUP, 8, LANES), np.float32)
    t[:, :DIL_GROUPS, :] = _alibi_slopes().T[:, :, None]
    return jnp.asarray(t)


def _phase_rows(start, d):
    return pl.ds(start, W, stride=d) if d > 1 else pl.ds(start, W)


def _for_phases(d, unit):
    if d == 1:
        unit(0)
    else:
        lax.fori_loop(0, d, lambda r, c: (unit(r), c)[1], 0)


def _window_geometry(shape, q_axis_offset):
    i_ = lax.broadcasted_iota(jnp.int32, shape, 0)
    j_ = lax.broadcasted_iota(jnp.int32, shape, 1)
    dist = q_axis_offset + i_ - j_
    return i_, j_, dist, (dist >= 0) & (dist <= W)


def _dil_fwd(qkv, z):
    S = qkv.shape[0]
    rb = min(DIL_ROWS, S)
    nb = S // rb
    spans = [W * d for _, d in DIL_PATTERN]

    def body(*refs):
        slope_ref = refs[0]
        q, kc, kp, vc, vp = refs[1:4], refs[4:7], refs[7:10], refs[10:13], refs[13:16]
        z_ref, o_ref, g_ref, l_ref = refs[16:20]
        og, lg = refs[20:23], refs[23:26]
        i = pl.program_id(1)
        _, j_, dist, inwin = _window_geometry((W, 2 * W), W)
        distf = dist.astype(F32)
        for g, (_, d) in enumerate(DIL_PATTERN):
            span = spans[g]
            bias = -(slope_ref[0, g:g + 1, 0:1] * float(d)) * distf
            def phase(r, g=g, d=d, span=span, bias=bias):
                kprev, vprev = kp[g][_phase_rows(r, d), :].astype(BF16), vp[g][_phase_rows(r, d), :].astype(BF16)
                for b in range(rb // span):
                    rows = _phase_rows(b * span + r, d)
                    kcur, vcur = kc[g][rows, :].astype(BF16), vc[g][rows, :].astype(BF16)
                    valid = inwin if b > 0 else inwin & ((j_ >= W) | (i > 0))
                    k2 = jnp.concatenate([kprev, kcur], axis=0)
                    v2 = jnp.concatenate([vprev, vcur], axis=0)
                    s = lax.dot_general(q[g][rows, :].astype(BF16), k2, NT, preferred_element_type=F32)
                    s = jnp.where(valid, s * DIL_SCALE + bias, NEG)
                    m = jnp.max(s, axis=1, keepdims=True)
                    pr = jnp.exp(s - m)
                    l = jnp.sum(pr, axis=1, keepdims=True)
                    og[g][rows, :] = jnp.dot(pr.astype(BF16), v2, preferred_element_type=F32) / l
                    lg[g][rows, :] = jnp.broadcast_to(m + jnp.log(l), (W, LANES))
                    kprev, vprev = kcur, vcur

            _for_phases(d, phase)

        def mix(cix, c):
            sl = pl.ds(pl.multiple_of(cix * 256, 256), 256)
            l1, l2, l3 = lg[0][sl, :], lg[1][sl, :], lg[2][sl, :]
            m = jnp.maximum(jnp.maximum(l1, l2), l3)
            e1, e2, e3 = jnp.exp(l1 - m), jnp.exp(l2 - m), jnp.exp(l3 - m)
            tot = e1 + e2 + e3
            o = (e1 * og[0][sl, :] + e2 * og[1][sl, :] + e3 * og[2][sl, :]) / tot
            o_ref[sl, :] = o
            g_ref[sl, :] = _gate_fwd(o, z_ref[sl, :]).astype(BF16)
            l_ref[sl, :] = m + jnp.log(tot)
            return c

        lax.fori_loop(0, rb // 256, mix, 0)

    cur = lambda off, g: pl.BlockSpec((rb, LANES), lambda h, i: (i, off * QKV_BLOCKS + g * DIL_HEADS_PER_GROUP + h))
    prev = lambda off, g: pl.BlockSpec(
        (spans[g], LANES),
        lambda h, i: (jnp.maximum(i * (rb // spans[g]) - 1, 0), off * QKV_BLOCKS + g * DIL_HEADS_PER_GROUP + h))
    row = pl.BlockSpec((rb, LANES), lambda h, i: (i, h))
    groups = range(DIL_GROUPS)
    in_specs = [pl.BlockSpec((1, 8, LANES), lambda h, i: (h, 0, 0))]
    in_specs += [cur(0, g) for g in groups] + [cur(1, g) for g in groups] + [prev(1, g) for g in groups]
    in_specs += [cur(2, g) for g in groups] + [prev(2, g) for g in groups] + [row]
    f32 = jax.ShapeDtypeStruct((S, D_MODEL), F32)
    return pl.pallas_call(
        body, name="dil_attn_fwd", grid=(DIL_HEADS_PER_GROUP, nb), in_specs=in_specs,
        out_specs=[row, row, row], out_shape=[f32, jax.ShapeDtypeStruct((S, D_MODEL), BF16), f32],
        scratch_shapes=[pltpu.VMEM((rb, LANES), F32)] * 6,
        compiler_params=_cparams("parallel", "parallel"),
    )(_slope_table(), *([qkv] * 15), z)


def _dil_bwd_q(qkv, do, ld, buf, g):
    S = qkv.shape[0]
    d = DIL_PATTERN[g][1]
    span = W * d
    rb = min(max(DIL_ROWS, min(8 * span, DIL_BWD_ROWS)), S)
    nb = S // rb

    def body(slope_ref, q_ref, kc, kp, vc, vp, do_ref, ld_ref, buf_ref, dq_ref, dq_f32):
        i = pl.program_id(1)
        _, j_, dist, inwin = _window_geometry((W, 2 * W), W)
        bias = -(slope_ref[0, g:g + 1, 0:1] * float(d)) * dist.astype(F32)
        def phase(r):
            kprev, vprev = kp[_phase_rows(r, d), :].astype(BF16), vp[_phase_rows(r, d), :].astype(BF16)
            for b in range(rb // span):
                rows = _phase_rows(b * span + r, d)
                kcur, vcur = kc[rows, :].astype(BF16), vc[rows, :].astype(BF16)
                valid = inwin if b > 0 else inwin & ((j_ >= W) | (i > 0))
                k2 = jnp.concatenate([kprev, kcur], axis=0)
                v2 = jnp.concatenate([vprev, vcur], axis=0)
                ld = ld_ref[rows, :]
                s = lax.dot_general(q_ref[rows, :].astype(BF16), k2, NT, preferred_element_type=F32)
                s = jnp.where(valid, s * DIL_SCALE + bias - ld[:, 0:1], NEG)
                pr = jnp.exp(s)
                dp = lax.dot_general(do_ref[rows, :].astype(BF16), v2, NT, preferred_element_type=F32)
                ds = pr * (dp - ld[:, LANES // 2:LANES // 2 + 1])
                dq_f32[rows, :] = jnp.dot(ds.astype(BF16), k2, preferred_element_type=F32) * DIL_SCALE
                kprev, vprev = kcur, vcur

        _for_phases(d, phase)
        dq_ref[...] = dq_f32[...].astype(BF16)

    col = lambda off: off * QKV_BLOCKS + g * DIL_HEADS_PER_GROUP
    cur = lambda off: pl.BlockSpec((rb, LANES), lambda h, i: (i, col(off) + h))
    prev = lambda off: pl.BlockSpec((span, LANES), lambda h, i: (jnp.maximum(i * (rb // span) - 1, 0), col(off) + h))
    row = pl.BlockSpec((rb, LANES), lambda h, i: (i, h))
    return pl.pallas_call(
        body, name=f"dil_attn_bwd_q_g{g}", grid=(DIL_HEADS_PER_GROUP, nb),
        in_specs=[pl.BlockSpec((1, 8, LANES), lambda h, i: (h, 0, 0)), cur(0), cur(1), prev(1), cur(2), prev(2),
                  row, row, pl.BlockSpec(memory_space=pl.ANY)],
        out_specs=pl.BlockSpec((rb, LANES), lambda h, i: (i, g * DIL_HEADS_PER_GROUP + h)),
        out_shape=jax.ShapeDtypeStruct(buf.shape, buf.dtype), input_output_aliases={8: 0},
        scratch_shapes=[pltpu.VMEM((rb, LANES), F32)],
        compiler_params=_cparams("parallel", "parallel"),
    )(_slope_table(), qkv, qkv, qkv, qkv, qkv, do, ld, buf)


def _dil_bwd_kv(qkv, do, ld, bufk, bufv, g):
    S = qkv.shape[0]
    d = DIL_PATTERN[g][1]
    span = W * d
    rb = min(max(DIL_ROWS, min(8 * span, DIL_BWD_ROWS)), S)
    nb = S // rb
    nub = rb // span

    def body(slope_ref, k_ref, v_ref, qc, qn, doc, don, ldc, ldn, bufk_ref, bufv_ref, dk_ref, dv_ref, dk_f32, dv_f32):
        i = pl.program_id(1)
        i_, _, dist, inwin = _window_geometry((2 * W, W), 0)
        bias = -(slope_ref[0, g:g + 1, 0:1] * float(d)) * dist.astype(F32)
        def phase(r):
            first = _phase_rows(r, d)
            qcur, docur, ldcur = qc[first, :].astype(BF16), doc[first, :].astype(BF16), ldc[first, :]
            for b in range(nub):
                rows = _phase_rows(b * span + r, d)
                if b < nub - 1:
                    nxt = _phase_rows((b + 1) * span + r, d)
                    qnext, donext, ldnext, valid = qc[nxt, :].astype(BF16), doc[nxt, :].astype(BF16), ldc[nxt, :], inwin
                else:
                    qnext, donext, ldnext = qn[first, :].astype(BF16), don[first, :].astype(BF16), ldn[first, :]
                    valid = inwin & ((i_ < W) | (i < nb - 1))
                q2 = jnp.concatenate([qcur, qnext], axis=0)
                do2 = jnp.concatenate([docur, donext], axis=0)
                ld2 = jnp.concatenate([ldcur, ldnext], axis=0)
                s = lax.dot_general(q2, k_ref[rows, :].astype(BF16), NT, preferred_element_type=F32)
                s = jnp.where(valid, s * DIL_SCALE + bias - ld2[:, 0:1], NEG)
                pr = jnp.exp(s)
                dp = lax.dot_general(do2, v_ref[rows, :].astype(BF16), NT, preferred_element_type=F32)
                ds = pr * (dp - ld2[:, LANES // 2:LANES // 2 + 1])
                dv_f32[rows, :] = lax.dot_general(pr.astype(BF16), do2, TN, preferred_element_type=F32)
                dk_f32[rows, :] = lax.dot_general(ds.astype(BF16), q2, TN, preferred_element_type=F32) * DIL_SCALE
                qcur, docur, ldcur = qnext, donext, ldnext

        _for_phases(d, phase)
        dk_ref[...] = dk_f32[...].astype(BF16)
        dv_ref[...] = dv_f32[...].astype(BF16)

    col = lambda off: off * QKV_BLOCKS + g * DIL_HEADS_PER_GROUP
    cur = lambda off: pl.BlockSpec((rb, LANES), lambda h, i: (i, col(off) + h))
    nxt_blk = lambda i: jnp.minimum((i + 1) * nub, S // span - 1)
    nxt = lambda off: pl.BlockSpec((span, LANES), lambda h, i: (nxt_blk(i), col(off) + h))
    row = pl.BlockSpec((rb, LANES), lambda h, i: (i, h))
    row_nxt = pl.BlockSpec((span, LANES), lambda h, i: (nxt_blk(i), h))
    any_ = pl.BlockSpec(memory_space=pl.ANY)
    out = pl.BlockSpec((rb, LANES), lambda h, i: (i, g * DIL_HEADS_PER_GROUP + h))
    return pl.pallas_call(
        body, name=f"dil_attn_bwd_kv_g{g}", grid=(DIL_HEADS_PER_GROUP, nb),
        in_specs=[pl.BlockSpec((1, 8, LANES), lambda h, i: (h, 0, 0)), cur(1), cur(2), cur(0), nxt(0),
                  row, row_nxt, row, row_nxt, any_, any_],
        out_specs=[out, out],
        out_shape=[jax.ShapeDtypeStruct(bufk.shape, bufk.dtype), jax.ShapeDtypeStruct(bufv.shape, bufv.dtype)],
        input_output_aliases={9: 0, 10: 1},
        scratch_shapes=[pltpu.VMEM((rb, LANES), F32)] * 2,
        compiler_params=_cparams("parallel", "parallel"),
    )(_slope_table(), qkv, qkv, qkv, qkv, do, do, ld, ld, bufk, bufv)


def _place():
    x, y, c = lax.axis_index("x"), lax.axis_index("y"), lax.axis_index("c")
    chips = [(1 - x, y), (x, 1 - y), (1 - x, 1 - y)]
    return x, y, c, chips


def _gather_weights(wb):
    R = wb.shape[0]
    H = R // 2

    def body(w_ref, out_ref, send1, recv1, send2, recv2):
        x, y, c, chips = _place()
        me = 2 * x + y
        sib = (x, y, 1 - c)
        half = pl.ds(c * H, H)
        first = [pltpu.make_async_remote_copy(
            src_ref=w_ref.at[half], dst_ref=out_ref.at[me, half], send_sem=send1.at[k], recv_sem=recv1.at[k],
            device_id=(*chip, c), device_id_type=MESH) for k, chip in enumerate(chips)]
        for cp in first:
            cp.start()
        passed = []
        for k, (cx, cy) in enumerate(chips):
            slot = out_ref.at[2 * cx + cy, half]
            pltpu.make_async_remote_copy(src_ref=slot, dst_ref=slot, send_sem=send1.at[k], recv_sem=recv1.at[k],
                                         device_id=(cx, cy, c), device_id_type=MESH).wait_recv()
            cp = pltpu.make_async_remote_copy(src_ref=slot, dst_ref=slot, send_sem=send2.at[k], recv_sem=recv2.at[k],
                                              device_id=sib, device_id_type=MESH)
            cp.start()
            passed.append(cp)
        for k, (cx, cy) in enumerate(chips):
            slot = out_ref.at[2 * cx + cy, pl.ds((1 - c) * H, H)]
            pltpu.make_async_remote_copy(src_ref=slot, dst_ref=slot, send_sem=send2.at[k], recv_sem=recv2.at[k],
                                         device_id=sib, device_id_type=MESH).wait_recv()
        for cp in first + passed:
            cp.wait_send()

    return pl.pallas_call(
        body, name="gather_weights", in_specs=[ANY], out_specs=ANY,
        out_shape=jax.ShapeDtypeStruct((4, R, D_MODEL), wb.dtype),
        scratch_shapes=[pltpu.SemaphoreType.DMA((3,)), pltpu.SemaphoreType.DMA((3,)),
                        pltpu.SemaphoreType.DMA((3,)), pltpu.SemaphoreType.DMA((3,))],
    )(wb)


def _swap_halves(g):
    H = g.shape[1] // 2

    def body(g_ref, out_ref, send, recv):
        x, y, c, _ = _place()
        sib = (x, y, 1 - c)
        cps = [pltpu.make_async_remote_copy(
            src_ref=g_ref.at[s, pl.ds((1 - c) * H, H)], dst_ref=out_ref.at[s], send_sem=send.at[s],
            recv_sem=recv.at[s], device_id=sib, device_id_type=MESH) for s in range(4)]
        for cp in cps:
            cp.start()
        for cp in cps:
            cp.wait()

    return pl.pallas_call(
        body, name="swap_halves", in_specs=[ANY], out_specs=ANY,
        out_shape=jax.ShapeDtypeStruct((4, H, D_MODEL), g.dtype),
        scratch_shapes=[pltpu.SemaphoreType.DMA((4,)), pltpu.SemaphoreType.DMA((4,))],
    )(g)


def _sibling_half(f, name):
    def body(f_ref, out_ref, send, recv):
        x, y, c, _ = _place()
        cp = pltpu.make_async_remote_copy(src_ref=f_ref, dst_ref=out_ref, send_sem=send, recv_sem=recv,
                                          device_id=(x, y, 1 - c), device_id_type=MESH)
        cp.start()
        cp.wait()

    return pl.pallas_call(
        body, name=name, in_specs=[ANY], out_specs=ANY,
        out_shape=jax.ShapeDtypeStruct(f.shape, f.dtype),
        scratch_shapes=[pltpu.SemaphoreType.DMA, pltpu.SemaphoreType.DMA],
    )(f)


def _gather_tiles(tile, name):
    m_per = tile.shape[0]

    def body(x_ref, out_ref, send_sems, recv_sems, local_sem):
        x, y, c, chips = _place()
        me, sibling = (x, y, c), (x, y, 1 - c)

        def rows(px, py, pc):
            return out_ref.at[pl.ds((4 * px + 2 * py + pc) * m_per, m_per), :]

        def copy(k, block, to, src=None):
            return pltpu.make_async_remote_copy(
                src_ref=rows(*block) if src is None else src, dst_ref=rows(*block),
                send_sem=send_sems.at[k], recv_sem=recv_sems.at[k], device_id=to, device_id_type=MESH)

        mine = pltpu.make_async_copy(x_ref, rows(*me), local_sem)
        mine.start()
        first = [copy(0, me, sibling, src=x_ref)]
        first += [copy(1 + j, me, (*chip, c), src=x_ref) for j, chip in enumerate(chips)]
        for cp in first:
            cp.start()
        passed = [copy(4 + j, (*chip, c), sibling) for j, chip in enumerate(chips)]
        for j, chip in enumerate(chips):
            copy(1 + j, (*chip, c), me).wait_recv()
            passed[j].start()
        copy(0, sibling, me).wait_recv()
        for j, chip in enumerate(chips):
            copy(4 + j, (*chip, 1 - c), me).wait_recv()
        for cp in first + passed:
            cp.wait_send()
        mine.wait()

    return pl.pallas_call(
        body, name=name, out_shape=jax.ShapeDtypeStruct((8 * m_per, D_MODEL), tile.dtype),
        in_specs=[pl.BlockSpec(memory_space=pltpu.VMEM)], out_specs=pl.BlockSpec(memory_space=pltpu.VMEM),
        scratch_shapes=[pltpu.SemaphoreType.DMA((7,)), pltpu.SemaphoreType.DMA((7,)), pltpu.SemaphoreType.DMA],
    )(tile)


def _pad_rows(a, rows):
    return jnp.pad(a, ((0, rows - a.shape[0]), (0, 0)))


def _pad_row(v):
    v = v.reshape(1, -1)
    return jnp.pad(v, ((0, 0), (0, D_MODEL - v.shape[1])))


def _pack_shard(fwi, fwo, dwi, dwo, pwu, pwg):
    parts = [_pad_rows(fwi.reshape(1028, D_MODEL), FLAT_A), fwo.reshape(256, D_MODEL), dwi.reshape(2560, D_MODEL),
             dwo.reshape(256, D_MODEL), pwu.reshape(128, D_MODEL), pwg.reshape(512, D_MODEL),
             jnp.zeros((FLAT_TOTAL - FLAT_USED, D_MODEL), fwi.dtype)]
    return jnp.concatenate(parts, axis=0)


def _unpack_shard(flat):
    out, r0 = {}, 0
    for name, rows, slot in FLAT_ROWS:
        out[name] = flat[r0:r0 + rows]
        r0 += slot
    return (out["fox_w_in"].reshape(1, D_MODEL, 1028), out["fox_w_out"].reshape(1, 256, D_MODEL),
            out["dil_w_in"].reshape(1, D_MODEL, 2560), out["dil_w_out"].reshape(1, 256, D_MODEL),
            out["ple_w_up"].reshape(2, PLE_DIM, 256), out["ple_w_gate"].reshape(2, 256, D_MODEL))


def _from_shard_columns(a, n):
    return a.reshape(4, D_MODEL, n).transpose(1, 0, 2).reshape(D_MODEL, 4 * n)


def _to_shard_columns(a, n):
    return a.reshape(D_MODEL, 4, n).transpose(1, 0, 2).reshape(4, n, D_MODEL)


def _unpack_a(wall_a):
    return _from_shard_columns(wall_a[:, :1028], 1028)


def _unpack_b(wall_b):
    out, r0 = {}, 0
    for name, rows, slot in FLAT_ROWS[1:]:
        out[name] = wall_b[:, r0:r0 + rows]
        r0 += slot
    pwu4 = out["ple_w_up"].reshape(4, 2, PLE_DIM, 256)
    pwg4 = out["ple_w_gate"].reshape(4, 2, 256, D_MODEL)
    pwu = [pwu4[:, i].transpose(1, 0, 2).reshape(PLE_DIM, D_MODEL) for i in range(2)]
    pwg = [pwg4[:, i].reshape(D_MODEL, D_MODEL) for i in range(2)]
    return (out["fox_w_out"].reshape(D_MODEL, D_MODEL), _from_shard_columns(out["dil_w_in"], 2560),
            out["dil_w_out"].reshape(D_MODEL, D_MODEL), pwu, pwg)


def _gathered_b_weights(wall_b, w_b, chip):
    return _unpack_b(lax.dynamic_update_slice(wall_b, w_b[None], (chip, 0, 0)))


def _pack_grads_a(gfwi):
    return jnp.pad(_to_shard_columns(gfwi, 1028), ((0, 0), (0, FLAT_A - 1028), (0, 0)))


def _pack_grads_b(gfwo, gdwi, gdwo, gpwu, gpwg):
    up = jnp.stack([a.reshape(PLE_DIM, 4, 256).transpose(1, 0, 2) for a in gpwu], axis=1)
    gate = jnp.stack([a.reshape(4, 256, D_MODEL) for a in gpwg], axis=1)
    parts = [gfwo.reshape(4, 256, D_MODEL), _to_shard_columns(gdwi, 2560), gdwo.reshape(4, 256, D_MODEL),
             up.reshape(4, 128, D_MODEL), gate.reshape(4, 512, D_MODEL)]
    return jnp.concatenate(parts, axis=1).astype(BF16)


def _local_step(x, p0, p1, tgt, fox_g, dil_g, fin_g, b_f, fwi, w_b, chip, core):
    S = x.shape[0]
    tm = min(512, S)
    w_qkv0 = jnp.concatenate([fwi[:, :D_MODEL] * 0.125, fwi[:, D_MODEL:3 * D_MODEL]], axis=1)
    w_z0 = fwi[:, 3 * D_MODEL:4 * D_MODEL]
    w_f0 = jnp.pad(fwi[:, 4 * D_MODEL:], ((0, 0), (0, LANES - FOX_HEADS)))
    b_full = jnp.pad(b_f.reshape(1, FOX_HEADS), ((0, 0), (0, LANES - FOX_HEADS)))

    n0, = _rows(_rms_fwd, [x], [(D_MODEL, BF16)], name="norm0", tm=tm, bcast=[fox_g])
    qkv0 = _mm(n0, w_qkv0, out_dtype=BF16, name="proj_qkv0")
    z0 = _mm(n0, w_z0, name="proj_z0")
    fl0 = _mm(n0, w_f0, name="proj_f0")
    qx0, kx0 = _cumsum_rows([fl0], name="forget_cumsum", width=LANES, pre=_log_forget, bcast=[b_full],
                            post=_fox_extras, outs=[(D_MODEL, BF16), (D_MODEL, BF16)])
    o0, a0, g0, wall_b = _fox_fwd(qkv0, qx0, kx0, z0, w_b)
    fwo, dwi, dwo, pwu, pwg = _gathered_b_weights(wall_b, w_b, chip)
    w_z1 = dwi[:, 3 * DIL_QKV:]
    h1 = _mm(g0, fwo, add=x, name="out_proj0")
    u0 = _mm(p0, pwu[0], name="ple_up0")
    t0 = _mm(h1, pwg[0], name="ple_gate0")
    h2, n1 = _rows(_ple_norm, [h1, u0, t0], [(D_MODEL, F32), (D_MODEL, BF16)], name="ple_mix0_norm1", tm=tm,
                   bcast=[dil_g])

    qkv1 = _mm(n1, dwi, n_cols=3 * DIL_QKV, name="proj_qkv1")
    z1 = _mm(n1, w_z1, name="proj_z1")
    o1, g1, lse1 = _dil_fwd(qkv1, z1)
    h3 = _mm(g1, dwo, add=h2, name="out_proj1")
    u1 = _mm(p1, pwu[1], name="ple_up1")
    t1 = _mm(h3, pwg[1], name="ple_gate1")

    du1, dt1, dh4, g_fin, loss = _rows(
        _ple_loss, [h3, u1, t1, tgt], [(D_MODEL, BF16), (D_MODEL, BF16), (D_MODEL, F32)], name="ple_mix1_loss_head",
        tm=tm, bcast=[fin_g], acc=[((1, D_MODEL), F32), ((1, LANES), F32)])
    g_up1 = _mm(p1, du1, ta=True, out_dtype=BF16, name="grad_ple_up1")
    g_gate1 = _mm(h3, dt1, ta=True, out_dtype=BF16, name="grad_ple_gate1")
    dh3 = _mm(dt1, pwg[1], tb=True, add=dh4, name="ple_back1")
    dg1 = _mm(dh3, dwo, tb=True, name="out_back1")
    g_dwo = _mm(g1, dh3, ta=True, out_dtype=BF16, name="grad_out1")
    do1, dz1, ld1 = _rows(_gate_bwd, [dg1, o1, z1, lse1], [(D_MODEL, F32), (D_MODEL, BF16), (D_MODEL, F32)],
                          name="gate_bwd1", tm=tm)
    dq1, dk1, dv1 = (lax.empty((S, DIL_QKV), BF16) for _ in range(3))
    for g in range(DIL_GROUPS):
        dq1 = _dil_bwd_q(qkv1, do1, ld1, dq1, g)
        dk1, dv1 = _dil_bwd_kv(qkv1, do1, ld1, dk1, dv1, g)
    g_dwi = _mm_tn_parts(n1, [dq1, dk1, dv1, dz1], name="grad_in1", out_dtype=BF16)
    dn1 = _mm_nt_parts([dq1, dk1, dv1, dz1], dwi, name="in_back1")
    dh2, du0, dt0, g_dil = _rows(_norm_ple_bwd, [dn1, h2, dh3, u0, t0], [(D_MODEL, F32), (D_MODEL, BF16), (D_MODEL, BF16)],
                                 name="norm_bwd1_ple_bwd0", tm=tm, bcast=[dil_g], acc=[((1, D_MODEL), F32)])
    g_up0 = _mm(p0, du0, ta=True, out_dtype=BF16, name="grad_ple_up0")
    g_gate0 = _mm(h1, dt0, ta=True, out_dtype=BF16, name="grad_ple_gate0")
    dh1 = _mm(dt0, pwg[0], tb=True, add=dh2, name="ple_back0")
    dg0 = _mm(dh1, fwo, tb=True, name="out_back0")
    g_fwo = _mm(g0, dh1, ta=True, out_dtype=BF16, name="grad_out0")
    do0, dz0, qxa0, dx0 = _rows(_gate_bwd_fox, [dg0, o0, z0, a0], [(D_MODEL, BF16)] * 4, name="gate_bwd0", tm=tm)
    g_b = _pack_grads_b(g_fwo, g_dwi, g_dwo, [g_up0, g_up1], [g_gate0, g_gate1])
    dq0, drx, dk0, dv0, dkx, got_b = _fox_bwd(qkv0, do0, qxa0, dx0, kx0, g_b)
    dlogf, = _cumsum_rows([drx, dkx], name="forget_cumsum_bwd", width=LANES, reverse=True, pre=_forget_dc)
    df0, g_bf = _rows(_forget_bwd, [dlogf, fl0], [(LANES, BF16)], name="forget_bwd", tm=tm, bcast=[b_full],
                      acc=[((1, LANES), F32)])
    g_fwi_main = _mm_tn_parts(n0, [dq0, dk0, dv0, dz0], name="grad_in0")
    g_fwi_f = _mm(n0, df0, ta=True, name="grad_in0_forget")
    g_fwi = jnp.concatenate([g_fwi_main, g_fwi_f[:, :FOX_HEADS]], axis=1)
    half = FLAT_A // 2
    g_a = _pack_grads_a(g_fwi)
    theirs = _swap_halves(g_a)
    mine = lax.dynamic_slice_in_dim(g_a, core * half, half, axis=1)
    part_a, = _rows(lambda a, b: a + b, [mine.reshape(4 * half, D_MODEL), theirs.reshape(4 * half, D_MODEL)],
                    [(D_MODEL, BF16)], name="pair_sum", tm=half)
    part_a = part_a.reshape(4, half, D_MODEL)
    dn0, got_a = _mm_nt_parts([dq0, dk0, dv0, dz0], fwi, name="in_back0", scatter=part_a)
    dn0 = _mm(df0, w_f0, tb=True, add=dn0, name="in_back0_forget")
    dx, g_fox = _rows(_rms_bwd, [dn0, x, dh1], [(D_MODEL, F32)], name="norm_bwd0", tm=tm, bcast=[fox_g],
                      acc=[((1, D_MODEL), F32)])
    return loss, dx, part_a, got_a, g_b, got_b, (g_fox, _pad_row(g_bf[:, :FOX_HEADS]), g_fin, g_dil)


def kernel(x, p, fox_norm, fox_w_in, fox_b_f, fox_w_out, dil_norm, dil_w_in, dil_w_out, ple_w_up, ple_w_gate, final_norm, loss_target, m_fox_norm, m_fox_w_in, m_fox_b_f, m_fox_w_out, m_dil_norm, m_dil_w_in, m_dil_w_out, m_ple_w_up, m_ple_w_gate, m_final_norm, v_fox_norm, v_fox_w_in, v_fox_b_f, v_fox_w_out, v_dil_norm, v_dil_w_in, v_dil_w_out, v_ple_w_up, v_ple_w_gate, v_final_norm):
    xi, yi, ci = lax.axis_index("x"), lax.axis_index("y"), lax.axis_index("c")
    chip = 2 * xi + yi

    w_flat = _pack_shard(*(w.astype(BF16) for w in (fox_w_in, fox_w_out, dil_w_in, dil_w_out, ple_w_up, ple_w_gate)))
    w_a, w_b = w_flat[:FLAT_A], w_flat[FLAT_A:FLAT_USED]
    fwi = _unpack_a(lax.dynamic_update_slice(_gather_weights(w_a), w_a[None], (chip, 0, 0)))
    gains = _gather_tiles(_pad_rows(_pad_row(dil_norm), 8), "gather_gain").reshape(4, 2, 8, D_MODEL)
    dil_g = gains[:, 0, 0, :256].reshape(1, D_MODEL)

    loss_part, dx, part_a, got_a, g_b, got_b, small = _local_step(
        x[0], p[0, 0], p[1, 0], loss_target[0], fox_norm.reshape(1, D_MODEL), dil_g,
        final_norm.reshape(1, D_MODEL), fox_b_f, fwi, w_b, chip, ci)
    g_fox, g_bf, g_fin, g_dil = small
    sum4 = lambda a, b, c, d: ((a.astype(F32) + b.astype(F32)) + c.astype(F32)) + d.astype(F32)

    got_b = lax.dynamic_update_slice(got_b, lax.dynamic_slice_in_dim(g_b, chip, 1, axis=0), (chip, 0, 0))
    sum_b, = _rows(sum4, [got_b[s] for s in range(4)], [(D_MODEL, F32)], name="chip_sum_b", tm=FLAT_B // 8)
    g_flat_b, = _rows(lambda a, b: a + b, [sum_b, _sibling_half(sum_b, "sibling_sum_b")], [(D_MODEL, F32)],
                      name="pair_sum_b", tm=FLAT_B // 8)

    half = FLAT_A // 2
    by_chip = lax.dynamic_update_slice(got_a, lax.dynamic_slice_in_dim(part_a, chip, 1, axis=0), (chip, 0, 0))
    half_sum, = _rows(sum4, [by_chip[s] for s in range(4)], [(D_MODEL, F32)], name="chip_sum", tm=half)
    other_half = _sibling_half(half_sum, "sibling_half")
    g_flat_a = jnp.where(ci == 0, jnp.concatenate([half_sum, other_half], axis=0),
                         jnp.concatenate([other_half, half_sum], axis=0))
    g_flat = jnp.concatenate([g_flat_a, g_flat_b, jnp.zeros((FLAT_TOTAL - FLAT_USED, D_MODEL), F32)], axis=0)

    tile = jnp.concatenate([g_fox, g_bf, g_fin, g_dil, jnp.pad(loss_part, ((0, 0), (0, D_MODEL - LANES))),
                            jnp.zeros((3, D_MODEL), F32)], axis=0)
    tiles = _gather_tiles(tile, "gather_small")
    tot, = _rows(lambda *t: functools.reduce(lambda a, b: a + b, t), [tiles[8 * s:8 * s + 8] for s in range(8)],
                 [(D_MODEL, F32)], name="small_sum", tm=8)
    loss = tot[4, 0]
    g_small = jnp.concatenate([
        tot[0:3],
        _pad_row(lax.dynamic_slice_in_dim(tot[3], chip * 256, 256)), jnp.zeros((4, D_MODEL), F32)], axis=0)
    small_tile = lambda a, b, c, d: jnp.concatenate(
        [_pad_row(a), _pad_row(b), _pad_row(c), _pad_row(d), jnp.zeros((4, D_MODEL), F32)], axis=0)
    w_small = small_tile(fox_norm, fox_b_f, final_norm, dil_norm)
    m_small = small_tile(m_fox_norm, m_fox_b_f, m_final_norm, m_dil_norm)
    v_small = small_tile(v_fox_norm, v_fox_b_f, v_final_norm, v_dil_norm)

    d_small, nm_small, nv_small = _rows(_adamw, [w_small, g_small, m_small, v_small], [(D_MODEL, F32)] * 3,
                                        name="adamw_small", tm=8)
    grads = _unpack_shard(g_flat)
    weights = (fox_w_in, fox_w_out, dil_w_in, dil_w_out, ple_w_up, ple_w_gate)
    firsts = (m_fox_w_in, m_fox_w_out, m_dil_w_in, m_dil_w_out, m_ple_w_up, m_ple_w_gate)
    seconds = (v_fox_w_in, v_fox_w_out, v_dil_w_in, v_dil_w_out, v_ple_w_up, v_ple_w_gate)
    big = []
    for (name, _, _), w, g, m, v in zip(FLAT_ROWS, weights, grads, firsts, seconds):
        width = w.shape[-1]
        flat2 = lambda a: a.reshape(-1, width)
        res = _rows(_adamw, [flat2(w), flat2(g), flat2(m), flat2(v)], [(width, F32)] * 3, name=f"adamw_{name}",
                    tm=128 if width > D_MODEL + LANES else 256)
        big.append([r.reshape(w.shape) for r in res])

    def leaves(shards, small_rows):
        fwi_, fwo_, dwi_, dwo_, pwu_, pwg_ = shards
        return (small_rows[0:1], fwi_, small_rows[1:2, :FOX_HEADS], fwo_, small_rows[3:4, :256], dwi_, dwo_,
                pwu_, pwg_, small_rows[2])

    per_kind = lambda k: [b[k] for b in big]
    return (loss, dx[None], *leaves(grads, g_small), *leaves(per_kind(0), d_small), *leaves(per_kind(1), nm_small),
            *leaves(per_kind(2), nv_small))
```
